```python
import jax, jax.numpy as jnp
from jax import lax
import numpy as np

D_MODEL = 1024
BATCH = 8
SEQ = 8192
DEPTH = 1

CHUNK = 64
D_SHORT = D_MODEL
D_CONF = D_MODEL
SHORT_K = 3
CONF_K = 31
D_FF = -(-8 * D_MODEL // (3 * 256)) * 256
N_MOD = 6
EPS = 1e-6
LN_EPS = 1e-5
SPLITS = (D_SHORT, 2 * D_SHORT, 3 * D_SHORT,
          3 * D_SHORT + D_CONF, 3 * D_SHORT + 2 * D_CONF,
          3 * D_SHORT + 2 * D_CONF + D_MODEL)
D_IN = 3 * D_SHORT + 2 * D_CONF + 2 * D_MODEL

kernel_name = 'hybrid_shortconv_conformer_gated_block'


def rms_norm(x, g):
    xf = x.astype(jnp.float32)
    y = xf * lax.rsqrt(jnp.mean(xf * xf, axis=-1, keepdims=True) + EPS)
    return (y * g.astype(jnp.float32)).astype(x.dtype)


def layer_norm(x, g, b):
    xf = x.astype(jnp.float32)
    mu = jnp.mean(xf, axis=-1, keepdims=True)
    var = jnp.mean(jnp.square(xf - mu), axis=-1, keepdims=True)
    y = (xf - mu) * lax.rsqrt(var + LN_EPS)
    return (y * g.astype(jnp.float32) + b.astype(jnp.float32)).astype(x.dtype)


def causal_dwconv(u, w):
    k, ch = w.shape
    return lax.conv_general_dilated(
        u, w.astype(u.dtype)[:, None, :], window_strides=(1,), padding=[(k - 1, 0)],
        dimension_numbers=('NWC', 'WIO', 'NWC'), feature_group_count=ch)


def _fwd_setup_inputs(seed: int = 0) -> dict:
    key = jax.random.key(seed)
    ks = jax.random.split(key, 20)
    n = jax.random.normal
    d = D_MODEL
    return {
        'x': n(ks[0], (BATCH, SEQ, d), jnp.float32),
        'c': n(ks[1], (BATCH, d), jnp.float32),
        'w_ada': n(ks[2], (DEPTH, d, N_MOD * d), jnp.float32) * (0.5 * d ** -0.5),
        'b_ada': n(ks[3], (DEPTH, N_MOD * d), jnp.float32) * 0.02,
        'norm_mix_g': 1.0 + 0.05 * n(ks[4], (DEPTH, d), jnp.float32),
        'w_in': n(ks[5], (DEPTH, d, D_IN), jnp.float32) * d ** -0.5,
        'conv_short_w': n(ks[6], (DEPTH, SHORT_K, D_SHORT), jnp.float32) * SHORT_K ** -0.5,
        'w_short_out': n(ks[7], (DEPTH, D_SHORT, d), jnp.float32) * D_SHORT ** -0.5,
        'conv_conf_w': n(ks[8], (DEPTH, CONF_K, D_CONF), jnp.float32) * CONF_K ** -0.5,
        'conv_conf_b': n(ks[9], (DEPTH, D_CONF), jnp.float32) * 0.02,
        'conf_ln_g': 1.0 + 0.05 * n(ks[10], (DEPTH, D_CONF), jnp.float32),
        'conf_ln_b': n(ks[11], (DEPTH, D_CONF), jnp.float32) * 0.02,
        'w_conf_out': n(ks[12], (DEPTH, D_CONF, d), jnp.float32) * D_CONF ** -0.5,
        'w_o': n(ks[13], (DEPTH, d, d), jnp.float32) * d ** -0.5,
        'norm_ffn_g': 1.0 + 0.05 * n(ks[14], (DEPTH, d), jnp.float32),
        'w_ffn_in': n(ks[15], (DEPTH, d, 2 * D_FF), jnp.float32) * d ** -0.5,
        'w_ffn_out': n(ks[16], (DEPTH, D_FF, d), jnp.float32) * D_FF ** -0.5,
        'final_norm_g': 1.0 + 0.05 * n(ks[17], (d,), jnp.float32),
    }


def _fwd_reference(x, c, w_ada, b_ada, norm_mix_g, w_in, conv_short_w, w_short_out,
              conv_conf_w, conv_conf_b, conf_ln_g, conf_ln_b, w_conf_out, w_o,
              norm_ffn_g, w_ffn_in, w_ffn_out, final_norm_g):
    for l in range(DEPTH):
        mod = jax.nn.silu(c) @ w_ada[l] + b_ada[l]
        sh1, sc1, g1, sh2, sc2, g2 = jnp.split(mod[:, None, :], N_MOD, axis=-1)

        h = rms_norm(x, norm_mix_g[l]) * (1.0 + sc1) + sh1
        proj = jnp.einsum('bsd,de->bse', h, w_in[l])
        b_s, c_s, v_s, v_c, gl_c, gate_br = jnp.split(proj, SPLITS[:-1], axis=-1)

        y_a = b_s * causal_dwconv(c_s * v_s, conv_short_w[l])
        y_a = jnp.einsum('bsc,cd->bsd', y_a, w_short_out[l])

        u = v_c * jax.nn.sigmoid(gl_c)
        u = causal_dwconv(u, conv_conf_w[l]) + conv_conf_b[l]
        u = jax.nn.silu(layer_norm(u, conf_ln_g[l], conf_ln_b[l]))
        y_b = jnp.einsum('bsc,cd->bsd', u, w_conf_out[l])

        g_a, g_b = jnp.split(jax.nn.sigmoid(gate_br), 2, axis=-1)
        mix = jnp.einsum('bsd,de->bse', g_a * y_a + g_b * y_b, w_o[l])
        x = x + g1 * mix

        h2 = rms_norm(x, norm_ffn_g[l]) * (1.0 + sc2) + sh2
        a, bgate = jnp.split(jnp.einsum('bsd,df->bsf', h2, w_ffn_in[l]), 2, axis=-1)
        x = x + g2 * jnp.einsum('bsf,fd->bsd', jax.nn.silu(a) * bgate, w_ffn_out[l])

    return rms_norm(x, final_norm_g)


import jax as _jax
import jax.numpy as _jnp

TWIN_FORMAT = 'train_step'
FWD_PARAMS = ['x', 'c', 'w_ada', 'b_ada', 'norm_mix_g', 'w_in', 'conv_short_w', 'w_short_out', 'conv_conf_w', 'conv_conf_b', 'conf_ln_g', 'conf_ln_b', 'w_conf_out', 'w_o', 'norm_ffn_g', 'w_ffn_in', 'w_ffn_out', 'final_norm_g']
TWIN_WEIGHTS = ['w_ada', 'b_ada', 'norm_mix_g', 'w_in', 'conv_short_w', 'w_short_out', 'conv_conf_w', 'conv_conf_b', 'conf_ln_g', 'conf_ln_b', 'w_conf_out', 'w_o', 'norm_ffn_g', 'w_ffn_in', 'w_ffn_out', 'final_norm_g']
TWIN_DIFF_INPUT = 'x'
TWIN_INPUTS = ['x', 'c', 'w_ada', 'b_ada', 'norm_mix_g', 'w_in', 'conv_short_w', 'w_short_out', 'conv_conf_w', 'conv_conf_b', 'conf_ln_g', 'conf_ln_b', 'w_conf_out', 'w_o', 'norm_ffn_g', 'w_ffn_in', 'w_ffn_out', 'final_norm_g', 'loss_target', 'm_w_ada', 'm_b_ada', 'm_norm_mix_g', 'm_w_in', 'm_conv_short_w', 'm_w_short_out', 'm_conv_conf_w', 'm_conv_conf_b', 'm_conf_ln_g', 'm_conf_ln_b', 'm_w_conf_out', 'm_w_o', 'm_norm_ffn_g', 'm_w_ffn_in', 'm_w_ffn_out', 'm_final_norm_g', 'v_w_ada', 'v_b_ada', 'v_norm_mix_g', 'v_w_in', 'v_conv_short_w', 'v_w_short_out', 'v_conv_conf_w', 'v_conv_conf_b', 'v_conf_ln_g', 'v_conf_ln_b', 'v_w_conf_out', 'v_w_o', 'v_norm_ffn_g', 'v_w_ffn_in', 'v_w_ffn_out', 'v_final_norm_g']
TWIN_OUTPUTS = ['loss', 'grad_x', 'grad_w_ada', 'grad_b_ada', 'grad_norm_mix_g', 'grad_w_in', 'grad_conv_short_w', 'grad_w_short_out', 'grad_conv_conf_w', 'grad_conv_conf_b', 'grad_conf_ln_g', 'grad_conf_ln_b', 'grad_w_conf_out', 'grad_w_o', 'grad_norm_ffn_g', 'grad_w_ffn_in', 'grad_w_ffn_out', 'grad_final_norm_g', 'delta_w_ada', 'delta_b_ada', 'delta_norm_mix_g', 'delta_w_in', 'delta_conv_short_w', 'delta_w_short_out', 'delta_conv_conf_w', 'delta_conv_conf_b', 'delta_conf_ln_g', 'delta_conf_ln_b', 'delta_w_conf_out', 'delta_w_o', 'delta_norm_ffn_g', 'delta_w_ffn_in', 'delta_w_ffn_out', 'delta_final_norm_g', 'new_m_w_ada', 'new_m_b_ada', 'new_m_norm_mix_g', 'new_m_w_in', 'new_m_conv_short_w', 'new_m_w_short_out', 'new_m_conv_conf_w', 'new_m_conv_conf_b', 'new_m_conf_ln_g', 'new_m_conf_ln_b', 'new_m_w_conf_out', 'new_m_w_o', 'new_m_norm_ffn_g', 'new_m_w_ffn_in', 'new_m_w_ffn_out', 'new_m_final_norm_g', 'new_v_w_ada', 'new_v_b_ada', 'new_v_norm_mix_g', 'new_v_w_in', 'new_v_conv_short_w', 'new_v_w_short_out', 'new_v_conv_conf_w', 'new_v_conv_conf_b', 'new_v_conf_ln_g', 'new_v_conf_ln_b', 'new_v_w_conf_out', 'new_v_w_o', 'new_v_norm_ffn_g', 'new_v_w_ffn_in', 'new_v_w_ffn_out', 'new_v_final_norm_g']
TWIN_LEAF_KINDS = {'loss': 'loss', 'grad_x': 'grad_x', 'grad_w_ada': 'grad_w', 'grad_b_ada': 'grad_w', 'grad_norm_mix_g': 'grad_w', 'grad_w_in': 'grad_w', 'grad_conv_short_w': 'grad_w', 'grad_w_short_out': 'grad_w', 'grad_conv_conf_w': 'grad_w', 'grad_conv_conf_b': 'grad_w', 'grad_conf_ln_g': 'grad_w', 'grad_conf_ln_b': 'grad_w', 'grad_w_conf_out': 'grad_w', 'grad_w_o': 'grad_w', 'grad_norm_ffn_g': 'grad_w', 'grad_w_ffn_in': 'grad_w', 'grad_w_ffn_out': 'grad_w', 'grad_final_norm_g': 'grad_w', 'delta_w_ada': 'delta_w', 'delta_b_ada': 'delta_w', 'delta_norm_mix_g': 'delta_w', 'delta_w_in': 'delta_w', 'delta_conv_short_w': 'delta_w', 'delta_w_short_out': 'delta_w', 'delta_conv_conf_w': 'delta_w', 'delta_conv_conf_b': 'delta_w', 'delta_conf_ln_g': 'delta_w', 'delta_conf_ln_b': 'delta_w', 'delta_w_conf_out': 'delta_w', 'delta_w_o': 'delta_w', 'delta_norm_ffn_g': 'delta_w', 'delta_w_ffn_in': 'delta_w', 'delta_w_ffn_out': 'delta_w', 'delta_final_norm_g': 'delta_w', 'new_m_w_ada': 'new_m', 'new_m_b_ada': 'new_m', 'new_m_norm_mix_g': 'new_m', 'new_m_w_in': 'new_m', 'new_m_conv_short_w': 'new_m', 'new_m_w_short_out': 'new_m', 'new_m_conv_conf_w': 'new_m', 'new_m_conv_conf_b': 'new_m', 'new_m_conf_ln_g': 'new_m', 'new_m_conf_ln_b': 'new_m', 'new_m_w_conf_out': 'new_m', 'new_m_w_o': 'new_m', 'new_m_norm_ffn_g': 'new_m', 'new_m_w_ffn_in': 'new_m', 'new_m_w_ffn_out': 'new_m', 'new_m_final_norm_g': 'new_m', 'new_v_w_ada': 'new_v', 'new_v_b_ada': 'new_v', 'new_v_norm_mix_g': 'new_v', 'new_v_w_in': 'new_v', 'new_v_conv_short_w': 'new_v', 'new_v_w_short_out': 'new_v', 'new_v_conv_conf_w': 'new_v', 'new_v_conv_conf_b': 'new_v', 'new_v_conf_ln_g': 'new_v', 'new_v_conf_ln_b': 'new_v', 'new_v_w_conf_out': 'new_v', 'new_v_w_o': 'new_v', 'new_v_norm_ffn_g': 'new_v', 'new_v_w_ffn_in': 'new_v', 'new_v_w_ffn_out': 'new_v', 'new_v_final_norm_g': 'new_v'}


def _forward(args):
    return _fwd_reference(*[args[k] for k in FWD_PARAMS])


def _output_shape():
    out = _jax.eval_shape(lambda: _forward(_fwd_setup_inputs(0)))
    return out.shape, out.dtype

N_MICROBATCH = 1
ADAM_LR = 0.001
ADAM_B1 = 0.9
ADAM_B2 = 0.999
ADAM_EPS = 1e-08
ADAM_WD = 0.01
ADAM_STEP = 10
PER_EXAMPLE_BATCH_AXIS = {'x': 0, 'c': 0, 'loss_target': 0}
SHARED_INPUTS = []
_WEIGHT_DTYPES = {'w_ada': _jnp.float32, 'b_ada': _jnp.float32, 'norm_mix_g': _jnp.float32, 'w_in': _jnp.float32, 'conv_short_w': _jnp.float32, 'w_short_out': _jnp.float32, 'conv_conf_w': _jnp.float32, 'conv_conf_b': _jnp.float32, 'conf_ln_g': _jnp.float32, 'conf_ln_b': _jnp.float32, 'w_conf_out': _jnp.float32, 'w_o': _jnp.float32, 'norm_ffn_g': _jnp.float32, 'w_ffn_in': _jnp.float32, 'w_ffn_out': _jnp.float32, 'final_norm_g': _jnp.float32}
MOMENT_SCALE = {'w_ada': 1.162929e-01, 'b_ada': 2.121064e-01, 'norm_mix_g': 1.032684e-01, 'w_in': 3.905999e-02, 'conv_short_w': 5.521462e-02, 'w_short_out': 5.640649e-02, 'conv_conf_w': 2.646277e-02, 'conv_conf_b': 5.149115e-02, 'conf_ln_g': 3.166692e-02, 'conf_ln_b': 2.654955e-02, 'w_conf_out': 2.551976e-02, 'w_o': 6.211557e-02, 'norm_ffn_g': 7.730129e-02, 'w_ffn_in': 3.272872e-02, 'w_ffn_out': 5.341366e-02, 'final_norm_g': 6.424944e+01}


def _to_microbatches(a, axis):
    t = _jnp.moveaxis(a, axis, 0)
    t = t.reshape((N_MICROBATCH, t.shape[0] // N_MICROBATCH) + t.shape[1:])
    return _jnp.moveaxis(t, 1, axis + 1)


def setup_inputs(seed: int = 0) -> dict:
    inp = _fwd_setup_inputs(seed)
    key = _jax.random.fold_in(_jax.random.key(seed), 7919)
    shape, _ = _output_shape()
    out = dict(inp)
    out["loss_target"] = _jax.random.normal(_jax.random.fold_in(key, 0), shape, _jnp.float32)
    for i, name in enumerate(TWIN_WEIGHTS):
        w = inp[name].astype(_jnp.float32)
        if MOMENT_SCALE is None:
            s = _jnp.sqrt(_jnp.mean(_jnp.square(w)) + 1e-30)
        else:
            s = MOMENT_SCALE[name]
        km, kv = _jax.random.split(_jax.random.fold_in(key, i + 1))
        out[name] = w
        out["m_" + name] = s * _jax.random.normal(km, w.shape, _jnp.float32)
        out["v_" + name] = (s * s) * _jax.random.uniform(kv, w.shape, _jnp.float32, 0.5, 1.5)
    if N_MICROBATCH > 1:
        for name, axis in PER_EXAMPLE_BATCH_AXIS.items():
            out[name] = _to_microbatches(out[name], axis)
    return {'x': out['x'], 'c': out['c'], 'w_ada': out['w_ada'], 'b_ada': out['b_ada'], 'norm_mix_g': out['norm_mix_g'], 'w_in': out['w_in'], 'conv_short_w': out['conv_short_w'], 'w_short_out': out['w_short_out'], 'conv_conf_w': out['conv_conf_w'], 'conv_conf_b': out['conv_conf_b'], 'conf_ln_g': out['conf_ln_g'], 'conf_ln_b': out['conf_ln_b'], 'w_conf_out': out['w_conf_out'], 'w_o': out['w_o'], 'norm_ffn_g': out['norm_ffn_g'], 'w_ffn_in': out['w_ffn_in'], 'w_ffn_out': out['w_ffn_out'], 'final_norm_g': out['final_norm_g'], 'loss_target': out['loss_target'], 'm_w_ada': out['m_w_ada'], 'm_b_ada': out['m_b_ada'], 'm_norm_mix_g': out['m_norm_mix_g'], 'm_w_in': out['m_w_in'], 'm_conv_short_w': out['m_conv_short_w'], 'm_w_short_out': out['m_w_short_out'], 'm_conv_conf_w': out['m_conv_conf_w'], 'm_conv_conf_b': out['m_conv_conf_b'], 'm_conf_ln_g': out['m_conf_ln_g'], 'm_conf_ln_b': out['m_conf_ln_b'], 'm_w_conf_out': out['m_w_conf_out'], 'm_w_o': out['m_w_o'], 'm_norm_ffn_g': out['m_norm_ffn_g'], 'm_w_ffn_in': out['m_w_ffn_in'], 'm_w_ffn_out': out['m_w_ffn_out'], 'm_final_norm_g': out['m_final_norm_g'], 'v_w_ada': out['v_w_ada'], 'v_b_ada': out['v_b_ada'], 'v_norm_mix_g': out['v_norm_mix_g'], 'v_w_in': out['v_w_in'], 'v_conv_short_w': out['v_conv_short_w'], 'v_w_short_out': out['v_w_short_out'], 'v_conv_conf_w': out['v_conv_conf_w'], 'v_conv_conf_b': out['v_conv_conf_b'], 'v_conf_ln_g': out['v_conf_ln_g'], 'v_conf_ln_b': out['v_conf_ln_b'], 'v_w_conf_out': out['v_w_conf_out'], 'v_w_o': out['v_w_o'], 'v_norm_ffn_g': out['v_norm_ffn_g'], 'v_w_ffn_in': out['v_w_ffn_in'], 'v_w_ffn_out': out['v_w_ffn_out'], 'v_final_norm_g': out['v_final_norm_g']}


def _loss(weights, diff, rest, loss_target):
    with _jax.named_scope("forward"):
        args = {**rest, TWIN_DIFF_INPUT: diff, **{k: w.astype(_WEIGHT_DTYPES[k]) for k, w in weights.items()}}
        y = _forward(args)
    with _jax.named_scope("loss_head"):
        err = _jnp.square(y.astype(_jnp.float32) - loss_target)
        return 0.5 * _jnp.sum(_jnp.mean(err, axis=-1)) if err.ndim else 0.5 * err


def _adamw(w, g, m, v):
    m = ADAM_B1 * m + (1.0 - ADAM_B1) * g
    v = ADAM_B2 * v + (1.0 - ADAM_B2) * _jnp.square(g)
    m_hat = m / (1.0 - ADAM_B1 ** ADAM_STEP)
    v_hat = v / (1.0 - ADAM_B2 ** ADAM_STEP)
    delta = -ADAM_LR * (m_hat / (_jnp.sqrt(v_hat) + ADAM_EPS) + ADAM_WD * w)
    return delta, m, v


def reference(x, c, w_ada, b_ada, norm_mix_g, w_in, conv_short_w, w_short_out, conv_conf_w, conv_conf_b, conf_ln_g, conf_ln_b, w_conf_out, w_o, norm_ffn_g, w_ffn_in, w_ffn_out, final_norm_g, loss_target, m_w_ada, m_b_ada, m_norm_mix_g, m_w_in, m_conv_short_w, m_w_short_out, m_conv_conf_w, m_conv_conf_b, m_conf_ln_g, m_conf_ln_b, m_w_conf_out, m_w_o, m_norm_ffn_g, m_w_ffn_in, m_w_ffn_out, m_final_norm_g, v_w_ada, v_b_ada, v_norm_mix_g, v_w_in, v_conv_short_w, v_w_short_out, v_conv_conf_w, v_conv_conf_b, v_conf_ln_g, v_conf_ln_b, v_w_conf_out, v_w_o, v_norm_ffn_g, v_w_ffn_in, v_w_ffn_out, v_final_norm_g):
    given = dict(x=x, c=c, w_ada=w_ada, b_ada=b_ada, norm_mix_g=norm_mix_g, w_in=w_in, conv_short_w=conv_short_w, w_short_out=w_short_out, conv_conf_w=conv_conf_w, conv_conf_b=conv_conf_b, conf_ln_g=conf_ln_g, conf_ln_b=conf_ln_b, w_conf_out=w_conf_out, w_o=w_o, norm_ffn_g=norm_ffn_g, w_ffn_in=w_ffn_in, w_ffn_out=w_ffn_out, final_norm_g=final_norm_g, loss_target=loss_target, m_w_ada=m_w_ada, m_b_ada=m_b_ada, m_norm_mix_g=m_norm_mix_g, m_w_in=m_w_in, m_conv_short_w=m_conv_short_w, m_w_short_out=m_w_short_out, m_conv_conf_w=m_conv_conf_w, m_conv_conf_b=m_conv_conf_b, m_conf_ln_g=m_conf_ln_g, m_conf_ln_b=m_conf_ln_b, m_w_conf_out=m_w_conf_out, m_w_o=m_w_o, m_norm_ffn_g=m_norm_ffn_g, m_w_ffn_in=m_w_ffn_in, m_w_ffn_out=m_w_ffn_out, m_final_norm_g=m_final_norm_g, v_w_ada=v_w_ada, v_b_ada=v_b_ada, v_norm_mix_g=v_norm_mix_g, v_w_in=v_w_in, v_conv_short_w=v_conv_short_w, v_w_short_out=v_w_short_out, v_conv_conf_w=v_conv_conf_w, v_conv_conf_b=v_conv_conf_b, v_conf_ln_g=v_conf_ln_g, v_conf_ln_b=v_conf_ln_b, v_w_conf_out=v_w_conf_out, v_w_o=v_w_o, v_norm_ffn_g=v_norm_ffn_g, v_w_ffn_in=v_w_ffn_in, v_w_ffn_out=v_w_ffn_out, v_final_norm_g=v_final_norm_g)
    weights = {n: given[n] for n in TWIN_WEIGHTS}
    shared = {n: given[n] for n in SHARED_INPUTS}
    per_example = {n: given[n] for n in ['x', 'c']}
    grad_fn = _jax.value_and_grad(_loss, argnums=(0, 1))

    def one_microbatch(ex, loss_target):
        ex = dict(ex)
        diff = ex.pop(TWIN_DIFF_INPUT)
        return grad_fn(weights, diff, {**shared, **ex}, loss_target)

    if N_MICROBATCH == 1:
        loss, (grad_w, grad_x) = one_microbatch(per_example, given["loss_target"])
    else:
        def body(carry, xs):
            loss_sum, grad_sum = carry
            l_k, (gw_k, gx_k) = one_microbatch(xs[0], xs[1])
            with _jax.named_scope("update"):
                return (loss_sum + l_k, _jax.tree.map(_jnp.add, grad_sum, gw_k)), gx_k

        init = (_jnp.zeros((), _jnp.float32), _jax.tree.map(_jnp.zeros_like, weights))
        (loss, grad_w), grad_x = _jax.lax.scan(body, init, (per_example, given["loss_target"]))
    with _jax.named_scope("update"):
        delta_w, new_m, new_v = {}, {}, {}
        for n in TWIN_WEIGHTS:
            delta_w[n], new_m[n], new_v[n] = _adamw(weights[n], grad_w[n], given["m_" + n], given["v_" + n])
    return (loss, grad_x, *[grad_w[n] for n in TWIN_WEIGHTS], *[delta_w[n] for n in TWIN_WEIGHTS],
            *[new_m[n] for n in TWIN_WEIGHTS], *[new_v[n] for n in TWIN_WEIGHTS])
```

```python
import functools

import jax
import jax.numpy as jnp
from jax import lax
from jax.experimental import pallas as pl
from jax.experimental.pallas import tpu as pltpu

F32 = jnp.float32
BF16 = jnp.bfloat16
MESH = pl.DeviceIdType.MESH

N_DEV = 8
EPS = 1e-6
LN_EPS = 1e-5
SHORT_K = 3
CONF_K = 31
ADAM_LR = 0.001
ADAM_B1 = 0.9
ADAM_B2 = 0.999
ADAM_EPS = 1e-08
ADAM_WD = 0.01
ADAM_STEP = 10

LANES = 128
SUBLANES = 8
CONV_ROWS = 64
HALO_SHORT = 8
HALO_CONF = 32
VMEM_LIMIT = 56 * 1024 * 1024

M_SH1, M_SC1, M_G1, M_SH2, M_SC2, M_G2 = range(6)
P_GMIX, P_GFFN, P_GFIN, P_CBIAS, P_LNG, P_LNB = range(6)


def _params(sem=None, **kw):
    return pltpu.CompilerParams(dimension_semantics=sem, vmem_limit_bytes=VMEM_LIMIT, **kw)


def _sigmoid(v):
    return jax.nn.sigmoid(v)


def _dot(a, b):
    return jnp.dot(a, b, preferred_element_type=F32)


def _dot_nt(a, b):
    return lax.dot_general(a, b, (((1,), (1,)), ((), ())), preferred_element_type=F32)


def _dot_tn(a, b):
    return lax.dot_general(a, b, (((0,), (0,)), ((), ())), preferred_element_type=F32)


def _colsum(v):
    return jnp.sum(v, axis=0, keepdims=True)


def _rowmean(v):
    return jnp.mean(v, axis=-1, keepdims=True)


def _my_coords():
    return lax.axis_index("x"), lax.axis_index("y"), lax.axis_index("c")


def _slot(px, py, pc):
    return 4 * px + 2 * py + pc


def _flip(v, bit):
    return 1 - v if bit else v


def _taps_by_residue(taps):
    by_res = {}
    for wi, off in taps:
        by_res.setdefault(off % SUBLANES, []).append((wi, off // SUBLANES))
    return sorted(by_res.items())


def _tap_conv(ext_ref, w_ref, taps, tm, extra, out_ref):
    d = out_ref.shape[1]
    rb = min(CONV_ROWS, tm)
    wrows = rb + extra
    groups = _taps_by_residue(taps)

    def block(i, carry):
        base = pl.multiple_of(i * rb, SUBLANES)
        for lc in range(d // LANES):
            ls = pl.ds(lc * LANES, LANES)
            win = ext_ref[pl.ds(base, wrows), ls]
            acc = None
            for r, lst in groups:
                sh = win if r == 0 else pltpu.roll(win, wrows - r, 0)
                for wi, q in lst:
                    term = w_ref[wi:wi + 1, ls] * sh[SUBLANES * q:SUBLANES * q + rb, :]
                    acc = term if acc is None else acc + term
            out_ref[pl.ds(base, rb), ls] = acc
        return carry

    lax.fori_loop(0, tm // rb, block, 0)


def _tap_wgrad(a_ref, ext_ref, taps, tm, extra, acc_ref):
    d = a_ref.shape[1]
    rb = min(CONV_ROWS, tm)
    wrows = rb + extra
    groups = _taps_by_residue(taps)

    def block(i, carry):
        base = pl.multiple_of(i * rb, SUBLANES)
        for lc in range(d // LANES):
            ls = pl.ds(lc * LANES, LANES)
            a_blk = a_ref[pl.ds(base, rb), ls]
            win = ext_ref[pl.ds(base, wrows), ls]
            for r, lst in groups:
                sh = win if r == 0 else pltpu.roll(win, wrows - r, 0)
                for wi, q in lst:
                    prod = a_blk * sh[SUBLANES * q:SUBLANES * q + rb, :]
                    part = prod[0:SUBLANES, :]
                    for s in range(1, rb // SUBLANES):
                        part = part + prod[SUBLANES * s:SUBLANES * (s + 1), :]
                    rows = pl.ds(SUBLANES * wi, SUBLANES)
                    acc_ref[rows, ls] = acc_ref[rows, ls] + part
        return carry

    lax.fori_loop(0, tm // rb, block, 0)


def _causal_taps(k, halo):
    return [(i, halo - (k - 1) + i) for i in range(k)]


def _anticausal_taps(k):
    return [(i, (k - 1) - i) for i in range(k)]


def _ada_forward(c, w_ada_loc, b_ada, taps_loc):
    d = c.shape[1]
    nloc = w_ada_loc.shape[1]
    trows = taps_loc.shape[0]

    def body(c_ref, w_ref, b_ref, t_ref, mod_ref, sc_ref, taps_ref,
             part_ref, modrecv_ref, tapsall_ref, send_sems, recv_sems):
        x, y, cc = _my_coords()
        me = _slot(x, y, cc)
        cv = c_ref[...]
        sc_ref[me] = cv * _sigmoid(cv)
        tapsall_ref[me] = t_ref[...]

        def peer_of(k):
            return (_flip(x, k & 4), _flip(y, k & 2), _flip(cc, k & 1))

        def gather_copy(ref, base, k):
            return pltpu.make_async_remote_copy(
                src_ref=ref.at[me], dst_ref=ref.at[me], send_sem=send_sems.at[base + k - 1],
                recv_sem=recv_sems.at[base + k - 1], device_id=peer_of(k), device_id_type=MESH)

        first = [gather_copy(sc_ref, 0, k) for k in range(1, N_DEV)]
        first += [gather_copy(tapsall_ref, 7, k) for k in range(1, N_DEV)]
        for cp in first:
            cp.start()
        for cp in first[:7]:
            cp.wait_recv()
        sc_all = jnp.concatenate([sc_ref[s] for s in range(N_DEV)], axis=0)
        part = jnp.dot(sc_all, w_ref[...], preferred_element_type=F32,
                       precision=lax.Precision.HIGHEST)
        for b in range(N_DEV):
            part_ref[b] = part[b:b + 1, :]
        modrecv_ref[me] = part_ref[me]
        second = []
        for k in range(1, N_DEV):
            px, py, pc = peer_of(k)
            second.append(pltpu.make_async_remote_copy(
                src_ref=part_ref.at[_slot(px, py, pc)], dst_ref=modrecv_ref.at[me],
                send_sem=send_sems.at[14 + k - 1], recv_sem=recv_sems.at[14 + k - 1],
                device_id=(px, py, pc), device_id_type=MESH))
        for cp in second:
            cp.start()
        for cp in second:
            cp.wait_recv()
        mod = jnp.concatenate([modrecv_ref[s] for s in range(N_DEV)], axis=1)
        mod_ref[...] = mod + b_ref[...]
        for cp in first[7:]:
            cp.wait_recv()
        taps_ref[...] = jnp.concatenate([tapsall_ref[s] for s in range(N_DEV)], axis=1)
        for cp in first + second:
            cp.wait_send()

    return pl.pallas_call(
        body, name="ada_forward",
        out_shape=(jax.ShapeDtypeStruct((1, N_DEV * nloc), F32),
                   jax.ShapeDtypeStruct((N_DEV, 1, d), F32),
                   jax.ShapeDtypeStruct((trows, N_DEV * LANES), F32)),
        in_specs=[pl.BlockSpec(memory_space=pltpu.VMEM)] * 4,
        out_specs=[pl.BlockSpec(memory_space=pltpu.VMEM)] * 3,
        scratch_shapes=[pltpu.VMEM((N_DEV, 1, nloc), F32), pltpu.VMEM((N_DEV, 1, nloc), F32),
                        pltpu.VMEM((N_DEV, trows, LANES), F32),
                        pltpu.SemaphoreType.DMA((21,)), pltpu.SemaphoreType.DMA((21,))],
        compiler_params=_params(),
    )(c, w_ada_loc, b_ada, taps_loc)


def _gather_weights(shards):
    na = len(shards)

    def body(*refs):
        ins, outs = refs[:na], refs[na:2 * na]
        send_sems, recv_sems, local_sems = refs[2 * na:]
        x, y, c = _my_coords()
        me, sibling = (x, y, c), (x, y, 1 - c)
        chips = [(1 - x, y), (x, 1 - y), (1 - x, 1 - y)]

        def copy(a, k, block, to, src=None):
            dst = outs[a].at[_slot(*block)]
            return pltpu.make_async_remote_copy(
                src_ref=dst if src is None else src, dst_ref=dst,
                send_sem=send_sems.at[7 * a + k], recv_sem=recv_sems.at[7 * a + k],
                device_id=to, device_id_type=MESH)

        mine = [pltpu.make_async_copy(ins[a], outs[a].at[_slot(*me)], local_sems.at[a]) for a in range(na)]
        for cp in mine:
            cp.start()
        first = []
        for a in range(na):
            first.append(copy(a, 0, me, sibling, src=ins[a]))
            first += [copy(a, 1 + j, me, (*chip, c), src=ins[a]) for j, chip in enumerate(chips)]
        for cp in first:
            cp.start()
        passed = []
        for a in range(na):
            for j, chip in enumerate(chips):
                copy(a, 1 + j, (*chip, c), me).wait_recv()
                fwd = copy(a, 4 + j, (*chip, c), sibling)
                fwd.start()
                passed.append(fwd)
        for a in range(na):
            copy(a, 0, sibling, me).wait_recv()
            for j, chip in enumerate(chips):
                copy(a, 4 + j, (*chip, 1 - c), me).wait_recv()
        for cp in first + passed:
            cp.wait_send()
        for cp in mine:
            cp.wait()

    any_spec = pl.BlockSpec(memory_space=pl.ANY)
    return pl.pallas_call(
        body, name="gather_weights",
        out_shape=tuple(jax.ShapeDtypeStruct((N_DEV,) + s.shape, s.dtype) for s in shards),
        in_specs=[any_spec] * na, out_specs=[any_spec] * na,
        scratch_shapes=[pltpu.SemaphoreType.DMA((7 * na,)), pltpu.SemaphoreType.DMA((7 * na,)),
                        pltpu.SemaphoreType.DMA((na,))],
        compiler_params=_params(),
    )(*shards)


def _scatter_grads(partials):
    na = len(partials)

    def body(*refs):
        ins, outs = refs[:na], refs[na:2 * na]
        send_sems, recv_sems, local_sems = refs[2 * na:]
        x, y, c = _my_coords()
        me = _slot(x, y, c)
        mine = [pltpu.make_async_copy(ins[a].at[me], outs[a].at[me], local_sems.at[a]) for a in range(na)]
        for cp in mine:
            cp.start()
        copies = []
        for k in range(1, N_DEV):
            peer = (_flip(x, k & 4), _flip(y, k & 2), _flip(c, k & 1))
            for a in range(na):
                copies.append(pltpu.make_async_remote_copy(
                    src_ref=ins[a].at[_slot(*peer)], dst_ref=outs[a].at[me],
                    send_sem=send_sems.at[7 * a + k - 1], recv_sem=recv_sems.at[7 * a + k - 1],
                    device_id=peer, device_id_type=MESH))
        for cp in copies:
            cp.start()
        for cp in copies:
            cp.wait_recv()
        for cp in copies:
            cp.wait_send()
        for cp in mine:
            cp.wait()

    any_spec = pl.BlockSpec(memory_space=pl.ANY)
    return pl.pallas_call(
        body, name="scatter_grads",
        out_shape=tuple(jax.ShapeDtypeStruct(p.shape, p.dtype) for p in partials),
        in_specs=[any_spec] * na, out_specs=[any_spec] * na,
        scratch_shapes=[pltpu.SemaphoreType.DMA((7 * na,)), pltpu.SemaphoreType.DMA((7 * na,)),
                        pltpu.SemaphoreType.DMA((na,))],
        compiler_params=_params(),
    )(*partials)


def _forward_in(x, mod, prm, w_in_g, tm):
    t, d = x.shape
    ns, _, n = w_in_g.shape

    def body(x_ref, mod_ref, prm_ref, w_ref, proj_ref, h_ref, hs_ref):
        @pl.when(pl.program_id(1) == 0)
        def _():
            xv = x_ref[...]
            r = lax.rsqrt(_rowmean(xv * xv) + EPS)
            h = xv * r * prm_ref[P_GMIX:P_GMIX + 1, :] * (1.0 + mod_ref[M_SC1:M_SC1 + 1, :]) \
                + mod_ref[M_SH1:M_SH1 + 1, :]
            hb = h.astype(BF16)
            hs_ref[...] = hb
            h_ref[...] = hb

        proj_ref[...] = _dot(hs_ref[...], w_ref[0]).astype(BF16)

    return pl.pallas_call(
        body, name="forward_in", grid=(t // tm, ns),
        out_shape=(jax.ShapeDtypeStruct((t, ns * n), BF16), jax.ShapeDtypeStruct((t, d), BF16)),
        in_specs=[pl.BlockSpec((tm, d), lambda i, j: (i, 0)),
                  pl.BlockSpec((8, d), lambda i, j: (0, 0)),
                  pl.BlockSpec((8, d), lambda i, j: (0, 0)),
                  pl.BlockSpec((1, d, n), lambda i, j: (j, 0, 0))],
        out_specs=[pl.BlockSpec((tm, n), lambda i, j: (i, j)),
                   pl.BlockSpec((tm, d), lambda i, j: (i, 0))],
        scratch_shapes=[pltpu.VMEM((tm, d), BF16)],
        compiler_params=_params(("arbitrary", "arbitrary")),
    )(x, mod, prm, w_in_g)


def _forward_mix(x, proj, mod, prm, taps, w_so, w_co, w_o, tm):
    t, d = x.shape
    hs, hc = HALO_SHORT, HALO_CONF
    r3, r31 = 0, 8

    def body(x_ref, proj_ref, mod_ref, prm_ref, taps_ref, wso_ref, wco_ref, wo_ref,
             x1_ref, ya_ref, yb_ref, mix_ref, u1_ref,
             cv_ext, u0_ext, conv3_ref, u1f_ref):
        i = pl.program_id(0)

        @pl.when(i == 0)
        def _():
            cv_ext[0:hs, :] = jnp.zeros((hs, d), F32)
            u0_ext[0:hc, :] = jnp.zeros((hc, d), F32)

        def col(g):
            return proj_ref[:, g * d:(g + 1) * d].astype(F32)

        cv_ext[hs:hs + tm, :] = col(1) * col(2)
        u0_ext[hc:hc + tm, :] = col(3) * _sigmoid(col(4))
        _tap_conv(cv_ext, taps_ref, [(r3 + wi, off) for wi, off in _causal_taps(SHORT_K, hs)], tm, hs, conv3_ref)
        _tap_conv(u0_ext, taps_ref, [(r31 + wi, off) for wi, off in _causal_taps(CONF_K, hc)], tm, hc, u1f_ref)
        cv_ext[0:hs, :] = cv_ext[tm:tm + hs, :]
        u0_ext[0:hc, :] = u0_ext[tm:tm + hc, :]

        ya_pre = (col(0) * conv3_ref[...]).astype(BF16)
        y_a = _dot(ya_pre, wso_ref[...])
        u1 = u1f_ref[...] + prm_ref[P_CBIAS:P_CBIAS + 1, :]
        u1_ref[...] = u1.astype(BF16)
        u1 = u1_ref[...].astype(F32)
        mu = _rowmean(u1)
        uc = u1 - mu
        rstd = lax.rsqrt(_rowmean(uc * uc) + LN_EPS)
        u2 = uc * rstd * prm_ref[P_LNG:P_LNG + 1, :] + prm_ref[P_LNB:P_LNB + 1, :]
        u3 = (u2 * _sigmoid(u2)).astype(BF16)
        y_b = _dot(u3, wco_ref[...])
        ya_ref[...] = y_a.astype(BF16)
        yb_ref[...] = y_b.astype(BF16)
        merged = _sigmoid(col(5)) * ya_ref[...].astype(F32) + _sigmoid(col(6)) * yb_ref[...].astype(F32)
        mix = _dot(merged.astype(BF16), wo_ref[...])
        mix_ref[...] = mix.astype(BF16)
        x1_ref[...] = x_ref[...] + mod_ref[M_G1:M_G1 + 1, :] * mix

    tile = pl.BlockSpec((tm, d), lambda i: (i, 0))
    whole = lambda shape: pl.BlockSpec(shape, lambda i: (0,) * len(shape))
    return pl.pallas_call(
        body, name="forward_mix", grid=(t // tm,),
        out_shape=(jax.ShapeDtypeStruct((t, d), F32),) + (jax.ShapeDtypeStruct((t, d), BF16),) * 4,
        in_specs=[tile, pl.BlockSpec((tm, 7 * d), lambda i: (i, 0)), whole((8, d)), whole((8, d)),
                  whole(taps.shape), whole((d, d)), whole((d, d)), whole((d, d))],
        out_specs=[tile] * 5,
        scratch_shapes=[pltpu.VMEM((hs + tm, d), F32), pltpu.VMEM((hc + tm, d), F32),
                        pltpu.VMEM((tm, d), F32), pltpu.VMEM((tm, d), F32)],
        compiler_params=_params(("arbitrary",)),
    )(x, proj, mod, prm, taps, w_so, w_co, w_o)


def _forward_ffn(x1, tgt, mod, prm, w_fi_g, w_fo_g, tm):
    t, d = x1.shape
    ns, _, fb = w_fi_g.shape
    nh = ns // 2

    def body(x1_ref, tgt_ref, mod_ref, prm_ref, wfi_hbm, wfo_hbm,
             dx2_ref, ab_ref, h2_ref, sums_ref, wfi_ref, wfo_ref):
        i = pl.program_id(0)

        @pl.when(i == 0)
        def _():
            pltpu.sync_copy(wfi_hbm, wfi_ref)
            pltpu.sync_copy(wfo_hbm, wfo_ref)
            sums_ref[...] = jnp.zeros((8, d), F32)

        x1v = x1_ref[...]
        r2 = lax.rsqrt(_rowmean(x1v * x1v) + EPS)
        h2 = (x1v * r2 * prm_ref[P_GFFN:P_GFFN + 1, :] * (1.0 + mod_ref[M_SC2:M_SC2 + 1, :])
              + mod_ref[M_SH2:M_SH2 + 1, :]).astype(BF16)
        h2_ref[...] = h2
        f = jnp.zeros((tm, d), F32)
        for j in range(nh):
            ab_ref[j] = _dot(h2, wfi_ref[j]).astype(BF16)
            ab_ref[j + nh] = _dot(h2, wfi_ref[j + nh]).astype(BF16)
            a = ab_ref[j].astype(F32)
            act = (a * _sigmoid(a) * ab_ref[j + nh].astype(F32)).astype(BF16)
            f = f + _dot(act, wfo_ref[j])
        x2 = x1v + mod_ref[M_G2:M_G2 + 1, :] * f
        r3 = lax.rsqrt(_rowmean(x2 * x2) + EPS)
        xn3 = x2 * r3
        gfin = prm_ref[P_GFIN:P_GFIN + 1, :]
        err = xn3 * gfin - tgt_ref[...]
        dy = err * (1.0 / d)
        dxn3 = dy * gfin
        dx2 = r3 * (dxn3 - xn3 * _rowmean(dxn3 * xn3))
        dx2_ref[...] = dx2
        sums_ref[0:1, :] = sums_ref[0:1, :] + _colsum(dy * xn3)
        sums_ref[1:2, :] = sums_ref[1:2, :] + _colsum(dx2 * f)
        sums_ref[2:3, :] = sums_ref[2:3, :] + _colsum(err * err) * (0.5 / d)

    tile = pl.BlockSpec((tm, d), lambda i: (i, 0))
    whole = lambda shape: pl.BlockSpec(shape, lambda i: (0,) * len(shape))
    any_spec = pl.BlockSpec(memory_space=pl.ANY)
    return pl.pallas_call(
        body, name="forward_ffn", grid=(t // tm,),
        out_shape=(jax.ShapeDtypeStruct((t, d), F32), jax.ShapeDtypeStruct((ns, t, fb), BF16),
                   jax.ShapeDtypeStruct((t, d), BF16), jax.ShapeDtypeStruct((8, d), F32)),
        in_specs=[tile, tile, whole((8, d)), whole((8, d)), any_spec, any_spec],
        out_specs=[tile, pl.BlockSpec((ns, tm, fb), lambda i: (0, i, 0)), tile, whole((8, d))],
        scratch_shapes=[pltpu.VMEM(w_fi_g.shape, BF16), pltpu.VMEM(w_fo_g.shape, BF16)],
        compiler_params=_params(("arbitrary",)),
    )(x1, tgt, mod, prm, w_fi_g, w_fo_g)


def _backward_ffn(dx2, x1, ab, mod, prm, w_fi_g, w_fo_g, tm):
    t, d = x1.shape
    ns, _, fb = w_fi_g.shape
    nh = ns // 2

    def body(dx2_ref, x1_ref, ab_ref, mod_ref, prm_ref, wfi_hbm, wfo_hbm,
             dx1_ref, df_ref, act_ref, dab_ref, sums_ref, wfi_ref, wfo_ref):
        i = pl.program_id(0)

        @pl.when(i == 0)
        def _():
            pltpu.sync_copy(wfi_hbm, wfi_ref)
            pltpu.sync_copy(wfo_hbm, wfo_ref)
            sums_ref[...] = jnp.zeros((8, d), F32)

        dx2v = dx2_ref[...]
        df = (mod_ref[M_G2:M_G2 + 1, :] * dx2v).astype(BF16)
        df_ref[...] = df
        dh2 = jnp.zeros((tm, d), F32)
        for j in range(nh):
            dact = _dot_nt(df, wfo_ref[j])
            a = ab_ref[j].astype(F32)
            b = ab_ref[j + nh].astype(F32)
            s = _sigmoid(a)
            sil = a * s
            act_ref[j] = (sil * b).astype(BF16)
            da = (dact * b * (s * (1.0 + a * (1.0 - s)))).astype(BF16)
            db = (dact * sil).astype(BF16)
            dab_ref[j] = da
            dab_ref[j + nh] = db
            dh2 = dh2 + _dot_nt(da, wfi_ref[j]) + _dot_nt(db, wfi_ref[j + nh])
        x1v = x1_ref[...]
        r2 = lax.rsqrt(_rowmean(x1v * x1v) + EPS)
        xn2 = x1v * r2
        gffn = prm_ref[P_GFFN:P_GFFN + 1, :]
        scale = 1.0 + mod_ref[M_SC2:M_SC2 + 1, :]
        dxn2 = dh2 * gffn * scale
        dx1_ref[...] = dx2v + r2 * (dxn2 - xn2 * _rowmean(dxn2 * xn2))
        hx = dh2 * xn2
        sums_ref[0:1, :] = sums_ref[0:1, :] + _colsum(dh2)
        sums_ref[1:2, :] = sums_ref[1:2, :] + _colsum(hx) * gffn
        sums_ref[2:3, :] = sums_ref[2:3, :] + _colsum(hx) * scale

    tile = pl.BlockSpec((tm, d), lambda i: (i, 0))
    whole = lambda shape: pl.BlockSpec(shape, lambda i: (0,) * len(shape))
    any_spec = pl.BlockSpec(memory_space=pl.ANY)
    return pl.pallas_call(
        body, name="backward_ffn", grid=(t // tm,),
        out_shape=(jax.ShapeDtypeStruct((t, d), F32), jax.ShapeDtypeStruct((t, d), BF16),
                   jax.ShapeDtypeStruct((nh, t, fb), BF16), jax.ShapeDtypeStruct((ns, t, fb), BF16),
                   jax.ShapeDtypeStruct((8, d), F32)),
        in_specs=[tile, tile, pl.BlockSpec((ns, tm, fb), lambda i: (0, i, 0)), whole((8, d)), whole((8, d)),
                  any_spec, any_spec],
        out_specs=[tile, tile, pl.BlockSpec((nh, tm, fb), lambda i: (0, i, 0)),
                   pl.BlockSpec((ns, tm, fb), lambda i: (0, i, 0)), whole((8, d))],
        scratch_shapes=[pltpu.VMEM(w_fi_g.shape, BF16), pltpu.VMEM(w_fo_g.shape, BF16)],
        compiler_params=_params(("arbitrary",)),
    )(dx2, x1, ab, mod, prm, w_fi_g, w_fo_g)


def _backward_mix(dx1, proj, y_a, y_b, mix, u1, mod, prm, taps, w_so, w_co, w_o, tm):
    t, d = dx1.shape
    nt = t // tm
    hs, hc = HALO_SHORT, HALO_CONF
    r3, r31 = 0, 8

    def body(dx1_ref, proj_ref, halo_ref, ya_ref, yb_ref, mix_ref, u1_ref, mod_ref, prm_ref, taps_ref,
             wso_ref, wco_ref, wo_ref,
             dproj_ref, dmix_ref, dya_ref, dyb_ref, merged_ref, yapre_ref, u3_ref, sums_ref, dw3_ref, dw31_ref,
             cv_ext, u0_ext, d3_ext, du1_ext, tmp_ref):
        i = pl.program_id(0)
        first_tile = i == nt - 1

        @pl.when(i == 0)
        def _():
            sums_ref[...] = jnp.zeros((8, d), F32)
            dw3_ref[...] = jnp.zeros(dw3_ref.shape, F32)
            dw31_ref[...] = jnp.zeros(dw31_ref.shape, F32)
            d3_ext[tm:tm + hs, :] = jnp.zeros((hs, d), F32)
            du1_ext[tm:tm + hc, :] = jnp.zeros((hc, d), F32)

        def col(g):
            return proj_ref[:, g * d:(g + 1) * d].astype(F32)

        def hcol(g, rows):
            v = halo_ref[HALO_CONF - rows:HALO_CONF, g * d:(g + 1) * d].astype(F32)
            return jnp.where(first_tile, 0.0, v)

        dx1v = dx1_ref[...]
        mixv = mix_ref[...].astype(F32)
        dmix = (mod_ref[M_G1:M_G1 + 1, :] * dx1v).astype(BF16)
        dmix_ref[...] = dmix
        sums_ref[0:1, :] = sums_ref[0:1, :] + _colsum(dx1v * mixv)
        dmerged = _dot_nt(dmix, wo_ref[...])
        ga = _sigmoid(col(5))
        gb = _sigmoid(col(6))
        yav = ya_ref[...].astype(F32)
        ybv = yb_ref[...].astype(F32)
        dya = (dmerged * ga).astype(BF16)
        dyb = (dmerged * gb).astype(BF16)
        dya_ref[...] = dya
        dyb_ref[...] = dyb
        dproj_ref[:, 5 * d:6 * d] = (dmerged * yav * ga * (1.0 - ga)).astype(BF16)
        dproj_ref[:, 6 * d:7 * d] = (dmerged * ybv * gb * (1.0 - gb)).astype(BF16)
        merged_ref[...] = (ga * yav + gb * ybv).astype(BF16)

        dya_pre = _dot_nt(dya, wso_ref[...])
        c_s, v_s, b_s = col(1), col(2), col(0)
        cv_ext[0:hs, :] = hcol(1, hs) * hcol(2, hs)
        cv_ext[hs:hs + tm, :] = c_s * v_s
        _tap_conv(cv_ext, taps_ref, [(r3 + wi, off) for wi, off in _causal_taps(SHORT_K, hs)], tm, hs, tmp_ref)
        conv3 = tmp_ref[...]
        yapre_ref[...] = (b_s * conv3).astype(BF16)
        dproj_ref[:, 0:d] = (dya_pre * conv3).astype(BF16)
        d3_ext[0:tm, :] = dya_pre * b_s
        _tap_wgrad(d3_ext, cv_ext, _causal_taps(SHORT_K, hs), tm, hs, dw3_ref)
        _tap_conv(d3_ext, taps_ref, [(r3 + wi, off) for wi, off in _anticausal_taps(SHORT_K)], tm, hs, tmp_ref)
        dcv = tmp_ref[...]
        dproj_ref[:, d:2 * d] = (dcv * v_s).astype(BF16)
        dproj_ref[:, 2 * d:3 * d] = (dcv * c_s).astype(BF16)
        d3_ext[tm:tm + hs, :] = d3_ext[0:hs, :]

        du3 = _dot_nt(dyb, wco_ref[...])
        u1v = u1_ref[...].astype(F32)
        mu = _rowmean(u1v)
        uc = u1v - mu
        rstd = lax.rsqrt(_rowmean(uc * uc) + LN_EPS)
        uhat = uc * rstd
        lng = prm_ref[P_LNG:P_LNG + 1, :]
        u2 = uhat * lng + prm_ref[P_LNB:P_LNB + 1, :]
        s2 = _sigmoid(u2)
        u3_ref[...] = (u2 * s2).astype(BF16)
        du2 = du3 * (s2 * (1.0 + u2 * (1.0 - s2)))
        sums_ref[1:2, :] = sums_ref[1:2, :] + _colsum(du2 * uhat)
        sums_ref[2:3, :] = sums_ref[2:3, :] + _colsum(du2)
        duhat = du2 * lng
        du1 = rstd * (duhat - _rowmean(duhat) - uhat * _rowmean(duhat * uhat))
        sums_ref[3:4, :] = sums_ref[3:4, :] + _colsum(du1)
        du1_ext[0:tm, :] = du1
        v_c = col(3)
        sg = _sigmoid(col(4))
        u0_ext[0:hc, :] = hcol(3, hc) * _sigmoid(hcol(4, hc))
        u0_ext[hc:hc + tm, :] = v_c * sg
        _tap_wgrad(du1_ext, u0_ext, _causal_taps(CONF_K, hc), tm, hc, dw31_ref)
        _tap_conv(du1_ext, taps_ref, [(r31 + wi, off) for wi, off in _anticausal_taps(CONF_K)], tm, hc, tmp_ref)
        du0 = tmp_ref[...]
        dproj_ref[:, 3 * d:4 * d] = (du0 * sg).astype(BF16)
        dproj_ref[:, 4 * d:5 * d] = (du0 * v_c * sg * (1.0 - sg)).astype(BF16)
        du1_ext[tm:tm + hc, :] = du1_ext[0:hc, :]

    rev = lambda i: (nt - 1 - i, 0)
    tile = pl.BlockSpec((tm, d), rev)
    whole = lambda shape: pl.BlockSpec(shape, lambda i: (0,) * len(shape))
    hblocks = tm // HALO_CONF
    halo = pl.BlockSpec((HALO_CONF, 7 * d), lambda i: (jnp.maximum((nt - 1 - i) * hblocks - 1, 0), 0))
    bf = jax.ShapeDtypeStruct((t, d), BF16)
    return pl.pallas_call(
        body, name="backward_mix", grid=(nt,),
        out_shape=(jax.ShapeDtypeStruct((t, 7 * d), BF16), bf, bf, bf, bf, bf, bf,
                   jax.ShapeDtypeStruct((8, d), F32),
                   jax.ShapeDtypeStruct((SUBLANES * SHORT_K, d), F32),
                   jax.ShapeDtypeStruct((SUBLANES * CONF_K, d), F32)),
        in_specs=[tile, pl.BlockSpec((tm, 7 * d), rev), halo, tile, tile, tile, tile,
                  whole((8, d)), whole((8, d)), whole(taps.shape), whole((d, d)), whole((d, d)), whole((d, d))],
        out_specs=[pl.BlockSpec((tm, 7 * d), rev), tile, tile, tile, tile, tile, tile,
                   whole((8, d)), whole((SUBLANES * SHORT_K, d)), whole((SUBLANES * CONF_K, d))],
        scratch_shapes=[pltpu.VMEM((hs + tm, d), F32), pltpu.VMEM((hc + tm, d), F32),
                        pltpu.VMEM((tm + hs, d), F32), pltpu.VMEM((tm + hc, d), F32),
                        pltpu.VMEM((tm, d), F32)],
        compiler_params=_params(("arbitrary",)),
    )(dx1, proj, proj, y_a, y_b, mix, u1, mod, prm, taps, w_so, w_co, w_o)


def _backward_in(dproj, x, dx1, mod, prm, w_in_g, tm):
    t, d = x.shape
    ns, _, n = w_in_g.shape

    def body(dproj_ref, x_ref, dx1_ref, mod_ref, prm_ref, w_ref, gx_ref, sums_ref, dh_ref):
        i, j = pl.program_id(0), pl.program_id(1)

        @pl.when((i == 0) & (j == 0))
        def _():
            sums_ref[...] = jnp.zeros((8, d), F32)

        part = _dot_nt(dproj_ref[...], w_ref[0])

        @pl.when(j == 0)
        def _():
            dh_ref[...] = part

        @pl.when(j > 0)
        def _():
            dh_ref[...] = dh_ref[...] + part

        @pl.when(j == ns - 1)
        def _():
            dh = dh_ref[...]
            xv = x_ref[...]
            r1 = lax.rsqrt(_rowmean(xv * xv) + EPS)
            xn = xv * r1
            gmix = prm_ref[P_GMIX:P_GMIX + 1, :]
            scale = 1.0 + mod_ref[M_SC1:M_SC1 + 1, :]
            dxn = dh * gmix * scale
            gx_ref[...] = dx1_ref[...] + r1 * (dxn - xn * _rowmean(dxn * xn))
            hx = dh * xn
            sums_ref[0:1, :] = sums_ref[0:1, :] + _colsum(dh)
            sums_ref[1:2, :] = sums_ref[1:2, :] + _colsum(hx) * gmix
            sums_ref[2:3, :] = sums_ref[2:3, :] + _colsum(hx) * scale

    tile = pl.BlockSpec((tm, d), lambda i, j: (i, 0))
    whole = pl.BlockSpec((8, d), lambda i, j: (0, 0))
    return pl.pallas_call(
        body, name="backward_in", grid=(t // tm, ns),
        out_shape=(jax.ShapeDtypeStruct((t, d), F32), jax.ShapeDtypeStruct((8, d), F32)),
        in_specs=[pl.BlockSpec((tm, n), lambda i, j: (i, j)), tile, tile, whole, whole,
                  pl.BlockSpec((1, d, n), lambda i, j: (j, 0, 0))],
        out_specs=[tile, whole],
        scratch_shapes=[pltpu.VMEM((tm, d), F32)],
        compiler_params=_params(("arbitrary", "arbitrary")),
    )(dproj, x, dx1, mod, prm, w_in_g)


def _weight_grad(a, b, a_spec, b_spec, ns, m, n, nk, name):
    def body(a_ref, b_ref, o_ref, acc_ref):
        k = pl.program_id(1)
        av = a_ref[0] if len(a_ref.shape) == 3 else a_ref[...]
        bv = b_ref[0] if len(b_ref.shape) == 3 else b_ref[...]
        part = _dot_tn(av, bv)

        @pl.when(k == 0)
        def _():
            acc_ref[...] = part

        @pl.when(k > 0)
        def _():
            acc_ref[...] = acc_ref[...] + part

        @pl.when(k == nk - 1)
        def _():
            o_ref[0] = acc_ref[...].astype(BF16)

    return pl.pallas_call(
        body, name=name, grid=(ns, nk),
        out_shape=jax.ShapeDtypeStruct((ns, m, n), BF16),
        in_specs=[a_spec, b_spec],
        out_specs=pl.BlockSpec((1, m, n), lambda s, k: (s, 0, 0)),
        scratch_shapes=[pltpu.VMEM((m, n), F32)],
        compiler_params=_params(("arbitrary", "arbitrary")),
    )(a, b)


def _adamw(w, g, m, v):
    m = ADAM_B1 * m + (1.0 - ADAM_B1) * g
    v = ADAM_B2 * v + (1.0 - ADAM_B2) * (g * g)
    m_hat = m / (1.0 - ADAM_B1 ** ADAM_STEP)
    v_hat = v / (1.0 - ADAM_B2 ** ADAM_STEP)
    delta = -ADAM_LR * (m_hat / (jnp.sqrt(v_hat) + ADAM_EPS) + ADAM_WD * w)
    return delta, m, v


def _adamw_shard(parts, w, m, v, tr, name):
    r, c = w.shape

    def body(p_ref, w_ref, m_ref, v_ref, g_ref, d_ref, nm_ref, nv_ref):
        g = p_ref[0].astype(F32)
        for s in range(1, N_DEV):
            g = g + p_ref[s].astype(F32)
        delta, nm, nv = _adamw(w_ref[...], g, m_ref[...], v_ref[...])
        g_ref[...] = g
        d_ref[...] = delta
        nm_ref[...] = nm
        nv_ref[...] = nv

    tile = pl.BlockSpec((tr, c), lambda i: (i, 0))
    return pl.pallas_call(
        body, name=name, grid=(r // tr,),
        out_shape=(jax.ShapeDtypeStruct((r, c), F32),) * 4,
        in_specs=[pl.BlockSpec((N_DEV, tr, c), lambda i: (0, i, 0)), tile, tile, tile],
        out_specs=[tile] * 4,
        compiler_params=_params(("arbitrary",)),
    )(parts, w, m, v)


def _ada_update(sc_all, dmod_cols, w, m, v, tr):
    d, n = w.shape

    def body(sc_ref, dm_ref, w_ref, m_ref, v_ref, g_ref, d_ref, nm_ref, nv_ref):
        g = lax.dot_general(sc_ref[...], dm_ref[...], (((0,), (0,)), ((), ())),
                            preferred_element_type=F32, precision=lax.Precision.HIGHEST)
        delta, nm, nv = _adamw(w_ref[...], g, m_ref[...], v_ref[...])
        g_ref[...] = g
        d_ref[...] = delta
        nm_ref[...] = nm
        nv_ref[...] = nv

    tile = pl.BlockSpec((tr, n), lambda i: (i, 0))
    return pl.pallas_call(
        body, name="ada_update", grid=(d // tr,),
        out_shape=(jax.ShapeDtypeStruct((d, n), F32),) * 4,
        in_specs=[pl.BlockSpec((N_DEV, tr), lambda i: (0, i)), pl.BlockSpec((N_DEV, n), lambda i: (0, 0)),
                  tile, tile, tile],
        out_specs=[tile] * 4,
        compiler_params=_params(("arbitrary",)),
    )(sc_all, dmod_cols, w, m, v)


def _small_update(vec, cg, smalls):
    l = vec.shape[2]
    rows = cg.shape[1]
    ns = len(smalls)

    def body(*refs):
        vec_ref, cg_ref = refs[0], refs[1]
        wmv = refs[2:2 + 3 * ns]
        vall_ref = refs[2 + 3 * ns]
        outs = refs[3 + 3 * ns:3 + 7 * ns]
        cgr_ref, send_sems, recv_sems = refs[3 + 7 * ns:]
        x, y, c = _my_coords()
        me = _slot(x, y, c)
        vall_ref[me] = vec_ref[0]
        cgr_ref[me] = cg_ref[me]
        copies = []
        for k in range(1, N_DEV):
            peer = (_flip(x, k & 4), _flip(y, k & 2), _flip(c, k & 1))
            copies.append(pltpu.make_async_remote_copy(
                src_ref=vall_ref.at[me], dst_ref=vall_ref.at[me], send_sem=send_sems.at[k - 1],
                recv_sem=recv_sems.at[k - 1], device_id=peer, device_id_type=MESH))
            copies.append(pltpu.make_async_remote_copy(
                src_ref=cg_ref.at[_slot(*peer)], dst_ref=cgr_ref.at[me], send_sem=send_sems.at[7 + k - 1],
                recv_sem=recv_sems.at[7 + k - 1], device_id=peer, device_id_type=MESH))
        for cp in copies:
            cp.start()
        for cp in copies:
            cp.wait_recv()
        for p, (_, _, _, lo, hi, kind) in enumerate(smalls):
            w_ref, m_ref, v_ref = wmv[3 * p:3 * p + 3]
            part = (lambda s: vall_ref[s, :, lo:hi]) if kind == "vec" else (lambda s: cgr_ref[s, lo:hi, :])
            g = part(0)
            for s in range(1, N_DEV):
                g = g + part(s)
            delta, nm, nv = _adamw(w_ref[...], g, m_ref[...], v_ref[...])
            for o_ref, val in zip(outs[4 * p:4 * p + 4], (g, delta, nm, nv)):
                o_ref[...] = val
        for cp in copies:
            cp.wait_send()

    vm = pl.BlockSpec(memory_space=pltpu.VMEM)
    args = [vec, cg]
    out_shape = [jax.ShapeDtypeStruct((N_DEV, 1, l), F32)]
    for w, m, v, _, _, _ in smalls:
        args += [w, m, v]
    for w, _, _, _, _, _ in smalls:
        out_shape += [jax.ShapeDtypeStruct(w.shape, F32)] * 4
    res = pl.pallas_call(
        body, name="small_update",
        out_shape=tuple(out_shape),
        in_specs=[vm] * len(args), out_specs=[vm] * len(out_shape),
        scratch_shapes=[pltpu.VMEM((N_DEV, rows, LANES), F32),
                        pltpu.SemaphoreType.DMA((14,)), pltpu.SemaphoreType.DMA((14,))],
        compiler_params=_params(),
    )(*args)
    return res[0], [res[1 + 4 * p:5 + 4 * p] for p in range(ns)]


def _pick(t, want):
    return want if t % want == 0 else t


def kernel(x, c, w_ada, b_ada, norm_mix_g, w_in, conv_short_w, w_short_out, conv_conf_w, conv_conf_b, conf_ln_g, conf_ln_b, w_conf_out, w_o, norm_ffn_g, w_ffn_in, w_ffn_out, final_norm_g, loss_target, m_w_ada, m_b_ada, m_norm_mix_g, m_w_in, m_conv_short_w, m_w_short_out, m_conv_conf_w, m_conv_conf_b, m_conf_ln_g, m_conf_ln_b, m_w_conf_out, m_w_o, m_norm_ffn_g, m_w_ffn_in, m_w_ffn_out, m_final_norm_g, v_w_ada, v_b_ada, v_norm_mix_g, v_w_in, v_conv_short_w, v_w_short_out, v_conv_conf_w, v_conv_conf_b, v_conf_ln_g, v_conf_ln_b, v_w_conf_out, v_w_o, v_norm_ffn_g, v_w_ffn_in, v_w_ffn_out, v_final_norm_g):
    t, d = x.shape[1], x.shape[2]
    x2 = x.reshape(t, d)
    tgt = loss_target.reshape(t, d)
    me = _slot(*_my_coords())
    tm = _pick(t, 256)
    tm_in = _pick(t, 512)
    tk = _pick(t, 512)

    taps_loc = jnp.zeros((40, LANES), F32)
    taps_loc = taps_loc.at[0:SHORT_K].set(conv_short_w[0]).at[8:8 + CONF_K].set(conv_conf_w[0])
    mod_flat, sc_all3, taps = _ada_forward(c, w_ada[0], b_ada, taps_loc)
    mod = jnp.concatenate([mod_flat.reshape(6, d), jnp.zeros((2, d), F32)], axis=0)
    prm = jnp.concatenate([norm_mix_g, norm_ffn_g, final_norm_g.reshape(1, d), conv_conf_b, conf_ln_g, conf_ln_b,
                           jnp.zeros((2, d), F32)], axis=0)

    w_in_g, w_so_g, w_co_g, w_o_g, w_fi_g, w_fo_g = _gather_weights(
        [w_in[0].astype(BF16), w_short_out[0].astype(BF16), w_conf_out[0].astype(BF16), w_o[0].astype(BF16),
         w_ffn_in[0].astype(BF16), w_ffn_out[0].astype(BF16)])
    n_in = w_in_g.shape[2]
    fb = w_fi_g.shape[2]
    w_so = w_so_g.reshape(d, d)
    w_co = w_co_g.reshape(d, d)
    w_oo = w_o_g.reshape(d, d)
    w_fo4 = w_fo_g.reshape(N_DEV // 2, fb, d)

    proj, h = _forward_in(x2, mod, prm, w_in_g, tm_in)
    x1, y_a, y_b, mix, u1 = _forward_mix(x2, proj, mod, prm, taps, w_so, w_co, w_oo, tm)
    dx2, ab, h2, sums_f = _forward_ffn(x1, tgt, mod, prm, w_fi_g, w_fo4, tm)

    dx1, df, act, dab, sums_b = _backward_ffn(dx2, x1, ab, mod, prm, w_fi_g, w_fo4, tm)
    dproj, dmix, dya, dyb, merged, ya_pre, u3, sums_m, dw3p, dw31p = _backward_mix(
        dx1, proj, y_a, y_b, mix, u1, mod, prm, taps, w_so, w_co, w_oo, tm)
    grad_x, sums_i = _backward_in(dproj, x2, dx1, mod, prm, w_in_g, tm_in)

    nk = t // tk
    tok = pl.BlockSpec((tk, d), lambda s, k: (k, 0))
    g_in = _weight_grad(h, dproj, tok, pl.BlockSpec((tk, n_in), lambda s, k: (k, s)), N_DEV, d, n_in, nk, "grad_w_in")
    g_fi = _weight_grad(h2, dab, tok, pl.BlockSpec((1, tk, fb), lambda s, k: (s, k, 0)), N_DEV, d, fb, nk, "grad_w_ffn_in")
    g_fo = _weight_grad(act, df, pl.BlockSpec((1, tk, fb), lambda s, k: (s, k, 0)), tok, N_DEV // 2, fb, d, nk, "grad_w_ffn_out")
    g_so = _weight_grad(ya_pre, dya, tok, tok, 1, d, d, nk, "grad_w_short_out")
    g_co = _weight_grad(u3, dyb, tok, tok, 1, d, d, nk, "grad_w_conf_out")
    g_oo = _weight_grad(merged, dmix, tok, tok, 1, d, d, nk, "grad_w_o")
    rows = d // N_DEV
    frows = w_ffn_out.shape[1]
    parts = _scatter_grads([g_in, g_so.reshape(N_DEV, rows, d), g_co.reshape(N_DEV, rows, d),
                            g_oo.reshape(N_DEV, rows, d), g_fi, g_fo.reshape(N_DEV, frows, d)])

    up_in = _adamw_shard(parts[0], w_in[0], m_w_in[0], v_w_in[0], _pick(d, 256), "adamw_w_in")
    up_so = _adamw_shard(parts[1], w_short_out[0], m_w_short_out[0], v_w_short_out[0], rows, "adamw_w_short_out")
    up_co = _adamw_shard(parts[2], w_conf_out[0], m_w_conf_out[0], v_w_conf_out[0], rows, "adamw_w_conf_out")
    up_oo = _adamw_shard(parts[3], w_o[0], m_w_o[0], v_w_o[0], rows, "adamw_w_o")
    up_fi = _adamw_shard(parts[4], w_ffn_in[0], m_w_ffn_in[0], v_w_ffn_in[0], _pick(d, 256), "adamw_w_ffn_in")
    up_fo = _adamw_shard(parts[5], w_ffn_out[0], m_w_ffn_out[0], v_w_ffn_out[0], frows, "adamw_w_ffn_out")

    vec = jnp.concatenate([sums_i[0:2], sums_m[0:1], sums_b[0:2], sums_f[1:2],
                           sums_i[2:3], sums_m[3:4], sums_m[1:3], sums_b[2:3], sums_f[0:1]], axis=0)
    vec = vec.reshape(1, 1, 12 * d)
    dw3 = dw3p.reshape(SHORT_K, SUBLANES, d).sum(axis=1)
    dw31 = dw31p.reshape(CONF_K, SUBLANES, d).sum(axis=1)
    cg = jnp.zeros((40, d), F32).at[0:SHORT_K].set(dw3).at[8:8 + CONF_K].set(dw31)
    cg = cg.reshape(40, N_DEV, LANES).transpose(1, 0, 2)
    fin = lambda a: a.reshape(1, d)
    tap = lambda a: a.reshape(a.shape[1:])
    smalls = [
        (b_ada, m_b_ada, v_b_ada, 0, 6 * d, "vec"),
        (norm_mix_g, m_norm_mix_g, v_norm_mix_g, 6 * d, 7 * d, "vec"),
        (tap(conv_short_w), tap(m_conv_short_w), tap(v_conv_short_w), 0, SHORT_K, "cg"),
        (tap(conv_conf_w), tap(m_conv_conf_w), tap(v_conv_conf_w), 8, 8 + CONF_K, "cg"),
        (conv_conf_b, m_conv_conf_b, v_conv_conf_b, 7 * d, 8 * d, "vec"),
        (conf_ln_g, m_conf_ln_g, v_conf_ln_g, 8 * d, 9 * d, "vec"),
        (conf_ln_b, m_conf_ln_b, v_conf_ln_b, 9 * d, 10 * d, "vec"),
        (norm_ffn_g, m_norm_ffn_g, v_norm_ffn_g, 10 * d, 11 * d, "vec"),
        (fin(final_norm_g), fin(m_final_norm_g), fin(v_final_norm_g), 11 * d, 12 * d, "vec"),
    ]
    vall, up_small = _small_update(vec, cg, smalls)
    n_ada = w_ada.shape[2]
    dmod_all = vall.reshape(N_DEV, 12 * d)[:, 0:6 * d]
    dmod_cols = lax.dynamic_slice(dmod_all, (0, me * n_ada), (N_DEV, n_ada))
    up_ada = _ada_update(sc_all3.reshape(N_DEV, d), dmod_cols, w_ada[0], m_w_ada[0], v_w_ada[0], _pick(d, 256))

    loss = lax.psum(jnp.sum(sums_f[2]), ("x", "y", "c"))

    lead = lambda a: a.reshape((1,) + a.shape)
    ups = [tuple(lead(a) for a in up_ada), up_small[0], up_small[1], tuple(lead(a) for a in up_in),
           tuple(lead(a) for a in up_small[2]), tuple(lead(a) for a in up_so), tuple(lead(a) for a in up_small[3]),
           up_small[4], up_small[5], up_small[6],
           tuple(lead(a) for a in up_co), tuple(lead(a) for a in up_oo), up_small[7],
           tuple(lead(a) for a in up_fi), tuple(lead(a) for a in up_fo),
           tuple(a.reshape(d) for a in up_small[8])]
    grads = [u[0] for u in ups]
    deltas = [u[1] for u in ups]
    new_m = [u[2] for u in ups]
    new_v = [u[3] for u in ups]
    return (loss, grad_x.reshape(1, t, d), *grads, *deltas, *new_m, *new_v)
```

```python
import functools

import jax
import jax.numpy as jnp
from jax import lax
from jax.experimental import pallas as pl
from jax.experimental.pallas import tpu as pltpu

F32 = jnp.float32
BF16 = jnp.bfloat16
MESH = pl.DeviceIdType.MESH

N_DEV = 8
EPS = 1e-6
LN_EPS = 1e-5
SHORT_K = 3
CONF_K = 31
ADAM_LR = 0.001
ADAM_B1 = 0.9
ADAM_B2 = 0.999
ADAM_EPS = 1e-08
ADAM_WD = 0.01
ADAM_STEP = 10

LANES = 128
SUBLANES = 8
CONV_ROWS = 64
HALO_SHORT = 8
HALO_CONF = 32
VMEM_LIMIT = 56 * 1024 * 1024

M_SH1, M_SC1, M_G1, M_SH2, M_SC2, M_G2 = range(6)
P_GMIX, P_GFFN, P_GFIN, P_CBIAS, P_LNG, P_LNB = range(6)


def _params(sem=None, **kw):
    return pltpu.CompilerParams(dimension_semantics=sem, vmem_limit_bytes=VMEM_LIMIT, **kw)


def _sigmoid(v):
    return jax.nn.sigmoid(v)


def _dot(a, b):
    return jnp.dot(a, b, preferred_element_type=F32)


def _dot_nt(a, b):
    return lax.dot_general(a, b, (((1,), (1,)), ((), ())), preferred_element_type=F32)


def _dot_tn(a, b):
    return lax.dot_general(a, b, (((0,), (0,)), ((), ())), preferred_element_type=F32)


def _colsum(v):
    return jnp.sum(v, axis=0, keepdims=True)


def _rowmean(v):
    return jnp.mean(v, axis=-1, keepdims=True)


def _my_coords():
    return lax.axis_index("x"), lax.axis_index("y"), lax.axis_index("c")


def _slot(px, py, pc):
    return 4 * px + 2 * py + pc


def _flip(v, bit):
    return 1 - v if bit else v


def _taps_by_residue(taps):
    by_res = {}
    for wi, off in taps:
        by_res.setdefault(off % SUBLANES, []).append((wi, off // SUBLANES))
    return sorted(by_res.items())


def _tap_conv(ext_ref, w_ref, taps, tm, extra, out_ref):
    d = out_ref.shape[1]
    rb = min(CONV_ROWS, tm)
    wrows = rb + extra
    groups = _taps_by_residue(taps)

    def block(i, carry):
        base = pl.multiple_of(i * rb, SUBLANES)
        for lc in range(d // LANES):
            ls = pl.ds(lc * LANES, LANES)
            win = ext_ref[pl.ds(base, wrows), ls]
            acc = None
            for r, lst in groups:
                sh = win if r == 0 else pltpu.roll(win, wrows - r, 0)
                for wi, q in lst:
                    term = w_ref[wi:wi + 1, ls] * sh[SUBLANES * q:SUBLANES * q + rb, :]
                    acc = term if acc is None else acc + term
            out_ref[pl.ds(base, rb), ls] = acc
        return carry

    lax.fori_loop(0, tm // rb, block, 0)


def _tap_wgrad(a_ref, ext_ref, taps, tm, extra, acc_ref):
    d = a_ref.shape[1]
    rb = min(CONV_ROWS, tm)
    wrows = rb + extra
    groups = _taps_by_residue(taps)

    def block(i, carry):
        base = pl.multiple_of(i * rb, SUBLANES)
        for lc in range(d // LANES):
            ls = pl.ds(lc * LANES, LANES)
            a_blk = a_ref[pl.ds(base, rb), ls]
            win = ext_ref[pl.ds(base, wrows), ls]
            for r, lst in groups:
                sh = win if r == 0 else pltpu.roll(win, wrows - r, 0)
                for wi, q in lst:
                    prod = a_blk * sh[SUBLANES * q:SUBLANES * q + rb, :]
                    part = prod[0:SUBLANES, :]
                    for s in range(1, rb // SUBLANES):
                        part = part + prod[SUBLANES * s:SUBLANES * (s + 1), :]
                    rows = pl.ds(SUBLANES * wi, SUBLANES)
                    acc_ref[rows, ls] = acc_ref[rows, ls] + part
        return carry

    lax.fori_loop(0, tm // rb, block, 0)


def _causal_taps(k, halo):
    return [(i, halo - (k - 1) + i) for i in range(k)]


def _anticausal_taps(k):
    return [(i, (k - 1) - i) for i in range(k)]


def _ada_forward(c, w_ada_loc, b_ada, taps_loc):
    d = c.shape[1]
    nloc = w_ada_loc.shape[1]
    trows = taps_loc.shape[0]

    def body(c_ref, w_ref, b_ref, t_ref, mod_ref, sc_ref, taps_ref,
             part_ref, modrecv_ref, tapsall_ref, send_sems, recv_sems):
        x, y, cc = _my_coords()
        me = _slot(x, y, cc)
        cv = c_ref[...]
        sc_ref[me] = cv * _sigmoid(cv)
        tapsall_ref[me] = t_ref[...]

        def peer_of(k):
            return (_flip(x, k & 4), _flip(y, k & 2), _flip(cc, k & 1))

        def gather_copy(ref, base, k):
            return pltpu.make_async_remote_copy(
                src_ref=ref.at[me], dst_ref=ref.at[me], send_sem=send_sems.at[base + k - 1],
                recv_sem=recv_sems.at[base + k - 1], device_id=peer_of(k), device_id_type=MESH)

        first = [gather_copy(sc_ref, 0, k) for k in range(1, N_DEV)]
        first += [gather_copy(tapsall_ref, 7, k) for k in range(1, N_DEV)]
        for cp in first:
            cp.start()
        for cp in first[:7]:
            cp.wait_recv()
        sc_all = jnp.concatenate([sc_ref[s] for s in range(N_DEV)], axis=0)
        part = jnp.dot(sc_all, w_ref[...], preferred_element_type=F32,
                       precision=lax.Precision.HIGHEST)
        for b in range(N_DEV):
            part_ref[b] = part[b:b + 1, :]
        modrecv_ref[me] = part_ref[me]
        second = []
        for k in range(1, N_DEV):
            px, py, pc = peer_of(k)
            second.append(pltpu.make_async_remote_copy(
                src_ref=part_ref.at[_slot(px, py, pc)], dst_ref=modrecv_ref.at[me],
                send_sem=send_sems.at[14 + k - 1], recv_sem=recv_sems.at[14 + k - 1],
                device_id=(px, py, pc), device_id_type=MESH))
        for cp in second:
            cp.start()
        for cp in second:
            cp.wait_recv()
        mod = jnp.concatenate([modrecv_ref[s] for s in range(N_DEV)], axis=1)
        mod_ref[...] = mod + b_ref[...]
        for cp in first[7:]:
            cp.wait_recv()
        taps_ref[...] = jnp.concatenate([tapsall_ref[s] for s in range(N_DEV)], axis=1)
        for cp in first + second:
            cp.wait_send()

    return pl.pallas_call(
        body, name="ada_forward",
        out_shape=(jax.ShapeDtypeStruct((1, N_DEV * nloc), F32),
                   jax.ShapeDtypeStruct((N_DEV, 1, d), F32),
                   jax.ShapeDtypeStruct((trows, N_DEV * LANES), F32)),
        in_specs=[pl.BlockSpec(memory_space=pltpu.VMEM)] * 4,
        out_specs=[pl.BlockSpec(memory_space=pltpu.VMEM)] * 3,
        scratch_shapes=[pltpu.VMEM((N_DEV, 1, nloc), F32), pltpu.VMEM((N_DEV, 1, nloc), F32),
                        pltpu.VMEM((N_DEV, trows, LANES), F32),
                        pltpu.SemaphoreType.DMA((21,)), pltpu.SemaphoreType.DMA((21,))],
        compiler_params=_params(),
    )(c, w_ada_loc, b_ada, taps_loc)


ANY_SPEC = pl.BlockSpec(memory_space=pl.ANY)


def _comm_scratch(na):
    return [pltpu.SemaphoreType.DMA((7 * na,)), pltpu.SemaphoreType.DMA((7 * na,)), pltpu.SemaphoreType.DMA((na,))]


def _gather_plan(ins, outs, send_sems, recv_sems, local_sems):
    na = len(ins)
    x, y, c = _my_coords()
    me, sibling = (x, y, c), (x, y, 1 - c)
    chips = [(1 - x, y), (x, 1 - y), (1 - x, 1 - y)]

    def copy(a, k, block, to, src=None):
        dst = outs[a].at[_slot(*block)]
        return pltpu.make_async_remote_copy(
            src_ref=dst if src is None else src, dst_ref=dst,
            send_sem=send_sems.at[7 * a + k], recv_sem=recv_sems.at[7 * a + k],
            device_id=to, device_id_type=MESH)

    mine = [pltpu.make_async_copy(ins[a], outs[a].at[_slot(*me)], local_sems.at[a]) for a in range(na)]
    first = []
    for a in range(na):
        first.append(copy(a, 0, me, sibling, src=ins[a]))
        first += [copy(a, 1 + j, me, (*chip, c), src=ins[a]) for j, chip in enumerate(chips)]
    passed = [copy(a, 4 + j, (*chip, c), sibling) for a in range(na) for j, chip in enumerate(chips)]

    def start():
        for cp in mine + first:
            cp.start()

    def forward():
        for a in range(na):
            for j, chip in enumerate(chips):
                copy(a, 1 + j, (*chip, c), me).wait_recv()
                passed[3 * a + j].start()

    def finish():
        for a in range(na):
            copy(a, 0, sibling, me).wait_recv()
            for j, chip in enumerate(chips):
                copy(a, 4 + j, (*chip, 1 - c), me).wait_recv()
        for cp in first + passed:
            cp.wait_send()
        for cp in mine:
            cp.wait()

    return start, forward, finish


def _gather_weights(shards, name):
    na = len(shards)

    def body(*refs):
        start, forward, finish = _gather_plan(refs[:na], refs[na:2 * na], *refs[2 * na:])
        start()
        forward()
        finish()

    return pl.pallas_call(
        body, name=name,
        out_shape=tuple(jax.ShapeDtypeStruct((N_DEV,) + s.shape, s.dtype) for s in shards),
        in_specs=[ANY_SPEC] * na, out_specs=[ANY_SPEC] * na,
        scratch_shapes=_comm_scratch(na),
        compiler_params=_params(),
    )(*shards)


def _scatter_plan(ins, outs, send_sems, recv_sems, local_sems):
    na = len(ins)
    x, y, c = _my_coords()
    me = _slot(x, y, c)
    mine = [pltpu.make_async_copy(ins[a].at[me], outs[a].at[me], local_sems.at[a]) for a in range(na)]
    copies = []
    for k in range(1, N_DEV):
        peer = (_flip(x, k & 4), _flip(y, k & 2), _flip(c, k & 1))
        for a in range(na):
            copies.append(pltpu.make_async_remote_copy(
                src_ref=ins[a].at[_slot(*peer)], dst_ref=outs[a].at[me],
                send_sem=send_sems.at[7 * a + k - 1], recv_sem=recv_sems.at[7 * a + k - 1],
                device_id=peer, device_id_type=MESH))

    def start():
        for cp in mine + copies:
            cp.start()

    def finish():
        for cp in copies:
            cp.wait_recv()
        for cp in copies:
            cp.wait_send()
        for cp in mine:
            cp.wait()

    return start, finish


def _scatter_grads(partials, name):
    na = len(partials)

    def body(*refs):
        start, finish = _scatter_plan(refs[:na], refs[na:2 * na], *refs[2 * na:])
        start()
        finish()

    return pl.pallas_call(
        body, name=name,
        out_shape=tuple(jax.ShapeDtypeStruct(p.shape, p.dtype) for p in partials),
        in_specs=[ANY_SPEC] * na, out_specs=[ANY_SPEC] * na,
        scratch_shapes=_comm_scratch(na),
        compiler_params=_params(),
    )(*partials)


def _forward_in(x, mod, prm, w_in_g, tm, shards):
    t, d = x.shape
    ns, _, n = w_in_g.shape
    na = len(shards)
    nt = t // tm

    def body(*refs):
        x_ref, mod_ref, prm_ref, w_ref = refs[:4]
        proj_ref, h_ref = refs[4 + na:6 + na]
        hs_ref = refs[6 + 2 * na]
        i, j = pl.program_id(0), pl.program_id(1)
        start, forward, finish = _gather_plan(refs[4:4 + na], refs[6 + na:6 + 2 * na], *refs[7 + 2 * na:])

        @pl.when((i == 0) & (j == 0))
        def _():
            start()

        @pl.when((i == (3 * nt) // 4) & (j == 0))
        def _():
            forward()

        @pl.when(j == 0)
        def _():
            xv = x_ref[...]
            r = lax.rsqrt(_rowmean(xv * xv) + EPS)
            h = xv * r * prm_ref[P_GMIX:P_GMIX + 1, :] * (1.0 + mod_ref[M_SC1:M_SC1 + 1, :]) \
                + mod_ref[M_SH1:M_SH1 + 1, :]
            hb = h.astype(BF16)
            hs_ref[...] = hb
            h_ref[...] = hb

        proj_ref[...] = _dot(hs_ref[...], w_ref[0]).astype(BF16)

        @pl.when((i == nt - 1) & (j == ns - 1))
        def _():
            finish()

    res = pl.pallas_call(
        body, name="forward_in", grid=(nt, ns),
        out_shape=(jax.ShapeDtypeStruct((t, ns * n), BF16), jax.ShapeDtypeStruct((t, d), BF16))
        + tuple(jax.ShapeDtypeStruct((N_DEV,) + s.shape, s.dtype) for s in shards),
        in_specs=[pl.BlockSpec((tm, d), lambda i, j: (i, 0)),
                  pl.BlockSpec((8, d), lambda i, j: (0, 0)),
                  pl.BlockSpec((8, d), lambda i, j: (0, 0)),
                  pl.BlockSpec((1, d, n), lambda i, j: (j, 0, 0))] + [ANY_SPEC] * na,
        out_specs=[pl.BlockSpec((tm, n), lambda i, j: (i, j)),
                   pl.BlockSpec((tm, d), lambda i, j: (i, 0))] + [ANY_SPEC] * na,
        scratch_shapes=[pltpu.VMEM((tm, d), BF16)] + _comm_scratch(na),
        compiler_params=_params(("arbitrary", "arbitrary")),
    )(x, mod, prm, w_in_g, *shards)
    return res[0], res[1], res[2:]


def _forward_mix(x, proj, mod, prm, taps, w_so, w_co, w_o, tm):
    t, d = x.shape
    hs, hc = HALO_SHORT, HALO_CONF
    r3, r31 = 0, 8

    def body(x_ref, proj_ref, mod_ref, prm_ref, taps_ref, wso_ref, wco_ref, wo_ref,
             x1_ref, ya_ref, yb_ref, mix_ref, u1_ref,
             cv_ext, u0_ext, conv3_ref, u1f_ref):
        i = pl.program_id(0)

        @pl.when(i == 0)
        def _():
            cv_ext[0:hs, :] = jnp.zeros((hs, d), F32)
            u0_ext[0:hc, :] = jnp.zeros((hc, d), F32)

        def col(g):
            return proj_ref[:, g * d:(g + 1) * d].astype(F32)

        cv_ext[hs:hs + tm, :] = col(1) * col(2)
        u0_ext[hc:hc + tm, :] = col(3) * _sigmoid(col(4))
        _tap_conv(cv_ext, taps_ref, [(r3 + wi, off) for wi, off in _causal_taps(SHORT_K, hs)], tm, hs, conv3_ref)
        _tap_conv(u0_ext, taps_ref, [(r31 + wi, off) for wi, off in _causal_taps(CONF_K, hc)], tm, hc, u1f_ref)
        cv_ext[0:hs, :] = cv_ext[tm:tm + hs, :]
        u0_ext[0:hc, :] = u0_ext[tm:tm + hc, :]

        ya_pre = (col(0) * conv3_ref[...]).astype(BF16)
        y_a = _dot(ya_pre, wso_ref[...])
        u1 = u1f_ref[...] + prm_ref[P_CBIAS:P_CBIAS + 1, :]
        u1_ref[...] = u1.astype(BF16)
        u1 = u1_ref[...].astype(F32)
        mu = _rowmean(u1)
        uc = u1 - mu
        rstd = lax.rsqrt(_rowmean(uc * uc) + LN_EPS)
        u2 = uc * rstd * prm_ref[P_LNG:P_LNG + 1, :] + prm_ref[P_LNB:P_LNB + 1, :]
        u3 = (u2 * _sigmoid(u2)).astype(BF16)
        y_b = _dot(u3, wco_ref[...])
        ya_ref[...] = y_a.astype(BF16)
        yb_ref[...] = y_b.astype(BF16)
        merged = _sigmoid(col(5)) * ya_ref[...].astype(F32) + _sigmoid(col(6)) * yb_ref[...].astype(F32)
        mix = _dot(merged.astype(BF16), wo_ref[...])
        mix_ref[...] = mix.astype(BF16)
        x1_ref[...] = x_ref[...] + mod_ref[M_G1:M_G1 + 1, :] * mix

    tile = pl.BlockSpec((tm, d), lambda i: (i, 0))
    whole = lambda shape: pl.BlockSpec(shape, lambda i: (0,) * len(shape))
    return pl.pallas_call(
        body, name="forward_mix", grid=(t // tm,),
        out_shape=(jax.ShapeDtypeStruct((t, d), F32),) + (jax.ShapeDtypeStruct((t, d), BF16),) * 4,
        in_specs=[tile, pl.BlockSpec((tm, 7 * d), lambda i: (i, 0)), whole((8, d)), whole((8, d)),
                  whole(taps.shape), whole((d, d)), whole((d, d)), whole((d, d))],
        out_specs=[tile] * 5,
        scratch_shapes=[pltpu.VMEM((hs + tm, d), F32), pltpu.VMEM((hc + tm, d), F32),
                        pltpu.VMEM((tm, d), F32), pltpu.VMEM((tm, d), F32)],
        compiler_params=_params(("arbitrary",)),
    )(x, proj, mod, prm, taps, w_so, w_co, w_o)


def _forward_ffn(x1, tgt, mod, prm, w_fi_g, w_fo_g, tm):
    t, d = x1.shape
    ns, _, fb = w_fi_g.shape
    nh = ns // 2

    def body(x1_ref, tgt_ref, mod_ref, prm_ref, wfi_hbm, wfo_hbm,
             dx2_ref, ab_ref, h2_ref, sums_ref, wfi_ref, wfo_ref):
        i = pl.program_id(0)

        @pl.when(i == 0)
        def _():
            pltpu.sync_copy(wfi_hbm, wfi_ref)
            pltpu.sync_copy(wfo_hbm, wfo_ref)
            sums_ref[...] = jnp.zeros((8, d), F32)

        x1v = x1_ref[...]
        r2 = lax.rsqrt(_rowmean(x1v * x1v) + EPS)
        h2 = (x1v * r2 * prm_ref[P_GFFN:P_GFFN + 1, :] * (1.0 + mod_ref[M_SC2:M_SC2 + 1, :])
              + mod_ref[M_SH2:M_SH2 + 1, :]).astype(BF16)
        h2_ref[...] = h2
        f = jnp.zeros((tm, d), F32)
        for j in range(nh):
            ab_ref[j] = _dot(h2, wfi_ref[j]).astype(BF16)
            ab_ref[j + nh] = _dot(h2, wfi_ref[j + nh]).astype(BF16)
            a = ab_ref[j].astype(F32)
            act = (a * _sigmoid(a) * ab_ref[j + nh].astype(F32)).astype(BF16)
            f = f + _dot(act, wfo_ref[j])
        x2 = x1v + mod_ref[M_G2:M_G2 + 1, :] * f
        r3 = lax.rsqrt(_rowmean(x2 * x2) + EPS)
        xn3 = x2 * r3
        gfin = prm_ref[P_GFIN:P_GFIN + 1, :]
        err = xn3 * gfin - tgt_ref[...]
        dy = err * (1.0 / d)
        dxn3 = dy * gfin
        dx2 = r3 * (dxn3 - xn3 * _rowmean(dxn3 * xn3))
        dx2_ref[...] = dx2
        sums_ref[0:1, :] = sums_ref[0:1, :] + _colsum(dy * xn3)
        sums_ref[1:2, :] = sums_ref[1:2, :] + _colsum(dx2 * f)
        sums_ref[2:3, :] = sums_ref[2:3, :] + _colsum(err * err) * (0.5 / d)

    tile = pl.BlockSpec((tm, d), lambda i: (i, 0))
    whole = lambda shape: pl.BlockSpec(shape, lambda i: (0,) * len(shape))
    any_spec = pl.BlockSpec(memory_space=pl.ANY)
    return pl.pallas_call(
        body, name="forward_ffn", grid=(t // tm,),
        out_shape=(jax.ShapeDtypeStruct((t, d), F32), jax.ShapeDtypeStruct((ns, t, fb), BF16),
                   jax.ShapeDtypeStruct((t, d), BF16), jax.ShapeDtypeStruct((8, d), F32)),
        in_specs=[tile, tile, whole((8, d)), whole((8, d)), any_spec, any_spec],
        out_specs=[tile, pl.BlockSpec((ns, tm, fb), lambda i: (0, i, 0)), tile, whole((8, d))],
        scratch_shapes=[pltpu.VMEM(w_fi_g.shape, BF16), pltpu.VMEM(w_fo_g.shape, BF16)],
        compiler_params=_params(("arbitrary",)),
    )(x1, tgt, mod, prm, w_fi_g, w_fo_g)


def _backward_ffn(dx2, x1, ab, mod, prm, w_fi_g, w_fo_g, tm):
    t, d = x1.shape
    ns, _, fb = w_fi_g.shape
    nh = ns // 2

    def body(dx2_ref, x1_ref, ab_ref, mod_ref, prm_ref, wfi_hbm, wfo_hbm,
             dx1_ref, df_ref, act_ref, dab_ref, sums_ref, wfi_ref, wfo_ref):
        i = pl.program_id(0)

        @pl.when(i == 0)
        def _():
            pltpu.sync_copy(wfi_hbm, wfi_ref)
            pltpu.sync_copy(wfo_hbm, wfo_ref)
            sums_ref[...] = jnp.zeros((8, d), F32)

        dx2v = dx2_ref[...]
        df = (mod_ref[M_G2:M_G2 + 1, :] * dx2v).astype(BF16)
        df_ref[...] = df
        dh2 = jnp.zeros((tm, d), F32)
        for j in range(nh):
            dact = _dot_nt(df, wfo_ref[j])
            a = ab_ref[j].astype(F32)
            b = ab_ref[j + nh].astype(F32)
            s = _sigmoid(a)
            sil = a * s
            act_ref[j] = (sil * b).astype(BF16)
            da = (dact * b * (s * (1.0 + a * (1.0 - s)))).astype(BF16)
            db = (dact * sil).astype(BF16)
            dab_ref[j] = da
            dab_ref[j + nh] = db
            dh2 = dh2 + _dot_nt(da, wfi_ref[j]) + _dot_nt(db, wfi_ref[j + nh])
        x1v = x1_ref[...]
        r2 = lax.rsqrt(_rowmean(x1v * x1v) + EPS)
        xn2 = x1v * r2
        gffn = prm_ref[P_GFFN:P_GFFN + 1, :]
        scale = 1.0 + mod_ref[M_SC2:M_SC2 + 1, :]
        dxn2 = dh2 * gffn * scale
        dx1_ref[...] = dx2v + r2 * (dxn2 - xn2 * _rowmean(dxn2 * xn2))
        hx = dh2 * xn2
        sums_ref[0:1, :] = sums_ref[0:1, :] + _colsum(dh2)
        sums_ref[1:2, :] = sums_ref[1:2, :] + _colsum(hx) * gffn
        sums_ref[2:3, :] = sums_ref[2:3, :] + _colsum(hx) * scale

    tile = pl.BlockSpec((tm, d), lambda i: (i, 0))
    whole = lambda shape: pl.BlockSpec(shape, lambda i: (0,) * len(shape))
    any_spec = pl.BlockSpec(memory_space=pl.ANY)
    return pl.pallas_call(
        body, name="backward_ffn", grid=(t // tm,),
        out_shape=(jax.ShapeDtypeStruct((t, d), F32), jax.ShapeDtypeStruct((t, d), BF16),
                   jax.ShapeDtypeStruct((nh, t, fb), BF16), jax.ShapeDtypeStruct((ns, t, fb), BF16),
                   jax.ShapeDtypeStruct((8, d), F32)),
        in_specs=[tile, tile, pl.BlockSpec((ns, tm, fb), lambda i: (0, i, 0)), whole((8, d)), whole((8, d)),
                  any_spec, any_spec],
        out_specs=[tile, tile, pl.BlockSpec((nh, tm, fb), lambda i: (0, i, 0)),
                   pl.BlockSpec((ns, tm, fb), lambda i: (0, i, 0)), whole((8, d))],
        scratch_shapes=[pltpu.VMEM(w_fi_g.shape, BF16), pltpu.VMEM(w_fo_g.shape, BF16)],
        compiler_params=_params(("arbitrary",)),
    )(dx2, x1, ab, mod, prm, w_fi_g, w_fo_g)


def _backward_mix(dx1, proj, y_a, y_b, mix, u1, mod, prm, taps, w_so, w_co, w_o, tm, partials):
    t, d = dx1.shape
    nt = t // tm
    na = len(partials)
    hs, hc = HALO_SHORT, HALO_CONF
    r3, r31 = 0, 8

    def body(*refs):
        (dx1_ref, proj_ref, halo_ref, ya_ref, yb_ref, mix_ref, u1_ref, mod_ref, prm_ref, taps_ref,
         wso_ref, wco_ref, wo_ref) = refs[:13]
        (dproj_ref, dmix_ref, dya_ref, dyb_ref, merged_ref, yapre_ref, u3_ref, sums_ref, dw3_ref,
         dw31_ref) = refs[13 + na:23 + na]
        cv_ext, u0_ext, d3_ext, du1_ext, tmp_ref = refs[23 + 2 * na:28 + 2 * na]
        scatter_start, scatter_finish = _scatter_plan(refs[13:13 + na], refs[23 + na:23 + 2 * na], *refs[28 + 2 * na:])
        i = pl.program_id(0)
        first_tile = i == nt - 1

        @pl.when(i == 0)
        def _():
            scatter_start()
            sums_ref[...] = jnp.zeros((8, d), F32)
            dw3_ref[...] = jnp.zeros(dw3_ref.shape, F32)
            dw31_ref[...] = jnp.zeros(dw31_ref.shape, F32)
            d3_ext[tm:tm + hs, :] = jnp.zeros((hs, d), F32)
            du1_ext[tm:tm + hc, :] = jnp.zeros((hc, d), F32)

        def col(g):
            return proj_ref[:, g * d:(g + 1) * d].astype(F32)

        def hcol(g, rows):
            v = halo_ref[HALO_CONF - rows:HALO_CONF, g * d:(g + 1) * d].astype(F32)
            return jnp.where(first_tile, 0.0, v)

        dx1v = dx1_ref[...]
        mixv = mix_ref[...].astype(F32)
        dmix = (mod_ref[M_G1:M_G1 + 1, :] * dx1v).astype(BF16)
        dmix_ref[...] = dmix
        sums_ref[0:1, :] = sums_ref[0:1, :] + _colsum(dx1v * mixv)
        dmerged = _dot_nt(dmix, wo_ref[...])
        ga = _sigmoid(col(5))
        gb = _sigmoid(col(6))
        yav = ya_ref[...].astype(F32)
        ybv = yb_ref[...].astype(F32)
        dya = (dmerged * ga).astype(BF16)
        dyb = (dmerged * gb).astype(BF16)
        dya_ref[...] = dya
        dyb_ref[...] = dyb
        dproj_ref[:, 5 * d:6 * d] = (dmerged * yav * ga * (1.0 - ga)).astype(BF16)
        dproj_ref[:, 6 * d:7 * d] = (dmerged * ybv * gb * (1.0 - gb)).astype(BF16)
        merged_ref[...] = (ga * yav + gb * ybv).astype(BF16)

        dya_pre = _dot_nt(dya, wso_ref[...])
        c_s, v_s, b_s = col(1), col(2), col(0)
        cv_ext[0:hs, :] = hcol(1, hs) * hcol(2, hs)
        cv_ext[hs:hs + tm, :] = c_s * v_s
        _tap_conv(cv_ext, taps_ref, [(r3 + wi, off) for wi, off in _causal_taps(SHORT_K, hs)], tm, hs, tmp_ref)
        conv3 = tmp_ref[...]
        yapre_ref[...] = (b_s * conv3).astype(BF16)
        dproj_ref[:, 0:d] = (dya_pre * conv3).astype(BF16)
        d3_ext[0:tm, :] = dya_pre * b_s
        _tap_wgrad(d3_ext, cv_ext, _causal_taps(SHORT_K, hs), tm, hs, dw3_ref)
        _tap_conv(d3_ext, taps_ref, [(r3 + wi, off) for wi, off in _anticausal_taps(SHORT_K)], tm, hs, tmp_ref)
        dcv = tmp_ref[...]
        dproj_ref[:, d:2 * d] = (dcv * v_s).astype(BF16)
        dproj_ref[:, 2 * d:3 * d] = (dcv * c_s).astype(BF16)
        d3_ext[tm:tm + hs, :] = d3_ext[0:hs, :]

        du3 = _dot_nt(dyb, wco_ref[...])
        u1v = u1_ref[...].astype(F32)
        mu = _rowmean(u1v)
        uc = u1v - mu
        rstd = lax.rsqrt(_rowmean(uc * uc) + LN_EPS)
        uhat = uc * rstd
        lng = prm_ref[P_LNG:P_LNG + 1, :]
        u2 = uhat * lng + prm_ref[P_LNB:P_LNB + 1, :]
        s2 = _sigmoid(u2)
        u3_ref[...] = (u2 * s2).astype(BF16)
        du2 = du3 * (s2 * (1.0 + u2 * (1.0 - s2)))
        sums_ref[1:2, :] = sums_ref[1:2, :] + _colsum(du2 * uhat)
        sums_ref[2:3, :] = sums_ref[2:3, :] + _colsum(du2)
        duhat = du2 * lng
        du1 = rstd * (duhat - _rowmean(duhat) - uhat * _rowmean(duhat * uhat))
        sums_ref[3:4, :] = sums_ref[3:4, :] + _colsum(du1)
        du1_ext[0:tm, :] = du1
        v_c = col(3)
        sg = _sigmoid(col(4))
        u0_ext[0:hc, :] = hcol(3, hc) * _sigmoid(hcol(4, hc))
        u0_ext[hc:hc + tm, :] = v_c * sg
        _tap_wgrad(du1_ext, u0_ext, _causal_taps(CONF_K, hc), tm, hc, dw31_ref)
        _tap_conv(du1_ext, taps_ref, [(r31 + wi, off) for wi, off in _anticausal_taps(CONF_K)], tm, hc, tmp_ref)
        du0 = tmp_ref[...]
        dproj_ref[:, 3 * d:4 * d] = (du0 * sg).astype(BF16)
        dproj_ref[:, 4 * d:5 * d] = (du0 * v_c * sg * (1.0 - sg)).astype(BF16)
        du1_ext[tm:tm + hc, :] = du1_ext[0:hc, :]

        @pl.when(i == nt - 1)
        def _():
            scatter_finish()

    rev = lambda i: (nt - 1 - i, 0)
    tile = pl.BlockSpec((tm, d), rev)
    whole = lambda shape: pl.BlockSpec(shape, lambda i: (0,) * len(shape))
    hblocks = tm // HALO_CONF
    halo = pl.BlockSpec((HALO_CONF, 7 * d), lambda i: (jnp.maximum((nt - 1 - i) * hblocks - 1, 0), 0))
    bf = jax.ShapeDtypeStruct((t, d), BF16)
    res = pl.pallas_call(
        body, name="backward_mix", grid=(nt,),
        out_shape=(jax.ShapeDtypeStruct((t, 7 * d), BF16), bf, bf, bf, bf, bf, bf,
                   jax.ShapeDtypeStruct((8, d), F32),
                   jax.ShapeDtypeStruct((SUBLANES * SHORT_K, d), F32),
                   jax.ShapeDtypeStruct((SUBLANES * CONF_K, d), F32))
        + tuple(jax.ShapeDtypeStruct(p.shape, p.dtype) for p in partials),
        in_specs=[tile, pl.BlockSpec((tm, 7 * d), rev), halo, tile, tile, tile, tile,
                  whole((8, d)), whole((8, d)), whole(taps.shape), whole((d, d)), whole((d, d)), whole((d, d))]
        + [ANY_SPEC] * na,
        out_specs=[pl.BlockSpec((tm, 7 * d), rev), tile, tile, tile, tile, tile, tile,
                   whole((8, d)), whole((SUBLANES * SHORT_K, d)), whole((SUBLANES * CONF_K, d))] + [ANY_SPEC] * na,
        scratch_shapes=[pltpu.VMEM((hs + tm, d), F32), pltpu.VMEM((hc + tm, d), F32),
                        pltpu.VMEM((tm + hs, d), F32), pltpu.VMEM((tm + hc, d), F32),
                        pltpu.VMEM((tm, d), F32)] + _comm_scratch(na),
        compiler_params=_params(("arbitrary",)),
    )(dx1, proj, proj, y_a, y_b, mix, u1, mod, prm, taps, w_so, w_co, w_o, *partials)
    return res[:10], res[10:]


def _backward_in(dproj, x, dx1, mod, prm, w_in_g, tm, partials):
    t, d = x.shape
    ns, _, n = w_in_g.shape
    na = len(partials)
    nt = t // tm

    def body(*refs):
        dproj_ref, x_ref, dx1_ref, mod_ref, prm_ref, w_ref = refs[:6]
        gx_ref, sums_ref = refs[6 + na:8 + na]
        dh_ref = refs[8 + 2 * na]
        scatter_start, scatter_finish = _scatter_plan(refs[6:6 + na], refs[8 + na:8 + 2 * na], *refs[9 + 2 * na:])
        i, j = pl.program_id(0), pl.program_id(1)

        @pl.when((i == 0) & (j == 0))
        def _():
            scatter_start()
            sums_ref[...] = jnp.zeros((8, d), F32)

        part = _dot_nt(dproj_ref[...], w_ref[0])

        @pl.when(j == 0)
        def _():
            dh_ref[...] = part

        @pl.when(j > 0)
        def _():
            dh_ref[...] = dh_ref[...] + part

        @pl.when(j == ns - 1)
        def _():
            dh = dh_ref[...]
            xv = x_ref[...]
            r1 = lax.rsqrt(_rowmean(xv * xv) + EPS)
            xn = xv * r1
            gmix = prm_ref[P_GMIX:P_GMIX + 1, :]
            scale = 1.0 + mod_ref[M_SC1:M_SC1 + 1, :]
            dxn = dh * gmix * scale
            gx_ref[...] = dx1_ref[...] + r1 * (dxn - xn * _rowmean(dxn * xn))
            hx = dh * xn
            sums_ref[0:1, :] = sums_ref[0:1, :] + _colsum(dh)
            sums_ref[1:2, :] = sums_ref[1:2, :] + _colsum(hx) * gmix
            sums_ref[2:3, :] = sums_ref[2:3, :] + _colsum(hx) * scale

        @pl.when((i == nt - 1) & (j == ns - 1))
        def _():
            scatter_finish()

    tile = pl.BlockSpec((tm, d), lambda i, j: (i, 0))
    whole = pl.BlockSpec((8, d), lambda i, j: (0, 0))
    res = pl.pallas_call(
        body, name="backward_in", grid=(nt, ns),
        out_shape=(jax.ShapeDtypeStruct((t, d), F32), jax.ShapeDtypeStruct((8, d), F32))
        + tuple(jax.ShapeDtypeStruct(p.shape, p.dtype) for p in partials),
        in_specs=[pl.BlockSpec((tm, n), lambda i, j: (i, j)), tile, tile, whole, whole,
                  pl.BlockSpec((1, d, n), lambda i, j: (j, 0, 0))] + [ANY_SPEC] * na,
        out_specs=[tile, whole] + [ANY_SPEC] * na,
        scratch_shapes=[pltpu.VMEM((tm, d), F32)] + _comm_scratch(na),
        compiler_params=_params(("arbitrary", "arbitrary")),
    )(dproj, x, dx1, mod, prm, w_in_g, *partials)
    return res[0], res[1], res[2:]


def _weight_grad(a, b, a_spec, b_spec, ns, m, n, nk, name):
    def body(a_ref, b_ref, o_ref, acc_ref):
        k = pl.program_id(1)
        av = a_ref[0] if len(a_ref.shape) == 3 else a_ref[...]
        bv = b_ref[0] if len(b_ref.shape) == 3 else b_ref[...]
        part = _dot_tn(av, bv)

        @pl.when(k == 0)
        def _():
            acc_ref[...] = part

        @pl.when(k > 0)
        def _():
            acc_ref[...] = acc_ref[...] + part

        @pl.when(k == nk - 1)
        def _():
            o_ref[0] = acc_ref[...].astype(BF16)

    return pl.pallas_call(
        body, name=name, grid=(ns, nk),
        out_shape=jax.ShapeDtypeStruct((ns, m, n), BF16),
        in_specs=[a_spec, b_spec],
        out_specs=pl.BlockSpec((1, m, n), lambda s, k: (s, 0, 0)),
        scratch_shapes=[pltpu.VMEM((m, n), F32)],
        compiler_params=_params(("arbitrary", "arbitrary")),
    )(a, b)


def _adamw(w, g, m, v):
    m = ADAM_B1 * m + (1.0 - ADAM_B1) * g
    v = ADAM_B2 * v + (1.0 - ADAM_B2) * (g * g)
    m_hat = m / (1.0 - ADAM_B1 ** ADAM_STEP)
    v_hat = v / (1.0 - ADAM_B2 ** ADAM_STEP)
    delta = -ADAM_LR * (m_hat / (jnp.sqrt(v_hat) + ADAM_EPS) + ADAM_WD * w)
    return delta, m, v


def _adamw_shard(parts, w, m, v, tr, name):
    r, c = w.shape

    def body(p_ref, w_ref, m_ref, v_ref, g_ref, d_ref, nm_ref, nv_ref):
        g = p_ref[0].astype(F32)
        for s in range(1, N_DEV):
            g = g + p_ref[s].astype(F32)
        delta, nm, nv = _adamw(w_ref[...], g, m_ref[...], v_ref[...])
        g_ref[...] = g
        d_ref[...] = delta
        nm_ref[...] = nm
        nv_ref[...] = nv

    tile = pl.BlockSpec((tr, c), lambda i: (i, 0))
    return pl.pallas_call(
        body, name=name, grid=(r // tr,),
        out_shape=(jax.ShapeDtypeStruct((r, c), F32),) * 4,
        in_specs=[pl.BlockSpec((N_DEV, tr, c), lambda i: (0, i, 0)), tile, tile, tile],
        out_specs=[tile] * 4,
        compiler_params=_params(("arbitrary",)),
    )(parts, w, m, v)


def _ada_update(sc_all, dmod_cols, w, m, v, tr):
    d, n = w.shape

    def body(sc_ref, dm_ref, w_ref, m_ref, v_ref, g_ref, d_ref, nm_ref, nv_ref):
        g = lax.dot_general(sc_ref[...], dm_ref[...], (((0,), (0,)), ((), ())),
                            preferred_element_type=F32, precision=lax.Precision.HIGHEST)
        delta, nm, nv = _adamw(w_ref[...], g, m_ref[...], v_ref[...])
        g_ref[...] = g
        d_ref[...] = delta
        nm_ref[...] = nm
        nv_ref[...] = nv

    tile = pl.BlockSpec((tr, n), lambda i: (i, 0))
    return pl.pallas_call(
        body, name="ada_update", grid=(d // tr,),
        out_shape=(jax.ShapeDtypeStruct((d, n), F32),) * 4,
        in_specs=[pl.BlockSpec((N_DEV, tr), lambda i: (0, i)), pl.BlockSpec((N_DEV, n), lambda i: (0, 0)),
                  tile, tile, tile],
        out_specs=[tile] * 4,
        compiler_params=_params(("arbitrary",)),
    )(sc_all, dmod_cols, w, m, v)


def _small_update(vec, cg, smalls):
    l = vec.shape[2]
    rows = cg.shape[1]
    ns = len(smalls)

    def body(*refs):
        vec_ref, cg_ref = refs[0], refs[1]
        wmv = refs[2:2 + 3 * ns]
        vall_ref = refs[2 + 3 * ns]
        outs = refs[3 + 3 * ns:3 + 7 * ns]
        cgr_ref, send_sems, recv_sems = refs[3 + 7 * ns:]
        x, y, c = _my_coords()
        me = _slot(x, y, c)
        vall_ref[me] = vec_ref[0]
        cgr_ref[me] = cg_ref[me]
        copies = []
        for k in range(1, N_DEV):
            peer = (_flip(x, k & 4), _flip(y, k & 2), _flip(c, k & 1))
            copies.append(pltpu.make_async_remote_copy(
                src_ref=vall_ref.at[me], dst_ref=vall_ref.at[me], send_sem=send_sems.at[k - 1],
                recv_sem=recv_sems.at[k - 1], device_id=peer, device_id_type=MESH))
            copies.append(pltpu.make_async_remote_copy(
                src_ref=cg_ref.at[_slot(*peer)], dst_ref=cgr_ref.at[me], send_sem=send_sems.at[7 + k - 1],
                recv_sem=recv_sems.at[7 + k - 1], device_id=peer, device_id_type=MESH))
        for cp in copies:
            cp.start()
        for cp in copies:
            cp.wait_recv()
        for p, (_, _, _, lo, hi, kind) in enumerate(smalls):
            w_ref, m_ref, v_ref = wmv[3 * p:3 * p + 3]
            part = (lambda s: vall_ref[s, :, lo:hi]) if kind == "vec" else (lambda s: cgr_ref[s, lo:hi, :])
            g = part(0)
            for s in range(1, N_DEV):
                g = g + part(s)
            delta, nm, nv = _adamw(w_ref[...], g, m_ref[...], v_ref[...])
            for o_ref, val in zip(outs[4 * p:4 * p + 4], (g, delta, nm, nv)):
                o_ref[...] = val
        for cp in copies:
            cp.wait_send()

    vm = pl.BlockSpec(memory_space=pltpu.VMEM)
    args = [vec, cg]
    out_shape = [jax.ShapeDtypeStruct((N_DEV, 1, l), F32)]
    for w, m, v, _, _, _ in smalls:
        args += [w, m, v]
    for w, _, _, _, _, _ in smalls:
        out_shape += [jax.ShapeDtypeStruct(w.shape, F32)] * 4
    res = pl.pallas_call(
        body, name="small_update",
        out_shape=tuple(out_shape),
        in_specs=[vm] * len(args), out_specs=[vm] * len(out_shape),
        scratch_shapes=[pltpu.VMEM((N_DEV, rows, LANES), F32),
                        pltpu.SemaphoreType.DMA((14,)), pltpu.SemaphoreType.DMA((14,))],
        compiler_params=_params(),
    )(*args)
    return res[0], [res[1 + 4 * p:5 + 4 * p] for p in range(ns)]


def _pick(t, want):
    return want if t % want == 0 else t


def kernel(x, c, w_ada, b_ada, norm_mix_g, w_in, conv_short_w, w_short_out, conv_conf_w, conv_conf_b, conf_ln_g, conf_ln_b, w_conf_out, w_o, norm_ffn_g, w_ffn_in, w_ffn_out, final_norm_g, loss_target, m_w_ada, m_b_ada, m_norm_mix_g, m_w_in, m_conv_short_w, m_w_short_out, m_conv_conf_w, m_conv_conf_b, m_conf_ln_g, m_conf_ln_b, m_w_conf_out, m_w_o, m_norm_ffn_g, m_w_ffn_in, m_w_ffn_out, m_final_norm_g, v_w_ada, v_b_ada, v_norm_mix_g, v_w_in, v_conv_short_w, v_w_short_out, v_conv_conf_w, v_conv_conf_b, v_conf_ln_g, v_conf_ln_b, v_w_conf_out, v_w_o, v_norm_ffn_g, v_w_ffn_in, v_w_ffn_out, v_final_norm_g):
    t, d = x.shape[1], x.shape[2]
    x2 = x.reshape(t, d)
    tgt = loss_target.reshape(t, d)
    me = _slot(*_my_coords())
    tm = _pick(t, 256)
    tm_in = _pick(t, 512)
    tk = _pick(t, 512)

    taps_loc = jnp.zeros((40, LANES), F32)
    taps_loc = taps_loc.at[0:SHORT_K].set(conv_short_w[0]).at[8:8 + CONF_K].set(conv_conf_w[0])
    mod_flat, sc_all3, taps = _ada_forward(c, w_ada[0], b_ada, taps_loc)
    mod = jnp.concatenate([mod_flat.reshape(6, d), jnp.zeros((2, d), F32)], axis=0)
    prm = jnp.concatenate([norm_mix_g, norm_ffn_g, final_norm_g.reshape(1, d), conv_conf_b, conf_ln_g, conf_ln_b,
                           jnp.zeros((2, d), F32)], axis=0)

    (w_in_g,) = _gather_weights([w_in[0].astype(BF16)], "gather_w_in")
    n_in = w_in_g.shape[2]
    nk = t // tk
    tok = pl.BlockSpec((tk, d), lambda s, k: (k, 0))
    rows = d // N_DEV
    frows = w_ffn_out.shape[1]

    proj, h, (w_so_g, w_co_g, w_o_g, w_fi_g, w_fo_g) = _forward_in(
        x2, mod, prm, w_in_g, tm_in,
        [w_short_out[0].astype(BF16), w_conf_out[0].astype(BF16), w_o[0].astype(BF16),
         w_ffn_in[0].astype(BF16), w_ffn_out[0].astype(BF16)])
    fb = w_fi_g.shape[2]
    w_so = w_so_g.reshape(d, d)
    w_co = w_co_g.reshape(d, d)
    w_oo = w_o_g.reshape(d, d)
    w_fo4 = w_fo_g.reshape(N_DEV // 2, fb, d)
    x1, y_a, y_b, mix, u1 = _forward_mix(x2, proj, mod, prm, taps, w_so, w_co, w_oo, tm)
    dx2, ab, h2, sums_f = _forward_ffn(x1, tgt, mod, prm, w_fi_g, w_fo4, tm)

    dx1, df, act, dab, sums_b = _backward_ffn(dx2, x1, ab, mod, prm, w_fi_g, w_fo4, tm)
    g_fi = _weight_grad(h2, dab, tok, pl.BlockSpec((1, tk, fb), lambda s, k: (s, k, 0)), N_DEV, d, fb, nk, "grad_w_ffn_in")
    g_fo = _weight_grad(act, df, pl.BlockSpec((1, tk, fb), lambda s, k: (s, k, 0)), tok, N_DEV // 2, fb, d, nk, "grad_w_ffn_out")
    (dproj, dmix, dya, dyb, merged, ya_pre, u3, sums_m, dw3p, dw31p), (p_fi, p_fo) = _backward_mix(
        dx1, proj, y_a, y_b, mix, u1, mod, prm, taps, w_so, w_co, w_oo, tm,
        [g_fi, g_fo.reshape(N_DEV, frows, d)])
    g_so = _weight_grad(ya_pre, dya, tok, tok, 1, d, d, nk, "grad_w_short_out")
    g_co = _weight_grad(u3, dyb, tok, tok, 1, d, d, nk, "grad_w_conf_out")
    g_oo = _weight_grad(merged, dmix, tok, tok, 1, d, d, nk, "grad_w_o")
    grad_x, sums_i, (p_so, p_co, p_oo) = _backward_in(
        dproj, x2, dx1, mod, prm, w_in_g, tm_in,
        [g_so.reshape(N_DEV, rows, d), g_co.reshape(N_DEV, rows, d), g_oo.reshape(N_DEV, rows, d)])
    g_in = _weight_grad(h, dproj, tok, pl.BlockSpec((tk, n_in), lambda s, k: (k, s)), N_DEV, d, n_in, nk, "grad_w_in")
    (p_in,) = _scatter_grads([g_in], "scatter_w_in")

    up_in = _adamw_shard(p_in, w_in[0], m_w_in[0], v_w_in[0], _pick(d, 256), "adamw_w_in")
    up_so = _adamw_shard(p_so, w_short_out[0], m_w_short_out[0], v_w_short_out[0], rows, "adamw_w_short_out")
    up_co = _adamw_shard(p_co, w_conf_out[0], m_w_conf_out[0], v_w_conf_out[0], rows, "adamw_w_conf_out")
    up_oo = _adamw_shard(p_oo, w_o[0], m_w_o[0], v_w_o[0], rows, "adamw_w_o")
    up_fi = _adamw_shard(p_fi, w_ffn_in[0], m_w_ffn_in[0], v_w_ffn_in[0], _pick(d, 256), "adamw_w_ffn_in")
    up_fo = _adamw_shard(p_fo, w_ffn_out[0], m_w_ffn_out[0], v_w_ffn_out[0], frows, "adamw_w_ffn_out")

    vec = jnp.concatenate([sums_i[0:2], sums_m[0:1], sums_b[0:2], sums_f[1:2],
                           sums_i[2:3], sums_m[3:4], sums_m[1:3], sums_b[2:3], sums_f[0:1]], axis=0)
    vec = vec.reshape(1, 1, 12 * d)
    dw3 = dw3p.reshape(SHORT_K, SUBLANES, d).sum(axis=1)
    dw31 = dw31p.reshape(CONF_K, SUBLANES, d).sum(axis=1)
    cg = jnp.zeros((40, d), F32).at[0:SHORT_K].set(dw3).at[8:8 + CONF_K].set(dw31)
    cg = cg.reshape(40, N_DEV, LANES).transpose(1, 0, 2)
    fin = lambda a: a.reshape(1, d)
    tap = lambda a: a.reshape(a.shape[1:])
    smalls = [
        (b_ada, m_b_ada, v_b_ada, 0, 6 * d, "vec"),
        (norm_mix_g, m_norm_mix_g, v_norm_mix_g, 6 * d, 7 * d, "vec"),
        (tap(conv_short_w), tap(m_conv_short_w), tap(v_conv_short_w), 0, SHORT_K, "cg"),
        (tap(conv_conf_w), tap(m_conv_conf_w), tap(v_conv_conf_w), 8, 8 + CONF_K, "cg"),
        (conv_conf_b, m_conv_conf_b, v_conv_conf_b, 7 * d, 8 * d, "vec"),
        (conf_ln_g, m_conf_ln_g, v_conf_ln_g, 8 * d, 9 * d, "vec"),
        (conf_ln_b, m_conf_ln_b, v_conf_ln_b, 9 * d, 10 * d, "vec"),
        (norm_ffn_g, m_norm_ffn_g, v_norm_ffn_g, 10 * d, 11 * d, "vec"),
        (fin(final_norm_g), fin(m_final_norm_g), fin(v_final_norm_g), 11 * d, 12 * d, "vec"),
    ]
    vall, up_small = _small_update(vec, cg, smalls)
    n_ada = w_ada.shape[2]
    dmod_all = vall.reshape(N_DEV, 12 * d)[:, 0:6 * d]
    dmod_cols = lax.dynamic_slice(dmod_all, (0, me * n_ada), (N_DEV, n_ada))
    up_ada = _ada_update(sc_all3.reshape(N_DEV, d), dmod_cols, w_ada[0], m_w_ada[0], v_w_ada[0], _pick(d, 256))

    loss = lax.psum(jnp.sum(sums_f[2]), ("x", "y", "c"))

    lead = lambda a: a.reshape((1,) + a.shape)
    ups = [tuple(lead(a) for a in up_ada), up_small[0], up_small[1], tuple(lead(a) for a in up_in),
           tuple(lead(a) for a in up_small[2]), tuple(lead(a) for a in up_so), tuple(lead(a) for a in up_small[3]),
           up_small[4], up_small[5], up_small[6],
           tuple(lead(a) for a in up_co), tuple(lead(a) for a in up_oo), up_small[7],
           tuple(lead(a) for a in up_fi), tuple(lead(a) for a in up_fo),
           tuple(a.reshape(d) for a in up_small[8])]
    grads = [u[0] for u in ups]
    deltas = [u[1] for u in ups]
    new_m = [u[2] for u in ups]
    new_v = [u[3] for u in ups]
    return (loss, grad_x.reshape(1, t, d), *grads, *deltas, *new_m, *new_v)
```

```python
import functools

import jax
import jax.numpy as jnp
from jax import lax
from jax.experimental import pallas as pl
from jax.experimental.pallas import tpu as pltpu

F32 = jnp.float32
BF16 = jnp.bfloat16
MESH = pl.DeviceIdType.MESH

N_DEV = 8
EPS = 1e-6
LN_EPS = 1e-5
SHORT_K = 3
CONF_K = 31
ADAM_LR = 0.001
ADAM_B1 = 0.9
ADAM_B2 = 0.999
ADAM_EPS = 1e-08
ADAM_WD = 0.01
ADAM_STEP = 10

LANES = 128
SUBLANES = 8
CONV_ROWS = 64
HALO_SHORT = 8
HALO_CONF = 32
VMEM_LIMIT = 56 * 1024 * 1024

M_SH1, M_SC1, M_G1, M_SH2, M_SC2, M_G2 = range(6)
P_GMIX, P_GFFN, P_GFIN, P_CBIAS, P_LNG, P_LNB = range(6)


def _params(sem=None, **kw):
    return pltpu.CompilerParams(dimension_semantics=sem, vmem_limit_bytes=VMEM_LIMIT, **kw)


def _sigmoid(v):
    return jax.nn.sigmoid(v)


def _dot(a, b):
    return jnp.dot(a, b, preferred_element_type=F32)


def _dot_nt(a, b):
    return lax.dot_general(a, b, (((1,), (1,)), ((), ())), preferred_element_type=F32)


def _dot_tn(a, b):
    return lax.dot_general(a, b, (((0,), (0,)), ((), ())), preferred_element_type=F32)


def _colsum(v):
    return jnp.sum(v, axis=0, keepdims=True)


def _rowmean(v):
    return jnp.mean(v, axis=-1, keepdims=True)


def _my_coords():
    return lax.axis_index("x"), lax.axis_index("y"), lax.axis_index("c")


def _slot(px, py, pc):
    return 4 * px + 2 * py + pc


def _flip(v, bit):
    return 1 - v if bit else v


def _taps_by_residue(taps):
    by_res = {}
    for wi, off in taps:
        by_res.setdefault(off % SUBLANES, []).append((wi, off // SUBLANES))
    return sorted(by_res.items())


def _tap_conv(ext_ref, w_ref, taps, tm, extra, out_ref):
    d = out_ref.shape[1]
    rb = min(CONV_ROWS, tm)
    wrows = rb + extra
    groups = _taps_by_residue(taps)

    def block(i, carry):
        base = pl.multiple_of(i * rb, SUBLANES)
        for lc in range(d // LANES):
            ls = pl.ds(lc * LANES, LANES)
            win = ext_ref[pl.ds(base, wrows), ls]
            acc = None
            for r, lst in groups:
                sh = win if r == 0 else pltpu.roll(win, wrows - r, 0)
                for wi, q in lst:
                    term = w_ref[wi:wi + 1, ls] * sh[SUBLANES * q:SUBLANES * q + rb, :]
                    acc = term if acc is None else acc + term
            out_ref[pl.ds(base, rb), ls] = acc
        return carry

    lax.fori_loop(0, tm // rb, block, 0)


def _tap_wgrad(a_ref, ext_ref, taps, tm, extra, acc_ref):
    d = a_ref.shape[1]
    rb = min(CONV_ROWS, tm)
    wrows = rb + extra
    groups = _taps_by_residue(taps)

    def block(i, carry):
        base = pl.multiple_of(i * rb, SUBLANES)
        for lc in range(d // LANES):
            ls = pl.ds(lc * LANES, LANES)
            a_blk = a_ref[pl.ds(base, rb), ls]
            win = ext_ref[pl.ds(base, wrows), ls]
            for r, lst in groups:
                sh = win if r == 0 else pltpu.roll(win, wrows - r, 0)
                for wi, q in lst:
                    prod = a_blk * sh[SUBLANES * q:SUBLANES * q + rb, :]
                    part = prod[0:SUBLANES, :]
                    for s in range(1, rb // SUBLANES):
                        part = part + prod[SUBLANES * s:SUBLANES * (s + 1), :]
                    rows = pl.ds(SUBLANES * wi, SUBLANES)
                    acc_ref[rows, ls] = acc_ref[rows, ls] + part
        return carry

    lax.fori_loop(0, tm // rb, block, 0)


def _causal_taps(k, halo):
    return [(i, halo - (k - 1) + i) for i in range(k)]


def _anticausal_taps(k):
    return [(i, (k - 1) - i) for i in range(k)]


def _ada_forward(c, w_ada_loc, b_ada, taps_loc):
    d = c.shape[1]
    nloc = w_ada_loc.shape[1]
    trows = taps_loc.shape[0]

    def body(c_ref, w_ref, b_ref, t_ref, mod_ref, sc_ref, taps_ref,
             part_ref, modrecv_ref, tapsall_ref, send_sems, recv_sems):
        x, y, cc = _my_coords()
        me = _slot(x, y, cc)
        cv = c_ref[...]
        sc_ref[me] = cv * _sigmoid(cv)
        tapsall_ref[me] = t_ref[...]

        def peer_of(k):
            return (_flip(x, k & 4), _flip(y, k & 2), _flip(cc, k & 1))

        def gather_copy(ref, base, k):
            return pltpu.make_async_remote_copy(
                src_ref=ref.at[me], dst_ref=ref.at[me], send_sem=send_sems.at[base + k - 1],
                recv_sem=recv_sems.at[base + k - 1], device_id=peer_of(k), device_id_type=MESH)

        first = [gather_copy(sc_ref, 0, k) for k in range(1, N_DEV)]
        first += [gather_copy(tapsall_ref, 7, k) for k in range(1, N_DEV)]
        for cp in first:
            cp.start()
        for cp in first[:7]:
            cp.wait_recv()
        sc_all = jnp.concatenate([sc_ref[s] for s in range(N_DEV)], axis=0)
        part = jnp.dot(sc_all, w_ref[...], preferred_element_type=F32,
                       precision=lax.Precision.HIGHEST)
        for b in range(N_DEV):
            part_ref[b] = part[b:b + 1, :]
        modrecv_ref[me] = part_ref[me]
        second = []
        for k in range(1, N_DEV):
            px, py, pc = peer_of(k)
            second.append(pltpu.make_async_remote_copy(
                src_ref=part_ref.at[_slot(px, py, pc)], dst_ref=modrecv_ref.at[me],
                send_sem=send_sems.at[14 + k - 1], recv_sem=recv_sems.at[14 + k - 1],
                device_id=(px, py, pc), device_id_type=MESH))
        for cp in second:
            cp.start()
        for cp in second:
            cp.wait_recv()
        mod = jnp.concatenate([modrecv_ref[s] for s in range(N_DEV)], axis=1)
        mod_ref[...] = mod + b_ref[...]
        for cp in first[7:]:
            cp.wait_recv()
        taps_ref[...] = jnp.concatenate([tapsall_ref[s] for s in range(N_DEV)], axis=1)
        for cp in first + second:
            cp.wait_send()

    return pl.pallas_call(
        body, name="ada_forward",
        out_shape=(jax.ShapeDtypeStruct((1, N_DEV * nloc), F32),
                   jax.ShapeDtypeStruct((N_DEV, 1, d), F32),
                   jax.ShapeDtypeStruct((trows, N_DEV * LANES), F32)),
        in_specs=[pl.BlockSpec(memory_space=pltpu.VMEM)] * 4,
        out_specs=[pl.BlockSpec(memory_space=pltpu.VMEM)] * 3,
        scratch_shapes=[pltpu.VMEM((N_DEV, 1, nloc), F32), pltpu.VMEM((N_DEV, 1, nloc), F32),
                        pltpu.VMEM((N_DEV, trows, LANES), F32),
                        pltpu.SemaphoreType.DMA((21,)), pltpu.SemaphoreType.DMA((21,))],
        compiler_params=_params(),
    )(c, w_ada_loc, b_ada, taps_loc)


ANY_SPEC = pl.BlockSpec(memory_space=pl.ANY)


def _comm_scratch(na):
    return [pltpu.SemaphoreType.DMA((7 * na,)), pltpu.SemaphoreType.DMA((7 * na,)), pltpu.SemaphoreType.DMA((na,))]


class _Gather:
    def __init__(self, ins, outs, send_sems, recv_sems, local_sems):
        self.na = len(ins)
        x, y, c = _my_coords()
        self.c = c
        self.me, self.sibling = (x, y, c), (x, y, 1 - c)
        self.chips = [(1 - x, y), (x, 1 - y), (1 - x, 1 - y)]
        self.outs, self.send_sems, self.recv_sems = outs, send_sems, recv_sems
        self.mine = [pltpu.make_async_copy(ins[a], outs[a].at[_slot(*self.me)], local_sems.at[a])
                     for a in range(self.na)]
        self.first = []
        for a in range(self.na):
            self.first.append(self._copy(a, 0, self.me, self.sibling, src=ins[a]))
            self.first += [self._copy(a, 1 + j, self.me, (*chip, c), src=ins[a]) for j, chip in enumerate(self.chips)]
        self.passed = [self._copy(a, 4 + j, (*chip, c), self.sibling)
                       for a in range(self.na) for j, chip in enumerate(self.chips)]

    def _copy(self, a, k, block, to, src=None):
        dst = self.outs[a].at[_slot(*block)]
        return pltpu.make_async_remote_copy(
            src_ref=dst if src is None else src, dst_ref=dst,
            send_sem=self.send_sems.at[7 * a + k], recv_sem=self.recv_sems.at[7 * a + k],
            device_id=to, device_id_type=MESH)

    def start(self):
        for cp in self.mine + self.first:
            cp.start()

    def forward_chip(self, j):
        for a in range(self.na):
            self._copy(a, 1 + j, (*self.chips[j], self.c), self.me).wait_recv()
            self.passed[3 * a + j].start()

    def wait_sibling_own(self):
        for a in range(self.na):
            self._copy(a, 0, self.sibling, self.me).wait_recv()

    def wait_sibling_passed(self, j):
        for a in range(self.na):
            self._copy(a, 4 + j, (*self.chips[j], 1 - self.c), self.me).wait_recv()

    def finish_sends(self):
        for cp in self.first + self.passed:
            cp.wait_send()
        for cp in self.mine:
            cp.wait()

    def forward(self):
        for j in range(3):
            self.forward_chip(j)

    def finish(self):
        self.wait_sibling_own()
        for j in range(3):
            self.wait_sibling_passed(j)
        self.finish_sends()


def _scatter_plan(ins, outs, send_sems, recv_sems, local_sems):
    na = len(ins)
    x, y, c = _my_coords()
    me = _slot(x, y, c)
    mine = [pltpu.make_async_copy(ins[a].at[me], outs[a].at[me], local_sems.at[a]) for a in range(na)]
    copies = []
    for k in range(1, N_DEV):
        peer = (_flip(x, k & 4), _flip(y, k & 2), _flip(c, k & 1))
        for a in range(na):
            copies.append(pltpu.make_async_remote_copy(
                src_ref=ins[a].at[_slot(*peer)], dst_ref=outs[a].at[me],
                send_sem=send_sems.at[7 * a + k - 1], recv_sem=recv_sems.at[7 * a + k - 1],
                device_id=peer, device_id_type=MESH))

    def start():
        for cp in mine + copies:
            cp.start()

    def finish():
        for cp in copies:
            cp.wait_recv()
        for cp in copies:
            cp.wait_send()
        for cp in mine:
            cp.wait()

    return start, finish


def _scatter_grads(partials, name):
    na = len(partials)

    def body(*refs):
        start, finish = _scatter_plan(refs[:na], refs[na:2 * na], *refs[2 * na:])
        start()
        finish()

    return pl.pallas_call(
        body, name=name,
        out_shape=tuple(jax.ShapeDtypeStruct(p.shape, p.dtype) for p in partials),
        in_specs=[ANY_SPEC] * na, out_specs=[ANY_SPEC] * na,
        scratch_shapes=_comm_scratch(na),
        compiler_params=_params(),
    )(*partials)


def _forward_in(me_arr, x, mod, prm, w_in_loc, tm, shards):
    t, d = x.shape
    n = w_in_loc.shape[1]
    ns = N_DEV
    na = len(shards)
    nt = t // tm

    def body(*refs):
        x_ref, mod_ref, prm_ref, wloc_ref = refs[1:5]
        proj_ref, h_ref, wg_ref = refs[5 + na:8 + na]
        hall_ref, wv_ref, wv_sem = refs[8 + 2 * na:11 + 2 * na]
        sems = refs[11 + 2 * na:]
        s, i = pl.program_id(0), pl.program_id(1)
        g_in = _Gather([wloc_ref], [wg_ref], *sems[0:3])
        g_rest = _Gather(refs[5:5 + na], refs[8 + na:8 + 2 * na], *sems[3:6])
        me = _slot(*_my_coords())

        def load_shard(src):
            cp = pltpu.make_async_copy(src, wv_ref, wv_sem)
            cp.start()
            cp.wait()

        @pl.when((s == 0) & (i == 0))
        def _():
            g_in.start()
            g_rest.start()
            load_shard(wloc_ref)

        arrivals = {1: g_in.wait_sibling_own,
                    2: functools.partial(g_in.forward_chip, 1), 3: functools.partial(g_in.wait_sibling_passed, 1),
                    4: functools.partial(g_in.forward_chip, 0), 5: functools.partial(g_in.wait_sibling_passed, 0),
                    6: functools.partial(g_in.forward_chip, 2), 7: functools.partial(g_in.wait_sibling_passed, 2)}
        for step, arrive in arrivals.items():
            @pl.when((s == step) & (i == 0))
            def _(step=step, arrive=arrive):
                arrive()
                if step == 6:
                    g_rest.forward()
                load_shard(wg_ref.at[me ^ step])

        @pl.when(s == 0)
        def _():
            xv = x_ref[...]
            r = lax.rsqrt(_rowmean(xv * xv) + EPS)
            h = xv * r * prm_ref[P_GMIX:P_GMIX + 1, :] * (1.0 + mod_ref[M_SC1:M_SC1 + 1, :]) \
                + mod_ref[M_SH1:M_SH1 + 1, :]
            hb = h.astype(BF16)
            hall_ref[i] = hb
            h_ref[...] = hb

        proj_ref[...] = _dot(hall_ref[i], wv_ref[...]).astype(BF16)

        @pl.when((s == ns - 1) & (i == nt - 1))
        def _():
            g_in.finish_sends()
            g_rest.finish()

    x_tile = pl.BlockSpec((tm, d), lambda s, i, me: (jnp.where(s == 0, i, nt - 1), 0))
    whole = pl.BlockSpec((8, d), lambda s, i, me: (0, 0))
    res = pl.pallas_call(
        body, name="forward_in",
        grid_spec=pltpu.PrefetchScalarGridSpec(
            num_scalar_prefetch=1, grid=(ns, nt),
            in_specs=[x_tile, whole, whole, ANY_SPEC] + [ANY_SPEC] * na,
            out_specs=[pl.BlockSpec((tm, n), lambda s, i, me: (i, me[0] ^ s)), x_tile, ANY_SPEC] + [ANY_SPEC] * na,
            scratch_shapes=[pltpu.VMEM((nt, tm, d), BF16), pltpu.VMEM((d, n), BF16), pltpu.SemaphoreType.DMA]
            + _comm_scratch(1) + _comm_scratch(na)),
        out_shape=(jax.ShapeDtypeStruct((t, ns * n), BF16), jax.ShapeDtypeStruct((t, d), BF16),
                   jax.ShapeDtypeStruct((N_DEV, d, n), BF16))
        + tuple(jax.ShapeDtypeStruct((N_DEV,) + a.shape, a.dtype) for a in shards),
        compiler_params=_params(("arbitrary", "arbitrary")),
    )(me_arr, x, mod, prm, w_in_loc, *shards)
    return res[0], res[1], res[2], res[3:]


def _forward_mix(x, proj, mod, prm, taps, w_so, w_co, w_o, tm):
    t, d = x.shape
    hs, hc = HALO_SHORT, HALO_CONF
    r3, r31 = 0, 8

    def body(x_ref, proj_ref, mod_ref, prm_ref, taps_ref, wso_ref, wco_ref, wo_ref,
             x1_ref, ya_ref, yb_ref, mix_ref, u1_ref,
             cv_ext, u0_ext, conv3_ref, u1f_ref):
        i = pl.program_id(0)

        @pl.when(i == 0)
        def _():
            cv_ext[0:hs, :] = jnp.zeros((hs, d), F32)
            u0_ext[0:hc, :] = jnp.zeros((hc, d), F32)

        def col(g):
            return proj_ref[:, g * d:(g + 1) * d].astype(F32)

        cv_ext[hs:hs + tm, :] = col(1) * col(2)
        u0_ext[hc:hc + tm, :] = col(3) * _sigmoid(col(4))
        _tap_conv(cv_ext, taps_ref, [(r3 + wi, off) for wi, off in _causal_taps(SHORT_K, hs)], tm, hs, conv3_ref)
        _tap_conv(u0_ext, taps_ref, [(r31 + wi, off) for wi, off in _causal_taps(CONF_K, hc)], tm, hc, u1f_ref)
        cv_ext[0:hs, :] = cv_ext[tm:tm + hs, :]
        u0_ext[0:hc, :] = u0_ext[tm:tm + hc, :]

        ya_pre = (col(0) * conv3_ref[...]).astype(BF16)
        y_a = _dot(ya_pre, wso_ref[...])
        u1 = u1f_ref[...] + prm_ref[P_CBIAS:P_CBIAS + 1, :]
        u1_ref[...] = u1.astype(BF16)
        u1 = u1_ref[...].astype(F32)
        mu = _rowmean(u1)
        uc = u1 - mu
        rstd = lax.rsqrt(_rowmean(uc * uc) + LN_EPS)
        u2 = uc * rstd * prm_ref[P_LNG:P_LNG + 1, :] + prm_ref[P_LNB:P_LNB + 1, :]
        u3 = (u2 * _sigmoid(u2)).astype(BF16)
        y_b = _dot(u3, wco_ref[...])
        ya_ref[...] = y_a.astype(BF16)
        yb_ref[...] = y_b.astype(BF16)
        merged = _sigmoid(col(5)) * ya_ref[...].astype(F32) + _sigmoid(col(6)) * yb_ref[...].astype(F32)
        mix = _dot(merged.astype(BF16), wo_ref[...])
        mix_ref[...] = mix.astype(BF16)
        x1_ref[...] = x_ref[...] + mod_ref[M_G1:M_G1 + 1, :] * mix

    tile = pl.BlockSpec((tm, d), lambda i: (i, 0))
    whole = lambda shape: pl.BlockSpec(shape, lambda i: (0,) * len(shape))
    return pl.pallas_call(
        body, name="forward_mix", grid=(t // tm,),
        out_shape=(jax.ShapeDtypeStruct((t, d), F32),) + (jax.ShapeDtypeStruct((t, d), BF16),) * 4,
        in_specs=[tile, pl.BlockSpec((tm, 7 * d), lambda i: (i, 0)), whole((8, d)), whole((8, d)),
                  whole(taps.shape), whole((d, d)), whole((d, d)), whole((d, d))],
        out_specs=[tile] * 5,
        scratch_shapes=[pltpu.VMEM((hs + tm, d), F32), pltpu.VMEM((hc + tm, d), F32),
                        pltpu.VMEM((tm, d), F32), pltpu.VMEM((tm, d), F32)],
        compiler_params=_params(("arbitrary",)),
    )(x, proj, mod, prm, taps, w_so, w_co, w_o)


def _forward_ffn(x1, tgt, mod, prm, w_fi_g, w_fo_g, tm):
    t, d = x1.shape
    ns, _, fb = w_fi_g.shape
    nh = ns // 2

    def body(x1_ref, tgt_ref, mod_ref, prm_ref, wfi_hbm, wfo_hbm,
             dx2_ref, ab_ref, h2_ref, sums_ref, wfi_ref, wfo_ref):
        i = pl.program_id(0)

        @pl.when(i == 0)
        def _():
            pltpu.sync_copy(wfi_hbm, wfi_ref)
            pltpu.sync_copy(wfo_hbm, wfo_ref)
            sums_ref[...] = jnp.zeros((8, d), F32)

        x1v = x1_ref[...]
        r2 = lax.rsqrt(_rowmean(x1v * x1v) + EPS)
        h2 = (x1v * r2 * prm_ref[P_GFFN:P_GFFN + 1, :] * (1.0 + mod_ref[M_SC2:M_SC2 + 1, :])
              + mod_ref[M_SH2:M_SH2 + 1, :]).astype(BF16)
        h2_ref[...] = h2
        f = jnp.zeros((tm, d), F32)
        for j in range(nh):
            ab_ref[j] = _dot(h2, wfi_ref[j]).astype(BF16)
            ab_ref[j + nh] = _dot(h2, wfi_ref[j + nh]).astype(BF16)
            a = ab_ref[j].astype(F32)
            act = (a * _sigmoid(a) * ab_ref[j + nh].astype(F32)).astype(BF16)
            f = f + _dot(act, wfo_ref[j])
        x2 = x1v + mod_ref[M_G2:M_G2 + 1, :] * f
        r3 = lax.rsqrt(_rowmean(x2 * x2) + EPS)
        xn3 = x2 * r3
        gfin = prm_ref[P_GFIN:P_GFIN + 1, :]
        err = xn3 * gfin - tgt_ref[...]
        dy = err * (1.0 / d)
        dxn3 = dy * gfin
        dx2 = r3 * (dxn3 - xn3 * _rowmean(dxn3 * xn3))
        dx2_ref[...] = dx2
        sums_ref[0:1, :] = sums_ref[0:1, :] + _colsum(dy * xn3)
        sums_ref[1:2, :] = sums_ref[1:2, :] + _colsum(dx2 * f)
        sums_ref[2:3, :] = sums_ref[2:3, :] + _colsum(err * err) * (0.5 / d)

    tile = pl.BlockSpec((tm, d), lambda i: (i, 0))
    whole = lambda shape: pl.BlockSpec(shape, lambda i: (0,) * len(shape))
    any_spec = pl.BlockSpec(memory_space=pl.ANY)
    return pl.pallas_call(
        body, name="forward_ffn", grid=(t // tm,),
        out_shape=(jax.ShapeDtypeStruct((t, d), F32), jax.ShapeDtypeStruct((ns, t, fb), BF16),
                   jax.ShapeDtypeStruct((t, d), BF16), jax.ShapeDtypeStruct((8, d), F32)),
        in_specs=[tile, tile, whole((8, d)), whole((8, d)), any_spec, any_spec],
        out_specs=[tile, pl.BlockSpec((ns, tm, fb), lambda i: (0, i, 0)), tile, whole((8, d))],
        scratch_shapes=[pltpu.VMEM(w_fi_g.shape, BF16), pltpu.VMEM(w_fo_g.shape, BF16)],
        compiler_params=_params(("arbitrary",)),
    )(x1, tgt, mod, prm, w_fi_g, w_fo_g)


def _backward_ffn(dx2, x1, ab, mod, prm, w_fi_g, w_fo_g, tm):
    t, d = x1.shape
    ns, _, fb = w_fi_g.shape
    nh = ns // 2

    def body(dx2_ref, x1_ref, ab_ref, mod_ref, prm_ref, wfi_hbm, wfo_hbm,
             dx1_ref, df_ref, act_ref, dab_ref, sums_ref, wfi_ref, wfo_ref):
        i = pl.program_id(0)

        @pl.when(i == 0)
        def _():
            pltpu.sync_copy(wfi_hbm, wfi_ref)
            pltpu.sync_copy(wfo_hbm, wfo_ref)
            sums_ref[...] = jnp.zeros((8, d), F32)

        dx2v = dx2_ref[...]
        df = (mod_ref[M_G2:M_G2 + 1, :] * dx2v).astype(BF16)
        df_ref[...] = df
        dh2 = jnp.zeros((tm, d), F32)
        for j in range(nh):
            dact = _dot_nt(df, wfo_ref[j])
            a = ab_ref[j].astype(F32)
            b = ab_ref[j + nh].astype(F32)
            s = _sigmoid(a)
            sil = a * s
            act_ref[j] = (sil * b).astype(BF16)
            da = (dact * b * (s * (1.0 + a * (1.0 - s)))).astype(BF16)
            db = (dact * sil).astype(BF16)
            dab_ref[j] = da
            dab_ref[j + nh] = db
            dh2 = dh2 + _dot_nt(da, wfi_ref[j]) + _dot_nt(db, wfi_ref[j + nh])
        x1v = x1_ref[...]
        r2 = lax.rsqrt(_rowmean(x1v * x1v) + EPS)
        xn2 = x1v * r2
        gffn = prm_ref[P_GFFN:P_GFFN + 1, :]
        scale = 1.0 + mod_ref[M_SC2:M_SC2 + 1, :]
        dxn2 = dh2 * gffn * scale
        dx1_ref[...] = dx2v + r2 * (dxn2 - xn2 * _rowmean(dxn2 * xn2))
        hx = dh2 * xn2
        sums_ref[0:1, :] = sums_ref[0:1, :] + _colsum(dh2)
        sums_ref[1:2, :] = sums_ref[1:2, :] + _colsum(hx) * gffn
        sums_ref[2:3, :] = sums_ref[2:3, :] + _colsum(hx) * scale

    tile = pl.BlockSpec((tm, d), lambda i: (i, 0))
    whole = lambda shape: pl.BlockSpec(shape, lambda i: (0,) * len(shape))
    any_spec = pl.BlockSpec(memory_space=pl.ANY)
    return pl.pallas_call(
        body, name="backward_ffn", grid=(t // tm,),
        out_shape=(jax.ShapeDtypeStruct((t, d), F32), jax.ShapeDtypeStruct((t, d), BF16),
                   jax.ShapeDtypeStruct((nh, t, fb), BF16), jax.ShapeDtypeStruct((ns, t, fb), BF16),
                   jax.ShapeDtypeStruct((8, d), F32)),
        in_specs=[tile, tile, pl.BlockSpec((ns, tm, fb), lambda i: (0, i, 0)), whole((8, d)), whole((8, d)),
                  any_spec, any_spec],
        out_specs=[tile, tile, pl.BlockSpec((nh, tm, fb), lambda i: (0, i, 0)),
                   pl.BlockSpec((ns, tm, fb), lambda i: (0, i, 0)), whole((8, d))],
        scratch_shapes=[pltpu.VMEM(w_fi_g.shape, BF16), pltpu.VMEM(w_fo_g.shape, BF16)],
        compiler_params=_params(("arbitrary",)),
    )(dx2, x1, ab, mod, prm, w_fi_g, w_fo_g)


def _backward_mix(dx1, proj, y_a, y_b, mix, u1, mod, prm, taps, w_so, w_co, w_o, tm, partials):
    t, d = dx1.shape
    nt = t // tm
    na = len(partials)
    hs, hc = HALO_SHORT, HALO_CONF
    r3, r31 = 0, 8

    def body(*refs):
        (dx1_ref, proj_ref, halo_ref, ya_ref, yb_ref, mix_ref, u1_ref, mod_ref, prm_ref, taps_ref,
         wso_ref, wco_ref, wo_ref) = refs[:13]
        (dproj_ref, dmix_ref, dya_ref, dyb_ref, merged_ref, yapre_ref, u3_ref, sums_ref, dw3_ref,
         dw31_ref) = refs[13 + na:23 + na]
        cv_ext, u0_ext, d3_ext, du1_ext, tmp_ref = refs[23 + 2 * na:28 + 2 * na]
        scatter_start, scatter_finish = _scatter_plan(refs[13:13 + na], refs[23 + na:23 + 2 * na], *refs[28 + 2 * na:])
        i = pl.program_id(0)
        first_tile = i == nt - 1

        @pl.when(i == 0)
        def _():
            scatter_start()
            sums_ref[...] = jnp.zeros((8, d), F32)
            dw3_ref[...] = jnp.zeros(dw3_ref.shape, F32)
            dw31_ref[...] = jnp.zeros(dw31_ref.shape, F32)
            d3_ext[tm:tm + hs, :] = jnp.zeros((hs, d), F32)
            du1_ext[tm:tm + hc, :] = jnp.zeros((hc, d), F32)

        def col(g):
            return proj_ref[:, g * d:(g + 1) * d].astype(F32)

        def hcol(g, rows):
            v = halo_ref[HALO_CONF - rows:HALO_CONF, g * d:(g + 1) * d].astype(F32)
            return jnp.where(first_tile, 0.0, v)

        dx1v = dx1_ref[...]
        mixv = mix_ref[...].astype(F32)
        dmix = (mod_ref[M_G1:M_G1 + 1, :] * dx1v).astype(BF16)
        dmix_ref[...] = dmix
        sums_ref[0:1, :] = sums_ref[0:1, :] + _colsum(dx1v * mixv)
        dmerged = _dot_nt(dmix, wo_ref[...])
        ga = _sigmoid(col(5))
        gb = _sigmoid(col(6))
        yav = ya_ref[...].astype(F32)
        ybv = yb_ref[...].astype(F32)
        dya = (dmerged * ga).astype(BF16)
        dyb = (dmerged * gb).astype(BF16)
        dya_ref[...] = dya
        dyb_ref[...] = dyb
        dproj_ref[:, 5 * d:6 * d] = (dmerged * yav * ga * (1.0 - ga)).astype(BF16)
        dproj_ref[:, 6 * d:7 * d] = (dmerged * ybv * gb * (1.0 - gb)).astype(BF16)
        merged_ref[...] = (ga * yav + gb * ybv).astype(BF16)

        dya_pre = _dot_nt(dya, wso_ref[...])
        c_s, v_s, b_s = col(1), col(2), col(0)
        cv_ext[0:hs, :] = hcol(1, hs) * hcol(2, hs)
        cv_ext[hs:hs + tm, :] = c_s * v_s
        _tap_conv(cv_ext, taps_ref, [(r3 + wi, off) for wi, off in _causal_taps(SHORT_K, hs)], tm, hs, tmp_ref)
        conv3 = tmp_ref[...]
        yapre_ref[...] = (b_s * conv3).astype(BF16)
        dproj_ref[:, 0:d] = (dya_pre * conv3).astype(BF16)
        d3_ext[0:tm, :] = dya_pre * b_s
        _tap_wgrad(d3_ext, cv_ext, _causal_taps(SHORT_K, hs), tm, hs, dw3_ref)
        _tap_conv(d3_ext, taps_ref, [(r3 + wi, off) for wi, off in _anticausal_taps(SHORT_K)], tm, hs, tmp_ref)
        dcv = tmp_ref[...]
        dproj_ref[:, d:2 * d] = (dcv * v_s).astype(BF16)
        dproj_ref[:, 2 * d:3 * d] = (dcv * c_s).astype(BF16)
        d3_ext[tm:tm + hs, :] = d3_ext[0:hs, :]

        du3 = _dot_nt(dyb, wco_ref[...])
        u1v = u1_ref[...].astype(F32)
        mu = _rowmean(u1v)
        uc = u1v - mu
        rstd = lax.rsqrt(_rowmean(uc * uc) + LN_EPS)
        uhat = uc * rstd
        lng = prm_ref[P_LNG:P_LNG + 1, :]
        u2 = uhat * lng + prm_ref[P_LNB:P_LNB + 1, :]
        s2 = _sigmoid(u2)
        u3_ref[...] = (u2 * s2).astype(BF16)
        du2 = du3 * (s2 * (1.0 + u2 * (1.0 - s2)))
        sums_ref[1:2, :] = sums_ref[1:2, :] + _colsum(du2 * uhat)
        sums_ref[2:3, :] = sums_ref[2:3, :] + _colsum(du2)
        duhat = du2 * lng
        du1 = rstd * (duhat - _rowmean(duhat) - uhat * _rowmean(duhat * uhat))
        sums_ref[3:4, :] = sums_ref[3:4, :] + _colsum(du1)
        du1_ext[0:tm, :] = du1
        v_c = col(3)
        sg = _sigmoid(col(4))
        u0_ext[0:hc, :] = hcol(3, hc) * _sigmoid(hcol(4, hc))
        u0_ext[hc:hc + tm, :] = v_c * sg
        _tap_wgrad(du1_ext, u0_ext, _causal_taps(CONF_K, hc), tm, hc, dw31_ref)
        _tap_conv(du1_ext, taps_ref, [(r31 + wi, off) for wi, off in _anticausal_taps(CONF_K)], tm, hc, tmp_ref)
        du0 = tmp_ref[...]
        dproj_ref[:, 3 * d:4 * d] = (du0 * sg).astype(BF16)
        dproj_ref[:, 4 * d:5 * d] = (du0 * v_c * sg * (1.0 - sg)).astype(BF16)
        du1_ext[tm:tm + hc, :] = du1_ext[0:hc, :]

        @pl.when(i == nt - 1)
        def _():
            scatter_finish()

    rev = lambda i: (nt - 1 - i, 0)
    tile = pl.BlockSpec((tm, d), rev)
    whole = lambda shape: pl.BlockSpec(shape, lambda i: (0,) * len(shape))
    hblocks = tm // HALO_CONF
    halo = pl.BlockSpec((HALO_CONF, 7 * d), lambda i: (jnp.maximum((nt - 1 - i) * hblocks - 1, 0), 0))
    bf = jax.ShapeDtypeStruct((t, d), BF16)
    res = pl.pallas_call(
        body, name="backward_mix", grid=(nt,),
        out_shape=(jax.ShapeDtypeStruct((t, 7 * d), BF16), bf, bf, bf, bf, bf, bf,
                   jax.ShapeDtypeStruct((8, d), F32),
                   jax.ShapeDtypeStruct((SUBLANES * SHORT_K, d), F32),
                   jax.ShapeDtypeStruct((SUBLANES * CONF_K, d), F32))
        + tuple(jax.ShapeDtypeStruct(p.shape, p.dtype) for p in partials),
        in_specs=[tile, pl.BlockSpec((tm, 7 * d), rev), halo, tile, tile, tile, tile,
                  whole((8, d)), whole((8, d)), whole(taps.shape), whole((d, d)), whole((d, d)), whole((d, d))]
        + [ANY_SPEC] * na,
        out_specs=[pl.BlockSpec((tm, 7 * d), rev), tile, tile, tile, tile, tile, tile,
                   whole((8, d)), whole((SUBLANES * SHORT_K, d)), whole((SUBLANES * CONF_K, d))] + [ANY_SPEC] * na,
        scratch_shapes=[pltpu.VMEM((hs + tm, d), F32), pltpu.VMEM((hc + tm, d), F32),
                        pltpu.VMEM((tm + hs, d), F32), pltpu.VMEM((tm + hc, d), F32),
                        pltpu.VMEM((tm, d), F32)] + _comm_scratch(na),
        compiler_params=_params(("arbitrary",)),
    )(dx1, proj, proj, y_a, y_b, mix, u1, mod, prm, taps, w_so, w_co, w_o, *partials)
    return res[:10], res[10:]


def _backward_in(dproj, x, dx1, mod, prm, w_in_g, tm):
    t, d = x.shape
    ns, _, n = w_in_g.shape

    def body(dproj_ref, x_ref, dx1_ref, mod_ref, prm_ref, w_hbm, gx_ref, sums_ref, w_ref):
        @pl.when(pl.program_id(0) == 0)
        def _():
            pltpu.sync_copy(w_hbm, w_ref)
            sums_ref[...] = jnp.zeros((8, d), F32)

        dh = _dot_nt(dproj_ref[:, 0:n], w_ref[0])
        for j in range(1, ns):
            dh = dh + _dot_nt(dproj_ref[:, j * n:(j + 1) * n], w_ref[j])
        xv = x_ref[...]
        r1 = lax.rsqrt(_rowmean(xv * xv) + EPS)
        xn = xv * r1
        gmix = prm_ref[P_GMIX:P_GMIX + 1, :]
        scale = 1.0 + mod_ref[M_SC1:M_SC1 + 1, :]
        dxn = dh * gmix * scale
        gx_ref[...] = dx1_ref[...] + r1 * (dxn - xn * _rowmean(dxn * xn))
        hx = dh * xn
        sums_ref[0:1, :] = sums_ref[0:1, :] + _colsum(dh)
        sums_ref[1:2, :] = sums_ref[1:2, :] + _colsum(hx) * gmix
        sums_ref[2:3, :] = sums_ref[2:3, :] + _colsum(hx) * scale

    tile = pl.BlockSpec((tm, d), lambda i: (i, 0))
    whole = pl.BlockSpec((8, d), lambda i: (0, 0))
    return pl.pallas_call(
        body, name="backward_in", grid=(t // tm,),
        out_shape=(jax.ShapeDtypeStruct((t, d), F32), jax.ShapeDtypeStruct((8, d), F32)),
        in_specs=[pl.BlockSpec((tm, ns * n), lambda i: (i, 0)), tile, tile, whole, whole, ANY_SPEC],
        out_specs=[tile, whole],
        scratch_shapes=[pltpu.VMEM(w_in_g.shape, BF16)],
        compiler_params=_params(("arbitrary",)),
    )(dproj, x, dx1, mod, prm, w_in_g)


def _weight_grad(a, b, a_spec, b_spec, ns, m, n, nk, name, partials=()):
    na = len(partials)

    def body(*refs):
        a_ref, b_ref = refs[:2]
        o_ref = refs[2 + na]
        acc_ref = refs[3 + 2 * na]
        s, k = pl.program_id(0), pl.program_id(1)
        if na:
            scatter_start, scatter_finish = _scatter_plan(refs[2:2 + na], refs[3 + na:3 + 2 * na], *refs[4 + 2 * na:])

            @pl.when((s == 0) & (k == 0))
            def _():
                scatter_start()

        av = a_ref[0] if len(a_ref.shape) == 3 else a_ref[...]
        bv = b_ref[0] if len(b_ref.shape) == 3 else b_ref[...]
        part = _dot_tn(av, bv)

        @pl.when(k == 0)
        def _():
            acc_ref[...] = part

        @pl.when(k > 0)
        def _():
            acc_ref[...] = acc_ref[...] + part

        @pl.when(k == nk - 1)
        def _():
            o_ref[0] = acc_ref[...].astype(BF16)

        if na:
            @pl.when((s == ns - 1) & (k == nk - 1))
            def _():
                scatter_finish()

    res = pl.pallas_call(
        body, name=name, grid=(ns, nk),
        out_shape=(jax.ShapeDtypeStruct((ns, m, n), BF16),)
        + tuple(jax.ShapeDtypeStruct(p.shape, p.dtype) for p in partials),
        in_specs=[a_spec, b_spec] + [ANY_SPEC] * na,
        out_specs=[pl.BlockSpec((1, m, n), lambda s, k: (s, 0, 0))] + [ANY_SPEC] * na,
        scratch_shapes=[pltpu.VMEM((m, n), F32)] + (_comm_scratch(na) if na else []),
        compiler_params=_params(("arbitrary", "arbitrary")),
    )(a, b, *partials)
    return (res[0], res[1:]) if na else res[0]


def _adamw(w, g, m, v):
    m = ADAM_B1 * m + (1.0 - ADAM_B1) * g
    v = ADAM_B2 * v + (1.0 - ADAM_B2) * (g * g)
    m_hat = m / (1.0 - ADAM_B1 ** ADAM_STEP)
    v_hat = v / (1.0 - ADAM_B2 ** ADAM_STEP)
    delta = -ADAM_LR * (m_hat / (jnp.sqrt(v_hat) + ADAM_EPS) + ADAM_WD * w)
    return delta, m, v


def _adamw_shard(parts, w, m, v, tr, name):
    r, c = w.shape

    def body(p_ref, w_ref, m_ref, v_ref, g_ref, d_ref, nm_ref, nv_ref):
        g = p_ref[0].astype(F32)
        for s in range(1, N_DEV):
            g = g + p_ref[s].astype(F32)
        delta, nm, nv = _adamw(w_ref[...], g, m_ref[...], v_ref[...])
        g_ref[...] = g
        d_ref[...] = delta
        nm_ref[...] = nm
        nv_ref[...] = nv

    tile = pl.BlockSpec((tr, c), lambda i: (i, 0))
    return pl.pallas_call(
        body, name=name, grid=(r // tr,),
        out_shape=(jax.ShapeDtypeStruct((r, c), F32),) * 4,
        in_specs=[pl.BlockSpec((N_DEV, tr, c), lambda i: (0, i, 0)), tile, tile, tile],
        out_specs=[tile] * 4,
        compiler_params=_params(("arbitrary",)),
    )(parts, w, m, v)


def _ada_update(sc_all, dmod_cols, w, m, v, tr):
    d, n = w.shape

    def body(sc_ref, dm_ref, w_ref, m_ref, v_ref, g_ref, d_ref, nm_ref, nv_ref):
        g = lax.dot_general(sc_ref[...], dm_ref[...], (((0,), (0,)), ((), ())),
                            preferred_element_type=F32, precision=lax.Precision.HIGHEST)
        delta, nm, nv = _adamw(w_ref[...], g, m_ref[...], v_ref[...])
        g_ref[...] = g
        d_ref[...] = delta
        nm_ref[...] = nm
        nv_ref[...] = nv

    tile = pl.BlockSpec((tr, n), lambda i: (i, 0))
    return pl.pallas_call(
        body, name="ada_update", grid=(d // tr,),
        out_shape=(jax.ShapeDtypeStruct((d, n), F32),) * 4,
        in_specs=[pl.BlockSpec((N_DEV, tr), lambda i: (0, i)), pl.BlockSpec((N_DEV, n), lambda i: (0, 0)),
                  tile, tile, tile],
        out_specs=[tile] * 4,
        compiler_params=_params(("arbitrary",)),
    )(sc_all, dmod_cols, w, m, v)


def _small_update(vec, cg, smalls):
    l = vec.shape[2]
    rows = cg.shape[1]
    ns = len(smalls)

    def body(*refs):
        vec_ref, cg_ref = refs[0], refs[1]
        wmv = refs[2:2 + 3 * ns]
        vall_ref = refs[2 + 3 * ns]
        outs = refs[3 + 3 * ns:3 + 7 * ns]
        cgr_ref, send_sems, recv_sems = refs[3 + 7 * ns:]
        x, y, c = _my_coords()
        me = _slot(x, y, c)
        vall_ref[me] = vec_ref[0]
        cgr_ref[me] = cg_ref[me]
        copies = []
        for k in range(1, N_DEV):
            peer = (_flip(x, k & 4), _flip(y, k & 2), _flip(c, k & 1))
            copies.append(pltpu.make_async_remote_copy(
                src_ref=vall_ref.at[me], dst_ref=vall_ref.at[me], send_sem=send_sems.at[k - 1],
                recv_sem=recv_sems.at[k - 1], device_id=peer, device_id_type=MESH))
            copies.append(pltpu.make_async_remote_copy(
                src_ref=cg_ref.at[_slot(*peer)], dst_ref=cgr_ref.at[me], send_sem=send_sems.at[7 + k - 1],
                recv_sem=recv_sems.at[7 + k - 1], device_id=peer, device_id_type=MESH))
        for cp in copies:
            cp.start()
        for cp in copies:
            cp.wait_recv()
        for p, (_, _, _, lo, hi, kind) in enumerate(smalls):
            w_ref, m_ref, v_ref = wmv[3 * p:3 * p + 3]
            part = (lambda s: vall_ref[s, :, lo:hi]) if kind == "vec" else (lambda s: cgr_ref[s, lo:hi, :])
            g = part(0)
            for s in range(1, N_DEV):
                g = g + part(s)
            delta, nm, nv = _adamw(w_ref[...], g, m_ref[...], v_ref[...])
            for o_ref, val in zip(outs[4 * p:4 * p + 4], (g, delta, nm, nv)):
                o_ref[...] = val
        for cp in copies:
            cp.wait_send()

    vm = pl.BlockSpec(memory_space=pltpu.VMEM)
    args = [vec, cg]
    out_shape = [jax.ShapeDtypeStruct((N_DEV, 1, l), F32)]
    for w, m, v, _, _, _ in smalls:
        args += [w, m, v]
    for w, _, _, _, _, _ in smalls:
        out_shape += [jax.ShapeDtypeStruct(w.shape, F32)] * 4
    res = pl.pallas_call(
        body, name="small_update",
        out_shape=tuple(out_shape),
        in_specs=[vm] * len(args), out_specs=[vm] * len(out_shape),
        scratch_shapes=[pltpu.VMEM((N_DEV, rows, LANES), F32),
                        pltpu.SemaphoreType.DMA((14,)), pltpu.SemaphoreType.DMA((14,))],
        compiler_params=_params(),
    )(*args)
    return res[0], [res[1 + 4 * p:5 + 4 * p] for p in range(ns)]


def _pick(t, want):
    return want if t % want == 0 else t


def kernel(x, c, w_ada, b_ada, norm_mix_g, w_in, conv_short_w, w_short_out, conv_conf_w, conv_conf_b, conf_ln_g, conf_ln_b, w_conf_out, w_o, norm_ffn_g, w_ffn_in, w_ffn_out, final_norm_g, loss_target, m_w_ada, m_b_ada, m_norm_mix_g, m_w_in, m_conv_short_w, m_w_short_out, m_conv_conf_w, m_conv_conf_b, m_conf_ln_g, m_conf_ln_b, m_w_conf_out, m_w_o, m_norm_ffn_g, m_w_ffn_in, m_w_ffn_out, m_final_norm_g, v_w_ada, v_b_ada, v_norm_mix_g, v_w_in, v_conv_short_w, v_w_short_out, v_conv_conf_w, v_conv_conf_b, v_conf_ln_g, v_conf_ln_b, v_w_conf_out, v_w_o, v_norm_ffn_g, v_w_ffn_in, v_w_ffn_out, v_final_norm_g):
    t, d = x.shape[1], x.shape[2]
    x2 = x.reshape(t, d)
    tgt = loss_target.reshape(t, d)
    me = _slot(*_my_coords())
    tm = _pick(t, 256)
    tm_fwd_in = _pick(t, 1024)
    tk = _pick(t, 2048)

    taps_loc = jnp.zeros((40, LANES), F32)
    taps_loc = taps_loc.at[0:SHORT_K].set(conv_short_w[0]).at[8:8 + CONF_K].set(conv_conf_w[0])
    mod_flat, sc_all3, taps = _ada_forward(c, w_ada[0], b_ada, taps_loc)
    mod = jnp.concatenate([mod_flat.reshape(6, d), jnp.zeros((2, d), F32)], axis=0)
    prm = jnp.concatenate([norm_mix_g, norm_ffn_g, final_norm_g.reshape(1, d), conv_conf_b, conf_ln_g, conf_ln_b,
                           jnp.zeros((2, d), F32)], axis=0)

    n_in = w_in.shape[2]
    nk = t // tk
    tok = pl.BlockSpec((tk, d), lambda s, k: (k, 0))
    rows = d // N_DEV
    frows = w_ffn_out.shape[1]

    proj, h, w_in_g, (w_so_g, w_co_g, w_o_g, w_fi_g, w_fo_g) = _forward_in(
        jnp.reshape(me, (1,)).astype(jnp.int32), x2, mod, prm, w_in[0].astype(BF16), tm_fwd_in,
        [w_short_out[0].astype(BF16), w_conf_out[0].astype(BF16), w_o[0].astype(BF16),
         w_ffn_in[0].astype(BF16), w_ffn_out[0].astype(BF16)])
    fb = w_fi_g.shape[2]
    w_so = w_so_g.reshape(d, d)
    w_co = w_co_g.reshape(d, d)
    w_oo = w_o_g.reshape(d, d)
    w_fo4 = w_fo_g.reshape(N_DEV // 2, fb, d)
    x1, y_a, y_b, mix, u1 = _forward_mix(x2, proj, mod, prm, taps, w_so, w_co, w_oo, tm)
    dx2, ab, h2, sums_f = _forward_ffn(x1, tgt, mod, prm, w_fi_g, w_fo4, tm)

    dx1, df, act, dab, sums_b = _backward_ffn(dx2, x1, ab, mod, prm, w_fi_g, w_fo4, tm)
    g_fi = _weight_grad(h2, dab, tok, pl.BlockSpec((1, tk, fb), lambda s, k: (s, k, 0)), N_DEV, d, fb, nk, "grad_w_ffn_in")
    g_fo = _weight_grad(act, df, pl.BlockSpec((1, tk, fb), lambda s, k: (s, k, 0)), tok, N_DEV // 2, fb, d, nk, "grad_w_ffn_out")
    (dproj, dmix, dya, dyb, merged, ya_pre, u3, sums_m, dw3p, dw31p), (p_fi, p_fo) = _backward_mix(
        dx1, proj, y_a, y_b, mix, u1, mod, prm, taps, w_so, w_co, w_oo, tm,
        [g_fi, g_fo.reshape(N_DEV, frows, d)])
    g_so = _weight_grad(ya_pre, dya, tok, tok, 1, d, d, nk, "grad_w_short_out")
    g_co = _weight_grad(u3, dyb, tok, tok, 1, d, d, nk, "grad_w_conf_out")
    g_oo = _weight_grad(merged, dmix, tok, tok, 1, d, d, nk, "grad_w_o")
    grad_x, sums_i = _backward_in(dproj, x2, dx1, mod, prm, w_in_g, tm)
    g_in, (p_so, p_co, p_oo) = _weight_grad(
        h, dproj, tok, pl.BlockSpec((tk, n_in), lambda s, k: (k, s)), N_DEV, d, n_in, nk, "grad_w_in",
        [g_so.reshape(N_DEV, rows, d), g_co.reshape(N_DEV, rows, d), g_oo.reshape(N_DEV, rows, d)])
    (p_in,) = _scatter_grads([g_in], "scatter_w_in")

    up_in = _adamw_shard(p_in, w_in[0], m_w_in[0], v_w_in[0], _pick(d, 256), "adamw_w_in")
    up_so = _adamw_shard(p_so, w_short_out[0], m_w_short_out[0], v_w_short_out[0], rows, "adamw_w_short_out")
    up_co = _adamw_shard(p_co, w_conf_out[0], m_w_conf_out[0], v_w_conf_out[0], rows, "adamw_w_conf_out")
    up_oo = _adamw_shard(p_oo, w_o[0], m_w_o[0], v_w_o[0], rows, "adamw_w_o")
    up_fi = _adamw_shard(p_fi, w_ffn_in[0], m_w_ffn_in[0], v_w_ffn_in[0], _pick(d, 256), "adamw_w_ffn_in")
    up_fo = _adamw_shard(p_fo, w_ffn_out[0], m_w_ffn_out[0], v_w_ffn_out[0], frows, "adamw_w_ffn_out")

    vec = jnp.concatenate([sums_i[0:2], sums_m[0:1], sums_b[0:2], sums_f[1:2],
                           sums_i[2:3], sums_m[3:4], sums_m[1:3], sums_b[2:3], sums_f[0:1]], axis=0)
    vec = vec.reshape(1, 1, 12 * d)
    dw3 = dw3p.reshape(SHORT_K, SUBLANES, d).sum(axis=1)
    dw31 = dw31p.reshape(CONF_K, SUBLANES, d).sum(axis=1)
    cg = jnp.zeros((40, d), F32).at[0:SHORT_K].set(dw3).at[8:8 + CONF_K].set(dw31)
    cg = cg.reshape(40, N_DEV, LANES).transpose(1, 0, 2)
    fin = lambda a: a.reshape(1, d)
    tap = lambda a: a.reshape(a.shape[1:])
    smalls = [
        (b_ada, m_b_ada, v_b_ada, 0, 6 * d, "vec"),
        (norm_mix_g, m_norm_mix_g, v_norm_mix_g, 6 * d, 7 * d, "vec"),
        (tap(conv_short_w), tap(m_conv_short_w), tap(v_conv_short_w), 0, SHORT_K, "cg"),
        (tap(conv_conf_w), tap(m_conv_conf_w), tap(v_conv_conf_w), 8, 8 + CONF_K, "cg"),
        (conv_conf_b, m_conv_conf_b, v_conv_conf_b, 7 * d, 8 * d, "vec"),
        (conf_ln_g, m_conf_ln_g, v_conf_ln_g, 8 * d, 9 * d, "vec"),
        (conf_ln_b, m_conf_ln_b, v_conf_ln_b, 9 * d, 10 * d, "vec"),
        (norm_ffn_g, m_norm_ffn_g, v_norm_ffn_g, 10 * d, 11 * d, "vec"),
        (fin(final_norm_g), fin(m_final_norm_g), fin(v_final_norm_g), 11 * d, 12 * d, "vec"),
    ]
    vall, up_small = _small_update(vec, cg, smalls)
    n_ada = w_ada.shape[2]
    dmod_all = vall.reshape(N_DEV, 12 * d)[:, 0:6 * d]
    dmod_cols = lax.dynamic_slice(dmod_all, (0, me * n_ada), (N_DEV, n_ada))
    up_ada = _ada_update(sc_all3.reshape(N_DEV, d), dmod_cols, w_ada[0], m_w_ada[0], v_w_ada[0], _pick(d, 256))

    loss = lax.psum(jnp.sum(sums_f[2]), ("x", "y", "c"))

    lead = lambda a: a.reshape((1,) + a.shape)
    ups = [tuple(lead(a) for a in up_ada), up_small[0], up_small[1], tuple(lead(a) for a in up_in),
           tuple(lead(a) for a in up_small[2]), tuple(lead(a) for a in up_so), tuple(lead(a) for a in up_small[3]),
           up_small[4], up_small[5], up_small[6],
           tuple(lead(a) for a in up_co), tuple(lead(a) for a in up_oo), up_small[7],
           tuple(lead(a) for a in up_fi), tuple(lead(a) for a in up_fo),
           tuple(a.reshape(d) for a in up_small[8])]
    grads = [u[0] for u in ups]
    deltas = [u[1] for u in ups]
    new_m = [u[2] for u in ups]
    new_v = [u[3] for u in ups]
    return (loss, grad_x.reshape(1, t, d), *grads, *deltas, *new_m, *new_v)
```

```python
import functools

import jax
import jax.numpy as jnp
from jax import lax
from jax.experimental import pallas as pl
from jax.experimental.pallas import tpu as pltpu

F32 = jnp.float32
BF16 = jnp.bfloat16
MESH = pl.DeviceIdType.MESH

N_DEV = 8
EPS = 1e-6
LN_EPS = 1e-5
SHORT_K = 3
CONF_K = 31
ADAM_LR = 0.001
ADAM_B1 = 0.9
ADAM_B2 = 0.999
ADAM_EPS = 1e-08
ADAM_WD = 0.01
ADAM_STEP = 10

LANES = 128
SUBLANES = 8
CONV_ROWS = 64
HALO_SHORT = 8
HALO_CONF = 32
VMEM_LIMIT = 56 * 1024 * 1024

M_SH1, M_SC1, M_G1, M_SH2, M_SC2, M_G2 = range(6)
P_GMIX, P_GFFN, P_GFIN, P_CBIAS, P_LNG, P_LNB = range(6)


def _params(sem=None, **kw):
    return pltpu.CompilerParams(dimension_semantics=sem, vmem_limit_bytes=VMEM_LIMIT, **kw)


def _sigmoid(v):
    return jax.nn.sigmoid(v)


def _dot(a, b):
    return jnp.dot(a, b, preferred_element_type=F32)


def _dot_nt(a, b):
    return lax.dot_general(a, b, (((1,), (1,)), ((), ())), preferred_element_type=F32)


def _dot_tn(a, b):
    return lax.dot_general(a, b, (((0,), (0,)), ((), ())), preferred_element_type=F32)


def _colsum(v):
    return jnp.sum(v, axis=0, keepdims=True)


def _rowmean(v):
    return jnp.mean(v, axis=-1, keepdims=True)


def _my_coords():
    return lax.axis_index("x"), lax.axis_index("y"), lax.axis_index("c")


def _slot(px, py, pc):
    return 4 * px + 2 * py + pc


def _flip(v, bit):
    return 1 - v if bit else v


def _taps_by_residue(taps):
    by_res = {}
    for wi, off in taps:
        by_res.setdefault(off % SUBLANES, []).append((wi, off // SUBLANES))
    return sorted(by_res.items())


def _tap_conv(ext_ref, w_ref, taps, tm, extra, out_ref):
    d = out_ref.shape[1]
    rb = min(CONV_ROWS, tm)
    wrows = rb + extra
    groups = _taps_by_residue(taps)

    def block(i, carry):
        base = pl.multiple_of(i * rb, SUBLANES)
        for lc in range(d // LANES):
            ls = pl.ds(lc * LANES, LANES)
            win = ext_ref[pl.ds(base, wrows), ls]
            acc = None
            for r, lst in groups:
                sh = win if r == 0 else pltpu.roll(win, wrows - r, 0)
                for wi, q in lst:
                    term = w_ref[wi:wi + 1, ls] * sh[SUBLANES * q:SUBLANES * q + rb, :]
                    acc = term if acc is None else acc + term
            out_ref[pl.ds(base, rb), ls] = acc
        return carry

    lax.fori_loop(0, tm // rb, block, 0)


def _tap_wgrad(a_ref, ext_ref, taps, tm, extra, acc_ref):
    d = a_ref.shape[1]
    rb = min(CONV_ROWS, tm)
    wrows = rb + extra
    groups = _taps_by_residue(taps)

    def block(i, carry):
        base = pl.multiple_of(i * rb, SUBLANES)
        for lc in range(d // LANES):
            ls = pl.ds(lc * LANES, LANES)
            a_blk = a_ref[pl.ds(base, rb), ls]
            win = ext_ref[pl.ds(base, wrows), ls]
            for r, lst in groups:
                sh = win if r == 0 else pltpu.roll(win, wrows - r, 0)
                for wi, q in lst:
                    prod = a_blk * sh[SUBLANES * q:SUBLANES * q + rb, :]
                    part = prod[0:SUBLANES, :]
                    for s in range(1, rb // SUBLANES):
                        part = part + prod[SUBLANES * s:SUBLANES * (s + 1), :]
                    rows = pl.ds(SUBLANES * wi, SUBLANES)
                    acc_ref[rows, ls] = acc_ref[rows, ls] + part
        return carry

    lax.fori_loop(0, tm // rb, block, 0)


def _causal_taps(k, halo):
    return [(i, halo - (k - 1) + i) for i in range(k)]


def _anticausal_taps(k):
    return [(i, (k - 1) - i) for i in range(k)]


def _ada_forward(c, w_ada_loc, b_ada, taps_loc):
    d = c.shape[1]
    nloc = w_ada_loc.shape[1]
    trows = taps_loc.shape[0]

    def body(c_ref, w_ref, b_ref, t_ref, mod_ref, sc_ref, taps_ref,
             part_ref, modrecv_ref, tapsall_ref, send_sems, recv_sems):
        x, y, cc = _my_coords()
        me = _slot(x, y, cc)
        cv = c_ref[...]
        sc_ref[me] = cv * _sigmoid(cv)
        tapsall_ref[me] = t_ref[...]

        def peer_of(k):
            return (_flip(x, k & 4), _flip(y, k & 2), _flip(cc, k & 1))

        def gather_copy(ref, base, k):
            return pltpu.make_async_remote_copy(
                src_ref=ref.at[me], dst_ref=ref.at[me], send_sem=send_sems.at[base + k - 1],
                recv_sem=recv_sems.at[base + k - 1], device_id=peer_of(k), device_id_type=MESH)

        first = [gather_copy(sc_ref, 0, k) for k in range(1, N_DEV)]
        first += [gather_copy(tapsall_ref, 7, k) for k in range(1, N_DEV)]
        for cp in first:
            cp.start()
        for cp in first[:7]:
            cp.wait_recv()
        sc_all = jnp.concatenate([sc_ref[s] for s in range(N_DEV)], axis=0)
        part = jnp.dot(sc_all, w_ref[...], preferred_element_type=F32,
                       precision=lax.Precision.HIGHEST)
        for b in range(N_DEV):
            part_ref[b] = part[b:b + 1, :]
        modrecv_ref[me] = part_ref[me]
        second = []
        for k in range(1, N_DEV):
            px, py, pc = peer_of(k)
            second.append(pltpu.make_async_remote_copy(
                src_ref=part_ref.at[_slot(px, py, pc)], dst_ref=modrecv_ref.at[me],
                send_sem=send_sems.at[14 + k - 1], recv_sem=recv_sems.at[14 + k - 1],
                device_id=(px, py, pc), device_id_type=MESH))
        for cp in second:
            cp.start()
        for cp in second:
            cp.wait_recv()
        mod = jnp.concatenate([modrecv_ref[s] for s in range(N_DEV)], axis=1)
        mod_ref[...] = mod + b_ref[...]
        for cp in first[7:]:
            cp.wait_recv()
        taps_ref[...] = jnp.concatenate([tapsall_ref[s] for s in range(N_DEV)], axis=1)
        for cp in first + second:
            cp.wait_send()

    return pl.pallas_call(
        body, name="ada_forward",
        out_shape=(jax.ShapeDtypeStruct((1, N_DEV * nloc), F32),
                   jax.ShapeDtypeStruct((N_DEV, 1, d), F32),
                   jax.ShapeDtypeStruct((trows, N_DEV * LANES), F32)),
        in_specs=[pl.BlockSpec(memory_space=pltpu.VMEM)] * 4,
        out_specs=[pl.BlockSpec(memory_space=pltpu.VMEM)] * 3,
        scratch_shapes=[pltpu.VMEM((N_DEV, 1, nloc), F32), pltpu.VMEM((N_DEV, 1, nloc), F32),
                        pltpu.VMEM((N_DEV, trows, LANES), F32),
                        pltpu.SemaphoreType.DMA((21,)), pltpu.SemaphoreType.DMA((21,))],
        compiler_params=_params(),
    )(c, w_ada_loc, b_ada, taps_loc)


ANY_SPEC = pl.BlockSpec(memory_space=pl.ANY)


def _comm_scratch(na):
    return [pltpu.SemaphoreType.DMA((7 * na,)), pltpu.SemaphoreType.DMA((7 * na,)), pltpu.SemaphoreType.DMA((na,))]


class _Gather:
    def __init__(self, ins, outs, send_sems, recv_sems, local_sems):
        self.na = len(ins)
        x, y, c = _my_coords()
        self.c = c
        self.me, self.sibling = (x, y, c), (x, y, 1 - c)
        self.chips = [(1 - x, y), (x, 1 - y), (1 - x, 1 - y)]
        self.outs, self.send_sems, self.recv_sems = outs, send_sems, recv_sems
        self.mine = [pltpu.make_async_copy(ins[a], outs[a].at[_slot(*self.me)], local_sems.at[a])
                     for a in range(self.na)]
        self.first = []
        for a in range(self.na):
            self.first.append(self._copy(a, 0, self.me, self.sibling, src=ins[a]))
            self.first += [self._copy(a, 1 + j, self.me, (*chip, c), src=ins[a]) for j, chip in enumerate(self.chips)]
        self.passed = [self._copy(a, 4 + j, (*chip, c), self.sibling)
                       for a in range(self.na) for j, chip in enumerate(self.chips)]

    def _copy(self, a, k, block, to, src=None):
        dst = self.outs[a].at[_slot(*block)]
        return pltpu.make_async_remote_copy(
            src_ref=dst if src is None else src, dst_ref=dst,
            send_sem=self.send_sems.at[7 * a + k], recv_sem=self.recv_sems.at[7 * a + k],
            device_id=to, device_id_type=MESH)

    def start(self):
        for cp in self.mine + self.first:
            cp.start()

    def forward_chip(self, j):
        for a in range(self.na):
            self._copy(a, 1 + j, (*self.chips[j], self.c), self.me).wait_recv()
            self.passed[3 * a + j].start()

    def wait_sibling_own(self):
        for a in range(self.na):
            self._copy(a, 0, self.sibling, self.me).wait_recv()

    def wait_sibling_passed(self, j):
        for a in range(self.na):
            self._copy(a, 4 + j, (*self.chips[j], 1 - self.c), self.me).wait_recv()

    def finish_sends(self):
        for cp in self.first + self.passed:
            cp.wait_send()
        for cp in self.mine:
            cp.wait()

    def forward(self):
        for j in range(3):
            self.forward_chip(j)

    def finish(self):
        self.wait_sibling_own()
        for j in range(3):
            self.wait_sibling_passed(j)
        self.finish_sends()


def _scatter_plan(ins, outs, send_sems, recv_sems, local_sems):
    na = len(ins)
    x, y, c = _my_coords()
    me = _slot(x, y, c)
    mine = [pltpu.make_async_copy(ins[a].at[me], outs[a].at[me], local_sems.at[a]) for a in range(na)]
    copies = []
    for k in range(1, N_DEV):
        peer = (_flip(x, k & 4), _flip(y, k & 2), _flip(c, k & 1))
        for a in range(na):
            copies.append(pltpu.make_async_remote_copy(
                src_ref=ins[a].at[_slot(*peer)], dst_ref=outs[a].at[me],
                send_sem=send_sems.at[7 * a + k - 1], recv_sem=recv_sems.at[7 * a + k - 1],
                device_id=peer, device_id_type=MESH))

    def start():
        for cp in mine + copies:
            cp.start()

    def finish():
        for cp in copies:
            cp.wait_recv()
        for cp in copies:
            cp.wait_send()
        for cp in mine:
            cp.wait()

    return start, finish


def _forward_in(me_arr, x, mod, prm, w_in_loc, tm, shards):
    t, d = x.shape
    n = w_in_loc.shape[1]
    ns = N_DEV
    na = len(shards)
    nt = t // tm

    def body(*refs):
        x_ref, mod_ref, prm_ref, wloc_ref = refs[1:5]
        proj_ref, h_ref, wg_ref = refs[5 + na:8 + na]
        hall_ref, wv_ref, wv_sem = refs[8 + 2 * na:11 + 2 * na]
        sems = refs[11 + 2 * na:]
        s, i = pl.program_id(0), pl.program_id(1)
        g_in = _Gather([wloc_ref], [wg_ref], *sems[0:3])
        g_rest = _Gather(refs[5:5 + na], refs[8 + na:8 + 2 * na], *sems[3:6])
        me = _slot(*_my_coords())

        def load_shard(src):
            cp = pltpu.make_async_copy(src, wv_ref, wv_sem)
            cp.start()
            cp.wait()

        @pl.when((s == 0) & (i == 0))
        def _():
            g_in.start()
            g_rest.start()
            load_shard(wloc_ref)

        arrivals = {1: g_in.wait_sibling_own,
                    2: functools.partial(g_in.forward_chip, 1), 3: functools.partial(g_in.wait_sibling_passed, 1),
                    4: functools.partial(g_in.forward_chip, 0), 5: functools.partial(g_in.wait_sibling_passed, 0),
                    6: functools.partial(g_in.forward_chip, 2), 7: functools.partial(g_in.wait_sibling_passed, 2)}
        for step, arrive in arrivals.items():
            @pl.when((s == step) & (i == 0))
            def _(step=step, arrive=arrive):
                arrive()
                if step == 6:
                    g_rest.forward()
                load_shard(wg_ref.at[me ^ step])

        @pl.when(s == 0)
        def _():
            xv = x_ref[...]
            r = lax.rsqrt(_rowmean(xv * xv) + EPS)
            h = xv * r * prm_ref[P_GMIX:P_GMIX + 1, :] * (1.0 + mod_ref[M_SC1:M_SC1 + 1, :]) \
                + mod_ref[M_SH1:M_SH1 + 1, :]
            hb = h.astype(BF16)
            hall_ref[i] = hb
            h_ref[...] = hb

        proj_ref[...] = _dot(hall_ref[i], wv_ref[...]).astype(BF16)

        @pl.when((s == ns - 1) & (i == nt - 1))
        def _():
            g_in.finish_sends()
            g_rest.finish()

    x_tile = pl.BlockSpec((tm, d), lambda s, i, me: (jnp.where(s == 0, i, nt - 1), 0))
    whole = pl.BlockSpec((8, d), lambda s, i, me: (0, 0))
    res = pl.pallas_call(
        body, name="forward_in",
        grid_spec=pltpu.PrefetchScalarGridSpec(
            num_scalar_prefetch=1, grid=(ns, nt),
            in_specs=[x_tile, whole, whole, ANY_SPEC] + [ANY_SPEC] * na,
            out_specs=[pl.BlockSpec((tm, n), lambda s, i, me: (i, me[0] ^ s)), x_tile, ANY_SPEC] + [ANY_SPEC] * na,
            scratch_shapes=[pltpu.VMEM((nt, tm, d), BF16), pltpu.VMEM((d, n), BF16), pltpu.SemaphoreType.DMA]
            + _comm_scratch(1) + _comm_scratch(na)),
        out_shape=(jax.ShapeDtypeStruct((t, ns * n), BF16), jax.ShapeDtypeStruct((t, d), BF16),
                   jax.ShapeDtypeStruct((N_DEV, d, n), BF16))
        + tuple(jax.ShapeDtypeStruct((N_DEV,) + a.shape, a.dtype) for a in shards),
        compiler_params=_params(("arbitrary", "arbitrary")),
    )(me_arr, x, mod, prm, w_in_loc, *shards)
    return res[0], res[1], res[2], res[3:]


def _forward_mix(x, proj, mod, prm, taps, w_so, w_co, w_o, tm, shards):
    t, d = x.shape
    nt = t // tm
    na = len(shards)
    hs, hc = HALO_SHORT, HALO_CONF
    r3, r31 = 0, 8

    def body(*refs):
        x_ref, proj_ref, mod_ref, prm_ref, taps_ref, wso_ref, wco_ref, wo_ref = refs[:8]
        x1_ref, ya_ref, yb_ref, mix_ref, u1_ref = refs[8 + na:13 + na]
        cv_ext, u0_ext, conv3_ref, u1f_ref = refs[13 + 2 * na:17 + 2 * na]
        gather = _Gather(refs[8:8 + na], refs[13 + na:13 + 2 * na], *refs[17 + 2 * na:])
        i = pl.program_id(0)

        @pl.when(i == 0)
        def _():
            gather.start()
            cv_ext[0:hs, :] = jnp.zeros((hs, d), F32)
            u0_ext[0:hc, :] = jnp.zeros((hc, d), F32)

        @pl.when(i == (3 * nt) // 4)
        def _():
            gather.forward()

        def col(g):
            return proj_ref[:, g * d:(g + 1) * d].astype(F32)

        cv_ext[hs:hs + tm, :] = col(1) * col(2)
        u0_ext[hc:hc + tm, :] = col(3) * _sigmoid(col(4))
        _tap_conv(cv_ext, taps_ref, [(r3 + wi, off) for wi, off in _causal_taps(SHORT_K, hs)], tm, hs, conv3_ref)
        _tap_conv(u0_ext, taps_ref, [(r31 + wi, off) for wi, off in _causal_taps(CONF_K, hc)], tm, hc, u1f_ref)
        cv_ext[0:hs, :] = cv_ext[tm:tm + hs, :]
        u0_ext[0:hc, :] = u0_ext[tm:tm + hc, :]

        ya_pre = (col(0) * conv3_ref[...]).astype(BF16)
        y_a = _dot(ya_pre, wso_ref[...])
        u1 = u1f_ref[...] + prm_ref[P_CBIAS:P_CBIAS + 1, :]
        u1_ref[...] = u1.astype(BF16)
        u1 = u1_ref[...].astype(F32)
        mu = _rowmean(u1)
        uc = u1 - mu
        rstd = lax.rsqrt(_rowmean(uc * uc) + LN_EPS)
        u2 = uc * rstd * prm_ref[P_LNG:P_LNG + 1, :] + prm_ref[P_LNB:P_LNB + 1, :]
        u3 = (u2 * _sigmoid(u2)).astype(BF16)
        y_b = _dot(u3, wco_ref[...])
        ya_ref[...] = y_a.astype(BF16)
        yb_ref[...] = y_b.astype(BF16)
        merged = _sigmoid(col(5)) * ya_ref[...].astype(F32) + _sigmoid(col(6)) * yb_ref[...].astype(F32)
        mix = _dot(merged.astype(BF16), wo_ref[...])
        mix_ref[...] = mix.astype(BF16)
        x1_ref[...] = x_ref[...] + mod_ref[M_G1:M_G1 + 1, :] * mix

        @pl.when(i == nt - 1)
        def _():
            gather.finish()

    tile = pl.BlockSpec((tm, d), lambda i: (i, 0))
    whole = lambda shape: pl.BlockSpec(shape, lambda i: (0,) * len(shape))
    res = pl.pallas_call(
        body, name="forward_mix", grid=(nt,),
        out_shape=(jax.ShapeDtypeStruct((t, d), F32),) + (jax.ShapeDtypeStruct((t, d), BF16),) * 4
        + tuple(jax.ShapeDtypeStruct((N_DEV,) + a.shape, a.dtype) for a in shards),
        in_specs=[tile, pl.BlockSpec((tm, 7 * d), lambda i: (i, 0)), whole((8, d)), whole((8, d)),
                  whole(taps.shape), whole((d, d)), whole((d, d)), whole((d, d))] + [ANY_SPEC] * na,
        out_specs=[tile] * 5 + [ANY_SPEC] * na,
        scratch_shapes=[pltpu.VMEM((hs + tm, d), F32), pltpu.VMEM((hc + tm, d), F32),
                        pltpu.VMEM((tm, d), F32), pltpu.VMEM((tm, d), F32)] + _comm_scratch(na),
        compiler_params=_params(("arbitrary",)),
    )(x, proj, mod, prm, taps, w_so, w_co, w_o, *shards)
    return res[:5], res[5:]


def _forward_ffn(x1, tgt, mod, prm, w_fi_g, w_fo_g, tm):
    t, d = x1.shape
    ns, _, fb = w_fi_g.shape
    nh = ns // 2

    def body(x1_ref, tgt_ref, mod_ref, prm_ref, wfi_hbm, wfo_hbm,
             dx2_ref, ab_ref, h2_ref, sums_ref, wfi_ref, wfo_ref):
        i = pl.program_id(0)

        @pl.when(i == 0)
        def _():
            pltpu.sync_copy(wfi_hbm, wfi_ref)
            pltpu.sync_copy(wfo_hbm, wfo_ref)
            sums_ref[...] = jnp.zeros((8, d), F32)

        x1v = x1_ref[...]
        r2 = lax.rsqrt(_rowmean(x1v * x1v) + EPS)
        h2 = (x1v * r2 * prm_ref[P_GFFN:P_GFFN + 1, :] * (1.0 + mod_ref[M_SC2:M_SC2 + 1, :])
              + mod_ref[M_SH2:M_SH2 + 1, :]).astype(BF16)
        h2_ref[...] = h2
        f = jnp.zeros((tm, d), F32)
        for j in range(nh):
            ab_ref[j] = _dot(h2, wfi_ref[j]).astype(BF16)
            ab_ref[j + nh] = _dot(h2, wfi_ref[j + nh]).astype(BF16)
            a = ab_ref[j].astype(F32)
            act = (a * _sigmoid(a) * ab_ref[j + nh].astype(F32)).astype(BF16)
            f = f + _dot(act, wfo_ref[j])
        x2 = x1v + mod_ref[M_G2:M_G2 + 1, :] * f
        r3 = lax.rsqrt(_rowmean(x2 * x2) + EPS)
        xn3 = x2 * r3
        gfin = prm_ref[P_GFIN:P_GFIN + 1, :]
        err = xn3 * gfin - tgt_ref[...]
        dy = err * (1.0 / d)
        dxn3 = dy * gfin
        dx2 = r3 * (dxn3 - xn3 * _rowmean(dxn3 * xn3))
        dx2_ref[...] = dx2
        sums_ref[0:1, :] = sums_ref[0:1, :] + _colsum(dy * xn3)
        sums_ref[1:2, :] = sums_ref[1:2, :] + _colsum(dx2 * f)
        sums_ref[2:3, :] = sums_ref[2:3, :] + _colsum(err * err) * (0.5 / d)

    tile = pl.BlockSpec((tm, d), lambda i: (i, 0))
    whole = lambda shape: pl.BlockSpec(shape, lambda i: (0,) * len(shape))
    any_spec = pl.BlockSpec(memory_space=pl.ANY)
    return pl.pallas_call(
        body, name="forward_ffn", grid=(t // tm,),
        out_shape=(jax.ShapeDtypeStruct((t, d), F32), jax.ShapeDtypeStruct((ns, t, fb), BF16),
                   jax.ShapeDtypeStruct((t, d), BF16), jax.ShapeDtypeStruct((8, d), F32)),
        in_specs=[tile, tile, whole((8, d)), whole((8, d)), any_spec, any_spec],
        out_specs=[tile, pl.BlockSpec((ns, tm, fb), lambda i: (0, i, 0)), tile, whole((8, d))],
        scratch_shapes=[pltpu.VMEM(w_fi_g.shape, BF16), pltpu.VMEM(w_fo_g.shape, BF16)],
        compiler_params=_params(("arbitrary",)),
    )(x1, tgt, mod, prm, w_fi_g, w_fo_g)


def _backward_ffn(dx2, x1, ab, mod, prm, w_fi_g, w_fo_g, tm):
    t, d = x1.shape
    ns, _, fb = w_fi_g.shape
    nh = ns // 2

    def body(dx2_ref, x1_ref, ab_ref, mod_ref, prm_ref, wfi_hbm, wfo_hbm,
             dx1_ref, df_ref, act_ref, dab_ref, sums_ref, wfi_ref, wfo_ref):
        i = pl.program_id(0)

        @pl.when(i == 0)
        def _():
            pltpu.sync_copy(wfi_hbm, wfi_ref)
            pltpu.sync_copy(wfo_hbm, wfo_ref)
            sums_ref[...] = jnp.zeros((8, d), F32)

        dx2v = dx2_ref[...]
        df = (mod_ref[M_G2:M_G2 + 1, :] * dx2v).astype(BF16)
        df_ref[...] = df
        dh2 = jnp.zeros((tm, d), F32)
        for j in range(nh):
            dact = _dot_nt(df, wfo_ref[j])
            a = ab_ref[j].astype(F32)
            b = ab_ref[j + nh].astype(F32)
            s = _sigmoid(a)
            sil = a * s
            act_ref[j] = (sil * b).astype(BF16)
            da = (dact * b * (s * (1.0 + a * (1.0 - s)))).astype(BF16)
            db = (dact * sil).astype(BF16)
            dab_ref[j] = da
            dab_ref[j + nh] = db
            dh2 = dh2 + _dot_nt(da, wfi_ref[j]) + _dot_nt(db, wfi_ref[j + nh])
        x1v = x1_ref[...]
        r2 = lax.rsqrt(_rowmean(x1v * x1v) + EPS)
        xn2 = x1v * r2
        gffn = prm_ref[P_GFFN:P_GFFN + 1, :]
        scale = 1.0 + mod_ref[M_SC2:M_SC2 + 1, :]
        dxn2 = dh2 * gffn * scale
        dx1_ref[...] = dx2v + r2 * (dxn2 - xn2 * _rowmean(dxn2 * xn2))
        hx = dh2 * xn2
        sums_ref[0:1, :] = sums_ref[0:1, :] + _colsum(dh2)
        sums_ref[1:2, :] = sums_ref[1:2, :] + _colsum(hx) * gffn
        sums_ref[2:3, :] = sums_ref[2:3, :] + _colsum(hx) * scale

    tile = pl.BlockSpec((tm, d), lambda i: (i, 0))
    whole = lambda shape: pl.BlockSpec(shape, lambda i: (0,) * len(shape))
    any_spec = pl.BlockSpec(memory_space=pl.ANY)
    return pl.pallas_call(
        body, name="backward_ffn", grid=(t // tm,),
        out_shape=(jax.ShapeDtypeStruct((t, d), F32), jax.ShapeDtypeStruct((t, d), BF16),
                   jax.ShapeDtypeStruct((nh, t, fb), BF16), jax.ShapeDtypeStruct((ns, t, fb), BF16),
                   jax.ShapeDtypeStruct((8, d), F32)),
        in_specs=[tile, tile, pl.BlockSpec((ns, tm, fb), lambda i: (0, i, 0)), whole((8, d)), whole((8, d)),
                  any_spec, any_spec],
        out_specs=[tile, tile, pl.BlockSpec((nh, tm, fb), lambda i: (0, i, 0)),
                   pl.BlockSpec((ns, tm, fb), lambda i: (0, i, 0)), whole((8, d))],
        scratch_shapes=[pltpu.VMEM(w_fi_g.shape, BF16), pltpu.VMEM(w_fo_g.shape, BF16)],
        compiler_params=_params(("arbitrary",)),
    )(dx2, x1, ab, mod, prm, w_fi_g, w_fo_g)


def _backward_mix(dx1, proj, y_a, y_b, mix, u1, mod, prm, taps, w_so, w_co, w_o, tm, partials):
    t, d = dx1.shape
    nt = t // tm
    na = len(partials)
    hs, hc = HALO_SHORT, HALO_CONF
    r3, r31 = 0, 8

    def body(*refs):
        (dx1_ref, proj_ref, halo_ref, ya_ref, yb_ref, mix_ref, u1_ref, mod_ref, prm_ref, taps_ref,
         wso_ref, wco_ref, wo_ref) = refs[:13]
        (dproj_ref, dmix_ref, dya_ref, dyb_ref, merged_ref, yapre_ref, u3_ref, sums_ref, dw3_ref,
         dw31_ref) = refs[13 + na:23 + na]
        cv_ext, u0_ext, d3_ext, du1_ext, tmp_ref = refs[23 + 2 * na:28 + 2 * na]
        scatter_start, scatter_finish = _scatter_plan(refs[13:13 + na], refs[23 + na:23 + 2 * na], *refs[28 + 2 * na:])
        i = pl.program_id(0)
        first_tile = i == nt - 1

        @pl.when(i == 0)
        def _():
            scatter_start()
            sums_ref[...] = jnp.zeros((8, d), F32)
            dw3_ref[...] = jnp.zeros(dw3_ref.shape, F32)
            dw31_ref[...] = jnp.zeros(dw31_ref.shape, F32)
            d3_ext[tm:tm + hs, :] = jnp.zeros((hs, d), F32)
            du1_ext[tm:tm + hc, :] = jnp.zeros((hc, d), F32)

        def col(g):
            return proj_ref[:, g * d:(g + 1) * d].astype(F32)

        def hcol(g, rows):
            v = halo_ref[HALO_CONF - rows:HALO_CONF, g * d:(g + 1) * d].astype(F32)
            return jnp.where(first_tile, 0.0, v)

        dx1v = dx1_ref[...]
        mixv = mix_ref[...].astype(F32)
        dmix = (mod_ref[M_G1:M_G1 + 1, :] * dx1v).astype(BF16)
        dmix_ref[...] = dmix
        sums_ref[0:1, :] = sums_ref[0:1, :] + _colsum(dx1v * mixv)
        dmerged = _dot_nt(dmix, wo_ref[...])
        ga = _sigmoid(col(5))
        gb = _sigmoid(col(6))
        yav = ya_ref[...].astype(F32)
        ybv = yb_ref[...].astype(F32)
        dya = (dmerged * ga).astype(BF16)
        dyb = (dmerged * gb).astype(BF16)
        dya_ref[...] = dya
        dyb_ref[...] = dyb
        dproj_ref[:, 5 * d:6 * d] = (dmerged * yav * ga * (1.0 - ga)).astype(BF16)
        dproj_ref[:, 6 * d:7 * d] = (dmerged * ybv * gb * (1.0 - gb)).astype(BF16)
        merged_ref[...] = (ga * yav + gb * ybv).astype(BF16)

        dya_pre = _dot_nt(dya, wso_ref[...])
        c_s, v_s, b_s = col(1), col(2), col(0)
        cv_ext[0:hs, :] = hcol(1, hs) * hcol(2, hs)
        cv_ext[hs:hs + tm, :] = c_s * v_s
        _tap_conv(cv_ext, taps_ref, [(r3 + wi, off) for wi, off in _causal_taps(SHORT_K, hs)], tm, hs, tmp_ref)
        conv3 = tmp_ref[...]
        yapre_ref[...] = (b_s * conv3).astype(BF16)
        dproj_ref[:, 0:d] = (dya_pre * conv3).astype(BF16)
        d3_ext[0:tm, :] = dya_pre * b_s
        _tap_wgrad(d3_ext, cv_ext, _causal_taps(SHORT_K, hs), tm, hs, dw3_ref)
        _tap_conv(d3_ext, taps_ref, [(r3 + wi, off) for wi, off in _anticausal_taps(SHORT_K)], tm, hs, tmp_ref)
        dcv = tmp_ref[...]
        dproj_ref[:, d:2 * d] = (dcv * v_s).astype(BF16)
        dproj_ref[:, 2 * d:3 * d] = (dcv * c_s).astype(BF16)
        d3_ext[tm:tm + hs, :] = d3_ext[0:hs, :]

        du3 = _dot_nt(dyb, wco_ref[...])
        u1v = u1_ref[...].astype(F32)
        mu = _rowmean(u1v)
        uc = u1v - mu
        rstd = lax.rsqrt(_rowmean(uc * uc) + LN_EPS)
        uhat = uc * rstd
        lng = prm_ref[P_LNG:P_LNG + 1, :]
        u2 = uhat * lng + prm_ref[P_LNB:P_LNB + 1, :]
        s2 = _sigmoid(u2)
        u3_ref[...] = (u2 * s2).astype(BF16)
        du2 = du3 * (s2 * (1.0 + u2 * (1.0 - s2)))
        sums_ref[1:2, :] = sums_ref[1:2, :] + _colsum(du2 * uhat)
        sums_ref[2:3, :] = sums_ref[2:3, :] + _colsum(du2)
        duhat = du2 * lng
        du1 = rstd * (duhat - _rowmean(duhat) - uhat * _rowmean(duhat * uhat))
        sums_ref[3:4, :] = sums_ref[3:4, :] + _colsum(du1)
        du1_ext[0:tm, :] = du1
        v_c = col(3)
        sg = _sigmoid(col(4))
        u0_ext[0:hc, :] = hcol(3, hc) * _sigmoid(hcol(4, hc))
        u0_ext[hc:hc + tm, :] = v_c * sg
        _tap_wgrad(du1_ext, u0_ext, _causal_taps(CONF_K, hc), tm, hc, dw31_ref)
        _tap_conv(du1_ext, taps_ref, [(r31 + wi, off) for wi, off in _anticausal_taps(CONF_K)], tm, hc, tmp_ref)
        du0 = tmp_ref[...]
        dproj_ref[:, 3 * d:4 * d] = (du0 * sg).astype(BF16)
        dproj_ref[:, 4 * d:5 * d] = (du0 * v_c * sg * (1.0 - sg)).astype(BF16)
        du1_ext[tm:tm + hc, :] = du1_ext[0:hc, :]

        @pl.when(i == nt - 1)
        def _():
            scatter_finish()

    rev = lambda i: (nt - 1 - i, 0)
    tile = pl.BlockSpec((tm, d), rev)
    whole = lambda shape: pl.BlockSpec(shape, lambda i: (0,) * len(shape))
    hblocks = tm // HALO_CONF
    halo = pl.BlockSpec((HALO_CONF, 7 * d), lambda i: (jnp.maximum((nt - 1 - i) * hblocks - 1, 0), 0))
    bf = jax.ShapeDtypeStruct((t, d), BF16)
    res = pl.pallas_call(
        body, name="backward_mix", grid=(nt,),
        out_shape=(jax.ShapeDtypeStruct((t, 7 * d), BF16), bf, bf, bf, bf, bf, bf,
                   jax.ShapeDtypeStruct((8, d), F32),
                   jax.ShapeDtypeStruct((SUBLANES * SHORT_K, d), F32),
                   jax.ShapeDtypeStruct((SUBLANES * CONF_K, d), F32))
        + tuple(jax.ShapeDtypeStruct(p.shape, p.dtype) for p in partials),
        in_specs=[tile, pl.BlockSpec((tm, 7 * d), rev), halo, tile, tile, tile, tile,
                  whole((8, d)), whole((8, d)), whole(taps.shape), whole((d, d)), whole((d, d)), whole((d, d))]
        + [ANY_SPEC] * na,
        out_specs=[pl.BlockSpec((tm, 7 * d), rev), tile, tile, tile, tile, tile, tile,
                   whole((8, d)), whole((SUBLANES * SHORT_K, d)), whole((SUBLANES * CONF_K, d))] + [ANY_SPEC] * na,
        scratch_shapes=[pltpu.VMEM((hs + tm, d), F32), pltpu.VMEM((hc + tm, d), F32),
                        pltpu.VMEM((tm + hs, d), F32), pltpu.VMEM((tm + hc, d), F32),
                        pltpu.VMEM((tm, d), F32)] + _comm_scratch(na),
        compiler_params=_params(("arbitrary",)),
    )(dx1, proj, proj, y_a, y_b, mix, u1, mod, prm, taps, w_so, w_co, w_o, *partials)
    return res[:10], res[10:]


def _backward_in(dproj, x, dx1, mod, prm, w_in_g, tm, partials):
    t, d = x.shape
    ns, _, n = w_in_g.shape
    na = len(partials)
    nt = t // tm

    def body(*refs):
        dproj_ref, x_ref, dx1_ref, mod_ref, prm_ref, w_hbm = refs[:6]
        gx_ref, sums_ref = refs[6 + na:8 + na]
        w_ref = refs[8 + 2 * na]
        scatter_start, scatter_finish = _scatter_plan(refs[6:6 + na], refs[8 + na:8 + 2 * na], *refs[9 + 2 * na:])

        @pl.when(pl.program_id(0) == 0)
        def _():
            scatter_start()
            pltpu.sync_copy(w_hbm, w_ref)
            sums_ref[...] = jnp.zeros((8, d), F32)

        dh = _dot_nt(dproj_ref[:, 0:n], w_ref[0])
        for j in range(1, ns):
            dh = dh + _dot_nt(dproj_ref[:, j * n:(j + 1) * n], w_ref[j])
        xv = x_ref[...]
        r1 = lax.rsqrt(_rowmean(xv * xv) + EPS)
        xn = xv * r1
        gmix = prm_ref[P_GMIX:P_GMIX + 1, :]
        scale = 1.0 + mod_ref[M_SC1:M_SC1 + 1, :]
        dxn = dh * gmix * scale
        gx_ref[...] = dx1_ref[...] + r1 * (dxn - xn * _rowmean(dxn * xn))
        hx = dh * xn
        sums_ref[0:1, :] = sums_ref[0:1, :] + _colsum(dh)
        sums_ref[1:2, :] = sums_ref[1:2, :] + _colsum(hx) * gmix
        sums_ref[2:3, :] = sums_ref[2:3, :] + _colsum(hx) * scale

        @pl.when(pl.program_id(0) == nt - 1)
        def _():
            scatter_finish()

    tile = pl.BlockSpec((tm, d), lambda i: (i, 0))
    whole = pl.BlockSpec((8, d), lambda i: (0, 0))
    res = pl.pallas_call(
        body, name="backward_in", grid=(nt,),
        out_shape=(jax.ShapeDtypeStruct((t, d), F32), jax.ShapeDtypeStruct((8, d), F32))
        + tuple(jax.ShapeDtypeStruct(p.shape, p.dtype) for p in partials),
        in_specs=[pl.BlockSpec((tm, ns * n), lambda i: (i, 0)), tile, tile, whole, whole, ANY_SPEC] + [ANY_SPEC] * na,
        out_specs=[tile, whole] + [ANY_SPEC] * na,
        scratch_shapes=[pltpu.VMEM(w_in_g.shape, BF16)] + _comm_scratch(na),
        compiler_params=_params(("arbitrary",)),
    )(dproj, x, dx1, mod, prm, w_in_g, *partials)
    return res[0], res[1], res[2:]


def _weight_grad(a, b, a_spec, b_spec, ns, m, n, nk, name, partials=()):
    na = len(partials)

    def body(*refs):
        a_ref, b_ref = refs[:2]
        o_ref = refs[2 + na]
        acc_ref = refs[3 + 2 * na]
        s, k = pl.program_id(0), pl.program_id(1)
        if na:
            scatter_start, scatter_finish = _scatter_plan(refs[2:2 + na], refs[3 + na:3 + 2 * na], *refs[4 + 2 * na:])

            @pl.when((s == 0) & (k == 0))
            def _():
                scatter_start()

        av = a_ref[0] if len(a_ref.shape) == 3 else a_ref[...]
        bv = b_ref[0] if len(b_ref.shape) == 3 else b_ref[...]
        part = _dot_tn(av, bv)

        @pl.when(k == 0)
        def _():
            acc_ref[...] = part

        @pl.when(k > 0)
        def _():
            acc_ref[...] = acc_ref[...] + part

        @pl.when(k == nk - 1)
        def _():
            o_ref[0] = acc_ref[...].astype(BF16)

        if na:
            @pl.when((s == ns - 1) & (k == nk - 1))
            def _():
                scatter_finish()

    res = pl.pallas_call(
        body, name=name, grid=(ns, nk),
        out_shape=(jax.ShapeDtypeStruct((ns, m, n), BF16),)
        + tuple(jax.ShapeDtypeStruct(p.shape, p.dtype) for p in partials),
        in_specs=[a_spec, b_spec] + [ANY_SPEC] * na,
        out_specs=[pl.BlockSpec((1, m, n), lambda s, k: (s, 0, 0))] + [ANY_SPEC] * na,
        scratch_shapes=[pltpu.VMEM((m, n), F32)] + (_comm_scratch(na) if na else []),
        compiler_params=_params(("arbitrary", "arbitrary")),
    )(a, b, *partials)
    return (res[0], res[1:]) if na else res[0]


def _adamw(w, g, m, v):
    m = ADAM_B1 * m + (1.0 - ADAM_B1) * g
    v = ADAM_B2 * v + (1.0 - ADAM_B2) * (g * g)
    m_hat = m / (1.0 - ADAM_B1 ** ADAM_STEP)
    v_hat = v / (1.0 - ADAM_B2 ** ADAM_STEP)
    delta = -ADAM_LR * (m_hat / (jnp.sqrt(v_hat) + ADAM_EPS) + ADAM_WD * w)
    return delta, m, v


def _adamw_shard(parts, w, m, v, tr, name):
    r, c = w.shape

    def body(p_ref, w_ref, m_ref, v_ref, g_ref, d_ref, nm_ref, nv_ref):
        g = p_ref[0].astype(F32)
        for s in range(1, N_DEV):
            g = g + p_ref[s].astype(F32)
        delta, nm, nv = _adamw(w_ref[...], g, m_ref[...], v_ref[...])
        g_ref[...] = g
        d_ref[...] = delta
        nm_ref[...] = nm
        nv_ref[...] = nv

    tile = pl.BlockSpec((tr, c), lambda i: (i, 0))
    return pl.pallas_call(
        body, name=name, grid=(r // tr,),
        out_shape=(jax.ShapeDtypeStruct((r, c), F32),) * 4,
        in_specs=[pl.BlockSpec((N_DEV, tr, c), lambda i: (0, i, 0)), tile, tile, tile],
        out_specs=[tile] * 4,
        compiler_params=_params(("arbitrary",)),
    )(parts, w, m, v)


def _ada_update(sc_all, dmod_cols, w, m, v, tr):
    d, n = w.shape

    def body(sc_ref, dm_ref, w_ref, m_ref, v_ref, g_ref, d_ref, nm_ref, nv_ref):
        g = lax.dot_general(sc_ref[...], dm_ref[...], (((0,), (0,)), ((), ())),
                            preferred_element_type=F32, precision=lax.Precision.HIGHEST)
        delta, nm, nv = _adamw(w_ref[...], g, m_ref[...], v_ref[...])
        g_ref[...] = g
        d_ref[...] = delta
        nm_ref[...] = nm
        nv_ref[...] = nv

    tile = pl.BlockSpec((tr, n), lambda i: (i, 0))
    return pl.pallas_call(
        body, name="ada_update", grid=(d // tr,),
        out_shape=(jax.ShapeDtypeStruct((d, n), F32),) * 4,
        in_specs=[pl.BlockSpec((N_DEV, tr), lambda i: (0, i)), pl.BlockSpec((N_DEV, n), lambda i: (0, 0)),
                  tile, tile, tile],
        out_specs=[tile] * 4,
        compiler_params=_params(("arbitrary",)),
    )(sc_all, dmod_cols, w, m, v)


def _small_update(vec, cg, smalls):
    l = vec.shape[2]
    rows = cg.shape[1]
    ns = len(smalls)

    def body(*refs):
        vec_ref, cg_ref = refs[0], refs[1]
        wmv = refs[2:2 + 3 * ns]
        vall_ref = refs[2 + 3 * ns]
        outs = refs[3 + 3 * ns:3 + 7 * ns]
        cgr_ref, send_sems, recv_sems = refs[3 + 7 * ns:]
        x, y, c = _my_coords()
        me = _slot(x, y, c)
        vall_ref[me] = vec_ref[0]
        cgr_ref[me] = cg_ref[me]
        copies = []
        for k in range(1, N_DEV):
            peer = (_flip(x, k & 4), _flip(y, k & 2), _flip(c, k & 1))
            copies.append(pltpu.make_async_remote_copy(
                src_ref=vall_ref.at[me], dst_ref=vall_ref.at[me], send_sem=send_sems.at[k - 1],
                recv_sem=recv_sems.at[k - 1], device_id=peer, device_id_type=MESH))
            copies.append(pltpu.make_async_remote_copy(
                src_ref=cg_ref.at[_slot(*peer)], dst_ref=cgr_ref.at[me], send_sem=send_sems.at[7 + k - 1],
                recv_sem=recv_sems.at[7 + k - 1], device_id=peer, device_id_type=MESH))
        for cp in copies:
            cp.start()
        for cp in copies:
            cp.wait_recv()
        for p, (_, _, _, lo, hi, kind) in enumerate(smalls):
            w_ref, m_ref, v_ref = wmv[3 * p:3 * p + 3]
            part = (lambda s: vall_ref[s, :, lo:hi]) if kind == "vec" else (lambda s: cgr_ref[s, lo:hi, :])
            g = part(0)
            for s in range(1, N_DEV):
                g = g + part(s)
            delta, nm, nv = _adamw(w_ref[...], g, m_ref[...], v_ref[...])
            for o_ref, val in zip(outs[4 * p:4 * p + 4], (g, delta, nm, nv)):
                o_ref[...] = val
        for cp in copies:
            cp.wait_send()

    vm = pl.BlockSpec(memory_space=pltpu.VMEM)
    args = [vec, cg]
    out_shape = [jax.ShapeDtypeStruct((N_DEV, 1, l), F32)]
    for w, m, v, _, _, _ in smalls:
        args += [w, m, v]
    for w, _, _, _, _, _ in smalls:
        out_shape += [jax.ShapeDtypeStruct(w.shape, F32)] * 4
    res = pl.pallas_call(
        body, name="small_update",
        out_shape=tuple(out_shape),
        in_specs=[vm] * len(args), out_specs=[vm] * len(out_shape),
        scratch_shapes=[pltpu.VMEM((N_DEV, rows, LANES), F32),
                        pltpu.SemaphoreType.DMA((14,)), pltpu.SemaphoreType.DMA((14,))],
        compiler_params=_params(),
    )(*args)
    return res[0], [res[1 + 4 * p:5 + 4 * p] for p in range(ns)]


def _pick(t, want):
    return want if t % want == 0 else t


def kernel(x, c, w_ada, b_ada, norm_mix_g, w_in, conv_short_w, w_short_out, conv_conf_w, conv_conf_b, conf_ln_g, conf_ln_b, w_conf_out, w_o, norm_ffn_g, w_ffn_in, w_ffn_out, final_norm_g, loss_target, m_w_ada, m_b_ada, m_norm_mix_g, m_w_in, m_conv_short_w, m_w_short_out, m_conv_conf_w, m_conv_conf_b, m_conf_ln_g, m_conf_ln_b, m_w_conf_out, m_w_o, m_norm_ffn_g, m_w_ffn_in, m_w_ffn_out, m_final_norm_g, v_w_ada, v_b_ada, v_norm_mix_g, v_w_in, v_conv_short_w, v_w_short_out, v_conv_conf_w, v_conv_conf_b, v_conf_ln_g, v_conf_ln_b, v_w_conf_out, v_w_o, v_norm_ffn_g, v_w_ffn_in, v_w_ffn_out, v_final_norm_g):
    t, d = x.shape[1], x.shape[2]
    x2 = x.reshape(t, d)
    tgt = loss_target.reshape(t, d)
    me = _slot(*_my_coords())
    tm = _pick(t, 256)
    tm_fwd_in = _pick(t, 1024)
    tk = _pick(t, 2048)

    taps_loc = jnp.zeros((40, LANES), F32)
    taps_loc = taps_loc.at[0:SHORT_K].set(conv_short_w[0]).at[8:8 + CONF_K].set(conv_conf_w[0])
    mod_flat, sc_all3, taps = _ada_forward(c, w_ada[0], b_ada, taps_loc)
    mod = jnp.concatenate([mod_flat.reshape(6, d), jnp.zeros((2, d), F32)], axis=0)
    prm = jnp.concatenate([norm_mix_g, norm_ffn_g, final_norm_g.reshape(1, d), conv_conf_b, conf_ln_g, conf_ln_b,
                           jnp.zeros((2, d), F32)], axis=0)

    n_in = w_in.shape[2]
    nk = t // tk
    tok = pl.BlockSpec((tk, d), lambda s, k: (k, 0))
    rows = d // N_DEV
    frows = w_ffn_out.shape[1]

    proj, h, w_in_g, (w_so_g, w_co_g, w_o_g) = _forward_in(
        jnp.reshape(me, (1,)).astype(jnp.int32), x2, mod, prm, w_in[0].astype(BF16), tm_fwd_in,
        [w_short_out[0].astype(BF16), w_conf_out[0].astype(BF16), w_o[0].astype(BF16)])
    w_so = w_so_g.reshape(d, d)
    w_co = w_co_g.reshape(d, d)
    w_oo = w_o_g.reshape(d, d)
    (x1, y_a, y_b, mix, u1), (w_fi_g, w_fo_g) = _forward_mix(
        x2, proj, mod, prm, taps, w_so, w_co, w_oo, tm, [w_ffn_in[0].astype(BF16), w_ffn_out[0].astype(BF16)])
    fb = w_fi_g.shape[2]
    w_fo4 = w_fo_g.reshape(N_DEV // 2, fb, d)
    dx2, ab, h2, sums_f = _forward_ffn(x1, tgt, mod, prm, w_fi_g, w_fo4, tm)

    dx1, df, act, dab, sums_b = _backward_ffn(dx2, x1, ab, mod, prm, w_fi_g, w_fo4, tm)
    g_fi = _weight_grad(h2, dab, tok, pl.BlockSpec((1, tk, fb), lambda s, k: (s, k, 0)), N_DEV, d, fb, nk, "grad_w_ffn_in")
    g_fo = _weight_grad(act, df, pl.BlockSpec((1, tk, fb), lambda s, k: (s, k, 0)), tok, N_DEV // 2, fb, d, nk, "grad_w_ffn_out")
    (dproj, dmix, dya, dyb, merged, ya_pre, u3, sums_m, dw3p, dw31p), (p_fi, p_fo) = _backward_mix(
        dx1, proj, y_a, y_b, mix, u1, mod, prm, taps, w_so, w_co, w_oo, tm,
        [g_fi, g_fo.reshape(N_DEV, frows, d)])
    g_so = _weight_grad(ya_pre, dya, tok, tok, 1, d, d, nk, "grad_w_short_out")
    g_co = _weight_grad(u3, dyb, tok, tok, 1, d, d, nk, "grad_w_conf_out")
    g_oo = _weight_grad(merged, dmix, tok, tok, 1, d, d, nk, "grad_w_o")
    g_in, (p_so, p_co, p_oo) = _weight_grad(
        h, dproj, tok, pl.BlockSpec((tk, n_in), lambda s, k: (k, s)), N_DEV, d, n_in, nk, "grad_w_in",
        [g_so.reshape(N_DEV, rows, d), g_co.reshape(N_DEV, rows, d), g_oo.reshape(N_DEV, rows, d)])
    grad_x, sums_i, (p_in,) = _backward_in(dproj, x2, dx1, mod, prm, w_in_g, tm, [g_in])

    up_in = _adamw_shard(p_in, w_in[0], m_w_in[0], v_w_in[0], _pick(d, 256), "adamw_w_in")
    up_so = _adamw_shard(p_so, w_short_out[0], m_w_short_out[0], v_w_short_out[0], rows, "adamw_w_short_out")
    up_co = _adamw_shard(p_co, w_conf_out[0], m_w_conf_out[0], v_w_conf_out[0], rows, "adamw_w_conf_out")
    up_oo = _adamw_shard(p_oo, w_o[0], m_w_o[0], v_w_o[0], rows, "adamw_w_o")
    up_fi = _adamw_shard(p_fi, w_ffn_in[0], m_w_ffn_in[0], v_w_ffn_in[0], _pick(d, 256), "adamw_w_ffn_in")
    up_fo = _adamw_shard(p_fo, w_ffn_out[0], m_w_ffn_out[0], v_w_ffn_out[0], frows, "adamw_w_ffn_out")

    vec = jnp.concatenate([sums_i[0:2], sums_m[0:1], sums_b[0:2], sums_f[1:2],
                           sums_i[2:3], sums_m[3:4], sums_m[1:3], sums_b[2:3], sums_f[0:1]], axis=0)
    vec = vec.reshape(1, 1, 12 * d)
    dw3 = dw3p.reshape(SHORT_K, SUBLANES, d).sum(axis=1)
    dw31 = dw31p.reshape(CONF_K, SUBLANES, d).sum(axis=1)
    cg = jnp.zeros((40, d), F32).at[0:SHORT_K].set(dw3).at[8:8 + CONF_K].set(dw31)
    cg = cg.reshape(40, N_DEV, LANES).transpose(1, 0, 2)
    fin = lambda a: a.reshape(1, d)
    tap = lambda a: a.reshape(a.shape[1:])
    smalls = [
        (b_ada, m_b_ada, v_b_ada, 0, 6 * d, "vec"),
        (norm_mix_g, m_norm_mix_g, v_norm_mix_g, 6 * d, 7 * d, "vec"),
        (tap(conv_short_w), tap(m_conv_short_w), tap(v_conv_short_w), 0, SHORT_K, "cg"),
        (tap(conv_conf_w), tap(m_conv_conf_w), tap(v_conv_conf_w), 8, 8 + CONF_K, "cg"),
        (conv_conf_b, m_conv_conf_b, v_conv_conf_b, 7 * d, 8 * d, "vec"),
        (conf_ln_g, m_conf_ln_g, v_conf_ln_g, 8 * d, 9 * d, "vec"),
        (conf_ln_b, m_conf_ln_b, v_conf_ln_b, 9 * d, 10 * d, "vec"),
        (norm_ffn_g, m_norm_ffn_g, v_norm_ffn_g, 10 * d, 11 * d, "vec"),
        (fin(final_norm_g), fin(m_final_norm_g), fin(v_final_norm_g), 11 * d, 12 * d, "vec"),
    ]
    vall, up_small = _small_update(vec, cg, smalls)
    n_ada = w_ada.shape[2]
    dmod_all = vall.reshape(N_DEV, 12 * d)[:, 0:6 * d]
    dmod_cols = lax.dynamic_slice(dmod_all, (0, me * n_ada), (N_DEV, n_ada))
    up_ada = _ada_update(sc_all3.reshape(N_DEV, d), dmod_cols, w_ada[0], m_w_ada[0], v_w_ada[0], _pick(d, 256))

    loss = lax.psum(jnp.sum(sums_f[2]), ("x", "y", "c"))

    lead = lambda a: a.reshape((1,) + a.shape)
    ups = [tuple(lead(a) for a in up_ada), up_small[0], up_small[1], tuple(lead(a) for a in up_in),
           tuple(lead(a) for a in up_small[2]), tuple(lead(a) for a in up_so), tuple(lead(a) for a in up_small[3]),
           up_small[4], up_small[5], up_small[6],
           tuple(lead(a) for a in up_co), tuple(lead(a) for a in up_oo), up_small[7],
           tuple(lead(a) for a in up_fi), tuple(lead(a) for a in up_fo),
           tuple(a.reshape(d) for a in up_small[8])]
    grads = [u[0] for u in ups]
    deltas = [u[1] for u in ups]
    new_m = [u[2] for u in ups]
    new_v = [u[3] for u in ups]
    return (loss, grad_x.reshape(1, t, d), *grads, *deltas, *new_m, *new_v)
```

```python
import functools

import jax
import jax.numpy as jnp
from jax import lax
from jax.experimental import pallas as pl
from jax.experimental.pallas import tpu as pltpu

F32 = jnp.float32
BF16 = jnp.bfloat16
MESH = pl.DeviceIdType.MESH

N_DEV = 8
EPS = 1e-6
LN_EPS = 1e-5
SHORT_K = 3
CONF_K = 31
ADAM_LR = 0.001
ADAM_B1 = 0.9
ADAM_B2 = 0.999
ADAM_EPS = 1e-08
ADAM_WD = 0.01
ADAM_STEP = 10

LANES = 128
SUBLANES = 8
CONV_ROWS = 64
HALO_SHORT = 8
HALO_CONF = 32
VMEM_LIMIT = 56 * 1024 * 1024

M_SH1, M_SC1, M_G1, M_SH2, M_SC2, M_G2 = range(6)
P_GMIX, P_GFFN, P_GFIN, P_CBIAS, P_LNG, P_LNB = range(6)


def _params(sem=None, **kw):
    return pltpu.CompilerParams(dimension_semantics=sem, vmem_limit_bytes=VMEM_LIMIT, **kw)


def _sigmoid(v):
    return jax.nn.sigmoid(v)


def _dot(a, b):
    return jnp.dot(a, b, preferred_element_type=F32)


def _dot_nt(a, b):
    return lax.dot_general(a, b, (((1,), (1,)), ((), ())), preferred_element_type=F32)


def _dot_tn(a, b):
    return lax.dot_general(a, b, (((0,), (0,)), ((), ())), preferred_element_type=F32)


def _colsum(v):
    return jnp.sum(v, axis=0, keepdims=True)


def _rowmean(v):
    return jnp.mean(v, axis=-1, keepdims=True)


def _my_coords():
    return lax.axis_index("x"), lax.axis_index("y"), lax.axis_index("c")


def _slot(px, py, pc):
    return 4 * px + 2 * py + pc


def _flip(v, bit):
    return 1 - v if bit else v


def _taps_by_residue(taps):
    by_res = {}
    for wi, off in taps:
        by_res.setdefault(off % SUBLANES, []).append((wi, off // SUBLANES))
    return sorted(by_res.items())


def _tap_conv(ext_ref, w_ref, taps, tm, extra, out_ref):
    d = out_ref.shape[1]
    rb = min(CONV_ROWS, tm)
    wrows = rb + extra
    groups = _taps_by_residue(taps)

    def block(i, carry):
        base = pl.multiple_of(i * rb, SUBLANES)
        for lc in range(d // LANES):
            ls = pl.ds(lc * LANES, LANES)
            win = ext_ref[pl.ds(base, wrows), ls]
            acc = None
            for r, lst in groups:
                sh = win if r == 0 else pltpu.roll(win, wrows - r, 0)
                for wi, q in lst:
                    term = w_ref[wi:wi + 1, ls] * sh[SUBLANES * q:SUBLANES * q + rb, :]
                    acc = term if acc is None else acc + term
            out_ref[pl.ds(base, rb), ls] = acc
        return carry

    lax.fori_loop(0, tm // rb, block, 0)


def _tap_wgrad(a_ref, ext_ref, taps, tm, extra, acc_ref):
    d = a_ref.shape[1]
    rb = min(CONV_ROWS, tm)
    wrows = rb + extra
    groups = _taps_by_residue(taps)

    def block(i, carry):
        base = pl.multiple_of(i * rb, SUBLANES)
        for lc in range(d // LANES):
            ls = pl.ds(lc * LANES, LANES)
            a_blk = a_ref[pl.ds(base, rb), ls]
            win = ext_ref[pl.ds(base, wrows), ls]
            for r, lst in groups:
                sh = win if r == 0 else pltpu.roll(win, wrows - r, 0)
                for wi, q in lst:
                    prod = a_blk * sh[SUBLANES * q:SUBLANES * q + rb, :]
                    part = prod[0:SUBLANES, :]
                    for s in range(1, rb // SUBLANES):
                        part = part + prod[SUBLANES * s:SUBLANES * (s + 1), :]
                    rows = pl.ds(SUBLANES * wi, SUBLANES)
                    acc_ref[rows, ls] = acc_ref[rows, ls] + part
        return carry

    lax.fori_loop(0, tm // rb, block, 0)


def _causal_taps(k, halo):
    return [(i, halo - (k - 1) + i) for i in range(k)]


def _anticausal_taps(k):
    return [(i, (k - 1) - i) for i in range(k)]


def _ada_forward(c, w_ada_loc, b_ada, taps_loc):
    d = c.shape[1]
    nloc = w_ada_loc.shape[1]
    trows = taps_loc.shape[0]

    def body(c_ref, w_ref, b_ref, t_ref, mod_ref, sc_ref, taps_ref,
             part_ref, modrecv_ref, tapsall_ref, send_sems, recv_sems):
        x, y, cc = _my_coords()
        me = _slot(x, y, cc)
        cv = c_ref[...]
        sc_ref[me] = cv * _sigmoid(cv)
        tapsall_ref[me] = t_ref[...]

        def peer_of(k):
            return (_flip(x, k & 4), _flip(y, k & 2), _flip(cc, k & 1))

        def gather_copy(ref, base, k):
            return pltpu.make_async_remote_copy(
                src_ref=ref.at[me], dst_ref=ref.at[me], send_sem=send_sems.at[base + k - 1],
                recv_sem=recv_sems.at[base + k - 1], device_id=peer_of(k), device_id_type=MESH)

        first = [gather_copy(sc_ref, 0, k) for k in range(1, N_DEV)]
        first += [gather_copy(tapsall_ref, 7, k) for k in range(1, N_DEV)]
        for cp in first:
            cp.start()
        for cp in first[:7]:
            cp.wait_recv()
        sc_all = jnp.concatenate([sc_ref[s] for s in range(N_DEV)], axis=0)
        part = jnp.dot(sc_all, w_ref[...], preferred_element_type=F32,
                       precision=lax.Precision.HIGHEST)
        for b in range(N_DEV):
            part_ref[b] = part[b:b + 1, :]
        modrecv_ref[me] = part_ref[me]
        second = []
        for k in range(1, N_DEV):
            px, py, pc = peer_of(k)
            second.append(pltpu.make_async_remote_copy(
                src_ref=part_ref.at[_slot(px, py, pc)], dst_ref=modrecv_ref.at[me],
                send_sem=send_sems.at[14 + k - 1], recv_sem=recv_sems.at[14 + k - 1],
                device_id=(px, py, pc), device_id_type=MESH))
        for cp in second:
            cp.start()
        for cp in second:
            cp.wait_recv()
        mod = jnp.concatenate([modrecv_ref[s] for s in range(N_DEV)], axis=1)
        mod_ref[...] = mod + b_ref[...]
        for cp in first[7:]:
            cp.wait_recv()
        taps_ref[...] = jnp.concatenate([tapsall_ref[s] for s in range(N_DEV)], axis=1)
        for cp in first + second:
            cp.wait_send()

    return pl.pallas_call(
        body, name="ada_forward",
        out_shape=(jax.ShapeDtypeStruct((1, N_DEV * nloc), F32),
                   jax.ShapeDtypeStruct((N_DEV, 1, d), F32),
                   jax.ShapeDtypeStruct((trows, N_DEV * LANES), F32)),
        in_specs=[pl.BlockSpec(memory_space=pltpu.VMEM)] * 4,
        out_specs=[pl.BlockSpec(memory_space=pltpu.VMEM)] * 3,
        scratch_shapes=[pltpu.VMEM((N_DEV, 1, nloc), F32), pltpu.VMEM((N_DEV, 1, nloc), F32),
                        pltpu.VMEM((N_DEV, trows, LANES), F32),
                        pltpu.SemaphoreType.DMA((21,)), pltpu.SemaphoreType.DMA((21,))],
        compiler_params=_params(),
    )(c, w_ada_loc, b_ada, taps_loc)


ANY_SPEC = pl.BlockSpec(memory_space=pl.ANY)


def _comm_scratch(na):
    return [pltpu.SemaphoreType.DMA((7 * na,)), pltpu.SemaphoreType.DMA((7 * na,)), pltpu.SemaphoreType.DMA((na,))]


def _leading_block(ref, slot):
    return ref.at[slot]


def _column_block(width):
    def view(ref, slot):
        return ref.at[:, pl.ds(pl.multiple_of(slot * width, LANES), width)]
    return view


class _Gather:
    def __init__(self, ins, outs, send_sems, recv_sems, local_sems, view=_leading_block):
        self.na = len(ins)
        x, y, c = _my_coords()
        self.c = c
        self.view = view
        self.me, self.sibling = (x, y, c), (x, y, 1 - c)
        self.chips = [(1 - x, y), (x, 1 - y), (1 - x, 1 - y)]
        self.outs, self.send_sems, self.recv_sems = outs, send_sems, recv_sems
        self.mine = [pltpu.make_async_copy(ins[a], view(outs[a], _slot(*self.me)), local_sems.at[a])
                     for a in range(self.na)]
        self.first = []
        for a in range(self.na):
            self.first.append(self._copy(a, 0, self.me, self.sibling, src=ins[a]))
            self.first += [self._copy(a, 1 + j, self.me, (*chip, c), src=ins[a]) for j, chip in enumerate(self.chips)]
        self.passed = [self._copy(a, 4 + j, (*chip, c), self.sibling)
                       for a in range(self.na) for j, chip in enumerate(self.chips)]

    def _copy(self, a, k, block, to, src=None):
        dst = self.view(self.outs[a], _slot(*block))
        return pltpu.make_async_remote_copy(
            src_ref=dst if src is None else src, dst_ref=dst,
            send_sem=self.send_sems.at[7 * a + k], recv_sem=self.recv_sems.at[7 * a + k],
            device_id=to, device_id_type=MESH)

    def start(self):
        for cp in self.mine + self.first:
            cp.start()

    def forward_chip(self, j):
        for a in range(self.na):
            self._copy(a, 1 + j, (*self.chips[j], self.c), self.me).wait_recv()
            self.passed[3 * a + j].start()

    def wait_sibling_own(self):
        for a in range(self.na):
            self._copy(a, 0, self.sibling, self.me).wait_recv()

    def wait_sibling_passed(self, j):
        for a in range(self.na):
            self._copy(a, 4 + j, (*self.chips[j], 1 - self.c), self.me).wait_recv()

    def finish_sends(self):
        for cp in self.first + self.passed:
            cp.wait_send()
        for cp in self.mine:
            cp.wait()

    def forward(self):
        for j in range(3):
            self.forward_chip(j)

    def finish(self):
        self.wait_sibling_own()
        for j in range(3):
            self.wait_sibling_passed(j)
        self.finish_sends()


def _scatter_plan(ins, outs, send_sems, recv_sems, local_sems, view=_leading_block):
    na = len(ins)
    x, y, c = _my_coords()
    me = _slot(x, y, c)
    mine = [pltpu.make_async_copy(view(ins[a], me), outs[a].at[me], local_sems.at[a]) for a in range(na)]
    copies = []
    for k in range(1, N_DEV):
        peer = (_flip(x, k & 4), _flip(y, k & 2), _flip(c, k & 1))
        for a in range(na):
            copies.append(pltpu.make_async_remote_copy(
                src_ref=view(ins[a], _slot(*peer)), dst_ref=outs[a].at[me],
                send_sem=send_sems.at[7 * a + k - 1], recv_sem=recv_sems.at[7 * a + k - 1],
                device_id=peer, device_id_type=MESH))

    def start():
        for cp in mine + copies:
            cp.start()

    def finish():
        for cp in copies:
            cp.wait_recv()
        for cp in copies:
            cp.wait_send()
        for cp in mine:
            cp.wait()

    return start, finish


def _forward_in(me_arr, x, mod, prm, w_in_loc, tm, shards):
    t, d = x.shape
    n = w_in_loc.shape[1]
    ns = N_DEV
    na = len(shards)
    nt = t // tm

    def body(*refs):
        x_ref, mod_ref, prm_ref, wloc_ref = refs[1:5]
        proj_ref, h_ref, wg_ref = refs[5 + na:8 + na]
        hall_ref, wv_ref, wv_sem = refs[8 + 2 * na:11 + 2 * na]
        sems = refs[11 + 2 * na:]
        s, i = pl.program_id(0), pl.program_id(1)
        columns = _column_block(n)
        g_in = _Gather([wloc_ref], [wg_ref], *sems[0:3], view=columns)
        g_rest = _Gather(refs[5:5 + na], refs[8 + na:8 + 2 * na], *sems[3:6])
        me = _slot(*_my_coords())

        def load_shard(src):
            cp = pltpu.make_async_copy(src, wv_ref, wv_sem)
            cp.start()
            cp.wait()

        @pl.when((s == 0) & (i == 0))
        def _():
            g_in.start()
            g_rest.start()
            load_shard(wloc_ref)

        arrivals = {1: g_in.wait_sibling_own,
                    2: functools.partial(g_in.forward_chip, 1), 3: functools.partial(g_in.wait_sibling_passed, 1),
                    4: functools.partial(g_in.forward_chip, 0), 5: functools.partial(g_in.wait_sibling_passed, 0),
                    6: functools.partial(g_in.forward_chip, 2), 7: functools.partial(g_in.wait_sibling_passed, 2)}
        for step, arrive in arrivals.items():
            @pl.when((s == step) & (i == 0))
            def _(step=step, arrive=arrive):
                arrive()
                if step == 6:
                    g_rest.forward()
                load_shard(columns(wg_ref, me ^ step))

        @pl.when(s == 0)
        def _():
            xv = x_ref[...]
            r = lax.rsqrt(_rowmean(xv * xv) + EPS)
            h = xv * r * prm_ref[P_GMIX:P_GMIX + 1, :] * (1.0 + mod_ref[M_SC1:M_SC1 + 1, :]) \
                + mod_ref[M_SH1:M_SH1 + 1, :]
            hb = h.astype(BF16)
            hall_ref[i] = hb
            h_ref[...] = hb

        proj_ref[...] = _dot(hall_ref[i], wv_ref[...]).astype(BF16)

        @pl.when((s == ns - 1) & (i == nt - 1))
        def _():
            g_in.finish_sends()
            g_rest.finish()

    x_tile = pl.BlockSpec((tm, d), lambda s, i, me: (jnp.where(s == 0, i, nt - 1), 0))
    whole = pl.BlockSpec((8, d), lambda s, i, me: (0, 0))
    res = pl.pallas_call(
        body, name="forward_in",
        grid_spec=pltpu.PrefetchScalarGridSpec(
            num_scalar_prefetch=1, grid=(ns, nt),
            in_specs=[x_tile, whole, whole, ANY_SPEC] + [ANY_SPEC] * na,
            out_specs=[pl.BlockSpec((tm, n), lambda s, i, me: (i, me[0] ^ s)), x_tile, ANY_SPEC] + [ANY_SPEC] * na,
            scratch_shapes=[pltpu.VMEM((nt, tm, d), BF16), pltpu.VMEM((d, n), BF16), pltpu.SemaphoreType.DMA]
            + _comm_scratch(1) + _comm_scratch(na)),
        out_shape=(jax.ShapeDtypeStruct((t, ns * n), BF16), jax.ShapeDtypeStruct((t, d), BF16),
                   jax.ShapeDtypeStruct((d, ns * n), BF16))
        + tuple(jax.ShapeDtypeStruct((N_DEV,) + a.shape, a.dtype) for a in shards),
        compiler_params=_params(("arbitrary", "arbitrary")),
    )(me_arr, x, mod, prm, w_in_loc, *shards)
    return res[0], res[1], res[2], res[3:]


def _forward_mix(x, proj, mod, prm, taps, w_so, w_co, w_o, tm, shards):
    t, d = x.shape
    nt = t // tm
    na = len(shards)
    hs, hc = HALO_SHORT, HALO_CONF
    r3, r31 = 0, 8

    def body(*refs):
        x_ref, proj_ref, mod_ref, prm_ref, taps_ref, wso_ref, wco_ref, wo_ref = refs[:8]
        x1_ref, ya_ref, yb_ref, mix_ref, u1_ref = refs[8 + na:13 + na]
        cv_ext, u0_ext, conv3_ref, u1f_ref = refs[13 + 2 * na:17 + 2 * na]
        gather = _Gather(refs[8:8 + na], refs[13 + na:13 + 2 * na], *refs[17 + 2 * na:])
        i = pl.program_id(0)

        @pl.when(i == 0)
        def _():
            gather.start()
            cv_ext[0:hs, :] = jnp.zeros((hs, d), F32)
            u0_ext[0:hc, :] = jnp.zeros((hc, d), F32)

        @pl.when(i == (3 * nt) // 4)
        def _():
            gather.forward()

        def col(g):
            return proj_ref[:, g * d:(g + 1) * d].astype(F32)

        cv_ext[hs:hs + tm, :] = col(1) * col(2)
        u0_ext[hc:hc + tm, :] = col(3) * _sigmoid(col(4))
        _tap_conv(cv_ext, taps_ref, [(r3 + wi, off) for wi, off in _causal_taps(SHORT_K, hs)], tm, hs, conv3_ref)
        _tap_conv(u0_ext, taps_ref, [(r31 + wi, off) for wi, off in _causal_taps(CONF_K, hc)], tm, hc, u1f_ref)
        cv_ext[0:hs, :] = cv_ext[tm:tm + hs, :]
        u0_ext[0:hc, :] = u0_ext[tm:tm + hc, :]

        ya_pre = (col(0) * conv3_ref[...]).astype(BF16)
        y_a = _dot(ya_pre, wso_ref[...])
        u1 = u1f_ref[...] + prm_ref[P_CBIAS:P_CBIAS + 1, :]
        u1_ref[...] = u1.astype(BF16)
        u1 = u1_ref[...].astype(F32)
        mu = _rowmean(u1)
        uc = u1 - mu
        rstd = lax.rsqrt(_rowmean(uc * uc) + LN_EPS)
        u2 = uc * rstd * prm_ref[P_LNG:P_LNG + 1, :] + prm_ref[P_LNB:P_LNB + 1, :]
        u3 = (u2 * _sigmoid(u2)).astype(BF16)
        y_b = _dot(u3, wco_ref[...])
        ya_ref[...] = y_a.astype(BF16)
        yb_ref[...] = y_b.astype(BF16)
        merged = _sigmoid(col(5)) * ya_ref[...].astype(F32) + _sigmoid(col(6)) * yb_ref[...].astype(F32)
        mix = _dot(merged.astype(BF16), wo_ref[...])
        mix_ref[...] = mix.astype(BF16)
        x1_ref[...] = x_ref[...] + mod_ref[M_G1:M_G1 + 1, :] * mix

        @pl.when(i == nt - 1)
        def _():
            gather.finish()

    tile = pl.BlockSpec((tm, d), lambda i: (i, 0))
    whole = lambda shape: pl.BlockSpec(shape, lambda i: (0,) * len(shape))
    res = pl.pallas_call(
        body, name="forward_mix", grid=(nt,),
        out_shape=(jax.ShapeDtypeStruct((t, d), F32),) + (jax.ShapeDtypeStruct((t, d), BF16),) * 4
        + tuple(jax.ShapeDtypeStruct((N_DEV,) + a.shape, a.dtype) for a in shards),
        in_specs=[tile, pl.BlockSpec((tm, 7 * d), lambda i: (i, 0)), whole((8, d)), whole((8, d)),
                  whole(taps.shape), whole((d, d)), whole((d, d)), whole((d, d))] + [ANY_SPEC] * na,
        out_specs=[tile] * 5 + [ANY_SPEC] * na,
        scratch_shapes=[pltpu.VMEM((hs + tm, d), F32), pltpu.VMEM((hc + tm, d), F32),
                        pltpu.VMEM((tm, d), F32), pltpu.VMEM((tm, d), F32)] + _comm_scratch(na),
        compiler_params=_params(("arbitrary",)),
    )(x, proj, mod, prm, taps, w_so, w_co, w_o, *shards)
    return res[:5], res[5:]


def _forward_ffn(x1, tgt, mod, prm, w_fi_g, w_fo_g, tm):
    t, d = x1.shape
    ns, fb, _ = w_fi_g.shape
    nh = ns // 2

    def body(x1_ref, tgt_ref, mod_ref, prm_ref, wfi_hbm, wfo_hbm,
             dx2_ref, ab_ref, h2_ref, sums_ref, wfi_ref, wfo_ref):
        i = pl.program_id(0)

        @pl.when(i == 0)
        def _():
            pltpu.sync_copy(wfi_hbm, wfi_ref)
            pltpu.sync_copy(wfo_hbm, wfo_ref)
            sums_ref[...] = jnp.zeros((8, d), F32)

        x1v = x1_ref[...]
        r2 = lax.rsqrt(_rowmean(x1v * x1v) + EPS)
        h2 = (x1v * r2 * prm_ref[P_GFFN:P_GFFN + 1, :] * (1.0 + mod_ref[M_SC2:M_SC2 + 1, :])
              + mod_ref[M_SH2:M_SH2 + 1, :]).astype(BF16)
        h2_ref[...] = h2
        f = jnp.zeros((tm, d), F32)
        for j in range(nh):
            ab_ref[j] = _dot_nt(h2, wfi_ref[j]).astype(BF16)
            ab_ref[j + nh] = _dot_nt(h2, wfi_ref[j + nh]).astype(BF16)
            a = ab_ref[j].astype(F32)
            act = (a * _sigmoid(a) * ab_ref[j + nh].astype(F32)).astype(BF16)
            f = f + _dot(act, wfo_ref[j])
        x2 = x1v + mod_ref[M_G2:M_G2 + 1, :] * f
        r3 = lax.rsqrt(_rowmean(x2 * x2) + EPS)
        xn3 = x2 * r3
        gfin = prm_ref[P_GFIN:P_GFIN + 1, :]
        err = xn3 * gfin - tgt_ref[...]
        dy = err * (1.0 / d)
        dxn3 = dy * gfin
        dx2 = r3 * (dxn3 - xn3 * _rowmean(dxn3 * xn3))
        dx2_ref[...] = dx2
        sums_ref[0:1, :] = sums_ref[0:1, :] + _colsum(dy * xn3)
        sums_ref[1:2, :] = sums_ref[1:2, :] + _colsum(dx2 * f)
        sums_ref[2:3, :] = sums_ref[2:3, :] + _colsum(err * err) * (0.5 / d)

    tile = pl.BlockSpec((tm, d), lambda i: (i, 0))
    whole = lambda shape: pl.BlockSpec(shape, lambda i: (0,) * len(shape))
    any_spec = pl.BlockSpec(memory_space=pl.ANY)
    return pl.pallas_call(
        body, name="forward_ffn", grid=(t // tm,),
        out_shape=(jax.ShapeDtypeStruct((t, d), F32), jax.ShapeDtypeStruct((ns, t, fb), BF16),
                   jax.ShapeDtypeStruct((t, d), BF16), jax.ShapeDtypeStruct((8, d), F32)),
        in_specs=[tile, tile, whole((8, d)), whole((8, d)), any_spec, any_spec],
        out_specs=[tile, pl.BlockSpec((ns, tm, fb), lambda i: (0, i, 0)), tile, whole((8, d))],
        scratch_shapes=[pltpu.VMEM(w_fi_g.shape, BF16), pltpu.VMEM(w_fo_g.shape, BF16)],
        compiler_params=_params(("arbitrary",)),
    )(x1, tgt, mod, prm, w_fi_g, w_fo_g)


def _backward_ffn(dx2, x1, ab, mod, prm, w_fi_g, w_fo_g, tm):
    t, d = x1.shape
    ns, fb, _ = w_fi_g.shape
    nh = ns // 2

    def body(dx2_ref, x1_ref, ab_ref, mod_ref, prm_ref, wfi_hbm, wfo_hbm,
             dx1_ref, df_ref, act_ref, dab_ref, sums_ref, wfi_ref, wfo_ref):
        i = pl.program_id(0)

        @pl.when(i == 0)
        def _():
            pltpu.sync_copy(wfi_hbm, wfi_ref)
            pltpu.sync_copy(wfo_hbm, wfo_ref)
            sums_ref[...] = jnp.zeros((8, d), F32)

        dx2v = dx2_ref[...]
        df = (mod_ref[M_G2:M_G2 + 1, :] * dx2v).astype(BF16)
        df_ref[...] = df
        dh2 = jnp.zeros((tm, d), F32)
        for j in range(nh):
            dact = _dot_nt(df, wfo_ref[j])
            a = ab_ref[j].astype(F32)
            b = ab_ref[j + nh].astype(F32)
            s = _sigmoid(a)
            sil = a * s
            act_ref[j] = (sil * b).astype(BF16)
            da = (dact * b * (s * (1.0 + a * (1.0 - s)))).astype(BF16)
            db = (dact * sil).astype(BF16)
            dab_ref[j] = da
            dab_ref[j + nh] = db
            dh2 = dh2 + _dot(da, wfi_ref[j]) + _dot(db, wfi_ref[j + nh])
        x1v = x1_ref[...]
        r2 = lax.rsqrt(_rowmean(x1v * x1v) + EPS)
        xn2 = x1v * r2
        gffn = prm_ref[P_GFFN:P_GFFN + 1, :]
        scale = 1.0 + mod_ref[M_SC2:M_SC2 + 1, :]
        dxn2 = dh2 * gffn * scale
        dx1_ref[...] = dx2v + r2 * (dxn2 - xn2 * _rowmean(dxn2 * xn2))
        hx = dh2 * xn2
        sums_ref[0:1, :] = sums_ref[0:1, :] + _colsum(dh2)
        sums_ref[1:2, :] = sums_ref[1:2, :] + _colsum(hx) * gffn
        sums_ref[2:3, :] = sums_ref[2:3, :] + _colsum(hx) * scale

    tile = pl.BlockSpec((tm, d), lambda i: (i, 0))
    whole = lambda shape: pl.BlockSpec(shape, lambda i: (0,) * len(shape))
    any_spec = pl.BlockSpec(memory_space=pl.ANY)
    return pl.pallas_call(
        body, name="backward_ffn", grid=(t // tm,),
        out_shape=(jax.ShapeDtypeStruct((t, d), F32), jax.ShapeDtypeStruct((t, d), BF16),
                   jax.ShapeDtypeStruct((nh, t, fb), BF16), jax.ShapeDtypeStruct((ns, t, fb), BF16),
                   jax.ShapeDtypeStruct((8, d), F32)),
        in_specs=[tile, tile, pl.BlockSpec((ns, tm, fb), lambda i: (0, i, 0)), whole((8, d)), whole((8, d)),
                  any_spec, any_spec],
        out_specs=[tile, tile, pl.BlockSpec((nh, tm, fb), lambda i: (0, i, 0)),
                   pl.BlockSpec((ns, tm, fb), lambda i: (0, i, 0)), whole((8, d))],
        scratch_shapes=[pltpu.VMEM(w_fi_g.shape, BF16), pltpu.VMEM(w_fo_g.shape, BF16)],
        compiler_params=_params(("arbitrary",)),
    )(dx2, x1, ab, mod, prm, w_fi_g, w_fo_g)


def _backward_mix(dx1, proj, y_a, y_b, mix, u1, mod, prm, taps, w_so, w_co, w_o, tm, partials):
    t, d = dx1.shape
    nt = t // tm
    na = len(partials)
    hs, hc = HALO_SHORT, HALO_CONF
    r3, r31 = 0, 8

    def body(*refs):
        (dx1_ref, proj_ref, halo_ref, ya_ref, yb_ref, mix_ref, u1_ref, mod_ref, prm_ref, taps_ref,
         wso_ref, wco_ref, wo_ref) = refs[:13]
        (dproj_ref, dmix_ref, dya_ref, dyb_ref, merged_ref, yapre_ref, u3_ref, sums_ref, dw3_ref,
         dw31_ref) = refs[13 + na:23 + na]
        cv_ext, u0_ext, d3_ext, du1_ext, tmp_ref = refs[23 + 2 * na:28 + 2 * na]
        scatter_start, scatter_finish = _scatter_plan(refs[13:13 + na], refs[23 + na:23 + 2 * na], *refs[28 + 2 * na:])
        i = pl.program_id(0)
        first_tile = i == nt - 1

        @pl.when(i == 0)
        def _():
            scatter_start()
            sums_ref[...] = jnp.zeros((8, d), F32)
            dw3_ref[...] = jnp.zeros(dw3_ref.shape, F32)
            dw31_ref[...] = jnp.zeros(dw31_ref.shape, F32)
            d3_ext[tm:tm + hs, :] = jnp.zeros((hs, d), F32)
            du1_ext[tm:tm + hc, :] = jnp.zeros((hc, d), F32)

        def col(g):
            return proj_ref[:, g * d:(g + 1) * d].astype(F32)

        def hcol(g, rows):
            v = halo_ref[HALO_CONF - rows:HALO_CONF, g * d:(g + 1) * d].astype(F32)
            return jnp.where(first_tile, 0.0, v)

        dx1v = dx1_ref[...]
        mixv = mix_ref[...].astype(F32)
        dmix = (mod_ref[M_G1:M_G1 + 1, :] * dx1v).astype(BF16)
        dmix_ref[...] = dmix
        sums_ref[0:1, :] = sums_ref[0:1, :] + _colsum(dx1v * mixv)
        dmerged = _dot_nt(dmix, wo_ref[...])
        ga = _sigmoid(col(5))
        gb = _sigmoid(col(6))
        yav = ya_ref[...].astype(F32)
        ybv = yb_ref[...].astype(F32)
        dya = (dmerged * ga).astype(BF16)
        dyb = (dmerged * gb).astype(BF16)
        dya_ref[...] = dya
        dyb_ref[...] = dyb
        dproj_ref[:, 5 * d:6 * d] = (dmerged * yav * ga * (1.0 - ga)).astype(BF16)
        dproj_ref[:, 6 * d:7 * d] = (dmerged * ybv * gb * (1.0 - gb)).astype(BF16)
        merged_ref[...] = (ga * yav + gb * ybv).astype(BF16)

        dya_pre = _dot_nt(dya, wso_ref[...])
        c_s, v_s, b_s = col(1), col(2), col(0)
        cv_ext[0:hs, :] = hcol(1, hs) * hcol(2, hs)
        cv_ext[hs:hs + tm, :] = c_s * v_s
        _tap_conv(cv_ext, taps_ref, [(r3 + wi, off) for wi, off in _causal_taps(SHORT_K, hs)], tm, hs, tmp_ref)
        conv3 = tmp_ref[...]
        yapre_ref[...] = (b_s * conv3).astype(BF16)
        dproj_ref[:, 0:d] = (dya_pre * conv3).astype(BF16)
        d3_ext[0:tm, :] = dya_pre * b_s
        _tap_wgrad(d3_ext, cv_ext, _causal_taps(SHORT_K, hs), tm, hs, dw3_ref)
        _tap_conv(d3_ext, taps_ref, [(r3 + wi, off) for wi, off in _anticausal_taps(SHORT_K)], tm, hs, tmp_ref)
        dcv = tmp_ref[...]
        dproj_ref[:, d:2 * d] = (dcv * v_s).astype(BF16)
        dproj_ref[:, 2 * d:3 * d] = (dcv * c_s).astype(BF16)
        d3_ext[tm:tm + hs, :] = d3_ext[0:hs, :]

        du3 = _dot_nt(dyb, wco_ref[...])
        u1v = u1_ref[...].astype(F32)
        mu = _rowmean(u1v)
        uc = u1v - mu
        rstd = lax.rsqrt(_rowmean(uc * uc) + LN_EPS)
        uhat = uc * rstd
        lng = prm_ref[P_LNG:P_LNG + 1, :]
        u2 = uhat * lng + prm_ref[P_LNB:P_LNB + 1, :]
        s2 = _sigmoid(u2)
        u3_ref[...] = (u2 * s2).astype(BF16)
        du2 = du3 * (s2 * (1.0 + u2 * (1.0 - s2)))
        sums_ref[1:2, :] = sums_ref[1:2, :] + _colsum(du2 * uhat)
        sums_ref[2:3, :] = sums_ref[2:3, :] + _colsum(du2)
        duhat = du2 * lng
        du1 = rstd * (duhat - _rowmean(duhat) - uhat * _rowmean(duhat * uhat))
        sums_ref[3:4, :] = sums_ref[3:4, :] + _colsum(du1)
        du1_ext[0:tm, :] = du1
        v_c = col(3)
        sg = _sigmoid(col(4))
        u0_ext[0:hc, :] = hcol(3, hc) * _sigmoid(hcol(4, hc))
        u0_ext[hc:hc + tm, :] = v_c * sg
        _tap_wgrad(du1_ext, u0_ext, _causal_taps(CONF_K, hc), tm, hc, dw31_ref)
        _tap_conv(du1_ext, taps_ref, [(r31 + wi, off) for wi, off in _anticausal_taps(CONF_K)], tm, hc, tmp_ref)
        du0 = tmp_ref[...]
        dproj_ref[:, 3 * d:4 * d] = (du0 * sg).astype(BF16)
        dproj_ref[:, 4 * d:5 * d] = (du0 * v_c * sg * (1.0 - sg)).astype(BF16)
        du1_ext[tm:tm + hc, :] = du1_ext[0:hc, :]

        @pl.when(i == nt - 1)
        def _():
            scatter_finish()

    rev = lambda i: (nt - 1 - i, 0)
    tile = pl.BlockSpec((tm, d), rev)
    whole = lambda shape: pl.BlockSpec(shape, lambda i: (0,) * len(shape))
    hblocks = tm // HALO_CONF
    halo = pl.BlockSpec((HALO_CONF, 7 * d), lambda i: (jnp.maximum((nt - 1 - i) * hblocks - 1, 0), 0))
    bf = jax.ShapeDtypeStruct((t, d), BF16)
    res = pl.pallas_call(
        body, name="backward_mix", grid=(nt,),
        out_shape=(jax.ShapeDtypeStruct((t, 7 * d), BF16), bf, bf, bf, bf, bf, bf,
                   jax.ShapeDtypeStruct((8, d), F32),
                   jax.ShapeDtypeStruct((SUBLANES * SHORT_K, d), F32),
                   jax.ShapeDtypeStruct((SUBLANES * CONF_K, d), F32))
        + tuple(jax.ShapeDtypeStruct(p.shape, p.dtype) for p in partials),
        in_specs=[tile, pl.BlockSpec((tm, 7 * d), rev), halo, tile, tile, tile, tile,
                  whole((8, d)), whole((8, d)), whole(taps.shape), whole((d, d)), whole((d, d)), whole((d, d))]
        + [ANY_SPEC] * na,
        out_specs=[pl.BlockSpec((tm, 7 * d), rev), tile, tile, tile, tile, tile, tile,
                   whole((8, d)), whole((SUBLANES * SHORT_K, d)), whole((SUBLANES * CONF_K, d))] + [ANY_SPEC] * na,
        scratch_shapes=[pltpu.VMEM((hs + tm, d), F32), pltpu.VMEM((hc + tm, d), F32),
                        pltpu.VMEM((tm + hs, d), F32), pltpu.VMEM((tm + hc, d), F32),
                        pltpu.VMEM((tm, d), F32)] + _comm_scratch(na),
        compiler_params=_params(("arbitrary",)),
    )(dx1, proj, proj, y_a, y_b, mix, u1, mod, prm, taps, w_so, w_co, w_o, *partials)
    return res[:10], res[10:]


def _backward_in(dproj, x, dx1, mod, prm, w_in_g, tm, partials, partial_view, exchanged_shapes):
    t, d = x.shape
    n_all = w_in_g.shape[1]
    na = len(partials)
    nt = t // tm

    def body(*refs):
        dproj_ref, x_ref, dx1_ref, mod_ref, prm_ref, w_hbm = refs[:6]
        gx_ref, sums_ref = refs[6 + na:8 + na]
        w_ref = refs[8 + 2 * na]
        scatter_start, scatter_finish = _scatter_plan(refs[6:6 + na], refs[8 + na:8 + 2 * na], *refs[9 + 2 * na:],
                                                      view=partial_view)

        @pl.when(pl.program_id(0) == 0)
        def _():
            scatter_start()
            pltpu.sync_copy(w_hbm, w_ref)
            sums_ref[...] = jnp.zeros((8, d), F32)

        dh = _dot_nt(dproj_ref[...], w_ref[...])
        xv = x_ref[...]
        r1 = lax.rsqrt(_rowmean(xv * xv) + EPS)
        xn = xv * r1
        gmix = prm_ref[P_GMIX:P_GMIX + 1, :]
        scale = 1.0 + mod_ref[M_SC1:M_SC1 + 1, :]
        dxn = dh * gmix * scale
        gx_ref[...] = dx1_ref[...] + r1 * (dxn - xn * _rowmean(dxn * xn))
        hx = dh * xn
        sums_ref[0:1, :] = sums_ref[0:1, :] + _colsum(dh)
        sums_ref[1:2, :] = sums_ref[1:2, :] + _colsum(hx) * gmix
        sums_ref[2:3, :] = sums_ref[2:3, :] + _colsum(hx) * scale

        @pl.when(pl.program_id(0) == nt - 1)
        def _():
            scatter_finish()

    tile = pl.BlockSpec((tm, d), lambda i: (i, 0))
    whole = pl.BlockSpec((8, d), lambda i: (0, 0))
    res = pl.pallas_call(
        body, name="backward_in", grid=(nt,),
        out_shape=(jax.ShapeDtypeStruct((t, d), F32), jax.ShapeDtypeStruct((8, d), F32))
        + tuple(jax.ShapeDtypeStruct(s, p.dtype) for p, s in zip(partials, exchanged_shapes)),
        in_specs=[pl.BlockSpec((tm, n_all), lambda i: (i, 0)), tile, tile, whole, whole, ANY_SPEC] + [ANY_SPEC] * na,
        out_specs=[tile, whole] + [ANY_SPEC] * na,
        scratch_shapes=[pltpu.VMEM(w_in_g.shape, BF16)] + _comm_scratch(na),
        compiler_params=_params(("arbitrary",)),
    )(dproj, x, dx1, mod, prm, w_in_g, *partials)
    return res[0], res[1], res[2:]


def _weight_grad(a, b, a_spec, b_spec, ns, m, n, nk, name, partials=(), by_columns=False):
    na = len(partials)

    def body(*refs):
        a_ref, b_ref = refs[:2]
        o_ref = refs[2 + na]
        acc_ref = refs[3 + 2 * na]
        s, k = pl.program_id(0), pl.program_id(1)
        if na:
            scatter_start, scatter_finish = _scatter_plan(refs[2:2 + na], refs[3 + na:3 + 2 * na], *refs[4 + 2 * na:])

            @pl.when((s == 0) & (k == 0))
            def _():
                scatter_start()

        av = a_ref[0] if len(a_ref.shape) == 3 else a_ref[...]
        bv = b_ref[0] if len(b_ref.shape) == 3 else b_ref[...]
        part = _dot_tn(av, bv)

        @pl.when(k == 0)
        def _():
            acc_ref[...] = part

        @pl.when(k > 0)
        def _():
            acc_ref[...] = acc_ref[...] + part

        @pl.when(k == nk - 1)
        def _():
            if by_columns:
                o_ref[...] = acc_ref[...].astype(BF16)
            else:
                o_ref[0] = acc_ref[...].astype(BF16)

        if na:
            @pl.when((s == ns - 1) & (k == nk - 1))
            def _():
                scatter_finish()

    if by_columns:
        out_shape, out_spec = (m, ns * n), pl.BlockSpec((m, n), lambda s, k: (0, s))
    else:
        out_shape, out_spec = (ns, m, n), pl.BlockSpec((1, m, n), lambda s, k: (s, 0, 0))
    res = pl.pallas_call(
        body, name=name, grid=(ns, nk),
        out_shape=(jax.ShapeDtypeStruct(out_shape, BF16),)
        + tuple(jax.ShapeDtypeStruct(p.shape, p.dtype) for p in partials),
        in_specs=[a_spec, b_spec] + [ANY_SPEC] * na,
        out_specs=[out_spec] + [ANY_SPEC] * na,
        scratch_shapes=[pltpu.VMEM((m, n), F32)] + (_comm_scratch(na) if na else []),
        compiler_params=_params(("arbitrary", "arbitrary")),
    )(a, b, *partials)
    return (res[0], res[1:]) if na else res[0]


def _adamw(w, g, m, v):
    m = ADAM_B1 * m + (1.0 - ADAM_B1) * g
    v = ADAM_B2 * v + (1.0 - ADAM_B2) * (g * g)
    m_hat = m / (1.0 - ADAM_B1 ** ADAM_STEP)
    v_hat = v / (1.0 - ADAM_B2 ** ADAM_STEP)
    delta = -ADAM_LR * (m_hat / (jnp.sqrt(v_hat) + ADAM_EPS) + ADAM_WD * w)
    return delta, m, v


def _adamw_shard(parts, w, m, v, tr, name):
    r, c = w.shape

    def body(p_ref, w_ref, m_ref, v_ref, g_ref, d_ref, nm_ref, nv_ref):
        g = p_ref[0].astype(F32)
        for s in range(1, N_DEV):
            g = g + p_ref[s].astype(F32)
        delta, nm, nv = _adamw(w_ref[...], g, m_ref[...], v_ref[...])
        g_ref[...] = g
        d_ref[...] = delta
        nm_ref[...] = nm
        nv_ref[...] = nv

    tile = pl.BlockSpec((tr, c), lambda i: (i, 0))
    return pl.pallas_call(
        body, name=name, grid=(r // tr,),
        out_shape=(jax.ShapeDtypeStruct((r, c), F32),) * 4,
        in_specs=[pl.BlockSpec((N_DEV, tr, c), lambda i: (0, i, 0)), tile, tile, tile],
        out_specs=[tile] * 4,
        compiler_params=_params(("arbitrary",)),
    )(parts, w, m, v)


def _ada_update(sc_all, dmod_cols, w, m, v, tr):
    d, n = w.shape

    def body(sc_ref, dm_ref, w_ref, m_ref, v_ref, g_ref, d_ref, nm_ref, nv_ref):
        g = lax.dot_general(sc_ref[...], dm_ref[...], (((0,), (0,)), ((), ())),
                            preferred_element_type=F32, precision=lax.Precision.HIGHEST)
        delta, nm, nv = _adamw(w_ref[...], g, m_ref[...], v_ref[...])
        g_ref[...] = g
        d_ref[...] = delta
        nm_ref[...] = nm
        nv_ref[...] = nv

    tile = pl.BlockSpec((tr, n), lambda i: (i, 0))
    return pl.pallas_call(
        body, name="ada_update", grid=(d // tr,),
        out_shape=(jax.ShapeDtypeStruct((d, n), F32),) * 4,
        in_specs=[pl.BlockSpec((N_DEV, tr), lambda i: (0, i)), pl.BlockSpec((N_DEV, n), lambda i: (0, 0)),
                  tile, tile, tile],
        out_specs=[tile] * 4,
        compiler_params=_params(("arbitrary",)),
    )(sc_all, dmod_cols, w, m, v)


def _small_exchange(vec, cg):
    l = vec.shape[2]
    rows = cg.shape[1]

    def body(vec_ref, cg_ref, vall_ref, cgr_ref, send_sems, recv_sems):
        x, y, c = _my_coords()
        me = _slot(x, y, c)
        vall_ref[me] = vec_ref[0]
        cgr_ref[me] = cg_ref[me]
        copies = []
        for k in range(1, N_DEV):
            peer = (_flip(x, k & 4), _flip(y, k & 2), _flip(c, k & 1))
            copies.append(pltpu.make_async_remote_copy(
                src_ref=vall_ref.at[me], dst_ref=vall_ref.at[me], send_sem=send_sems.at[k - 1],
                recv_sem=recv_sems.at[k - 1], device_id=peer, device_id_type=MESH))
            copies.append(pltpu.make_async_remote_copy(
                src_ref=cg_ref.at[_slot(*peer)], dst_ref=cgr_ref.at[me], send_sem=send_sems.at[7 + k - 1],
                recv_sem=recv_sems.at[7 + k - 1], device_id=peer, device_id_type=MESH))
        for cp in copies:
            cp.start()
        for cp in copies:
            cp.wait_recv()
        for cp in copies:
            cp.wait_send()

    vm = pl.BlockSpec(memory_space=pltpu.VMEM)
    return pl.pallas_call(
        body, name="small_exchange",
        out_shape=(jax.ShapeDtypeStruct((N_DEV, 1, l), F32), jax.ShapeDtypeStruct((N_DEV, rows, LANES), F32)),
        in_specs=[vm, vm], out_specs=[vm, vm],
        scratch_shapes=[pltpu.SemaphoreType.DMA((14,)), pltpu.SemaphoreType.DMA((14,))],
        compiler_params=_params(),
    )(vec, cg)


def _small_update(vall, cgr, smalls):
    ns = len(smalls)

    def body(*refs):
        vall_ref, cgr_ref = refs[0], refs[1]
        wmv = refs[2:2 + 3 * ns]
        outs = refs[2 + 3 * ns:]
        for p, (_, _, _, lo, hi, kind) in enumerate(smalls):
            w_ref, m_ref, v_ref = wmv[3 * p:3 * p + 3]
            part = (lambda s: vall_ref[s, :, lo:hi]) if kind == "vec" else (lambda s: cgr_ref[s, lo:hi, :])
            g = part(0)
            for s in range(1, N_DEV):
                g = g + part(s)
            delta, nm, nv = _adamw(w_ref[...], g, m_ref[...], v_ref[...])
            for o_ref, val in zip(outs[4 * p:4 * p + 4], (g, delta, nm, nv)):
                o_ref[...] = val

    vm = pl.BlockSpec(memory_space=pltpu.VMEM)
    args = [vall, cgr]
    out_shape = []
    for w, m, v, _, _, _ in smalls:
        args += [w, m, v]
        out_shape += [jax.ShapeDtypeStruct(w.shape, F32)] * 4
    res = pl.pallas_call(
        body, name="small_update",
        out_shape=tuple(out_shape),
        in_specs=[vm] * len(args), out_specs=[vm] * len(out_shape),
        compiler_params=_params(),
    )(*args)
    return [res[4 * p:4 + 4 * p] for p in range(ns)]


def _pick(t, want):
    return want if t % want == 0 else t


def kernel(x, c, w_ada, b_ada, norm_mix_g, w_in, conv_short_w, w_short_out, conv_conf_w, conv_conf_b, conf_ln_g, conf_ln_b, w_conf_out, w_o, norm_ffn_g, w_ffn_in, w_ffn_out, final_norm_g, loss_target, m_w_ada, m_b_ada, m_norm_mix_g, m_w_in, m_conv_short_w, m_w_short_out, m_conv_conf_w, m_conv_conf_b, m_conf_ln_g, m_conf_ln_b, m_w_conf_out, m_w_o, m_norm_ffn_g, m_w_ffn_in, m_w_ffn_out, m_final_norm_g, v_w_ada, v_b_ada, v_norm_mix_g, v_w_in, v_conv_short_w, v_w_short_out, v_conv_conf_w, v_conv_conf_b, v_conf_ln_g, v_conf_ln_b, v_w_conf_out, v_w_o, v_norm_ffn_g, v_w_ffn_in, v_w_ffn_out, v_final_norm_g):
    t, d = x.shape[1], x.shape[2]
    x2 = x.reshape(t, d)
    tgt = loss_target.reshape(t, d)
    me = _slot(*_my_coords())
    tm = _pick(t, 256)
    tm_fwd_in = _pick(t, 1024)
    tk = _pick(t, 2048)

    taps_loc = jnp.zeros((40, LANES), F32)
    taps_loc = taps_loc.at[0:SHORT_K].set(conv_short_w[0]).at[8:8 + CONF_K].set(conv_conf_w[0])
    mod_flat, sc_all3, taps = _ada_forward(c, w_ada[0], b_ada, taps_loc)
    mod = jnp.concatenate([mod_flat.reshape(6, d), jnp.zeros((2, d), F32)], axis=0)
    prm = jnp.concatenate([norm_mix_g, norm_ffn_g, final_norm_g.reshape(1, d), conv_conf_b, conf_ln_g, conf_ln_b,
                           jnp.zeros((2, d), F32)], axis=0)

    n_in = w_in.shape[2]
    nk = t // tk
    tok = pl.BlockSpec((tk, d), lambda s, k: (k, 0))
    rows = d // N_DEV
    frows = w_ffn_out.shape[1]

    proj, h, w_in_g, (w_so_g, w_co_g, w_o_g) = _forward_in(
        jnp.reshape(me, (1,)).astype(jnp.int32), x2, mod, prm, w_in[0].astype(BF16), tm_fwd_in,
        [w_short_out[0].astype(BF16), w_conf_out[0].astype(BF16), w_o[0].astype(BF16)])
    w_so = w_so_g.reshape(d, d)
    w_co = w_co_g.reshape(d, d)
    w_oo = w_o_g.reshape(d, d)
    w_fi_t, m_fi_t, v_fi_t = w_ffn_in[0].T, m_w_ffn_in[0].T, v_w_ffn_in[0].T
    (x1, y_a, y_b, mix, u1), (w_fi_g, w_fo_g) = _forward_mix(
        x2, proj, mod, prm, taps, w_so, w_co, w_oo, tm, [w_fi_t.astype(BF16), w_ffn_out[0].astype(BF16)])
    fb = w_fi_g.shape[1]
    w_fo4 = w_fo_g.reshape(N_DEV // 2, fb, d)
    dx2, ab, h2, sums_f = _forward_ffn(x1, tgt, mod, prm, w_fi_g, w_fo4, tm)

    dx1, df, act, dab, sums_b = _backward_ffn(dx2, x1, ab, mod, prm, w_fi_g, w_fo4, tm)
    shard_tok = pl.BlockSpec((1, tk, fb), lambda s, k: (s, k, 0))
    g_fi = _weight_grad(dab, h2, shard_tok, tok, N_DEV, fb, d, nk, "grad_w_ffn_in")
    g_fo = _weight_grad(act, df, shard_tok, tok, N_DEV // 2, fb, d, nk, "grad_w_ffn_out")
    (dproj, dmix, dya, dyb, merged, ya_pre, u3, sums_m, dw3p, dw31p), (p_fi, p_fo) = _backward_mix(
        dx1, proj, y_a, y_b, mix, u1, mod, prm, taps, w_so, w_co, w_oo, tm,
        [g_fi, g_fo.reshape(N_DEV, frows, d)])
    g_so = _weight_grad(ya_pre, dya, tok, tok, 1, d, d, nk, "grad_w_short_out")
    g_co = _weight_grad(u3, dyb, tok, tok, 1, d, d, nk, "grad_w_conf_out")
    g_oo = _weight_grad(merged, dmix, tok, tok, 1, d, d, nk, "grad_w_o")
    n_blk = _pick(N_DEV * n_in, d)
    g_in, (p_so, p_co, p_oo) = _weight_grad(
        h, dproj, tok, pl.BlockSpec((tk, n_blk), lambda s, k: (k, s)), N_DEV * n_in // n_blk, d, n_blk, nk, "grad_w_in",
        [g_so.reshape(N_DEV, rows, d), g_co.reshape(N_DEV, rows, d), g_oo.reshape(N_DEV, rows, d)], by_columns=True)
    grad_x, sums_i, (p_in,) = _backward_in(dproj, x2, dx1, mod, prm, w_in_g, tm, [g_in], _column_block(n_in),
                                           [(N_DEV, d, n_in)])

    up_in = _adamw_shard(p_in, w_in[0], m_w_in[0], v_w_in[0], _pick(d, 256), "adamw_w_in")
    up_so = _adamw_shard(p_so, w_short_out[0], m_w_short_out[0], v_w_short_out[0], rows, "adamw_w_short_out")
    up_co = _adamw_shard(p_co, w_conf_out[0], m_w_conf_out[0], v_w_conf_out[0], rows, "adamw_w_conf_out")
    up_oo = _adamw_shard(p_oo, w_o[0], m_w_o[0], v_w_o[0], rows, "adamw_w_o")
    up_fi = tuple(a.T for a in _adamw_shard(p_fi, w_fi_t, m_fi_t, v_fi_t, fb // 2, "adamw_w_ffn_in"))
    up_fo = _adamw_shard(p_fo, w_ffn_out[0], m_w_ffn_out[0], v_w_ffn_out[0], frows, "adamw_w_ffn_out")

    vec = jnp.concatenate([sums_i[0:2], sums_m[0:1], sums_b[0:2], sums_f[1:2],
                           sums_i[2:3], sums_m[3:4], sums_m[1:3], sums_b[2:3], sums_f[0:1]], axis=0)
    vec = vec.reshape(1, 1, 12 * d)
    dw3 = dw3p.reshape(SHORT_K, SUBLANES, d).sum(axis=1)
    dw31 = dw31p.reshape(CONF_K, SUBLANES, d).sum(axis=1)
    cg = jnp.zeros((40, d), F32).at[0:SHORT_K].set(dw3).at[8:8 + CONF_K].set(dw31)
    cg = cg.reshape(40, N_DEV, LANES).transpose(1, 0, 2)
    fin = lambda a: a.reshape(1, d)
    tap = lambda a: a.reshape(a.shape[1:])
    smalls = [
        (b_ada, m_b_ada, v_b_ada, 0, 6 * d, "vec"),
        (norm_mix_g, m_norm_mix_g, v_norm_mix_g, 6 * d, 7 * d, "vec"),
        (tap(conv_short_w), tap(m_conv_short_w), tap(v_conv_short_w), 0, SHORT_K, "cg"),
        (tap(conv_conf_w), tap(m_conv_conf_w), tap(v_conv_conf_w), 8, 8 + CONF_K, "cg"),
        (conv_conf_b, m_conv_conf_b, v_conv_conf_b, 7 * d, 8 * d, "vec"),
        (conf_ln_g, m_conf_ln_g, v_conf_ln_g, 8 * d, 9 * d, "vec"),
        (conf_ln_b, m_conf_ln_b, v_conf_ln_b, 9 * d, 10 * d, "vec"),
        (norm_ffn_g, m_norm_ffn_g, v_norm_ffn_g, 10 * d, 11 * d, "vec"),
        (fin(final_norm_g), fin(m_final_norm_g), fin(v_final_norm_g), 11 * d, 12 * d, "vec"),
    ]
    vall, cgr = _small_exchange(vec, cg)
    up_small = _small_update(vall, cgr, smalls)
    n_ada = w_ada.shape[2]
    dmod_all = vall.reshape(N_DEV, 12 * d)[:, 0:6 * d]
    dmod_cols = lax.dynamic_slice(dmod_all, (0, me * n_ada), (N_DEV, n_ada))
    up_ada = _ada_update(sc_all3.reshape(N_DEV, d), dmod_cols, w_ada[0], m_w_ada[0], v_w_ada[0], _pick(d, 256))

    loss = lax.psum(jnp.sum(sums_f[2]), ("x", "y", "c"))

    lead = lambda a: a.reshape((1,) + a.shape)
    ups = [tuple(lead(a) for a in up_ada), up_small[0], up_small[1], tuple(lead(a) for a in up_in),
           tuple(lead(a) for a in up_small[2]), tuple(lead(a) for a in up_so), tuple(lead(a) for a in up_small[3]),
           up_small[4], up_small[5], up_small[6],
           tuple(lead(a) for a in up_co), tuple(lead(a) for a in up_oo), up_small[7],
           tuple(lead(a) for a in up_fi), tuple(lead(a) for a in up_fo),
           tuple(a.reshape(d) for a in up_small[8])]
    grads = [u[0] for u in ups]
    deltas = [u[1] for u in ups]
    new_m = [u[2] for u in ups]
    new_v = [u[3] for u in ups]
    return (loss, grad_x.reshape(1, t, d), *grads, *deltas, *new_m, *new_v)
```

```python
import functools

import jax
import jax.numpy as jnp
from jax import lax
from jax.experimental import pallas as pl
from jax.experimental.pallas import tpu as pltpu

F32 = jnp.float32
BF16 = jnp.bfloat16
MESH = pl.DeviceIdType.MESH

N_DEV = 8
EPS = 1e-6
LN_EPS = 1e-5
SHORT_K = 3
CONF_K = 31
ADAM_LR = 0.001
ADAM_B1 = 0.9
ADAM_B2 = 0.999
ADAM_EPS = 1e-08
ADAM_WD = 0.01
ADAM_STEP = 10

LANES = 128
SUBLANES = 8
CONV_ROWS = 64
HALO_SHORT = 8
HALO_CONF = 32
VMEM_LIMIT = 56 * 1024 * 1024

M_SH1, M_SC1, M_G1, M_SH2, M_SC2, M_G2 = range(6)
P_GMIX, P_GFFN, P_GFIN, P_CBIAS, P_LNG, P_LNB = range(6)


def _params(sem=None, **kw):
    return pltpu.CompilerParams(dimension_semantics=sem, vmem_limit_bytes=VMEM_LIMIT, **kw)


def _sigmoid(v):
    return jax.nn.sigmoid(v)


def _dot(a, b):
    return jnp.dot(a, b, preferred_element_type=F32)


def _dot_nt(a, b):
    return lax.dot_general(a, b, (((1,), (1,)), ((), ())), preferred_element_type=F32)


def _dot_tn(a, b):
    return lax.dot_general(a, b, (((0,), (0,)), ((), ())), preferred_element_type=F32)


def _colsum(v):
    return jnp.sum(v, axis=0, keepdims=True)


def _rowmean(v):
    return jnp.mean(v, axis=-1, keepdims=True)


def _my_coords():
    return lax.axis_index("x"), lax.axis_index("y"), lax.axis_index("c")


def _slot(px, py, pc):
    return 4 * px + 2 * py + pc


def _flip(v, bit):
    return 1 - v if bit else v


def _taps_by_residue(taps):
    by_res = {}
    for wi, off in taps:
        by_res.setdefault(off % SUBLANES, []).append((wi, off // SUBLANES))
    return sorted(by_res.items())


def _tap_conv(ext_ref, w_ref, taps, tm, extra, out_ref):
    d = out_ref.shape[1]
    rb = min(CONV_ROWS, tm)
    wrows = rb + extra
    groups = _taps_by_residue(taps)

    def block(i, carry):
        base = pl.multiple_of(i * rb, SUBLANES)
        for lc in range(d // LANES):
            ls = pl.ds(lc * LANES, LANES)
            win = ext_ref[pl.ds(base, wrows), ls]
            acc = None
            for r, lst in groups:
                sh = win if r == 0 else pltpu.roll(win, wrows - r, 0)
                for wi, q in lst:
                    term = w_ref[wi:wi + 1, ls] * sh[SUBLANES * q:SUBLANES * q + rb, :]
                    acc = term if acc is None else acc + term
            out_ref[pl.ds(base, rb), ls] = acc
        return carry

    lax.fori_loop(0, tm // rb, block, 0)


def _tap_wgrad(a_ref, ext_ref, taps, tm, extra, acc_ref):
    d = a_ref.shape[1]
    rb = min(CONV_ROWS, tm)
    wrows = rb + extra
    groups = _taps_by_residue(taps)

    def block(i, carry):
        base = pl.multiple_of(i * rb, SUBLANES)
        for lc in range(d // LANES):
            ls = pl.ds(lc * LANES, LANES)
            a_blk = a_ref[pl.ds(base, rb), ls]
            win = ext_ref[pl.ds(base, wrows), ls]
            for r, lst in groups:
                sh = win if r == 0 else pltpu.roll(win, wrows - r, 0)
                for wi, q in lst:
                    prod = a_blk * sh[SUBLANES * q:SUBLANES * q + rb, :]
                    part = prod[0:SUBLANES, :]
                    for s in range(1, rb // SUBLANES):
                        part = part + prod[SUBLANES * s:SUBLANES * (s + 1), :]
                    rows = pl.ds(SUBLANES * wi, SUBLANES)
                    acc_ref[rows, ls] = acc_ref[rows, ls] + part
        return carry

    lax.fori_loop(0, tm // rb, block, 0)


def _causal_taps(k, halo):
    return [(i, halo - (k - 1) + i) for i in range(k)]


def _anticausal_taps(k):
    return [(i, (k - 1) - i) for i in range(k)]


def _ada_forward(c, w_ada_loc, b_ada, taps_loc):
    d = c.shape[1]
    nloc = w_ada_loc.shape[1]
    trows = taps_loc.shape[0]

    def body(c_ref, w_ref, b_ref, t_ref, mod_ref, sc_ref, taps_ref,
             part_ref, modrecv_ref, tapsall_ref, send_sems, recv_sems):
        x, y, cc = _my_coords()
        me = _slot(x, y, cc)
        cv = c_ref[...]
        sc_ref[me] = cv * _sigmoid(cv)
        tapsall_ref[me] = t_ref[...]

        def peer_of(k):
            return (_flip(x, k & 4), _flip(y, k & 2), _flip(cc, k & 1))

        def gather_copy(ref, base, k):
            return pltpu.make_async_remote_copy(
                src_ref=ref.at[me], dst_ref=ref.at[me], send_sem=send_sems.at[base + k - 1],
                recv_sem=recv_sems.at[base + k - 1], device_id=peer_of(k), device_id_type=MESH)

        first = [gather_copy(sc_ref, 0, k) for k in range(1, N_DEV)]
        first += [gather_copy(tapsall_ref, 7, k) for k in range(1, N_DEV)]
        for cp in first:
            cp.start()
        for cp in first[:7]:
            cp.wait_recv()
        sc_all = jnp.concatenate([sc_ref[s] for s in range(N_DEV)], axis=0)
        part = jnp.dot(sc_all, w_ref[...], preferred_element_type=F32,
                       precision=lax.Precision.HIGHEST)
        for b in range(N_DEV):
            part_ref[b] = part[b:b + 1, :]
        modrecv_ref[me] = part_ref[me]
        second = []
        for k in range(1, N_DEV):
            px, py, pc = peer_of(k)
            second.append(pltpu.make_async_remote_copy(
                src_ref=part_ref.at[_slot(px, py, pc)], dst_ref=modrecv_ref.at[me],
                send_sem=send_sems.at[14 + k - 1], recv_sem=recv_sems.at[14 + k - 1],
                device_id=(px, py, pc), device_id_type=MESH))
        for cp in second:
            cp.start()
        for cp in second:
            cp.wait_recv()
        mod = jnp.concatenate([modrecv_ref[s] for s in range(N_DEV)], axis=1)
        mod_ref[...] = mod + b_ref[...]
        for cp in first[7:]:
            cp.wait_recv()
        taps_ref[...] = jnp.concatenate([tapsall_ref[s] for s in range(N_DEV)], axis=1)
        for cp in first + second:
            cp.wait_send()

    return pl.pallas_call(
        body, name="ada_forward",
        out_shape=(jax.ShapeDtypeStruct((1, N_DEV * nloc), F32),
                   jax.ShapeDtypeStruct((N_DEV, 1, d), F32),
                   jax.ShapeDtypeStruct((trows, N_DEV * LANES), F32)),
        in_specs=[pl.BlockSpec(memory_space=pltpu.VMEM)] * 4,
        out_specs=[pl.BlockSpec(memory_space=pltpu.VMEM)] * 3,
        scratch_shapes=[pltpu.VMEM((N_DEV, 1, nloc), F32), pltpu.VMEM((N_DEV, 1, nloc), F32),
                        pltpu.VMEM((N_DEV, trows, LANES), F32),
                        pltpu.SemaphoreType.DMA((21,)), pltpu.SemaphoreType.DMA((21,))],
        compiler_params=_params(),
    )(c, w_ada_loc, b_ada, taps_loc)


ANY_SPEC = pl.BlockSpec(memory_space=pl.ANY)


def _comm_scratch(na):
    return [pltpu.SemaphoreType.DMA((7 * na,)), pltpu.SemaphoreType.DMA((7 * na,)), pltpu.SemaphoreType.DMA((na,))]


def _leading_block(ref, slot):
    return ref.at[slot]


def _column_block(width):
    def view(ref, slot):
        return ref.at[:, pl.ds(pl.multiple_of(slot * width, LANES), width)]
    return view


class _Gather:
    def __init__(self, ins, outs, send_sems, recv_sems, local_sems, view=_leading_block):
        self.na = len(ins)
        x, y, c = _my_coords()
        self.c = c
        self.view = view
        self.me, self.sibling = (x, y, c), (x, y, 1 - c)
        self.chips = [(1 - x, y), (x, 1 - y), (1 - x, 1 - y)]
        self.outs, self.send_sems, self.recv_sems = outs, send_sems, recv_sems
        self.mine = [pltpu.make_async_copy(ins[a], view(outs[a], _slot(*self.me)), local_sems.at[a])
                     for a in range(self.na)]
        self.first = []
        for a in range(self.na):
            self.first.append(self._copy(a, 0, self.me, self.sibling, src=ins[a]))
            self.first += [self._copy(a, 1 + j, self.me, (*chip, c), src=ins[a]) for j, chip in enumerate(self.chips)]
        self.passed = [self._copy(a, 4 + j, (*chip, c), self.sibling)
                       for a in range(self.na) for j, chip in enumerate(self.chips)]

    def _copy(self, a, k, block, to, src=None):
        dst = self.view(self.outs[a], _slot(*block))
        return pltpu.make_async_remote_copy(
            src_ref=dst if src is None else src, dst_ref=dst,
            send_sem=self.send_sems.at[7 * a + k], recv_sem=self.recv_sems.at[7 * a + k],
            device_id=to, device_id_type=MESH)

    def start(self):
        for cp in self.mine + self.first:
            cp.start()

    def forward_chip(self, j):
        for a in range(self.na):
            self._copy(a, 1 + j, (*self.chips[j], self.c), self.me).wait_recv()
            self.passed[3 * a + j].start()

    def wait_sibling_own(self):
        for a in range(self.na):
            self._copy(a, 0, self.sibling, self.me).wait_recv()

    def wait_sibling_passed(self, j):
        for a in range(self.na):
            self._copy(a, 4 + j, (*self.chips[j], 1 - self.c), self.me).wait_recv()

    def finish_sends(self):
        for cp in self.first + self.passed:
            cp.wait_send()
        for cp in self.mine:
            cp.wait()

    def forward(self):
        for j in range(3):
            self.forward_chip(j)

    def finish(self):
        self.wait_sibling_own()
        for j in range(3):
            self.wait_sibling_passed(j)
        self.finish_sends()


def _scatter_plan(ins, outs, send_sems, recv_sems, local_sems, view=_leading_block):
    na = len(ins)
    x, y, c = _my_coords()
    me = _slot(x, y, c)
    mine = [pltpu.make_async_copy(view(ins[a], me), outs[a].at[me], local_sems.at[a]) for a in range(na)]
    copies = []
    for k in range(1, N_DEV):
        peer = (_flip(x, k & 4), _flip(y, k & 2), _flip(c, k & 1))
        for a in range(na):
            copies.append(pltpu.make_async_remote_copy(
                src_ref=view(ins[a], _slot(*peer)), dst_ref=outs[a].at[me],
                send_sem=send_sems.at[7 * a + k - 1], recv_sem=recv_sems.at[7 * a + k - 1],
                device_id=peer, device_id_type=MESH))

    def start():
        for cp in mine + copies:
            cp.start()

    def finish():
        for cp in copies:
            cp.wait_recv()
        for cp in copies:
            cp.wait_send()
        for cp in mine:
            cp.wait()

    return start, finish


def _forward_in(me_arr, x, mod, prm, w_in_loc, tm, shards):
    t, d = x.shape
    n = w_in_loc.shape[1]
    ns = N_DEV
    na = len(shards)
    nt = t // tm

    def body(*refs):
        x_ref, mod_ref, prm_ref, wloc_ref = refs[1:5]
        proj_ref, h_ref, wg_ref = refs[5 + na:8 + na]
        hall_ref, wv_ref, wv_sem = refs[8 + 2 * na:11 + 2 * na]
        sems = refs[11 + 2 * na:]
        s, i = pl.program_id(0), pl.program_id(1)
        columns = _column_block(n)
        g_in = _Gather([wloc_ref], [wg_ref], *sems[0:3], view=columns)
        g_rest = _Gather(refs[5:5 + na], refs[8 + na:8 + 2 * na], *sems[3:6])
        me = _slot(*_my_coords())

        def load_shard(src):
            cp = pltpu.make_async_copy(src, wv_ref, wv_sem)
            cp.start()
            cp.wait()

        @pl.when((s == 0) & (i == 0))
        def _():
            g_in.start()
            g_rest.start()
            load_shard(wloc_ref)

        arrivals = {1: g_in.wait_sibling_own,
                    2: functools.partial(g_in.forward_chip, 1), 3: functools.partial(g_in.wait_sibling_passed, 1),
                    4: functools.partial(g_in.forward_chip, 0), 5: functools.partial(g_in.wait_sibling_passed, 0),
                    6: functools.partial(g_in.forward_chip, 2), 7: functools.partial(g_in.wait_sibling_passed, 2)}
        for step, arrive in arrivals.items():
            @pl.when((s == step) & (i == 0))
            def _(step=step, arrive=arrive):
                arrive()
                if step == 6:
                    g_rest.forward()
                load_shard(columns(wg_ref, me ^ step))

        @pl.when(s == 0)
        def _():
            xv = x_ref[...]
            r = lax.rsqrt(_rowmean(xv * xv) + EPS)
            h = xv * r * prm_ref[P_GMIX:P_GMIX + 1, :] * (1.0 + mod_ref[M_SC1:M_SC1 + 1, :]) \
                + mod_ref[M_SH1:M_SH1 + 1, :]
            hb = h.astype(BF16)
            hall_ref[i] = hb
            h_ref[...] = hb

        proj_ref[...] = _dot(hall_ref[i], wv_ref[...]).astype(BF16)

        @pl.when((s == ns - 1) & (i == nt - 1))
        def _():
            g_in.finish_sends()
            g_rest.finish()

    x_tile = pl.BlockSpec((tm, d), lambda s, i, me: (jnp.where(s == 0, i, nt - 1), 0))
    whole = pl.BlockSpec((8, d), lambda s, i, me: (0, 0))
    res = pl.pallas_call(
        body, name="forward_in",
        grid_spec=pltpu.PrefetchScalarGridSpec(
            num_scalar_prefetch=1, grid=(ns, nt),
            in_specs=[x_tile, whole, whole, ANY_SPEC] + [ANY_SPEC] * na,
            out_specs=[pl.BlockSpec((tm, n), lambda s, i, me: (i, me[0] ^ s)), x_tile, ANY_SPEC] + [ANY_SPEC] * na,
            scratch_shapes=[pltpu.VMEM((nt, tm, d), BF16), pltpu.VMEM((d, n), BF16), pltpu.SemaphoreType.DMA]
            + _comm_scratch(1) + _comm_scratch(na)),
        out_shape=(jax.ShapeDtypeStruct((t, ns * n), BF16), jax.ShapeDtypeStruct((t, d), BF16),
                   jax.ShapeDtypeStruct((d, ns * n), BF16))
        + tuple(jax.ShapeDtypeStruct((N_DEV,) + a.shape, a.dtype) for a in shards),
        compiler_params=_params(("arbitrary", "arbitrary")),
    )(me_arr, x, mod, prm, w_in_loc, *shards)
    return res[0], res[1], res[2], res[3:]


def _forward_mix(x, proj, mod, prm, taps, w_so, w_co, w_o, tm, shards):
    t, d = x.shape
    nt = t // tm
    na = len(shards)
    hs, hc = HALO_SHORT, HALO_CONF
    r3, r31 = 0, 8

    def body(*refs):
        x_ref, proj_ref, mod_ref, prm_ref, taps_ref, wso_ref, wco_ref, wo_ref = refs[:8]
        x1_ref, ya_ref, yb_ref, mix_ref, u1_ref = refs[8 + na:13 + na]
        cv_ext, u0_ext, conv3_ref, u1f_ref = refs[13 + 2 * na:17 + 2 * na]
        gather = _Gather(refs[8:8 + na], refs[13 + na:13 + 2 * na], *refs[17 + 2 * na:])
        i = pl.program_id(0)

        @pl.when(i == 0)
        def _():
            gather.start()
            cv_ext[0:hs, :] = jnp.zeros((hs, d), F32)
            u0_ext[0:hc, :] = jnp.zeros((hc, d), F32)

        @pl.when(i == (3 * nt) // 4)
        def _():
            gather.forward()

        def col(g):
            return proj_ref[:, g * d:(g + 1) * d].astype(F32)

        cv_ext[hs:hs + tm, :] = col(1) * col(2)
        u0_ext[hc:hc + tm, :] = col(3) * _sigmoid(col(4))
        _tap_conv(cv_ext, taps_ref, [(r3 + wi, off) for wi, off in _causal_taps(SHORT_K, hs)], tm, hs, conv3_ref)
        _tap_conv(u0_ext, taps_ref, [(r31 + wi, off) for wi, off in _causal_taps(CONF_K, hc)], tm, hc, u1f_ref)
        cv_ext[0:hs, :] = cv_ext[tm:tm + hs, :]
        u0_ext[0:hc, :] = u0_ext[tm:tm + hc, :]

        ya_pre = (col(0) * conv3_ref[...]).astype(BF16)
        y_a = _dot(ya_pre, wso_ref[...])
        u1 = u1f_ref[...] + prm_ref[P_CBIAS:P_CBIAS + 1, :]
        u1_ref[...] = u1.astype(BF16)
        u1 = u1_ref[...].astype(F32)
        mu = _rowmean(u1)
        uc = u1 - mu
        rstd = lax.rsqrt(_rowmean(uc * uc) + LN_EPS)
        u2 = uc * rstd * prm_ref[P_LNG:P_LNG + 1, :] + prm_ref[P_LNB:P_LNB + 1, :]
        u3 = (u2 * _sigmoid(u2)).astype(BF16)
        y_b = _dot(u3, wco_ref[...])
        ya_ref[...] = y_a.astype(BF16)
        yb_ref[...] = y_b.astype(BF16)
        merged = _sigmoid(col(5)) * ya_ref[...].astype(F32) + _sigmoid(col(6)) * yb_ref[...].astype(F32)
        mix = _dot(merged.astype(BF16), wo_ref[...])
        mix_ref[...] = mix.astype(BF16)
        x1_ref[...] = x_ref[...] + mod_ref[M_G1:M_G1 + 1, :] * mix

        @pl.when(i == nt - 1)
        def _():
            gather.finish()

    tile = pl.BlockSpec((tm, d), lambda i: (i, 0))
    whole = lambda shape: pl.BlockSpec(shape, lambda i: (0,) * len(shape))
    res = pl.pallas_call(
        body, name="forward_mix", grid=(nt,),
        out_shape=(jax.ShapeDtypeStruct((t, d), F32),) + (jax.ShapeDtypeStruct((t, d), BF16),) * 4
        + tuple(jax.ShapeDtypeStruct((N_DEV,) + a.shape, a.dtype) for a in shards),
        in_specs=[tile, pl.BlockSpec((tm, 7 * d), lambda i: (i, 0)), whole((8, d)), whole((8, d)),
                  whole(taps.shape), whole((d, d)), whole((d, d)), whole((d, d))] + [ANY_SPEC] * na,
        out_specs=[tile] * 5 + [ANY_SPEC] * na,
        scratch_shapes=[pltpu.VMEM((hs + tm, d), F32), pltpu.VMEM((hc + tm, d), F32),
                        pltpu.VMEM((tm, d), F32), pltpu.VMEM((tm, d), F32)] + _comm_scratch(na),
        compiler_params=_params(("arbitrary",)),
    )(x, proj, mod, prm, taps, w_so, w_co, w_o, *shards)
    return res[:5], res[5:]


def _ffn_chunks(ff):
    mxu = 2 * LANES
    cut = (ff // mxu + 1) // 2 * mxu
    return [(0, cut), (cut, ff)] if 0 < cut < ff and ff % mxu == 0 else [(0, ff)]


def _forward_ffn(x1, tgt, mod, prm, w_fi_t, w_fo, tm):
    t, d = x1.shape
    ff = w_fo.shape[0]

    def body(x1_ref, tgt_ref, mod_ref, prm_ref, wfi_hbm, wfo_hbm,
             dx2_ref, ab_ref, h2_ref, sums_ref, wfi_ref, wfo_ref):
        i = pl.program_id(0)

        @pl.when(i == 0)
        def _():
            pltpu.sync_copy(wfi_hbm, wfi_ref)
            pltpu.sync_copy(wfo_hbm, wfo_ref)
            sums_ref[...] = jnp.zeros((8, d), F32)

        x1v = x1_ref[...]
        r2 = lax.rsqrt(_rowmean(x1v * x1v) + EPS)
        h2 = (x1v * r2 * prm_ref[P_GFFN:P_GFFN + 1, :] * (1.0 + mod_ref[M_SC2:M_SC2 + 1, :])
              + mod_ref[M_SH2:M_SH2 + 1, :]).astype(BF16)
        h2_ref[...] = h2
        f = jnp.zeros((tm, d), F32)
        for c0, c1 in _ffn_chunks(ff):
            ab_ref[:, c0:c1] = _dot_nt(h2, wfi_ref[c0:c1, :]).astype(BF16)
            ab_ref[:, ff + c0:ff + c1] = _dot_nt(h2, wfi_ref[ff + c0:ff + c1, :]).astype(BF16)
            a = ab_ref[:, c0:c1].astype(F32)
            act = (a * _sigmoid(a) * ab_ref[:, ff + c0:ff + c1].astype(F32)).astype(BF16)
            f = f + _dot(act, wfo_ref[c0:c1, :])
        x2 = x1v + mod_ref[M_G2:M_G2 + 1, :] * f
        r3 = lax.rsqrt(_rowmean(x2 * x2) + EPS)
        xn3 = x2 * r3
        gfin = prm_ref[P_GFIN:P_GFIN + 1, :]
        err = xn3 * gfin - tgt_ref[...]
        dy = err * (1.0 / d)
        dxn3 = dy * gfin
        dx2 = r3 * (dxn3 - xn3 * _rowmean(dxn3 * xn3))
        dx2_ref[...] = dx2
        sums_ref[0:1, :] = sums_ref[0:1, :] + _colsum(dy * xn3)
        sums_ref[1:2, :] = sums_ref[1:2, :] + _colsum(dx2 * f)
        sums_ref[2:3, :] = sums_ref[2:3, :] + _colsum(err * err) * (0.5 / d)

    tile = pl.BlockSpec((tm, d), lambda i: (i, 0))
    whole = lambda shape: pl.BlockSpec(shape, lambda i: (0,) * len(shape))
    return pl.pallas_call(
        body, name="forward_ffn", grid=(t // tm,),
        out_shape=(jax.ShapeDtypeStruct((t, d), F32), jax.ShapeDtypeStruct((t, 2 * ff), BF16),
                   jax.ShapeDtypeStruct((t, d), BF16), jax.ShapeDtypeStruct((8, d), F32)),
        in_specs=[tile, tile, whole((8, d)), whole((8, d)), ANY_SPEC, ANY_SPEC],
        out_specs=[tile, pl.BlockSpec((tm, 2 * ff), lambda i: (i, 0)), tile, whole((8, d))],
        scratch_shapes=[pltpu.VMEM(w_fi_t.shape, BF16), pltpu.VMEM(w_fo.shape, BF16)],
        compiler_params=_params(("arbitrary",)),
    )(x1, tgt, mod, prm, w_fi_t, w_fo)


def _backward_ffn(dx2, x1, ab, mod, prm, w_fi_t, w_fo, tm):
    t, d = x1.shape
    ff = w_fo.shape[0]

    def body(dx2_ref, x1_ref, ab_ref, mod_ref, prm_ref, wfi_hbm, wfo_hbm,
             dx1_ref, df_ref, act_ref, dab_ref, sums_ref, wfi_ref, wfo_ref):
        i = pl.program_id(0)

        @pl.when(i == 0)
        def _():
            pltpu.sync_copy(wfi_hbm, wfi_ref)
            pltpu.sync_copy(wfo_hbm, wfo_ref)
            sums_ref[...] = jnp.zeros((8, d), F32)

        dx2v = dx2_ref[...]
        df = (mod_ref[M_G2:M_G2 + 1, :] * dx2v).astype(BF16)
        df_ref[...] = df
        dh2 = jnp.zeros((tm, d), F32)
        for c0, c1 in _ffn_chunks(ff):
            dact = _dot_nt(df, wfo_ref[c0:c1, :])
            a = ab_ref[:, c0:c1].astype(F32)
            b = ab_ref[:, ff + c0:ff + c1].astype(F32)
            s = _sigmoid(a)
            sil = a * s
            act_ref[:, c0:c1] = (sil * b).astype(BF16)
            da = (dact * b * (s * (1.0 + a * (1.0 - s)))).astype(BF16)
            db = (dact * sil).astype(BF16)
            dab_ref[:, c0:c1] = da
            dab_ref[:, ff + c0:ff + c1] = db
            dh2 = dh2 + _dot(da, wfi_ref[c0:c1, :]) + _dot(db, wfi_ref[ff + c0:ff + c1, :])
        x1v = x1_ref[...]
        r2 = lax.rsqrt(_rowmean(x1v * x1v) + EPS)
        xn2 = x1v * r2
        gffn = prm_ref[P_GFFN:P_GFFN + 1, :]
        scale = 1.0 + mod_ref[M_SC2:M_SC2 + 1, :]
        dxn2 = dh2 * gffn * scale
        dx1_ref[...] = dx2v + r2 * (dxn2 - xn2 * _rowmean(dxn2 * xn2))
        hx = dh2 * xn2
        sums_ref[0:1, :] = sums_ref[0:1, :] + _colsum(dh2)
        sums_ref[1:2, :] = sums_ref[1:2, :] + _colsum(hx) * gffn
        sums_ref[2:3, :] = sums_ref[2:3, :] + _colsum(hx) * scale

    tile = pl.BlockSpec((tm, d), lambda i: (i, 0))
    whole = lambda shape: pl.BlockSpec(shape, lambda i: (0,) * len(shape))
    wide = lambda n: pl.BlockSpec((tm, n), lambda i: (i, 0))
    return pl.pallas_call(
        body, name="backward_ffn", grid=(t // tm,),
        out_shape=(jax.ShapeDtypeStruct((t, d), F32), jax.ShapeDtypeStruct((t, d), BF16),
                   jax.ShapeDtypeStruct((t, ff), BF16), jax.ShapeDtypeStruct((t, 2 * ff), BF16),
                   jax.ShapeDtypeStruct((8, d), F32)),
        in_specs=[tile, tile, wide(2 * ff), whole((8, d)), whole((8, d)), ANY_SPEC, ANY_SPEC],
        out_specs=[tile, tile, wide(ff), wide(2 * ff), whole((8, d))],
        scratch_shapes=[pltpu.VMEM(w_fi_t.shape, BF16), pltpu.VMEM(w_fo.shape, BF16)],
        compiler_params=_params(("arbitrary",)),
    )(dx2, x1, ab, mod, prm, w_fi_t, w_fo)


def _backward_mix(dx1, proj, y_a, y_b, mix, u1, mod, prm, taps, w_so, w_co, w_o, tm, partials):
    t, d = dx1.shape
    nt = t // tm
    na = len(partials)
    hs, hc = HALO_SHORT, HALO_CONF
    r3, r31 = 0, 8

    def body(*refs):
        (dx1_ref, proj_ref, halo_ref, ya_ref, yb_ref, mix_ref, u1_ref, mod_ref, prm_ref, taps_ref,
         wso_ref, wco_ref, wo_ref) = refs[:13]
        (dproj_ref, dmix_ref, dya_ref, dyb_ref, merged_ref, yapre_ref, u3_ref, sums_ref, dw3_ref,
         dw31_ref) = refs[13 + na:23 + na]
        cv_ext, u0_ext, d3_ext, du1_ext, tmp_ref = refs[23 + 2 * na:28 + 2 * na]
        scatter_start, scatter_finish = _scatter_plan(refs[13:13 + na], refs[23 + na:23 + 2 * na], *refs[28 + 2 * na:])
        i = pl.program_id(0)
        first_tile = i == nt - 1

        @pl.when(i == 0)
        def _():
            scatter_start()
            sums_ref[...] = jnp.zeros((8, d), F32)
            dw3_ref[...] = jnp.zeros(dw3_ref.shape, F32)
            dw31_ref[...] = jnp.zeros(dw31_ref.shape, F32)
            d3_ext[tm:tm + hs, :] = jnp.zeros((hs, d), F32)
            du1_ext[tm:tm + hc, :] = jnp.zeros((hc, d), F32)

        def col(g):
            return proj_ref[:, g * d:(g + 1) * d].astype(F32)

        def hcol(g, rows):
            v = halo_ref[HALO_CONF - rows:HALO_CONF, g * d:(g + 1) * d].astype(F32)
            return jnp.where(first_tile, 0.0, v)

        dx1v = dx1_ref[...]
        mixv = mix_ref[...].astype(F32)
        dmix = (mod_ref[M_G1:M_G1 + 1, :] * dx1v).astype(BF16)
        dmix_ref[...] = dmix
        sums_ref[0:1, :] = sums_ref[0:1, :] + _colsum(dx1v * mixv)
        dmerged = _dot_nt(dmix, wo_ref[...])
        ga = _sigmoid(col(5))
        gb = _sigmoid(col(6))
        yav = ya_ref[...].astype(F32)
        ybv = yb_ref[...].astype(F32)
        dya = (dmerged * ga).astype(BF16)
        dyb = (dmerged * gb).astype(BF16)
        dya_ref[...] = dya
        dyb_ref[...] = dyb
        dproj_ref[:, 5 * d:6 * d] = (dmerged * yav * ga * (1.0 - ga)).astype(BF16)
        dproj_ref[:, 6 * d:7 * d] = (dmerged * ybv * gb * (1.0 - gb)).astype(BF16)
        merged_ref[...] = (ga * yav + gb * ybv).astype(BF16)

        dya_pre = _dot_nt(dya, wso_ref[...])
        c_s, v_s, b_s = col(1), col(2), col(0)
        cv_ext[0:hs, :] = hcol(1, hs) * hcol(2, hs)
        cv_ext[hs:hs + tm, :] = c_s * v_s
        _tap_conv(cv_ext, taps_ref, [(r3 + wi, off) for wi, off in _causal_taps(SHORT_K, hs)], tm, hs, tmp_ref)
        conv3 = tmp_ref[...]
        yapre_ref[...] = (b_s * conv3).astype(BF16)
        dproj_ref[:, 0:d] = (dya_pre * conv3).astype(BF16)
        d3_ext[0:tm, :] = dya_pre * b_s
        _tap_wgrad(d3_ext, cv_ext, _causal_taps(SHORT_K, hs), tm, hs, dw3_ref)
        _tap_conv(d3_ext, taps_ref, [(r3 + wi, off) for wi, off in _anticausal_taps(SHORT_K)], tm, hs, tmp_ref)
        dcv = tmp_ref[...]
        dproj_ref[:, d:2 * d] = (dcv * v_s).astype(BF16)
        dproj_ref[:, 2 * d:3 * d] = (dcv * c_s).astype(BF16)
        d3_ext[tm:tm + hs, :] = d3_ext[0:hs, :]

        du3 = _dot_nt(dyb, wco_ref[...])
        u1v = u1_ref[...].astype(F32)
        mu = _rowmean(u1v)
        uc = u1v - mu
        rstd = lax.rsqrt(_rowmean(uc * uc) + LN_EPS)
        uhat = uc * rstd
        lng = prm_ref[P_LNG:P_LNG + 1, :]
        u2 = uhat * lng + prm_ref[P_LNB:P_LNB + 1, :]
        s2 = _sigmoid(u2)
        u3_ref[...] = (u2 * s2).astype(BF16)
        du2 = du3 * (s2 * (1.0 + u2 * (1.0 - s2)))
        sums_ref[1:2, :] = sums_ref[1:2, :] + _colsum(du2 * uhat)
        sums_ref[2:3, :] = sums_ref[2:3, :] + _colsum(du2)
        duhat = du2 * lng
        du1 = rstd * (duhat - _rowmean(duhat) - uhat * _rowmean(duhat * uhat))
        sums_ref[3:4, :] = sums_ref[3:4, :] + _colsum(du1)
        du1_ext[0:tm, :] = du1
        v_c = col(3)
        sg = _sigmoid(col(4))
        u0_ext[0:hc, :] = hcol(3, hc) * _sigmoid(hcol(4, hc))
        u0_ext[hc:hc + tm, :] = v_c * sg
        _tap_wgrad(du1_ext, u0_ext, _causal_taps(CONF_K, hc), tm, hc, dw31_ref)
        _tap_conv(du1_ext, taps_ref, [(r31 + wi, off) for wi, off in _anticausal_taps(CONF_K)], tm, hc, tmp_ref)
        du0 = tmp_ref[...]
        dproj_ref[:, 3 * d:4 * d] = (du0 * sg).astype(BF16)
        dproj_ref[:, 4 * d:5 * d] = (du0 * v_c * sg * (1.0 - sg)).astype(BF16)
        du1_ext[tm:tm + hc, :] = du1_ext[0:hc, :]

        @pl.when(i == nt - 1)
        def _():
            scatter_finish()

    rev = lambda i: (nt - 1 - i, 0)
    tile = pl.BlockSpec((tm, d), rev)
    whole = lambda shape: pl.BlockSpec(shape, lambda i: (0,) * len(shape))
    hblocks = tm // HALO_CONF
    halo = pl.BlockSpec((HALO_CONF, 7 * d), lambda i: (jnp.maximum((nt - 1 - i) * hblocks - 1, 0), 0))
    bf = jax.ShapeDtypeStruct((t, d), BF16)
    res = pl.pallas_call(
        body, name="backward_mix", grid=(nt,),
        out_shape=(jax.ShapeDtypeStruct((t, 7 * d), BF16), bf, bf, bf, bf, bf, bf,
                   jax.ShapeDtypeStruct((8, d), F32),
                   jax.ShapeDtypeStruct((SUBLANES * SHORT_K, d), F32),
                   jax.ShapeDtypeStruct((SUBLANES * CONF_K, d), F32))
        + tuple(jax.ShapeDtypeStruct(p.shape, p.dtype) for p in partials),
        in_specs=[tile, pl.BlockSpec((tm, 7 * d), rev), halo, tile, tile, tile, tile,
                  whole((8, d)), whole((8, d)), whole(taps.shape), whole((d, d)), whole((d, d)), whole((d, d))]
        + [ANY_SPEC] * na,
        out_specs=[pl.BlockSpec((tm, 7 * d), rev), tile, tile, tile, tile, tile, tile,
                   whole((8, d)), whole((SUBLANES * SHORT_K, d)), whole((SUBLANES * CONF_K, d))] + [ANY_SPEC] * na,
        scratch_shapes=[pltpu.VMEM((hs + tm, d), F32), pltpu.VMEM((hc + tm, d), F32),
                        pltpu.VMEM((tm + hs, d), F32), pltpu.VMEM((tm + hc, d), F32),
                        pltpu.VMEM((tm, d), F32)] + _comm_scratch(na),
        compiler_params=_params(("arbitrary",)),
    )(dx1, proj, proj, y_a, y_b, mix, u1, mod, prm, taps, w_so, w_co, w_o, *partials)
    return res[:10], res[10:]


def _backward_in(dproj, x, dx1, mod, prm, w_in_g, tm, partials, partial_view, exchanged_shapes):
    t, d = x.shape
    n_all = w_in_g.shape[1]
    na = len(partials)
    nt = t // tm

    def body(*refs):
        dproj_ref, x_ref, dx1_ref, mod_ref, prm_ref, w_hbm = refs[:6]
        gx_ref, sums_ref = refs[6 + na:8 + na]
        w_ref = refs[8 + 2 * na]
        scatter_start, scatter_finish = _scatter_plan(refs[6:6 + na], refs[8 + na:8 + 2 * na], *refs[9 + 2 * na:],
                                                      view=partial_view)

        @pl.when(pl.program_id(0) == 0)
        def _():
            scatter_start()
            pltpu.sync_copy(w_hbm, w_ref)
            sums_ref[...] = jnp.zeros((8, d), F32)

        dh = _dot_nt(dproj_ref[...], w_ref[...])
        xv = x_ref[...]
        r1 = lax.rsqrt(_rowmean(xv * xv) + EPS)
        xn = xv * r1
        gmix = prm_ref[P_GMIX:P_GMIX + 1, :]
        scale = 1.0 + mod_ref[M_SC1:M_SC1 + 1, :]
        dxn = dh * gmix * scale
        gx_ref[...] = dx1_ref[...] + r1 * (dxn - xn * _rowmean(dxn * xn))
        hx = dh * xn
        sums_ref[0:1, :] = sums_ref[0:1, :] + _colsum(dh)
        sums_ref[1:2, :] = sums_ref[1:2, :] + _colsum(hx) * gmix
        sums_ref[2:3, :] = sums_ref[2:3, :] + _colsum(hx) * scale

        @pl.when(pl.program_id(0) == nt - 1)
        def _():
            scatter_finish()

    tile = pl.BlockSpec((tm, d), lambda i: (i, 0))
    whole = pl.BlockSpec((8, d), lambda i: (0, 0))
    res = pl.pallas_call(
        body, name="backward_in", grid=(nt,),
        out_shape=(jax.ShapeDtypeStruct((t, d), F32), jax.ShapeDtypeStruct((8, d), F32))
        + tuple(jax.ShapeDtypeStruct(s, p.dtype) for p, s in zip(partials, exchanged_shapes)),
        in_specs=[pl.BlockSpec((tm, n_all), lambda i: (i, 0)), tile, tile, whole, whole, ANY_SPEC] + [ANY_SPEC] * na,
        out_specs=[tile, whole] + [ANY_SPEC] * na,
        scratch_shapes=[pltpu.VMEM(w_in_g.shape, BF16)] + _comm_scratch(na),
        compiler_params=_params(("arbitrary",)),
    )(dproj, x, dx1, mod, prm, w_in_g, *partials)
    return res[0], res[1], res[2:]


def _weight_grad(a, b, a_spec, b_spec, ns, m, n, nk, name, partials=(), by_columns=False):
    na = len(partials)

    def body(*refs):
        a_ref, b_ref = refs[:2]
        o_ref = refs[2 + na]
        acc_ref = refs[3 + 2 * na]
        s, k = pl.program_id(0), pl.program_id(1)
        if na:
            scatter_start, scatter_finish = _scatter_plan(refs[2:2 + na], refs[3 + na:3 + 2 * na], *refs[4 + 2 * na:])

            @pl.when((s == 0) & (k == 0))
            def _():
                scatter_start()

        av = a_ref[0] if len(a_ref.shape) == 3 else a_ref[...]
        bv = b_ref[0] if len(b_ref.shape) == 3 else b_ref[...]
        part = _dot_tn(av, bv)

        @pl.when(k == 0)
        def _():
            acc_ref[...] = part

        @pl.when(k > 0)
        def _():
            acc_ref[...] = acc_ref[...] + part

        @pl.when(k == nk - 1)
        def _():
            if by_columns:
                o_ref[...] = acc_ref[...].astype(BF16)
            else:
                o_ref[0] = acc_ref[...].astype(BF16)

        if na:
            @pl.when((s == ns - 1) & (k == nk - 1))
            def _():
                scatter_finish()

    if by_columns:
        out_shape, out_spec = (m, ns * n), pl.BlockSpec((m, n), lambda s, k: (0, s))
    else:
        out_shape, out_spec = (ns, m, n), pl.BlockSpec((1, m, n), lambda s, k: (s, 0, 0))
    res = pl.pallas_call(
        body, name=name, grid=(ns, nk),
        out_shape=(jax.ShapeDtypeStruct(out_shape, BF16),)
        + tuple(jax.ShapeDtypeStruct(p.shape, p.dtype) for p in partials),
        in_specs=[a_spec, b_spec] + [ANY_SPEC] * na,
        out_specs=[out_spec] + [ANY_SPEC] * na,
        scratch_shapes=[pltpu.VMEM((m, n), F32)] + (_comm_scratch(na) if na else []),
        compiler_params=_params(("arbitrary", "arbitrary")),
    )(a, b, *partials)
    return (res[0], res[1:]) if na else res[0]


def _adamw(w, g, m, v):
    m = ADAM_B1 * m + (1.0 - ADAM_B1) * g
    v = ADAM_B2 * v + (1.0 - ADAM_B2) * (g * g)
    m_hat = m / (1.0 - ADAM_B1 ** ADAM_STEP)
    v_hat = v / (1.0 - ADAM_B2 ** ADAM_STEP)
    delta = -ADAM_LR * (m_hat / (jnp.sqrt(v_hat) + ADAM_EPS) + ADAM_WD * w)
    return delta, m, v


def _adamw_shard(parts, w, m, v, tr, name):
    r, c = w.shape

    def body(p_ref, w_ref, m_ref, v_ref, g_ref, d_ref, nm_ref, nv_ref):
        g = p_ref[0].astype(F32)
        for s in range(1, N_DEV):
            g = g + p_ref[s].astype(F32)
        delta, nm, nv = _adamw(w_ref[...], g, m_ref[...], v_ref[...])
        g_ref[...] = g
        d_ref[...] = delta
        nm_ref[...] = nm
        nv_ref[...] = nv

    tile = pl.BlockSpec((tr, c), lambda i: (i, 0))
    return pl.pallas_call(
        body, name=name, grid=(r // tr,),
        out_shape=(jax.ShapeDtypeStruct((r, c), F32),) * 4,
        in_specs=[pl.BlockSpec((N_DEV, tr, c), lambda i: (0, i, 0)), tile, tile, tile],
        out_specs=[tile] * 4,
        compiler_params=_params(("arbitrary",)),
    )(parts, w, m, v)


def _ada_update(sc_all, dmod_cols, w, m, v, tr):
    d, n = w.shape

    def body(sc_ref, dm_ref, w_ref, m_ref, v_ref, g_ref, d_ref, nm_ref, nv_ref):
        g = lax.dot_general(sc_ref[...], dm_ref[...], (((0,), (0,)), ((), ())),
                            preferred_element_type=F32, precision=lax.Precision.HIGHEST)
        delta, nm, nv = _adamw(w_ref[...], g, m_ref[...], v_ref[...])
        g_ref[...] = g
        d_ref[...] = delta
        nm_ref[...] = nm
        nv_ref[...] = nv

    tile = pl.BlockSpec((tr, n), lambda i: (i, 0))
    return pl.pallas_call(
        body, name="ada_update", grid=(d // tr,),
        out_shape=(jax.ShapeDtypeStruct((d, n), F32),) * 4,
        in_specs=[pl.BlockSpec((N_DEV, tr), lambda i: (0, i)), pl.BlockSpec((N_DEV, n), lambda i: (0, 0)),
                  tile, tile, tile],
        out_specs=[tile] * 4,
        compiler_params=_params(("arbitrary",)),
    )(sc_all, dmod_cols, w, m, v)


def _small_exchange(vec, cg):
    l = vec.shape[2]
    rows = cg.shape[1]

    def body(vec_ref, cg_ref, vall_ref, cgr_ref, send_sems, recv_sems):
        x, y, c = _my_coords()
        me = _slot(x, y, c)
        vall_ref[me] = vec_ref[0]
        cgr_ref[me] = cg_ref[me]
        copies = []
        for k in range(1, N_DEV):
            peer = (_flip(x, k & 4), _flip(y, k & 2), _flip(c, k & 1))
            copies.append(pltpu.make_async_remote_copy(
                src_ref=vall_ref.at[me], dst_ref=vall_ref.at[me], send_sem=send_sems.at[k - 1],
                recv_sem=recv_sems.at[k - 1], device_id=peer, device_id_type=MESH))
            copies.append(pltpu.make_async_remote_copy(
                src_ref=cg_ref.at[_slot(*peer)], dst_ref=cgr_ref.at[me], send_sem=send_sems.at[7 + k - 1],
                recv_sem=recv_sems.at[7 + k - 1], device_id=peer, device_id_type=MESH))
        for cp in copies:
            cp.start()
        for cp in copies:
            cp.wait_recv()
        for cp in copies:
            cp.wait_send()

    vm = pl.BlockSpec(memory_space=pltpu.VMEM)
    return pl.pallas_call(
        body, name="small_exchange",
        out_shape=(jax.ShapeDtypeStruct((N_DEV, 1, l), F32), jax.ShapeDtypeStruct((N_DEV, rows, LANES), F32)),
        in_specs=[vm, vm], out_specs=[vm, vm],
        scratch_shapes=[pltpu.SemaphoreType.DMA((14,)), pltpu.SemaphoreType.DMA((14,))],
        compiler_params=_params(),
    )(vec, cg)


def _small_update(vall, cgr, smalls):
    ns = len(smalls)

    def body(*refs):
        vall_ref, cgr_ref = refs[0], refs[1]
        wmv = refs[2:2 + 3 * ns]
        outs = refs[2 + 3 * ns:]
        for p, (_, _, _, lo, hi, kind) in enumerate(smalls):
            w_ref, m_ref, v_ref = wmv[3 * p:3 * p + 3]
            part = (lambda s: vall_ref[s, :, lo:hi]) if kind == "vec" else (lambda s: cgr_ref[s, lo:hi, :])
            g = part(0)
            for s in range(1, N_DEV):
                g = g + part(s)
            delta, nm, nv = _adamw(w_ref[...], g, m_ref[...], v_ref[...])
            for o_ref, val in zip(outs[4 * p:4 * p + 4], (g, delta, nm, nv)):
                o_ref[...] = val

    vm = pl.BlockSpec(memory_space=pltpu.VMEM)
    args = [vall, cgr]
    out_shape = []
    for w, m, v, _, _, _ in smalls:
        args += [w, m, v]
        out_shape += [jax.ShapeDtypeStruct(w.shape, F32)] * 4
    res = pl.pallas_call(
        body, name="small_update",
        out_shape=tuple(out_shape),
        in_specs=[vm] * len(args), out_specs=[vm] * len(out_shape),
        compiler_params=_params(),
    )(*args)
    return [res[4 * p:4 + 4 * p] for p in range(ns)]


def _pick(t, want):
    return want if t % want == 0 else t


def kernel(x, c, w_ada, b_ada, norm_mix_g, w_in, conv_short_w, w_short_out, conv_conf_w, conv_conf_b, conf_ln_g, conf_ln_b, w_conf_out, w_o, norm_ffn_g, w_ffn_in, w_ffn_out, final_norm_g, loss_target, m_w_ada, m_b_ada, m_norm_mix_g, m_w_in, m_conv_short_w, m_w_short_out, m_conv_conf_w, m_conv_conf_b, m_conf_ln_g, m_conf_ln_b, m_w_conf_out, m_w_o, m_norm_ffn_g, m_w_ffn_in, m_w_ffn_out, m_final_norm_g, v_w_ada, v_b_ada, v_norm_mix_g, v_w_in, v_conv_short_w, v_w_short_out, v_conv_conf_w, v_conv_conf_b, v_conf_ln_g, v_conf_ln_b, v_w_conf_out, v_w_o, v_norm_ffn_g, v_w_ffn_in, v_w_ffn_out, v_final_norm_g):
    t, d = x.shape[1], x.shape[2]
    x2 = x.reshape(t, d)
    tgt = loss_target.reshape(t, d)
    me = _slot(*_my_coords())
    tm = _pick(t, 256)
    tm_fwd_in = _pick(t, 1024)
    tk = _pick(t, 2048)

    taps_loc = jnp.zeros((40, LANES), F32)
    taps_loc = taps_loc.at[0:SHORT_K].set(conv_short_w[0]).at[8:8 + CONF_K].set(conv_conf_w[0])
    mod_flat, sc_all3, taps = _ada_forward(c, w_ada[0], b_ada, taps_loc)
    mod = jnp.concatenate([mod_flat.reshape(6, d), jnp.zeros((2, d), F32)], axis=0)
    prm = jnp.concatenate([norm_mix_g, norm_ffn_g, final_norm_g.reshape(1, d), conv_conf_b, conf_ln_g, conf_ln_b,
                           jnp.zeros((2, d), F32)], axis=0)

    n_in = w_in.shape[2]
    nk = t // tk
    tok = pl.BlockSpec((tk, d), lambda s, k: (k, 0))
    rows = d // N_DEV
    frows = w_ffn_out.shape[1]

    proj, h, w_in_g, (w_so_g, w_co_g, w_o_g) = _forward_in(
        jnp.reshape(me, (1,)).astype(jnp.int32), x2, mod, prm, w_in[0].astype(BF16), tm_fwd_in,
        [w_short_out[0].astype(BF16), w_conf_out[0].astype(BF16), w_o[0].astype(BF16)])
    w_so = w_so_g.reshape(d, d)
    w_co = w_co_g.reshape(d, d)
    w_oo = w_o_g.reshape(d, d)
    w_fi_t, m_fi_t, v_fi_t = w_ffn_in[0].T, m_w_ffn_in[0].T, v_w_ffn_in[0].T
    (x1, y_a, y_b, mix, u1), (w_fi_g, w_fo_g) = _forward_mix(
        x2, proj, mod, prm, taps, w_so, w_co, w_oo, tm, [w_fi_t.astype(BF16), w_ffn_out[0].astype(BF16)])
    fb = w_fi_g.shape[1]
    ff = N_DEV * frows
    w_fi_all = w_fi_g.reshape(N_DEV * fb, d)
    w_fo_all = w_fo_g.reshape(ff, d)
    dx2, ab, h2, sums_f = _forward_ffn(x1, tgt, mod, prm, w_fi_all, w_fo_all, tm)

    dx1, df, act, dab, sums_b = _backward_ffn(dx2, x1, ab, mod, prm, w_fi_all, w_fo_all, tm)
    fb2 = 2 * fb
    pair_tok = pl.BlockSpec((tk, fb2), lambda s, k: (k, s))
    g_fi = _weight_grad(dab, h2, pair_tok, tok, N_DEV // 2, fb2, d, nk, "grad_w_ffn_in")
    g_fo = _weight_grad(act, df, pair_tok, tok, N_DEV // 4, fb2, d, nk, "grad_w_ffn_out")
    (dproj, dmix, dya, dyb, merged, ya_pre, u3, sums_m, dw3p, dw31p), (p_fi, p_fo) = _backward_mix(
        dx1, proj, y_a, y_b, mix, u1, mod, prm, taps, w_so, w_co, w_oo, tm,
        [g_fi.reshape(N_DEV, fb, d), g_fo.reshape(N_DEV, frows, d)])
    g_so = _weight_grad(ya_pre, dya, tok, tok, 1, d, d, nk, "grad_w_short_out")
    g_co = _weight_grad(u3, dyb, tok, tok, 1, d, d, nk, "grad_w_conf_out")
    g_oo = _weight_grad(merged, dmix, tok, tok, 1, d, d, nk, "grad_w_o")
    n_blk = _pick(N_DEV * n_in, d)
    g_in, (p_so, p_co, p_oo) = _weight_grad(
        h, dproj, tok, pl.BlockSpec((tk, n_blk), lambda s, k: (k, s)), N_DEV * n_in // n_blk, d, n_blk, nk, "grad_w_in",
        [g_so.reshape(N_DEV, rows, d), g_co.reshape(N_DEV, rows, d), g_oo.reshape(N_DEV, rows, d)], by_columns=True)
    grad_x, sums_i, (p_in,) = _backward_in(dproj, x2, dx1, mod, prm, w_in_g, tm, [g_in], _column_block(n_in),
                                           [(N_DEV, d, n_in)])

    up_in = _adamw_shard(p_in, w_in[0], m_w_in[0], v_w_in[0], _pick(d, 256), "adamw_w_in")
    up_so = _adamw_shard(p_so, w_short_out[0], m_w_short_out[0], v_w_short_out[0], rows, "adamw_w_short_out")
    up_co = _adamw_shard(p_co, w_conf_out[0], m_w_conf_out[0], v_w_conf_out[0], rows, "adamw_w_conf_out")
    up_oo = _adamw_shard(p_oo, w_o[0], m_w_o[0], v_w_o[0], rows, "adamw_w_o")
    up_fi = tuple(a.T for a in _adamw_shard(p_fi, w_fi_t, m_fi_t, v_fi_t, fb // 2, "adamw_w_ffn_in"))
    up_fo = _adamw_shard(p_fo, w_ffn_out[0], m_w_ffn_out[0], v_w_ffn_out[0], frows, "adamw_w_ffn_out")

    vec = jnp.concatenate([sums_i[0:2], sums_m[0:1], sums_b[0:2], sums_f[1:2],
                           sums_i[2:3], sums_m[3:4], sums_m[1:3], sums_b[2:3], sums_f[0:1],
                           sums_f[2:3]], axis=0)
    vec = vec.reshape(1, 1, 13 * d)
    dw3 = dw3p.reshape(SHORT_K, SUBLANES, d).sum(axis=1)
    dw31 = dw31p.reshape(CONF_K, SUBLANES, d).sum(axis=1)
    cg = jnp.zeros((40, d), F32).at[0:SHORT_K].set(dw3).at[8:8 + CONF_K].set(dw31)
    cg = cg.reshape(40, N_DEV, LANES).transpose(1, 0, 2)
    fin = lambda a: a.reshape(1, d)
    tap = lambda a: a.reshape(a.shape[1:])
    smalls = [
        (b_ada, m_b_ada, v_b_ada, 0, 6 * d, "vec"),
        (norm_mix_g, m_norm_mix_g, v_norm_mix_g, 6 * d, 7 * d, "vec"),
        (tap(conv_short_w), tap(m_conv_short_w), tap(v_conv_short_w), 0, SHORT_K, "cg"),
        (tap(conv_conf_w), tap(m_conv_conf_w), tap(v_conv_conf_w), 8, 8 + CONF_K, "cg"),
        (conv_conf_b, m_conv_conf_b, v_conv_conf_b, 7 * d, 8 * d, "vec"),
        (conf_ln_g, m_conf_ln_g, v_conf_ln_g, 8 * d, 9 * d, "vec"),
        (conf_ln_b, m_conf_ln_b, v_conf_ln_b, 9 * d, 10 * d, "vec"),
        (norm_ffn_g, m_norm_ffn_g, v_norm_ffn_g, 10 * d, 11 * d, "vec"),
        (fin(final_norm_g), fin(m_final_norm_g), fin(v_final_norm_g), 11 * d, 12 * d, "vec"),
    ]
    vall, cgr = _small_exchange(vec, cg)
    up_small = _small_update(vall, cgr, smalls)
    n_ada = w_ada.shape[2]
    dmod_all = vall.reshape(N_DEV, 13 * d)[:, 0:6 * d]
    dmod_cols = lax.dynamic_slice(dmod_all, (0, me * n_ada), (N_DEV, n_ada))
    up_ada = _ada_update(sc_all3.reshape(N_DEV, d), dmod_cols, w_ada[0], m_w_ada[0], v_w_ada[0], _pick(d, 256))

    loss = jnp.sum(vall.reshape(N_DEV, 13 * d)[:, 12 * d:])

    lead = lambda a: a.reshape((1,) + a.shape)
    ups = [tuple(lead(a) for a in up_ada), up_small[0], up_small[1], tuple(lead(a) for a in up_in),
           tuple(lead(a) for a in up_small[2]), tuple(lead(a) for a in up_so), tuple(lead(a) for a in up_small[3]),
           up_small[4], up_small[5], up_small[6],
           tuple(lead(a) for a in up_co), tuple(lead(a) for a in up_oo), up_small[7],
           tuple(lead(a) for a in up_fi), tuple(lead(a) for a in up_fo),
           tuple(a.reshape(d) for a in up_small[8])]
    grads = [u[0] for u in ups]
    deltas = [u[1] for u in ups]
    new_m = [u[2] for u in ups]
    new_v = [u[3] for u in ups]
    return (loss, grad_x.reshape(1, t, d), *grads, *deltas, *new_m, *new_v)
```

```python
import functools

import jax
import jax.numpy as jnp
from jax import lax
from jax.experimental import pallas as pl
from jax.experimental.pallas import tpu as pltpu

F32 = jnp.float32
BF16 = jnp.bfloat16
MESH = pl.DeviceIdType.MESH

N_DEV = 8
EPS = 1e-6
LN_EPS = 1e-5
SHORT_K = 3
CONF_K = 31
ADAM_LR = 0.001
ADAM_B1 = 0.9
ADAM_B2 = 0.999
ADAM_EPS = 1e-08
ADAM_WD = 0.01
ADAM_STEP = 10

LANES = 128
SUBLANES = 8
CONV_ROWS = 64
HALO_SHORT = 8
HALO_CONF = 32
VMEM_LIMIT = 56 * 1024 * 1024

M_SH1, M_SC1, M_G1, M_SH2, M_SC2, M_G2 = range(6)
P_GMIX, P_GFFN, P_GFIN, P_CBIAS, P_LNG, P_LNB = range(6)


def _params(sem=None, **kw):
    return pltpu.CompilerParams(dimension_semantics=sem, vmem_limit_bytes=VMEM_LIMIT, **kw)


def _sigmoid(v):
    return jax.nn.sigmoid(v)


def _dot(a, b):
    return jnp.dot(a, b, preferred_element_type=F32)


def _dot_nt(a, b):
    return lax.dot_general(a, b, (((1,), (1,)), ((), ())), preferred_element_type=F32)


def _dot_tn(a, b):
    return lax.dot_general(a, b, (((0,), (0,)), ((), ())), preferred_element_type=F32)


def _colsum(v):
    return jnp.sum(v, axis=0, keepdims=True)


def _rowmean(v):
    return jnp.mean(v, axis=-1, keepdims=True)


def _my_coords():
    return lax.axis_index("x"), lax.axis_index("y"), lax.axis_index("c")


def _slot(px, py, pc):
    return 4 * px + 2 * py + pc


def _flip(v, bit):
    return 1 - v if bit else v


def _taps_by_residue(taps):
    by_res = {}
    for wi, off in taps:
        by_res.setdefault(off % SUBLANES, []).append((wi, off // SUBLANES))
    return sorted(by_res.items())


def _tap_conv(ext_ref, w_ref, taps, tm, extra, out_ref):
    d = out_ref.shape[1]
    rb = min(CONV_ROWS, tm)
    wrows = rb + extra
    groups = _taps_by_residue(taps)

    def block(i, carry):
        base = pl.multiple_of(i * rb, SUBLANES)
        for lc in range(d // LANES):
            ls = pl.ds(lc * LANES, LANES)
            win = ext_ref[pl.ds(base, wrows), ls]
            acc = None
            for r, lst in groups:
                sh = win if r == 0 else pltpu.roll(win, wrows - r, 0)
                for wi, q in lst:
                    term = w_ref[wi:wi + 1, ls] * sh[SUBLANES * q:SUBLANES * q + rb, :]
                    acc = term if acc is None else acc + term
            out_ref[pl.ds(base, rb), ls] = acc
        return carry

    lax.fori_loop(0, tm // rb, block, 0)


def _tap_wgrad(a_ref, ext_ref, taps, tm, extra, acc_ref):
    d = a_ref.shape[1]
    rb = min(CONV_ROWS, tm)
    wrows = rb + extra
    groups = _taps_by_residue(taps)

    def block(i, carry):
        base = pl.multiple_of(i * rb, SUBLANES)
        for lc in range(d // LANES):
            ls = pl.ds(lc * LANES, LANES)
            a_blk = a_ref[pl.ds(base, rb), ls]
            win = ext_ref[pl.ds(base, wrows), ls]
            for r, lst in groups:
                sh = win if r == 0 else pltpu.roll(win, wrows - r, 0)
                for wi, q in lst:
                    prod = a_blk * sh[SUBLANES * q:SUBLANES * q + rb, :]
                    part = prod[0:SUBLANES, :]
                    for s in range(1, rb // SUBLANES):
                        part = part + prod[SUBLANES * s:SUBLANES * (s + 1), :]
                    rows = pl.ds(SUBLANES * wi, SUBLANES)
                    acc_ref[rows, ls] = acc_ref[rows, ls] + part
        return carry

    lax.fori_loop(0, tm // rb, block, 0)


def _causal_taps(k, halo):
    return [(i, halo - (k - 1) + i) for i in range(k)]


def _anticausal_taps(k):
    return [(i, (k - 1) - i) for i in range(k)]


def _ada_forward(c, w_ada_loc, b_ada, taps_loc):
    d = c.shape[1]
    nloc = w_ada_loc.shape[1]
    trows = taps_loc.shape[0]

    def body(c_ref, w_ref, b_ref, t_ref, mod_ref, sc_ref, taps_ref,
             part_ref, modrecv_ref, tapsall_ref, send_sems, recv_sems):
        x, y, cc = _my_coords()
        me = _slot(x, y, cc)
        cv = c_ref[...]
        sc_ref[me] = cv * _sigmoid(cv)
        tapsall_ref[me] = t_ref[...]

        def peer_of(k):
            return (_flip(x, k & 4), _flip(y, k & 2), _flip(cc, k & 1))

        def gather_copy(ref, base, k):
            return pltpu.make_async_remote_copy(
                src_ref=ref.at[me], dst_ref=ref.at[me], send_sem=send_sems.at[base + k - 1],
                recv_sem=recv_sems.at[base + k - 1], device_id=peer_of(k), device_id_type=MESH)

        first = [gather_copy(sc_ref, 0, k) for k in range(1, N_DEV)]
        first += [gather_copy(tapsall_ref, 7, k) for k in range(1, N_DEV)]
        for cp in first:
            cp.start()
        for cp in first[:7]:
            cp.wait_recv()
        sc_all = jnp.concatenate([sc_ref[s] for s in range(N_DEV)], axis=0)
        part = jnp.dot(sc_all, w_ref[...], preferred_element_type=F32,
                       precision=lax.Precision.HIGHEST)
        for b in range(N_DEV):
            part_ref[b] = part[b:b + 1, :]
        modrecv_ref[me] = part_ref[me]
        second = []
        for k in range(1, N_DEV):
            px, py, pc = peer_of(k)
            second.append(pltpu.make_async_remote_copy(
                src_ref=part_ref.at[_slot(px, py, pc)], dst_ref=modrecv_ref.at[me],
                send_sem=send_sems.at[14 + k - 1], recv_sem=recv_sems.at[14 + k - 1],
                device_id=(px, py, pc), device_id_type=MESH))
        for cp in second:
            cp.start()
        for cp in second:
            cp.wait_recv()
        mod = jnp.concatenate([modrecv_ref[s] for s in range(N_DEV)], axis=1)
        mod_ref[...] = mod + b_ref[...]
        for cp in first[7:]:
            cp.wait_recv()
        taps_ref[...] = jnp.concatenate([tapsall_ref[s] for s in range(N_DEV)], axis=1)
        for cp in first + second:
            cp.wait_send()

    return pl.pallas_call(
        body, name="ada_forward",
        out_shape=(jax.ShapeDtypeStruct((1, N_DEV * nloc), F32),
                   jax.ShapeDtypeStruct((N_DEV, 1, d), F32),
                   jax.ShapeDtypeStruct((trows, N_DEV * LANES), F32)),
        in_specs=[pl.BlockSpec(memory_space=pltpu.VMEM)] * 4,
        out_specs=[pl.BlockSpec(memory_space=pltpu.VMEM)] * 3,
        scratch_shapes=[pltpu.VMEM((N_DEV, 1, nloc), F32), pltpu.VMEM((N_DEV, 1, nloc), F32),
                        pltpu.VMEM((N_DEV, trows, LANES), F32),
                        pltpu.SemaphoreType.DMA((21,)), pltpu.SemaphoreType.DMA((21,))],
        compiler_params=_params(),
    )(c, w_ada_loc, b_ada, taps_loc)


ANY_SPEC = pl.BlockSpec(memory_space=pl.ANY)


def _comm_scratch(na):
    return [pltpu.SemaphoreType.DMA((7 * na,)), pltpu.SemaphoreType.DMA((7 * na,)), pltpu.SemaphoreType.DMA((na,))]


def _leading_block(ref, slot):
    return ref.at[slot]


def _column_block(width):
    def view(ref, slot):
        return ref.at[:, pl.ds(pl.multiple_of(slot * width, LANES), width)]
    return view


class _Gather:
    def __init__(self, ins, outs, send_sems, recv_sems, local_sems, view=_leading_block):
        self.na = len(ins)
        x, y, c = _my_coords()
        self.c = c
        self.view = view
        self.me, self.sibling = (x, y, c), (x, y, 1 - c)
        self.chips = [(1 - x, y), (x, 1 - y), (1 - x, 1 - y)]
        self.outs, self.send_sems, self.recv_sems = outs, send_sems, recv_sems
        self.mine = [pltpu.make_async_copy(ins[a], view(outs[a], _slot(*self.me)), local_sems.at[a])
                     for a in range(self.na)]
        self.first = []
        for a in range(self.na):
            self.first.append(self._copy(a, 0, self.me, self.sibling, src=ins[a]))
            self.first += [self._copy(a, 1 + j, self.me, (*chip, c), src=ins[a]) for j, chip in enumerate(self.chips)]
        self.passed = [self._copy(a, 4 + j, (*chip, c), self.sibling)
                       for a in range(self.na) for j, chip in enumerate(self.chips)]

    def _copy(self, a, k, block, to, src=None):
        dst = self.view(self.outs[a], _slot(*block))
        return pltpu.make_async_remote_copy(
            src_ref=dst if src is None else src, dst_ref=dst,
            send_sem=self.send_sems.at[7 * a + k], recv_sem=self.recv_sems.at[7 * a + k],
            device_id=to, device_id_type=MESH)

    def start(self):
        for cp in self.mine + self.first:
            cp.start()

    def forward_chip(self, j):
        for a in range(self.na):
            self._copy(a, 1 + j, (*self.chips[j], self.c), self.me).wait_recv()
            self.passed[3 * a + j].start()

    def wait_sibling_own(self):
        for a in range(self.na):
            self._copy(a, 0, self.sibling, self.me).wait_recv()

    def wait_sibling_passed(self, j):
        for a in range(self.na):
            self._copy(a, 4 + j, (*self.chips[j], 1 - self.c), self.me).wait_recv()

    def finish_sends(self):
        for cp in self.first + self.passed:
            cp.wait_send()
        for cp in self.mine:
            cp.wait()

    def forward(self):
        for j in range(3):
            self.forward_chip(j)

    def finish(self):
        self.wait_sibling_own()
        for j in range(3):
            self.wait_sibling_passed(j)
        self.finish_sends()


def _scatter_plan(ins, outs, send_sems, recv_sems, local_sems, view=_leading_block):
    na = len(ins)
    x, y, c = _my_coords()
    me = _slot(x, y, c)
    mine = [pltpu.make_async_copy(view(ins[a], me), outs[a].at[me], local_sems.at[a]) for a in range(na)]
    copies = []
    for k in range(1, N_DEV):
        peer = (_flip(x, k & 4), _flip(y, k & 2), _flip(c, k & 1))
        for a in range(na):
            copies.append(pltpu.make_async_remote_copy(
                src_ref=view(ins[a], _slot(*peer)), dst_ref=outs[a].at[me],
                send_sem=send_sems.at[7 * a + k - 1], recv_sem=recv_sems.at[7 * a + k - 1],
                device_id=peer, device_id_type=MESH))

    def start():
        for cp in mine + copies:
            cp.start()

    def finish():
        for cp in copies:
            cp.wait_recv()
        for cp in copies:
            cp.wait_send()
        for cp in mine:
            cp.wait()

    return start, finish


def _forward_in(me_arr, x, mod, prm, w_in_loc, tm, shards):
    t, d = x.shape
    n = w_in_loc.shape[1]
    ns = N_DEV
    na = len(shards)
    nt = t // tm

    def body(*refs):
        x_ref, mod_ref, prm_ref, wloc_ref = refs[1:5]
        proj_ref, h_ref, wg_ref = refs[5 + na:8 + na]
        hall_ref, wv_ref, wv_sem = refs[8 + 2 * na:11 + 2 * na]
        sems = refs[11 + 2 * na:]
        s, i = pl.program_id(0), pl.program_id(1)
        columns = _column_block(n)

        def gathers():
            return (_Gather([wloc_ref], [wg_ref], *sems[0:3], view=columns),
                    _Gather(refs[5:5 + na], refs[8 + na:8 + 2 * na], *sems[3:6]))

        def load_shard(src):
            cp = pltpu.make_async_copy(src, wv_ref, wv_sem)
            cp.start()
            cp.wait()

        @pl.when(i == 0)
        def _():
            g_in, g_rest = gathers()
            me = _slot(*_my_coords())

            @pl.when(s == 0)
            def _():
                g_in.start()
                g_rest.start()
                load_shard(wloc_ref)

            arrivals = {1: g_in.wait_sibling_own,
                        2: functools.partial(g_in.forward_chip, 1), 3: functools.partial(g_in.wait_sibling_passed, 1),
                        4: functools.partial(g_in.forward_chip, 0), 5: functools.partial(g_in.wait_sibling_passed, 0),
                        6: functools.partial(g_in.forward_chip, 2), 7: functools.partial(g_in.wait_sibling_passed, 2)}
            for step, arrive in arrivals.items():
                @pl.when(s == step)
                def _(step=step, arrive=arrive):
                    arrive()
                    if step == 6:
                        g_rest.forward()
                    load_shard(columns(wg_ref, me ^ step))

        @pl.when(s == 0)
        def _():
            xv = x_ref[...]
            r = lax.rsqrt(_rowmean(xv * xv) + EPS)
            h = xv * r * prm_ref[P_GMIX:P_GMIX + 1, :] * (1.0 + mod_ref[M_SC1:M_SC1 + 1, :]) \
                + mod_ref[M_SH1:M_SH1 + 1, :]
            hb = h.astype(BF16)
            hall_ref[i] = hb
            h_ref[...] = hb

        proj_ref[...] = _dot(hall_ref[i], wv_ref[...]).astype(BF16)

        @pl.when((s == ns - 1) & (i == nt - 1))
        def _():
            g_in, g_rest = gathers()
            g_in.finish_sends()
            g_rest.finish()

    x_tile = pl.BlockSpec((tm, d), lambda s, i, me: (jnp.where(s == 0, i, nt - 1), 0))
    whole = pl.BlockSpec((8, d), lambda s, i, me: (0, 0))
    res = pl.pallas_call(
        body, name="forward_in",
        grid_spec=pltpu.PrefetchScalarGridSpec(
            num_scalar_prefetch=1, grid=(ns, nt),
            in_specs=[x_tile, whole, whole, ANY_SPEC] + [ANY_SPEC] * na,
            out_specs=[pl.BlockSpec((tm, n), lambda s, i, me: (i, me[0] ^ s)), x_tile, ANY_SPEC] + [ANY_SPEC] * na,
            scratch_shapes=[pltpu.VMEM((nt, tm, d), BF16), pltpu.VMEM((d, n), BF16), pltpu.SemaphoreType.DMA]
            + _comm_scratch(1) + _comm_scratch(na)),
        out_shape=(jax.ShapeDtypeStruct((t, ns * n), BF16), jax.ShapeDtypeStruct((t, d), BF16),
                   jax.ShapeDtypeStruct((d, ns * n), BF16))
        + tuple(jax.ShapeDtypeStruct((N_DEV,) + a.shape, a.dtype) for a in shards),
        compiler_params=_params(("arbitrary", "arbitrary")),
    )(me_arr, x, mod, prm, w_in_loc, *shards)
    return res[0], res[1], res[2], res[3:]


def _forward_mix(x, proj, mod, prm, taps, w_so, w_co, w_o, tm, shards):
    t, d = x.shape
    nt = t // tm
    na = len(shards)
    hs, hc = HALO_SHORT, HALO_CONF
    r3, r31 = 0, 8

    def body(*refs):
        x_ref, proj_ref, mod_ref, prm_ref, taps_ref, wso_ref, wco_ref, wo_ref = refs[:8]
        x1_ref, ya_ref, yb_ref, mix_ref, u1_ref = refs[8 + na:13 + na]
        cv_ext, u0_ext, conv3_ref, u1f_ref = refs[13 + 2 * na:17 + 2 * na]
        gather = _Gather(refs[8:8 + na], refs[13 + na:13 + 2 * na], *refs[17 + 2 * na:])
        i = pl.program_id(0)

        @pl.when(i == 0)
        def _():
            gather.start()
            cv_ext[0:hs, :] = jnp.zeros((hs, d), F32)
            u0_ext[0:hc, :] = jnp.zeros((hc, d), F32)

        @pl.when(i == (3 * nt) // 4)
        def _():
            gather.forward()

        def col(g):
            return proj_ref[:, g * d:(g + 1) * d].astype(F32)

        cv_ext[hs:hs + tm, :] = col(1) * col(2)
        u0_ext[hc:hc + tm, :] = col(3) * _sigmoid(col(4))
        _tap_conv(cv_ext, taps_ref, [(r3 + wi, off) for wi, off in _causal_taps(SHORT_K, hs)], tm, hs, conv3_ref)
        _tap_conv(u0_ext, taps_ref, [(r31 + wi, off) for wi, off in _causal_taps(CONF_K, hc)], tm, hc, u1f_ref)
        cv_ext[0:hs, :] = cv_ext[tm:tm + hs, :]
        u0_ext[0:hc, :] = u0_ext[tm:tm + hc, :]

        ya_pre = (col(0) * conv3_ref[...]).astype(BF16)
        y_a = _dot(ya_pre, wso_ref[...])
        u1 = u1f_ref[...] + prm_ref[P_CBIAS:P_CBIAS + 1, :]
        u1_ref[...] = u1.astype(BF16)
        u1 = u1_ref[...].astype(F32)
        mu = _rowmean(u1)
        uc = u1 - mu
        rstd = lax.rsqrt(_rowmean(uc * uc) + LN_EPS)
        u2 = uc * rstd * prm_ref[P_LNG:P_LNG + 1, :] + prm_ref[P_LNB:P_LNB + 1, :]
        u3 = (u2 * _sigmoid(u2)).astype(BF16)
        y_b = _dot(u3, wco_ref[...])
        ya_ref[...] = y_a.astype(BF16)
        yb_ref[...] = y_b.astype(BF16)
        merged = _sigmoid(col(5)) * ya_ref[...].astype(F32) + _sigmoid(col(6)) * yb_ref[...].astype(F32)
        mix = _dot(merged.astype(BF16), wo_ref[...])
        mix_ref[...] = mix.astype(BF16)
        x1_ref[...] = x_ref[...] + mod_ref[M_G1:M_G1 + 1, :] * mix

        @pl.when(i == nt - 1)
        def _():
            gather.finish()

    tile = pl.BlockSpec((tm, d), lambda i: (i, 0))
    whole = lambda shape: pl.BlockSpec(shape, lambda i: (0,) * len(shape))
    res = pl.pallas_call(
        body, name="forward_mix", grid=(nt,),
        out_shape=(jax.ShapeDtypeStruct((t, d), F32),) + (jax.ShapeDtypeStruct((t, d), BF16),) * 4
        + tuple(jax.ShapeDtypeStruct((N_DEV,) + a.shape, a.dtype) for a in shards),
        in_specs=[tile, pl.BlockSpec((tm, 7 * d), lambda i: (i, 0)), whole((8, d)), whole((8, d)),
                  whole(taps.shape), whole((d, d)), whole((d, d)), whole((d, d))] + [ANY_SPEC] * na,
        out_specs=[tile] * 5 + [ANY_SPEC] * na,
        scratch_shapes=[pltpu.VMEM((hs + tm, d), F32), pltpu.VMEM((hc + tm, d), F32),
                        pltpu.VMEM((tm, d), F32), pltpu.VMEM((tm, d), F32)] + _comm_scratch(na),
        compiler_params=_params(("arbitrary",)),
    )(x, proj, mod, prm, taps, w_so, w_co, w_o, *shards)
    return res[:5], res[5:]


def _ffn_chunks(ff):
    mxu = 2 * LANES
    cut = (ff // mxu + 1) // 2 * mxu
    return [(0, cut), (cut, ff)] if 0 < cut < ff and ff % mxu == 0 else [(0, ff)]


def _forward_ffn(x1, tgt, mod, prm, w_fi_t, w_fo, tm):
    t, d = x1.shape
    ff = w_fo.shape[0]

    def body(x1_ref, tgt_ref, mod_ref, prm_ref, wfi_hbm, wfo_hbm,
             dx2_ref, ab_ref, h2_ref, sums_ref, wfi_ref, wfo_ref):
        i = pl.program_id(0)

        @pl.when(i == 0)
        def _():
            pltpu.sync_copy(wfi_hbm, wfi_ref)
            pltpu.sync_copy(wfo_hbm, wfo_ref)
            sums_ref[...] = jnp.zeros((8, d), F32)

        x1v = x1_ref[...]
        r2 = lax.rsqrt(_rowmean(x1v * x1v) + EPS)
        h2 = (x1v * r2 * prm_ref[P_GFFN:P_GFFN + 1, :] * (1.0 + mod_ref[M_SC2:M_SC2 + 1, :])
              + mod_ref[M_SH2:M_SH2 + 1, :]).astype(BF16)
        h2_ref[...] = h2
        f = jnp.zeros((tm, d), F32)
        for c0, c1 in _ffn_chunks(ff):
            ab_ref[:, c0:c1] = _dot_nt(h2, wfi_ref[c0:c1, :]).astype(BF16)
            ab_ref[:, ff + c0:ff + c1] = _dot_nt(h2, wfi_ref[ff + c0:ff + c1, :]).astype(BF16)
            a = ab_ref[:, c0:c1].astype(F32)
            act = (a * _sigmoid(a) * ab_ref[:, ff + c0:ff + c1].astype(F32)).astype(BF16)
            f = f + _dot(act, wfo_ref[c0:c1, :])
        x2 = x1v + mod_ref[M_G2:M_G2 + 1, :] * f
        r3 = lax.rsqrt(_rowmean(x2 * x2) + EPS)
        xn3 = x2 * r3
        gfin = prm_ref[P_GFIN:P_GFIN + 1, :]
        err = xn3 * gfin - tgt_ref[...]
        dy = err * (1.0 / d)
        dxn3 = dy * gfin
        dx2 = r3 * (dxn3 - xn3 * _rowmean(dxn3 * xn3))
        dx2_ref[...] = dx2
        sums_ref[0:1, :] = sums_ref[0:1, :] + _colsum(dy * xn3)
        sums_ref[1:2, :] = sums_ref[1:2, :] + _colsum(dx2 * f)
        sums_ref[2:3, :] = sums_ref[2:3, :] + _colsum(err * err) * (0.5 / d)

    tile = pl.BlockSpec((tm, d), lambda i: (i, 0))
    whole = lambda shape: pl.BlockSpec(shape, lambda i: (0,) * len(shape))
    return pl.pallas_call(
        body, name="forward_ffn", grid=(t // tm,),
        out_shape=(jax.ShapeDtypeStruct((t, d), F32), jax.ShapeDtypeStruct((t, 2 * ff), BF16),
                   jax.ShapeDtypeStruct((t, d), BF16), jax.ShapeDtypeStruct((8, d), F32)),
        in_specs=[tile, tile, whole((8, d)), whole((8, d)), ANY_SPEC, ANY_SPEC],
        out_specs=[tile, pl.BlockSpec((tm, 2 * ff), lambda i: (i, 0)), tile, whole((8, d))],
        scratch_shapes=[pltpu.VMEM(w_fi_t.shape, BF16), pltpu.VMEM(w_fo.shape, BF16)],
        compiler_params=_params(("arbitrary",)),
    )(x1, tgt, mod, prm, w_fi_t, w_fo)


def _backward_ffn(dx2, x1, ab, mod, prm, w_fi_t, w_fo, tm):
    t, d = x1.shape
    ff = w_fo.shape[0]

    def body(dx2_ref, x1_ref, ab_ref, mod_ref, prm_ref, wfi_hbm, wfo_hbm,
             dx1_ref, df_ref, act_ref, dab_ref, sums_ref, wfi_ref, wfo_ref):
        i = pl.program_id(0)

        @pl.when(i == 0)
        def _():
            pltpu.sync_copy(wfi_hbm, wfi_ref)
            pltpu.sync_copy(wfo_hbm, wfo_ref)
            sums_ref[...] = jnp.zeros((8, d), F32)

        dx2v = dx2_ref[...]
        df = (mod_ref[M_G2:M_G2 + 1, :] * dx2v).astype(BF16)
        df_ref[...] = df
        dh2 = jnp.zeros((tm, d), F32)
        for c0, c1 in _ffn_chunks(ff):
            dact = _dot_nt(df, wfo_ref[c0:c1, :])
            a = ab_ref[:, c0:c1].astype(F32)
            b = ab_ref[:, ff + c0:ff + c1].astype(F32)
            s = _sigmoid(a)
            sil = a * s
            act_ref[:, c0:c1] = (sil * b).astype(BF16)
            da = (dact * b * (s * (1.0 + a * (1.0 - s)))).astype(BF16)
            db = (dact * sil).astype(BF16)
            dab_ref[:, c0:c1] = da
            dab_ref[:, ff + c0:ff + c1] = db
            dh2 = dh2 + _dot(da, wfi_ref[c0:c1, :]) + _dot(db, wfi_ref[ff + c0:ff + c1, :])
        x1v = x1_ref[...]
        r2 = lax.rsqrt(_rowmean(x1v * x1v) + EPS)
        xn2 = x1v * r2
        gffn = prm_ref[P_GFFN:P_GFFN + 1, :]
        scale = 1.0 + mod_ref[M_SC2:M_SC2 + 1, :]
        dxn2 = dh2 * gffn * scale
        dx1_ref[...] = dx2v + r2 * (dxn2 - xn2 * _rowmean(dxn2 * xn2))
        hx = dh2 * xn2
        sums_ref[0:1, :] = sums_ref[0:1, :] + _colsum(dh2)
        sums_ref[1:2, :] = sums_ref[1:2, :] + _colsum(hx) * gffn
        sums_ref[2:3, :] = sums_ref[2:3, :] + _colsum(hx) * scale

    tile = pl.BlockSpec((tm, d), lambda i: (i, 0))
    whole = lambda shape: pl.BlockSpec(shape, lambda i: (0,) * len(shape))
    wide = lambda n: pl.BlockSpec((tm, n), lambda i: (i, 0))
    return pl.pallas_call(
        body, name="backward_ffn", grid=(t // tm,),
        out_shape=(jax.ShapeDtypeStruct((t, d), F32), jax.ShapeDtypeStruct((t, d), BF16),
                   jax.ShapeDtypeStruct((t, ff), BF16), jax.ShapeDtypeStruct((t, 2 * ff), BF16),
                   jax.ShapeDtypeStruct((8, d), F32)),
        in_specs=[tile, tile, wide(2 * ff), whole((8, d)), whole((8, d)), ANY_SPEC, ANY_SPEC],
        out_specs=[tile, tile, wide(ff), wide(2 * ff), whole((8, d))],
        scratch_shapes=[pltpu.VMEM(w_fi_t.shape, BF16), pltpu.VMEM(w_fo.shape, BF16)],
        compiler_params=_params(("arbitrary",)),
    )(dx2, x1, ab, mod, prm, w_fi_t, w_fo)


def _backward_mix(dx1, proj, y_a, y_b, mix, u1, mod, prm, taps, w_so, w_co, w_o, tm, partials):
    t, d = dx1.shape
    nt = t // tm
    na = len(partials)
    hs, hc = HALO_SHORT, HALO_CONF
    r3, r31 = 0, 8

    def body(*refs):
        (dx1_ref, proj_ref, halo_ref, ya_ref, yb_ref, mix_ref, u1_ref, mod_ref, prm_ref, taps_ref,
         wso_ref, wco_ref, wo_ref) = refs[:13]
        (dproj_ref, dmix_ref, dya_ref, dyb_ref, merged_ref, yapre_ref, u3_ref, sums_ref, dw3_ref,
         dw31_ref) = refs[13 + na:23 + na]
        cv_ext, u0_ext, d3_ext, du1_ext, tmp_ref = refs[23 + 2 * na:28 + 2 * na]
        scatter_start, scatter_finish = _scatter_plan(refs[13:13 + na], refs[23 + na:23 + 2 * na], *refs[28 + 2 * na:])
        i = pl.program_id(0)
        first_tile = i == nt - 1

        @pl.when(i == 0)
        def _():
            scatter_start()
            sums_ref[...] = jnp.zeros((8, d), F32)
            dw3_ref[...] = jnp.zeros(dw3_ref.shape, F32)
            dw31_ref[...] = jnp.zeros(dw31_ref.shape, F32)
            d3_ext[tm:tm + hs, :] = jnp.zeros((hs, d), F32)
            du1_ext[tm:tm + hc, :] = jnp.zeros((hc, d), F32)

        def col(g):
            return proj_ref[:, g * d:(g + 1) * d].astype(F32)

        def hcol(g, rows):
            v = halo_ref[HALO_CONF - rows:HALO_CONF, g * d:(g + 1) * d].astype(F32)
            return jnp.where(first_tile, 0.0, v)

        dx1v = dx1_ref[...]
        mixv = mix_ref[...].astype(F32)
        dmix = (mod_ref[M_G1:M_G1 + 1, :] * dx1v).astype(BF16)
        dmix_ref[...] = dmix
        sums_ref[0:1, :] = sums_ref[0:1, :] + _colsum(dx1v * mixv)
        dmerged = _dot_nt(dmix, wo_ref[...])
        ga = _sigmoid(col(5))
        gb = _sigmoid(col(6))
        yav = ya_ref[...].astype(F32)
        ybv = yb_ref[...].astype(F32)
        dya = (dmerged * ga).astype(BF16)
        dyb = (dmerged * gb).astype(BF16)
        dya_ref[...] = dya
        dyb_ref[...] = dyb
        dproj_ref[:, 5 * d:6 * d] = (dmerged * yav * ga * (1.0 - ga)).astype(BF16)
        dproj_ref[:, 6 * d:7 * d] = (dmerged * ybv * gb * (1.0 - gb)).astype(BF16)
        merged_ref[...] = (ga * yav + gb * ybv).astype(BF16)

        dya_pre = _dot_nt(dya, wso_ref[...])
        c_s, v_s, b_s = col(1), col(2), col(0)
        cv_ext[0:hs, :] = hcol(1, hs) * hcol(2, hs)
        cv_ext[hs:hs + tm, :] = c_s * v_s
        _tap_conv(cv_ext, taps_ref, [(r3 + wi, off) for wi, off in _causal_taps(SHORT_K, hs)], tm, hs, tmp_ref)
        conv3 = tmp_ref[...]
        yapre_ref[...] = (b_s * conv3).astype(BF16)
        dproj_ref[:, 0:d] = (dya_pre * conv3).astype(BF16)
        d3_ext[0:tm, :] = dya_pre * b_s
        _tap_wgrad(d3_ext, cv_ext, _causal_taps(SHORT_K, hs), tm, hs, dw3_ref)
        _tap_conv(d3_ext, taps_ref, [(r3 + wi, off) for wi, off in _anticausal_taps(SHORT_K)], tm, hs, tmp_ref)
        dcv = tmp_ref[...]
        dproj_ref[:, d:2 * d] = (dcv * v_s).astype(BF16)
        dproj_ref[:, 2 * d:3 * d] = (dcv * c_s).astype(BF16)
        d3_ext[tm:tm + hs, :] = d3_ext[0:hs, :]

        du3 = _dot_nt(dyb, wco_ref[...])
        u1v = u1_ref[...].astype(F32)
        mu = _rowmean(u1v)
        uc = u1v - mu
        rstd = lax.rsqrt(_rowmean(uc * uc) + LN_EPS)
        uhat = uc * rstd
        lng = prm_ref[P_LNG:P_LNG + 1, :]
        u2 = uhat * lng + prm_ref[P_LNB:P_LNB + 1, :]
        s2 = _sigmoid(u2)
        u3_ref[...] = (u2 * s2).astype(BF16)
        du2 = du3 * (s2 * (1.0 + u2 * (1.0 - s2)))
        sums_ref[1:2, :] = sums_ref[1:2, :] + _colsum(du2 * uhat)
        sums_ref[2:3, :] = sums_ref[2:3, :] + _colsum(du2)
        duhat = du2 * lng
        du1 = rstd * (duhat - _rowmean(duhat) - uhat * _rowmean(duhat * uhat))
        sums_ref[3:4, :] = sums_ref[3:4, :] + _colsum(du1)
        du1_ext[0:tm, :] = du1
        v_c = col(3)
        sg = _sigmoid(col(4))
        u0_ext[0:hc, :] = hcol(3, hc) * _sigmoid(hcol(4, hc))
        u0_ext[hc:hc + tm, :] = v_c * sg
        _tap_wgrad(du1_ext, u0_ext, _causal_taps(CONF_K, hc), tm, hc, dw31_ref)
        _tap_conv(du1_ext, taps_ref, [(r31 + wi, off) for wi, off in _anticausal_taps(CONF_K)], tm, hc, tmp_ref)
        du0 = tmp_ref[...]
        dproj_ref[:, 3 * d:4 * d] = (du0 * sg).astype(BF16)
        dproj_ref[:, 4 * d:5 * d] = (du0 * v_c * sg * (1.0 - sg)).astype(BF16)
        du1_ext[tm:tm + hc, :] = du1_ext[0:hc, :]

        @pl.when(i == nt - 1)
        def _():
            scatter_finish()

    rev = lambda i: (nt - 1 - i, 0)
    tile = pl.BlockSpec((tm, d), rev)
    whole = lambda shape: pl.BlockSpec(shape, lambda i: (0,) * len(shape))
    hblocks = tm // HALO_CONF
    halo = pl.BlockSpec((HALO_CONF, 7 * d), lambda i: (jnp.maximum((nt - 1 - i) * hblocks - 1, 0), 0))
    bf = jax.ShapeDtypeStruct((t, d), BF16)
    res = pl.pallas_call(
        body, name="backward_mix", grid=(nt,),
        out_shape=(jax.ShapeDtypeStruct((t, 7 * d), BF16), bf, bf, bf, bf, bf, bf,
                   jax.ShapeDtypeStruct((8, d), F32),
                   jax.ShapeDtypeStruct((SUBLANES * SHORT_K, d), F32),
                   jax.ShapeDtypeStruct((SUBLANES * CONF_K, d), F32))
        + tuple(jax.ShapeDtypeStruct(p.shape, p.dtype) for p in partials),
        in_specs=[tile, pl.BlockSpec((tm, 7 * d), rev), halo, tile, tile, tile, tile,
                  whole((8, d)), whole((8, d)), whole(taps.shape), whole((d, d)), whole((d, d)), whole((d, d))]
        + [ANY_SPEC] * na,
        out_specs=[pl.BlockSpec((tm, 7 * d), rev), tile, tile, tile, tile, tile, tile,
                   whole((8, d)), whole((SUBLANES * SHORT_K, d)), whole((SUBLANES * CONF_K, d))] + [ANY_SPEC] * na,
        scratch_shapes=[pltpu.VMEM((hs + tm, d), F32), pltpu.VMEM((hc + tm, d), F32),
                        pltpu.VMEM((tm + hs, d), F32), pltpu.VMEM((tm + hc, d), F32),
                        pltpu.VMEM((tm, d), F32)] + _comm_scratch(na),
        compiler_params=_params(("arbitrary",)),
    )(dx1, proj, proj, y_a, y_b, mix, u1, mod, prm, taps, w_so, w_co, w_o, *partials)
    return res[:10], res[10:]


def _backward_in(dproj, x, dx1, mod, prm, w_in_g, tm, partials, partial_view, exchanged_shapes):
    t, d = x.shape
    n_all = w_in_g.shape[1]
    na = len(partials)
    nt = t // tm

    def body(*refs):
        dproj_ref, x_ref, dx1_ref, mod_ref, prm_ref, w_hbm = refs[:6]
        gx_ref, sums_ref = refs[6 + na:8 + na]
        w_ref = refs[8 + 2 * na]
        scatter_start, scatter_finish = _scatter_plan(refs[6:6 + na], refs[8 + na:8 + 2 * na], *refs[9 + 2 * na:],
                                                      view=partial_view)

        @pl.when(pl.program_id(0) == 0)
        def _():
            scatter_start()
            pltpu.sync_copy(w_hbm, w_ref)
            sums_ref[...] = jnp.zeros((8, d), F32)

        dh = _dot_nt(dproj_ref[...], w_ref[...])
        xv = x_ref[...]
        r1 = lax.rsqrt(_rowmean(xv * xv) + EPS)
        xn = xv * r1
        gmix = prm_ref[P_GMIX:P_GMIX + 1, :]
        scale = 1.0 + mod_ref[M_SC1:M_SC1 + 1, :]
        dxn = dh * gmix * scale
        gx_ref[...] = dx1_ref[...] + r1 * (dxn - xn * _rowmean(dxn * xn))
        hx = dh * xn
        sums_ref[0:1, :] = sums_ref[0:1, :] + _colsum(dh)
        sums_ref[1:2, :] = sums_ref[1:2, :] + _colsum(hx) * gmix
        sums_ref[2:3, :] = sums_ref[2:3, :] + _colsum(hx) * scale

        @pl.when(pl.program_id(0) == nt - 1)
        def _():
            scatter_finish()

    tile = pl.BlockSpec((tm, d), lambda i: (i, 0))
    whole = pl.BlockSpec((8, d), lambda i: (0, 0))
    res = pl.pallas_call(
        body, name="backward_in", grid=(nt,),
        out_shape=(jax.ShapeDtypeStruct((t, d), F32), jax.ShapeDtypeStruct((8, d), F32))
        + tuple(jax.ShapeDtypeStruct(s, p.dtype) for p, s in zip(partials, exchanged_shapes)),
        in_specs=[pl.BlockSpec((tm, n_all), lambda i: (i, 0)), tile, tile, whole, whole, ANY_SPEC] + [ANY_SPEC] * na,
        out_specs=[tile, whole] + [ANY_SPEC] * na,
        scratch_shapes=[pltpu.VMEM(w_in_g.shape, BF16)] + _comm_scratch(na),
        compiler_params=_params(("arbitrary",)),
    )(dproj, x, dx1, mod, prm, w_in_g, *partials)
    return res[0], res[1], res[2:]


def _weight_grad(a, b, a_spec, b_spec, ns, m, n, nk, name, partials=(), by_columns=False):
    na = len(partials)

    def body(*refs):
        a_ref, b_ref = refs[:2]
        o_ref = refs[2 + na]
        acc_ref = refs[3 + 2 * na]
        s, k = pl.program_id(0), pl.program_id(1)
        if na:
            scatter_start, scatter_finish = _scatter_plan(refs[2:2 + na], refs[3 + na:3 + 2 * na], *refs[4 + 2 * na:])

            @pl.when((s == 0) & (k == 0))
            def _():
                scatter_start()

        av = a_ref[0] if len(a_ref.shape) == 3 else a_ref[...]
        bv = b_ref[0] if len(b_ref.shape) == 3 else b_ref[...]
        part = _dot_tn(av, bv)

        @pl.when(k == 0)
        def _():
            acc_ref[...] = part

        @pl.when(k > 0)
        def _():
            acc_ref[...] = acc_ref[...] + part

        @pl.when(k == nk - 1)
        def _():
            if by_columns:
                o_ref[...] = acc_ref[...].astype(BF16)
            else:
                o_ref[0] = acc_ref[...].astype(BF16)

        if na:
            @pl.when((s == ns - 1) & (k == nk - 1))
            def _():
                scatter_finish()

    if by_columns:
        out_shape, out_spec = (m, ns * n), pl.BlockSpec((m, n), lambda s, k: (0, s))
    else:
        out_shape, out_spec = (ns, m, n), pl.BlockSpec((1, m, n), lambda s, k: (s, 0, 0))
    res = pl.pallas_call(
        body, name=name, grid=(ns, nk),
        out_shape=(jax.ShapeDtypeStruct(out_shape, BF16),)
        + tuple(jax.ShapeDtypeStruct(p.shape, p.dtype) for p in partials),
        in_specs=[a_spec, b_spec] + [ANY_SPEC] * na,
        out_specs=[out_spec] + [ANY_SPEC] * na,
        scratch_shapes=[pltpu.VMEM((m, n), F32)] + (_comm_scratch(na) if na else []),
        compiler_params=_params(("arbitrary", "arbitrary")),
    )(a, b, *partials)
    return (res[0], res[1:]) if na else res[0]


def _adamw(w, g, m, v):
    m = ADAM_B1 * m + (1.0 - ADAM_B1) * g
    v = ADAM_B2 * v + (1.0 - ADAM_B2) * (g * g)
    m_hat = m / (1.0 - ADAM_B1 ** ADAM_STEP)
    v_hat = v / (1.0 - ADAM_B2 ** ADAM_STEP)
    delta = -ADAM_LR * (m_hat / (jnp.sqrt(v_hat) + ADAM_EPS) + ADAM_WD * w)
    return delta, m, v


def _adamw_shard(parts, w, m, v, tr, name):
    r, c = w.shape

    def body(p_ref, w_ref, m_ref, v_ref, g_ref, d_ref, nm_ref, nv_ref):
        g = p_ref[0].astype(F32)
        for s in range(1, N_DEV):
            g = g + p_ref[s].astype(F32)
        delta, nm, nv = _adamw(w_ref[...], g, m_ref[...], v_ref[...])
        g_ref[...] = g
        d_ref[...] = delta
        nm_ref[...] = nm
        nv_ref[...] = nv

    tile = pl.BlockSpec((tr, c), lambda i: (i, 0))
    return pl.pallas_call(
        body, name=name, grid=(r // tr,),
        out_shape=(jax.ShapeDtypeStruct((r, c), F32),) * 4,
        in_specs=[pl.BlockSpec((N_DEV, tr, c), lambda i: (0, i, 0)), tile, tile, tile],
        out_specs=[tile] * 4,
        compiler_params=_params(("arbitrary",)),
    )(parts, w, m, v)


def _ada_update(sc_all, dmod_cols, w, m, v, tr):
    d, n = w.shape

    def body(sc_ref, dm_ref, w_ref, m_ref, v_ref, g_ref, d_ref, nm_ref, nv_ref):
        g = lax.dot_general(sc_ref[...], dm_ref[...], (((0,), (0,)), ((), ())),
                            preferred_element_type=F32, precision=lax.Precision.HIGHEST)
        delta, nm, nv = _adamw(w_ref[...], g, m_ref[...], v_ref[...])
        g_ref[...] = g
        d_ref[...] = delta
        nm_ref[...] = nm
        nv_ref[...] = nv

    tile = pl.BlockSpec((tr, n), lambda i: (i, 0))
    return pl.pallas_call(
        body, name="ada_update", grid=(d // tr,),
        out_shape=(jax.ShapeDtypeStruct((d, n), F32),) * 4,
        in_specs=[pl.BlockSpec((N_DEV, tr), lambda i: (0, i)), pl.BlockSpec((N_DEV, n), lambda i: (0, 0)),
                  tile, tile, tile],
        out_specs=[tile] * 4,
        compiler_params=_params(("arbitrary",)),
    )(sc_all, dmod_cols, w, m, v)


def _small_exchange(vec, cg):
    l = vec.shape[2]
    rows = cg.shape[1]

    def body(vec_ref, cg_ref, vall_ref, cgr_ref, send_sems, recv_sems):
        x, y, c = _my_coords()
        me = _slot(x, y, c)
        vall_ref[me] = vec_ref[0]
        cgr_ref[me] = cg_ref[me]
        copies = []
        for k in range(1, N_DEV):
            peer = (_flip(x, k & 4), _flip(y, k & 2), _flip(c, k & 1))
            copies.append(pltpu.make_async_remote_copy(
                src_ref=vall_ref.at[me], dst_ref=vall_ref.at[me], send_sem=send_sems.at[k - 1],
                recv_sem=recv_sems.at[k - 1], device_id=peer, device_id_type=MESH))
            copies.append(pltpu.make_async_remote_copy(
                src_ref=cg_ref.at[_slot(*peer)], dst_ref=cgr_ref.at[me], send_sem=send_sems.at[7 + k - 1],
                recv_sem=recv_sems.at[7 + k - 1], device_id=peer, device_id_type=MESH))
        for cp in copies:
            cp.start()
        for cp in copies:
            cp.wait_recv()
        for cp in copies:
            cp.wait_send()

    vm = pl.BlockSpec(memory_space=pltpu.VMEM)
    return pl.pallas_call(
        body, name="small_exchange",
        out_shape=(jax.ShapeDtypeStruct((N_DEV, 1, l), F32), jax.ShapeDtypeStruct((N_DEV, rows, LANES), F32)),
        in_specs=[vm, vm], out_specs=[vm, vm],
        scratch_shapes=[pltpu.SemaphoreType.DMA((14,)), pltpu.SemaphoreType.DMA((14,))],
        compiler_params=_params(),
    )(vec, cg)


def _small_update(vall, cgr, smalls):
    ns = len(smalls)

    def body(*refs):
        vall_ref, cgr_ref = refs[0], refs[1]
        wmv = refs[2:2 + 3 * ns]
        outs = refs[2 + 3 * ns:]
        for p, (_, _, _, lo, hi, kind) in enumerate(smalls):
            w_ref, m_ref, v_ref = wmv[3 * p:3 * p + 3]
            part = (lambda s: vall_ref[s, :, lo:hi]) if kind == "vec" else (lambda s: cgr_ref[s, lo:hi, :])
            g = part(0)
            for s in range(1, N_DEV):
                g = g + part(s)
            delta, nm, nv = _adamw(w_ref[...], g, m_ref[...], v_ref[...])
            for o_ref, val in zip(outs[4 * p:4 * p + 4], (g, delta, nm, nv)):
                o_ref[...] = val

    vm = pl.BlockSpec(memory_space=pltpu.VMEM)
    args = [vall, cgr]
    out_shape = []
    for w, m, v, _, _, _ in smalls:
        args += [w, m, v]
        out_shape += [jax.ShapeDtypeStruct(w.shape, F32)] * 4
    res = pl.pallas_call(
        body, name="small_update",
        out_shape=tuple(out_shape),
        in_specs=[vm] * len(args), out_specs=[vm] * len(out_shape),
        compiler_params=_params(),
    )(*args)
    return [res[4 * p:4 + 4 * p] for p in range(ns)]


def _pick(t, want):
    return want if t % want == 0 else t


def kernel(x, c, w_ada, b_ada, norm_mix_g, w_in, conv_short_w, w_short_out, conv_conf_w, conv_conf_b, conf_ln_g, conf_ln_b, w_conf_out, w_o, norm_ffn_g, w_ffn_in, w_ffn_out, final_norm_g, loss_target, m_w_ada, m_b_ada, m_norm_mix_g, m_w_in, m_conv_short_w, m_w_short_out, m_conv_conf_w, m_conv_conf_b, m_conf_ln_g, m_conf_ln_b, m_w_conf_out, m_w_o, m_norm_ffn_g, m_w_ffn_in, m_w_ffn_out, m_final_norm_g, v_w_ada, v_b_ada, v_norm_mix_g, v_w_in, v_conv_short_w, v_w_short_out, v_conv_conf_w, v_conv_conf_b, v_conf_ln_g, v_conf_ln_b, v_w_conf_out, v_w_o, v_norm_ffn_g, v_w_ffn_in, v_w_ffn_out, v_final_norm_g):
    t, d = x.shape[1], x.shape[2]
    x2 = x.reshape(t, d)
    tgt = loss_target.reshape(t, d)
    me = _slot(*_my_coords())
    tm = _pick(t, 256)
    tm_fwd_in = _pick(t, 1024)
    tk = _pick(t, 2048)

    taps_loc = jnp.zeros((40, LANES), F32)
    taps_loc = taps_loc.at[0:SHORT_K].set(conv_short_w[0]).at[8:8 + CONF_K].set(conv_conf_w[0])
    mod_flat, sc_all3, taps = _ada_forward(c, w_ada[0], b_ada, taps_loc)
    mod = jnp.concatenate([mod_flat.reshape(6, d), jnp.zeros((2, d), F32)], axis=0)
    prm = jnp.concatenate([norm_mix_g, norm_ffn_g, final_norm_g.reshape(1, d), conv_conf_b, conf_ln_g, conf_ln_b,
                           jnp.zeros((2, d), F32)], axis=0)

    n_in = w_in.shape[2]
    nk = t // tk
    tok = pl.BlockSpec((tk, d), lambda s, k: (k, 0))
    rows = d // N_DEV
    frows = w_ffn_out.shape[1]

    proj, h, w_in_g, (w_so_g, w_co_g, w_o_g) = _forward_in(
        jnp.reshape(me, (1,)).astype(jnp.int32), x2, mod, prm, w_in[0].astype(BF16), tm_fwd_in,
        [w_short_out[0].astype(BF16), w_conf_out[0].astype(BF16), w_o[0].astype(BF16)])
    w_so = w_so_g.reshape(d, d)
    w_co = w_co_g.reshape(d, d)
    w_oo = w_o_g.reshape(d, d)
    w_fi_t, m_fi_t, v_fi_t = w_ffn_in[0].T, m_w_ffn_in[0].T, v_w_ffn_in[0].T
    (x1, y_a, y_b, mix, u1), (w_fi_g, w_fo_g) = _forward_mix(
        x2, proj, mod, prm, taps, w_so, w_co, w_oo, tm, [w_fi_t.astype(BF16), w_ffn_out[0].astype(BF16)])
    fb = w_fi_g.shape[1]
    ff = N_DEV * frows
    w_fi_all = w_fi_g.reshape(N_DEV * fb, d)
    w_fo_all = w_fo_g.reshape(ff, d)
    dx2, ab, h2, sums_f = _forward_ffn(x1, tgt, mod, prm, w_fi_all, w_fo_all, tm)

    dx1, df, act, dab, sums_b = _backward_ffn(dx2, x1, ab, mod, prm, w_fi_all, w_fo_all, tm)
    fb2 = 2 * fb
    pair_tok = pl.BlockSpec((tk, fb2), lambda s, k: (k, s))
    g_fi = _weight_grad(dab, h2, pair_tok, tok, N_DEV // 2, fb2, d, nk, "grad_w_ffn_in")
    g_fo = _weight_grad(act, df, pair_tok, tok, N_DEV // 4, fb2, d, nk, "grad_w_ffn_out")
    (dproj, dmix, dya, dyb, merged, ya_pre, u3, sums_m, dw3p, dw31p), (p_fi, p_fo) = _backward_mix(
        dx1, proj, y_a, y_b, mix, u1, mod, prm, taps, w_so, w_co, w_oo, tm,
        [g_fi.reshape(N_DEV, fb, d), g_fo.reshape(N_DEV, frows, d)])
    g_so = _weight_grad(ya_pre, dya, tok, tok, 1, d, d, nk, "grad_w_short_out")
    g_co = _weight_grad(u3, dyb, tok, tok, 1, d, d, nk, "grad_w_conf_out")
    g_oo = _weight_grad(merged, dmix, tok, tok, 1, d, d, nk, "grad_w_o")
    n_blk = _pick(N_DEV * n_in, d)
    tk_in = _pick(t, 4096)
    g_in, (p_so, p_co, p_oo) = _weight_grad(
        h, dproj, pl.BlockSpec((tk_in, d), lambda s, k: (k, 0)), pl.BlockSpec((tk_in, n_blk), lambda s, k: (k, s)),
        N_DEV * n_in // n_blk, d, n_blk, t // tk_in, "grad_w_in",
        [g_so.reshape(N_DEV, rows, d), g_co.reshape(N_DEV, rows, d), g_oo.reshape(N_DEV, rows, d)], by_columns=True)
    grad_x, sums_i, (p_in,) = _backward_in(dproj, x2, dx1, mod, prm, w_in_g, tm, [g_in], _column_block(n_in),
                                           [(N_DEV, d, n_in)])

    up_in = _adamw_shard(p_in, w_in[0], m_w_in[0], v_w_in[0], _pick(d, 256), "adamw_w_in")
    up_so = _adamw_shard(p_so, w_short_out[0], m_w_short_out[0], v_w_short_out[0], rows, "adamw_w_short_out")
    up_co = _adamw_shard(p_co, w_conf_out[0], m_w_conf_out[0], v_w_conf_out[0], rows, "adamw_w_conf_out")
    up_oo = _adamw_shard(p_oo, w_o[0], m_w_o[0], v_w_o[0], rows, "adamw_w_o")
    up_fi = tuple(a.T for a in _adamw_shard(p_fi, w_fi_t, m_fi_t, v_fi_t, fb // 2, "adamw_w_ffn_in"))
    up_fo = _adamw_shard(p_fo, w_ffn_out[0], m_w_ffn_out[0], v_w_ffn_out[0], frows, "adamw_w_ffn_out")

    vec = jnp.concatenate([sums_i[0:2], sums_m[0:1], sums_b[0:2], sums_f[1:2],
                           sums_i[2:3], sums_m[3:4], sums_m[1:3], sums_b[2:3], sums_f[0:1],
                           sums_f[2:3]], axis=0)
    vec = vec.reshape(1, 1, 13 * d)
    dw3 = dw3p.reshape(SHORT_K, SUBLANES, d).sum(axis=1)
    dw31 = dw31p.reshape(CONF_K, SUBLANES, d).sum(axis=1)
    cg = jnp.zeros((40, d), F32).at[0:SHORT_K].set(dw3).at[8:8 + CONF_K].set(dw31)
    cg = cg.reshape(40, N_DEV, LANES).transpose(1, 0, 2)
    fin = lambda a: a.reshape(1, d)
    tap = lambda a: a.reshape(a.shape[1:])
    smalls = [
        (b_ada, m_b_ada, v_b_ada, 0, 6 * d, "vec"),
        (norm_mix_g, m_norm_mix_g, v_norm_mix_g, 6 * d, 7 * d, "vec"),
        (tap(conv_short_w), tap(m_conv_short_w), tap(v_conv_short_w), 0, SHORT_K, "cg"),
        (tap(conv_conf_w), tap(m_conv_conf_w), tap(v_conv_conf_w), 8, 8 + CONF_K, "cg"),
        (conv_conf_b, m_conv_conf_b, v_conv_conf_b, 7 * d, 8 * d, "vec"),
        (conf_ln_g, m_conf_ln_g, v_conf_ln_g, 8 * d, 9 * d, "vec"),
        (conf_ln_b, m_conf_ln_b, v_conf_ln_b, 9 * d, 10 * d, "vec"),
        (norm_ffn_g, m_norm_ffn_g, v_norm_ffn_g, 10 * d, 11 * d, "vec"),
        (fin(final_norm_g), fin(m_final_norm_g), fin(v_final_norm_g), 11 * d, 12 * d, "vec"),
    ]
    vall, cgr = _small_exchange(vec, cg)
    up_small = _small_update(vall, cgr, smalls)
    n_ada = w_ada.shape[2]
    dmod_all = vall.reshape(N_DEV, 13 * d)[:, 0:6 * d]
    dmod_cols = lax.dynamic_slice(dmod_all, (0, me * n_ada), (N_DEV, n_ada))
    up_ada = _ada_update(sc_all3.reshape(N_DEV, d), dmod_cols, w_ada[0], m_w_ada[0], v_w_ada[0], _pick(d, 256))

    loss = jnp.sum(vall.reshape(N_DEV, 13 * d)[:, 12 * d:])

    lead = lambda a: a.reshape((1,) + a.shape)
    ups = [tuple(lead(a) for a in up_ada), up_small[0], up_small[1], tuple(lead(a) for a in up_in),
           tuple(lead(a) for a in up_small[2]), tuple(lead(a) for a in up_so), tuple(lead(a) for a in up_small[3]),
           up_small[4], up_small[5], up_small[6],
           tuple(lead(a) for a in up_co), tuple(lead(a) for a in up_oo), up_small[7],
           tuple(lead(a) for a in up_fi), tuple(lead(a) for a in up_fo),
           tuple(a.reshape(d) for a in up_small[8])]
    grads = [u[0] for u in ups]
    deltas = [u[1] for u in ups]
    new_m = [u[2] for u in ups]
    new_v = [u[3] for u in ups]
    return (loss, grad_x.reshape(1, t, d), *grads, *deltas, *new_m, *new_v)
```

```python
import functools

import jax
import jax.numpy as jnp
from jax import lax
from jax.experimental import pallas as pl
from jax.experimental.pallas import tpu as pltpu

F32 = jnp.float32
BF16 = jnp.bfloat16
MESH = pl.DeviceIdType.MESH

N_DEV = 8
EPS = 1e-6
LN_EPS = 1e-5
SHORT_K = 3
CONF_K = 31
ADAM_LR = 0.001
ADAM_B1 = 0.9
ADAM_B2 = 0.999
ADAM_EPS = 1e-08
ADAM_WD = 0.01
ADAM_STEP = 10

LANES = 128
SUBLANES = 8
CONV_ROWS = 64
HALO_SHORT = 8
HALO_CONF = 32
VMEM_LIMIT = 56 * 1024 * 1024

M_SH1, M_SC1, M_G1, M_SH2, M_SC2, M_G2 = range(6)
P_GMIX, P_GFFN, P_GFIN, P_CBIAS, P_LNG, P_LNB = range(6)


def _params(sem=None, **kw):
    return pltpu.CompilerParams(dimension_semantics=sem, vmem_limit_bytes=VMEM_LIMIT, **kw)


def _sigmoid(v):
    return jax.nn.sigmoid(v)


def _dot(a, b):
    return jnp.dot(a, b, preferred_element_type=F32)


def _dot_nt(a, b):
    return lax.dot_general(a, b, (((1,), (1,)), ((), ())), preferred_element_type=F32)


def _dot_tn(a, b):
    return lax.dot_general(a, b, (((0,), (0,)), ((), ())), preferred_element_type=F32)


def _colsum(v):
    return jnp.sum(v, axis=0, keepdims=True)


def _rowmean(v):
    return jnp.mean(v, axis=-1, keepdims=True)


def _my_coords():
    return lax.axis_index("x"), lax.axis_index("y"), lax.axis_index("c")


def _slot(px, py, pc):
    return 4 * px + 2 * py + pc


def _flip(v, bit):
    return 1 - v if bit else v


def _taps_by_residue(taps):
    by_res = {}
    for wi, off in taps:
        by_res.setdefault(off % SUBLANES, []).append((wi, off // SUBLANES))
    return sorted(by_res.items())


def _tap_conv(ext_ref, w_ref, taps, tm, extra, out_ref):
    d = out_ref.shape[1]
    rb = min(CONV_ROWS, tm)
    wrows = rb + extra
    groups = _taps_by_residue(taps)

    def block(i, carry):
        base = pl.multiple_of(i * rb, SUBLANES)
        for lc in range(d // LANES):
            ls = pl.ds(lc * LANES, LANES)
            win = ext_ref[pl.ds(base, wrows), ls]
            acc = None
            for r, lst in groups:
                sh = win if r == 0 else pltpu.roll(win, wrows - r, 0)
                for wi, q in lst:
                    term = w_ref[wi:wi + 1, ls] * sh[SUBLANES * q:SUBLANES * q + rb, :]
                    acc = term if acc is None else acc + term
            out_ref[pl.ds(base, rb), ls] = acc
        return carry

    lax.fori_loop(0, tm // rb, block, 0)


def _tap_wgrad(a_ref, ext_ref, taps, tm, extra, acc_ref):
    d = a_ref.shape[1]
    rb = min(CONV_ROWS, tm)
    wrows = rb + extra
    groups = _taps_by_residue(taps)

    def block(i, carry):
        base = pl.multiple_of(i * rb, SUBLANES)
        for lc in range(d // LANES):
            ls = pl.ds(lc * LANES, LANES)
            a_blk = a_ref[pl.ds(base, rb), ls]
            win = ext_ref[pl.ds(base, wrows), ls]
            for r, lst in groups:
                sh = win if r == 0 else pltpu.roll(win, wrows - r, 0)
                for wi, q in lst:
                    prod = a_blk * sh[SUBLANES * q:SUBLANES * q + rb, :]
                    part = prod[0:SUBLANES, :]
                    for s in range(1, rb // SUBLANES):
                        part = part + prod[SUBLANES * s:SUBLANES * (s + 1), :]
                    rows = pl.ds(SUBLANES * wi, SUBLANES)
                    acc_ref[rows, ls] = acc_ref[rows, ls] + part
        return carry

    lax.fori_loop(0, tm // rb, block, 0)


def _causal_taps(k, halo):
    return [(i, halo - (k - 1) + i) for i in range(k)]


def _anticausal_taps(k):
    return [(i, (k - 1) - i) for i in range(k)]


def _ada_forward(c, w_ada_loc, b_ada, taps_loc):
    d = c.shape[1]
    nloc = w_ada_loc.shape[1]
    trows = taps_loc.shape[0]

    def body(c_ref, w_ref, b_ref, t_ref, mod_ref, sc_ref, taps_ref,
             part_ref, modrecv_ref, tapsall_ref, send_sems, recv_sems):
        x, y, cc = _my_coords()
        me = _slot(x, y, cc)
        cv = c_ref[...]
        sc_ref[me] = cv * _sigmoid(cv)
        tapsall_ref[me] = t_ref[...]

        def peer_of(k):
            return (_flip(x, k & 4), _flip(y, k & 2), _flip(cc, k & 1))

        def gather_copy(ref, base, k):
            return pltpu.make_async_remote_copy(
                src_ref=ref.at[me], dst_ref=ref.at[me], send_sem=send_sems.at[base + k - 1],
                recv_sem=recv_sems.at[base + k - 1], device_id=peer_of(k), device_id_type=MESH)

        first = [gather_copy(sc_ref, 0, k) for k in range(1, N_DEV)]
        first += [gather_copy(tapsall_ref, 7, k) for k in range(1, N_DEV)]
        for cp in first:
            cp.start()
        for cp in first[:7]:
            cp.wait_recv()
        sc_all = jnp.concatenate([sc_ref[s] for s in range(N_DEV)], axis=0)
        part = jnp.dot(sc_all, w_ref[...], preferred_element_type=F32,
                       precision=lax.Precision.HIGHEST)
        for b in range(N_DEV):
            part_ref[b] = part[b:b + 1, :]
        modrecv_ref[me] = part_ref[me]
        second = []
        for k in range(1, N_DEV):
            px, py, pc = peer_of(k)
            second.append(pltpu.make_async_remote_copy(
                src_ref=part_ref.at[_slot(px, py, pc)], dst_ref=modrecv_ref.at[me],
                send_sem=send_sems.at[14 + k - 1], recv_sem=recv_sems.at[14 + k - 1],
                device_id=(px, py, pc), device_id_type=MESH))
        for cp in second:
            cp.start()
        for cp in second:
            cp.wait_recv()
        mod = jnp.concatenate([modrecv_ref[s] for s in range(N_DEV)], axis=1)
        mod_ref[...] = mod + b_ref[...]
        for cp in first[7:]:
            cp.wait_recv()
        taps_ref[...] = jnp.concatenate([tapsall_ref[s] for s in range(N_DEV)], axis=1)
        for cp in first + second:
            cp.wait_send()

    return pl.pallas_call(
        body, name="ada_forward",
        out_shape=(jax.ShapeDtypeStruct((1, N_DEV * nloc), F32),
                   jax.ShapeDtypeStruct((N_DEV, 1, d), F32),
                   jax.ShapeDtypeStruct((trows, N_DEV * LANES), F32)),
        in_specs=[pl.BlockSpec(memory_space=pltpu.VMEM)] * 4,
        out_specs=[pl.BlockSpec(memory_space=pltpu.VMEM)] * 3,
        scratch_shapes=[pltpu.VMEM((N_DEV, 1, nloc), F32), pltpu.VMEM((N_DEV, 1, nloc), F32),
                        pltpu.VMEM((N_DEV, trows, LANES), F32),
                        pltpu.SemaphoreType.DMA((21,)), pltpu.SemaphoreType.DMA((21,))],
        compiler_params=_params(),
    )(c, w_ada_loc, b_ada, taps_loc)


ANY_SPEC = pl.BlockSpec(memory_space=pl.ANY)


def _comm_scratch(na):
    return [pltpu.SemaphoreType.DMA((7 * na,)), pltpu.SemaphoreType.DMA((7 * na,)), pltpu.SemaphoreType.DMA((na,))]


def _leading_block(ref, slot):
    return ref.at[slot]


def _column_block(width):
    def view(ref, slot):
        return ref.at[:, pl.ds(pl.multiple_of(slot * width, LANES), width)]
    return view


class _Gather:
    def __init__(self, ins, outs, send_sems, recv_sems, local_sems, view=_leading_block):
        self.na = len(ins)
        x, y, c = _my_coords()
        self.c = c
        self.view = view
        self.me, self.sibling = (x, y, c), (x, y, 1 - c)
        self.chips = [(1 - x, y), (x, 1 - y), (1 - x, 1 - y)]
        self.outs, self.send_sems, self.recv_sems = outs, send_sems, recv_sems
        self.mine = [pltpu.make_async_copy(ins[a], view(outs[a], _slot(*self.me)), local_sems.at[a])
                     for a in range(self.na)]
        self.first = []
        for a in range(self.na):
            self.first.append(self._copy(a, 0, self.me, self.sibling, src=ins[a]))
            self.first += [self._copy(a, 1 + j, self.me, (*chip, c), src=ins[a]) for j, chip in enumerate(self.chips)]
        self.passed = [self._copy(a, 4 + j, (*chip, c), self.sibling)
                       for a in range(self.na) for j, chip in enumerate(self.chips)]

    def _copy(self, a, k, block, to, src=None):
        dst = self.view(self.outs[a], _slot(*block))
        return pltpu.make_async_remote_copy(
            src_ref=dst if src is None else src, dst_ref=dst,
            send_sem=self.send_sems.at[7 * a + k], recv_sem=self.recv_sems.at[7 * a + k],
            device_id=to, device_id_type=MESH)

    def start(self):
        for cp in self.mine + self.first:
            cp.start()

    def forward_chip(self, j):
        for a in range(self.na):
            self._copy(a, 1 + j, (*self.chips[j], self.c), self.me).wait_recv()
            self.passed[3 * a + j].start()

    def wait_sibling_own(self):
        for a in range(self.na):
            self._copy(a, 0, self.sibling, self.me).wait_recv()

    def wait_sibling_passed(self, j):
        for a in range(self.na):
            self._copy(a, 4 + j, (*self.chips[j], 1 - self.c), self.me).wait_recv()

    def finish_sends(self):
        for cp in self.first + self.passed:
            cp.wait_send()
        for cp in self.mine:
            cp.wait()

    def forward(self):
        for j in range(3):
            self.forward_chip(j)

    def finish(self):
        self.wait_sibling_own()
        for j in range(3):
            self.wait_sibling_passed(j)
        self.finish_sends()


def _scatter_plan(ins, outs, send_sems, recv_sems, local_sems, view=_leading_block):
    na = len(ins)
    x, y, c = _my_coords()
    me = _slot(x, y, c)
    mine = [pltpu.make_async_copy(view(ins[a], me), outs[a].at[me], local_sems.at[a]) for a in range(na)]
    copies = []
    for k in range(1, N_DEV):
        peer = (_flip(x, k & 4), _flip(y, k & 2), _flip(c, k & 1))
        for a in range(na):
            copies.append(pltpu.make_async_remote_copy(
                src_ref=view(ins[a], _slot(*peer)), dst_ref=outs[a].at[me],
                send_sem=send_sems.at[7 * a + k - 1], recv_sem=recv_sems.at[7 * a + k - 1],
                device_id=peer, device_id_type=MESH))

    def start():
        for cp in mine + copies:
            cp.start()

    def finish():
        for cp in copies:
            cp.wait_recv()
        for cp in copies:
            cp.wait_send()
        for cp in mine:
            cp.wait()

    return start, finish


def _forward_in(me_arr, x, mod, prm, w_in_loc, tm, shards):
    t, d = x.shape
    n = w_in_loc.shape[1]
    ns = N_DEV
    na = len(shards)
    nt = t // tm

    def body(*refs):
        x_ref, mod_ref, prm_ref, wloc_ref = refs[1:5]
        proj_ref, h_ref, wg_ref = refs[5 + na:8 + na]
        hall_ref, wv_ref, wv_sem = refs[8 + 2 * na:11 + 2 * na]
        sems = refs[11 + 2 * na:]
        s, i = pl.program_id(0), pl.program_id(1)
        columns = _column_block(n)

        def gathers():
            return (_Gather([wloc_ref], [wg_ref], *sems[0:3], view=columns),
                    _Gather(refs[5:5 + na], refs[8 + na:8 + 2 * na], *sems[3:6]))

        def load_shard(src):
            cp = pltpu.make_async_copy(src, wv_ref, wv_sem)
            cp.start()
            cp.wait()

        @pl.when(i == 0)
        def _():
            g_in, g_rest = gathers()
            me = _slot(*_my_coords())

            @pl.when(s == 0)
            def _():
                g_in.start()
                load_shard(wloc_ref)

            arrivals = {1: g_in.wait_sibling_own,
                        2: functools.partial(g_in.forward_chip, 1), 3: functools.partial(g_in.wait_sibling_passed, 1),
                        4: functools.partial(g_in.forward_chip, 0), 5: functools.partial(g_in.wait_sibling_passed, 0),
                        6: functools.partial(g_in.forward_chip, 2), 7: functools.partial(g_in.wait_sibling_passed, 2)}
            for step, arrive in arrivals.items():
                @pl.when(s == step)
                def _(step=step, arrive=arrive):
                    arrive()
                    if step == 4:
                        g_rest.start()
                    if step == 7:
                        g_rest.forward()
                    load_shard(columns(wg_ref, me ^ step))

        @pl.when(s == 0)
        def _():
            xv = x_ref[...]
            r = lax.rsqrt(_rowmean(xv * xv) + EPS)
            h = xv * r * prm_ref[P_GMIX:P_GMIX + 1, :] * (1.0 + mod_ref[M_SC1:M_SC1 + 1, :]) \
                + mod_ref[M_SH1:M_SH1 + 1, :]
            hb = h.astype(BF16)
            hall_ref[i] = hb
            h_ref[...] = hb

        proj_ref[...] = _dot(hall_ref[i], wv_ref[...]).astype(BF16)

        @pl.when((s == ns - 1) & (i == nt - 1))
        def _():
            g_in, g_rest = gathers()
            g_in.finish_sends()
            g_rest.finish()

    x_tile = pl.BlockSpec((tm, d), lambda s, i, me: (jnp.where(s == 0, i, nt - 1), 0))
    whole = pl.BlockSpec((8, d), lambda s, i, me: (0, 0))
    res = pl.pallas_call(
        body, name="forward_in",
        grid_spec=pltpu.PrefetchScalarGridSpec(
            num_scalar_prefetch=1, grid=(ns, nt),
            in_specs=[x_tile, whole, whole, ANY_SPEC] + [ANY_SPEC] * na,
            out_specs=[pl.BlockSpec((tm, n), lambda s, i, me: (i, me[0] ^ s)), x_tile, ANY_SPEC] + [ANY_SPEC] * na,
            scratch_shapes=[pltpu.VMEM((nt, tm, d), BF16), pltpu.VMEM((d, n), BF16), pltpu.SemaphoreType.DMA]
            + _comm_scratch(1) + _comm_scratch(na)),
        out_shape=(jax.ShapeDtypeStruct((t, ns * n), BF16), jax.ShapeDtypeStruct((t, d), BF16),
                   jax.ShapeDtypeStruct((d, ns * n), BF16))
        + tuple(jax.ShapeDtypeStruct((N_DEV,) + a.shape, a.dtype) for a in shards),
        compiler_params=_params(("arbitrary", "arbitrary")),
    )(me_arr, x, mod, prm, w_in_loc, *shards)
    return res[0], res[1], res[2], res[3:]


def _forward_mix(x, proj, mod, prm, taps, w_so, w_co, w_o, tm, shards):
    t, d = x.shape
    nt = t // tm
    na = len(shards)
    hs, hc = HALO_SHORT, HALO_CONF
    r3, r31 = 0, 8

    def body(*refs):
        x_ref, proj_ref, mod_ref, prm_ref, taps_ref, wso_ref, wco_ref, wo_ref = refs[:8]
        x1_ref, ya_ref, yb_ref, mix_ref, u1_ref, sg_ref, ga_ref, gb_ref = refs[8 + na:16 + na]
        cv_ext, u0_ext, conv3_ref, u1f_ref = refs[16 + 2 * na:20 + 2 * na]
        gather = _Gather(refs[8:8 + na], refs[16 + na:16 + 2 * na], *refs[20 + 2 * na:])
        i = pl.program_id(0)

        @pl.when(i == 0)
        def _():
            gather.start()
            cv_ext[0:hs, :] = jnp.zeros((hs, d), F32)
            u0_ext[0:hc, :] = jnp.zeros((hc, d), F32)

        @pl.when(i == (3 * nt) // 4)
        def _():
            gather.forward()

        def col(g):
            return proj_ref[:, g * d:(g + 1) * d].astype(F32)

        cv_ext[hs:hs + tm, :] = col(1) * col(2)
        sg_ref[...] = _sigmoid(col(4)).astype(BF16)
        u0_ext[hc:hc + tm, :] = col(3) * sg_ref[...].astype(F32)
        _tap_conv(cv_ext, taps_ref, [(r3 + wi, off) for wi, off in _causal_taps(SHORT_K, hs)], tm, hs, conv3_ref)
        _tap_conv(u0_ext, taps_ref, [(r31 + wi, off) for wi, off in _causal_taps(CONF_K, hc)], tm, hc, u1f_ref)
        cv_ext[0:hs, :] = cv_ext[tm:tm + hs, :]
        u0_ext[0:hc, :] = u0_ext[tm:tm + hc, :]

        ya_pre = (col(0) * conv3_ref[...]).astype(BF16)
        y_a = _dot(ya_pre, wso_ref[...])
        u1 = u1f_ref[...] + prm_ref[P_CBIAS:P_CBIAS + 1, :]
        u1_ref[...] = u1.astype(BF16)
        u1 = u1_ref[...].astype(F32)
        mu = _rowmean(u1)
        uc = u1 - mu
        rstd = lax.rsqrt(_rowmean(uc * uc) + LN_EPS)
        u2 = uc * rstd * prm_ref[P_LNG:P_LNG + 1, :] + prm_ref[P_LNB:P_LNB + 1, :]
        u3 = (u2 * _sigmoid(u2)).astype(BF16)
        y_b = _dot(u3, wco_ref[...])
        ya_ref[...] = y_a.astype(BF16)
        yb_ref[...] = y_b.astype(BF16)
        ga_ref[...] = _sigmoid(col(5)).astype(BF16)
        gb_ref[...] = _sigmoid(col(6)).astype(BF16)
        merged = ga_ref[...].astype(F32) * ya_ref[...].astype(F32) + gb_ref[...].astype(F32) * yb_ref[...].astype(F32)
        mix = _dot(merged.astype(BF16), wo_ref[...])
        mix_ref[...] = mix.astype(BF16)
        x1_ref[...] = x_ref[...] + mod_ref[M_G1:M_G1 + 1, :] * mix

        @pl.when(i == nt - 1)
        def _():
            gather.finish()

    tile = pl.BlockSpec((tm, d), lambda i: (i, 0))
    whole = lambda shape: pl.BlockSpec(shape, lambda i: (0,) * len(shape))
    res = pl.pallas_call(
        body, name="forward_mix", grid=(nt,),
        out_shape=(jax.ShapeDtypeStruct((t, d), F32),) + (jax.ShapeDtypeStruct((t, d), BF16),) * 7
        + tuple(jax.ShapeDtypeStruct((N_DEV,) + a.shape, a.dtype) for a in shards),
        in_specs=[tile, pl.BlockSpec((tm, 7 * d), lambda i: (i, 0)), whole((8, d)), whole((8, d)),
                  whole(taps.shape), whole((d, d)), whole((d, d)), whole((d, d))] + [ANY_SPEC] * na,
        out_specs=[tile] * 8 + [ANY_SPEC] * na,
        scratch_shapes=[pltpu.VMEM((hs + tm, d), F32), pltpu.VMEM((hc + tm, d), F32),
                        pltpu.VMEM((tm, d), F32), pltpu.VMEM((tm, d), F32)] + _comm_scratch(na),
        compiler_params=_params(("arbitrary",)),
    )(x, proj, mod, prm, taps, w_so, w_co, w_o, *shards)
    return res[:8], res[8:]


def _ffn_chunks(ff):
    mxu = 2 * LANES
    cut = (ff // mxu + 1) // 2 * mxu
    return [(0, cut), (cut, ff)] if 0 < cut < ff and ff % mxu == 0 else [(0, ff)]


def _forward_ffn(x1, tgt, mod, prm, w_fi_t, w_fo, tm):
    t, d = x1.shape
    ff = w_fo.shape[0]

    def body(x1_ref, tgt_ref, mod_ref, prm_ref, wfi_hbm, wfo_hbm,
             dx2_ref, ab_ref, h2_ref, sums_ref, wfi_ref, wfo_ref):
        i = pl.program_id(0)

        @pl.when(i == 0)
        def _():
            pltpu.sync_copy(wfi_hbm, wfi_ref)
            pltpu.sync_copy(wfo_hbm, wfo_ref)
            sums_ref[...] = jnp.zeros((8, d), F32)

        x1v = x1_ref[...]
        r2 = lax.rsqrt(_rowmean(x1v * x1v) + EPS)
        h2 = (x1v * r2 * prm_ref[P_GFFN:P_GFFN + 1, :] * (1.0 + mod_ref[M_SC2:M_SC2 + 1, :])
              + mod_ref[M_SH2:M_SH2 + 1, :]).astype(BF16)
        h2_ref[...] = h2
        f = jnp.zeros((tm, d), F32)
        for c0, c1 in _ffn_chunks(ff):
            ab_ref[:, c0:c1] = _dot_nt(h2, wfi_ref[c0:c1, :]).astype(BF16)
            ab_ref[:, ff + c0:ff + c1] = _dot_nt(h2, wfi_ref[ff + c0:ff + c1, :]).astype(BF16)
            a = ab_ref[:, c0:c1].astype(F32)
            act = (a * _sigmoid(a) * ab_ref[:, ff + c0:ff + c1].astype(F32)).astype(BF16)
            f = f + _dot(act, wfo_ref[c0:c1, :])
        x2 = x1v + mod_ref[M_G2:M_G2 + 1, :] * f
        r3 = lax.rsqrt(_rowmean(x2 * x2) + EPS)
        xn3 = x2 * r3
        gfin = prm_ref[P_GFIN:P_GFIN + 1, :]
        err = xn3 * gfin - tgt_ref[...]
        dy = err * (1.0 / d)
        dxn3 = dy * gfin
        dx2 = r3 * (dxn3 - xn3 * _rowmean(dxn3 * xn3))
        dx2_ref[...] = dx2
        sums_ref[0:1, :] = sums_ref[0:1, :] + _colsum(dy * xn3)
        sums_ref[1:2, :] = sums_ref[1:2, :] + _colsum(dx2 * f)
        sums_ref[2:3, :] = sums_ref[2:3, :] + _colsum(err * err) * (0.5 / d)

    tile = pl.BlockSpec((tm, d), lambda i: (i, 0))
    whole = lambda shape: pl.BlockSpec(shape, lambda i: (0,) * len(shape))
    return pl.pallas_call(
        body, name="forward_ffn", grid=(t // tm,),
        out_shape=(jax.ShapeDtypeStruct((t, d), F32), jax.ShapeDtypeStruct((t, 2 * ff), BF16),
                   jax.ShapeDtypeStruct((t, d), BF16), jax.ShapeDtypeStruct((8, d), F32)),
        in_specs=[tile, tile, whole((8, d)), whole((8, d)), ANY_SPEC, ANY_SPEC],
        out_specs=[tile, pl.BlockSpec((tm, 2 * ff), lambda i: (i, 0)), tile, whole((8, d))],
        scratch_shapes=[pltpu.VMEM(w_fi_t.shape, BF16), pltpu.VMEM(w_fo.shape, BF16)],
        compiler_params=_params(("arbitrary",)),
    )(x1, tgt, mod, prm, w_fi_t, w_fo)


def _backward_ffn(dx2, x1, ab, mod, prm, w_fi_t, w_fo, tm):
    t, d = x1.shape
    ff = w_fo.shape[0]

    def body(dx2_ref, x1_ref, ab_ref, mod_ref, prm_ref, wfi_hbm, wfo_hbm,
             dx1_ref, df_ref, act_ref, dab_ref, sums_ref, wfi_ref, wfo_ref):
        i = pl.program_id(0)

        @pl.when(i == 0)
        def _():
            pltpu.sync_copy(wfi_hbm, wfi_ref)
            pltpu.sync_copy(wfo_hbm, wfo_ref)
            sums_ref[...] = jnp.zeros((8, d), F32)

        dx2v = dx2_ref[...]
        df = (mod_ref[M_G2:M_G2 + 1, :] * dx2v).astype(BF16)
        df_ref[...] = df
        dh2 = jnp.zeros((tm, d), F32)
        for c0, c1 in _ffn_chunks(ff):
            dact = _dot_nt(df, wfo_ref[c0:c1, :])
            a = ab_ref[:, c0:c1].astype(F32)
            b = ab_ref[:, ff + c0:ff + c1].astype(F32)
            s = _sigmoid(a)
            sil = a * s
            act_ref[:, c0:c1] = (sil * b).astype(BF16)
            da = (dact * b * (s * (1.0 + a * (1.0 - s)))).astype(BF16)
            db = (dact * sil).astype(BF16)
            dab_ref[:, c0:c1] = da
            dab_ref[:, ff + c0:ff + c1] = db
            dh2 = dh2 + _dot(da, wfi_ref[c0:c1, :]) + _dot(db, wfi_ref[ff + c0:ff + c1, :])
        x1v = x1_ref[...]
        r2 = lax.rsqrt(_rowmean(x1v * x1v) + EPS)
        xn2 = x1v * r2
        gffn = prm_ref[P_GFFN:P_GFFN + 1, :]
        scale = 1.0 + mod_ref[M_SC2:M_SC2 + 1, :]
        dxn2 = dh2 * gffn * scale
        dx1_ref[...] = dx2v + r2 * (dxn2 - xn2 * _rowmean(dxn2 * xn2))
        hx = dh2 * xn2
        sums_ref[0:1, :] = sums_ref[0:1, :] + _colsum(dh2)
        sums_ref[1:2, :] = sums_ref[1:2, :] + _colsum(hx) * gffn
        sums_ref[2:3, :] = sums_ref[2:3, :] + _colsum(hx) * scale

    tile = pl.BlockSpec((tm, d), lambda i: (i, 0))
    whole = lambda shape: pl.BlockSpec(shape, lambda i: (0,) * len(shape))
    wide = lambda n: pl.BlockSpec((tm, n), lambda i: (i, 0))
    return pl.pallas_call(
        body, name="backward_ffn", grid=(t // tm,),
        out_shape=(jax.ShapeDtypeStruct((t, d), F32), jax.ShapeDtypeStruct((t, d), BF16),
                   jax.ShapeDtypeStruct((t, ff), BF16), jax.ShapeDtypeStruct((t, 2 * ff), BF16),
                   jax.ShapeDtypeStruct((8, d), F32)),
        in_specs=[tile, tile, wide(2 * ff), whole((8, d)), whole((8, d)), ANY_SPEC, ANY_SPEC],
        out_specs=[tile, tile, wide(ff), wide(2 * ff), whole((8, d))],
        scratch_shapes=[pltpu.VMEM(w_fi_t.shape, BF16), pltpu.VMEM(w_fo.shape, BF16)],
        compiler_params=_params(("arbitrary",)),
    )(dx2, x1, ab, mod, prm, w_fi_t, w_fo)


def _backward_mix(dx1, proj, y_a, y_b, mix, u1, sg, ga, gb, mod, prm, taps, w_so, w_co, w_o, tm, partials):
    t, d = dx1.shape
    nt = t // tm
    na = len(partials)
    hs, hc = HALO_SHORT, HALO_CONF
    r3, r31 = 0, 8

    def body(*refs):
        (dx1_ref, proj_ref, halo_ref, ya_ref, yb_ref, mix_ref, u1_ref, sg_ref, ga_ref, gb_ref, mod_ref, prm_ref,
         taps_ref, wso_ref, wco_ref, wo_ref) = refs[:16]
        (dproj_ref, dmix_ref, dya_ref, dyb_ref, merged_ref, yapre_ref, u3_ref, sums_ref, dw3_ref,
         dw31_ref) = refs[16 + na:26 + na]
        cv_ext, u0_ext, d3_ext, du1_ext, tmp_ref = refs[26 + 2 * na:31 + 2 * na]
        scatter_start, scatter_finish = _scatter_plan(refs[16:16 + na], refs[26 + na:26 + 2 * na], *refs[31 + 2 * na:])
        i = pl.program_id(0)
        first_tile = i == nt - 1

        @pl.when(i == 0)
        def _():
            scatter_start()
            sums_ref[...] = jnp.zeros((8, d), F32)
            dw3_ref[...] = jnp.zeros(dw3_ref.shape, F32)
            dw31_ref[...] = jnp.zeros(dw31_ref.shape, F32)
            d3_ext[tm:tm + hs, :] = jnp.zeros((hs, d), F32)
            du1_ext[tm:tm + hc, :] = jnp.zeros((hc, d), F32)

        def col(g):
            return proj_ref[:, g * d:(g + 1) * d].astype(F32)

        def hcol(g, rows):
            v = halo_ref[HALO_CONF - rows:HALO_CONF, g * d:(g + 1) * d].astype(F32)
            return jnp.where(first_tile, 0.0, v)

        dx1v = dx1_ref[...]
        mixv = mix_ref[...].astype(F32)
        dmix = (mod_ref[M_G1:M_G1 + 1, :] * dx1v).astype(BF16)
        dmix_ref[...] = dmix
        sums_ref[0:1, :] = sums_ref[0:1, :] + _colsum(dx1v * mixv)
        dmerged = _dot_nt(dmix, wo_ref[...])
        ga = ga_ref[...].astype(F32)
        gb = gb_ref[...].astype(F32)
        yav = ya_ref[...].astype(F32)
        ybv = yb_ref[...].astype(F32)
        dya_f = dmerged * ga
        dyb_f = dmerged * gb
        dya = dya_f.astype(BF16)
        dyb = dyb_f.astype(BF16)
        dya_ref[...] = dya
        dyb_ref[...] = dyb
        dproj_ref[:, 5 * d:6 * d] = (dya_f * yav * (1.0 - ga)).astype(BF16)
        dproj_ref[:, 6 * d:7 * d] = (dyb_f * ybv * (1.0 - gb)).astype(BF16)
        merged_ref[...] = (ga * yav + gb * ybv).astype(BF16)

        dya_pre = _dot_nt(dya, wso_ref[...])
        c_s, v_s, b_s = col(1), col(2), col(0)
        cv_ext[0:hs, :] = hcol(1, hs) * hcol(2, hs)
        cv_ext[hs:hs + tm, :] = c_s * v_s
        _tap_conv(cv_ext, taps_ref, [(r3 + wi, off) for wi, off in _causal_taps(SHORT_K, hs)], tm, hs, tmp_ref)
        conv3 = tmp_ref[...]
        yapre_ref[...] = (b_s * conv3).astype(BF16)
        dproj_ref[:, 0:d] = (dya_pre * conv3).astype(BF16)
        d3_ext[0:tm, :] = dya_pre * b_s
        _tap_wgrad(d3_ext, cv_ext, _causal_taps(SHORT_K, hs), tm, hs, dw3_ref)
        _tap_conv(d3_ext, taps_ref, [(r3 + wi, off) for wi, off in _anticausal_taps(SHORT_K)], tm, hs, tmp_ref)
        dcv = tmp_ref[...]
        dproj_ref[:, d:2 * d] = (dcv * v_s).astype(BF16)
        dproj_ref[:, 2 * d:3 * d] = (dcv * c_s).astype(BF16)
        d3_ext[tm:tm + hs, :] = d3_ext[0:hs, :]

        du3 = _dot_nt(dyb, wco_ref[...])
        u1v = u1_ref[...].astype(F32)
        mu = _rowmean(u1v)
        uc = u1v - mu
        rstd = lax.rsqrt(_rowmean(uc * uc) + LN_EPS)
        uhat = uc * rstd
        lng = prm_ref[P_LNG:P_LNG + 1, :]
        u2 = uhat * lng + prm_ref[P_LNB:P_LNB + 1, :]
        s2 = _sigmoid(u2)
        u3_ref[...] = (u2 * s2).astype(BF16)
        du2 = du3 * (s2 * (1.0 + u2 * (1.0 - s2)))
        sums_ref[1:2, :] = sums_ref[1:2, :] + _colsum(du2 * uhat)
        sums_ref[2:3, :] = sums_ref[2:3, :] + _colsum(du2)
        duhat = du2 * lng
        du1 = rstd * (duhat - _rowmean(duhat) - uhat * _rowmean(duhat * uhat))
        sums_ref[3:4, :] = sums_ref[3:4, :] + _colsum(du1)
        du1_ext[0:tm, :] = du1
        v_c = col(3)
        sg = sg_ref[...].astype(F32)
        u0_ext[0:hc, :] = hcol(3, hc) * _sigmoid(hcol(4, hc)).astype(BF16).astype(F32)
        u0_ext[hc:hc + tm, :] = v_c * sg
        _tap_wgrad(du1_ext, u0_ext, _causal_taps(CONF_K, hc), tm, hc, dw31_ref)
        _tap_conv(du1_ext, taps_ref, [(r31 + wi, off) for wi, off in _anticausal_taps(CONF_K)], tm, hc, tmp_ref)
        dv_c = tmp_ref[...] * sg
        dproj_ref[:, 3 * d:4 * d] = dv_c.astype(BF16)
        dproj_ref[:, 4 * d:5 * d] = (dv_c * v_c * (1.0 - sg)).astype(BF16)
        du1_ext[tm:tm + hc, :] = du1_ext[0:hc, :]

        @pl.when(i == nt - 1)
        def _():
            scatter_finish()

    rev = lambda i: (nt - 1 - i, 0)
    tile = pl.BlockSpec((tm, d), rev)
    whole = lambda shape: pl.BlockSpec(shape, lambda i: (0,) * len(shape))
    hblocks = tm // HALO_CONF
    halo = pl.BlockSpec((HALO_CONF, 7 * d), lambda i: (jnp.maximum((nt - 1 - i) * hblocks - 1, 0), 0))
    bf = jax.ShapeDtypeStruct((t, d), BF16)
    res = pl.pallas_call(
        body, name="backward_mix", grid=(nt,),
        out_shape=(jax.ShapeDtypeStruct((t, 7 * d), BF16), bf, bf, bf, bf, bf, bf,
                   jax.ShapeDtypeStruct((8, d), F32),
                   jax.ShapeDtypeStruct((SUBLANES * SHORT_K, d), F32),
                   jax.ShapeDtypeStruct((SUBLANES * CONF_K, d), F32))
        + tuple(jax.ShapeDtypeStruct(p.shape, p.dtype) for p in partials),
        in_specs=[tile, pl.BlockSpec((tm, 4 * d), rev), halo, tile, tile, tile, tile, tile, tile, tile,
                  whole((8, d)), whole((8, d)), whole(taps.shape), whole((d, d)), whole((d, d)), whole((d, d))]
        + [ANY_SPEC] * na,
        out_specs=[pl.BlockSpec((tm, 7 * d), rev), tile, tile, tile, tile, tile, tile,
                   whole((8, d)), whole((SUBLANES * SHORT_K, d)), whole((SUBLANES * CONF_K, d))] + [ANY_SPEC] * na,
        scratch_shapes=[pltpu.VMEM((hs + tm, d), F32), pltpu.VMEM((hc + tm, d), F32),
                        pltpu.VMEM((tm + hs, d), F32), pltpu.VMEM((tm + hc, d), F32),
                        pltpu.VMEM((tm, d), F32)] + _comm_scratch(na),
        compiler_params=_params(("arbitrary",)),
    )(dx1, proj, proj, y_a, y_b, mix, u1, sg, ga, gb, mod, prm, taps, w_so, w_co, w_o, *partials)
    return res[:10], res[10:]


def _backward_in(dproj, x, dx1, mod, prm, w_in_g, tm, partials, partial_view, exchanged_shapes):
    t, d = x.shape
    n_all = w_in_g.shape[1]
    na = len(partials)
    nt = t // tm

    def body(*refs):
        dproj_ref, x_ref, dx1_ref, mod_ref, prm_ref, w_hbm = refs[:6]
        gx_ref, sums_ref = refs[6 + na:8 + na]
        w_ref = refs[8 + 2 * na]
        scatter_start, scatter_finish = _scatter_plan(refs[6:6 + na], refs[8 + na:8 + 2 * na], *refs[9 + 2 * na:],
                                                      view=partial_view)

        @pl.when(pl.program_id(0) == 0)
        def _():
            scatter_start()
            pltpu.sync_copy(w_hbm, w_ref)
            sums_ref[...] = jnp.zeros((8, d), F32)

        dh = _dot_nt(dproj_ref[...], w_ref[...])
        xv = x_ref[...]
        r1 = lax.rsqrt(_rowmean(xv * xv) + EPS)
        xn = xv * r1
        gmix = prm_ref[P_GMIX:P_GMIX + 1, :]
        scale = 1.0 + mod_ref[M_SC1:M_SC1 + 1, :]
        dxn = dh * gmix * scale
        gx_ref[...] = dx1_ref[...] + r1 * (dxn - xn * _rowmean(dxn * xn))
        hx = dh * xn
        sums_ref[0:1, :] = sums_ref[0:1, :] + _colsum(dh)
        sums_ref[1:2, :] = sums_ref[1:2, :] + _colsum(hx) * gmix
        sums_ref[2:3, :] = sums_ref[2:3, :] + _colsum(hx) * scale

        @pl.when(pl.program_id(0) == nt - 1)
        def _():
            scatter_finish()

    tile = pl.BlockSpec((tm, d), lambda i: (i, 0))
    whole = pl.BlockSpec((8, d), lambda i: (0, 0))
    res = pl.pallas_call(
        body, name="backward_in", grid=(nt,),
        out_shape=(jax.ShapeDtypeStruct((t, d), F32), jax.ShapeDtypeStruct((8, d), F32))
        + tuple(jax.ShapeDtypeStruct(s, p.dtype) for p, s in zip(partials, exchanged_shapes)),
        in_specs=[pl.BlockSpec((tm, n_all), lambda i: (i, 0)), tile, tile, whole, whole, ANY_SPEC] + [ANY_SPEC] * na,
        out_specs=[tile, whole] + [ANY_SPEC] * na,
        scratch_shapes=[pltpu.VMEM(w_in_g.shape, BF16)] + _comm_scratch(na),
        compiler_params=_params(("arbitrary",)),
    )(dproj, x, dx1, mod, prm, w_in_g, *partials)
    return res[0], res[1], res[2:]


def _weight_grad(a, b, a_spec, b_spec, ns, m, n, nk, name, partials=(), by_columns=False):
    na = len(partials)

    def body(*refs):
        a_ref, b_ref = refs[:2]
        o_ref = refs[2 + na]
        acc_ref = refs[3 + 2 * na]
        s, k = pl.program_id(0), pl.program_id(1)
        if na:
            scatter_start, scatter_finish = _scatter_plan(refs[2:2 + na], refs[3 + na:3 + 2 * na], *refs[4 + 2 * na:])

            @pl.when((s == 0) & (k == 0))
            def _():
                scatter_start()

        av = a_ref[0] if len(a_ref.shape) == 3 else a_ref[...]
        bv = b_ref[0] if len(b_ref.shape) == 3 else b_ref[...]
        part = _dot_tn(av, bv)

        @pl.when(k == 0)
        def _():
            acc_ref[...] = part

        @pl.when(k > 0)
        def _():
            acc_ref[...] = acc_ref[...] + part

        @pl.when(k == nk - 1)
        def _():
            if by_columns:
                o_ref[...] = acc_ref[...].astype(BF16)
            else:
                o_ref[0] = acc_ref[...].astype(BF16)

        if na:
            @pl.when((s == ns - 1) & (k == nk - 1))
            def _():
                scatter_finish()

    if by_columns:
        out_shape, out_spec = (m, ns * n), pl.BlockSpec((m, n), lambda s, k: (0, s))
    else:
        out_shape, out_spec = (ns, m, n), pl.BlockSpec((1, m, n), lambda s, k: (s, 0, 0))
    res = pl.pallas_call(
        body, name=name, grid=(ns, nk),
        out_shape=(jax.ShapeDtypeStruct(out_shape, BF16),)
        + tuple(jax.ShapeDtypeStruct(p.shape, p.dtype) for p in partials),
        in_specs=[a_spec, b_spec] + [ANY_SPEC] * na,
        out_specs=[out_spec] + [ANY_SPEC] * na,
        scratch_shapes=[pltpu.VMEM((m, n), F32)] + (_comm_scratch(na) if na else []),
        compiler_params=_params(("arbitrary", "arbitrary")),
    )(a, b, *partials)
    return (res[0], res[1:]) if na else res[0]


def _adamw(w, g, m, v):
    m = ADAM_B1 * m + (1.0 - ADAM_B1) * g
    v = ADAM_B2 * v + (1.0 - ADAM_B2) * (g * g)
    m_hat = m / (1.0 - ADAM_B1 ** ADAM_STEP)
    v_hat = v / (1.0 - ADAM_B2 ** ADAM_STEP)
    delta = -ADAM_LR * (m_hat / (jnp.sqrt(v_hat) + ADAM_EPS) + ADAM_WD * w)
    return delta, m, v


def _adamw_shard(parts, w, m, v, tr, name):
    r, c = w.shape

    def body(p_ref, w_ref, m_ref, v_ref, g_ref, d_ref, nm_ref, nv_ref):
        g = p_ref[0].astype(F32)
        for s in range(1, N_DEV):
            g = g + p_ref[s].astype(F32)
        delta, nm, nv = _adamw(w_ref[...], g, m_ref[...], v_ref[...])
        g_ref[...] = g
        d_ref[...] = delta
        nm_ref[...] = nm
        nv_ref[...] = nv

    tile = pl.BlockSpec((tr, c), lambda i: (i, 0))
    return pl.pallas_call(
        body, name=name, grid=(r // tr,),
        out_shape=(jax.ShapeDtypeStruct((r, c), F32),) * 4,
        in_specs=[pl.BlockSpec((N_DEV, tr, c), lambda i: (0, i, 0)), tile, tile, tile],
        out_specs=[tile] * 4,
        compiler_params=_params(("arbitrary",)),
    )(parts, w, m, v)


def _ada_update(sc_all, dmod_cols, w, m, v, tr):
    d, n = w.shape

    def body(sc_ref, dm_ref, w_ref, m_ref, v_ref, g_ref, d_ref, nm_ref, nv_ref):
        g = lax.dot_general(sc_ref[...], dm_ref[...], (((0,), (0,)), ((), ())),
                            preferred_element_type=F32, precision=lax.Precision.HIGHEST)
        delta, nm, nv = _adamw(w_ref[...], g, m_ref[...], v_ref[...])
        g_ref[...] = g
        d_ref[...] = delta
        nm_ref[...] = nm
        nv_ref[...] = nv

    tile = pl.BlockSpec((tr, n), lambda i: (i, 0))
    return pl.pallas_call(
        body, name="ada_update", grid=(d // tr,),
        out_shape=(jax.ShapeDtypeStruct((d, n), F32),) * 4,
        in_specs=[pl.BlockSpec((N_DEV, tr), lambda i: (0, i)), pl.BlockSpec((N_DEV, n), lambda i: (0, 0)),
                  tile, tile, tile],
        out_specs=[tile] * 4,
        compiler_params=_params(("arbitrary",)),
    )(sc_all, dmod_cols, w, m, v)


def _small_exchange(vec, cg):
    l = vec.shape[2]
    rows = cg.shape[1]

    def body(vec_ref, cg_ref, vall_ref, cgr_ref, send_sems, recv_sems):
        x, y, c = _my_coords()
        me = _slot(x, y, c)
        vall_ref[me] = vec_ref[0]
        cgr_ref[me] = cg_ref[me]
        copies = []
        for k in range(1, N_DEV):
            peer = (_flip(x, k & 4), _flip(y, k & 2), _flip(c, k & 1))
            copies.append(pltpu.make_async_remote_copy(
                src_ref=vall_ref.at[me], dst_ref=vall_ref.at[me], send_sem=send_sems.at[k - 1],
                recv_sem=recv_sems.at[k - 1], device_id=peer, device_id_type=MESH))
            copies.append(pltpu.make_async_remote_copy(
                src_ref=cg_ref.at[_slot(*peer)], dst_ref=cgr_ref.at[me], send_sem=send_sems.at[7 + k - 1],
                recv_sem=recv_sems.at[7 + k - 1], device_id=peer, device_id_type=MESH))
        for cp in copies:
            cp.start()
        for cp in copies:
            cp.wait_recv()
        for cp in copies:
            cp.wait_send()

    vm = pl.BlockSpec(memory_space=pltpu.VMEM)
    return pl.pallas_call(
        body, name="small_exchange",
        out_shape=(jax.ShapeDtypeStruct((N_DEV, 1, l), F32), jax.ShapeDtypeStruct((N_DEV, rows, LANES), F32)),
        in_specs=[vm, vm], out_specs=[vm, vm],
        scratch_shapes=[pltpu.SemaphoreType.DMA((14,)), pltpu.SemaphoreType.DMA((14,))],
        compiler_params=_params(),
    )(vec, cg)


def _small_update(vall, cgr, smalls):
    ns = len(smalls)

    def body(*refs):
        vall_ref, cgr_ref = refs[0], refs[1]
        wmv = refs[2:2 + 3 * ns]
        outs = refs[2 + 3 * ns:]
        for p, (_, _, _, lo, hi, kind) in enumerate(smalls):
            w_ref, m_ref, v_ref = wmv[3 * p:3 * p + 3]
            part = (lambda s: vall_ref[s, :, lo:hi]) if kind == "vec" else (lambda s: cgr_ref[s, lo:hi, :])
            g = part(0)
            for s in range(1, N_DEV):
                g = g + part(s)
            delta, nm, nv = _adamw(w_ref[...], g, m_ref[...], v_ref[...])
            for o_ref, val in zip(outs[4 * p:4 * p + 4], (g, delta, nm, nv)):
                o_ref[...] = val

    vm = pl.BlockSpec(memory_space=pltpu.VMEM)
    args = [vall, cgr]
    out_shape = []
    for w, m, v, _, _, _ in smalls:
        args += [w, m, v]
        out_shape += [jax.ShapeDtypeStruct(w.shape, F32)] * 4
    res = pl.pallas_call(
        body, name="small_update",
        out_shape=tuple(out_shape),
        in_specs=[vm] * len(args), out_specs=[vm] * len(out_shape),
        compiler_params=_params(),
    )(*args)
    return [res[4 * p:4 + 4 * p] for p in range(ns)]


def _pick(t, want):
    return want if t % want == 0 else t


def kernel(x, c, w_ada, b_ada, norm_mix_g, w_in, conv_short_w, w_short_out, conv_conf_w, conv_conf_b, conf_ln_g, conf_ln_b, w_conf_out, w_o, norm_ffn_g, w_ffn_in, w_ffn_out, final_norm_g, loss_target, m_w_ada, m_b_ada, m_norm_mix_g, m_w_in, m_conv_short_w, m_w_short_out, m_conv_conf_w, m_conv_conf_b, m_conf_ln_g, m_conf_ln_b, m_w_conf_out, m_w_o, m_norm_ffn_g, m_w_ffn_in, m_w_ffn_out, m_final_norm_g, v_w_ada, v_b_ada, v_norm_mix_g, v_w_in, v_conv_short_w, v_w_short_out, v_conv_conf_w, v_conv_conf_b, v_conf_ln_g, v_conf_ln_b, v_w_conf_out, v_w_o, v_norm_ffn_g, v_w_ffn_in, v_w_ffn_out, v_final_norm_g):
    t, d = x.shape[1], x.shape[2]
    x2 = x.reshape(t, d)
    tgt = loss_target.reshape(t, d)
    me = _slot(*_my_coords())
    tm = _pick(t, 256)
    tm_fwd_in = _pick(t, 1024)
    tk = _pick(t, 2048)

    taps_loc = jnp.zeros((40, LANES), F32)
    taps_loc = taps_loc.at[0:SHORT_K].set(conv_short_w[0]).at[8:8 + CONF_K].set(conv_conf_w[0])
    mod_flat, sc_all3, taps = _ada_forward(c, w_ada[0], b_ada, taps_loc)
    mod = jnp.concatenate([mod_flat.reshape(6, d), jnp.zeros((2, d), F32)], axis=0)
    prm = jnp.concatenate([norm_mix_g, norm_ffn_g, final_norm_g.reshape(1, d), conv_conf_b, conf_ln_g, conf_ln_b,
                           jnp.zeros((2, d), F32)], axis=0)

    n_in = w_in.shape[2]
    nk = t // tk
    tok = pl.BlockSpec((tk, d), lambda s, k: (k, 0))
    rows = d // N_DEV
    frows = w_ffn_out.shape[1]

    proj, h, w_in_g, (w_so_g, w_co_g, w_o_g) = _forward_in(
        jnp.reshape(me, (1,)).astype(jnp.int32), x2, mod, prm, w_in[0].astype(BF16), tm_fwd_in,
        [w_short_out[0].astype(BF16), w_conf_out[0].astype(BF16), w_o[0].astype(BF16)])
    w_so = w_so_g.reshape(d, d)
    w_co = w_co_g.reshape(d, d)
    w_oo = w_o_g.reshape(d, d)
    w_fi_t, m_fi_t, v_fi_t = w_ffn_in[0].T, m_w_ffn_in[0].T, v_w_ffn_in[0].T
    (x1, y_a, y_b, mix, u1, sg, ga, gb), (w_fi_g, w_fo_g) = _forward_mix(
        x2, proj, mod, prm, taps, w_so, w_co, w_oo, tm, [w_fi_t.astype(BF16), w_ffn_out[0].astype(BF16)])
    fb = w_fi_g.shape[1]
    ff = N_DEV * frows
    w_fi_all = w_fi_g.reshape(N_DEV * fb, d)
    w_fo_all = w_fo_g.reshape(ff, d)
    dx2, ab, h2, sums_f = _forward_ffn(x1, tgt, mod, prm, w_fi_all, w_fo_all, tm)

    dx1, df, act, dab, sums_b = _backward_ffn(dx2, x1, ab, mod, prm, w_fi_all, w_fo_all, tm)
    fb2 = 2 * fb
    pair_tok = pl.BlockSpec((tk, fb2), lambda s, k: (k, s))
    g_fi = _weight_grad(dab, h2, pair_tok, tok, N_DEV // 2, fb2, d, nk, "grad_w_ffn_in")
    g_fo = _weight_grad(act, df, pair_tok, tok, N_DEV // 4, fb2, d, nk, "grad_w_ffn_out")
    (dproj, dmix, dya, dyb, merged, ya_pre, u3, sums_m, dw3p, dw31p), (p_fi, p_fo) = _backward_mix(
        dx1, proj, y_a, y_b, mix, u1, sg, ga, gb, mod, prm, taps, w_so, w_co, w_oo, tm,
        [g_fi.reshape(N_DEV, fb, d), g_fo.reshape(N_DEV, frows, d)])
    g_so = _weight_grad(ya_pre, dya, tok, tok, 1, d, d, nk, "grad_w_short_out")
    g_co = _weight_grad(u3, dyb, tok, tok, 1, d, d, nk, "grad_w_conf_out")
    g_oo = _weight_grad(merged, dmix, tok, tok, 1, d, d, nk, "grad_w_o")
    n_blk = _pick(N_DEV * n_in, d)
    tk_in = _pick(t, 4096)
    g_in, (p_so, p_co, p_oo) = _weight_grad(
        h, dproj, pl.BlockSpec((tk_in, d), lambda s, k: (k, 0)), pl.BlockSpec((tk_in, n_blk), lambda s, k: (k, s)),
        N_DEV * n_in // n_blk, d, n_blk, t // tk_in, "grad_w_in",
        [g_so.reshape(N_DEV, rows, d), g_co.reshape(N_DEV, rows, d), g_oo.reshape(N_DEV, rows, d)], by_columns=True)
    grad_x, sums_i, (p_in,) = _backward_in(dproj, x2, dx1, mod, prm, w_in_g, _pick(t, 512), [g_in], _column_block(n_in),
                                           [(N_DEV, d, n_in)])

    up_in = _adamw_shard(p_in, w_in[0], m_w_in[0], v_w_in[0], _pick(d, 256), "adamw_w_in")
    up_so = _adamw_shard(p_so, w_short_out[0], m_w_short_out[0], v_w_short_out[0], rows, "adamw_w_short_out")
    up_co = _adamw_shard(p_co, w_conf_out[0], m_w_conf_out[0], v_w_conf_out[0], rows, "adamw_w_conf_out")
    up_oo = _adamw_shard(p_oo, w_o[0], m_w_o[0], v_w_o[0], rows, "adamw_w_o")
    up_fi = tuple(a.T for a in _adamw_shard(p_fi, w_fi_t, m_fi_t, v_fi_t, fb // 2, "adamw_w_ffn_in"))
    up_fo = _adamw_shard(p_fo, w_ffn_out[0], m_w_ffn_out[0], v_w_ffn_out[0], frows, "adamw_w_ffn_out")

    vec = jnp.concatenate([sums_i[0:2], sums_m[0:1], sums_b[0:2], sums_f[1:2],
                           sums_i[2:3], sums_m[3:4], sums_m[1:3], sums_b[2:3], sums_f[0:1],
                           sums_f[2:3]], axis=0)
    vec = vec.reshape(1, 1, 13 * d)
    dw3 = dw3p.reshape(SHORT_K, SUBLANES, d).sum(axis=1)
    dw31 = dw31p.reshape(CONF_K, SUBLANES, d).sum(axis=1)
    cg = jnp.zeros((40, d), F32).at[0:SHORT_K].set(dw3).at[8:8 + CONF_K].set(dw31)
    cg = cg.reshape(40, N_DEV, LANES).transpose(1, 0, 2)
    fin = lambda a: a.reshape(1, d)
    tap = lambda a: a.reshape(a.shape[1:])
    smalls = [
        (b_ada, m_b_ada, v_b_ada, 0, 6 * d, "vec"),
        (norm_mix_g, m_norm_mix_g, v_norm_mix_g, 6 * d, 7 * d, "vec"),
        (tap(conv_short_w), tap(m_conv_short_w), tap(v_conv_short_w), 0, SHORT_K, "cg"),
        (tap(conv_conf_w), tap(m_conv_conf_w), tap(v_conv_conf_w), 8, 8 + CONF_K, "cg"),
        (conv_conf_b, m_conv_conf_b, v_conv_conf_b, 7 * d, 8 * d, "vec"),
        (conf_ln_g, m_conf_ln_g, v_conf_ln_g, 8 * d, 9 * d, "vec"),
        (conf_ln_b, m_conf_ln_b, v_conf_ln_b, 9 * d, 10 * d, "vec"),
        (norm_ffn_g, m_norm_ffn_g, v_norm_ffn_g, 10 * d, 11 * d, "vec"),
        (fin(final_norm_g), fin(m_final_norm_g), fin(v_final_norm_g), 11 * d, 12 * d, "vec"),
    ]
    vall, cgr = _small_exchange(vec, cg)
    up_small = _small_update(vall, cgr, smalls)
    n_ada = w_ada.shape[2]
    dmod_all = vall.reshape(N_DEV, 13 * d)[:, 0:6 * d]
    dmod_cols = lax.dynamic_slice(dmod_all, (0, me * n_ada), (N_DEV, n_ada))
    up_ada = _ada_update(sc_all3.reshape(N_DEV, d), dmod_cols, w_ada[0], m_w_ada[0], v_w_ada[0], _pick(d, 256))

    loss = jnp.sum(vall.reshape(N_DEV, 13 * d)[:, 12 * d:])

    lead = lambda a: a.reshape((1,) + a.shape)
    ups = [tuple(lead(a) for a in up_ada), up_small[0], up_small[1], tuple(lead(a) for a in up_in),
           tuple(lead(a) for a in up_small[2]), tuple(lead(a) for a in up_so), tuple(lead(a) for a in up_small[3]),
           up_small[4], up_small[5], up_small[6],
           tuple(lead(a) for a in up_co), tuple(lead(a) for a in up_oo), up_small[7],
           tuple(lead(a) for a in up_fi), tuple(lead(a) for a in up_fo),
           tuple(a.reshape(d) for a in up_small[8])]
    grads = [u[0] for u in ups]
    deltas = [u[1] for u in ups]
    new_m = [u[2] for u in ups]
    new_v = [u[3] for u in ups]
    return (loss, grad_x.reshape(1, t, d), *grads, *deltas, *new_m, *new_v)
```

```python
import functools

import jax
import jax.numpy as jnp
from jax import lax
from jax.experimental import pallas as pl
from jax.experimental.pallas import tpu as pltpu

F32 = jnp.float32
BF16 = jnp.bfloat16
MESH = pl.DeviceIdType.MESH

N_DEV = 8
EPS = 1e-6
LN_EPS = 1e-5
SHORT_K = 3
CONF_K = 31
ADAM_LR = 0.001
ADAM_B1 = 0.9
ADAM_B2 = 0.999
ADAM_EPS = 1e-08
ADAM_WD = 0.01
ADAM_STEP = 10

LANES = 128
SUBLANES = 8
CONV_ROWS = 64
HALO_SHORT = 8
HALO_CONF = 32
VMEM_LIMIT = 56 * 1024 * 1024

M_SH1, M_SC1, M_G1, M_SH2, M_SC2, M_G2 = range(6)
P_GMIX, P_GFFN, P_GFIN, P_CBIAS, P_LNG, P_LNB = range(6)


def _params(sem=None, **kw):
    return pltpu.CompilerParams(dimension_semantics=sem, vmem_limit_bytes=VMEM_LIMIT, **kw)


def _sigmoid(v):
    return jax.nn.sigmoid(v)


def _dot(a, b):
    return jnp.dot(a, b, preferred_element_type=F32)


def _dot_nt(a, b):
    return lax.dot_general(a, b, (((1,), (1,)), ((), ())), preferred_element_type=F32)


def _dot_tn(a, b):
    return lax.dot_general(a, b, (((0,), (0,)), ((), ())), preferred_element_type=F32)


def _colsum(v):
    return jnp.sum(v, axis=0, keepdims=True)


def _rowmean(v):
    return jnp.mean(v, axis=-1, keepdims=True)


def _my_coords():
    return lax.axis_index("x"), lax.axis_index("y"), lax.axis_index("c")


def _slot(px, py, pc):
    return 4 * px + 2 * py + pc


def _flip(v, bit):
    return 1 - v if bit else v


def _taps_by_residue(taps):
    by_res = {}
    for wi, off in taps:
        by_res.setdefault(off % SUBLANES, []).append((wi, off // SUBLANES))
    return sorted(by_res.items())


def _tap_conv(ext_ref, w_ref, taps, tm, extra, out_ref):
    d = out_ref.shape[1]
    rb = min(CONV_ROWS, tm)
    wrows = rb + extra
    groups = _taps_by_residue(taps)

    def block(i, carry):
        base = pl.multiple_of(i * rb, SUBLANES)
        for lc in range(d // LANES):
            ls = pl.ds(lc * LANES, LANES)
            win = ext_ref[pl.ds(base, wrows), ls]
            acc = None
            for r, lst in groups:
                sh = win if r == 0 else pltpu.roll(win, wrows - r, 0)
                for wi, q in lst:
                    term = w_ref[wi:wi + 1, ls] * sh[SUBLANES * q:SUBLANES * q + rb, :]
                    acc = term if acc is None else acc + term
            out_ref[pl.ds(base, rb), ls] = acc
        return carry

    lax.fori_loop(0, tm // rb, block, 0)


def _tap_wgrad(a_ref, ext_ref, taps, tm, extra, acc_ref):
    d = a_ref.shape[1]
    rb = min(CONV_ROWS, tm)
    wrows = rb + extra
    groups = _taps_by_residue(taps)

    def block(i, carry):
        base = pl.multiple_of(i * rb, SUBLANES)
        for lc in range(d // LANES):
            ls = pl.ds(lc * LANES, LANES)
            a_blk = a_ref[pl.ds(base, rb), ls]
            win = ext_ref[pl.ds(base, wrows), ls]
            for r, lst in groups:
                sh = win if r == 0 else pltpu.roll(win, wrows - r, 0)
                for wi, q in lst:
                    prod = a_blk * sh[SUBLANES * q:SUBLANES * q + rb, :]
                    part = prod[0:SUBLANES, :]
                    for s in range(1, rb // SUBLANES):
                        part = part + prod[SUBLANES * s:SUBLANES * (s + 1), :]
                    rows = pl.ds(SUBLANES * wi, SUBLANES)
                    acc_ref[rows, ls] = acc_ref[rows, ls] + part
        return carry

    lax.fori_loop(0, tm // rb, block, 0)


def _causal_taps(k, halo):
    return [(i, halo - (k - 1) + i) for i in range(k)]


def _anticausal_taps(k):
    return [(i, (k - 1) - i) for i in range(k)]


def _ada_forward(c, w_ada_loc, b_ada, taps_loc):
    d = c.shape[1]
    nloc = w_ada_loc.shape[1]
    trows = taps_loc.shape[0]

    def body(c_ref, w_ref, b_ref, t_ref, mod_ref, sc_ref, taps_ref,
             part_ref, modrecv_ref, tapsall_ref, send_sems, recv_sems):
        x, y, cc = _my_coords()
        me = _slot(x, y, cc)
        cv = c_ref[...]
        sc_ref[me] = cv * _sigmoid(cv)
        tapsall_ref[me] = t_ref[...]

        def peer_of(k):
            return (_flip(x, k & 4), _flip(y, k & 2), _flip(cc, k & 1))

        def gather_copy(ref, base, k):
            return pltpu.make_async_remote_copy(
                src_ref=ref.at[me], dst_ref=ref.at[me], send_sem=send_sems.at[base + k - 1],
                recv_sem=recv_sems.at[base + k - 1], device_id=peer_of(k), device_id_type=MESH)

        first = [gather_copy(sc_ref, 0, k) for k in range(1, N_DEV)]
        first += [gather_copy(tapsall_ref, 7, k) for k in range(1, N_DEV)]
        for cp in first:
            cp.start()
        for cp in first[:7]:
            cp.wait_recv()
        sc_all = jnp.concatenate([sc_ref[s] for s in range(N_DEV)], axis=0)
        part = jnp.dot(sc_all, w_ref[...], preferred_element_type=F32,
                       precision=lax.Precision.HIGHEST)
        for b in range(N_DEV):
            part_ref[b] = part[b:b + 1, :]
        modrecv_ref[me] = part_ref[me]
        second = []
        for k in range(1, N_DEV):
            px, py, pc = peer_of(k)
            second.append(pltpu.make_async_remote_copy(
                src_ref=part_ref.at[_slot(px, py, pc)], dst_ref=modrecv_ref.at[me],
                send_sem=send_sems.at[14 + k - 1], recv_sem=recv_sems.at[14 + k - 1],
                device_id=(px, py, pc), device_id_type=MESH))
        for cp in second:
            cp.start()
        for cp in second:
            cp.wait_recv()
        mod = jnp.concatenate([modrecv_ref[s] for s in range(N_DEV)], axis=1)
        mod_ref[...] = mod + b_ref[...]
        for cp in first[7:]:
            cp.wait_recv()
        taps_ref[...] = jnp.concatenate([tapsall_ref[s] for s in range(N_DEV)], axis=1)
        for cp in first + second:
            cp.wait_send()

    return pl.pallas_call(
        body, name="ada_forward",
        out_shape=(jax.ShapeDtypeStruct((1, N_DEV * nloc), F32),
                   jax.ShapeDtypeStruct((N_DEV, 1, d), F32),
                   jax.ShapeDtypeStruct((trows, N_DEV * LANES), F32)),
        in_specs=[pl.BlockSpec(memory_space=pltpu.VMEM)] * 4,
        out_specs=[pl.BlockSpec(memory_space=pltpu.VMEM)] * 3,
        scratch_shapes=[pltpu.VMEM((N_DEV, 1, nloc), F32), pltpu.VMEM((N_DEV, 1, nloc), F32),
                        pltpu.VMEM((N_DEV, trows, LANES), F32),
                        pltpu.SemaphoreType.DMA((21,)), pltpu.SemaphoreType.DMA((21,))],
        compiler_params=_params(),
    )(c, w_ada_loc, b_ada, taps_loc)


ANY_SPEC = pl.BlockSpec(memory_space=pl.ANY)


def _comm_scratch(na):
    return [pltpu.SemaphoreType.DMA((7 * na,)), pltpu.SemaphoreType.DMA((7 * na,)), pltpu.SemaphoreType.DMA((na,))]


def _leading_block(ref, slot):
    return ref.at[slot]


def _column_block(width):
    def view(ref, slot):
        return ref.at[:, pl.ds(pl.multiple_of(slot * width, LANES), width)]
    return view


class _Gather:
    def __init__(self, ins, outs, send_sems, recv_sems, local_sems, view=_leading_block):
        self.na = len(ins)
        x, y, c = _my_coords()
        self.c = c
        self.view = view
        self.me, self.sibling = (x, y, c), (x, y, 1 - c)
        self.chips = [(1 - x, y), (x, 1 - y), (1 - x, 1 - y)]
        self.outs, self.send_sems, self.recv_sems = outs, send_sems, recv_sems
        self.mine = [pltpu.make_async_copy(ins[a], view(outs[a], _slot(*self.me)), local_sems.at[a])
                     for a in range(self.na)]
        self.first = []
        for a in range(self.na):
            self.first.append(self._copy(a, 0, self.me, self.sibling, src=ins[a]))
            self.first += [self._copy(a, 1 + j, self.me, (*chip, c), src=ins[a]) for j, chip in enumerate(self.chips)]
        self.passed = [self._copy(a, 4 + j, (*chip, c), self.sibling)
                       for a in range(self.na) for j, chip in enumerate(self.chips)]

    def _copy(self, a, k, block, to, src=None):
        dst = self.view(self.outs[a], _slot(*block))
        return pltpu.make_async_remote_copy(
            src_ref=dst if src is None else src, dst_ref=dst,
            send_sem=self.send_sems.at[7 * a + k], recv_sem=self.recv_sems.at[7 * a + k],
            device_id=to, device_id_type=MESH)

    def start(self):
        self.start_near()
        self.start_far()

    def start_near(self):
        for cp in self.mine + [cp for n, cp in enumerate(self.first) if n % 4 != 3]:
            cp.start()

    def start_far(self):
        for cp in self.first[3::4]:
            cp.start()

    def forward_chip(self, j):
        for a in range(self.na):
            self._copy(a, 1 + j, (*self.chips[j], self.c), self.me).wait_recv()
            self.passed[3 * a + j].start()

    def wait_sibling_own(self):
        for a in range(self.na):
            self._copy(a, 0, self.sibling, self.me).wait_recv()

    def wait_sibling_passed(self, j):
        for a in range(self.na):
            self._copy(a, 4 + j, (*self.chips[j], 1 - self.c), self.me).wait_recv()

    def finish_sends(self):
        for cp in self.first + self.passed:
            cp.wait_send()
        for cp in self.mine:
            cp.wait()

    def forward(self):
        for j in range(3):
            self.forward_chip(j)

    def finish(self):
        self.wait_sibling_own()
        for j in range(3):
            self.wait_sibling_passed(j)
        self.finish_sends()


def _scatter_plan(ins, outs, send_sems, recv_sems, local_sems, view=_leading_block):
    na = len(ins)
    x, y, c = _my_coords()
    me = _slot(x, y, c)
    mine = [pltpu.make_async_copy(view(ins[a], me), outs[a].at[me], local_sems.at[a]) for a in range(na)]
    copies = []
    for k in range(1, N_DEV):
        peer = (_flip(x, k & 4), _flip(y, k & 2), _flip(c, k & 1))
        for a in range(na):
            copies.append(pltpu.make_async_remote_copy(
                src_ref=view(ins[a], _slot(*peer)), dst_ref=outs[a].at[me],
                send_sem=send_sems.at[7 * a + k - 1], recv_sem=recv_sems.at[7 * a + k - 1],
                device_id=peer, device_id_type=MESH))

    def start():
        for cp in mine + copies:
            cp.start()

    def finish():
        for cp in copies:
            cp.wait_recv()
        for cp in copies:
            cp.wait_send()
        for cp in mine:
            cp.wait()

    return start, finish


def _forward_in(me_arr, x, mod, prm, w_in_loc, tm, shards):
    t, d = x.shape
    n = w_in_loc.shape[1]
    ns = N_DEV
    na = len(shards)
    nt = t // tm

    def body(*refs):
        x_ref, mod_ref, prm_ref, wloc_ref = refs[1:5]
        proj_ref, h_ref, wg_ref = refs[5 + na:8 + na]
        hall_ref, wv_ref, wv_sem = refs[8 + 2 * na:11 + 2 * na]
        sems = refs[11 + 2 * na:]
        s, i = pl.program_id(0), pl.program_id(1)
        columns = _column_block(n)

        def gathers():
            return (_Gather([wloc_ref], [wg_ref], *sems[0:3], view=columns),
                    _Gather(refs[5:5 + na], refs[8 + na:8 + 2 * na], *sems[3:6]))

        def load_shard(src):
            cp = pltpu.make_async_copy(src, wv_ref, wv_sem)
            cp.start()
            cp.wait()

        @pl.when(i == 0)
        def _():
            g_in, g_rest = gathers()
            me = _slot(*_my_coords())

            @pl.when(s == 0)
            def _():
                g_in.start_near()
                load_shard(wloc_ref)

            arrivals = {1: g_in.wait_sibling_own,
                        2: functools.partial(g_in.forward_chip, 1), 3: functools.partial(g_in.wait_sibling_passed, 1),
                        4: functools.partial(g_in.forward_chip, 0), 5: functools.partial(g_in.wait_sibling_passed, 0),
                        6: functools.partial(g_in.forward_chip, 2), 7: functools.partial(g_in.wait_sibling_passed, 2)}
            for step, arrive in arrivals.items():
                @pl.when(s == step)
                def _(step=step, arrive=arrive):
                    arrive()
                    if step == 2:
                        g_in.start_far()
                    if step == 4:
                        g_rest.start()
                    if step == 7:
                        g_rest.forward()
                    load_shard(columns(wg_ref, me ^ step))

        @pl.when(s == 0)
        def _():
            xv = x_ref[...]
            r = lax.rsqrt(_rowmean(xv * xv) + EPS)
            h = xv * r * prm_ref[P_GMIX:P_GMIX + 1, :] * (1.0 + mod_ref[M_SC1:M_SC1 + 1, :]) \
                + mod_ref[M_SH1:M_SH1 + 1, :]
            hb = h.astype(BF16)
            hall_ref[i] = hb
            h_ref[...] = hb

        proj_ref[...] = _dot(hall_ref[i], wv_ref[...]).astype(BF16)

        @pl.when((s == ns - 1) & (i == nt - 1))
        def _():
            g_in, g_rest = gathers()
            g_in.finish_sends()
            g_rest.finish()

    x_tile = pl.BlockSpec((tm, d), lambda s, i, me: (jnp.where(s == 0, i, nt - 1), 0))
    whole = pl.BlockSpec((8, d), lambda s, i, me: (0, 0))
    res = pl.pallas_call(
        body, name="forward_in",
        grid_spec=pltpu.PrefetchScalarGridSpec(
            num_scalar_prefetch=1, grid=(ns, nt),
            in_specs=[x_tile, whole, whole, ANY_SPEC] + [ANY_SPEC] * na,
            out_specs=[pl.BlockSpec((tm, n), lambda s, i, me: (i, me[0] ^ s)), x_tile, ANY_SPEC] + [ANY_SPEC] * na,
            scratch_shapes=[pltpu.VMEM((nt, tm, d), BF16), pltpu.VMEM((d, n), BF16), pltpu.SemaphoreType.DMA]
            + _comm_scratch(1) + _comm_scratch(na)),
        out_shape=(jax.ShapeDtypeStruct((t, ns * n), BF16), jax.ShapeDtypeStruct((t, d), BF16),
                   jax.ShapeDtypeStruct((d, ns * n), BF16))
        + tuple(jax.ShapeDtypeStruct((N_DEV,) + a.shape, a.dtype) for a in shards),
        compiler_params=_params(("arbitrary", "arbitrary")),
    )(me_arr, x, mod, prm, w_in_loc, *shards)
    return res[0], res[1], res[2], res[3:]


def _forward_mix(x, proj, mod, prm, taps, w_so, w_co, w_o, tm, shards):
    t, d = x.shape
    nt = t // tm
    na = len(shards)
    hs, hc = HALO_SHORT, HALO_CONF
    r3, r31 = 0, 8

    def body(*refs):
        x_ref, proj_ref, mod_ref, prm_ref, taps_ref, wso_ref, wco_ref, wo_ref = refs[:8]
        x1_ref, ya_ref, yb_ref, mix_ref, u1_ref = refs[8 + na:13 + na]
        cv_ext, u0_ext, conv3_ref, u1f_ref = refs[13 + 2 * na:17 + 2 * na]
        gather = _Gather(refs[8:8 + na], refs[13 + na:13 + 2 * na], *refs[17 + 2 * na:])
        i = pl.program_id(0)

        @pl.when(i == 0)
        def _():
            gather.start()
            cv_ext[0:hs, :] = jnp.zeros((hs, d), F32)
            u0_ext[0:hc, :] = jnp.zeros((hc, d), F32)

        @pl.when(i == (3 * nt) // 4)
        def _():
            gather.forward()

        def col(g):
            return proj_ref[:, g * d:(g + 1) * d].astype(F32)

        cv_ext[hs:hs + tm, :] = col(1) * col(2)
        u0_ext[hc:hc + tm, :] = col(3) * _sigmoid(col(4))
        _tap_conv(cv_ext, taps_ref, [(r3 + wi, off) for wi, off in _causal_taps(SHORT_K, hs)], tm, hs, conv3_ref)
        _tap_conv(u0_ext, taps_ref, [(r31 + wi, off) for wi, off in _causal_taps(CONF_K, hc)], tm, hc, u1f_ref)
        cv_ext[0:hs, :] = cv_ext[tm:tm + hs, :]
        u0_ext[0:hc, :] = u0_ext[tm:tm + hc, :]

        ya_pre = (col(0) * conv3_ref[...]).astype(BF16)
        y_a = _dot(ya_pre, wso_ref[...])
        u1 = u1f_ref[...] + prm_ref[P_CBIAS:P_CBIAS + 1, :]
        u1_ref[...] = u1.astype(BF16)
        u1 = u1_ref[...].astype(F32)
        mu = _rowmean(u1)
        uc = u1 - mu
        rstd = lax.rsqrt(_rowmean(uc * uc) + LN_EPS)
        u2 = uc * rstd * prm_ref[P_LNG:P_LNG + 1, :] + prm_ref[P_LNB:P_LNB + 1, :]
        u3 = (u2 * _sigmoid(u2)).astype(BF16)
        y_b = _dot(u3, wco_ref[...])
        ya_ref[...] = y_a.astype(BF16)
        yb_ref[...] = y_b.astype(BF16)
        merged = _sigmoid(col(5)) * ya_ref[...].astype(F32) + _sigmoid(col(6)) * yb_ref[...].astype(F32)
        mix = _dot(merged.astype(BF16), wo_ref[...])
        mix_ref[...] = mix.astype(BF16)
        x1_ref[...] = x_ref[...] + mod_ref[M_G1:M_G1 + 1, :] * mix

        @pl.when(i == nt - 1)
        def _():
            gather.finish()

    tile = pl.BlockSpec((tm, d), lambda i: (i, 0))
    whole = lambda shape: pl.BlockSpec(shape, lambda i: (0,) * len(shape))
    res = pl.pallas_call(
        body, name="forward_mix", grid=(nt,),
        out_shape=(jax.ShapeDtypeStruct((t, d), F32),) + (jax.ShapeDtypeStruct((t, d), BF16),) * 4
        + tuple(jax.ShapeDtypeStruct((N_DEV,) + a.shape, a.dtype) for a in shards),
        in_specs=[tile, pl.BlockSpec((tm, 7 * d), lambda i: (i, 0)), whole((8, d)), whole((8, d)),
                  whole(taps.shape), whole((d, d)), whole((d, d)), whole((d, d))] + [ANY_SPEC] * na,
        out_specs=[tile] * 5 + [ANY_SPEC] * na,
        scratch_shapes=[pltpu.VMEM((hs + tm, d), F32), pltpu.VMEM((hc + tm, d), F32),
                        pltpu.VMEM((tm, d), F32), pltpu.VMEM((tm, d), F32)] + _comm_scratch(na),
        compiler_params=_params(("arbitrary",)),
    )(x, proj, mod, prm, taps, w_so, w_co, w_o, *shards)
    return res[:5], res[5:]


def _ffn_chunks(ff):
    mxu = 2 * LANES
    cut = (ff // mxu + 1) // 2 * mxu
    return [(0, cut), (cut, ff)] if 0 < cut < ff and ff % mxu == 0 else [(0, ff)]


def _forward_ffn(x1, tgt, mod, prm, w_fi_t, w_fo, tm):
    t, d = x1.shape
    ff = w_fo.shape[0]

    def body(x1_ref, tgt_ref, mod_ref, prm_ref, wfi_hbm, wfo_hbm,
             dx2_ref, ab_ref, h2_ref, sums_ref, wfi_ref, wfo_ref):
        i = pl.program_id(0)

        @pl.when(i == 0)
        def _():
            pltpu.sync_copy(wfi_hbm, wfi_ref)
            pltpu.sync_copy(wfo_hbm, wfo_ref)
            sums_ref[...] = jnp.zeros((8, d), F32)

        x1v = x1_ref[...]
        r2 = lax.rsqrt(_rowmean(x1v * x1v) + EPS)
        h2 = (x1v * r2 * prm_ref[P_GFFN:P_GFFN + 1, :] * (1.0 + mod_ref[M_SC2:M_SC2 + 1, :])
              + mod_ref[M_SH2:M_SH2 + 1, :]).astype(BF16)
        h2_ref[...] = h2
        f = jnp.zeros((tm, d), F32)
        for c0, c1 in _ffn_chunks(ff):
            ab_ref[:, c0:c1] = _dot_nt(h2, wfi_ref[c0:c1, :]).astype(BF16)
            ab_ref[:, ff + c0:ff + c1] = _dot_nt(h2, wfi_ref[ff + c0:ff + c1, :]).astype(BF16)
            a = ab_ref[:, c0:c1].astype(F32)
            act = (a * _sigmoid(a) * ab_ref[:, ff + c0:ff + c1].astype(F32)).astype(BF16)
            f = f + _dot(act, wfo_ref[c0:c1, :])
        x2 = x1v + mod_ref[M_G2:M_G2 + 1, :] * f
        r3 = lax.rsqrt(_rowmean(x2 * x2) + EPS)
        xn3 = x2 * r3
        gfin = prm_ref[P_GFIN:P_GFIN + 1, :]
        err = xn3 * gfin - tgt_ref[...]
        dy = err * (1.0 / d)
        dxn3 = dy * gfin
        dx2 = r3 * (dxn3 - xn3 * _rowmean(dxn3 * xn3))
        dx2_ref[...] = dx2
        sums_ref[0:1, :] = sums_ref[0:1, :] + _colsum(dy * xn3)
        sums_ref[1:2, :] = sums_ref[1:2, :] + _colsum(dx2 * f)
        sums_ref[2:3, :] = sums_ref[2:3, :] + _colsum(err * err) * (0.5 / d)

    tile = pl.BlockSpec((tm, d), lambda i: (i, 0))
    whole = lambda shape: pl.BlockSpec(shape, lambda i: (0,) * len(shape))
    return pl.pallas_call(
        body, name="forward_ffn", grid=(t // tm,),
        out_shape=(jax.ShapeDtypeStruct((t, d), F32), jax.ShapeDtypeStruct((t, 2 * ff), BF16),
                   jax.ShapeDtypeStruct((t, d), BF16), jax.ShapeDtypeStruct((8, d), F32)),
        in_specs=[tile, tile, whole((8, d)), whole((8, d)), ANY_SPEC, ANY_SPEC],
        out_specs=[tile, pl.BlockSpec((tm, 2 * ff), lambda i: (i, 0)), tile, whole((8, d))],
        scratch_shapes=[pltpu.VMEM(w_fi_t.shape, BF16), pltpu.VMEM(w_fo.shape, BF16)],
        compiler_params=_params(("arbitrary",)),
    )(x1, tgt, mod, prm, w_fi_t, w_fo)


def _backward_ffn(dx2, x1, ab, mod, prm, w_fi_t, w_fo, tm):
    t, d = x1.shape
    ff = w_fo.shape[0]

    def body(dx2_ref, x1_ref, ab_ref, mod_ref, prm_ref, wfi_hbm, wfo_hbm,
             dx1_ref, df_ref, act_ref, dab_ref, sums_ref, wfi_ref, wfo_ref):
        i = pl.program_id(0)

        @pl.when(i == 0)
        def _():
            pltpu.sync_copy(wfi_hbm, wfi_ref)
            pltpu.sync_copy(wfo_hbm, wfo_ref)
            sums_ref[...] = jnp.zeros((8, d), F32)

        dx2v = dx2_ref[...]
        df = (mod_ref[M_G2:M_G2 + 1, :] * dx2v).astype(BF16)
        df_ref[...] = df
        dh2 = jnp.zeros((tm, d), F32)
        for c0, c1 in _ffn_chunks(ff):
            dact = _dot_nt(df, wfo_ref[c0:c1, :])
            a = ab_ref[:, c0:c1].astype(F32)
            b = ab_ref[:, ff + c0:ff + c1].astype(F32)
            s = _sigmoid(a)
            sil = a * s
            act_ref[:, c0:c1] = (sil * b).astype(BF16)
            da = (dact * b * (s * (1.0 + a * (1.0 - s)))).astype(BF16)
            db = (dact * sil).astype(BF16)
            dab_ref[:, c0:c1] = da
            dab_ref[:, ff + c0:ff + c1] = db
            dh2 = dh2 + _dot(da, wfi_ref[c0:c1, :]) + _dot(db, wfi_ref[ff + c0:ff + c1, :])
        x1v = x1_ref[...]
        r2 = lax.rsqrt(_rowmean(x1v * x1v) + EPS)
        xn2 = x1v * r2
        gffn = prm_ref[P_GFFN:P_GFFN + 1, :]
        scale = 1.0 + mod_ref[M_SC2:M_SC2 + 1, :]
        dxn2 = dh2 * gffn * scale
        dx1_ref[...] = dx2v + r2 * (dxn2 - xn2 * _rowmean(dxn2 * xn2))
        hx = dh2 * xn2
        sums_ref[0:1, :] = sums_ref[0:1, :] + _colsum(dh2)
        sums_ref[1:2, :] = sums_ref[1:2, :] + _colsum(hx) * gffn
        sums_ref[2:3, :] = sums_ref[2:3, :] + _colsum(hx) * scale

    tile = pl.BlockSpec((tm, d), lambda i: (i, 0))
    whole = lambda shape: pl.BlockSpec(shape, lambda i: (0,) * len(shape))
    wide = lambda n: pl.BlockSpec((tm, n), lambda i: (i, 0))
    return pl.pallas_call(
        body, name="backward_ffn", grid=(t // tm,),
        out_shape=(jax.ShapeDtypeStruct((t, d), F32), jax.ShapeDtypeStruct((t, d), BF16),
                   jax.ShapeDtypeStruct((t, ff), BF16), jax.ShapeDtypeStruct((t, 2 * ff), BF16),
                   jax.ShapeDtypeStruct((8, d), F32)),
        in_specs=[tile, tile, wide(2 * ff), whole((8, d)), whole((8, d)), ANY_SPEC, ANY_SPEC],
        out_specs=[tile, tile, wide(ff), wide(2 * ff), whole((8, d))],
        scratch_shapes=[pltpu.VMEM(w_fi_t.shape, BF16), pltpu.VMEM(w_fo.shape, BF16)],
        compiler_params=_params(("arbitrary",)),
    )(dx2, x1, ab, mod, prm, w_fi_t, w_fo)


def _backward_mix(dx1, proj, y_a, y_b, mix, u1, mod, prm, taps, w_so, w_co, w_o, tm, partials):
    t, d = dx1.shape
    nt = t // tm
    na = len(partials)
    hs, hc = HALO_SHORT, HALO_CONF
    r3, r31 = 0, 8

    def body(*refs):
        (dx1_ref, proj_ref, halo_ref, ya_ref, yb_ref, mix_ref, u1_ref, mod_ref, prm_ref, taps_ref,
         wso_ref, wco_ref, wo_ref) = refs[:13]
        (dproj_ref, dmix_ref, dya_ref, dyb_ref, merged_ref, yapre_ref, u3_ref, sums_ref, dw3_ref,
         dw31_ref) = refs[13 + na:23 + na]
        cv_ext, u0_ext, d3_ext, du1_ext, tmp_ref = refs[23 + 2 * na:28 + 2 * na]
        scatter_start, scatter_finish = _scatter_plan(refs[13:13 + na], refs[23 + na:23 + 2 * na], *refs[28 + 2 * na:])
        i = pl.program_id(0)
        first_tile = i == nt - 1

        @pl.when(i == 0)
        def _():
            scatter_start()
            sums_ref[...] = jnp.zeros((8, d), F32)
            dw3_ref[...] = jnp.zeros(dw3_ref.shape, F32)
            dw31_ref[...] = jnp.zeros(dw31_ref.shape, F32)
            d3_ext[tm:tm + hs, :] = jnp.zeros((hs, d), F32)
            du1_ext[tm:tm + hc, :] = jnp.zeros((hc, d), F32)

        def col(g):
            return proj_ref[:, g * d:(g + 1) * d].astype(F32)

        def hcol(g, rows):
            v = halo_ref[HALO_CONF - rows:HALO_CONF, g * d:(g + 1) * d].astype(F32)
            return jnp.where(first_tile, 0.0, v)

        dx1v = dx1_ref[...]
        mixv = mix_ref[...].astype(F32)
        dmix = (mod_ref[M_G1:M_G1 + 1, :] * dx1v).astype(BF16)
        dmix_ref[...] = dmix
        sums_ref[0:1, :] = sums_ref[0:1, :] + _colsum(dx1v * mixv)
        dmerged = _dot_nt(dmix, wo_ref[...])
        ga = _sigmoid(col(5))
        gb = _sigmoid(col(6))
        yav = ya_ref[...].astype(F32)
        ybv = yb_ref[...].astype(F32)
        dya_f = dmerged * ga
        dyb_f = dmerged * gb
        dya = dya_f.astype(BF16)
        dyb = dyb_f.astype(BF16)
        dya_ref[...] = dya
        dyb_ref[...] = dyb
        dproj_ref[:, 5 * d:6 * d] = (dya_f * yav * (1.0 - ga)).astype(BF16)
        dproj_ref[:, 6 * d:7 * d] = (dyb_f * ybv * (1.0 - gb)).astype(BF16)
        merged_ref[...] = (ga * yav + gb * ybv).astype(BF16)

        dya_pre = _dot_nt(dya, wso_ref[...])
        c_s, v_s, b_s = col(1), col(2), col(0)
        cv_ext[0:hs, :] = hcol(1, hs) * hcol(2, hs)
        cv_ext[hs:hs + tm, :] = c_s * v_s
        _tap_conv(cv_ext, taps_ref, [(r3 + wi, off) for wi, off in _causal_taps(SHORT_K, hs)], tm, hs, tmp_ref)
        conv3 = tmp_ref[...]
        yapre_ref[...] = (b_s * conv3).astype(BF16)
        dproj_ref[:, 0:d] = (dya_pre * conv3).astype(BF16)
        d3_ext[0:tm, :] = dya_pre * b_s
        _tap_wgrad(d3_ext, cv_ext, _causal_taps(SHORT_K, hs), tm, hs, dw3_ref)
        _tap_conv(d3_ext, taps_ref, [(r3 + wi, off) for wi, off in _anticausal_taps(SHORT_K)], tm, hs, tmp_ref)
        dcv = tmp_ref[...]
        dproj_ref[:, d:2 * d] = (dcv * v_s).astype(BF16)
        dproj_ref[:, 2 * d:3 * d] = (dcv * c_s).astype(BF16)
        d3_ext[tm:tm + hs, :] = d3_ext[0:hs, :]

        du3 = _dot_nt(dyb, wco_ref[...])
        u1v = u1_ref[...].astype(F32)
        mu = _rowmean(u1v)
        uc = u1v - mu
        rstd = lax.rsqrt(_rowmean(uc * uc) + LN_EPS)
        uhat = uc * rstd
        lng = prm_ref[P_LNG:P_LNG + 1, :]
        u2 = uhat * lng + prm_ref[P_LNB:P_LNB + 1, :]
        s2 = _sigmoid(u2)
        u3_ref[...] = (u2 * s2).astype(BF16)
        du2 = du3 * (s2 * (1.0 + u2 * (1.0 - s2)))
        sums_ref[1:2, :] = sums_ref[1:2, :] + _colsum(du2 * uhat)
        sums_ref[2:3, :] = sums_ref[2:3, :] + _colsum(du2)
        duhat = du2 * lng
        du1 = rstd * (duhat - _rowmean(duhat) - uhat * _rowmean(duhat * uhat))
        sums_ref[3:4, :] = sums_ref[3:4, :] + _colsum(du1)
        du1_ext[0:tm, :] = du1
        v_c = col(3)
        sg = _sigmoid(col(4))
        u0_ext[0:hc, :] = hcol(3, hc) * _sigmoid(hcol(4, hc))
        u0_ext[hc:hc + tm, :] = v_c * sg
        _tap_wgrad(du1_ext, u0_ext, _causal_taps(CONF_K, hc), tm, hc, dw31_ref)
        _tap_conv(du1_ext, taps_ref, [(r31 + wi, off) for wi, off in _anticausal_taps(CONF_K)], tm, hc, tmp_ref)
        dv_c = tmp_ref[...] * sg
        dproj_ref[:, 3 * d:4 * d] = dv_c.astype(BF16)
        dproj_ref[:, 4 * d:5 * d] = (dv_c * v_c * (1.0 - sg)).astype(BF16)
        du1_ext[tm:tm + hc, :] = du1_ext[0:hc, :]

        @pl.when(i == nt - 1)
        def _():
            scatter_finish()

    rev = lambda i: (nt - 1 - i, 0)
    tile = pl.BlockSpec((tm, d), rev)
    whole = lambda shape: pl.BlockSpec(shape, lambda i: (0,) * len(shape))
    hblocks = tm // HALO_CONF
    halo = pl.BlockSpec((HALO_CONF, 7 * d), lambda i: (jnp.maximum((nt - 1 - i) * hblocks - 1, 0), 0))
    bf = jax.ShapeDtypeStruct((t, d), BF16)
    res = pl.pallas_call(
        body, name="backward_mix", grid=(nt,),
        out_shape=(jax.ShapeDtypeStruct((t, 7 * d), BF16), bf, bf, bf, bf, bf, bf,
                   jax.ShapeDtypeStruct((8, d), F32),
                   jax.ShapeDtypeStruct((SUBLANES * SHORT_K, d), F32),
                   jax.ShapeDtypeStruct((SUBLANES * CONF_K, d), F32))
        + tuple(jax.ShapeDtypeStruct(p.shape, p.dtype) for p in partials),
        in_specs=[tile, pl.BlockSpec((tm, 7 * d), rev), halo, tile, tile, tile, tile,
                  whole((8, d)), whole((8, d)), whole(taps.shape), whole((d, d)), whole((d, d)), whole((d, d))]
        + [ANY_SPEC] * na,
        out_specs=[pl.BlockSpec((tm, 7 * d), rev), tile, tile, tile, tile, tile, tile,
                   whole((8, d)), whole((SUBLANES * SHORT_K, d)), whole((SUBLANES * CONF_K, d))] + [ANY_SPEC] * na,
        scratch_shapes=[pltpu.VMEM((hs + tm, d), F32), pltpu.VMEM((hc + tm, d), F32),
                        pltpu.VMEM((tm + hs, d), F32), pltpu.VMEM((tm + hc, d), F32),
                        pltpu.VMEM((tm, d), F32)] + _comm_scratch(na),
        compiler_params=_params(("arbitrary",)),
    )(dx1, proj, proj, y_a, y_b, mix, u1, mod, prm, taps, w_so, w_co, w_o, *partials)
    return res[:10], res[10:]


def _backward_in(dproj, x, dx1, mod, prm, w_in_g, tm, partials, partial_view, exchanged_shapes):
    t, d = x.shape
    n_all = w_in_g.shape[1]
    na = len(partials)
    nt = t // tm

    def body(*refs):
        dproj_ref, x_ref, dx1_ref, mod_ref, prm_ref, w_hbm = refs[:6]
        gx_ref, sums_ref = refs[6 + na:8 + na]
        w_ref = refs[8 + 2 * na]
        scatter_start, scatter_finish = _scatter_plan(refs[6:6 + na], refs[8 + na:8 + 2 * na], *refs[9 + 2 * na:],
                                                      view=partial_view)

        @pl.when(pl.program_id(0) == 0)
        def _():
            scatter_start()
            pltpu.sync_copy(w_hbm, w_ref)
            sums_ref[...] = jnp.zeros((8, d), F32)

        dh = _dot_nt(dproj_ref[...], w_ref[...])
        xv = x_ref[...]
        r1 = lax.rsqrt(_rowmean(xv * xv) + EPS)
        xn = xv * r1
        gmix = prm_ref[P_GMIX:P_GMIX + 1, :]
        scale = 1.0 + mod_ref[M_SC1:M_SC1 + 1, :]
        dxn = dh * gmix * scale
        gx_ref[...] = dx1_ref[...] + r1 * (dxn - xn * _rowmean(dxn * xn))
        hx = dh * xn
        sums_ref[0:1, :] = sums_ref[0:1, :] + _colsum(dh)
        sums_ref[1:2, :] = sums_ref[1:2, :] + _colsum(hx) * gmix
        sums_ref[2:3, :] = sums_ref[2:3, :] + _colsum(hx) * scale

        @pl.when(pl.program_id(0) == nt - 1)
        def _():
            scatter_finish()

    tile = pl.BlockSpec((tm, d), lambda i: (i, 0))
    whole = pl.BlockSpec((8, d), lambda i: (0, 0))
    res = pl.pallas_call(
        body, name="backward_in", grid=(nt,),
        out_shape=(jax.ShapeDtypeStruct((t, d), F32), jax.ShapeDtypeStruct((8, d), F32))
        + tuple(jax.ShapeDtypeStruct(s, p.dtype) for p, s in zip(partials, exchanged_shapes)),
        in_specs=[pl.BlockSpec((tm, n_all), lambda i: (i, 0)), tile, tile, whole, whole, ANY_SPEC] + [ANY_SPEC] * na,
        out_specs=[tile, whole] + [ANY_SPEC] * na,
        scratch_shapes=[pltpu.VMEM(w_in_g.shape, BF16)] + _comm_scratch(na),
        compiler_params=_params(("arbitrary",)),
    )(dproj, x, dx1, mod, prm, w_in_g, *partials)
    return res[0], res[1], res[2:]


def _weight_grad(a, b, a_spec, b_spec, ns, m, n, nk, name, partials=(), by_columns=False):
    na = len(partials)

    def body(*refs):
        a_ref, b_ref = refs[:2]
        o_ref = refs[2 + na]
        acc_ref = refs[3 + 2 * na]
        s, k = pl.program_id(0), pl.program_id(1)
        if na:
            scatter_start, scatter_finish = _scatter_plan(refs[2:2 + na], refs[3 + na:3 + 2 * na], *refs[4 + 2 * na:])

            @pl.when((s == 0) & (k == 0))
            def _():
                scatter_start()

        av = a_ref[0] if len(a_ref.shape) == 3 else a_ref[...]
        bv = b_ref[0] if len(b_ref.shape) == 3 else b_ref[...]
        part = _dot_tn(av, bv)

        @pl.when(k == 0)
        def _():
            acc_ref[...] = part

        @pl.when(k > 0)
        def _():
            acc_ref[...] = acc_ref[...] + part

        @pl.when(k == nk - 1)
        def _():
            if by_columns:
                o_ref[...] = acc_ref[...].astype(BF16)
            else:
                o_ref[0] = acc_ref[...].astype(BF16)

        if na:
            @pl.when((s == ns - 1) & (k == nk - 1))
            def _():
                scatter_finish()

    if by_columns:
        out_shape, out_spec = (m, ns * n), pl.BlockSpec((m, n), lambda s, k: (0, s))
    else:
        out_shape, out_spec = (ns, m, n), pl.BlockSpec((1, m, n), lambda s, k: (s, 0, 0))
    res = pl.pallas_call(
        body, name=name, grid=(ns, nk),
        out_shape=(jax.ShapeDtypeStruct(out_shape, BF16),)
        + tuple(jax.ShapeDtypeStruct(p.shape, p.dtype) for p in partials),
        in_specs=[a_spec, b_spec] + [ANY_SPEC] * na,
        out_specs=[out_spec] + [ANY_SPEC] * na,
        scratch_shapes=[pltpu.VMEM((m, n), F32)] + (_comm_scratch(na) if na else []),
        compiler_params=_params(("arbitrary", "arbitrary")),
    )(a, b, *partials)
    return (res[0], res[1:]) if na else res[0]


def _adamw(w, g, m, v):
    m = ADAM_B1 * m + (1.0 - ADAM_B1) * g
    v = ADAM_B2 * v + (1.0 - ADAM_B2) * (g * g)
    m_hat = m / (1.0 - ADAM_B1 ** ADAM_STEP)
    v_hat = v / (1.0 - ADAM_B2 ** ADAM_STEP)
    delta = -ADAM_LR * (m_hat / (jnp.sqrt(v_hat) + ADAM_EPS) + ADAM_WD * w)
    return delta, m, v


def _adamw_shard(parts, w, m, v, tr, name):
    r, c = w.shape

    def body(p_ref, w_ref, m_ref, v_ref, g_ref, d_ref, nm_ref, nv_ref):
        g = p_ref[0].astype(F32)
        for s in range(1, N_DEV):
            g = g + p_ref[s].astype(F32)
        delta, nm, nv = _adamw(w_ref[...], g, m_ref[...], v_ref[...])
        g_ref[...] = g
        d_ref[...] = delta
        nm_ref[...] = nm
        nv_ref[...] = nv

    tile = pl.BlockSpec((tr, c), lambda i: (i, 0))
    return pl.pallas_call(
        body, name=name, grid=(r // tr,),
        out_shape=(jax.ShapeDtypeStruct((r, c), F32),) * 4,
        in_specs=[pl.BlockSpec((N_DEV, tr, c), lambda i: (0, i, 0)), tile, tile, tile],
        out_specs=[tile] * 4,
        compiler_params=_params(("arbitrary",)),
    )(parts, w, m, v)


def _ada_update(sc_all, dmod_cols, w, m, v, tr):
    d, n = w.shape

    def body(sc_ref, dm_ref, w_ref, m_ref, v_ref, g_ref, d_ref, nm_ref, nv_ref):
        g = lax.dot_general(sc_ref[...], dm_ref[...], (((0,), (0,)), ((), ())),
                            preferred_element_type=F32, precision=lax.Precision.HIGHEST)
        delta, nm, nv = _adamw(w_ref[...], g, m_ref[...], v_ref[...])
        g_ref[...] = g
        d_ref[...] = delta
        nm_ref[...] = nm
        nv_ref[...] = nv

    tile = pl.BlockSpec((tr, n), lambda i: (i, 0))
    return pl.pallas_call(
        body, name="ada_update", grid=(d // tr,),
        out_shape=(jax.ShapeDtypeStruct((d, n), F32),) * 4,
        in_specs=[pl.BlockSpec((N_DEV, tr), lambda i: (0, i)), pl.BlockSpec((N_DEV, n), lambda i: (0, 0)),
                  tile, tile, tile],
        out_specs=[tile] * 4,
        compiler_params=_params(("arbitrary",)),
    )(sc_all, dmod_cols, w, m, v)


def _small_exchange(vec, cg):
    l = vec.shape[2]
    rows = cg.shape[1]

    def body(vec_ref, cg_ref, vall_ref, cgr_ref, send_sems, recv_sems):
        x, y, c = _my_coords()
        me = _slot(x, y, c)
        vall_ref[me] = vec_ref[0]
        cgr_ref[me] = cg_ref[me]
        copies = []
        for k in range(1, N_DEV):
            peer = (_flip(x, k & 4), _flip(y, k & 2), _flip(c, k & 1))
            copies.append(pltpu.make_async_remote_copy(
                src_ref=vall_ref.at[me], dst_ref=vall_ref.at[me], send_sem=send_sems.at[k - 1],
                recv_sem=recv_sems.at[k - 1], device_id=peer, device_id_type=MESH))
            copies.append(pltpu.make_async_remote_copy(
                src_ref=cg_ref.at[_slot(*peer)], dst_ref=cgr_ref.at[me], send_sem=send_sems.at[7 + k - 1],
                recv_sem=recv_sems.at[7 + k - 1], device_id=peer, device_id_type=MESH))
        for cp in copies:
            cp.start()
        for cp in copies:
            cp.wait_recv()
        for cp in copies:
            cp.wait_send()

    vm = pl.BlockSpec(memory_space=pltpu.VMEM)
    return pl.pallas_call(
        body, name="small_exchange",
        out_shape=(jax.ShapeDtypeStruct((N_DEV, 1, l), F32), jax.ShapeDtypeStruct((N_DEV, rows, LANES), F32)),
        in_specs=[vm, vm], out_specs=[vm, vm],
        scratch_shapes=[pltpu.SemaphoreType.DMA((14,)), pltpu.SemaphoreType.DMA((14,))],
        compiler_params=_params(),
    )(vec, cg)


def _small_update(vall, cgr, smalls):
    ns = len(smalls)

    def body(*refs):
        vall_ref, cgr_ref = refs[0], refs[1]
        wmv = refs[2:2 + 3 * ns]
        outs = refs[2 + 3 * ns:]
        for p, (_, _, _, lo, hi, kind) in enumerate(smalls):
            w_ref, m_ref, v_ref = wmv[3 * p:3 * p + 3]
            part = (lambda s: vall_ref[s, :, lo:hi]) if kind == "vec" else (lambda s: cgr_ref[s, lo:hi, :])
            g = part(0)
            for s in range(1, N_DEV):
                g = g + part(s)
            delta, nm, nv = _adamw(w_ref[...], g, m_ref[...], v_ref[...])
            for o_ref, val in zip(outs[4 * p:4 * p + 4], (g, delta, nm, nv)):
                o_ref[...] = val

    vm = pl.BlockSpec(memory_space=pltpu.VMEM)
    args = [vall, cgr]
    out_shape = []
    for w, m, v, _, _, _ in smalls:
        args += [w, m, v]
        out_shape += [jax.ShapeDtypeStruct(w.shape, F32)] * 4
    res = pl.pallas_call(
        body, name="small_update",
        out_shape=tuple(out_shape),
        in_specs=[vm] * len(args), out_specs=[vm] * len(out_shape),
        compiler_params=_params(),
    )(*args)
    return [res[4 * p:4 + 4 * p] for p in range(ns)]


def _pick(t, want):
    return want if t % want == 0 else t


def kernel(x, c, w_ada, b_ada, norm_mix_g, w_in, conv_short_w, w_short_out, conv_conf_w, conv_conf_b, conf_ln_g, conf_ln_b, w_conf_out, w_o, norm_ffn_g, w_ffn_in, w_ffn_out, final_norm_g, loss_target, m_w_ada, m_b_ada, m_norm_mix_g, m_w_in, m_conv_short_w, m_w_short_out, m_conv_conf_w, m_conv_conf_b, m_conf_ln_g, m_conf_ln_b, m_w_conf_out, m_w_o, m_norm_ffn_g, m_w_ffn_in, m_w_ffn_out, m_final_norm_g, v_w_ada, v_b_ada, v_norm_mix_g, v_w_in, v_conv_short_w, v_w_short_out, v_conv_conf_w, v_conv_conf_b, v_conf_ln_g, v_conf_ln_b, v_w_conf_out, v_w_o, v_norm_ffn_g, v_w_ffn_in, v_w_ffn_out, v_final_norm_g):
    t, d = x.shape[1], x.shape[2]
    x2 = x.reshape(t, d)
    tgt = loss_target.reshape(t, d)
    me = _slot(*_my_coords())
    tm = _pick(t, 256)
    tm_fwd_in = _pick(t, 1024)
    tk = _pick(t, 2048)

    taps_loc = jnp.zeros((40, LANES), F32)
    taps_loc = taps_loc.at[0:SHORT_K].set(conv_short_w[0]).at[8:8 + CONF_K].set(conv_conf_w[0])
    mod_flat, sc_all3, taps = _ada_forward(c, w_ada[0], b_ada, taps_loc)
    mod = jnp.concatenate([mod_flat.reshape(6, d), jnp.zeros((2, d), F32)], axis=0)
    prm = jnp.concatenate([norm_mix_g, norm_ffn_g, final_norm_g.reshape(1, d), conv_conf_b, conf_ln_g, conf_ln_b,
                           jnp.zeros((2, d), F32)], axis=0)

    n_in = w_in.shape[2]
    nk = t // tk
    tok = pl.BlockSpec((tk, d), lambda s, k: (k, 0))
    rows = d // N_DEV
    frows = w_ffn_out.shape[1]

    proj, h, w_in_g, (w_so_g, w_co_g, w_o_g) = _forward_in(
        jnp.reshape(me, (1,)).astype(jnp.int32), x2, mod, prm, w_in[0].astype(BF16), tm_fwd_in,
        [w_short_out[0].astype(BF16), w_conf_out[0].astype(BF16), w_o[0].astype(BF16)])
    w_so = w_so_g.reshape(d, d)
    w_co = w_co_g.reshape(d, d)
    w_oo = w_o_g.reshape(d, d)
    w_fi_t, m_fi_t, v_fi_t = w_ffn_in[0].T, m_w_ffn_in[0].T, v_w_ffn_in[0].T
    (x1, y_a, y_b, mix, u1), (w_fi_g, w_fo_g) = _forward_mix(
        x2, proj, mod, prm, taps, w_so, w_co, w_oo, tm, [w_fi_t.astype(BF16), w_ffn_out[0].astype(BF16)])
    fb = w_fi_g.shape[1]
    ff = N_DEV * frows
    w_fi_all = w_fi_g.reshape(N_DEV * fb, d)
    w_fo_all = w_fo_g.reshape(ff, d)
    dx2, ab, h2, sums_f = _forward_ffn(x1, tgt, mod, prm, w_fi_all, w_fo_all, tm)

    dx1, df, act, dab, sums_b = _backward_ffn(dx2, x1, ab, mod, prm, w_fi_all, w_fo_all, tm)
    fb2 = 2 * fb
    pair_tok = pl.BlockSpec((tk, fb2), lambda s, k: (k, s))
    g_fi = _weight_grad(dab, h2, pair_tok, tok, N_DEV // 2, fb2, d, nk, "grad_w_ffn_in")
    g_fo = _weight_grad(act, df, pair_tok, tok, N_DEV // 4, fb2, d, nk, "grad_w_ffn_out")
    (dproj, dmix, dya, dyb, merged, ya_pre, u3, sums_m, dw3p, dw31p), (p_fi, p_fo) = _backward_mix(
        dx1, proj, y_a, y_b, mix, u1, mod, prm, taps, w_so, w_co, w_oo, tm,
        [g_fi.reshape(N_DEV, fb, d), g_fo.reshape(N_DEV, frows, d)])
    g_so = _weight_grad(ya_pre, dya, tok, tok, 1, d, d, nk, "grad_w_short_out")
    g_co = _weight_grad(u3, dyb, tok, tok, 1, d, d, nk, "grad_w_conf_out")
    g_oo = _weight_grad(merged, dmix, tok, tok, 1, d, d, nk, "grad_w_o")
    n_blk = _pick(N_DEV * n_in, d)
    tk_in = _pick(t, 4096)
    g_in, (p_so, p_co, p_oo) = _weight_grad(
        h, dproj, pl.BlockSpec((tk_in, d), lambda s, k: (k, 0)), pl.BlockSpec((tk_in, n_blk), lambda s, k: (k, s)),
        N_DEV * n_in // n_blk, d, n_blk, t // tk_in, "grad_w_in",
        [g_so.reshape(N_DEV, rows, d), g_co.reshape(N_DEV, rows, d), g_oo.reshape(N_DEV, rows, d)], by_columns=True)
    grad_x, sums_i, (p_in,) = _backward_in(dproj, x2, dx1, mod, prm, w_in_g, tm, [g_in], _column_block(n_in),
                                           [(N_DEV, d, n_in)])

    up_in = _adamw_shard(p_in, w_in[0], m_w_in[0], v_w_in[0], _pick(d, 256), "adamw_w_in")
    up_so = _adamw_shard(p_so, w_short_out[0], m_w_short_out[0], v_w_short_out[0], rows, "adamw_w_short_out")
    up_co = _adamw_shard(p_co, w_conf_out[0], m_w_conf_out[0], v_w_conf_out[0], rows, "adamw_w_conf_out")
    up_oo = _adamw_shard(p_oo, w_o[0], m_w_o[0], v_w_o[0], rows, "adamw_w_o")
    up_fi = tuple(a.T for a in _adamw_shard(p_fi, w_fi_t, m_fi_t, v_fi_t, fb // 2, "adamw_w_ffn_in"))
    up_fo = _adamw_shard(p_fo, w_ffn_out[0], m_w_ffn_out[0], v_w_ffn_out[0], frows, "adamw_w_ffn_out")

    vec = jnp.concatenate([sums_i[0:2], sums_m[0:1], sums_b[0:2], sums_f[1:2],
                           sums_i[2:3], sums_m[3:4], sums_m[1:3], sums_b[2:3], sums_f[0:1],
                           sums_f[2:3]], axis=0)
    vec = vec.reshape(1, 1, 13 * d)
    dw3 = dw3p.reshape(SHORT_K, SUBLANES, d).sum(axis=1)
    dw31 = dw31p.reshape(CONF_K, SUBLANES, d).sum(axis=1)
    cg = jnp.zeros((40, d), F32).at[0:SHORT_K].set(dw3).at[8:8 + CONF_K].set(dw31)
    cg = cg.reshape(40, N_DEV, LANES).transpose(1, 0, 2)
    fin = lambda a: a.reshape(1, d)
    tap = lambda a: a.reshape(a.shape[1:])
    smalls = [
        (b_ada, m_b_ada, v_b_ada, 0, 6 * d, "vec"),
        (norm_mix_g, m_norm_mix_g, v_norm_mix_g, 6 * d, 7 * d, "vec"),
        (tap(conv_short_w), tap(m_conv_short_w), tap(v_conv_short_w), 0, SHORT_K, "cg"),
        (tap(conv_conf_w), tap(m_conv_conf_w), tap(v_conv_conf_w), 8, 8 + CONF_K, "cg"),
        (conv_conf_b, m_conv_conf_b, v_conv_conf_b, 7 * d, 8 * d, "vec"),
        (conf_ln_g, m_conf_ln_g, v_conf_ln_g, 8 * d, 9 * d, "vec"),
        (conf_ln_b, m_conf_ln_b, v_conf_ln_b, 9 * d, 10 * d, "vec"),
        (norm_ffn_g, m_norm_ffn_g, v_norm_ffn_g, 10 * d, 11 * d, "vec"),
        (fin(final_norm_g), fin(m_final_norm_g), fin(v_final_norm_g), 11 * d, 12 * d, "vec"),
    ]
    vall, cgr = _small_exchange(vec, cg)
    up_small = _small_update(vall, cgr, smalls)
    n_ada = w_ada.shape[2]
    dmod_all = vall.reshape(N_DEV, 13 * d)[:, 0:6 * d]
    dmod_cols = lax.dynamic_slice(dmod_all, (0, me * n_ada), (N_DEV, n_ada))
    up_ada = _ada_update(sc_all3.reshape(N_DEV, d), dmod_cols, w_ada[0], m_w_ada[0], v_w_ada[0], _pick(d, 256))

    loss = jnp.sum(vall.reshape(N_DEV, 13 * d)[:, 12 * d:])

    lead = lambda a: a.reshape((1,) + a.shape)
    ups = [tuple(lead(a) for a in up_ada), up_small[0], up_small[1], tuple(lead(a) for a in up_in),
           tuple(lead(a) for a in up_small[2]), tuple(lead(a) for a in up_so), tuple(lead(a) for a in up_small[3]),
           up_small[4], up_small[5], up_small[6],
           tuple(lead(a) for a in up_co), tuple(lead(a) for a in up_oo), up_small[7],
           tuple(lead(a) for a in up_fi), tuple(lead(a) for a in up_fo),
           tuple(a.reshape(d) for a in up_small[8])]
    grads = [u[0] for u in ups]
    deltas = [u[1] for u in ups]
    new_m = [u[2] for u in ups]
    new_v = [u[3] for u in ups]
    return (loss, grad_x.reshape(1, t, d), *grads, *deltas, *new_m, *new_v)
```

```python
import functools

import jax
import jax.numpy as jnp
from jax import lax
from jax.experimental import pallas as pl
from jax.experimental.pallas import tpu as pltpu

F32 = jnp.float32
BF16 = jnp.bfloat16
MESH = pl.DeviceIdType.MESH

N_DEV = 8
EPS = 1e-6
LN_EPS = 1e-5
SHORT_K = 3
CONF_K = 31
ADAM_LR = 0.001
ADAM_B1 = 0.9
ADAM_B2 = 0.999
ADAM_EPS = 1e-08
ADAM_WD = 0.01
ADAM_STEP = 10

LANES = 128
SUBLANES = 8
CONV_ROWS = 64
HALO_SHORT = 8
HALO_CONF = 32
VMEM_LIMIT = 56 * 1024 * 1024

M_SH1, M_SC1, M_G1, M_SH2, M_SC2, M_G2 = range(6)
P_GMIX, P_GFFN, P_GFIN, P_CBIAS, P_LNG, P_LNB = range(6)


def _params(sem=None, **kw):
    return pltpu.CompilerParams(dimension_semantics=sem, vmem_limit_bytes=VMEM_LIMIT, **kw)


def _sigmoid(v):
    return jax.nn.sigmoid(v)


def _dot(a, b):
    return jnp.dot(a, b, preferred_element_type=F32)


def _dot_nt(a, b):
    return lax.dot_general(a, b, (((1,), (1,)), ((), ())), preferred_element_type=F32)


def _dot_tn(a, b):
    return lax.dot_general(a, b, (((0,), (0,)), ((), ())), preferred_element_type=F32)


def _colsum(v):
    return jnp.sum(v, axis=0, keepdims=True)


def _rowmean(v):
    return jnp.mean(v, axis=-1, keepdims=True)


def _my_coords():
    return lax.axis_index("x"), lax.axis_index("y"), lax.axis_index("c")


def _slot(px, py, pc):
    return 4 * px + 2 * py + pc


def _flip(v, bit):
    return 1 - v if bit else v


def _taps_by_residue(taps):
    by_res = {}
    for wi, off in taps:
        by_res.setdefault(off % SUBLANES, []).append((wi, off // SUBLANES))
    return sorted(by_res.items())


def _tap_conv(ext_ref, w_ref, taps, tm, extra, out_ref):
    d = out_ref.shape[1]
    rb = min(CONV_ROWS, tm)
    wrows = rb + extra
    groups = _taps_by_residue(taps)

    def block(i, carry):
        base = pl.multiple_of(i * rb, SUBLANES)
        for lc in range(d // LANES):
            ls = pl.ds(lc * LANES, LANES)
            win = ext_ref[pl.ds(base, wrows), ls]
            acc = None
            for r, lst in groups:
                sh = win if r == 0 else pltpu.roll(win, wrows - r, 0)
                for wi, q in lst:
                    term = w_ref[wi:wi + 1, ls] * sh[SUBLANES * q:SUBLANES * q + rb, :]
                    acc = term if acc is None else acc + term
            out_ref[pl.ds(base, rb), ls] = acc
        return carry

    lax.fori_loop(0, tm // rb, block, 0)


def _tap_wgrad(a_ref, ext_ref, taps, tm, extra, acc_ref):
    d = a_ref.shape[1]
    rb = min(CONV_ROWS, tm)
    wrows = rb + extra
    groups = _taps_by_residue(taps)

    def block(i, carry):
        base = pl.multiple_of(i * rb, SUBLANES)
        for lc in range(d // LANES):
            ls = pl.ds(lc * LANES, LANES)
            a_blk = a_ref[pl.ds(base, rb), ls]
            win = ext_ref[pl.ds(base, wrows), ls]
            for r, lst in groups:
                sh = win if r == 0 else pltpu.roll(win, wrows - r, 0)
                for wi, q in lst:
                    prod = a_blk * sh[SUBLANES * q:SUBLANES * q + rb, :]
                    part = prod[0:SUBLANES, :]
                    for s in range(1, rb // SUBLANES):
                        part = part + prod[SUBLANES * s:SUBLANES * (s + 1), :]
                    rows = pl.ds(SUBLANES * wi, SUBLANES)
                    acc_ref[rows, ls] = acc_ref[rows, ls] + part
        return carry

    lax.fori_loop(0, tm // rb, block, 0)


def _causal_taps(k, halo):
    return [(i, halo - (k - 1) + i) for i in range(k)]


def _anticausal_taps(k):
    return [(i, (k - 1) - i) for i in range(k)]


def _ada_scratch(d, nloc, trows):
    return [pltpu.VMEM((N_DEV, 1, d), F32), pltpu.VMEM((N_DEV, 1, nloc), F32), pltpu.VMEM((N_DEV, 1, nloc), F32),
            pltpu.VMEM((N_DEV, trows, LANES), F32), pltpu.SemaphoreType.DMA((21,)), pltpu.SemaphoreType.DMA((21,))]


def _ada_exchange(c_ref, w_ref, b_ref, t_ref, mod_ref, sc_ref, taps_ref,
                  scall_ref, part_ref, modrecv_ref, tapsall_ref, send_sems, recv_sems):
    x, y, cc = _my_coords()
    me = _slot(x, y, cc)
    d = c_ref.shape[1]
    cv = c_ref[...]
    scall_ref[me] = cv * _sigmoid(cv)
    tapsall_ref[me] = t_ref[...]

    def peer_of(k):
        return (_flip(x, k & 4), _flip(y, k & 2), _flip(cc, k & 1))

    def gather_copy(ref, base, k):
        return pltpu.make_async_remote_copy(
            src_ref=ref.at[me], dst_ref=ref.at[me], send_sem=send_sems.at[base + k - 1],
            recv_sem=recv_sems.at[base + k - 1], device_id=peer_of(k), device_id_type=MESH)

    first = [gather_copy(scall_ref, 0, k) for k in range(1, N_DEV)]
    first += [gather_copy(tapsall_ref, 7, k) for k in range(1, N_DEV)]
    for cp in first:
        cp.start()
    for cp in first[:7]:
        cp.wait_recv()
    sc_all = jnp.concatenate([scall_ref[s] for s in range(N_DEV)], axis=0)
    for s in range(N_DEV):
        sc_ref[s] = scall_ref[s]
    part = jnp.dot(sc_all, w_ref[...], preferred_element_type=F32,
                   precision=lax.Precision.HIGHEST)
    for b in range(N_DEV):
        part_ref[b] = part[b:b + 1, :]
    modrecv_ref[me] = part_ref[me]
    second = []
    for k in range(1, N_DEV):
        px, py, pc = peer_of(k)
        second.append(pltpu.make_async_remote_copy(
            src_ref=part_ref.at[_slot(px, py, pc)], dst_ref=modrecv_ref.at[me],
            send_sem=send_sems.at[14 + k - 1], recv_sem=recv_sems.at[14 + k - 1],
            device_id=(px, py, pc), device_id_type=MESH))
    for cp in second:
        cp.start()
    for cp in second:
        cp.wait_recv()
    mod = jnp.concatenate([modrecv_ref[s] for s in range(N_DEV)], axis=1) + b_ref[...]
    for r in range(6):
        mod_ref[r:r + 1, :] = mod[:, r * d:(r + 1) * d]
    mod_ref[6:8, :] = jnp.zeros((2, d), F32)
    for cp in first[7:]:
        cp.wait_recv()
    taps_ref[...] = jnp.concatenate([tapsall_ref[s] for s in range(N_DEV)], axis=1)
    for cp in first + second:
        cp.wait_send()


ANY_SPEC = pl.BlockSpec(memory_space=pl.ANY)


def _comm_scratch(na):
    return [pltpu.SemaphoreType.DMA((7 * na,)), pltpu.SemaphoreType.DMA((7 * na,)), pltpu.SemaphoreType.DMA((na,))]


def _leading_block(ref, slot):
    return ref.at[slot]


def _column_block(width):
    def view(ref, slot):
        return ref.at[:, pl.ds(pl.multiple_of(slot * width, LANES), width)]
    return view


class _Gather:
    def __init__(self, ins, outs, send_sems, recv_sems, local_sems, view=_leading_block):
        self.na = len(ins)
        x, y, c = _my_coords()
        self.c = c
        self.view = view
        self.me, self.sibling = (x, y, c), (x, y, 1 - c)
        self.chips = [(1 - x, y), (x, 1 - y), (1 - x, 1 - y)]
        self.outs, self.send_sems, self.recv_sems = outs, send_sems, recv_sems
        self.mine = [pltpu.make_async_copy(ins[a], view(outs[a], _slot(*self.me)), local_sems.at[a])
                     for a in range(self.na)]
        self.first = []
        for a in range(self.na):
            self.first.append(self._copy(a, 0, self.me, self.sibling, src=ins[a]))
            self.first += [self._copy(a, 1 + j, self.me, (*chip, c), src=ins[a]) for j, chip in enumerate(self.chips)]
        self.passed = [self._copy(a, 4 + j, (*chip, c), self.sibling)
                       for a in range(self.na) for j, chip in enumerate(self.chips)]

    def _copy(self, a, k, block, to, src=None):
        dst = self.view(self.outs[a], _slot(*block))
        return pltpu.make_async_remote_copy(
            src_ref=dst if src is None else src, dst_ref=dst,
            send_sem=self.send_sems.at[7 * a + k], recv_sem=self.recv_sems.at[7 * a + k],
            device_id=to, device_id_type=MESH)

    def start(self):
        self.start_near()
        self.start_far()

    def start_near(self):
        for cp in self.mine + [cp for n, cp in enumerate(self.first) if n % 4 != 3]:
            cp.start()

    def start_far(self):
        for cp in self.first[3::4]:
            cp.start()

    def forward_chip(self, j):
        for a in range(self.na):
            self._copy(a, 1 + j, (*self.chips[j], self.c), self.me).wait_recv()
            self.passed[3 * a + j].start()

    def wait_sibling_own(self):
        for a in range(self.na):
            self._copy(a, 0, self.sibling, self.me).wait_recv()

    def wait_sibling_passed(self, j):
        for a in range(self.na):
            self._copy(a, 4 + j, (*self.chips[j], 1 - self.c), self.me).wait_recv()

    def finish_sends(self):
        for cp in self.first + self.passed:
            cp.wait_send()
        for cp in self.mine:
            cp.wait()

    def forward(self):
        for j in range(3):
            self.forward_chip(j)

    def finish(self):
        self.wait_sibling_own()
        for j in range(3):
            self.wait_sibling_passed(j)
        self.finish_sends()


def _scatter_plan(ins, outs, send_sems, recv_sems, local_sems, view=_leading_block):
    na = len(ins)
    x, y, c = _my_coords()
    me = _slot(x, y, c)
    mine = [pltpu.make_async_copy(view(ins[a], me), outs[a].at[me], local_sems.at[a]) for a in range(na)]
    copies = []
    for k in range(1, N_DEV):
        peer = (_flip(x, k & 4), _flip(y, k & 2), _flip(c, k & 1))
        for a in range(na):
            copies.append(pltpu.make_async_remote_copy(
                src_ref=view(ins[a], _slot(*peer)), dst_ref=outs[a].at[me],
                send_sem=send_sems.at[7 * a + k - 1], recv_sem=recv_sems.at[7 * a + k - 1],
                device_id=peer, device_id_type=MESH))

    def start():
        for cp in mine + copies:
            cp.start()

    def finish():
        for cp in copies:
            cp.wait_recv()
        for cp in copies:
            cp.wait_send()
        for cp in mine:
            cp.wait()

    return start, finish


def _forward_in(me_arr, x, prm, w_in_loc, ada, tm, shards):
    t, d = x.shape
    n = w_in_loc.shape[1]
    ns = N_DEV
    na = len(shards)
    nt = t // tm
    nloc = ada[1].shape[1]
    trows = ada[3].shape[0]

    def body(*refs):
        x_ref, prm_ref, wloc_ref = refs[1:4]
        ada_in = refs[4:8]
        shard_refs = refs[8:8 + na]
        proj_ref, h_ref, wg_ref, mod_ref, sc_ref, taps_ref = refs[8 + na:14 + na]
        gathered_refs = refs[14 + na:14 + 2 * na]
        hall_ref, wv_ref, wv_sem = refs[14 + 2 * na:17 + 2 * na]
        ada_scratch = refs[17 + 2 * na:23 + 2 * na]
        sems = refs[23 + 2 * na:]
        s, i = pl.program_id(0), pl.program_id(1)
        columns = _column_block(n)

        def gathers():
            return (_Gather([wloc_ref], [wg_ref], *sems[0:3], view=columns),
                    _Gather(shard_refs, gathered_refs, *sems[3:6]))

        def load_shard(src):
            cp = pltpu.make_async_copy(src, wv_ref, wv_sem)
            cp.start()
            cp.wait()

        @pl.when(i == 0)
        def _():
            g_in, g_rest = gathers()
            me = _slot(*_my_coords())

            @pl.when(s == 0)
            def _():
                g_in.start_near()
                _ada_exchange(*ada_in, mod_ref, sc_ref, taps_ref, *ada_scratch)
                load_shard(wloc_ref)

            arrivals = {1: g_in.wait_sibling_own,
                        2: functools.partial(g_in.forward_chip, 1), 3: functools.partial(g_in.wait_sibling_passed, 1),
                        4: functools.partial(g_in.forward_chip, 0), 5: functools.partial(g_in.wait_sibling_passed, 0),
                        6: functools.partial(g_in.forward_chip, 2), 7: functools.partial(g_in.wait_sibling_passed, 2)}
            for step, arrive in arrivals.items():
                @pl.when(s == step)
                def _(step=step, arrive=arrive):
                    arrive()
                    if step == 2:
                        g_in.start_far()
                    if step == 4:
                        g_rest.start()
                    if step == 7:
                        g_rest.forward()
                    load_shard(columns(wg_ref, me ^ step))

        @pl.when(s == 0)
        def _():
            xv = x_ref[...]
            r = lax.rsqrt(_rowmean(xv * xv) + EPS)
            h = xv * r * prm_ref[P_GMIX:P_GMIX + 1, :] * (1.0 + mod_ref[M_SC1:M_SC1 + 1, :]) \
                + mod_ref[M_SH1:M_SH1 + 1, :]
            hb = h.astype(BF16)
            hall_ref[i] = hb
            h_ref[...] = hb

        proj_ref[...] = _dot(hall_ref[i], wv_ref[...]).astype(BF16)

        @pl.when((s == ns - 1) & (i == nt - 1))
        def _():
            g_in, g_rest = gathers()
            g_in.finish_sends()
            g_rest.finish()

    x_tile = pl.BlockSpec((tm, d), lambda s, i, me: (jnp.where(s == 0, i, nt - 1), 0))
    whole = lambda a: pl.BlockSpec(a.shape, lambda s, i, me: (0,) * len(a.shape))
    small_out = (jax.ShapeDtypeStruct((8, d), F32), jax.ShapeDtypeStruct((N_DEV, 1, d), F32),
                 jax.ShapeDtypeStruct((trows, N_DEV * LANES), F32))
    res = pl.pallas_call(
        body, name="forward_in",
        grid_spec=pltpu.PrefetchScalarGridSpec(
            num_scalar_prefetch=1, grid=(ns, nt),
            in_specs=[x_tile, whole(prm), ANY_SPEC] + [whole(a) for a in ada] + [ANY_SPEC] * na,
            out_specs=[pl.BlockSpec((tm, n), lambda s, i, me: (i, me[0] ^ s)), x_tile, ANY_SPEC]
            + [whole(a) for a in small_out] + [ANY_SPEC] * na,
            scratch_shapes=[pltpu.VMEM((nt, tm, d), BF16), pltpu.VMEM((d, n), BF16), pltpu.SemaphoreType.DMA]
            + _ada_scratch(d, nloc, trows) + _comm_scratch(1) + _comm_scratch(na)),
        out_shape=(jax.ShapeDtypeStruct((t, ns * n), BF16), jax.ShapeDtypeStruct((t, d), BF16),
                   jax.ShapeDtypeStruct((d, ns * n), BF16)) + small_out
        + tuple(jax.ShapeDtypeStruct((N_DEV,) + a.shape, a.dtype) for a in shards),
        compiler_params=_params(("arbitrary", "arbitrary")),
    )(me_arr, x, prm, w_in_loc, *ada, *shards)
    return res[0], res[1], res[2], res[3], res[4], res[5], res[6:]


def _forward_mix(x, proj, mod, prm, taps, w_so, w_co, w_o, tm, shards):
    t, d = x.shape
    nt = t // tm
    na = len(shards)
    hs, hc = HALO_SHORT, HALO_CONF
    r3, r31 = 0, 8

    def body(*refs):
        x_ref, proj_ref, mod_ref, prm_ref, taps_ref, wso_ref, wco_ref, wo_ref = refs[:8]
        x1_ref, ya_ref, yb_ref, mix_ref, u1_ref = refs[8 + na:13 + na]
        cv_ext, u0_ext, conv3_ref, u1f_ref = refs[13 + 2 * na:17 + 2 * na]
        gather = _Gather(refs[8:8 + na], refs[13 + na:13 + 2 * na], *refs[17 + 2 * na:])
        i = pl.program_id(0)

        @pl.when(i == 0)
        def _():
            gather.start()
            cv_ext[0:hs, :] = jnp.zeros((hs, d), F32)
            u0_ext[0:hc, :] = jnp.zeros((hc, d), F32)

        @pl.when(i == (3 * nt) // 4)
        def _():
            gather.forward()

        def col(g):
            return proj_ref[:, g * d:(g + 1) * d].astype(F32)

        cv_ext[hs:hs + tm, :] = col(1) * col(2)
        u0_ext[hc:hc + tm, :] = col(3) * _sigmoid(col(4))
        _tap_conv(cv_ext, taps_ref, [(r3 + wi, off) for wi, off in _causal_taps(SHORT_K, hs)], tm, hs, conv3_ref)
        _tap_conv(u0_ext, taps_ref, [(r31 + wi, off) for wi, off in _causal_taps(CONF_K, hc)], tm, hc, u1f_ref)
        cv_ext[0:hs, :] = cv_ext[tm:tm + hs, :]
        u0_ext[0:hc, :] = u0_ext[tm:tm + hc, :]

        ya_pre = (col(0) * conv3_ref[...]).astype(BF16)
        y_a = _dot(ya_pre, wso_ref[...])
        u1 = u1f_ref[...] + prm_ref[P_CBIAS:P_CBIAS + 1, :]
        u1_ref[...] = u1.astype(BF16)
        u1 = u1_ref[...].astype(F32)
        mu = _rowmean(u1)
        uc = u1 - mu
        rstd = lax.rsqrt(_rowmean(uc * uc) + LN_EPS)
        u2 = uc * rstd * prm_ref[P_LNG:P_LNG + 1, :] + prm_ref[P_LNB:P_LNB + 1, :]
        u3 = (u2 * _sigmoid(u2)).astype(BF16)
        y_b = _dot(u3, wco_ref[...])
        ya_ref[...] = y_a.astype(BF16)
        yb_ref[...] = y_b.astype(BF16)
        merged = _sigmoid(col(5)) * ya_ref[...].astype(F32) + _sigmoid(col(6)) * yb_ref[...].astype(F32)
        mix = _dot(merged.astype(BF16), wo_ref[...])
        mix_ref[...] = mix.astype(BF16)
        x1_ref[...] = x_ref[...] + mod_ref[M_G1:M_G1 + 1, :] * mix

        @pl.when(i == nt - 1)
        def _():
            gather.finish()

    tile = pl.BlockSpec((tm, d), lambda i: (i, 0))
    whole = lambda shape: pl.BlockSpec(shape, lambda i: (0,) * len(shape))
    res = pl.pallas_call(
        body, name="forward_mix", grid=(nt,),
        out_shape=(jax.ShapeDtypeStruct((t, d), F32),) + (jax.ShapeDtypeStruct((t, d), BF16),) * 4
        + tuple(jax.ShapeDtypeStruct((N_DEV,) + a.shape, a.dtype) for a in shards),
        in_specs=[tile, pl.BlockSpec((tm, 7 * d), lambda i: (i, 0)), whole((8, d)), whole((8, d)),
                  whole(taps.shape), whole((d, d)), whole((d, d)), whole((d, d))] + [ANY_SPEC] * na,
        out_specs=[tile] * 5 + [ANY_SPEC] * na,
        scratch_shapes=[pltpu.VMEM((hs + tm, d), F32), pltpu.VMEM((hc + tm, d), F32),
                        pltpu.VMEM((tm, d), F32), pltpu.VMEM((tm, d), F32)] + _comm_scratch(na),
        compiler_params=_params(("arbitrary",)),
    )(x, proj, mod, prm, taps, w_so, w_co, w_o, *shards)
    return res[:5], res[5:]


def _ffn_chunks(ff):
    mxu = 2 * LANES
    cut = (ff // mxu + 1) // 2 * mxu
    return [(0, cut), (cut, ff)] if 0 < cut < ff and ff % mxu == 0 else [(0, ff)]


def _forward_ffn(x1, tgt, mod, prm, w_fi_t, w_fo, tm):
    t, d = x1.shape
    ff = w_fo.shape[0]

    def body(x1_ref, tgt_ref, mod_ref, prm_ref, wfi_hbm, wfo_hbm,
             dx2_ref, ab_ref, h2_ref, sums_ref, wfi_ref, wfo_ref):
        i = pl.program_id(0)

        @pl.when(i == 0)
        def _():
            pltpu.sync_copy(wfi_hbm, wfi_ref)
            pltpu.sync_copy(wfo_hbm, wfo_ref)
            sums_ref[...] = jnp.zeros((8, d), F32)

        x1v = x1_ref[...]
        r2 = lax.rsqrt(_rowmean(x1v * x1v) + EPS)
        h2 = (x1v * r2 * prm_ref[P_GFFN:P_GFFN + 1, :] * (1.0 + mod_ref[M_SC2:M_SC2 + 1, :])
              + mod_ref[M_SH2:M_SH2 + 1, :]).astype(BF16)
        h2_ref[...] = h2
        f = jnp.zeros((tm, d), F32)
        for c0, c1 in _ffn_chunks(ff):
            ab_ref[:, c0:c1] = _dot_nt(h2, wfi_ref[c0:c1, :]).astype(BF16)
            ab_ref[:, ff + c0:ff + c1] = _dot_nt(h2, wfi_ref[ff + c0:ff + c1, :]).astype(BF16)
            a = ab_ref[:, c0:c1].astype(F32)
            act = (a * _sigmoid(a) * ab_ref[:, ff + c0:ff + c1].astype(F32)).astype(BF16)
            f = f + _dot(act, wfo_ref[c0:c1, :])
        x2 = x1v + mod_ref[M_G2:M_G2 + 1, :] * f
        r3 = lax.rsqrt(_rowmean(x2 * x2) + EPS)
        xn3 = x2 * r3
        gfin = prm_ref[P_GFIN:P_GFIN + 1, :]
        err = xn3 * gfin - tgt_ref[...]
        dy = err * (1.0 / d)
        dxn3 = dy * gfin
        dx2 = r3 * (dxn3 - xn3 * _rowmean(dxn3 * xn3))
        dx2_ref[...] = dx2
        sums_ref[0:1, :] = sums_ref[0:1, :] + _colsum(dy * xn3)
        sums_ref[1:2, :] = sums_ref[1:2, :] + _colsum(dx2 * f)
        sums_ref[2:3, :] = sums_ref[2:3, :] + _colsum(err * err) * (0.5 / d)

    tile = pl.BlockSpec((tm, d), lambda i: (i, 0))
    whole = lambda shape: pl.BlockSpec(shape, lambda i: (0,) * len(shape))
    return pl.pallas_call(
        body, name="forward_ffn", grid=(t // tm,),
        out_shape=(jax.ShapeDtypeStruct((t, d), F32), jax.ShapeDtypeStruct((t, 2 * ff), BF16),
                   jax.ShapeDtypeStruct((t, d), BF16), jax.ShapeDtypeStruct((8, d), F32)),
        in_specs=[tile, tile, whole((8, d)), whole((8, d)), ANY_SPEC, ANY_SPEC],
        out_specs=[tile, pl.BlockSpec((tm, 2 * ff), lambda i: (i, 0)), tile, whole((8, d))],
        scratch_shapes=[pltpu.VMEM(w_fi_t.shape, BF16), pltpu.VMEM(w_fo.shape, BF16)],
        compiler_params=_params(("arbitrary",)),
    )(x1, tgt, mod, prm, w_fi_t, w_fo)


def _backward_ffn(dx2, x1, ab, mod, prm, w_fi_t, w_fo, tm):
    t, d = x1.shape
    ff = w_fo.shape[0]

    def body(dx2_ref, x1_ref, ab_ref, mod_ref, prm_ref, wfi_hbm, wfo_hbm,
             dx1_ref, df_ref, act_ref, dab_ref, sums_ref, wfi_ref, wfo_ref):
        i = pl.program_id(0)

        @pl.when(i == 0)
        def _():
            pltpu.sync_copy(wfi_hbm, wfi_ref)
            pltpu.sync_copy(wfo_hbm, wfo_ref)
            sums_ref[...] = jnp.zeros((8, d), F32)

        dx2v = dx2_ref[...]
        df = (mod_ref[M_G2:M_G2 + 1, :] * dx2v).astype(BF16)
        df_ref[...] = df
        dh2 = jnp.zeros((tm, d), F32)
        for c0, c1 in _ffn_chunks(ff):
            dact = _dot_nt(df, wfo_ref[c0:c1, :])
            a = ab_ref[:, c0:c1].astype(F32)
            b = ab_ref[:, ff + c0:ff + c1].astype(F32)
            s = _sigmoid(a)
            sil = a * s
            act_ref[:, c0:c1] = (sil * b).astype(BF16)
            da = (dact * b * (s * (1.0 + a * (1.0 - s)))).astype(BF16)
            db = (dact * sil).astype(BF16)
            dab_ref[:, c0:c1] = da
            dab_ref[:, ff + c0:ff + c1] = db
            dh2 = dh2 + _dot(da, wfi_ref[c0:c1, :]) + _dot(db, wfi_ref[ff + c0:ff + c1, :])
        x1v = x1_ref[...]
        r2 = lax.rsqrt(_rowmean(x1v * x1v) + EPS)
        xn2 = x1v * r2
        gffn = prm_ref[P_GFFN:P_GFFN + 1, :]
        scale = 1.0 + mod_ref[M_SC2:M_SC2 + 1, :]
        dxn2 = dh2 * gffn * scale
        dx1_ref[...] = dx2v + r2 * (dxn2 - xn2 * _rowmean(dxn2 * xn2))
        hx = dh2 * xn2
        sums_ref[0:1, :] = sums_ref[0:1, :] + _colsum(dh2)
        sums_ref[1:2, :] = sums_ref[1:2, :] + _colsum(hx) * gffn
        sums_ref[2:3, :] = sums_ref[2:3, :] + _colsum(hx) * scale

    tile = pl.BlockSpec((tm, d), lambda i: (i, 0))
    whole = lambda shape: pl.BlockSpec(shape, lambda i: (0,) * len(shape))
    wide = lambda n: pl.BlockSpec((tm, n), lambda i: (i, 0))
    return pl.pallas_call(
        body, name="backward_ffn", grid=(t // tm,),
        out_shape=(jax.ShapeDtypeStruct((t, d), F32), jax.ShapeDtypeStruct((t, d), BF16),
                   jax.ShapeDtypeStruct((t, ff), BF16), jax.ShapeDtypeStruct((t, 2 * ff), BF16),
                   jax.ShapeDtypeStruct((8, d), F32)),
        in_specs=[tile, tile, wide(2 * ff), whole((8, d)), whole((8, d)), ANY_SPEC, ANY_SPEC],
        out_specs=[tile, tile, wide(ff), wide(2 * ff), whole((8, d))],
        scratch_shapes=[pltpu.VMEM(w_fi_t.shape, BF16), pltpu.VMEM(w_fo.shape, BF16)],
        compiler_params=_params(("arbitrary",)),
    )(dx2, x1, ab, mod, prm, w_fi_t, w_fo)


def _backward_mix(dx1, proj, y_a, y_b, mix, u1, mod, prm, taps, w_so, w_co, w_o, tm, partials):
    t, d = dx1.shape
    nt = t // tm
    na = len(partials)
    hs, hc = HALO_SHORT, HALO_CONF
    r3, r31 = 0, 8

    def body(*refs):
        (dx1_ref, proj_ref, halo_ref, ya_ref, yb_ref, mix_ref, u1_ref, mod_ref, prm_ref, taps_ref,
         wso_ref, wco_ref, wo_ref) = refs[:13]
        (dproj_ref, dmix_ref, dya_ref, dyb_ref, merged_ref, yapre_ref, u3_ref, sums_ref, dw3_ref,
         dw31_ref) = refs[13 + na:23 + na]
        cv_ext, u0_ext, d3_ext, du1_ext, tmp_ref = refs[23 + 2 * na:28 + 2 * na]
        scatter_start, scatter_finish = _scatter_plan(refs[13:13 + na], refs[23 + na:23 + 2 * na], *refs[28 + 2 * na:])
        i = pl.program_id(0)
        first_tile = i == nt - 1

        @pl.when(i == 0)
        def _():
            scatter_start()
            sums_ref[...] = jnp.zeros((8, d), F32)
            dw3_ref[...] = jnp.zeros(dw3_ref.shape, F32)
            dw31_ref[...] = jnp.zeros(dw31_ref.shape, F32)
            d3_ext[tm:tm + hs, :] = jnp.zeros((hs, d), F32)
            du1_ext[tm:tm + hc, :] = jnp.zeros((hc, d), F32)

        def col(g):
            return proj_ref[:, g * d:(g + 1) * d].astype(F32)

        def hcol(g, rows):
            v = halo_ref[HALO_CONF - rows:HALO_CONF, g * d:(g + 1) * d].astype(F32)
            return jnp.where(first_tile, 0.0, v)

        dx1v = dx1_ref[...]
        mixv = mix_ref[...].astype(F32)
        dmix = (mod_ref[M_G1:M_G1 + 1, :] * dx1v).astype(BF16)
        dmix_ref[...] = dmix
        sums_ref[0:1, :] = sums_ref[0:1, :] + _colsum(dx1v * mixv)
        dmerged = _dot_nt(dmix, wo_ref[...])
        ga = _sigmoid(col(5))
        gb = _sigmoid(col(6))
        yav = ya_ref[...].astype(F32)
        ybv = yb_ref[...].astype(F32)
        dya_f = dmerged * ga
        dyb_f = dmerged * gb
        dya = dya_f.astype(BF16)
        dyb = dyb_f.astype(BF16)
        dya_ref[...] = dya
        dyb_ref[...] = dyb
        dproj_ref[:, 5 * d:6 * d] = (dya_f * yav * (1.0 - ga)).astype(BF16)
        dproj_ref[:, 6 * d:7 * d] = (dyb_f * ybv * (1.0 - gb)).astype(BF16)
        merged_ref[...] = (ga * yav + gb * ybv).astype(BF16)

        dya_pre = _dot_nt(dya, wso_ref[...])
        c_s, v_s, b_s = col(1), col(2), col(0)
        cv_ext[0:hs, :] = hcol(1, hs) * hcol(2, hs)
        cv_ext[hs:hs + tm, :] = c_s * v_s
        _tap_conv(cv_ext, taps_ref, [(r3 + wi, off) for wi, off in _causal_taps(SHORT_K, hs)], tm, hs, tmp_ref)
        conv3 = tmp_ref[...]
        yapre_ref[...] = (b_s * conv3).astype(BF16)
        dproj_ref[:, 0:d] = (dya_pre * conv3).astype(BF16)
        d3_ext[0:tm, :] = dya_pre * b_s
        _tap_wgrad(d3_ext, cv_ext, _causal_taps(SHORT_K, hs), tm, hs, dw3_ref)
        _tap_conv(d3_ext, taps_ref, [(r3 + wi, off) for wi, off in _anticausal_taps(SHORT_K)], tm, hs, tmp_ref)
        dcv = tmp_ref[...]
        dproj_ref[:, d:2 * d] = (dcv * v_s).astype(BF16)
        dproj_ref[:, 2 * d:3 * d] = (dcv * c_s).astype(BF16)
        d3_ext[tm:tm + hs, :] = d3_ext[0:hs, :]

        du3 = _dot_nt(dyb, wco_ref[...])
        u1v = u1_ref[...].astype(F32)
        mu = _rowmean(u1v)
        uc = u1v - mu
        rstd = lax.rsqrt(_rowmean(uc * uc) + LN_EPS)
        uhat = uc * rstd
        lng = prm_ref[P_LNG:P_LNG + 1, :]
        u2 = uhat * lng + prm_ref[P_LNB:P_LNB + 1, :]
        s2 = _sigmoid(u2)
        u3_ref[...] = (u2 * s2).astype(BF16)
        du2 = du3 * (s2 * (1.0 + u2 * (1.0 - s2)))
        sums_ref[1:2, :] = sums_ref[1:2, :] + _colsum(du2 * uhat)
        sums_ref[2:3, :] = sums_ref[2:3, :] + _colsum(du2)
        duhat = du2 * lng
        du1 = rstd * (duhat - _rowmean(duhat) - uhat * _rowmean(duhat * uhat))
        sums_ref[3:4, :] = sums_ref[3:4, :] + _colsum(du1)
        du1_ext[0:tm, :] = du1
        v_c = col(3)
        sg = _sigmoid(col(4))
        u0_ext[0:hc, :] = hcol(3, hc) * _sigmoid(hcol(4, hc))
        u0_ext[hc:hc + tm, :] = v_c * sg
        _tap_wgrad(du1_ext, u0_ext, _causal_taps(CONF_K, hc), tm, hc, dw31_ref)
        _tap_conv(du1_ext, taps_ref, [(r31 + wi, off) for wi, off in _anticausal_taps(CONF_K)], tm, hc, tmp_ref)
        dv_c = tmp_ref[...] * sg
        dproj_ref[:, 3 * d:4 * d] = dv_c.astype(BF16)
        dproj_ref[:, 4 * d:5 * d] = (dv_c * v_c * (1.0 - sg)).astype(BF16)
        du1_ext[tm:tm + hc, :] = du1_ext[0:hc, :]

        @pl.when(i == nt - 1)
        def _():
            scatter_finish()

    rev = lambda i: (nt - 1 - i, 0)
    tile = pl.BlockSpec((tm, d), rev)
    whole = lambda shape: pl.BlockSpec(shape, lambda i: (0,) * len(shape))
    hblocks = tm // HALO_CONF
    halo = pl.BlockSpec((HALO_CONF, 7 * d), lambda i: (jnp.maximum((nt - 1 - i) * hblocks - 1, 0), 0))
    bf = jax.ShapeDtypeStruct((t, d), BF16)
    res = pl.pallas_call(
        body, name="backward_mix", grid=(nt,),
        out_shape=(jax.ShapeDtypeStruct((t, 7 * d), BF16), bf, bf, bf, bf, bf, bf,
                   jax.ShapeDtypeStruct((8, d), F32),
                   jax.ShapeDtypeStruct((SUBLANES * SHORT_K, d), F32),
                   jax.ShapeDtypeStruct((SUBLANES * CONF_K, d), F32))
        + tuple(jax.ShapeDtypeStruct(p.shape, p.dtype) for p in partials),
        in_specs=[tile, pl.BlockSpec((tm, 7 * d), rev), halo, tile, tile, tile, tile,
                  whole((8, d)), whole((8, d)), whole(taps.shape), whole((d, d)), whole((d, d)), whole((d, d))]
        + [ANY_SPEC] * na,
        out_specs=[pl.BlockSpec((tm, 7 * d), rev), tile, tile, tile, tile, tile, tile,
                   whole((8, d)), whole((SUBLANES * SHORT_K, d)), whole((SUBLANES * CONF_K, d))] + [ANY_SPEC] * na,
        scratch_shapes=[pltpu.VMEM((hs + tm, d), F32), pltpu.VMEM((hc + tm, d), F32),
                        pltpu.VMEM((tm + hs, d), F32), pltpu.VMEM((tm + hc, d), F32),
                        pltpu.VMEM((tm, d), F32)] + _comm_scratch(na),
        compiler_params=_params(("arbitrary",)),
    )(dx1, proj, proj, y_a, y_b, mix, u1, mod, prm, taps, w_so, w_co, w_o, *partials)
    return res[:10], res[10:]


def _backward_in(dproj, x, dx1, mod, prm, w_in_g, tm, partials, partial_view, exchanged_shapes):
    t, d = x.shape
    n_all = w_in_g.shape[1]
    na = len(partials)
    nt = t // tm

    def body(*refs):
        dproj_ref, x_ref, dx1_ref, mod_ref, prm_ref, w_hbm = refs[:6]
        gx_ref, sums_ref = refs[6 + na:8 + na]
        w_ref = refs[8 + 2 * na]
        scatter_start, scatter_finish = _scatter_plan(refs[6:6 + na], refs[8 + na:8 + 2 * na], *refs[9 + 2 * na:],
                                                      view=partial_view)

        @pl.when(pl.program_id(0) == 0)
        def _():
            scatter_start()
            pltpu.sync_copy(w_hbm, w_ref)
            sums_ref[...] = jnp.zeros((8, d), F32)

        dh = _dot_nt(dproj_ref[...], w_ref[...])
        xv = x_ref[...]
        r1 = lax.rsqrt(_rowmean(xv * xv) + EPS)
        xn = xv * r1
        gmix = prm_ref[P_GMIX:P_GMIX + 1, :]
        scale = 1.0 + mod_ref[M_SC1:M_SC1 + 1, :]
        dxn = dh * gmix * scale
        gx_ref[...] = dx1_ref[...] + r1 * (dxn - xn * _rowmean(dxn * xn))
        hx = dh * xn
        sums_ref[0:1, :] = sums_ref[0:1, :] + _colsum(dh)
        sums_ref[1:2, :] = sums_ref[1:2, :] + _colsum(hx) * gmix
        sums_ref[2:3, :] = sums_ref[2:3, :] + _colsum(hx) * scale

        @pl.when(pl.program_id(0) == nt - 1)
        def _():
            scatter_finish()

    tile = pl.BlockSpec((tm, d), lambda i: (i, 0))
    whole = pl.BlockSpec((8, d), lambda i: (0, 0))
    res = pl.pallas_call(
        body, name="backward_in", grid=(nt,),
        out_shape=(jax.ShapeDtypeStruct((t, d), F32), jax.ShapeDtypeStruct((8, d), F32))
        + tuple(jax.ShapeDtypeStruct(s, p.dtype) for p, s in zip(partials, exchanged_shapes)),
        in_specs=[pl.BlockSpec((tm, n_all), lambda i: (i, 0)), tile, tile, whole, whole, ANY_SPEC] + [ANY_SPEC] * na,
        out_specs=[tile, whole] + [ANY_SPEC] * na,
        scratch_shapes=[pltpu.VMEM(w_in_g.shape, BF16)] + _comm_scratch(na),
        compiler_params=_params(("arbitrary",)),
    )(dproj, x, dx1, mod, prm, w_in_g, *partials)
    return res[0], res[1], res[2:]


def _weight_grad(a, b, a_spec, b_spec, ns, m, n, nk, name, partials=(), by_columns=False):
    na = len(partials)

    def body(*refs):
        a_ref, b_ref = refs[:2]
        o_ref = refs[2 + na]
        acc_ref = refs[3 + 2 * na]
        s, k = pl.program_id(0), pl.program_id(1)
        if na:
            scatter_start, scatter_finish = _scatter_plan(refs[2:2 + na], refs[3 + na:3 + 2 * na], *refs[4 + 2 * na:])

            @pl.when((s == 0) & (k == 0))
            def _():
                scatter_start()

        av = a_ref[0] if len(a_ref.shape) == 3 else a_ref[...]
        bv = b_ref[0] if len(b_ref.shape) == 3 else b_ref[...]
        part = _dot_tn(av, bv)

        @pl.when(k == 0)
        def _():
            acc_ref[...] = part

        @pl.when(k > 0)
        def _():
            acc_ref[...] = acc_ref[...] + part

        @pl.when(k == nk - 1)
        def _():
            if by_columns:
                o_ref[...] = acc_ref[...].astype(BF16)
            else:
                o_ref[0] = acc_ref[...].astype(BF16)

        if na:
            @pl.when((s == ns - 1) & (k == nk - 1))
            def _():
                scatter_finish()

    if by_columns:
        out_shape, out_spec = (m, ns * n), pl.BlockSpec((m, n), lambda s, k: (0, s))
    else:
        out_shape, out_spec = (ns, m, n), pl.BlockSpec((1, m, n), lambda s, k: (s, 0, 0))
    res = pl.pallas_call(
        body, name=name, grid=(ns, nk),
        out_shape=(jax.ShapeDtypeStruct(out_shape, BF16),)
        + tuple(jax.ShapeDtypeStruct(p.shape, p.dtype) for p in partials),
        in_specs=[a_spec, b_spec] + [ANY_SPEC] * na,
        out_specs=[out_spec] + [ANY_SPEC] * na,
        scratch_shapes=[pltpu.VMEM((m, n), F32)] + (_comm_scratch(na) if na else []),
        compiler_params=_params(("arbitrary", "arbitrary")),
    )(a, b, *partials)
    return (res[0], res[1:]) if na else res[0]


def _adamw(w, g, m, v):
    m = ADAM_B1 * m + (1.0 - ADAM_B1) * g
    v = ADAM_B2 * v + (1.0 - ADAM_B2) * (g * g)
    m_hat = m / (1.0 - ADAM_B1 ** ADAM_STEP)
    v_hat = v / (1.0 - ADAM_B2 ** ADAM_STEP)
    delta = -ADAM_LR * (m_hat / (jnp.sqrt(v_hat) + ADAM_EPS) + ADAM_WD * w)
    return delta, m, v


def _adamw_shard(parts, w, m, v, tr, name):
    r, c = w.shape

    def body(p_ref, w_ref, m_ref, v_ref, g_ref, d_ref, nm_ref, nv_ref):
        g = p_ref[0].astype(F32)
        for s in range(1, N_DEV):
            g = g + p_ref[s].astype(F32)
        delta, nm, nv = _adamw(w_ref[...], g, m_ref[...], v_ref[...])
        g_ref[...] = g
        d_ref[...] = delta
        nm_ref[...] = nm
        nv_ref[...] = nv

    tile = pl.BlockSpec((tr, c), lambda i: (i, 0))
    return pl.pallas_call(
        body, name=name, grid=(r // tr,),
        out_shape=(jax.ShapeDtypeStruct((r, c), F32),) * 4,
        in_specs=[pl.BlockSpec((N_DEV, tr, c), lambda i: (0, i, 0)), tile, tile, tile],
        out_specs=[tile] * 4,
        compiler_params=_params(("arbitrary",)),
    )(parts, w, m, v)


def _ada_update(sc_all, dmod_cols, w, m, v, tr):
    d, n = w.shape

    def body(sc_ref, dm_ref, w_ref, m_ref, v_ref, g_ref, d_ref, nm_ref, nv_ref):
        g = lax.dot_general(sc_ref[...], dm_ref[...], (((0,), (0,)), ((), ())),
                            preferred_element_type=F32, precision=lax.Precision.HIGHEST)
        delta, nm, nv = _adamw(w_ref[...], g, m_ref[...], v_ref[...])
        g_ref[...] = g
        d_ref[...] = delta
        nm_ref[...] = nm
        nv_ref[...] = nv

    tile = pl.BlockSpec((tr, n), lambda i: (i, 0))
    return pl.pallas_call(
        body, name="ada_update", grid=(d // tr,),
        out_shape=(jax.ShapeDtypeStruct((d, n), F32),) * 4,
        in_specs=[pl.BlockSpec((N_DEV, tr), lambda i: (0, i)), pl.BlockSpec((N_DEV, n), lambda i: (0, 0)),
                  tile, tile, tile],
        out_specs=[tile] * 4,
        compiler_params=_params(("arbitrary",)),
    )(sc_all, dmod_cols, w, m, v)


def _small_exchange(vec, cg):
    l = vec.shape[2]
    rows = cg.shape[1]

    def body(vec_ref, cg_ref, vall_ref, cgr_ref, send_sems, recv_sems):
        x, y, c = _my_coords()
        me = _slot(x, y, c)
        vall_ref[me] = vec_ref[0]
        cgr_ref[me] = cg_ref[me]
        copies = []
        for k in range(1, N_DEV):
            peer = (_flip(x, k & 4), _flip(y, k & 2), _flip(c, k & 1))
            copies.append(pltpu.make_async_remote_copy(
                src_ref=vall_ref.at[me], dst_ref=vall_ref.at[me], send_sem=send_sems.at[k - 1],
                recv_sem=recv_sems.at[k - 1], device_id=peer, device_id_type=MESH))
            copies.append(pltpu.make_async_remote_copy(
                src_ref=cg_ref.at[_slot(*peer)], dst_ref=cgr_ref.at[me], send_sem=send_sems.at[7 + k - 1],
                recv_sem=recv_sems.at[7 + k - 1], device_id=peer, device_id_type=MESH))
        for cp in copies:
            cp.start()
        for cp in copies:
            cp.wait_recv()
        for cp in copies:
            cp.wait_send()

    vm = pl.BlockSpec(memory_space=pltpu.VMEM)
    return pl.pallas_call(
        body, name="small_exchange",
        out_shape=(jax.ShapeDtypeStruct((N_DEV, 1, l), F32), jax.ShapeDtypeStruct((N_DEV, rows, LANES), F32)),
        in_specs=[vm, vm], out_specs=[vm, vm],
        scratch_shapes=[pltpu.SemaphoreType.DMA((14,)), pltpu.SemaphoreType.DMA((14,))],
        compiler_params=_params(),
    )(vec, cg)


def _small_update(vall, cgr, smalls):
    ns = len(smalls)

    def body(*refs):
        vall_ref, cgr_ref = refs[0], refs[1]
        wmv = refs[2:2 + 3 * ns]
        outs = refs[2 + 3 * ns:]
        for p, (_, _, _, lo, hi, kind) in enumerate(smalls):
            w_ref, m_ref, v_ref = wmv[3 * p:3 * p + 3]
            part = (lambda s: vall_ref[s, :, lo:hi]) if kind == "vec" else (lambda s: cgr_ref[s, lo:hi, :])
            g = part(0)
            for s in range(1, N_DEV):
                g = g + part(s)
            delta, nm, nv = _adamw(w_ref[...], g, m_ref[...], v_ref[...])
            for o_ref, val in zip(outs[4 * p:4 * p + 4], (g, delta, nm, nv)):
                o_ref[...] = val

    vm = pl.BlockSpec(memory_space=pltpu.VMEM)
    args = [vall, cgr]
    out_shape = []
    for w, m, v, _, _, _ in smalls:
        args += [w, m, v]
        out_shape += [jax.ShapeDtypeStruct(w.shape, F32)] * 4
    res = pl.pallas_call(
        body, name="small_update",
        out_shape=tuple(out_shape),
        in_specs=[vm] * len(args), out_specs=[vm] * len(out_shape),
        compiler_params=_params(),
    )(*args)
    return [res[4 * p:4 + 4 * p] for p in range(ns)]


def _pick(t, want):
    return want if t % want == 0 else t


def kernel(x, c, w_ada, b_ada, norm_mix_g, w_in, conv_short_w, w_short_out, conv_conf_w, conv_conf_b, conf_ln_g, conf_ln_b, w_conf_out, w_o, norm_ffn_g, w_ffn_in, w_ffn_out, final_norm_g, loss_target, m_w_ada, m_b_ada, m_norm_mix_g, m_w_in, m_conv_short_w, m_w_short_out, m_conv_conf_w, m_conv_conf_b, m_conf_ln_g, m_conf_ln_b, m_w_conf_out, m_w_o, m_norm_ffn_g, m_w_ffn_in, m_w_ffn_out, m_final_norm_g, v_w_ada, v_b_ada, v_norm_mix_g, v_w_in, v_conv_short_w, v_w_short_out, v_conv_conf_w, v_conv_conf_b, v_conf_ln_g, v_conf_ln_b, v_w_conf_out, v_w_o, v_norm_ffn_g, v_w_ffn_in, v_w_ffn_out, v_final_norm_g):
    t, d = x.shape[1], x.shape[2]
    x2 = x.reshape(t, d)
    tgt = loss_target.reshape(t, d)
    me = _slot(*_my_coords())
    tm = _pick(t, 256)
    tm_fwd_in = _pick(t, 1024)
    tk = _pick(t, 2048)

    taps_loc = jnp.zeros((40, LANES), F32)
    taps_loc = taps_loc.at[0:SHORT_K].set(conv_short_w[0]).at[8:8 + CONF_K].set(conv_conf_w[0])
    prm = jnp.concatenate([norm_mix_g, norm_ffn_g, final_norm_g.reshape(1, d), conv_conf_b, conf_ln_g, conf_ln_b,
                           jnp.zeros((2, d), F32)], axis=0)

    n_in = w_in.shape[2]
    nk = t // tk
    tok = pl.BlockSpec((tk, d), lambda s, k: (k, 0))
    rows = d // N_DEV
    frows = w_ffn_out.shape[1]

    proj, h, w_in_g, mod, sc_all3, taps, (w_so_g, w_co_g, w_o_g) = _forward_in(
        jnp.reshape(me, (1,)).astype(jnp.int32), x2, prm, w_in[0].astype(BF16),
        (c, w_ada[0], b_ada, taps_loc), tm_fwd_in,
        [w_short_out[0].astype(BF16), w_conf_out[0].astype(BF16), w_o[0].astype(BF16)])
    w_so = w_so_g.reshape(d, d)
    w_co = w_co_g.reshape(d, d)
    w_oo = w_o_g.reshape(d, d)
    w_fi_t, m_fi_t, v_fi_t = w_ffn_in[0].T, m_w_ffn_in[0].T, v_w_ffn_in[0].T
    (x1, y_a, y_b, mix, u1), (w_fi_g, w_fo_g) = _forward_mix(
        x2, proj, mod, prm, taps, w_so, w_co, w_oo, tm, [w_fi_t.astype(BF16), w_ffn_out[0].astype(BF16)])
    fb = w_fi_g.shape[1]
    ff = N_DEV * frows
    w_fi_all = w_fi_g.reshape(N_DEV * fb, d)
    w_fo_all = w_fo_g.reshape(ff, d)
    dx2, ab, h2, sums_f = _forward_ffn(x1, tgt, mod, prm, w_fi_all, w_fo_all, tm)

    dx1, df, act, dab, sums_b = _backward_ffn(dx2, x1, ab, mod, prm, w_fi_all, w_fo_all, tm)
    fb2 = 2 * fb
    pair_tok = pl.BlockSpec((tk, fb2), lambda s, k: (k, s))
    g_fi = _weight_grad(dab, h2, pair_tok, tok, N_DEV // 2, fb2, d, nk, "grad_w_ffn_in")
    g_fo = _weight_grad(act, df, pair_tok, tok, N_DEV // 4, fb2, d, nk, "grad_w_ffn_out")
    (dproj, dmix, dya, dyb, merged, ya_pre, u3, sums_m, dw3p, dw31p), (p_fi, p_fo) = _backward_mix(
        dx1, proj, y_a, y_b, mix, u1, mod, prm, taps, w_so, w_co, w_oo, tm,
        [g_fi.reshape(N_DEV, fb, d), g_fo.reshape(N_DEV, frows, d)])
    g_so = _weight_grad(ya_pre, dya, tok, tok, 1, d, d, nk, "grad_w_short_out")
    g_co = _weight_grad(u3, dyb, tok, tok, 1, d, d, nk, "grad_w_conf_out")
    g_oo = _weight_grad(merged, dmix, tok, tok, 1, d, d, nk, "grad_w_o")
    n_blk = _pick(N_DEV * n_in, d)
    tk_in = _pick(t, 4096)
    g_in, (p_so, p_co, p_oo) = _weight_grad(
        h, dproj, pl.BlockSpec((tk_in, d), lambda s, k: (k, 0)), pl.BlockSpec((tk_in, n_blk), lambda s, k: (k, s)),
        N_DEV * n_in // n_blk, d, n_blk, t // tk_in, "grad_w_in",
        [g_so.reshape(N_DEV, rows, d), g_co.reshape(N_DEV, rows, d), g_oo.reshape(N_DEV, rows, d)], by_columns=True)
    grad_x, sums_i, (p_in,) = _backward_in(dproj, x2, dx1, mod, prm, w_in_g, tm, [g_in], _column_block(n_in),
                                           [(N_DEV, d, n_in)])

    up_in = _adamw_shard(p_in, w_in[0], m_w_in[0], v_w_in[0], _pick(d, 256), "adamw_w_in")
    up_so = _adamw_shard(p_so, w_short_out[0], m_w_short_out[0], v_w_short_out[0], rows, "adamw_w_short_out")
    up_co = _adamw_shard(p_co, w_conf_out[0], m_w_conf_out[0], v_w_conf_out[0], rows, "adamw_w_conf_out")
    up_oo = _adamw_shard(p_oo, w_o[0], m_w_o[0], v_w_o[0], rows, "adamw_w_o")
    up_fi = tuple(a.T for a in _adamw_shard(p_fi, w_fi_t, m_fi_t, v_fi_t, fb // 2, "adamw_w_ffn_in"))
    up_fo = _adamw_shard(p_fo, w_ffn_out[0], m_w_ffn_out[0], v_w_ffn_out[0], frows, "adamw_w_ffn_out")

    vec = jnp.concatenate([sums_i[0:2], sums_m[0:1], sums_b[0:2], sums_f[1:2],
                           sums_i[2:3], sums_m[3:4], sums_m[1:3], sums_b[2:3], sums_f[0:1],
                           sums_f[2:3]], axis=0)
    vec = vec.reshape(1, 1, 13 * d)
    dw3 = dw3p.reshape(SHORT_K, SUBLANES, d).sum(axis=1)
    dw31 = dw31p.reshape(CONF_K, SUBLANES, d).sum(axis=1)
    cg = jnp.zeros((40, d), F32).at[0:SHORT_K].set(dw3).at[8:8 + CONF_K].set(dw31)
    cg = cg.reshape(40, N_DEV, LANES).transpose(1, 0, 2)
    fin = lambda a: a.reshape(1, d)
    tap = lambda a: a.reshape(a.shape[1:])
    smalls = [
        (b_ada, m_b_ada, v_b_ada, 0, 6 * d, "vec"),
        (norm_mix_g, m_norm_mix_g, v_norm_mix_g, 6 * d, 7 * d, "vec"),
        (tap(conv_short_w), tap(m_conv_short_w), tap(v_conv_short_w), 0, SHORT_K, "cg"),
        (tap(conv_conf_w), tap(m_conv_conf_w), tap(v_conv_conf_w), 8, 8 + CONF_K, "cg"),
        (conv_conf_b, m_conv_conf_b, v_conv_conf_b, 7 * d, 8 * d, "vec"),
        (conf_ln_g, m_conf_ln_g, v_conf_ln_g, 8 * d, 9 * d, "vec"),
        (conf_ln_b, m_conf_ln_b, v_conf_ln_b, 9 * d, 10 * d, "vec"),
        (norm_ffn_g, m_norm_ffn_g, v_norm_ffn_g, 10 * d, 11 * d, "vec"),
        (fin(final_norm_g), fin(m_final_norm_g), fin(v_final_norm_g), 11 * d, 12 * d, "vec"),
    ]
    vall, cgr = _small_exchange(vec, cg)
    up_small = _small_update(vall, cgr, smalls)
    n_ada = w_ada.shape[2]
    dmod_all = vall.reshape(N_DEV, 13 * d)[:, 0:6 * d]
    dmod_cols = lax.dynamic_slice(dmod_all, (0, me * n_ada), (N_DEV, n_ada))
    up_ada = _ada_update(sc_all3.reshape(N_DEV, d), dmod_cols, w_ada[0], m_w_ada[0], v_w_ada[0], _pick(d, 256))

    loss = jnp.sum(vall.reshape(N_DEV, 13 * d)[:, 12 * d:])

    lead = lambda a: a.reshape((1,) + a.shape)
    ups = [tuple(lead(a) for a in up_ada), up_small[0], up_small[1], tuple(lead(a) for a in up_in),
           tuple(lead(a) for a in up_small[2]), tuple(lead(a) for a in up_so), tuple(lead(a) for a in up_small[3]),
           up_small[4], up_small[5], up_small[6],
           tuple(lead(a) for a in up_co), tuple(lead(a) for a in up_oo), up_small[7],
           tuple(lead(a) for a in up_fi), tuple(lead(a) for a in up_fo),
           tuple(a.reshape(d) for a in up_small[8])]
    grads = [u[0] for u in ups]
    deltas = [u[1] for u in ups]
    new_m = [u[2] for u in ups]
    new_v = [u[3] for u in ups]
    return (loss, grad_x.reshape(1, t, d), *grads, *deltas, *new_m, *new_v)
```

```python
import functools

import jax
import jax.numpy as jnp
from jax import lax
from jax.experimental import pallas as pl
from jax.experimental.pallas import tpu as pltpu

F32 = jnp.float32
BF16 = jnp.bfloat16
MESH = pl.DeviceIdType.MESH

N_DEV = 8
EPS = 1e-6
LN_EPS = 1e-5
SHORT_K = 3
CONF_K = 31
ADAM_LR = 0.001
ADAM_B1 = 0.9
ADAM_B2 = 0.999
ADAM_EPS = 1e-08
ADAM_WD = 0.01
ADAM_STEP = 10

LANES = 128
SUBLANES = 8
CONV_ROWS = 64
HALO_SHORT = 8
HALO_CONF = 32
VMEM_LIMIT = 56 * 1024 * 1024

M_SH1, M_SC1, M_G1, M_SH2, M_SC2, M_G2 = range(6)
P_GMIX, P_GFFN, P_GFIN, P_CBIAS, P_LNG, P_LNB = range(6)


def _params(sem=None, **kw):
    return pltpu.CompilerParams(dimension_semantics=sem, vmem_limit_bytes=VMEM_LIMIT, **kw)


def _sigmoid(v):
    return jax.nn.sigmoid(v)


def _dot(a, b):
    return jnp.dot(a, b, preferred_element_type=F32)


def _dot_nt(a, b):
    return lax.dot_general(a, b, (((1,), (1,)), ((), ())), preferred_element_type=F32)


def _dot_tn(a, b):
    return lax.dot_general(a, b, (((0,), (0,)), ((), ())), preferred_element_type=F32)


def _colsum(v):
    return jnp.sum(v, axis=0, keepdims=True)


def _rowmean(v):
    return jnp.mean(v, axis=-1, keepdims=True)


def _my_coords():
    return lax.axis_index("x"), lax.axis_index("y"), lax.axis_index("c")


def _slot(px, py, pc):
    return 4 * px + 2 * py + pc


def _flip(v, bit):
    return 1 - v if bit else v


def _taps_by_residue(taps):
    by_res = {}
    for wi, off in taps:
        by_res.setdefault(off % SUBLANES, []).append((wi, off // SUBLANES))
    return sorted(by_res.items())


def _tap_conv(ext_ref, w_ref, taps, tm, extra, out_ref):
    d = out_ref.shape[1]
    rb = min(CONV_ROWS, tm)
    wrows = rb + extra
    groups = _taps_by_residue(taps)

    def block(i, carry):
        base = pl.multiple_of(i * rb, SUBLANES)
        for lc in range(d // LANES):
            ls = pl.ds(lc * LANES, LANES)
            win = ext_ref[pl.ds(base, wrows), ls]
            acc = None
            for r, lst in groups:
                sh = win if r == 0 else pltpu.roll(win, wrows - r, 0)
                for wi, q in lst:
                    term = w_ref[wi:wi + 1, ls] * sh[SUBLANES * q:SUBLANES * q + rb, :]
                    acc = term if acc is None else acc + term
            out_ref[pl.ds(base, rb), ls] = acc
        return carry

    lax.fori_loop(0, tm // rb, block, 0)


def _tap_wgrad(a_ref, ext_ref, taps, tm, extra, acc_ref):
    d = a_ref.shape[1]
    rb = min(CONV_ROWS, tm)
    wrows = rb + extra
    groups = _taps_by_residue(taps)

    def block(i, carry):
        base = pl.multiple_of(i * rb, SUBLANES)
        for lc in range(d // LANES):
            ls = pl.ds(lc * LANES, LANES)
            a_blk = a_ref[pl.ds(base, rb), ls]
            win = ext_ref[pl.ds(base, wrows), ls]
            for r, lst in groups:
                sh = win if r == 0 else pltpu.roll(win, wrows - r, 0)
                for wi, q in lst:
                    prod = a_blk * sh[SUBLANES * q:SUBLANES * q + rb, :]
                    part = prod[0:SUBLANES, :]
                    for s in range(1, rb // SUBLANES):
                        part = part + prod[SUBLANES * s:SUBLANES * (s + 1), :]
                    rows = pl.ds(SUBLANES * wi, SUBLANES)
                    acc_ref[rows, ls] = acc_ref[rows, ls] + part
        return carry

    lax.fori_loop(0, tm // rb, block, 0)


def _causal_taps(k, halo):
    return [(i, halo - (k - 1) + i) for i in range(k)]


def _anticausal_taps(k):
    return [(i, (k - 1) - i) for i in range(k)]


def _ada_scratch(d, nloc, trows):
    return [pltpu.VMEM((N_DEV, 1, d), F32), pltpu.VMEM((N_DEV, 1, nloc), F32), pltpu.VMEM((N_DEV, 1, nloc), F32),
            pltpu.VMEM((N_DEV, trows, LANES), F32), pltpu.SemaphoreType.DMA((21,)), pltpu.SemaphoreType.DMA((21,))]


def _ada_exchange(c_ref, w_ref, b_ref, t_ref, mod_ref, sc_ref, taps_ref,
                  scall_ref, part_ref, modrecv_ref, tapsall_ref, send_sems, recv_sems):
    x, y, cc = _my_coords()
    me = _slot(x, y, cc)
    d = c_ref.shape[1]
    cv = c_ref[...]
    scall_ref[me] = cv * _sigmoid(cv)
    tapsall_ref[me] = t_ref[...]

    def peer_of(k):
        return (_flip(x, k & 4), _flip(y, k & 2), _flip(cc, k & 1))

    def gather_copy(ref, base, k):
        return pltpu.make_async_remote_copy(
            src_ref=ref.at[me], dst_ref=ref.at[me], send_sem=send_sems.at[base + k - 1],
            recv_sem=recv_sems.at[base + k - 1], device_id=peer_of(k), device_id_type=MESH)

    first = [gather_copy(scall_ref, 0, k) for k in range(1, N_DEV)]
    first += [gather_copy(tapsall_ref, 7, k) for k in range(1, N_DEV)]
    for cp in first:
        cp.start()
    for cp in first[:7]:
        cp.wait_recv()
    sc_all = jnp.concatenate([scall_ref[s] for s in range(N_DEV)], axis=0)
    for s in range(N_DEV):
        sc_ref[s] = scall_ref[s]
    part = jnp.dot(sc_all, w_ref[...], preferred_element_type=F32,
                   precision=lax.Precision.HIGHEST)
    for b in range(N_DEV):
        part_ref[b] = part[b:b + 1, :]
    modrecv_ref[me] = part_ref[me]
    second = []
    for k in range(1, N_DEV):
        px, py, pc = peer_of(k)
        second.append(pltpu.make_async_remote_copy(
            src_ref=part_ref.at[_slot(px, py, pc)], dst_ref=modrecv_ref.at[me],
            send_sem=send_sems.at[14 + k - 1], recv_sem=recv_sems.at[14 + k - 1],
            device_id=(px, py, pc), device_id_type=MESH))
    for cp in second:
        cp.start()
    for cp in second:
        cp.wait_recv()
    mod = jnp.concatenate([modrecv_ref[s] for s in range(N_DEV)], axis=1) + b_ref[...]
    for r in range(6):
        mod_ref[r:r + 1, :] = mod[:, r * d:(r + 1) * d]
    mod_ref[6:8, :] = jnp.zeros((2, d), F32)
    for cp in first[7:]:
        cp.wait_recv()
    taps_ref[...] = jnp.concatenate([tapsall_ref[s] for s in range(N_DEV)], axis=1)
    for cp in first + second:
        cp.wait_send()


ANY_SPEC = pl.BlockSpec(memory_space=pl.ANY)


def _comm_scratch(na):
    return [pltpu.SemaphoreType.DMA((7 * na,)), pltpu.SemaphoreType.DMA((7 * na,)), pltpu.SemaphoreType.DMA((na,))]


def _leading_block(ref, slot):
    return ref.at[slot]


def _column_block(width):
    def view(ref, slot):
        return ref.at[:, pl.ds(pl.multiple_of(slot * width, LANES), width)]
    return view


class _Gather:
    def __init__(self, ins, outs, send_sems, recv_sems, local_sems, view=_leading_block):
        self.na = len(ins)
        x, y, c = _my_coords()
        self.c = c
        self.view = view
        self.me, self.sibling = (x, y, c), (x, y, 1 - c)
        self.chips = [(1 - x, y), (x, 1 - y), (1 - x, 1 - y)]
        self.outs, self.send_sems, self.recv_sems = outs, send_sems, recv_sems
        self.mine = [pltpu.make_async_copy(ins[a], view(outs[a], _slot(*self.me)), local_sems.at[a])
                     for a in range(self.na)]
        self.first = []
        for a in range(self.na):
            self.first.append(self._copy(a, 0, self.me, self.sibling, src=ins[a]))
            self.first += [self._copy(a, 1 + j, self.me, (*chip, c), src=ins[a]) for j, chip in enumerate(self.chips)]
        self.passed = [self._copy(a, 4 + j, (*chip, c), self.sibling)
                       for a in range(self.na) for j, chip in enumerate(self.chips)]

    def _copy(self, a, k, block, to, src=None):
        dst = self.view(self.outs[a], _slot(*block))
        return pltpu.make_async_remote_copy(
            src_ref=dst if src is None else src, dst_ref=dst,
            send_sem=self.send_sems.at[7 * a + k], recv_sem=self.recv_sems.at[7 * a + k],
            device_id=to, device_id_type=MESH)

    def start(self):
        self.start_near()
        self.start_far()

    def start_near(self):
        for cp in self.mine + [cp for n, cp in enumerate(self.first) if n % 4 != 3]:
            cp.start()

    def start_far(self):
        for cp in self.first[3::4]:
            cp.start()

    def forward_chip(self, j):
        for a in range(self.na):
            self._copy(a, 1 + j, (*self.chips[j], self.c), self.me).wait_recv()
            self.passed[3 * a + j].start()

    def wait_sibling_own(self):
        for a in range(self.na):
            self._copy(a, 0, self.sibling, self.me).wait_recv()

    def wait_sibling_passed(self, j):
        for a in range(self.na):
            self._copy(a, 4 + j, (*self.chips[j], 1 - self.c), self.me).wait_recv()

    def finish_sends(self):
        for cp in self.first + self.passed:
            cp.wait_send()
        for cp in self.mine:
            cp.wait()

    def forward(self):
        for j in range(3):
            self.forward_chip(j)

    def finish(self):
        self.wait_sibling_own()
        for j in range(3):
            self.wait_sibling_passed(j)
        self.finish_sends()


def _scatter_plan(ins, outs, send_sems, recv_sems, local_sems, view=_leading_block):
    na = len(ins)
    x, y, c = _my_coords()
    me = _slot(x, y, c)
    mine = [pltpu.make_async_copy(view(ins[a], me), outs[a].at[me], local_sems.at[a]) for a in range(na)]
    copies = []
    for k in range(1, N_DEV):
        peer = (_flip(x, k & 4), _flip(y, k & 2), _flip(c, k & 1))
        for a in range(na):
            copies.append(pltpu.make_async_remote_copy(
                src_ref=view(ins[a], _slot(*peer)), dst_ref=outs[a].at[me],
                send_sem=send_sems.at[7 * a + k - 1], recv_sem=recv_sems.at[7 * a + k - 1],
                device_id=peer, device_id_type=MESH))

    def start():
        for cp in mine + copies:
            cp.start()

    def finish():
        for cp in copies:
            cp.wait_recv()
        for cp in copies:
            cp.wait_send()
        for cp in mine:
            cp.wait()

    return start, finish


def _forward_in(me_arr, x, prm, w_in_loc, ada, tm, shards):
    t, d = x.shape
    n = w_in_loc.shape[1]
    ns = N_DEV
    na = len(shards)
    nt = t // tm
    nloc = ada[1].shape[1]
    trows = ada[3].shape[0]

    def body(*refs):
        x_ref, prm_ref, wloc_ref = refs[1:4]
        ada_in = refs[4:8]
        shard_refs = refs[8:8 + na]
        proj_ref, h_ref, wg_ref, mod_ref, sc_ref, taps_ref = refs[8 + na:14 + na]
        gathered_refs = refs[14 + na:14 + 2 * na]
        hall_ref, wv_ref, wv_sem = refs[14 + 2 * na:17 + 2 * na]
        ada_scratch = refs[17 + 2 * na:23 + 2 * na]
        sems = refs[23 + 2 * na:]
        s, i = pl.program_id(0), pl.program_id(1)
        columns = _column_block(n)

        def gathers():
            return (_Gather([wloc_ref], [wg_ref], *sems[0:3], view=columns),
                    _Gather(shard_refs, gathered_refs, *sems[3:6]))

        def load_shard(src):
            cp = pltpu.make_async_copy(src, wv_ref, wv_sem)
            cp.start()
            cp.wait()

        @pl.when(i == 0)
        def _():
            g_in, g_rest = gathers()
            me = _slot(*_my_coords())

            @pl.when(s == 0)
            def _():
                _ada_exchange(*ada_in, mod_ref, sc_ref, taps_ref, *ada_scratch)
                g_in.start_near()
                load_shard(wloc_ref)

            arrivals = {1: g_in.wait_sibling_own,
                        2: functools.partial(g_in.forward_chip, 1), 3: functools.partial(g_in.wait_sibling_passed, 1),
                        4: functools.partial(g_in.forward_chip, 0), 5: functools.partial(g_in.wait_sibling_passed, 0),
                        6: functools.partial(g_in.forward_chip, 2), 7: functools.partial(g_in.wait_sibling_passed, 2)}
            for step, arrive in arrivals.items():
                @pl.when(s == step)
                def _(step=step, arrive=arrive):
                    arrive()
                    if step == 2:
                        g_in.start_far()
                    if step == 4:
                        g_rest.start()
                    if step == 7:
                        g_rest.forward()
                    load_shard(columns(wg_ref, me ^ step))

        @pl.when(s == 0)
        def _():
            xv = x_ref[...]
            r = lax.rsqrt(_rowmean(xv * xv) + EPS)
            h = xv * r * prm_ref[P_GMIX:P_GMIX + 1, :] * (1.0 + mod_ref[M_SC1:M_SC1 + 1, :]) \
                + mod_ref[M_SH1:M_SH1 + 1, :]
            hb = h.astype(BF16)
            hall_ref[i] = hb
            h_ref[...] = hb

        proj_ref[...] = _dot(hall_ref[i], wv_ref[...]).astype(BF16)

        @pl.when((s == ns - 1) & (i == nt - 1))
        def _():
            g_in, g_rest = gathers()
            g_in.finish_sends()
            g_rest.finish()

    x_tile = pl.BlockSpec((tm, d), lambda s, i, me: (jnp.where(s == 0, i, nt - 1), 0))
    whole = lambda a: pl.BlockSpec(a.shape, lambda s, i, me: (0,) * len(a.shape))
    small_out = (jax.ShapeDtypeStruct((8, d), F32), jax.ShapeDtypeStruct((N_DEV, 1, d), F32),
                 jax.ShapeDtypeStruct((trows, N_DEV * LANES), F32))
    res = pl.pallas_call(
        body, name="forward_in",
        grid_spec=pltpu.PrefetchScalarGridSpec(
            num_scalar_prefetch=1, grid=(ns, nt),
            in_specs=[x_tile, whole(prm), ANY_SPEC] + [whole(a) for a in ada] + [ANY_SPEC] * na,
            out_specs=[pl.BlockSpec((tm, n), lambda s, i, me: (i, me[0] ^ s)), x_tile, ANY_SPEC]
            + [whole(a) for a in small_out] + [ANY_SPEC] * na,
            scratch_shapes=[pltpu.VMEM((nt, tm, d), BF16), pltpu.VMEM((d, n), BF16), pltpu.SemaphoreType.DMA]
            + _ada_scratch(d, nloc, trows) + _comm_scratch(1) + _comm_scratch(na)),
        out_shape=(jax.ShapeDtypeStruct((t, ns * n), BF16), jax.ShapeDtypeStruct((t, d), BF16),
                   jax.ShapeDtypeStruct((d, ns * n), BF16)) + small_out
        + tuple(jax.ShapeDtypeStruct((N_DEV,) + a.shape, a.dtype) for a in shards),
        compiler_params=_params(("arbitrary", "arbitrary")),
    )(me_arr, x, prm, w_in_loc, *ada, *shards)
    return res[0], res[1], res[2], res[3], res[4], res[5], res[6:]


def _forward_mix(x, proj, mod, prm, taps, w_so, w_co, w_o, tm, shards):
    t, d = x.shape
    nt = t // tm
    na = len(shards)
    hs, hc = HALO_SHORT, HALO_CONF
    r3, r31 = 0, 8

    def body(*refs):
        x_ref, proj_ref, mod_ref, prm_ref, taps_ref, wso_ref, wco_ref, wo_ref = refs[:8]
        x1_ref, ya_ref, yb_ref, mix_ref, u1_ref = refs[8 + na:13 + na]
        cv_ext, u0_ext, conv3_ref, u1f_ref = refs[13 + 2 * na:17 + 2 * na]
        gather = _Gather(refs[8:8 + na], refs[13 + na:13 + 2 * na], *refs[17 + 2 * na:])
        i = pl.program_id(0)

        @pl.when(i == 0)
        def _():
            gather.start()
            cv_ext[0:hs, :] = jnp.zeros((hs, d), F32)
            u0_ext[0:hc, :] = jnp.zeros((hc, d), F32)

        @pl.when(i == (3 * nt) // 4)
        def _():
            gather.forward()

        def col(g):
            return proj_ref[:, g * d:(g + 1) * d].astype(F32)

        cv_ext[hs:hs + tm, :] = col(1) * col(2)
        u0_ext[hc:hc + tm, :] = col(3) * _sigmoid(col(4))
        _tap_conv(cv_ext, taps_ref, [(r3 + wi, off) for wi, off in _causal_taps(SHORT_K, hs)], tm, hs, conv3_ref)
        _tap_conv(u0_ext, taps_ref, [(r31 + wi, off) for wi, off in _causal_taps(CONF_K, hc)], tm, hc, u1f_ref)
        cv_ext[0:hs, :] = cv_ext[tm:tm + hs, :]
        u0_ext[0:hc, :] = u0_ext[tm:tm + hc, :]

        ya_pre = (col(0) * conv3_ref[...]).astype(BF16)
        y_a = _dot(ya_pre, wso_ref[...])
        u1 = u1f_ref[...] + prm_ref[P_CBIAS:P_CBIAS + 1, :]
        u1_ref[...] = u1.astype(BF16)
        u1 = u1_ref[...].astype(F32)
        mu = _rowmean(u1)
        uc = u1 - mu
        rstd = lax.rsqrt(_rowmean(uc * uc) + LN_EPS)
        u2 = uc * rstd * prm_ref[P_LNG:P_LNG + 1, :] + prm_ref[P_LNB:P_LNB + 1, :]
        u3 = (u2 * _sigmoid(u2)).astype(BF16)
        y_b = _dot(u3, wco_ref[...])
        ya_ref[...] = y_a.astype(BF16)
        yb_ref[...] = y_b.astype(BF16)
        merged = _sigmoid(col(5)) * ya_ref[...].astype(F32) + _sigmoid(col(6)) * yb_ref[...].astype(F32)
        mix = _dot(merged.astype(BF16), wo_ref[...])
        mix_ref[...] = mix.astype(BF16)
        x1_ref[...] = x_ref[...] + mod_ref[M_G1:M_G1 + 1, :] * mix

        @pl.when(i == nt - 1)
        def _():
            gather.finish()

    tile = pl.BlockSpec((tm, d), lambda i: (i, 0))
    whole = lambda shape: pl.BlockSpec(shape, lambda i: (0,) * len(shape))
    res = pl.pallas_call(
        body, name="forward_mix", grid=(nt,),
        out_shape=(jax.ShapeDtypeStruct((t, d), F32),) + (jax.ShapeDtypeStruct((t, d), BF16),) * 4
        + tuple(jax.ShapeDtypeStruct((N_DEV,) + a.shape, a.dtype) for a in shards),
        in_specs=[tile, pl.BlockSpec((tm, 7 * d), lambda i: (i, 0)), whole((8, d)), whole((8, d)),
                  whole(taps.shape), whole((d, d)), whole((d, d)), whole((d, d))] + [ANY_SPEC] * na,
        out_specs=[tile] * 5 + [ANY_SPEC] * na,
        scratch_shapes=[pltpu.VMEM((hs + tm, d), F32), pltpu.VMEM((hc + tm, d), F32),
                        pltpu.VMEM((tm, d), F32), pltpu.VMEM((tm, d), F32)] + _comm_scratch(na),
        compiler_params=_params(("arbitrary",)),
    )(x, proj, mod, prm, taps, w_so, w_co, w_o, *shards)
    return res[:5], res[5:]


def _ffn_chunks(ff):
    mxu = 2 * LANES
    cut = (ff // mxu + 1) // 2 * mxu
    return [(0, cut), (cut, ff)] if 0 < cut < ff and ff % mxu == 0 else [(0, ff)]


def _forward_ffn(x1, tgt, mod, prm, w_fi_t, w_fo, tm):
    t, d = x1.shape
    ff = w_fo.shape[0]

    def body(x1_ref, tgt_ref, mod_ref, prm_ref, wfi_hbm, wfo_hbm,
             dx2_ref, ab_ref, h2_ref, sums_ref, wfi_ref, wfo_ref):
        i = pl.program_id(0)

        @pl.when(i == 0)
        def _():
            pltpu.sync_copy(wfi_hbm, wfi_ref)
            pltpu.sync_copy(wfo_hbm, wfo_ref)
            sums_ref[...] = jnp.zeros((8, d), F32)

        x1v = x1_ref[...]
        r2 = lax.rsqrt(_rowmean(x1v * x1v) + EPS)
        h2 = (x1v * r2 * prm_ref[P_GFFN:P_GFFN + 1, :] * (1.0 + mod_ref[M_SC2:M_SC2 + 1, :])
              + mod_ref[M_SH2:M_SH2 + 1, :]).astype(BF16)
        h2_ref[...] = h2
        f = jnp.zeros((tm, d), F32)
        for c0, c1 in _ffn_chunks(ff):
            ab_ref[:, c0:c1] = _dot_nt(h2, wfi_ref[c0:c1, :]).astype(BF16)
            ab_ref[:, ff + c0:ff + c1] = _dot_nt(h2, wfi_ref[ff + c0:ff + c1, :]).astype(BF16)
            a = ab_ref[:, c0:c1].astype(F32)
            act = (a * _sigmoid(a) * ab_ref[:, ff + c0:ff + c1].astype(F32)).astype(BF16)
            f = f + _dot(act, wfo_ref[c0:c1, :])
        x2 = x1v + mod_ref[M_G2:M_G2 + 1, :] * f
        r3 = lax.rsqrt(_rowmean(x2 * x2) + EPS)
        xn3 = x2 * r3
        gfin = prm_ref[P_GFIN:P_GFIN + 1, :]
        err = xn3 * gfin - tgt_ref[...]
        dy = err * (1.0 / d)
        dxn3 = dy * gfin
        dx2 = r3 * (dxn3 - xn3 * _rowmean(dxn3 * xn3))
        dx2_ref[...] = dx2
        sums_ref[0:1, :] = sums_ref[0:1, :] + _colsum(dy * xn3)
        sums_ref[1:2, :] = sums_ref[1:2, :] + _colsum(dx2 * f)
        sums_ref[2:3, :] = sums_ref[2:3, :] + _colsum(err * err) * (0.5 / d)

    tile = pl.BlockSpec((tm, d), lambda i: (i, 0))
    whole = lambda shape: pl.BlockSpec(shape, lambda i: (0,) * len(shape))
    return pl.pallas_call(
        body, name="forward_ffn", grid=(t // tm,),
        out_shape=(jax.ShapeDtypeStruct((t, d), F32), jax.ShapeDtypeStruct((t, 2 * ff), BF16),
                   jax.ShapeDtypeStruct((t, d), BF16), jax.ShapeDtypeStruct((8, d), F32)),
        in_specs=[tile, tile, whole((8, d)), whole((8, d)), ANY_SPEC, ANY_SPEC],
        out_specs=[tile, pl.BlockSpec((tm, 2 * ff), lambda i: (i, 0)), tile, whole((8, d))],
        scratch_shapes=[pltpu.VMEM(w_fi_t.shape, BF16), pltpu.VMEM(w_fo.shape, BF16)],
        compiler_params=_params(("arbitrary",)),
    )(x1, tgt, mod, prm, w_fi_t, w_fo)


def _backward_ffn(dx2, x1, ab, mod, prm, w_fi_t, w_fo, tm):
    t, d = x1.shape
    ff = w_fo.shape[0]

    def body(dx2_ref, x1_ref, ab_ref, mod_ref, prm_ref, wfi_hbm, wfo_hbm,
             dx1_ref, df_ref, act_ref, dab_ref, sums_ref, wfi_ref, wfo_ref):
        i = pl.program_id(0)

        @pl.when(i == 0)
        def _():
            pltpu.sync_copy(wfi_hbm, wfi_ref)
            pltpu.sync_copy(wfo_hbm, wfo_ref)
            sums_ref[...] = jnp.zeros((8, d), F32)

        dx2v = dx2_ref[...]
        df = (mod_ref[M_G2:M_G2 + 1, :] * dx2v).astype(BF16)
        df_ref[...] = df
        dh2 = jnp.zeros((tm, d), F32)
        for c0, c1 in _ffn_chunks(ff):
            dact = _dot_nt(df, wfo_ref[c0:c1, :])
            a = ab_ref[:, c0:c1].astype(F32)
            b = ab_ref[:, ff + c0:ff + c1].astype(F32)
            s = _sigmoid(a)
            sil = a * s
            act_ref[:, c0:c1] = (sil * b).astype(BF16)
            da = (dact * b * (s * (1.0 + a * (1.0 - s)))).astype(BF16)
            db = (dact * sil).astype(BF16)
            dab_ref[:, c0:c1] = da
            dab_ref[:, ff + c0:ff + c1] = db
            dh2 = dh2 + _dot(da, wfi_ref[c0:c1, :]) + _dot(db, wfi_ref[ff + c0:ff + c1, :])
        x1v = x1_ref[...]
        r2 = lax.rsqrt(_rowmean(x1v * x1v) + EPS)
        xn2 = x1v * r2
        gffn = prm_ref[P_GFFN:P_GFFN + 1, :]
        scale = 1.0 + mod_ref[M_SC2:M_SC2 + 1, :]
        dxn2 = dh2 * gffn * scale
        dx1_ref[...] = dx2v + r2 * (dxn2 - xn2 * _rowmean(dxn2 * xn2))
        hx = dh2 * xn2
        sums_ref[0:1, :] = sums_ref[0:1, :] + _colsum(dh2)
        sums_ref[1:2, :] = sums_ref[1:2, :] + _colsum(hx) * gffn
        sums_ref[2:3, :] = sums_ref[2:3, :] + _colsum(hx) * scale

    tile = pl.BlockSpec((tm, d), lambda i: (i, 0))
    whole = lambda shape: pl.BlockSpec(shape, lambda i: (0,) * len(shape))
    wide = lambda n: pl.BlockSpec((tm, n), lambda i: (i, 0))
    return pl.pallas_call(
        body, name="backward_ffn", grid=(t // tm,),
        out_shape=(jax.ShapeDtypeStruct((t, d), F32), jax.ShapeDtypeStruct((t, d), BF16),
                   jax.ShapeDtypeStruct((t, ff), BF16), jax.ShapeDtypeStruct((t, 2 * ff), BF16),
                   jax.ShapeDtypeStruct((8, d), F32)),
        in_specs=[tile, tile, wide(2 * ff), whole((8, d)), whole((8, d)), ANY_SPEC, ANY_SPEC],
        out_specs=[tile, tile, wide(ff), wide(2 * ff), whole((8, d))],
        scratch_shapes=[pltpu.VMEM(w_fi_t.shape, BF16), pltpu.VMEM(w_fo.shape, BF16)],
        compiler_params=_params(("arbitrary",)),
    )(dx2, x1, ab, mod, prm, w_fi_t, w_fo)


def _backward_mix(dx1, proj, y_a, y_b, mix, u1, mod, prm, taps, w_so, w_co, w_o, tm, partials):
    t, d = dx1.shape
    nt = t // tm
    na = len(partials)
    hs, hc = HALO_SHORT, HALO_CONF
    r3, r31 = 0, 8

    def body(*refs):
        (dx1_ref, proj_ref, halo_ref, ya_ref, yb_ref, mix_ref, u1_ref, mod_ref, prm_ref, taps_ref,
         wso_ref, wco_ref, wo_ref) = refs[:13]
        (dproj_ref, dmix_ref, dya_ref, dyb_ref, merged_ref, yapre_ref, u3_ref, sums_ref, dw3_ref,
         dw31_ref) = refs[13 + na:23 + na]
        cv_ext, u0_ext, d3_ext, du1_ext, tmp_ref = refs[23 + 2 * na:28 + 2 * na]
        scatter_start, scatter_finish = _scatter_plan(refs[13:13 + na], refs[23 + na:23 + 2 * na], *refs[28 + 2 * na:])
        i = pl.program_id(0)
        first_tile = i == nt - 1

        @pl.when(i == 0)
        def _():
            scatter_start()
            sums_ref[...] = jnp.zeros((8, d), F32)
            dw3_ref[...] = jnp.zeros(dw3_ref.shape, F32)
            dw31_ref[...] = jnp.zeros(dw31_ref.shape, F32)
            d3_ext[tm:tm + hs, :] = jnp.zeros((hs, d), F32)
            du1_ext[tm:tm + hc, :] = jnp.zeros((hc, d), F32)

        def col(g):
            return proj_ref[:, g * d:(g + 1) * d].astype(F32)

        def hcol(g, rows):
            v = halo_ref[HALO_CONF - rows:HALO_CONF, g * d:(g + 1) * d].astype(F32)
            return jnp.where(first_tile, 0.0, v)

        dx1v = dx1_ref[...]
        mixv = mix_ref[...].astype(F32)
        dmix = (mod_ref[M_G1:M_G1 + 1, :] * dx1v).astype(BF16)
        dmix_ref[...] = dmix
        sums_ref[0:1, :] = sums_ref[0:1, :] + _colsum(dx1v * mixv)
        dmerged = _dot_nt(dmix, wo_ref[...])
        ga = _sigmoid(col(5))
        gb = _sigmoid(col(6))
        yav = ya_ref[...].astype(F32)
        ybv = yb_ref[...].astype(F32)
        dya_f = dmerged * ga
        dyb_f = dmerged * gb
        dya = dya_f.astype(BF16)
        dyb = dyb_f.astype(BF16)
        dya_ref[...] = dya
        dyb_ref[...] = dyb
        dproj_ref[:, 5 * d:6 * d] = (dya_f * yav * (1.0 - ga)).astype(BF16)
        dproj_ref[:, 6 * d:7 * d] = (dyb_f * ybv * (1.0 - gb)).astype(BF16)
        merged_ref[...] = (ga * yav + gb * ybv).astype(BF16)

        dya_pre = _dot_nt(dya, wso_ref[...])
        c_s, v_s, b_s = col(1), col(2), col(0)
        cv_ext[0:hs, :] = hcol(1, hs) * hcol(2, hs)
        cv_ext[hs:hs + tm, :] = c_s * v_s
        _tap_conv(cv_ext, taps_ref, [(r3 + wi, off) for wi, off in _causal_taps(SHORT_K, hs)], tm, hs, tmp_ref)
        conv3 = tmp_ref[...]
        yapre_ref[...] = (b_s * conv3).astype(BF16)
        dproj_ref[:, 0:d] = (dya_pre * conv3).astype(BF16)
        d3_ext[0:tm, :] = dya_pre * b_s
        _tap_wgrad(d3_ext, cv_ext, _causal_taps(SHORT_K, hs), tm, hs, dw3_ref)
        _tap_conv(d3_ext, taps_ref, [(r3 + wi, off) for wi, off in _anticausal_taps(SHORT_K)], tm, hs, tmp_ref)
        dcv = tmp_ref[...]
        dproj_ref[:, d:2 * d] = (dcv * v_s).astype(BF16)
        dproj_ref[:, 2 * d:3 * d] = (dcv * c_s).astype(BF16)
        d3_ext[tm:tm + hs, :] = d3_ext[0:hs, :]

        du3 = _dot_nt(dyb, wco_ref[...])
        u1v = u1_ref[...].astype(F32)
        mu = _rowmean(u1v)
        uc = u1v - mu
        rstd = lax.rsqrt(_rowmean(uc * uc) + LN_EPS)
        uhat = uc * rstd
        lng = prm_ref[P_LNG:P_LNG + 1, :]
        u2 = uhat * lng + prm_ref[P_LNB:P_LNB + 1, :]
        s2 = _sigmoid(u2)
        u3_ref[...] = (u2 * s2).astype(BF16)
        du2 = du3 * (s2 * (1.0 + u2 * (1.0 - s2)))
        sums_ref[1:2, :] = sums_ref[1:2, :] + _colsum(du2 * uhat)
        sums_ref[2:3, :] = sums_ref[2:3, :] + _colsum(du2)
        duhat = du2 * lng
        du1 = rstd * (duhat - _rowmean(duhat) - uhat * _rowmean(duhat * uhat))
        sums_ref[3:4, :] = sums_ref[3:4, :] + _colsum(du1)
        du1_ext[0:tm, :] = du1
        v_c = col(3)
        sg = _sigmoid(col(4))
        u0_ext[0:hc, :] = hcol(3, hc) * _sigmoid(hcol(4, hc))
        u0_ext[hc:hc + tm, :] = v_c * sg
        _tap_wgrad(du1_ext, u0_ext, _causal_taps(CONF_K, hc), tm, hc, dw31_ref)
        _tap_conv(du1_ext, taps_ref, [(r31 + wi, off) for wi, off in _anticausal_taps(CONF_K)], tm, hc, tmp_ref)
        dv_c = tmp_ref[...] * sg
        dproj_ref[:, 3 * d:4 * d] = dv_c.astype(BF16)
        dproj_ref[:, 4 * d:5 * d] = (dv_c * v_c * (1.0 - sg)).astype(BF16)
        du1_ext[tm:tm + hc, :] = du1_ext[0:hc, :]

        @pl.when(i == nt - 1)
        def _():
            scatter_finish()

    rev = lambda i: (nt - 1 - i, 0)
    tile = pl.BlockSpec((tm, d), rev)
    whole = lambda shape: pl.BlockSpec(shape, lambda i: (0,) * len(shape))
    hblocks = tm // HALO_CONF
    halo = pl.BlockSpec((HALO_CONF, 7 * d), lambda i: (jnp.maximum((nt - 1 - i) * hblocks - 1, 0), 0))
    bf = jax.ShapeDtypeStruct((t, d), BF16)
    res = pl.pallas_call(
        body, name="backward_mix", grid=(nt,),
        out_shape=(jax.ShapeDtypeStruct((t, 7 * d), BF16), bf, bf, bf, bf, bf, bf,
                   jax.ShapeDtypeStruct((8, d), F32),
                   jax.ShapeDtypeStruct((SUBLANES * SHORT_K, d), F32),
                   jax.ShapeDtypeStruct((SUBLANES * CONF_K, d), F32))
        + tuple(jax.ShapeDtypeStruct(p.shape, p.dtype) for p in partials),
        in_specs=[tile, pl.BlockSpec((tm, 7 * d), rev), halo, tile, tile, tile, tile,
                  whole((8, d)), whole((8, d)), whole(taps.shape), whole((d, d)), whole((d, d)), whole((d, d))]
        + [ANY_SPEC] * na,
        out_specs=[pl.BlockSpec((tm, 7 * d), rev), tile, tile, tile, tile, tile, tile,
                   whole((8, d)), whole((SUBLANES * SHORT_K, d)), whole((SUBLANES * CONF_K, d))] + [ANY_SPEC] * na,
        scratch_shapes=[pltpu.VMEM((hs + tm, d), F32), pltpu.VMEM((hc + tm, d), F32),
                        pltpu.VMEM((tm + hs, d), F32), pltpu.VMEM((tm + hc, d), F32),
                        pltpu.VMEM((tm, d), F32)] + _comm_scratch(na),
        compiler_params=_params(("arbitrary",)),
    )(dx1, proj, proj, y_a, y_b, mix, u1, mod, prm, taps, w_so, w_co, w_o, *partials)
    return res[:10], res[10:]


def _backward_in(dproj, x, dx1, mod, prm, w_in_g, tm, partials, partial_view, exchanged_shapes):
    t, d = x.shape
    n_all = w_in_g.shape[1]
    na = len(partials)
    nt = t // tm

    def body(*refs):
        dproj_ref, x_ref, dx1_ref, mod_ref, prm_ref, w_hbm = refs[:6]
        gx_ref, sums_ref = refs[6 + na:8 + na]
        w_ref = refs[8 + 2 * na]
        scatter_start, scatter_finish = _scatter_plan(refs[6:6 + na], refs[8 + na:8 + 2 * na], *refs[9 + 2 * na:],
                                                      view=partial_view)

        @pl.when(pl.program_id(0) == 0)
        def _():
            scatter_start()
            pltpu.sync_copy(w_hbm, w_ref)
            sums_ref[...] = jnp.zeros((8, d), F32)

        dh = _dot_nt(dproj_ref[...], w_ref[...])
        xv = x_ref[...]
        r1 = lax.rsqrt(_rowmean(xv * xv) + EPS)
        xn = xv * r1
        gmix = prm_ref[P_GMIX:P_GMIX + 1, :]
        scale = 1.0 + mod_ref[M_SC1:M_SC1 + 1, :]
        dxn = dh * gmix * scale
        gx_ref[...] = dx1_ref[...] + r1 * (dxn - xn * _rowmean(dxn * xn))
        hx = dh * xn
        sums_ref[0:1, :] = sums_ref[0:1, :] + _colsum(dh)
        sums_ref[1:2, :] = sums_ref[1:2, :] + _colsum(hx) * gmix
        sums_ref[2:3, :] = sums_ref[2:3, :] + _colsum(hx) * scale

        @pl.when(pl.program_id(0) == nt - 1)
        def _():
            scatter_finish()

    tile = pl.BlockSpec((tm, d), lambda i: (i, 0))
    whole = pl.BlockSpec((8, d), lambda i: (0, 0))
    res = pl.pallas_call(
        body, name="backward_in", grid=(nt,),
        out_shape=(jax.ShapeDtypeStruct((t, d), F32), jax.ShapeDtypeStruct((8, d), F32))
        + tuple(jax.ShapeDtypeStruct(s, p.dtype) for p, s in zip(partials, exchanged_shapes)),
        in_specs=[pl.BlockSpec((tm, n_all), lambda i: (i, 0)), tile, tile, whole, whole, ANY_SPEC] + [ANY_SPEC] * na,
        out_specs=[tile, whole] + [ANY_SPEC] * na,
        scratch_shapes=[pltpu.VMEM(w_in_g.shape, BF16)] + _comm_scratch(na),
        compiler_params=_params(("arbitrary",)),
    )(dproj, x, dx1, mod, prm, w_in_g, *partials)
    return res[0], res[1], res[2:]


def _weight_grad(a, b, a_spec, b_spec, ns, m, n, nk, name, partials=(), by_columns=False):
    na = len(partials)

    def body(*refs):
        a_ref, b_ref = refs[:2]
        o_ref = refs[2 + na]
        acc_ref = refs[3 + 2 * na]
        s, k = pl.program_id(0), pl.program_id(1)
        if na:
            scatter_start, scatter_finish = _scatter_plan(refs[2:2 + na], refs[3 + na:3 + 2 * na], *refs[4 + 2 * na:])

            @pl.when((s == 0) & (k == 0))
            def _():
                scatter_start()

        av = a_ref[0] if len(a_ref.shape) == 3 else a_ref[...]
        bv = b_ref[0] if len(b_ref.shape) == 3 else b_ref[...]
        part = _dot_tn(av, bv)

        @pl.when(k == 0)
        def _():
            acc_ref[...] = part

        @pl.when(k > 0)
        def _():
            acc_ref[...] = acc_ref[...] + part

        @pl.when(k == nk - 1)
        def _():
            if by_columns:
                o_ref[...] = acc_ref[...].astype(BF16)
            else:
                o_ref[0] = acc_ref[...].astype(BF16)

        if na:
            @pl.when((s == ns - 1) & (k == nk - 1))
            def _():
                scatter_finish()

    if by_columns:
        out_shape, out_spec = (m, ns * n), pl.BlockSpec((m, n), lambda s, k: (0, s))
    else:
        out_shape, out_spec = (ns, m, n), pl.BlockSpec((1, m, n), lambda s, k: (s, 0, 0))
    res = pl.pallas_call(
        body, name=name, grid=(ns, nk),
        out_shape=(jax.ShapeDtypeStruct(out_shape, BF16),)
        + tuple(jax.ShapeDtypeStruct(p.shape, p.dtype) for p in partials),
        in_specs=[a_spec, b_spec] + [ANY_SPEC] * na,
        out_specs=[out_spec] + [ANY_SPEC] * na,
        scratch_shapes=[pltpu.VMEM((m, n), F32)] + (_comm_scratch(na) if na else []),
        compiler_params=_params(("arbitrary", "arbitrary")),
    )(a, b, *partials)
    return (res[0], res[1:]) if na else res[0]


def _adamw(w, g, m, v):
    m = ADAM_B1 * m + (1.0 - ADAM_B1) * g
    v = ADAM_B2 * v + (1.0 - ADAM_B2) * (g * g)
    m_hat = m / (1.0 - ADAM_B1 ** ADAM_STEP)
    v_hat = v / (1.0 - ADAM_B2 ** ADAM_STEP)
    delta = -ADAM_LR * (m_hat / (jnp.sqrt(v_hat) + ADAM_EPS) + ADAM_WD * w)
    return delta, m, v


def _adamw_shard(parts, w, m, v, tr, name):
    r, c = w.shape

    def body(p_ref, w_ref, m_ref, v_ref, g_ref, d_ref, nm_ref, nv_ref):
        g = p_ref[0].astype(F32)
        for s in range(1, N_DEV):
            g = g + p_ref[s].astype(F32)
        delta, nm, nv = _adamw(w_ref[...], g, m_ref[...], v_ref[...])
        g_ref[...] = g
        d_ref[...] = delta
        nm_ref[...] = nm
        nv_ref[...] = nv

    tile = pl.BlockSpec((tr, c), lambda i: (i, 0))
    return pl.pallas_call(
        body, name=name, grid=(r // tr,),
        out_shape=(jax.ShapeDtypeStruct((r, c), F32),) * 4,
        in_specs=[pl.BlockSpec((N_DEV, tr, c), lambda i: (0, i, 0)), tile, tile, tile],
        out_specs=[tile] * 4,
        compiler_params=_params(("arbitrary",)),
    )(parts, w, m, v)


def _ada_update(sc_all, dmod_cols, w, m, v, tr):
    d, n = w.shape

    def body(sc_ref, dm_ref, w_ref, m_ref, v_ref, g_ref, d_ref, nm_ref, nv_ref):
        g = lax.dot_general(sc_ref[...], dm_ref[...], (((0,), (0,)), ((), ())),
                            preferred_element_type=F32, precision=lax.Precision.HIGHEST)
        delta, nm, nv = _adamw(w_ref[...], g, m_ref[...], v_ref[...])
        g_ref[...] = g
        d_ref[...] = delta
        nm_ref[...] = nm
        nv_ref[...] = nv

    tile = pl.BlockSpec((tr, n), lambda i: (i, 0))
    return pl.pallas_call(
        body, name="ada_update", grid=(d // tr,),
        out_shape=(jax.ShapeDtypeStruct((d, n), F32),) * 4,
        in_specs=[pl.BlockSpec((N_DEV, tr), lambda i: (0, i)), pl.BlockSpec((N_DEV, n), lambda i: (0, 0)),
                  tile, tile, tile],
        out_specs=[tile] * 4,
        compiler_params=_params(("arbitrary",)),
    )(sc_all, dmod_cols, w, m, v)


def _small_exchange(vec, cg):
    l = vec.shape[2]
    rows = cg.shape[1]

    def body(vec_ref, cg_ref, vall_ref, cgr_ref, send_sems, recv_sems):
        x, y, c = _my_coords()
        me = _slot(x, y, c)
        vall_ref[me] = vec_ref[0]
        cgr_ref[me] = cg_ref[me]
        copies = []
        for k in range(1, N_DEV):
            peer = (_flip(x, k & 4), _flip(y, k & 2), _flip(c, k & 1))
            copies.append(pltpu.make_async_remote_copy(
                src_ref=vall_ref.at[me], dst_ref=vall_ref.at[me], send_sem=send_sems.at[k - 1],
                recv_sem=recv_sems.at[k - 1], device_id=peer, device_id_type=MESH))
            copies.append(pltpu.make_async_remote_copy(
                src_ref=cg_ref.at[_slot(*peer)], dst_ref=cgr_ref.at[me], send_sem=send_sems.at[7 + k - 1],
                recv_sem=recv_sems.at[7 + k - 1], device_id=peer, device_id_type=MESH))
        for cp in copies:
            cp.start()
        for cp in copies:
            cp.wait_recv()
        for cp in copies:
            cp.wait_send()

    vm = pl.BlockSpec(memory_space=pltpu.VMEM)
    return pl.pallas_call(
        body, name="small_exchange",
        out_shape=(jax.ShapeDtypeStruct((N_DEV, 1, l), F32), jax.ShapeDtypeStruct((N_DEV, rows, LANES), F32)),
        in_specs=[vm, vm], out_specs=[vm, vm],
        scratch_shapes=[pltpu.SemaphoreType.DMA((14,)), pltpu.SemaphoreType.DMA((14,))],
        compiler_params=_params(),
    )(vec, cg)


def _small_update(vall, cgr, smalls):
    ns = len(smalls)

    def body(*refs):
        vall_ref, cgr_ref = refs[0], refs[1]
        wmv = refs[2:2 + 3 * ns]
        outs = refs[2 + 3 * ns:]
        for p, (_, _, _, lo, hi, kind) in enumerate(smalls):
            w_ref, m_ref, v_ref = wmv[3 * p:3 * p + 3]
            part = (lambda s: vall_ref[s, :, lo:hi]) if kind == "vec" else (lambda s: cgr_ref[s, lo:hi, :])
            g = part(0)
            for s in range(1, N_DEV):
                g = g + part(s)
            delta, nm, nv = _adamw(w_ref[...], g, m_ref[...], v_ref[...])
            for o_ref, val in zip(outs[4 * p:4 * p + 4], (g, delta, nm, nv)):
                o_ref[...] = val

    vm = pl.BlockSpec(memory_space=pltpu.VMEM)
    args = [vall, cgr]
    out_shape = []
    for w, m, v, _, _, _ in smalls:
        args += [w, m, v]
        out_shape += [jax.ShapeDtypeStruct(w.shape, F32)] * 4
    res = pl.pallas_call(
        body, name="small_update",
        out_shape=tuple(out_shape),
        in_specs=[vm] * len(args), out_specs=[vm] * len(out_shape),
        compiler_params=_params(),
    )(*args)
    return [res[4 * p:4 + 4 * p] for p in range(ns)]


def _pick(t, want):
    return want if t % want == 0 else t


def kernel(x, c, w_ada, b_ada, norm_mix_g, w_in, conv_short_w, w_short_out, conv_conf_w, conv_conf_b, conf_ln_g, conf_ln_b, w_conf_out, w_o, norm_ffn_g, w_ffn_in, w_ffn_out, final_norm_g, loss_target, m_w_ada, m_b_ada, m_norm_mix_g, m_w_in, m_conv_short_w, m_w_short_out, m_conv_conf_w, m_conv_conf_b, m_conf_ln_g, m_conf_ln_b, m_w_conf_out, m_w_o, m_norm_ffn_g, m_w_ffn_in, m_w_ffn_out, m_final_norm_g, v_w_ada, v_b_ada, v_norm_mix_g, v_w_in, v_conv_short_w, v_w_short_out, v_conv_conf_w, v_conv_conf_b, v_conf_ln_g, v_conf_ln_b, v_w_conf_out, v_w_o, v_norm_ffn_g, v_w_ffn_in, v_w_ffn_out, v_final_norm_g):
    t, d = x.shape[1], x.shape[2]
    x2 = x.reshape(t, d)
    tgt = loss_target.reshape(t, d)
    me = _slot(*_my_coords())
    tm = _pick(t, 256)
    tm_fwd_in = _pick(t, 1024)
    tk = _pick(t, 2048)

    taps_loc = jnp.zeros((40, LANES), F32)
    taps_loc = taps_loc.at[0:SHORT_K].set(conv_short_w[0]).at[8:8 + CONF_K].set(conv_conf_w[0])
    prm = jnp.concatenate([norm_mix_g, norm_ffn_g, final_norm_g.reshape(1, d), conv_conf_b, conf_ln_g, conf_ln_b,
                           jnp.zeros((2, d), F32)], axis=0)

    n_in = w_in.shape[2]
    nk = t // tk
    tok = pl.BlockSpec((tk, d), lambda s, k: (k, 0))
    rows = d // N_DEV
    frows = w_ffn_out.shape[1]

    proj, h, w_in_g, mod, sc_all3, taps, (w_so_g, w_co_g, w_o_g) = _forward_in(
        jnp.reshape(me, (1,)).astype(jnp.int32), x2, prm, w_in[0].astype(BF16),
        (c, w_ada[0], b_ada, taps_loc), tm_fwd_in,
        [w_short_out[0].astype(BF16), w_conf_out[0].astype(BF16), w_o[0].astype(BF16)])
    w_so = w_so_g.reshape(d, d)
    w_co = w_co_g.reshape(d, d)
    w_oo = w_o_g.reshape(d, d)
    w_fi_t, m_fi_t, v_fi_t = w_ffn_in[0].T, m_w_ffn_in[0].T, v_w_ffn_in[0].T
    (x1, y_a, y_b, mix, u1), (w_fi_g, w_fo_g) = _forward_mix(
        x2, proj, mod, prm, taps, w_so, w_co, w_oo, tm, [w_fi_t.astype(BF16), w_ffn_out[0].astype(BF16)])
    fb = w_fi_g.shape[1]
    ff = N_DEV * frows
    w_fi_all = w_fi_g.reshape(N_DEV * fb, d)
    w_fo_all = w_fo_g.reshape(ff, d)
    dx2, ab, h2, sums_f = _forward_ffn(x1, tgt, mod, prm, w_fi_all, w_fo_all, tm)

    dx1, df, act, dab, sums_b = _backward_ffn(dx2, x1, ab, mod, prm, w_fi_all, w_fo_all, tm)
    fb2 = 2 * fb
    pair_tok = pl.BlockSpec((tk, fb2), lambda s, k: (k, s))
    g_fi = _weight_grad(dab, h2, pair_tok, tok, N_DEV // 2, fb2, d, nk, "grad_w_ffn_in")
    g_fo = _weight_grad(act, df, pair_tok, tok, N_DEV // 4, fb2, d, nk, "grad_w_ffn_out")
    (dproj, dmix, dya, dyb, merged, ya_pre, u3, sums_m, dw3p, dw31p), (p_fi, p_fo) = _backward_mix(
        dx1, proj, y_a, y_b, mix, u1, mod, prm, taps, w_so, w_co, w_oo, tm,
        [g_fi.reshape(N_DEV, fb, d), g_fo.reshape(N_DEV, frows, d)])
    tk_in = _pick(t, 4096)
    nk_in = t // tk_in
    tok_in = pl.BlockSpec((tk_in, d), lambda s, k: (k, 0))
    g_so = _weight_grad(ya_pre, dya, tok_in, tok_in, 1, d, d, nk_in, "grad_w_short_out")
    g_co = _weight_grad(u3, dyb, tok_in, tok_in, 1, d, d, nk_in, "grad_w_conf_out")
    g_oo = _weight_grad(merged, dmix, tok_in, tok_in, 1, d, d, nk_in, "grad_w_o")
    n_blk = _pick(N_DEV * n_in, d)
    g_in, (p_so, p_co, p_oo) = _weight_grad(
        h, dproj, tok_in, pl.BlockSpec((tk_in, n_blk), lambda s, k: (k, s)),
        N_DEV * n_in // n_blk, d, n_blk, nk_in, "grad_w_in",
        [g_so.reshape(N_DEV, rows, d), g_co.reshape(N_DEV, rows, d), g_oo.reshape(N_DEV, rows, d)], by_columns=True)
    grad_x, sums_i, (p_in,) = _backward_in(dproj, x2, dx1, mod, prm, w_in_g, tm, [g_in], _column_block(n_in),
                                           [(N_DEV, d, n_in)])

    up_in = _adamw_shard(p_in, w_in[0], m_w_in[0], v_w_in[0], _pick(d, 256), "adamw_w_in")
    up_so = _adamw_shard(p_so, w_short_out[0], m_w_short_out[0], v_w_short_out[0], rows, "adamw_w_short_out")
    up_co = _adamw_shard(p_co, w_conf_out[0], m_w_conf_out[0], v_w_conf_out[0], rows, "adamw_w_conf_out")
    up_oo = _adamw_shard(p_oo, w_o[0], m_w_o[0], v_w_o[0], rows, "adamw_w_o")
    up_fi = tuple(a.T for a in _adamw_shard(p_fi, w_fi_t, m_fi_t, v_fi_t, fb // 2, "adamw_w_ffn_in"))
    up_fo = _adamw_shard(p_fo, w_ffn_out[0], m_w_ffn_out[0], v_w_ffn_out[0], frows, "adamw_w_ffn_out")

    vec = jnp.concatenate([sums_i[0:2], sums_m[0:1], sums_b[0:2], sums_f[1:2],
                           sums_i[2:3], sums_m[3:4], sums_m[1:3], sums_b[2:3], sums_f[0:1],
                           sums_f[2:3]], axis=0)
    vec = vec.reshape(1, 1, 13 * d)
    dw3 = dw3p.reshape(SHORT_K, SUBLANES, d).sum(axis=1)
    dw31 = dw31p.reshape(CONF_K, SUBLANES, d).sum(axis=1)
    cg = jnp.zeros((40, d), F32).at[0:SHORT_K].set(dw3).at[8:8 + CONF_K].set(dw31)
    cg = cg.reshape(40, N_DEV, LANES).transpose(1, 0, 2)
    fin = lambda a: a.reshape(1, d)
    tap = lambda a: a.reshape(a.shape[1:])
    smalls = [
        (b_ada, m_b_ada, v_b_ada, 0, 6 * d, "vec"),
        (norm_mix_g, m_norm_mix_g, v_norm_mix_g, 6 * d, 7 * d, "vec"),
        (tap(conv_short_w), tap(m_conv_short_w), tap(v_conv_short_w), 0, SHORT_K, "cg"),
        (tap(conv_conf_w), tap(m_conv_conf_w), tap(v_conv_conf_w), 8, 8 + CONF_K, "cg"),
        (conv_conf_b, m_conv_conf_b, v_conv_conf_b, 7 * d, 8 * d, "vec"),
        (conf_ln_g, m_conf_ln_g, v_conf_ln_g, 8 * d, 9 * d, "vec"),
        (conf_ln_b, m_conf_ln_b, v_conf_ln_b, 9 * d, 10 * d, "vec"),
        (norm_ffn_g, m_norm_ffn_g, v_norm_ffn_g, 10 * d, 11 * d, "vec"),
        (fin(final_norm_g), fin(m_final_norm_g), fin(v_final_norm_g), 11 * d, 12 * d, "vec"),
    ]
    vall, cgr = _small_exchange(vec, cg)
    up_small = _small_update(vall, cgr, smalls)
    n_ada = w_ada.shape[2]
    dmod_all = vall.reshape(N_DEV, 13 * d)[:, 0:6 * d]
    dmod_cols = lax.dynamic_slice(dmod_all, (0, me * n_ada), (N_DEV, n_ada))
    up_ada = _ada_update(sc_all3.reshape(N_DEV, d), dmod_cols, w_ada[0], m_w_ada[0], v_w_ada[0], _pick(d, 256))

    loss = jnp.sum(vall.reshape(N_DEV, 13 * d)[:, 12 * d:])

    lead = lambda a: a.reshape((1,) + a.shape)
    ups = [tuple(lead(a) for a in up_ada), up_small[0], up_small[1], tuple(lead(a) for a in up_in),
           tuple(lead(a) for a in up_small[2]), tuple(lead(a) for a in up_so), tuple(lead(a) for a in up_small[3]),
           up_small[4], up_small[5], up_small[6],
           tuple(lead(a) for a in up_co), tuple(lead(a) for a in up_oo), up_small[7],
           tuple(lead(a) for a in up_fi), tuple(lead(a) for a in up_fo),
           tuple(a.reshape(d) for a in up_small[8])]
    grads = [u[0] for u in ups]
    deltas = [u[1] for u in ups]
    new_m = [u[2] for u in ups]
    new_v = [u[3] for u in ups]
    return (loss, grad_x.reshape(1, t, d), *grads, *deltas, *new_m, *new_v)
```

```python
import functools

import jax
import jax.numpy as jnp
from jax import lax
from jax.experimental import pallas as pl
from jax.experimental.pallas import tpu as pltpu

F32 = jnp.float32
BF16 = jnp.bfloat16
MESH = pl.DeviceIdType.MESH

N_DEV = 8
EPS = 1e-6
LN_EPS = 1e-5
SHORT_K = 3
CONF_K = 31
ADAM_LR = 0.001
ADAM_B1 = 0.9
ADAM_B2 = 0.999
ADAM_EPS = 1e-08
ADAM_WD = 0.01
ADAM_STEP = 10

LANES = 128
SUBLANES = 8
CONV_ROWS = 64
HALO_SHORT = 8
HALO_CONF = 32
VMEM_LIMIT = 56 * 1024 * 1024

M_SH1, M_SC1, M_G1, M_SH2, M_SC2, M_G2 = range(6)
P_GMIX, P_GFFN, P_GFIN, P_CBIAS, P_LNG, P_LNB = range(6)


def _params(sem=None, **kw):
    return pltpu.CompilerParams(dimension_semantics=sem, vmem_limit_bytes=VMEM_LIMIT, **kw)


def _sigmoid(v):
    return jax.nn.sigmoid(v)


def _dot(a, b):
    return jnp.dot(a, b, preferred_element_type=F32)


def _dot_nt(a, b):
    return lax.dot_general(a, b, (((1,), (1,)), ((), ())), preferred_element_type=F32)


def _dot_tn(a, b):
    return lax.dot_general(a, b, (((0,), (0,)), ((), ())), preferred_element_type=F32)


def _colsum(v):
    return jnp.sum(v, axis=0, keepdims=True)


def _rowmean(v):
    return jnp.mean(v, axis=-1, keepdims=True)


def _my_coords():
    return lax.axis_index("x"), lax.axis_index("y"), lax.axis_index("c")


def _slot(px, py, pc):
    return 4 * px + 2 * py + pc


def _flip(v, bit):
    return 1 - v if bit else v


def _taps_by_residue(taps):
    by_res = {}
    for wi, off in taps:
        by_res.setdefault(off % SUBLANES, []).append((wi, off // SUBLANES))
    return sorted(by_res.items())


def _tap_conv(ext_ref, w_ref, taps, tm, extra, out_ref):
    d = out_ref.shape[1]
    rb = min(CONV_ROWS, tm)
    wrows = rb + extra
    groups = _taps_by_residue(taps)

    def block(i, carry):
        base = pl.multiple_of(i * rb, SUBLANES)
        for lc in range(d // LANES):
            ls = pl.ds(lc * LANES, LANES)
            win = ext_ref[pl.ds(base, wrows), ls]
            acc = None
            for r, lst in groups:
                sh = win if r == 0 else pltpu.roll(win, wrows - r, 0)
                for wi, q in lst:
                    term = w_ref[wi:wi + 1, ls] * sh[SUBLANES * q:SUBLANES * q + rb, :]
                    acc = term if acc is None else acc + term
            out_ref[pl.ds(base, rb), ls] = acc
        return carry

    lax.fori_loop(0, tm // rb, block, 0)


def _tap_wgrad(a_ref, ext_ref, taps, tm, extra, acc_ref):
    d = a_ref.shape[1]
    rb = min(CONV_ROWS, tm)
    wrows = rb + extra
    groups = _taps_by_residue(taps)

    def block(i, carry):
        base = pl.multiple_of(i * rb, SUBLANES)
        for lc in range(d // LANES):
            ls = pl.ds(lc * LANES, LANES)
            a_blk = a_ref[pl.ds(base, rb), ls]
            win = ext_ref[pl.ds(base, wrows), ls]
            for r, lst in groups:
                sh = win if r == 0 else pltpu.roll(win, wrows - r, 0)
                for wi, q in lst:
                    prod = a_blk * sh[SUBLANES * q:SUBLANES * q + rb, :]
                    part = prod[0:SUBLANES, :]
                    for s in range(1, rb // SUBLANES):
                        part = part + prod[SUBLANES * s:SUBLANES * (s + 1), :]
                    rows = pl.ds(SUBLANES * wi, SUBLANES)
                    acc_ref[rows, ls] = acc_ref[rows, ls] + part
        return carry

    lax.fori_loop(0, tm // rb, block, 0)


def _causal_taps(k, halo):
    return [(i, halo - (k - 1) + i) for i in range(k)]


def _anticausal_taps(k):
    return [(i, (k - 1) - i) for i in range(k)]


def _ada_scratch(d, nloc, trows):
    return [pltpu.VMEM((N_DEV, 1, d), F32), pltpu.VMEM((N_DEV, 1, nloc), F32), pltpu.VMEM((N_DEV, 1, nloc), F32),
            pltpu.VMEM((N_DEV, trows, LANES), F32), pltpu.SemaphoreType.DMA((21,)), pltpu.SemaphoreType.DMA((21,))]


def _ada_exchange(c_ref, w_ref, b_ref, t_ref, mod_ref, sc_ref, taps_ref,
                  scall_ref, part_ref, modrecv_ref, tapsall_ref, send_sems, recv_sems):
    x, y, cc = _my_coords()
    me = _slot(x, y, cc)
    d = c_ref.shape[1]
    cv = c_ref[...]
    scall_ref[me] = cv * _sigmoid(cv)
    tapsall_ref[me] = t_ref[...]

    def peer_of(k):
        return (_flip(x, k & 4), _flip(y, k & 2), _flip(cc, k & 1))

    def gather_copy(ref, base, k):
        return pltpu.make_async_remote_copy(
            src_ref=ref.at[me], dst_ref=ref.at[me], send_sem=send_sems.at[base + k - 1],
            recv_sem=recv_sems.at[base + k - 1], device_id=peer_of(k), device_id_type=MESH)

    first = [gather_copy(scall_ref, 0, k) for k in range(1, N_DEV)]
    first += [gather_copy(tapsall_ref, 7, k) for k in range(1, N_DEV)]
    for cp in first:
        cp.start()
    for cp in first[:7]:
        cp.wait_recv()
    sc_all = jnp.concatenate([scall_ref[s] for s in range(N_DEV)], axis=0)
    for s in range(N_DEV):
        sc_ref[s] = scall_ref[s]
    part = jnp.dot(sc_all, w_ref[...], preferred_element_type=F32,
                   precision=lax.Precision.HIGHEST)
    for b in range(N_DEV):
        part_ref[b] = part[b:b + 1, :]
    modrecv_ref[me] = part_ref[me]
    second = []
    for k in range(1, N_DEV):
        px, py, pc = peer_of(k)
        second.append(pltpu.make_async_remote_copy(
            src_ref=part_ref.at[_slot(px, py, pc)], dst_ref=modrecv_ref.at[me],
            send_sem=send_sems.at[14 + k - 1], recv_sem=recv_sems.at[14 + k - 1],
            device_id=(px, py, pc), device_id_type=MESH))
    for cp in second:
        cp.start()
    for cp in second:
        cp.wait_recv()
    mod = jnp.concatenate([modrecv_ref[s] for s in range(N_DEV)], axis=1) + b_ref[...]
    for r in range(6):
        mod_ref[r:r + 1, :] = mod[:, r * d:(r + 1) * d]
    mod_ref[6:8, :] = jnp.zeros((2, d), F32)
    for cp in first[7:]:
        cp.wait_recv()
    taps_ref[...] = jnp.concatenate([tapsall_ref[s] for s in range(N_DEV)], axis=1)
    for cp in first + second:
        cp.wait_send()


ANY_SPEC = pl.BlockSpec(memory_space=pl.ANY)


def _comm_scratch(na):
    return [pltpu.SemaphoreType.DMA((7 * na,)), pltpu.SemaphoreType.DMA((7 * na,)), pltpu.SemaphoreType.DMA((na,))]


def _leading_block(ref, slot):
    return ref.at[slot]


def _column_block(width):
    def view(ref, slot):
        return ref.at[:, pl.ds(pl.multiple_of(slot * width, LANES), width)]
    return view


class _Gather:
    def __init__(self, ins, outs, send_sems, recv_sems, local_sems, view=_leading_block):
        self.na = len(ins)
        x, y, c = _my_coords()
        self.c = c
        self.view = view
        self.me, self.sibling = (x, y, c), (x, y, 1 - c)
        self.chips = [(1 - x, y), (x, 1 - y), (1 - x, 1 - y)]
        self.outs, self.send_sems, self.recv_sems = outs, send_sems, recv_sems
        self.mine = [pltpu.make_async_copy(ins[a], view(outs[a], _slot(*self.me)), local_sems.at[a])
                     for a in range(self.na)]
        self.first = []
        for a in range(self.na):
            self.first.append(self._copy(a, 0, self.me, self.sibling, src=ins[a]))
            self.first += [self._copy(a, 1 + j, self.me, (*chip, c), src=ins[a]) for j, chip in enumerate(self.chips)]
        self.passed = [self._copy(a, 4 + j, (*chip, c), self.sibling)
                       for a in range(self.na) for j, chip in enumerate(self.chips)]

    def _copy(self, a, k, block, to, src=None):
        dst = self.view(self.outs[a], _slot(*block))
        return pltpu.make_async_remote_copy(
            src_ref=dst if src is None else src, dst_ref=dst,
            send_sem=self.send_sems.at[7 * a + k], recv_sem=self.recv_sems.at[7 * a + k],
            device_id=to, device_id_type=MESH)

    def start(self):
        self.start_near()
        self.start_far()

    def start_near(self):
        for cp in self.mine + [cp for n, cp in enumerate(self.first) if n % 4 != 3]:
            cp.start()

    def start_far(self):
        for cp in self.first[3::4]:
            cp.start()

    def forward_chip(self, j):
        for a in range(self.na):
            self._copy(a, 1 + j, (*self.chips[j], self.c), self.me).wait_recv()
            self.passed[3 * a + j].start()

    def wait_sibling_own(self):
        for a in range(self.na):
            self._copy(a, 0, self.sibling, self.me).wait_recv()

    def wait_sibling_passed(self, j):
        for a in range(self.na):
            self._copy(a, 4 + j, (*self.chips[j], 1 - self.c), self.me).wait_recv()

    def finish_sends(self):
        for cp in self.first + self.passed:
            cp.wait_send()
        for cp in self.mine:
            cp.wait()

    def forward(self):
        for j in range(3):
            self.forward_chip(j)

    def finish(self):
        self.wait_sibling_own()
        for j in range(3):
            self.wait_sibling_passed(j)
        self.finish_sends()


def _scatter_plan(ins, outs, send_sems, recv_sems, local_sems, view=_leading_block):
    na = len(ins)
    x, y, c = _my_coords()
    me = _slot(x, y, c)
    mine = [pltpu.make_async_copy(view(ins[a], me), outs[a].at[me], local_sems.at[a]) for a in range(na)]
    copies = []
    for k in range(1, N_DEV):
        peer = (_flip(x, k & 4), _flip(y, k & 2), _flip(c, k & 1))
        for a in range(na):
            copies.append(pltpu.make_async_remote_copy(
                src_ref=view(ins[a], _slot(*peer)), dst_ref=outs[a].at[me],
                send_sem=send_sems.at[7 * a + k - 1], recv_sem=recv_sems.at[7 * a + k - 1],
                device_id=peer, device_id_type=MESH))

    def start():
        for cp in mine + copies:
            cp.start()

    def finish():
        for cp in copies:
            cp.wait_recv()
        for cp in copies:
            cp.wait_send()
        for cp in mine:
            cp.wait()

    return start, finish


def _forward_in(me_arr, x, prm, w_in_loc, ada, tm, shards):
    t, d = x.shape
    n = w_in_loc.shape[1]
    ns = N_DEV
    na = len(shards)
    nt = t // tm
    nloc = ada[1].shape[1]
    trows = ada[3].shape[0]

    def body(*refs):
        x_ref, prm_ref, wloc_ref = refs[1:4]
        ada_in = refs[4:8]
        shard_refs = refs[8:8 + na]
        proj_ref, h_ref, wg_ref, mod_ref, sc_ref, taps_ref = refs[8 + na:14 + na]
        gathered_refs = refs[14 + na:14 + 2 * na]
        hall_ref, wv_ref, wv_sem = refs[14 + 2 * na:17 + 2 * na]
        ada_scratch = refs[17 + 2 * na:23 + 2 * na]
        sems = refs[23 + 2 * na:]
        s, i = pl.program_id(0), pl.program_id(1)
        columns = _column_block(n)

        def gathers():
            return (_Gather([wloc_ref], [wg_ref], *sems[0:3], view=columns),
                    _Gather(shard_refs, gathered_refs, *sems[3:6]))

        def shard_copy(src, step):
            return pltpu.make_async_copy(src, wv_ref.at[step % 2], wv_sem.at[step % 2])

        @pl.when(i == 0)
        def _():
            me = _slot(*_my_coords())

            @pl.when(s == 0)
            def _():
                g_in, _ = gathers()
                _ada_exchange(*ada_in, mod_ref, sc_ref, taps_ref, *ada_scratch)
                g_in.start_near()
                own = shard_copy(wloc_ref, 0)
                own.start()
                own.wait()

            for step in range(1, ns):
                @pl.when(s == step)
                def _(step=step):
                    shard_copy(columns(wg_ref, me ^ step), step).wait()

        @pl.when(i == nt - 1)
        def _():
            g_in, g_rest = gathers()
            me = _slot(*_my_coords())
            arrivals = {1: g_in.wait_sibling_own,
                        2: functools.partial(g_in.forward_chip, 1), 3: functools.partial(g_in.wait_sibling_passed, 1),
                        4: functools.partial(g_in.forward_chip, 0), 5: functools.partial(g_in.wait_sibling_passed, 0),
                        6: functools.partial(g_in.forward_chip, 2), 7: functools.partial(g_in.wait_sibling_passed, 2)}
            for step, arrive in arrivals.items():
                @pl.when(s == step - 1)
                def _(step=step, arrive=arrive):
                    arrive()
                    if step == 2:
                        g_in.start_far()
                    if step == 4:
                        g_rest.start()
                    if step == 7:
                        g_rest.forward()
                    shard_copy(columns(wg_ref, me ^ step), step).start()

        @pl.when(s == 0)
        def _():
            xv = x_ref[...]
            r = lax.rsqrt(_rowmean(xv * xv) + EPS)
            h = xv * r * prm_ref[P_GMIX:P_GMIX + 1, :] * (1.0 + mod_ref[M_SC1:M_SC1 + 1, :]) \
                + mod_ref[M_SH1:M_SH1 + 1, :]
            hb = h.astype(BF16)
            hall_ref[i] = hb
            h_ref[...] = hb

        proj_ref[...] = _dot(hall_ref[i], wv_ref[s % 2]).astype(BF16)

        @pl.when((s == ns - 1) & (i == nt - 1))
        def _():
            g_in, g_rest = gathers()
            g_in.finish_sends()
            g_rest.finish()

    x_tile = pl.BlockSpec((tm, d), lambda s, i, me: (jnp.where(s == 0, i, nt - 1), 0))
    whole = lambda a: pl.BlockSpec(a.shape, lambda s, i, me: (0,) * len(a.shape))
    small_out = (jax.ShapeDtypeStruct((8, d), F32), jax.ShapeDtypeStruct((N_DEV, 1, d), F32),
                 jax.ShapeDtypeStruct((trows, N_DEV * LANES), F32))
    res = pl.pallas_call(
        body, name="forward_in",
        grid_spec=pltpu.PrefetchScalarGridSpec(
            num_scalar_prefetch=1, grid=(ns, nt),
            in_specs=[x_tile, whole(prm), ANY_SPEC] + [whole(a) for a in ada] + [ANY_SPEC] * na,
            out_specs=[pl.BlockSpec((tm, n), lambda s, i, me: (i, me[0] ^ s)), x_tile, ANY_SPEC]
            + [whole(a) for a in small_out] + [ANY_SPEC] * na,
            scratch_shapes=[pltpu.VMEM((nt, tm, d), BF16), pltpu.VMEM((2, d, n), BF16), pltpu.SemaphoreType.DMA((2,))]
            + _ada_scratch(d, nloc, trows) + _comm_scratch(1) + _comm_scratch(na)),
        out_shape=(jax.ShapeDtypeStruct((t, ns * n), BF16), jax.ShapeDtypeStruct((t, d), BF16),
                   jax.ShapeDtypeStruct((d, ns * n), BF16)) + small_out
        + tuple(jax.ShapeDtypeStruct((N_DEV,) + a.shape, a.dtype) for a in shards),
        compiler_params=_params(("arbitrary", "arbitrary")),
    )(me_arr, x, prm, w_in_loc, *ada, *shards)
    return res[0], res[1], res[2], res[3], res[4], res[5], res[6:]


def _forward_mix(x, proj, mod, prm, taps, w_so, w_co, w_o, tm, shards):
    t, d = x.shape
    nt = t // tm
    na = len(shards)
    hs, hc = HALO_SHORT, HALO_CONF
    r3, r31 = 0, 8

    def body(*refs):
        x_ref, proj_ref, mod_ref, prm_ref, taps_ref, wso_ref, wco_ref, wo_ref = refs[:8]
        x1_ref, ya_ref, yb_ref, mix_ref, u1_ref = refs[8 + na:13 + na]
        cv_ext, u0_ext, conv3_ref, u1f_ref = refs[13 + 2 * na:17 + 2 * na]
        gather = _Gather(refs[8:8 + na], refs[13 + na:13 + 2 * na], *refs[17 + 2 * na:])
        i = pl.program_id(0)

        @pl.when(i == 0)
        def _():
            gather.start()
            cv_ext[0:hs, :] = jnp.zeros((hs, d), F32)
            u0_ext[0:hc, :] = jnp.zeros((hc, d), F32)

        @pl.when(i == (3 * nt) // 4)
        def _():
            gather.forward()

        def col(g):
            return proj_ref[:, g * d:(g + 1) * d].astype(F32)

        cv_ext[hs:hs + tm, :] = col(1) * col(2)
        u0_ext[hc:hc + tm, :] = col(3) * _sigmoid(col(4))
        _tap_conv(cv_ext, taps_ref, [(r3 + wi, off) for wi, off in _causal_taps(SHORT_K, hs)], tm, hs, conv3_ref)
        _tap_conv(u0_ext, taps_ref, [(r31 + wi, off) for wi, off in _causal_taps(CONF_K, hc)], tm, hc, u1f_ref)
        cv_ext[0:hs, :] = cv_ext[tm:tm + hs, :]
        u0_ext[0:hc, :] = u0_ext[tm:tm + hc, :]

        ya_pre = (col(0) * conv3_ref[...]).astype(BF16)
        y_a = _dot(ya_pre, wso_ref[...])
        u1 = u1f_ref[...] + prm_ref[P_CBIAS:P_CBIAS + 1, :]
        u1_ref[...] = u1.astype(BF16)
        u1 = u1_ref[...].astype(F32)
        mu = _rowmean(u1)
        uc = u1 - mu
        rstd = lax.rsqrt(_rowmean(uc * uc) + LN_EPS)
        u2 = uc * rstd * prm_ref[P_LNG:P_LNG + 1, :] + prm_ref[P_LNB:P_LNB + 1, :]
        u3 = (u2 * _sigmoid(u2)).astype(BF16)
        y_b = _dot(u3, wco_ref[...])
        ya_ref[...] = y_a.astype(BF16)
        yb_ref[...] = y_b.astype(BF16)
        merged = _sigmoid(col(5)) * ya_ref[...].astype(F32) + _sigmoid(col(6)) * yb_ref[...].astype(F32)
        mix = _dot(merged.astype(BF16), wo_ref[...])
        mix_ref[...] = mix.astype(BF16)
        x1_ref[...] = x_ref[...] + mod_ref[M_G1:M_G1 + 1, :] * mix

        @pl.when(i == nt - 1)
        def _():
            gather.finish()

    tile = pl.BlockSpec((tm, d), lambda i: (i, 0))
    whole = lambda shape: pl.BlockSpec(shape, lambda i: (0,) * len(shape))
    res = pl.pallas_call(
        body, name="forward_mix", grid=(nt,),
        out_shape=(jax.ShapeDtypeStruct((t, d), F32),) + (jax.ShapeDtypeStruct((t, d), BF16),) * 4
        + tuple(jax.ShapeDtypeStruct((N_DEV,) + a.shape, a.dtype) for a in shards),
        in_specs=[tile, pl.BlockSpec((tm, 7 * d), lambda i: (i, 0)), whole((8, d)), whole((8, d)),
                  whole(taps.shape), whole((d, d)), whole((d, d)), whole((d, d))] + [ANY_SPEC] * na,
        out_specs=[tile] * 5 + [ANY_SPEC] * na,
        scratch_shapes=[pltpu.VMEM((hs + tm, d), F32), pltpu.VMEM((hc + tm, d), F32),
                        pltpu.VMEM((tm, d), F32), pltpu.VMEM((tm, d), F32)] + _comm_scratch(na),
        compiler_params=_params(("arbitrary",)),
    )(x, proj, mod, prm, taps, w_so, w_co, w_o, *shards)
    return res[:5], res[5:]


def _ffn_chunks(ff):
    mxu = 2 * LANES
    cut = (ff // mxu + 1) // 2 * mxu
    return [(0, cut), (cut, ff)] if 0 < cut < ff and ff % mxu == 0 else [(0, ff)]


def _forward_ffn(x1, tgt, mod, prm, w_fi_t, w_fo, tm):
    t, d = x1.shape
    ff = w_fo.shape[0]

    def body(x1_ref, tgt_ref, mod_ref, prm_ref, wfi_hbm, wfo_hbm,
             dx2_ref, ab_ref, h2_ref, sums_ref, wfi_ref, wfo_ref):
        i = pl.program_id(0)

        @pl.when(i == 0)
        def _():
            pltpu.sync_copy(wfi_hbm, wfi_ref)
            pltpu.sync_copy(wfo_hbm, wfo_ref)
            sums_ref[...] = jnp.zeros((8, d), F32)

        x1v = x1_ref[...]
        r2 = lax.rsqrt(_rowmean(x1v * x1v) + EPS)
        h2 = (x1v * r2 * prm_ref[P_GFFN:P_GFFN + 1, :] * (1.0 + mod_ref[M_SC2:M_SC2 + 1, :])
              + mod_ref[M_SH2:M_SH2 + 1, :]).astype(BF16)
        h2_ref[...] = h2
        f = jnp.zeros((tm, d), F32)
        for c0, c1 in _ffn_chunks(ff):
            ab_ref[:, c0:c1] = _dot_nt(h2, wfi_ref[c0:c1, :]).astype(BF16)
            ab_ref[:, ff + c0:ff + c1] = _dot_nt(h2, wfi_ref[ff + c0:ff + c1, :]).astype(BF16)
            a = ab_ref[:, c0:c1].astype(F32)
            act = (a * _sigmoid(a) * ab_ref[:, ff + c0:ff + c1].astype(F32)).astype(BF16)
            f = f + _dot(act, wfo_ref[c0:c1, :])
        x2 = x1v + mod_ref[M_G2:M_G2 + 1, :] * f
        r3 = lax.rsqrt(_rowmean(x2 * x2) + EPS)
        xn3 = x2 * r3
        gfin = prm_ref[P_GFIN:P_GFIN + 1, :]
        err = xn3 * gfin - tgt_ref[...]
        dy = err * (1.0 / d)
        dxn3 = dy * gfin
        dx2 = r3 * (dxn3 - xn3 * _rowmean(dxn3 * xn3))
        dx2_ref[...] = dx2
        sums_ref[0:1, :] = sums_ref[0:1, :] + _colsum(dy * xn3)
        sums_ref[1:2, :] = sums_ref[1:2, :] + _colsum(dx2 * f)
        sums_ref[2:3, :] = sums_ref[2:3, :] + _colsum(err * err) * (0.5 / d)

    tile = pl.BlockSpec((tm, d), lambda i: (i, 0))
    whole = lambda shape: pl.BlockSpec(shape, lambda i: (0,) * len(shape))
    return pl.pallas_call(
        body, name="forward_ffn", grid=(t // tm,),
        out_shape=(jax.ShapeDtypeStruct((t, d), F32), jax.ShapeDtypeStruct((t, 2 * ff), BF16),
                   jax.ShapeDtypeStruct((t, d), BF16), jax.ShapeDtypeStruct((8, d), F32)),
        in_specs=[tile, tile, whole((8, d)), whole((8, d)), ANY_SPEC, ANY_SPEC],
        out_specs=[tile, pl.BlockSpec((tm, 2 * ff), lambda i: (i, 0)), tile, whole((8, d))],
        scratch_shapes=[pltpu.VMEM(w_fi_t.shape, BF16), pltpu.VMEM(w_fo.shape, BF16)],
        compiler_params=_params(("arbitrary",)),
    )(x1, tgt, mod, prm, w_fi_t, w_fo)


def _backward_ffn(dx2, x1, ab, mod, prm, w_fi_t, w_fo, tm):
    t, d = x1.shape
    ff = w_fo.shape[0]

    def body(dx2_ref, x1_ref, ab_ref, mod_ref, prm_ref, wfi_hbm, wfo_hbm,
             dx1_ref, df_ref, act_ref, dab_ref, sums_ref, wfi_ref, wfo_ref):
        i = pl.program_id(0)

        @pl.when(i == 0)
        def _():
            pltpu.sync_copy(wfi_hbm, wfi_ref)
            pltpu.sync_copy(wfo_hbm, wfo_ref)
            sums_ref[...] = jnp.zeros((8, d), F32)

        dx2v = dx2_ref[...]
        df = (mod_ref[M_G2:M_G2 + 1, :] * dx2v).astype(BF16)
        df_ref[...] = df
        dh2 = jnp.zeros((tm, d), F32)
        for c0, c1 in _ffn_chunks(ff):
            dact = _dot_nt(df, wfo_ref[c0:c1, :])
            a = ab_ref[:, c0:c1].astype(F32)
            b = ab_ref[:, ff + c0:ff + c1].astype(F32)
            s = _sigmoid(a)
            sil = a * s
            act_ref[:, c0:c1] = (sil * b).astype(BF16)
            da = (dact * b * (s * (1.0 + a * (1.0 - s)))).astype(BF16)
            db = (dact * sil).astype(BF16)
            dab_ref[:, c0:c1] = da
            dab_ref[:, ff + c0:ff + c1] = db
            dh2 = dh2 + _dot(da, wfi_ref[c0:c1, :]) + _dot(db, wfi_ref[ff + c0:ff + c1, :])
        x1v = x1_ref[...]
        r2 = lax.rsqrt(_rowmean(x1v * x1v) + EPS)
        xn2 = x1v * r2
        gffn = prm_ref[P_GFFN:P_GFFN + 1, :]
        scale = 1.0 + mod_ref[M_SC2:M_SC2 + 1, :]
        dxn2 = dh2 * gffn * scale
        dx1_ref[...] = dx2v + r2 * (dxn2 - xn2 * _rowmean(dxn2 * xn2))
        hx = dh2 * xn2
        sums_ref[0:1, :] = sums_ref[0:1, :] + _colsum(dh2)
        sums_ref[1:2, :] = sums_ref[1:2, :] + _colsum(hx) * gffn
        sums_ref[2:3, :] = sums_ref[2:3, :] + _colsum(hx) * scale

    tile = pl.BlockSpec((tm, d), lambda i: (i, 0))
    whole = lambda shape: pl.BlockSpec(shape, lambda i: (0,) * len(shape))
    wide = lambda n: pl.BlockSpec((tm, n), lambda i: (i, 0))
    return pl.pallas_call(
        body, name="backward_ffn", grid=(t // tm,),
        out_shape=(jax.ShapeDtypeStruct((t, d), F32), jax.ShapeDtypeStruct((t, d), BF16),
                   jax.ShapeDtypeStruct((t, ff), BF16), jax.ShapeDtypeStruct((t, 2 * ff), BF16),
                   jax.ShapeDtypeStruct((8, d), F32)),
        in_specs=[tile, tile, wide(2 * ff), whole((8, d)), whole((8, d)), ANY_SPEC, ANY_SPEC],
        out_specs=[tile, tile, wide(ff), wide(2 * ff), whole((8, d))],
        scratch_shapes=[pltpu.VMEM(w_fi_t.shape, BF16), pltpu.VMEM(w_fo.shape, BF16)],
        compiler_params=_params(("arbitrary",)),
    )(dx2, x1, ab, mod, prm, w_fi_t, w_fo)


def _backward_mix(dx1, proj, y_a, y_b, mix, u1, mod, prm, taps, w_so, w_co, w_o, tm, partials):
    t, d = dx1.shape
    nt = t // tm
    na = len(partials)
    hs, hc = HALO_SHORT, HALO_CONF
    r3, r31 = 0, 8

    def body(*refs):
        (dx1_ref, proj_ref, halo_ref, ya_ref, yb_ref, mix_ref, u1_ref, mod_ref, prm_ref, taps_ref,
         wso_ref, wco_ref, wo_ref) = refs[:13]
        (dproj_ref, dmix_ref, dya_ref, dyb_ref, merged_ref, yapre_ref, u3_ref, sums_ref, dw3_ref,
         dw31_ref) = refs[13 + na:23 + na]
        cv_ext, u0_ext, d3_ext, du1_ext, tmp_ref = refs[23 + 2 * na:28 + 2 * na]
        scatter_start, scatter_finish = _scatter_plan(refs[13:13 + na], refs[23 + na:23 + 2 * na], *refs[28 + 2 * na:])
        i = pl.program_id(0)
        first_tile = i == nt - 1

        @pl.when(i == 0)
        def _():
            scatter_start()
            sums_ref[...] = jnp.zeros((8, d), F32)
            dw3_ref[...] = jnp.zeros(dw3_ref.shape, F32)
            dw31_ref[...] = jnp.zeros(dw31_ref.shape, F32)
            d3_ext[tm:tm + hs, :] = jnp.zeros((hs, d), F32)
            du1_ext[tm:tm + hc, :] = jnp.zeros((hc, d), F32)

        def col(g):
            return proj_ref[:, g * d:(g + 1) * d].astype(F32)

        def hcol(g, rows):
            v = halo_ref[HALO_CONF - rows:HALO_CONF, g * d:(g + 1) * d].astype(F32)
            return jnp.where(first_tile, 0.0, v)

        dx1v = dx1_ref[...]
        mixv = mix_ref[...].astype(F32)
        dmix = (mod_ref[M_G1:M_G1 + 1, :] * dx1v).astype(BF16)
        dmix_ref[...] = dmix
        sums_ref[0:1, :] = sums_ref[0:1, :] + _colsum(dx1v * mixv)
        dmerged = _dot_nt(dmix, wo_ref[...])
        ga = _sigmoid(col(5))
        gb = _sigmoid(col(6))
        yav = ya_ref[...].astype(F32)
        ybv = yb_ref[...].astype(F32)
        dya_f = dmerged * ga
        dyb_f = dmerged * gb
        dya = dya_f.astype(BF16)
        dyb = dyb_f.astype(BF16)
        dya_ref[...] = dya
        dyb_ref[...] = dyb
        dproj_ref[:, 5 * d:6 * d] = (dya_f * yav * (1.0 - ga)).astype(BF16)
        dproj_ref[:, 6 * d:7 * d] = (dyb_f * ybv * (1.0 - gb)).astype(BF16)
        merged_ref[...] = (ga * yav + gb * ybv).astype(BF16)

        dya_pre = _dot_nt(dya, wso_ref[...])
        c_s, v_s, b_s = col(1), col(2), col(0)
        cv_ext[0:hs, :] = hcol(1, hs) * hcol(2, hs)
        cv_ext[hs:hs + tm, :] = c_s * v_s
        _tap_conv(cv_ext, taps_ref, [(r3 + wi, off) for wi, off in _causal_taps(SHORT_K, hs)], tm, hs, tmp_ref)
        conv3 = tmp_ref[...]
        yapre_ref[...] = (b_s * conv3).astype(BF16)
        dproj_ref[:, 0:d] = (dya_pre * conv3).astype(BF16)
        d3_ext[0:tm, :] = dya_pre * b_s
        _tap_wgrad(d3_ext, cv_ext, _causal_taps(SHORT_K, hs), tm, hs, dw3_ref)
        _tap_conv(d3_ext, taps_ref, [(r3 + wi, off) for wi, off in _anticausal_taps(SHORT_K)], tm, hs, tmp_ref)
        dcv = tmp_ref[...]
        dproj_ref[:, d:2 * d] = (dcv * v_s).astype(BF16)
        dproj_ref[:, 2 * d:3 * d] = (dcv * c_s).astype(BF16)
        d3_ext[tm:tm + hs, :] = d3_ext[0:hs, :]

        du3 = _dot_nt(dyb, wco_ref[...])
        u1v = u1_ref[...].astype(F32)
        mu = _rowmean(u1v)
        uc = u1v - mu
        rstd = lax.rsqrt(_rowmean(uc * uc) + LN_EPS)
        uhat = uc * rstd
        lng = prm_ref[P_LNG:P_LNG + 1, :]
        u2 = uhat * lng + prm_ref[P_LNB:P_LNB + 1, :]
        s2 = _sigmoid(u2)
        u3_ref[...] = (u2 * s2).astype(BF16)
        du2 = du3 * (s2 * (1.0 + u2 * (1.0 - s2)))
        sums_ref[1:2, :] = sums_ref[1:2, :] + _colsum(du2 * uhat)
        sums_ref[2:3, :] = sums_ref[2:3, :] + _colsum(du2)
        duhat = du2 * lng
        du1 = rstd * (duhat - _rowmean(duhat) - uhat * _rowmean(duhat * uhat))
        sums_ref[3:4, :] = sums_ref[3:4, :] + _colsum(du1)
        du1_ext[0:tm, :] = du1
        v_c = col(3)
        sg = _sigmoid(col(4))
        u0_ext[0:hc, :] = hcol(3, hc) * _sigmoid(hcol(4, hc))
        u0_ext[hc:hc + tm, :] = v_c * sg
        _tap_wgrad(du1_ext, u0_ext, _causal_taps(CONF_K, hc), tm, hc, dw31_ref)
        _tap_conv(du1_ext, taps_ref, [(r31 + wi, off) for wi, off in _anticausal_taps(CONF_K)], tm, hc, tmp_ref)
        dv_c = tmp_ref[...] * sg
        dproj_ref[:, 3 * d:4 * d] = dv_c.astype(BF16)
        dproj_ref[:, 4 * d:5 * d] = (dv_c * v_c * (1.0 - sg)).astype(BF16)
        du1_ext[tm:tm + hc, :] = du1_ext[0:hc, :]

        @pl.when(i == nt - 1)
        def _():
            scatter_finish()

    rev = lambda i: (nt - 1 - i, 0)
    tile = pl.BlockSpec((tm, d), rev)
    whole = lambda shape: pl.BlockSpec(shape, lambda i: (0,) * len(shape))
    hblocks = tm // HALO_CONF
    halo = pl.BlockSpec((HALO_CONF, 7 * d), lambda i: (jnp.maximum((nt - 1 - i) * hblocks - 1, 0), 0))
    bf = jax.ShapeDtypeStruct((t, d), BF16)
    res = pl.pallas_call(
        body, name="backward_mix", grid=(nt,),
        out_shape=(jax.ShapeDtypeStruct((t, 7 * d), BF16), bf, bf, bf, bf, bf, bf,
                   jax.ShapeDtypeStruct((8, d), F32),
                   jax.ShapeDtypeStruct((SUBLANES * SHORT_K, d), F32),
                   jax.ShapeDtypeStruct((SUBLANES * CONF_K, d), F32))
        + tuple(jax.ShapeDtypeStruct(p.shape, p.dtype) for p in partials),
        in_specs=[tile, pl.BlockSpec((tm, 7 * d), rev), halo, tile, tile, tile, tile,
                  whole((8, d)), whole((8, d)), whole(taps.shape), whole((d, d)), whole((d, d)), whole((d, d))]
        + [ANY_SPEC] * na,
        out_specs=[pl.BlockSpec((tm, 7 * d), rev), tile, tile, tile, tile, tile, tile,
                   whole((8, d)), whole((SUBLANES * SHORT_K, d)), whole((SUBLANES * CONF_K, d))] + [ANY_SPEC] * na,
        scratch_shapes=[pltpu.VMEM((hs + tm, d), F32), pltpu.VMEM((hc + tm, d), F32),
                        pltpu.VMEM((tm + hs, d), F32), pltpu.VMEM((tm + hc, d), F32),
                        pltpu.VMEM((tm, d), F32)] + _comm_scratch(na),
        compiler_params=_params(("arbitrary",)),
    )(dx1, proj, proj, y_a, y_b, mix, u1, mod, prm, taps, w_so, w_co, w_o, *partials)
    return res[:10], res[10:]


def _backward_in(dproj, x, dx1, mod, prm, w_in_g, tm, partials, partial_view, exchanged_shapes):
    t, d = x.shape
    n_all = w_in_g.shape[1]
    na = len(partials)
    nt = t // tm

    def body(*refs):
        dproj_ref, x_ref, dx1_ref, mod_ref, prm_ref, w_hbm = refs[:6]
        gx_ref, sums_ref = refs[6 + na:8 + na]
        w_ref = refs[8 + 2 * na]
        scatter_start, scatter_finish = _scatter_plan(refs[6:6 + na], refs[8 + na:8 + 2 * na], *refs[9 + 2 * na:],
                                                      view=partial_view)

        @pl.when(pl.program_id(0) == 0)
        def _():
            scatter_start()
            pltpu.sync_copy(w_hbm, w_ref)
            sums_ref[...] = jnp.zeros((8, d), F32)

        dh = _dot_nt(dproj_ref[...], w_ref[...])
        xv = x_ref[...]
        r1 = lax.rsqrt(_rowmean(xv * xv) + EPS)
        xn = xv * r1
        gmix = prm_ref[P_GMIX:P_GMIX + 1, :]
        scale = 1.0 + mod_ref[M_SC1:M_SC1 + 1, :]
        dxn = dh * gmix * scale
        gx_ref[...] = dx1_ref[...] + r1 * (dxn - xn * _rowmean(dxn * xn))
        hx = dh * xn
        sums_ref[0:1, :] = sums_ref[0:1, :] + _colsum(dh)
        sums_ref[1:2, :] = sums_ref[1:2, :] + _colsum(hx) * gmix
        sums_ref[2:3, :] = sums_ref[2:3, :] + _colsum(hx) * scale

        @pl.when(pl.program_id(0) == nt - 1)
        def _():
            scatter_finish()

    tile = pl.BlockSpec((tm, d), lambda i: (i, 0))
    whole = pl.BlockSpec((8, d), lambda i: (0, 0))
    res = pl.pallas_call(
        body, name="backward_in", grid=(nt,),
        out_shape=(jax.ShapeDtypeStruct((t, d), F32), jax.ShapeDtypeStruct((8, d), F32))
        + tuple(jax.ShapeDtypeStruct(s, p.dtype) for p, s in zip(partials, exchanged_shapes)),
        in_specs=[pl.BlockSpec((tm, n_all), lambda i: (i, 0)), tile, tile, whole, whole, ANY_SPEC] + [ANY_SPEC] * na,
        out_specs=[tile, whole] + [ANY_SPEC] * na,
        scratch_shapes=[pltpu.VMEM(w_in_g.shape, BF16)] + _comm_scratch(na),
        compiler_params=_params(("arbitrary",)),
    )(dproj, x, dx1, mod, prm, w_in_g, *partials)
    return res[0], res[1], res[2:]


def _weight_grad(a, b, a_spec, b_spec, ns, m, n, nk, name, partials=(), by_columns=False):
    na = len(partials)

    def body(*refs):
        a_ref, b_ref = refs[:2]
        o_ref = refs[2 + na]
        acc_ref = refs[3 + 2 * na]
        s, k = pl.program_id(0), pl.program_id(1)
        if na:
            scatter_start, scatter_finish = _scatter_plan(refs[2:2 + na], refs[3 + na:3 + 2 * na], *refs[4 + 2 * na:])

            @pl.when((s == 0) & (k == 0))
            def _():
                scatter_start()

        av = a_ref[0] if len(a_ref.shape) == 3 else a_ref[...]
        bv = b_ref[0] if len(b_ref.shape) == 3 else b_ref[...]
        part = _dot_tn(av, bv)

        @pl.when(k == 0)
        def _():
            acc_ref[...] = part

        @pl.when(k > 0)
        def _():
            acc_ref[...] = acc_ref[...] + part

        @pl.when(k == nk - 1)
        def _():
            if by_columns:
                o_ref[...] = acc_ref[...].astype(BF16)
            else:
                o_ref[0] = acc_ref[...].astype(BF16)

        if na:
            @pl.when((s == ns - 1) & (k == nk - 1))
            def _():
                scatter_finish()

    if by_columns:
        out_shape, out_spec = (m, ns * n), pl.BlockSpec((m, n), lambda s, k: (0, s))
    else:
        out_shape, out_spec = (ns, m, n), pl.BlockSpec((1, m, n), lambda s, k: (s, 0, 0))
    res = pl.pallas_call(
        body, name=name, grid=(ns, nk),
        out_shape=(jax.ShapeDtypeStruct(out_shape, BF16),)
        + tuple(jax.ShapeDtypeStruct(p.shape, p.dtype) for p in partials),
        in_specs=[a_spec, b_spec] + [ANY_SPEC] * na,
        out_specs=[out_spec] + [ANY_SPEC] * na,
        scratch_shapes=[pltpu.VMEM((m, n), F32)] + (_comm_scratch(na) if na else []),
        compiler_params=_params(("arbitrary", "arbitrary")),
    )(a, b, *partials)
    return (res[0], res[1:]) if na else res[0]


def _adamw(w, g, m, v):
    m = ADAM_B1 * m + (1.0 - ADAM_B1) * g
    v = ADAM_B2 * v + (1.0 - ADAM_B2) * (g * g)
    m_hat = m / (1.0 - ADAM_B1 ** ADAM_STEP)
    v_hat = v / (1.0 - ADAM_B2 ** ADAM_STEP)
    delta = -ADAM_LR * (m_hat / (jnp.sqrt(v_hat) + ADAM_EPS) + ADAM_WD * w)
    return delta, m, v


def _adamw_shard(parts, w, m, v, tr, name):
    r, c = w.shape

    def body(p_ref, w_ref, m_ref, v_ref, g_ref, d_ref, nm_ref, nv_ref):
        g = p_ref[0].astype(F32)
        for s in range(1, N_DEV):
            g = g + p_ref[s].astype(F32)
        delta, nm, nv = _adamw(w_ref[...], g, m_ref[...], v_ref[...])
        g_ref[...] = g
        d_ref[...] = delta
        nm_ref[...] = nm
        nv_ref[...] = nv

    tile = pl.BlockSpec((tr, c), lambda i: (i, 0))
    return pl.pallas_call(
        body, name=name, grid=(r // tr,),
        out_shape=(jax.ShapeDtypeStruct((r, c), F32),) * 4,
        in_specs=[pl.BlockSpec((N_DEV, tr, c), lambda i: (0, i, 0)), tile, tile, tile],
        out_specs=[tile] * 4,
        compiler_params=_params(("arbitrary",)),
    )(parts, w, m, v)


def _ada_update(sc_all, dmod_cols, w, m, v, tr):
    d, n = w.shape

    def body(sc_ref, dm_ref, w_ref, m_ref, v_ref, g_ref, d_ref, nm_ref, nv_ref):
        g = lax.dot_general(sc_ref[...], dm_ref[...], (((0,), (0,)), ((), ())),
                            preferred_element_type=F32, precision=lax.Precision.HIGHEST)
        delta, nm, nv = _adamw(w_ref[...], g, m_ref[...], v_ref[...])
        g_ref[...] = g
        d_ref[...] = delta
        nm_ref[...] = nm
        nv_ref[...] = nv

    tile = pl.BlockSpec((tr, n), lambda i: (i, 0))
    return pl.pallas_call(
        body, name="ada_update", grid=(d // tr,),
        out_shape=(jax.ShapeDtypeStruct((d, n), F32),) * 4,
        in_specs=[pl.BlockSpec((N_DEV, tr), lambda i: (0, i)), pl.BlockSpec((N_DEV, n), lambda i: (0, 0)),
                  tile, tile, tile],
        out_specs=[tile] * 4,
        compiler_params=_params(("arbitrary",)),
    )(sc_all, dmod_cols, w, m, v)


def _small_exchange(vec, cg):
    l = vec.shape[2]
    rows = cg.shape[1]

    def body(vec_ref, cg_ref, vall_ref, cgr_ref, send_sems, recv_sems):
        x, y, c = _my_coords()
        me = _slot(x, y, c)
        vall_ref[me] = vec_ref[0]
        cgr_ref[me] = cg_ref[me]
        copies = []
        for k in range(1, N_DEV):
            peer = (_flip(x, k & 4), _flip(y, k & 2), _flip(c, k & 1))
            copies.append(pltpu.make_async_remote_copy(
                src_ref=vall_ref.at[me], dst_ref=vall_ref.at[me], send_sem=send_sems.at[k - 1],
                recv_sem=recv_sems.at[k - 1], device_id=peer, device_id_type=MESH))
            copies.append(pltpu.make_async_remote_copy(
                src_ref=cg_ref.at[_slot(*peer)], dst_ref=cgr_ref.at[me], send_sem=send_sems.at[7 + k - 1],
                recv_sem=recv_sems.at[7 + k - 1], device_id=peer, device_id_type=MESH))
        for cp in copies:
            cp.start()
        for cp in copies:
            cp.wait_recv()
        for cp in copies:
            cp.wait_send()

    vm = pl.BlockSpec(memory_space=pltpu.VMEM)
    return pl.pallas_call(
        body, name="small_exchange",
        out_shape=(jax.ShapeDtypeStruct((N_DEV, 1, l), F32), jax.ShapeDtypeStruct((N_DEV, rows, LANES), F32)),
        in_specs=[vm, vm], out_specs=[vm, vm],
        scratch_shapes=[pltpu.SemaphoreType.DMA((14,)), pltpu.SemaphoreType.DMA((14,))],
        compiler_params=_params(),
    )(vec, cg)


def _small_update(vall, cgr, smalls):
    ns = len(smalls)

    def body(*refs):
        vall_ref, cgr_ref = refs[0], refs[1]
        wmv = refs[2:2 + 3 * ns]
        outs = refs[2 + 3 * ns:]
        for p, (_, _, _, lo, hi, kind) in enumerate(smalls):
            w_ref, m_ref, v_ref = wmv[3 * p:3 * p + 3]
            part = (lambda s: vall_ref[s, :, lo:hi]) if kind == "vec" else (lambda s: cgr_ref[s, lo:hi, :])
            g = part(0)
            for s in range(1, N_DEV):
                g = g + part(s)
            delta, nm, nv = _adamw(w_ref[...], g, m_ref[...], v_ref[...])
            for o_ref, val in zip(outs[4 * p:4 * p + 4], (g, delta, nm, nv)):
                o_ref[...] = val

    vm = pl.BlockSpec(memory_space=pltpu.VMEM)
    args = [vall, cgr]
    out_shape = []
    for w, m, v, _, _, _ in smalls:
        args += [w, m, v]
        out_shape += [jax.ShapeDtypeStruct(w.shape, F32)] * 4
    res = pl.pallas_call(
        body, name="small_update",
        out_shape=tuple(out_shape),
        in_specs=[vm] * len(args), out_specs=[vm] * len(out_shape),
        compiler_params=_params(),
    )(*args)
    return [res[4 * p:4 + 4 * p] for p in range(ns)]


def _pick(t, want):
    return want if t % want == 0 else t


def kernel(x, c, w_ada, b_ada, norm_mix_g, w_in, conv_short_w, w_short_out, conv_conf_w, conv_conf_b, conf_ln_g, conf_ln_b, w_conf_out, w_o, norm_ffn_g, w_ffn_in, w_ffn_out, final_norm_g, loss_target, m_w_ada, m_b_ada, m_norm_mix_g, m_w_in, m_conv_short_w, m_w_short_out, m_conv_conf_w, m_conv_conf_b, m_conf_ln_g, m_conf_ln_b, m_w_conf_out, m_w_o, m_norm_ffn_g, m_w_ffn_in, m_w_ffn_out, m_final_norm_g, v_w_ada, v_b_ada, v_norm_mix_g, v_w_in, v_conv_short_w, v_w_short_out, v_conv_conf_w, v_conv_conf_b, v_conf_ln_g, v_conf_ln_b, v_w_conf_out, v_w_o, v_norm_ffn_g, v_w_ffn_in, v_w_ffn_out, v_final_norm_g):
    t, d = x.shape[1], x.shape[2]
    x2 = x.reshape(t, d)
    tgt = loss_target.reshape(t, d)
    me = _slot(*_my_coords())
    tm = _pick(t, 256)
    tm_fwd_in = _pick(t, 1024)
    tk = _pick(t, 2048)

    taps_loc = jnp.zeros((40, LANES), F32)
    taps_loc = taps_loc.at[0:SHORT_K].set(conv_short_w[0]).at[8:8 + CONF_K].set(conv_conf_w[0])
    prm = jnp.concatenate([norm_mix_g, norm_ffn_g, final_norm_g.reshape(1, d), conv_conf_b, conf_ln_g, conf_ln_b,
                           jnp.zeros((2, d), F32)], axis=0)

    n_in = w_in.shape[2]
    nk = t // tk
    tok = pl.BlockSpec((tk, d), lambda s, k: (k, 0))
    rows = d // N_DEV
    frows = w_ffn_out.shape[1]

    proj, h, w_in_g, mod, sc_all3, taps, (w_so_g, w_co_g, w_o_g) = _forward_in(
        jnp.reshape(me, (1,)).astype(jnp.int32), x2, prm, w_in[0].astype(BF16),
        (c, w_ada[0], b_ada, taps_loc), tm_fwd_in,
        [w_short_out[0].astype(BF16), w_conf_out[0].astype(BF16), w_o[0].astype(BF16)])
    w_so = w_so_g.reshape(d, d)
    w_co = w_co_g.reshape(d, d)
    w_oo = w_o_g.reshape(d, d)
    w_fi_t, m_fi_t, v_fi_t = w_ffn_in[0].T, m_w_ffn_in[0].T, v_w_ffn_in[0].T
    (x1, y_a, y_b, mix, u1), (w_fi_g, w_fo_g) = _forward_mix(
        x2, proj, mod, prm, taps, w_so, w_co, w_oo, tm, [w_fi_t.astype(BF16), w_ffn_out[0].astype(BF16)])
    fb = w_fi_g.shape[1]
    ff = N_DEV * frows
    w_fi_all = w_fi_g.reshape(N_DEV * fb, d)
    w_fo_all = w_fo_g.reshape(ff, d)
    dx2, ab, h2, sums_f = _forward_ffn(x1, tgt, mod, prm, w_fi_all, w_fo_all, tm)

    dx1, df, act, dab, sums_b = _backward_ffn(dx2, x1, ab, mod, prm, w_fi_all, w_fo_all, tm)
    fb2 = 2 * fb
    pair_tok = pl.BlockSpec((tk, fb2), lambda s, k: (k, s))
    g_fi = _weight_grad(dab, h2, pair_tok, tok, N_DEV // 2, fb2, d, nk, "grad_w_ffn_in")
    g_fo = _weight_grad(act, df, pair_tok, tok, N_DEV // 4, fb2, d, nk, "grad_w_ffn_out")
    (dproj, dmix, dya, dyb, merged, ya_pre, u3, sums_m, dw3p, dw31p), (p_fi, p_fo) = _backward_mix(
        dx1, proj, y_a, y_b, mix, u1, mod, prm, taps, w_so, w_co, w_oo, tm,
        [g_fi.reshape(N_DEV, fb, d), g_fo.reshape(N_DEV, frows, d)])
    tk_in = _pick(t, 4096)
    nk_in = t // tk_in
    tok_in = pl.BlockSpec((tk_in, d), lambda s, k: (k, 0))
    g_so = _weight_grad(ya_pre, dya, tok, tok, 1, d, d, nk, "grad_w_short_out")
    g_co = _weight_grad(u3, dyb, tok, tok, 1, d, d, nk, "grad_w_conf_out")
    g_oo = _weight_grad(merged, dmix, tok, tok, 1, d, d, nk, "grad_w_o")
    n_blk = _pick(N_DEV * n_in, d)
    g_in, (p_so, p_co, p_oo) = _weight_grad(
        h, dproj, tok_in, pl.BlockSpec((tk_in, n_blk), lambda s, k: (k, s)),
        N_DEV * n_in // n_blk, d, n_blk, nk_in, "grad_w_in",
        [g_so.reshape(N_DEV, rows, d), g_co.reshape(N_DEV, rows, d), g_oo.reshape(N_DEV, rows, d)], by_columns=True)
    grad_x, sums_i, (p_in,) = _backward_in(dproj, x2, dx1, mod, prm, w_in_g, tm, [g_in], _column_block(n_in),
                                           [(N_DEV, d, n_in)])

    up_in = _adamw_shard(p_in, w_in[0], m_w_in[0], v_w_in[0], _pick(d, 256), "adamw_w_in")
    up_so = _adamw_shard(p_so, w_short_out[0], m_w_short_out[0], v_w_short_out[0], rows, "adamw_w_short_out")
    up_co = _adamw_shard(p_co, w_conf_out[0], m_w_conf_out[0], v_w_conf_out[0], rows, "adamw_w_conf_out")
    up_oo = _adamw_shard(p_oo, w_o[0], m_w_o[0], v_w_o[0], rows, "adamw_w_o")
    up_fi = tuple(a.T for a in _adamw_shard(p_fi, w_fi_t, m_fi_t, v_fi_t, fb // 2, "adamw_w_ffn_in"))
    up_fo = _adamw_shard(p_fo, w_ffn_out[0], m_w_ffn_out[0], v_w_ffn_out[0], frows, "adamw_w_ffn_out")

    vec = jnp.concatenate([sums_i[0:2], sums_m[0:1], sums_b[0:2], sums_f[1:2],
                           sums_i[2:3], sums_m[3:4], sums_m[1:3], sums_b[2:3], sums_f[0:1],
                           sums_f[2:3]], axis=0)
    vec = vec.reshape(1, 1, 13 * d)
    dw3 = dw3p.reshape(SHORT_K, SUBLANES, d).sum(axis=1)
    dw31 = dw31p.reshape(CONF_K, SUBLANES, d).sum(axis=1)
    cg = jnp.zeros((40, d), F32).at[0:SHORT_K].set(dw3).at[8:8 + CONF_K].set(dw31)
    cg = cg.reshape(40, N_DEV, LANES).transpose(1, 0, 2)
    fin = lambda a: a.reshape(1, d)
    tap = lambda a: a.reshape(a.shape[1:])
    smalls = [
        (b_ada, m_b_ada, v_b_ada, 0, 6 * d, "vec"),
        (norm_mix_g, m_norm_mix_g, v_norm_mix_g, 6 * d, 7 * d, "vec"),
        (tap(conv_short_w), tap(m_conv_short_w), tap(v_conv_short_w), 0, SHORT_K, "cg"),
        (tap(conv_conf_w), tap(m_conv_conf_w), tap(v_conv_conf_w), 8, 8 + CONF_K, "cg"),
        (conv_conf_b, m_conv_conf_b, v_conv_conf_b, 7 * d, 8 * d, "vec"),
        (conf_ln_g, m_conf_ln_g, v_conf_ln_g, 8 * d, 9 * d, "vec"),
        (conf_ln_b, m_conf_ln_b, v_conf_ln_b, 9 * d, 10 * d, "vec"),
        (norm_ffn_g, m_norm_ffn_g, v_norm_ffn_g, 10 * d, 11 * d, "vec"),
        (fin(final_norm_g), fin(m_final_norm_g), fin(v_final_norm_g), 11 * d, 12 * d, "vec"),
    ]
    vall, cgr = _small_exchange(vec, cg)
    up_small = _small_update(vall, cgr, smalls)
    n_ada = w_ada.shape[2]
    dmod_all = vall.reshape(N_DEV, 13 * d)[:, 0:6 * d]
    dmod_cols = lax.dynamic_slice(dmod_all, (0, me * n_ada), (N_DEV, n_ada))
    up_ada = _ada_update(sc_all3.reshape(N_DEV, d), dmod_cols, w_ada[0], m_w_ada[0], v_w_ada[0], _pick(d, 256))

    loss = jnp.sum(vall.reshape(N_DEV, 13 * d)[:, 12 * d:])

    lead = lambda a: a.reshape((1,) + a.shape)
    ups = [tuple(lead(a) for a in up_ada), up_small[0], up_small[1], tuple(lead(a) for a in up_in),
           tuple(lead(a) for a in up_small[2]), tuple(lead(a) for a in up_so), tuple(lead(a) for a in up_small[3]),
           up_small[4], up_small[5], up_small[6],
           tuple(lead(a) for a in up_co), tuple(lead(a) for a in up_oo), up_small[7],
           tuple(lead(a) for a in up_fi), tuple(lead(a) for a in up_fo),
           tuple(a.reshape(d) for a in up_small[8])]
    grads = [u[0] for u in ups]
    deltas = [u[1] for u in ups]
    new_m = [u[2] for u in ups]
    new_v = [u[3] for u in ups]
    return (loss, grad_x.reshape(1, t, d), *grads, *deltas, *new_m, *new_v)
```

```python
import functools

import jax
import jax.numpy as jnp
from jax import lax
from jax.experimental import pallas as pl
from jax.experimental.pallas import tpu as pltpu

F32 = jnp.float32
BF16 = jnp.bfloat16
MESH = pl.DeviceIdType.MESH

N_DEV = 8
EPS = 1e-6
LN_EPS = 1e-5
SHORT_K = 3
CONF_K = 31
ADAM_LR = 0.001
ADAM_B1 = 0.9
ADAM_B2 = 0.999
ADAM_EPS = 1e-08
ADAM_WD = 0.01
ADAM_STEP = 10

LANES = 128
SUBLANES = 8
CONV_ROWS = 64
HALO_SHORT = 8
HALO_CONF = 32
VMEM_LIMIT = 56 * 1024 * 1024

M_SH1, M_SC1, M_G1, M_SH2, M_SC2, M_G2 = range(6)
P_GMIX, P_GFFN, P_GFIN, P_CBIAS, P_LNG, P_LNB = range(6)


def _params(sem=None, **kw):
    return pltpu.CompilerParams(dimension_semantics=sem, vmem_limit_bytes=VMEM_LIMIT, **kw)


def _sigmoid(v):
    return jax.nn.sigmoid(v)


def _dot(a, b):
    return jnp.dot(a, b, preferred_element_type=F32)


def _dot_nt(a, b):
    return lax.dot_general(a, b, (((1,), (1,)), ((), ())), preferred_element_type=F32)


def _dot_tn(a, b):
    return lax.dot_general(a, b, (((0,), (0,)), ((), ())), preferred_element_type=F32)


def _colsum(v):
    return jnp.sum(v, axis=0, keepdims=True)


def _rowmean(v):
    return jnp.mean(v, axis=-1, keepdims=True)


def _my_coords():
    return lax.axis_index("x"), lax.axis_index("y"), lax.axis_index("c")


def _slot(px, py, pc):
    return 4 * px + 2 * py + pc


def _flip(v, bit):
    return 1 - v if bit else v


def _taps_by_residue(taps):
    by_res = {}
    for wi, off in taps:
        by_res.setdefault(off % SUBLANES, []).append((wi, off // SUBLANES))
    return sorted(by_res.items())


def _tap_conv(ext_ref, w_ref, taps, tm, extra, out_ref):
    d = out_ref.shape[1]
    rb = min(CONV_ROWS, tm)
    wrows = rb + extra
    groups = _taps_by_residue(taps)

    def block(i, carry):
        base = pl.multiple_of(i * rb, SUBLANES)
        for lc in range(d // LANES):
            ls = pl.ds(lc * LANES, LANES)
            win = ext_ref[pl.ds(base, wrows), ls]
            acc = None
            for r, lst in groups:
                sh = win if r == 0 else pltpu.roll(win, wrows - r, 0)
                for wi, q in lst:
                    term = w_ref[wi:wi + 1, ls] * sh[SUBLANES * q:SUBLANES * q + rb, :]
                    acc = term if acc is None else acc + term
            out_ref[pl.ds(base, rb), ls] = acc
        return carry

    lax.fori_loop(0, tm // rb, block, 0)


def _tap_wgrad(a_ref, ext_ref, taps, tm, extra, acc_ref):
    d = a_ref.shape[1]
    rb = min(CONV_ROWS, tm)
    wrows = rb + extra
    groups = _taps_by_residue(taps)

    def block(i, carry):
        base = pl.multiple_of(i * rb, SUBLANES)
        for lc in range(d // LANES):
            ls = pl.ds(lc * LANES, LANES)
            a_blk = a_ref[pl.ds(base, rb), ls]
            win = ext_ref[pl.ds(base, wrows), ls]
            for r, lst in groups:
                sh = win if r == 0 else pltpu.roll(win, wrows - r, 0)
                for wi, q in lst:
                    prod = a_blk * sh[SUBLANES * q:SUBLANES * q + rb, :]
                    part = prod[0:SUBLANES, :]
                    for s in range(1, rb // SUBLANES):
                        part = part + prod[SUBLANES * s:SUBLANES * (s + 1), :]
                    rows = pl.ds(SUBLANES * wi, SUBLANES)
                    acc_ref[rows, ls] = acc_ref[rows, ls] + part
        return carry

    lax.fori_loop(0, tm // rb, block, 0)


def _causal_taps(k, halo):
    return [(i, halo - (k - 1) + i) for i in range(k)]


def _anticausal_taps(k):
    return [(i, (k - 1) - i) for i in range(k)]


def _ada_scratch(d, nloc, trows):
    return [pltpu.VMEM((N_DEV, 1, d), F32), pltpu.VMEM((N_DEV, 1, nloc), F32), pltpu.VMEM((N_DEV, 1, nloc), F32),
            pltpu.VMEM((N_DEV, trows, LANES), F32), pltpu.SemaphoreType.DMA((21,)), pltpu.SemaphoreType.DMA((21,))]


def _ada_exchange(c_ref, w_ref, b_ref, t_ref, mod_ref, sc_ref, taps_ref,
                  scall_ref, part_ref, modrecv_ref, tapsall_ref, send_sems, recv_sems):
    x, y, cc = _my_coords()
    me = _slot(x, y, cc)
    d = c_ref.shape[1]
    cv = c_ref[...]
    scall_ref[me] = cv * _sigmoid(cv)
    tapsall_ref[me] = t_ref[...]

    def peer_of(k):
        return (_flip(x, k & 4), _flip(y, k & 2), _flip(cc, k & 1))

    def gather_copy(ref, base, k):
        return pltpu.make_async_remote_copy(
            src_ref=ref.at[me], dst_ref=ref.at[me], send_sem=send_sems.at[base + k - 1],
            recv_sem=recv_sems.at[base + k - 1], device_id=peer_of(k), device_id_type=MESH)

    first = [gather_copy(scall_ref, 0, k) for k in range(1, N_DEV)]
    first += [gather_copy(tapsall_ref, 7, k) for k in range(1, N_DEV)]
    for cp in first:
        cp.start()
    for cp in first[:7]:
        cp.wait_recv()
    sc_all = jnp.concatenate([scall_ref[s] for s in range(N_DEV)], axis=0)
    for s in range(N_DEV):
        sc_ref[s] = scall_ref[s]
    part = jnp.dot(sc_all, w_ref[...], preferred_element_type=F32,
                   precision=lax.Precision.HIGHEST)
    for b in range(N_DEV):
        part_ref[b] = part[b:b + 1, :]
    modrecv_ref[me] = part_ref[me]
    second = []
    for k in range(1, N_DEV):
        px, py, pc = peer_of(k)
        second.append(pltpu.make_async_remote_copy(
            src_ref=part_ref.at[_slot(px, py, pc)], dst_ref=modrecv_ref.at[me],
            send_sem=send_sems.at[14 + k - 1], recv_sem=recv_sems.at[14 + k - 1],
            device_id=(px, py, pc), device_id_type=MESH))
    for cp in second:
        cp.start()
    for cp in second:
        cp.wait_recv()
    mod = jnp.concatenate([modrecv_ref[s] for s in range(N_DEV)], axis=1) + b_ref[...]
    for r in range(6):
        mod_ref[r:r + 1, :] = mod[:, r * d:(r + 1) * d]
    mod_ref[6:8, :] = jnp.zeros((2, d), F32)
    for cp in first[7:]:
        cp.wait_recv()
    taps_ref[...] = jnp.concatenate([tapsall_ref[s] for s in range(N_DEV)], axis=1)
    for cp in first + second:
        cp.wait_send()


ANY_SPEC = pl.BlockSpec(memory_space=pl.ANY)


def _comm_scratch(na):
    return [pltpu.SemaphoreType.DMA((7 * na,)), pltpu.SemaphoreType.DMA((7 * na,)), pltpu.SemaphoreType.DMA((na,))]


def _leading_block(ref, slot):
    return ref.at[slot]


def _column_block(width):
    def view(ref, slot):
        return ref.at[:, pl.ds(pl.multiple_of(slot * width, LANES), width)]
    return view


class _Gather:
    def __init__(self, ins, outs, send_sems, recv_sems, local_sems, view=_leading_block):
        self.na = len(ins)
        x, y, c = _my_coords()
        self.c = c
        self.view = view
        self.me, self.sibling = (x, y, c), (x, y, 1 - c)
        self.chips = [(1 - x, y), (x, 1 - y), (1 - x, 1 - y)]
        self.outs, self.send_sems, self.recv_sems = outs, send_sems, recv_sems
        self.mine = [pltpu.make_async_copy(ins[a], view(outs[a], _slot(*self.me)), local_sems.at[a])
                     for a in range(self.na)]
        self.first = []
        for a in range(self.na):
            self.first.append(self._copy(a, 0, self.me, self.sibling, src=ins[a]))
            self.first += [self._copy(a, 1 + j, self.me, (*chip, c), src=ins[a]) for j, chip in enumerate(self.chips)]
        self.passed = [self._copy(a, 4 + j, (*chip, c), self.sibling)
                       for a in range(self.na) for j, chip in enumerate(self.chips)]

    def _copy(self, a, k, block, to, src=None):
        dst = self.view(self.outs[a], _slot(*block))
        return pltpu.make_async_remote_copy(
            src_ref=dst if src is None else src, dst_ref=dst,
            send_sem=self.send_sems.at[7 * a + k], recv_sem=self.recv_sems.at[7 * a + k],
            device_id=to, device_id_type=MESH)

    def start(self):
        self.start_near()
        self.start_far()

    def start_near(self):
        for cp in self.mine + [cp for n, cp in enumerate(self.first) if n % 4 != 3]:
            cp.start()

    def start_far(self):
        for cp in self.first[3::4]:
            cp.start()

    def forward_chip(self, j):
        for a in range(self.na):
            self._copy(a, 1 + j, (*self.chips[j], self.c), self.me).wait_recv()
            self.passed[3 * a + j].start()

    def wait_sibling_own(self):
        for a in range(self.na):
            self._copy(a, 0, self.sibling, self.me).wait_recv()

    def wait_sibling_passed(self, j):
        for a in range(self.na):
            self._copy(a, 4 + j, (*self.chips[j], 1 - self.c), self.me).wait_recv()

    def finish_sends(self):
        for cp in self.first + self.passed:
            cp.wait_send()
        for cp in self.mine:
            cp.wait()

    def forward(self):
        for j in range(3):
            self.forward_chip(j)

    def finish(self):
        self.wait_sibling_own()
        for j in range(3):
            self.wait_sibling_passed(j)
        self.finish_sends()


def _scatter_plan(ins, outs, send_sems, recv_sems, local_sems, view=_leading_block):
    na = len(ins)
    x, y, c = _my_coords()
    me = _slot(x, y, c)
    mine = [pltpu.make_async_copy(view(ins[a], me), outs[a].at[me], local_sems.at[a]) for a in range(na)]
    copies = []
    for k in range(1, N_DEV):
        peer = (_flip(x, k & 4), _flip(y, k & 2), _flip(c, k & 1))
        for a in range(na):
            copies.append(pltpu.make_async_remote_copy(
                src_ref=view(ins[a], _slot(*peer)), dst_ref=outs[a].at[me],
                send_sem=send_sems.at[7 * a + k - 1], recv_sem=recv_sems.at[7 * a + k - 1],
                device_id=peer, device_id_type=MESH))

    def start():
        for cp in mine + copies:
            cp.start()

    def finish():
        for cp in copies:
            cp.wait_recv()
        for cp in copies:
            cp.wait_send()
        for cp in mine:
            cp.wait()

    return start, finish


def _forward_in(me_arr, x, prm, w_in_loc, ada, tm, shards):
    t, d = x.shape
    n = w_in_loc.shape[1]
    ns = N_DEV
    na = len(shards)
    nt = t // tm
    nloc = ada[1].shape[1]
    trows = ada[3].shape[0]

    def body(*refs):
        x_ref, prm_ref, wloc_ref = refs[1:4]
        ada_in = refs[4:8]
        shard_refs = refs[8:8 + na]
        proj_ref, h_ref, wg_ref, mod_ref, sc_ref, taps_ref = refs[8 + na:14 + na]
        gathered_refs = refs[14 + na:14 + 2 * na]
        hall_ref, wv_ref, wv_sem = refs[14 + 2 * na:17 + 2 * na]
        ada_scratch = refs[17 + 2 * na:23 + 2 * na]
        sems = refs[23 + 2 * na:]
        s, i = pl.program_id(0), pl.program_id(1)
        columns = _column_block(n)

        def gathers():
            return (_Gather([wloc_ref], [wg_ref], *sems[0:3], view=columns),
                    _Gather(shard_refs, gathered_refs, *sems[3:6]))

        def shard_copy(src, step):
            return pltpu.make_async_copy(src, wv_ref.at[step % 2], wv_sem.at[step % 2])

        @pl.when(i == 0)
        def _():
            me = _slot(*_my_coords())

            @pl.when(s == 0)
            def _():
                g_in, _ = gathers()
                _ada_exchange(*ada_in, mod_ref, sc_ref, taps_ref, *ada_scratch)
                g_in.start_near()
                own = shard_copy(wloc_ref, 0)
                own.start()
                own.wait()

            for step in range(1, ns):
                @pl.when(s == step)
                def _(step=step):
                    shard_copy(columns(wg_ref, me ^ step), step).wait()

        @pl.when(i == nt - 1)
        def _():
            g_in, g_rest = gathers()
            me = _slot(*_my_coords())
            arrivals = {1: g_in.wait_sibling_own,
                        2: functools.partial(g_in.forward_chip, 1), 3: functools.partial(g_in.wait_sibling_passed, 1),
                        4: functools.partial(g_in.forward_chip, 0), 5: functools.partial(g_in.wait_sibling_passed, 0),
                        6: functools.partial(g_in.forward_chip, 2), 7: functools.partial(g_in.wait_sibling_passed, 2)}
            for step, arrive in arrivals.items():
                @pl.when(s == step - 1)
                def _(step=step, arrive=arrive):
                    arrive()
                    if step == 2:
                        g_in.start_far()
                    if step == 4:
                        g_rest.start()
                    if step == 7:
                        g_rest.forward()
                    shard_copy(columns(wg_ref, me ^ step), step).start()

        @pl.when(s == 0)
        def _():
            xv = x_ref[...]
            r = lax.rsqrt(_rowmean(xv * xv) + EPS)
            h = xv * r * prm_ref[P_GMIX:P_GMIX + 1, :] * (1.0 + mod_ref[M_SC1:M_SC1 + 1, :]) \
                + mod_ref[M_SH1:M_SH1 + 1, :]
            hb = h.astype(BF16)
            hall_ref[i] = hb
            h_ref[...] = hb

        proj_ref[...] = _dot(hall_ref[i], wv_ref[s % 2]).astype(BF16)

        @pl.when((s == ns - 1) & (i == nt - 1))
        def _():
            g_in, g_rest = gathers()
            g_in.finish_sends()
            g_rest.finish()

    x_tile = pl.BlockSpec((tm, d), lambda s, i, me: (jnp.where(s == 0, i, nt - 1), 0))
    whole = lambda a: pl.BlockSpec(a.shape, lambda s, i, me: (0,) * len(a.shape))
    small_out = (jax.ShapeDtypeStruct((8, d), F32), jax.ShapeDtypeStruct((N_DEV, 1, d), F32),
                 jax.ShapeDtypeStruct((trows, N_DEV * LANES), F32))
    res = pl.pallas_call(
        body, name="forward_in",
        grid_spec=pltpu.PrefetchScalarGridSpec(
            num_scalar_prefetch=1, grid=(ns, nt),
            in_specs=[x_tile, whole(prm), ANY_SPEC] + [whole(a) for a in ada] + [ANY_SPEC] * na,
            out_specs=[pl.BlockSpec((tm, n), lambda s, i, me: (i, me[0] ^ s)), x_tile, ANY_SPEC]
            + [whole(a) for a in small_out] + [ANY_SPEC] * na,
            scratch_shapes=[pltpu.VMEM((nt, tm, d), BF16), pltpu.VMEM((2, d, n), BF16), pltpu.SemaphoreType.DMA((2,))]
            + _ada_scratch(d, nloc, trows) + _comm_scratch(1) + _comm_scratch(na)),
        out_shape=(jax.ShapeDtypeStruct((t, ns * n), BF16), jax.ShapeDtypeStruct((t, d), BF16),
                   jax.ShapeDtypeStruct((d, ns * n), BF16)) + small_out
        + tuple(jax.ShapeDtypeStruct((N_DEV,) + a.shape, a.dtype) for a in shards),
        compiler_params=_params(("arbitrary", "arbitrary")),
    )(me_arr, x, prm, w_in_loc, *ada, *shards)
    return res[0], res[1], res[2], res[3], res[4], res[5], res[6:]


def _forward_mix(x, proj, mod, prm, taps, w_so, w_co, w_o, tm, shards):
    t, d = x.shape
    nt = t // tm
    na = len(shards)
    hs, hc = HALO_SHORT, HALO_CONF
    r3, r31 = 0, 8

    def body(*refs):
        x_ref, proj_ref, mod_ref, prm_ref, taps_ref, wso_ref, wco_ref, wo_ref = refs[:8]
        x1_ref, ya_ref, yb_ref, mix_ref, u1_ref, c3_ref, yapre_ref, u3_ref, merged_ref = refs[8 + na:17 + na]
        cv_ext, u0_ext, conv3_ref, u1f_ref = refs[17 + 2 * na:21 + 2 * na]
        gather = _Gather(refs[8:8 + na], refs[17 + na:17 + 2 * na], *refs[21 + 2 * na:])
        i = pl.program_id(0)

        @pl.when(i == 0)
        def _():
            gather.start()
            cv_ext[0:hs, :] = jnp.zeros((hs, d), F32)
            u0_ext[0:hc, :] = jnp.zeros((hc, d), F32)

        @pl.when(i == (3 * nt) // 4)
        def _():
            gather.forward()

        def col(g):
            return proj_ref[:, g * d:(g + 1) * d].astype(F32)

        cv_ext[hs:hs + tm, :] = col(1) * col(2)
        u0_ext[hc:hc + tm, :] = col(3) * _sigmoid(col(4))
        _tap_conv(cv_ext, taps_ref, [(r3 + wi, off) for wi, off in _causal_taps(SHORT_K, hs)], tm, hs, conv3_ref)
        _tap_conv(u0_ext, taps_ref, [(r31 + wi, off) for wi, off in _causal_taps(CONF_K, hc)], tm, hc, u1f_ref)
        cv_ext[0:hs, :] = cv_ext[tm:tm + hs, :]
        u0_ext[0:hc, :] = u0_ext[tm:tm + hc, :]

        c3_ref[...] = conv3_ref[...].astype(BF16)
        ya_pre = (col(0) * c3_ref[...].astype(F32)).astype(BF16)
        yapre_ref[...] = ya_pre
        y_a = _dot(ya_pre, wso_ref[...])
        u1 = u1f_ref[...] + prm_ref[P_CBIAS:P_CBIAS + 1, :]
        u1_ref[...] = u1.astype(BF16)
        u1 = u1_ref[...].astype(F32)
        mu = _rowmean(u1)
        uc = u1 - mu
        rstd = lax.rsqrt(_rowmean(uc * uc) + LN_EPS)
        u2 = uc * rstd * prm_ref[P_LNG:P_LNG + 1, :] + prm_ref[P_LNB:P_LNB + 1, :]
        u3 = (u2 * _sigmoid(u2)).astype(BF16)
        u3_ref[...] = u3
        y_b = _dot(u3, wco_ref[...])
        ya_ref[...] = y_a.astype(BF16)
        yb_ref[...] = y_b.astype(BF16)
        merged = (_sigmoid(col(5)) * ya_ref[...].astype(F32) + _sigmoid(col(6)) * yb_ref[...].astype(F32)).astype(BF16)
        merged_ref[...] = merged
        mix = _dot(merged, wo_ref[...])
        mix_ref[...] = mix.astype(BF16)
        x1_ref[...] = x_ref[...] + mod_ref[M_G1:M_G1 + 1, :] * mix

        @pl.when(i == nt - 1)
        def _():
            gather.finish()

    tile = pl.BlockSpec((tm, d), lambda i: (i, 0))
    whole = lambda shape: pl.BlockSpec(shape, lambda i: (0,) * len(shape))
    res = pl.pallas_call(
        body, name="forward_mix", grid=(nt,),
        out_shape=(jax.ShapeDtypeStruct((t, d), F32),) + (jax.ShapeDtypeStruct((t, d), BF16),) * 8
        + tuple(jax.ShapeDtypeStruct((N_DEV,) + a.shape, a.dtype) for a in shards),
        in_specs=[tile, pl.BlockSpec((tm, 7 * d), lambda i: (i, 0)), whole((8, d)), whole((8, d)),
                  whole(taps.shape), whole((d, d)), whole((d, d)), whole((d, d))] + [ANY_SPEC] * na,
        out_specs=[tile] * 9 + [ANY_SPEC] * na,
        scratch_shapes=[pltpu.VMEM((hs + tm, d), F32), pltpu.VMEM((hc + tm, d), F32),
                        pltpu.VMEM((tm, d), F32), pltpu.VMEM((tm, d), F32)] + _comm_scratch(na),
        compiler_params=_params(("arbitrary",)),
    )(x, proj, mod, prm, taps, w_so, w_co, w_o, *shards)
    return res[:9], res[9:]


def _ffn_chunks(ff):
    mxu = 2 * LANES
    cut = (ff // mxu + 1) // 2 * mxu
    return [(0, cut), (cut, ff)] if 0 < cut < ff and ff % mxu == 0 else [(0, ff)]


def _forward_ffn(x1, tgt, mod, prm, w_fi_t, w_fo, tm):
    t, d = x1.shape
    ff = w_fo.shape[0]

    def body(x1_ref, tgt_ref, mod_ref, prm_ref, wfi_hbm, wfo_hbm,
             dx2_ref, ab_ref, h2_ref, sums_ref, wfi_ref, wfo_ref):
        i = pl.program_id(0)

        @pl.when(i == 0)
        def _():
            pltpu.sync_copy(wfi_hbm, wfi_ref)
            pltpu.sync_copy(wfo_hbm, wfo_ref)
            sums_ref[...] = jnp.zeros((8, d), F32)

        x1v = x1_ref[...]
        r2 = lax.rsqrt(_rowmean(x1v * x1v) + EPS)
        h2 = (x1v * r2 * prm_ref[P_GFFN:P_GFFN + 1, :] * (1.0 + mod_ref[M_SC2:M_SC2 + 1, :])
              + mod_ref[M_SH2:M_SH2 + 1, :]).astype(BF16)
        h2_ref[...] = h2
        f = jnp.zeros((tm, d), F32)
        for c0, c1 in _ffn_chunks(ff):
            ab_ref[:, c0:c1] = _dot_nt(h2, wfi_ref[c0:c1, :]).astype(BF16)
            ab_ref[:, ff + c0:ff + c1] = _dot_nt(h2, wfi_ref[ff + c0:ff + c1, :]).astype(BF16)
            a = ab_ref[:, c0:c1].astype(F32)
            act = (a * _sigmoid(a) * ab_ref[:, ff + c0:ff + c1].astype(F32)).astype(BF16)
            f = f + _dot(act, wfo_ref[c0:c1, :])
        x2 = x1v + mod_ref[M_G2:M_G2 + 1, :] * f
        r3 = lax.rsqrt(_rowmean(x2 * x2) + EPS)
        xn3 = x2 * r3
        gfin = prm_ref[P_GFIN:P_GFIN + 1, :]
        err = xn3 * gfin - tgt_ref[...]
        dy = err * (1.0 / d)
        dxn3 = dy * gfin
        dx2 = r3 * (dxn3 - xn3 * _rowmean(dxn3 * xn3))
        dx2_ref[...] = dx2
        sums_ref[0:1, :] = sums_ref[0:1, :] + _colsum(dy * xn3)
        sums_ref[1:2, :] = sums_ref[1:2, :] + _colsum(dx2 * f)
        sums_ref[2:3, :] = sums_ref[2:3, :] + _colsum(err * err) * (0.5 / d)

    tile = pl.BlockSpec((tm, d), lambda i: (i, 0))
    whole = lambda shape: pl.BlockSpec(shape, lambda i: (0,) * len(shape))
    return pl.pallas_call(
        body, name="forward_ffn", grid=(t // tm,),
        out_shape=(jax.ShapeDtypeStruct((t, d), F32), jax.ShapeDtypeStruct((t, 2 * ff), BF16),
                   jax.ShapeDtypeStruct((t, d), BF16), jax.ShapeDtypeStruct((8, d), F32)),
        in_specs=[tile, tile, whole((8, d)), whole((8, d)), ANY_SPEC, ANY_SPEC],
        out_specs=[tile, pl.BlockSpec((tm, 2 * ff), lambda i: (i, 0)), tile, whole((8, d))],
        scratch_shapes=[pltpu.VMEM(w_fi_t.shape, BF16), pltpu.VMEM(w_fo.shape, BF16)],
        compiler_params=_params(("arbitrary",)),
    )(x1, tgt, mod, prm, w_fi_t, w_fo)


def _backward_ffn(dx2, x1, ab, mod, prm, w_fi_t, w_fo, tm):
    t, d = x1.shape
    ff = w_fo.shape[0]

    def body(dx2_ref, x1_ref, ab_ref, mod_ref, prm_ref, wfi_hbm, wfo_hbm,
             dx1_ref, df_ref, act_ref, dab_ref, sums_ref, wfi_ref, wfo_ref):
        i = pl.program_id(0)

        @pl.when(i == 0)
        def _():
            pltpu.sync_copy(wfi_hbm, wfi_ref)
            pltpu.sync_copy(wfo_hbm, wfo_ref)
            sums_ref[...] = jnp.zeros((8, d), F32)

        dx2v = dx2_ref[...]
        df = (mod_ref[M_G2:M_G2 + 1, :] * dx2v).astype(BF16)
        df_ref[...] = df
        dh2 = jnp.zeros((tm, d), F32)
        for c0, c1 in _ffn_chunks(ff):
            dact = _dot_nt(df, wfo_ref[c0:c1, :])
            a = ab_ref[:, c0:c1].astype(F32)
            b = ab_ref[:, ff + c0:ff + c1].astype(F32)
            s = _sigmoid(a)
            sil = a * s
            act_ref[:, c0:c1] = (sil * b).astype(BF16)
            da = (dact * b * (s * (1.0 + a * (1.0 - s)))).astype(BF16)
            db = (dact * sil).astype(BF16)
            dab_ref[:, c0:c1] = da
            dab_ref[:, ff + c0:ff + c1] = db
            dh2 = dh2 + _dot(da, wfi_ref[c0:c1, :]) + _dot(db, wfi_ref[ff + c0:ff + c1, :])
        x1v = x1_ref[...]
        r2 = lax.rsqrt(_rowmean(x1v * x1v) + EPS)
        xn2 = x1v * r2
        gffn = prm_ref[P_GFFN:P_GFFN + 1, :]
        scale = 1.0 + mod_ref[M_SC2:M_SC2 + 1, :]
        dxn2 = dh2 * gffn * scale
        dx1_ref[...] = dx2v + r2 * (dxn2 - xn2 * _rowmean(dxn2 * xn2))
        hx = dh2 * xn2
        sums_ref[0:1, :] = sums_ref[0:1, :] + _colsum(dh2)
        sums_ref[1:2, :] = sums_ref[1:2, :] + _colsum(hx) * gffn
        sums_ref[2:3, :] = sums_ref[2:3, :] + _colsum(hx) * scale

    tile = pl.BlockSpec((tm, d), lambda i: (i, 0))
    whole = lambda shape: pl.BlockSpec(shape, lambda i: (0,) * len(shape))
    wide = lambda n: pl.BlockSpec((tm, n), lambda i: (i, 0))
    return pl.pallas_call(
        body, name="backward_ffn", grid=(t // tm,),
        out_shape=(jax.ShapeDtypeStruct((t, d), F32), jax.ShapeDtypeStruct((t, d), BF16),
                   jax.ShapeDtypeStruct((t, ff), BF16), jax.ShapeDtypeStruct((t, 2 * ff), BF16),
                   jax.ShapeDtypeStruct((8, d), F32)),
        in_specs=[tile, tile, wide(2 * ff), whole((8, d)), whole((8, d)), ANY_SPEC, ANY_SPEC],
        out_specs=[tile, tile, wide(ff), wide(2 * ff), whole((8, d))],
        scratch_shapes=[pltpu.VMEM(w_fi_t.shape, BF16), pltpu.VMEM(w_fo.shape, BF16)],
        compiler_params=_params(("arbitrary",)),
    )(dx2, x1, ab, mod, prm, w_fi_t, w_fo)


def _backward_mix(dx1, proj, y_a, y_b, mix, u1, conv3, mod, prm, taps, w_so, w_co, w_o, tm, partials):
    t, d = dx1.shape
    nt = t // tm
    na = len(partials)
    hs, hc = HALO_SHORT, HALO_CONF
    r3, r31 = 0, 8

    def body(*refs):
        (dx1_ref, proj_ref, halo_ref, ya_ref, yb_ref, mix_ref, u1_ref, c3_ref, mod_ref, prm_ref, taps_ref,
         wso_ref, wco_ref, wo_ref) = refs[:14]
        dproj_ref, dmix_ref, dya_ref, dyb_ref, sums_ref, dw3_ref, dw31_ref = refs[14 + na:21 + na]
        cv_ext, u0_ext, d3_ext, du1_ext, tmp_ref = refs[21 + 2 * na:26 + 2 * na]
        scatter_start, scatter_finish = _scatter_plan(refs[14:14 + na], refs[21 + na:21 + 2 * na], *refs[26 + 2 * na:])
        i = pl.program_id(0)
        first_tile = i == nt - 1

        @pl.when(i == 0)
        def _():
            scatter_start()
            sums_ref[...] = jnp.zeros((8, d), F32)
            dw3_ref[...] = jnp.zeros(dw3_ref.shape, F32)
            dw31_ref[...] = jnp.zeros(dw31_ref.shape, F32)
            d3_ext[tm:tm + hs, :] = jnp.zeros((hs, d), F32)
            du1_ext[tm:tm + hc, :] = jnp.zeros((hc, d), F32)

        def col(g):
            return proj_ref[:, g * d:(g + 1) * d].astype(F32)

        def hcol(g, rows):
            v = halo_ref[HALO_CONF - rows:HALO_CONF, g * d:(g + 1) * d].astype(F32)
            return jnp.where(first_tile, 0.0, v)

        dx1v = dx1_ref[...]
        mixv = mix_ref[...].astype(F32)
        dmix = (mod_ref[M_G1:M_G1 + 1, :] * dx1v).astype(BF16)
        dmix_ref[...] = dmix
        sums_ref[0:1, :] = sums_ref[0:1, :] + _colsum(dx1v * mixv)
        dmerged = _dot_nt(dmix, wo_ref[...])
        ga = _sigmoid(col(5))
        gb = _sigmoid(col(6))
        yav = ya_ref[...].astype(F32)
        ybv = yb_ref[...].astype(F32)
        dya_f = dmerged * ga
        dyb_f = dmerged * gb
        dya = dya_f.astype(BF16)
        dyb = dyb_f.astype(BF16)
        dya_ref[...] = dya
        dyb_ref[...] = dyb
        dproj_ref[:, 5 * d:6 * d] = (dya_f * yav * (1.0 - ga)).astype(BF16)
        dproj_ref[:, 6 * d:7 * d] = (dyb_f * ybv * (1.0 - gb)).astype(BF16)

        dya_pre = _dot_nt(dya, wso_ref[...])
        c_s, v_s, b_s = col(1), col(2), col(0)
        cv_ext[0:hs, :] = hcol(1, hs) * hcol(2, hs)
        cv_ext[hs:hs + tm, :] = c_s * v_s
        dproj_ref[:, 0:d] = (dya_pre * c3_ref[...].astype(F32)).astype(BF16)
        d3_ext[0:tm, :] = dya_pre * b_s
        _tap_wgrad(d3_ext, cv_ext, _causal_taps(SHORT_K, hs), tm, hs, dw3_ref)
        _tap_conv(d3_ext, taps_ref, [(r3 + wi, off) for wi, off in _anticausal_taps(SHORT_K)], tm, hs, tmp_ref)
        dcv = tmp_ref[...]
        dproj_ref[:, d:2 * d] = (dcv * v_s).astype(BF16)
        dproj_ref[:, 2 * d:3 * d] = (dcv * c_s).astype(BF16)
        d3_ext[tm:tm + hs, :] = d3_ext[0:hs, :]

        du3 = _dot_nt(dyb, wco_ref[...])
        u1v = u1_ref[...].astype(F32)
        mu = _rowmean(u1v)
        uc = u1v - mu
        rstd = lax.rsqrt(_rowmean(uc * uc) + LN_EPS)
        uhat = uc * rstd
        lng = prm_ref[P_LNG:P_LNG + 1, :]
        u2 = uhat * lng + prm_ref[P_LNB:P_LNB + 1, :]
        s2 = _sigmoid(u2)
        du2 = du3 * (s2 * (1.0 + u2 * (1.0 - s2)))
        sums_ref[1:2, :] = sums_ref[1:2, :] + _colsum(du2 * uhat)
        sums_ref[2:3, :] = sums_ref[2:3, :] + _colsum(du2)
        duhat = du2 * lng
        du1 = rstd * (duhat - _rowmean(duhat) - uhat * _rowmean(duhat * uhat))
        sums_ref[3:4, :] = sums_ref[3:4, :] + _colsum(du1)
        du1_ext[0:tm, :] = du1
        v_c = col(3)
        sg = _sigmoid(col(4))
        u0_ext[0:hc, :] = hcol(3, hc) * _sigmoid(hcol(4, hc))
        u0_ext[hc:hc + tm, :] = v_c * sg
        _tap_wgrad(du1_ext, u0_ext, _causal_taps(CONF_K, hc), tm, hc, dw31_ref)
        _tap_conv(du1_ext, taps_ref, [(r31 + wi, off) for wi, off in _anticausal_taps(CONF_K)], tm, hc, tmp_ref)
        dv_c = tmp_ref[...] * sg
        dproj_ref[:, 3 * d:4 * d] = dv_c.astype(BF16)
        dproj_ref[:, 4 * d:5 * d] = (dv_c * v_c * (1.0 - sg)).astype(BF16)
        du1_ext[tm:tm + hc, :] = du1_ext[0:hc, :]

        @pl.when(i == nt - 1)
        def _():
            scatter_finish()

    rev = lambda i: (nt - 1 - i, 0)
    tile = pl.BlockSpec((tm, d), rev)
    whole = lambda shape: pl.BlockSpec(shape, lambda i: (0,) * len(shape))
    hblocks = tm // HALO_CONF
    halo = pl.BlockSpec((HALO_CONF, 7 * d), lambda i: (jnp.maximum((nt - 1 - i) * hblocks - 1, 0), 0))
    bf = jax.ShapeDtypeStruct((t, d), BF16)
    res = pl.pallas_call(
        body, name="backward_mix", grid=(nt,),
        out_shape=(jax.ShapeDtypeStruct((t, 7 * d), BF16), bf, bf, bf,
                   jax.ShapeDtypeStruct((8, d), F32),
                   jax.ShapeDtypeStruct((SUBLANES * SHORT_K, d), F32),
                   jax.ShapeDtypeStruct((SUBLANES * CONF_K, d), F32))
        + tuple(jax.ShapeDtypeStruct(p.shape, p.dtype) for p in partials),
        in_specs=[tile, pl.BlockSpec((tm, 7 * d), rev), halo, tile, tile, tile, tile, tile,
                  whole((8, d)), whole((8, d)), whole(taps.shape), whole((d, d)), whole((d, d)), whole((d, d))]
        + [ANY_SPEC] * na,
        out_specs=[pl.BlockSpec((tm, 7 * d), rev), tile, tile, tile,
                   whole((8, d)), whole((SUBLANES * SHORT_K, d)), whole((SUBLANES * CONF_K, d))] + [ANY_SPEC] * na,
        scratch_shapes=[pltpu.VMEM((hs + tm, d), F32), pltpu.VMEM((hc + tm, d), F32),
                        pltpu.VMEM((tm + hs, d), F32), pltpu.VMEM((tm + hc, d), F32),
                        pltpu.VMEM((tm, d), F32)] + _comm_scratch(na),
        compiler_params=_params(("arbitrary",)),
    )(dx1, proj, proj, y_a, y_b, mix, u1, conv3, mod, prm, taps, w_so, w_co, w_o, *partials)
    return res[:7], res[7:]


def _backward_in(dproj, x, dx1, mod, prm, w_in_g, tm, partials, partial_view, exchanged_shapes):
    t, d = x.shape
    n_all = w_in_g.shape[1]
    na = len(partials)
    nt = t // tm

    def body(*refs):
        dproj_ref, x_ref, dx1_ref, mod_ref, prm_ref, w_hbm = refs[:6]
        gx_ref, sums_ref = refs[6 + na:8 + na]
        w_ref = refs[8 + 2 * na]
        scatter_start, scatter_finish = _scatter_plan(refs[6:6 + na], refs[8 + na:8 + 2 * na], *refs[9 + 2 * na:],
                                                      view=partial_view)

        @pl.when(pl.program_id(0) == 0)
        def _():
            scatter_start()
            pltpu.sync_copy(w_hbm, w_ref)
            sums_ref[...] = jnp.zeros((8, d), F32)

        dh = _dot_nt(dproj_ref[...], w_ref[...])
        xv = x_ref[...]
        r1 = lax.rsqrt(_rowmean(xv * xv) + EPS)
        xn = xv * r1
        gmix = prm_ref[P_GMIX:P_GMIX + 1, :]
        scale = 1.0 + mod_ref[M_SC1:M_SC1 + 1, :]
        dxn = dh * gmix * scale
        gx_ref[...] = dx1_ref[...] + r1 * (dxn - xn * _rowmean(dxn * xn))
        hx = dh * xn
        sums_ref[0:1, :] = sums_ref[0:1, :] + _colsum(dh)
        sums_ref[1:2, :] = sums_ref[1:2, :] + _colsum(hx) * gmix
        sums_ref[2:3, :] = sums_ref[2:3, :] + _colsum(hx) * scale

        @pl.when(pl.program_id(0) == nt - 1)
        def _():
            scatter_finish()

    tile = pl.BlockSpec((tm, d), lambda i: (i, 0))
    whole = pl.BlockSpec((8, d), lambda i: (0, 0))
    res = pl.pallas_call(
        body, name="backward_in", grid=(nt,),
        out_shape=(jax.ShapeDtypeStruct((t, d), F32), jax.ShapeDtypeStruct((8, d), F32))
        + tuple(jax.ShapeDtypeStruct(s, p.dtype) for p, s in zip(partials, exchanged_shapes)),
        in_specs=[pl.BlockSpec((tm, n_all), lambda i: (i, 0)), tile, tile, whole, whole, ANY_SPEC] + [ANY_SPEC] * na,
        out_specs=[tile, whole] + [ANY_SPEC] * na,
        scratch_shapes=[pltpu.VMEM(w_in_g.shape, BF16)] + _comm_scratch(na),
        compiler_params=_params(("arbitrary",)),
    )(dproj, x, dx1, mod, prm, w_in_g, *partials)
    return res[0], res[1], res[2:]


def _weight_grad(a, b, a_spec, b_spec, ns, m, n, nk, name, partials=(), by_columns=False):
    na = len(partials)

    def body(*refs):
        a_ref, b_ref = refs[:2]
        o_ref = refs[2 + na]
        acc_ref = refs[3 + 2 * na]
        s, k = pl.program_id(0), pl.program_id(1)
        if na:
            scatter_start, scatter_finish = _scatter_plan(refs[2:2 + na], refs[3 + na:3 + 2 * na], *refs[4 + 2 * na:])

            @pl.when((s == 0) & (k == 0))
            def _():
                scatter_start()

        av = a_ref[0] if len(a_ref.shape) == 3 else a_ref[...]
        bv = b_ref[0] if len(b_ref.shape) == 3 else b_ref[...]
        part = _dot_tn(av, bv)

        @pl.when(k == 0)
        def _():
            acc_ref[...] = part

        @pl.when(k > 0)
        def _():
            acc_ref[...] = acc_ref[...] + part

        @pl.when(k == nk - 1)
        def _():
            if by_columns:
                o_ref[...] = acc_ref[...].astype(BF16)
            else:
                o_ref[0] = acc_ref[...].astype(BF16)

        if na:
            @pl.when((s == ns - 1) & (k == nk - 1))
            def _():
                scatter_finish()

    if by_columns:
        out_shape, out_spec = (m, ns * n), pl.BlockSpec((m, n), lambda s, k: (0, s))
    else:
        out_shape, out_spec = (ns, m, n), pl.BlockSpec((1, m, n), lambda s, k: (s, 0, 0))
    res = pl.pallas_call(
        body, name=name, grid=(ns, nk),
        out_shape=(jax.ShapeDtypeStruct(out_shape, BF16),)
        + tuple(jax.ShapeDtypeStruct(p.shape, p.dtype) for p in partials),
        in_specs=[a_spec, b_spec] + [ANY_SPEC] * na,
        out_specs=[out_spec] + [ANY_SPEC] * na,
        scratch_shapes=[pltpu.VMEM((m, n), F32)] + (_comm_scratch(na) if na else []),
        compiler_params=_params(("arbitrary", "arbitrary")),
    )(a, b, *partials)
    return (res[0], res[1:]) if na else res[0]


def _adamw(w, g, m, v):
    m = ADAM_B1 * m + (1.0 - ADAM_B1) * g
    v = ADAM_B2 * v + (1.0 - ADAM_B2) * (g * g)
    m_hat = m / (1.0 - ADAM_B1 ** ADAM_STEP)
    v_hat = v / (1.0 - ADAM_B2 ** ADAM_STEP)
    delta = -ADAM_LR * (m_hat / (jnp.sqrt(v_hat) + ADAM_EPS) + ADAM_WD * w)
    return delta, m, v


def _adamw_shard(parts, w, m, v, tr, name):
    r, c = w.shape

    def body(p_ref, w_ref, m_ref, v_ref, g_ref, d_ref, nm_ref, nv_ref):
        g = p_ref[0].astype(F32)
        for s in range(1, N_DEV):
            g = g + p_ref[s].astype(F32)
        delta, nm, nv = _adamw(w_ref[...], g, m_ref[...], v_ref[...])
        g_ref[...] = g
        d_ref[...] = delta
        nm_ref[...] = nm
        nv_ref[...] = nv

    tile = pl.BlockSpec((tr, c), lambda i: (i, 0))
    return pl.pallas_call(
        body, name=name, grid=(r // tr,),
        out_shape=(jax.ShapeDtypeStruct((r, c), F32),) * 4,
        in_specs=[pl.BlockSpec((N_DEV, tr, c), lambda i: (0, i, 0)), tile, tile, tile],
        out_specs=[tile] * 4,
        compiler_params=_params(("arbitrary",)),
    )(parts, w, m, v)


def _ada_update(sc_all, dmod_cols, w, m, v, tr):
    d, n = w.shape

    def body(sc_ref, dm_ref, w_ref, m_ref, v_ref, g_ref, d_ref, nm_ref, nv_ref):
        g = lax.dot_general(sc_ref[...], dm_ref[...], (((0,), (0,)), ((), ())),
                            preferred_element_type=F32, precision=lax.Precision.HIGHEST)
        delta, nm, nv = _adamw(w_ref[...], g, m_ref[...], v_ref[...])
        g_ref[...] = g
        d_ref[...] = delta
        nm_ref[...] = nm
        nv_ref[...] = nv

    tile = pl.BlockSpec((tr, n), lambda i: (i, 0))
    return pl.pallas_call(
        body, name="ada_update", grid=(d // tr,),
        out_shape=(jax.ShapeDtypeStruct((d, n), F32),) * 4,
        in_specs=[pl.BlockSpec((N_DEV, tr), lambda i: (0, i)), pl.BlockSpec((N_DEV, n), lambda i: (0, 0)),
                  tile, tile, tile],
        out_specs=[tile] * 4,
        compiler_params=_params(("arbitrary",)),
    )(sc_all, dmod_cols, w, m, v)


def _small_exchange(vec, cg):
    l = vec.shape[2]
    rows = cg.shape[1]

    def body(vec_ref, cg_ref, vall_ref, cgr_ref, send_sems, recv_sems):
        x, y, c = _my_coords()
        me = _slot(x, y, c)
        vall_ref[me] = vec_ref[0]
        cgr_ref[me] = cg_ref[me]
        copies = []
        for k in range(1, N_DEV):
            peer = (_flip(x, k & 4), _flip(y, k & 2), _flip(c, k & 1))
            copies.append(pltpu.make_async_remote_copy(
                src_ref=vall_ref.at[me], dst_ref=vall_ref.at[me], send_sem=send_sems.at[k - 1],
                recv_sem=recv_sems.at[k - 1], device_id=peer, device_id_type=MESH))
            copies.append(pltpu.make_async_remote_copy(
                src_ref=cg_ref.at[_slot(*peer)], dst_ref=cgr_ref.at[me], send_sem=send_sems.at[7 + k - 1],
                recv_sem=recv_sems.at[7 + k - 1], device_id=peer, device_id_type=MESH))
        for cp in copies:
            cp.start()
        for cp in copies:
            cp.wait_recv()
        for cp in copies:
            cp.wait_send()

    vm = pl.BlockSpec(memory_space=pltpu.VMEM)
    return pl.pallas_call(
        body, name="small_exchange",
        out_shape=(jax.ShapeDtypeStruct((N_DEV, 1, l), F32), jax.ShapeDtypeStruct((N_DEV, rows, LANES), F32)),
        in_specs=[vm, vm], out_specs=[vm, vm],
        scratch_shapes=[pltpu.SemaphoreType.DMA((14,)), pltpu.SemaphoreType.DMA((14,))],
        compiler_params=_params(),
    )(vec, cg)


def _small_update(vall, cgr, smalls):
    ns = len(smalls)

    def body(*refs):
        vall_ref, cgr_ref = refs[0], refs[1]
        wmv = refs[2:2 + 3 * ns]
        outs = refs[2 + 3 * ns:]
        for p, (_, _, _, lo, hi, kind) in enumerate(smalls):
            w_ref, m_ref, v_ref = wmv[3 * p:3 * p + 3]
            part = (lambda s: vall_ref[s, :, lo:hi]) if kind == "vec" else (lambda s: cgr_ref[s, lo:hi, :])
            g = part(0)
            for s in range(1, N_DEV):
                g = g + part(s)
            delta, nm, nv = _adamw(w_ref[...], g, m_ref[...], v_ref[...])
            for o_ref, val in zip(outs[4 * p:4 * p + 4], (g, delta, nm, nv)):
                o_ref[...] = val

    vm = pl.BlockSpec(memory_space=pltpu.VMEM)
    args = [vall, cgr]
    out_shape = []
    for w, m, v, _, _, _ in smalls:
        args += [w, m, v]
        out_shape += [jax.ShapeDtypeStruct(w.shape, F32)] * 4
    res = pl.pallas_call(
        body, name="small_update",
        out_shape=tuple(out_shape),
        in_specs=[vm] * len(args), out_specs=[vm] * len(out_shape),
        compiler_params=_params(),
    )(*args)
    return [res[4 * p:4 + 4 * p] for p in range(ns)]


def _pick(t, want):
    return want if t % want == 0 else t


def kernel(x, c, w_ada, b_ada, norm_mix_g, w_in, conv_short_w, w_short_out, conv_conf_w, conv_conf_b, conf_ln_g, conf_ln_b, w_conf_out, w_o, norm_ffn_g, w_ffn_in, w_ffn_out, final_norm_g, loss_target, m_w_ada, m_b_ada, m_norm_mix_g, m_w_in, m_conv_short_w, m_w_short_out, m_conv_conf_w, m_conv_conf_b, m_conf_ln_g, m_conf_ln_b, m_w_conf_out, m_w_o, m_norm_ffn_g, m_w_ffn_in, m_w_ffn_out, m_final_norm_g, v_w_ada, v_b_ada, v_norm_mix_g, v_w_in, v_conv_short_w, v_w_short_out, v_conv_conf_w, v_conv_conf_b, v_conf_ln_g, v_conf_ln_b, v_w_conf_out, v_w_o, v_norm_ffn_g, v_w_ffn_in, v_w_ffn_out, v_final_norm_g):
    t, d = x.shape[1], x.shape[2]
    x2 = x.reshape(t, d)
    tgt = loss_target.reshape(t, d)
    me = _slot(*_my_coords())
    tm = _pick(t, 256)
    tm_fwd_in = _pick(t, 1024)
    tk = _pick(t, 2048)

    taps_loc = jnp.zeros((40, LANES), F32)
    taps_loc = taps_loc.at[0:SHORT_K].set(conv_short_w[0]).at[8:8 + CONF_K].set(conv_conf_w[0])
    prm = jnp.concatenate([norm_mix_g, norm_ffn_g, final_norm_g.reshape(1, d), conv_conf_b, conf_ln_g, conf_ln_b,
                           jnp.zeros((2, d), F32)], axis=0)

    n_in = w_in.shape[2]
    nk = t // tk
    tok = pl.BlockSpec((tk, d), lambda s, k: (k, 0))
    rows = d // N_DEV
    frows = w_ffn_out.shape[1]

    proj, h, w_in_g, mod, sc_all3, taps, (w_so_g, w_co_g, w_o_g) = _forward_in(
        jnp.reshape(me, (1,)).astype(jnp.int32), x2, prm, w_in[0].astype(BF16),
        (c, w_ada[0], b_ada, taps_loc), tm_fwd_in,
        [w_short_out[0].astype(BF16), w_conf_out[0].astype(BF16), w_o[0].astype(BF16)])
    w_so = w_so_g.reshape(d, d)
    w_co = w_co_g.reshape(d, d)
    w_oo = w_o_g.reshape(d, d)
    w_fi_t, m_fi_t, v_fi_t = w_ffn_in[0].T, m_w_ffn_in[0].T, v_w_ffn_in[0].T
    (x1, y_a, y_b, mix, u1, conv3, ya_pre, u3, merged), (w_fi_g, w_fo_g) = _forward_mix(
        x2, proj, mod, prm, taps, w_so, w_co, w_oo, tm, [w_fi_t.astype(BF16), w_ffn_out[0].astype(BF16)])
    fb = w_fi_g.shape[1]
    ff = N_DEV * frows
    w_fi_all = w_fi_g.reshape(N_DEV * fb, d)
    w_fo_all = w_fo_g.reshape(ff, d)
    dx2, ab, h2, sums_f = _forward_ffn(x1, tgt, mod, prm, w_fi_all, w_fo_all, tm)

    dx1, df, act, dab, sums_b = _backward_ffn(dx2, x1, ab, mod, prm, w_fi_all, w_fo_all, tm)
    fb2 = 2 * fb
    pair_tok = pl.BlockSpec((tk, fb2), lambda s, k: (k, s))
    g_fi = _weight_grad(dab, h2, pair_tok, tok, N_DEV // 2, fb2, d, nk, "grad_w_ffn_in")
    g_fo = _weight_grad(act, df, pair_tok, tok, N_DEV // 4, fb2, d, nk, "grad_w_ffn_out")
    (dproj, dmix, dya, dyb, sums_m, dw3p, dw31p), (p_fi, p_fo) = _backward_mix(
        dx1, proj, y_a, y_b, mix, u1, conv3, mod, prm, taps, w_so, w_co, w_oo, tm,
        [g_fi.reshape(N_DEV, fb, d), g_fo.reshape(N_DEV, frows, d)])
    tk_in = _pick(t, 4096)
    nk_in = t // tk_in
    tok_in = pl.BlockSpec((tk_in, d), lambda s, k: (k, 0))
    g_so = _weight_grad(ya_pre, dya, tok, tok, 1, d, d, nk, "grad_w_short_out")
    g_co = _weight_grad(u3, dyb, tok, tok, 1, d, d, nk, "grad_w_conf_out")
    g_oo = _weight_grad(merged, dmix, tok, tok, 1, d, d, nk, "grad_w_o")
    n_blk = _pick(N_DEV * n_in, d)
    g_in, (p_so, p_co, p_oo) = _weight_grad(
        h, dproj, tok_in, pl.BlockSpec((tk_in, n_blk), lambda s, k: (k, s)),
        N_DEV * n_in // n_blk, d, n_blk, nk_in, "grad_w_in",
        [g_so.reshape(N_DEV, rows, d), g_co.reshape(N_DEV, rows, d), g_oo.reshape(N_DEV, rows, d)], by_columns=True)
    grad_x, sums_i, (p_in,) = _backward_in(dproj, x2, dx1, mod, prm, w_in_g, tm, [g_in], _column_block(n_in),
                                           [(N_DEV, d, n_in)])

    up_in = _adamw_shard(p_in, w_in[0], m_w_in[0], v_w_in[0], _pick(d, 256), "adamw_w_in")
    up_so = _adamw_shard(p_so, w_short_out[0], m_w_short_out[0], v_w_short_out[0], rows, "adamw_w_short_out")
    up_co = _adamw_shard(p_co, w_conf_out[0], m_w_conf_out[0], v_w_conf_out[0], rows, "adamw_w_conf_out")
    up_oo = _adamw_shard(p_oo, w_o[0], m_w_o[0], v_w_o[0], rows, "adamw_w_o")
    up_fi = tuple(a.T for a in _adamw_shard(p_fi, w_fi_t, m_fi_t, v_fi_t, fb // 2, "adamw_w_ffn_in"))
    up_fo = _adamw_shard(p_fo, w_ffn_out[0], m_w_ffn_out[0], v_w_ffn_out[0], frows, "adamw_w_ffn_out")

    vec = jnp.concatenate([sums_i[0:2], sums_m[0:1], sums_b[0:2], sums_f[1:2],
                           sums_i[2:3], sums_m[3:4], sums_m[1:3], sums_b[2:3], sums_f[0:1],
                           sums_f[2:3]], axis=0)
    vec = vec.reshape(1, 1, 13 * d)
    dw3 = dw3p.reshape(SHORT_K, SUBLANES, d).sum(axis=1)
    dw31 = dw31p.reshape(CONF_K, SUBLANES, d).sum(axis=1)
    cg = jnp.zeros((40, d), F32).at[0:SHORT_K].set(dw3).at[8:8 + CONF_K].set(dw31)
    cg = cg.reshape(40, N_DEV, LANES).transpose(1, 0, 2)
    fin = lambda a: a.reshape(1, d)
    tap = lambda a: a.reshape(a.shape[1:])
    smalls = [
        (b_ada, m_b_ada, v_b_ada, 0, 6 * d, "vec"),
        (norm_mix_g, m_norm_mix_g, v_norm_mix_g, 6 * d, 7 * d, "vec"),
        (tap(conv_short_w), tap(m_conv_short_w), tap(v_conv_short_w), 0, SHORT_K, "cg"),
        (tap(conv_conf_w), tap(m_conv_conf_w), tap(v_conv_conf_w), 8, 8 + CONF_K, "cg"),
        (conv_conf_b, m_conv_conf_b, v_conv_conf_b, 7 * d, 8 * d, "vec"),
        (conf_ln_g, m_conf_ln_g, v_conf_ln_g, 8 * d, 9 * d, "vec"),
        (conf_ln_b, m_conf_ln_b, v_conf_ln_b, 9 * d, 10 * d, "vec"),
        (norm_ffn_g, m_norm_ffn_g, v_norm_ffn_g, 10 * d, 11 * d, "vec"),
        (fin(final_norm_g), fin(m_final_norm_g), fin(v_final_norm_g), 11 * d, 12 * d, "vec"),
    ]
    vall, cgr = _small_exchange(vec, cg)
    up_small = _small_update(vall, cgr, smalls)
    n_ada = w_ada.shape[2]
    dmod_all = vall.reshape(N_DEV, 13 * d)[:, 0:6 * d]
    dmod_cols = lax.dynamic_slice(dmod_all, (0, me * n_ada), (N_DEV, n_ada))
    up_ada = _ada_update(sc_all3.reshape(N_DEV, d), dmod_cols, w_ada[0], m_w_ada[0], v_w_ada[0], _pick(d, 256))

    loss = jnp.sum(vall.reshape(N_DEV, 13 * d)[:, 12 * d:])

    lead = lambda a: a.reshape((1,) + a.shape)
    ups = [tuple(lead(a) for a in up_ada), up_small[0], up_small[1], tuple(lead(a) for a in up_in),
           tuple(lead(a) for a in up_small[2]), tuple(lead(a) for a in up_so), tuple(lead(a) for a in up_small[3]),
           up_small[4], up_small[5], up_small[6],
           tuple(lead(a) for a in up_co), tuple(lead(a) for a in up_oo), up_small[7],
           tuple(lead(a) for a in up_fi), tuple(lead(a) for a in up_fo),
           tuple(a.reshape(d) for a in up_small[8])]
    grads = [u[0] for u in ups]
    deltas = [u[1] for u in ups]
    new_m = [u[2] for u in ups]
    new_v = [u[3] for u in ups]
    return (loss, grad_x.reshape(1, t, d), *grads, *deltas, *new_m, *new_v)
```

```python
import functools

import jax
import jax.numpy as jnp
from jax import lax
from jax.experimental import pallas as pl
from jax.experimental.pallas import tpu as pltpu

F32 = jnp.float32
BF16 = jnp.bfloat16
MESH = pl.DeviceIdType.MESH

N_DEV = 8
EPS = 1e-6
LN_EPS = 1e-5
SHORT_K = 3
CONF_K = 31
ADAM_LR = 0.001
ADAM_B1 = 0.9
ADAM_B2 = 0.999
ADAM_EPS = 1e-08
ADAM_WD = 0.01
ADAM_STEP = 10

LANES = 128
SUBLANES = 8
CONV_ROWS = 64
HALO_SHORT = 8
HALO_CONF = 32
VMEM_LIMIT = 56 * 1024 * 1024

M_SH1, M_SC1, M_G1, M_SH2, M_SC2, M_G2 = range(6)
P_GMIX, P_GFFN, P_GFIN, P_CBIAS, P_LNG, P_LNB = range(6)


def _params(sem=None, **kw):
    return pltpu.CompilerParams(dimension_semantics=sem, vmem_limit_bytes=VMEM_LIMIT, **kw)


def _sigmoid(v):
    return jax.nn.sigmoid(v)


def _dot(a, b):
    return jnp.dot(a, b, preferred_element_type=F32)


def _dot_nt(a, b):
    return lax.dot_general(a, b, (((1,), (1,)), ((), ())), preferred_element_type=F32)


def _dot_tn(a, b):
    return lax.dot_general(a, b, (((0,), (0,)), ((), ())), preferred_element_type=F32)


def _colsum(v):
    return jnp.sum(v, axis=0, keepdims=True)


def _rowmean(v):
    return jnp.mean(v, axis=-1, keepdims=True)


def _my_coords():
    return lax.axis_index("x"), lax.axis_index("y"), lax.axis_index("c")


def _slot(px, py, pc):
    return 4 * px + 2 * py + pc


def _flip(v, bit):
    return 1 - v if bit else v


def _taps_by_residue(taps):
    by_res = {}
    for wi, off in taps:
        by_res.setdefault(off % SUBLANES, []).append((wi, off // SUBLANES))
    return sorted(by_res.items())


def _tap_conv(ext_ref, w_ref, taps, tm, extra, out_ref):
    d = out_ref.shape[1]
    rb = min(CONV_ROWS, tm)
    wrows = rb + extra
    groups = _taps_by_residue(taps)

    def block(i, carry):
        base = pl.multiple_of(i * rb, SUBLANES)
        for lc in range(d // LANES):
            ls = pl.ds(lc * LANES, LANES)
            win = ext_ref[pl.ds(base, wrows), ls]
            acc = None
            for r, lst in groups:
                sh = win if r == 0 else pltpu.roll(win, wrows - r, 0)
                for wi, q in lst:
                    term = w_ref[wi:wi + 1, ls] * sh[SUBLANES * q:SUBLANES * q + rb, :]
                    acc = term if acc is None else acc + term
            out_ref[pl.ds(base, rb), ls] = acc
        return carry

    lax.fori_loop(0, tm // rb, block, 0)


def _tap_wgrad(a_ref, ext_ref, taps, tm, extra, acc_ref):
    d = a_ref.shape[1]
    rb = min(CONV_ROWS, tm)
    wrows = rb + extra
    groups = _taps_by_residue(taps)

    def block(i, carry):
        base = pl.multiple_of(i * rb, SUBLANES)
        for lc in range(d // LANES):
            ls = pl.ds(lc * LANES, LANES)
            a_blk = a_ref[pl.ds(base, rb), ls]
            win = ext_ref[pl.ds(base, wrows), ls]
            for r, lst in groups:
                sh = win if r == 0 else pltpu.roll(win, wrows - r, 0)
                for wi, q in lst:
                    prod = a_blk * sh[SUBLANES * q:SUBLANES * q + rb, :]
                    part = prod[0:SUBLANES, :]
                    for s in range(1, rb // SUBLANES):
                        part = part + prod[SUBLANES * s:SUBLANES * (s + 1), :]
                    rows = pl.ds(SUBLANES * wi, SUBLANES)
                    acc_ref[rows, ls] = acc_ref[rows, ls] + part
        return carry

    lax.fori_loop(0, tm // rb, block, 0)


def _causal_taps(k, halo):
    return [(i, halo - (k - 1) + i) for i in range(k)]


def _anticausal_taps(k):
    return [(i, (k - 1) - i) for i in range(k)]


def _ada_scratch(d, nloc, trows):
    return [pltpu.VMEM((N_DEV, 1, d), F32), pltpu.VMEM((N_DEV, 1, nloc), F32), pltpu.VMEM((N_DEV, 1, nloc), F32),
            pltpu.VMEM((N_DEV, trows, LANES), F32), pltpu.SemaphoreType.DMA((21,)), pltpu.SemaphoreType.DMA((21,))]


def _ada_exchange(c_ref, w_ref, b_ref, t_ref, mod_ref, sc_ref, taps_ref,
                  scall_ref, part_ref, modrecv_ref, tapsall_ref, send_sems, recv_sems):
    x, y, cc = _my_coords()
    me = _slot(x, y, cc)
    d = c_ref.shape[1]
    cv = c_ref[...]
    scall_ref[me] = cv * _sigmoid(cv)
    tapsall_ref[me] = t_ref[...]

    def peer_of(k):
        return (_flip(x, k & 4), _flip(y, k & 2), _flip(cc, k & 1))

    def gather_copy(ref, base, k):
        return pltpu.make_async_remote_copy(
            src_ref=ref.at[me], dst_ref=ref.at[me], send_sem=send_sems.at[base + k - 1],
            recv_sem=recv_sems.at[base + k - 1], device_id=peer_of(k), device_id_type=MESH)

    first = [gather_copy(scall_ref, 0, k) for k in range(1, N_DEV)]
    first += [gather_copy(tapsall_ref, 7, k) for k in range(1, N_DEV)]
    for cp in first:
        cp.start()
    for cp in first[:7]:
        cp.wait_recv()
    sc_all = jnp.concatenate([scall_ref[s] for s in range(N_DEV)], axis=0)
    for s in range(N_DEV):
        sc_ref[s] = scall_ref[s]
    part = jnp.dot(sc_all, w_ref[...], preferred_element_type=F32,
                   precision=lax.Precision.HIGHEST)
    for b in range(N_DEV):
        part_ref[b] = part[b:b + 1, :]
    modrecv_ref[me] = part_ref[me]
    second = []
    for k in range(1, N_DEV):
        px, py, pc = peer_of(k)
        second.append(pltpu.make_async_remote_copy(
            src_ref=part_ref.at[_slot(px, py, pc)], dst_ref=modrecv_ref.at[me],
            send_sem=send_sems.at[14 + k - 1], recv_sem=recv_sems.at[14 + k - 1],
            device_id=(px, py, pc), device_id_type=MESH))
    for cp in second:
        cp.start()
    for cp in second:
        cp.wait_recv()
    mod = jnp.concatenate([modrecv_ref[s] for s in range(N_DEV)], axis=1) + b_ref[...]
    for r in range(6):
        mod_ref[r:r + 1, :] = mod[:, r * d:(r + 1) * d]
    mod_ref[6:8, :] = jnp.zeros((2, d), F32)
    for cp in first[7:]:
        cp.wait_recv()
    taps_ref[...] = jnp.concatenate([tapsall_ref[s] for s in range(N_DEV)], axis=1)
    for cp in first + second:
        cp.wait_send()


ANY_SPEC = pl.BlockSpec(memory_space=pl.ANY)


def _comm_scratch(na):
    return [pltpu.SemaphoreType.DMA((7 * na,)), pltpu.SemaphoreType.DMA((7 * na,)), pltpu.SemaphoreType.DMA((na,))]


def _leading_block(ref, slot):
    return ref.at[slot]


def _column_block(width):
    def view(ref, slot):
        return ref.at[:, pl.ds(pl.multiple_of(slot * width, LANES), width)]
    return view


class _Gather:
    def __init__(self, ins, outs, send_sems, recv_sems, local_sems, view=_leading_block):
        self.na = len(ins)
        x, y, c = _my_coords()
        self.c = c
        self.view = view
        self.me, self.sibling = (x, y, c), (x, y, 1 - c)
        self.chips = [(1 - x, y), (x, 1 - y), (1 - x, 1 - y)]
        self.outs, self.send_sems, self.recv_sems = outs, send_sems, recv_sems
        self.mine = [pltpu.make_async_copy(ins[a], view(outs[a], _slot(*self.me)), local_sems.at[a])
                     for a in range(self.na)]
        self.first = []
        for a in range(self.na):
            self.first.append(self._copy(a, 0, self.me, self.sibling, src=ins[a]))
            self.first += [self._copy(a, 1 + j, self.me, (*chip, c), src=ins[a]) for j, chip in enumerate(self.chips)]
        self.passed = [self._copy(a, 4 + j, (*chip, c), self.sibling)
                       for a in range(self.na) for j, chip in enumerate(self.chips)]

    def _copy(self, a, k, block, to, src=None):
        dst = self.view(self.outs[a], _slot(*block))
        return pltpu.make_async_remote_copy(
            src_ref=dst if src is None else src, dst_ref=dst,
            send_sem=self.send_sems.at[7 * a + k], recv_sem=self.recv_sems.at[7 * a + k],
            device_id=to, device_id_type=MESH)

    def start(self):
        self.start_near()
        self.start_far()

    def start_near(self):
        for cp in self.mine + [cp for n, cp in enumerate(self.first) if n % 4 != 3]:
            cp.start()

    def start_far(self):
        for cp in self.first[3::4]:
            cp.start()

    def forward_chip(self, j):
        for a in range(self.na):
            self._copy(a, 1 + j, (*self.chips[j], self.c), self.me).wait_recv()
            self.passed[3 * a + j].start()

    def wait_sibling_own(self):
        for a in range(self.na):
            self._copy(a, 0, self.sibling, self.me).wait_recv()

    def wait_sibling_passed(self, j):
        for a in range(self.na):
            self._copy(a, 4 + j, (*self.chips[j], 1 - self.c), self.me).wait_recv()

    def finish_sends(self):
        for cp in self.first + self.passed:
            cp.wait_send()
        for cp in self.mine:
            cp.wait()

    def forward(self):
        for j in range(3):
            self.forward_chip(j)

    def finish(self):
        self.wait_sibling_own()
        for j in range(3):
            self.wait_sibling_passed(j)
        self.finish_sends()


def _scatter_plan(ins, outs, send_sems, recv_sems, local_sems, view=_leading_block):
    na = len(ins)
    x, y, c = _my_coords()
    me = _slot(x, y, c)
    mine = [pltpu.make_async_copy(view(ins[a], me), outs[a].at[me], local_sems.at[a]) for a in range(na)]
    copies = []
    for k in range(1, N_DEV):
        peer = (_flip(x, k & 4), _flip(y, k & 2), _flip(c, k & 1))
        for a in range(na):
            copies.append(pltpu.make_async_remote_copy(
                src_ref=view(ins[a], _slot(*peer)), dst_ref=outs[a].at[me],
                send_sem=send_sems.at[7 * a + k - 1], recv_sem=recv_sems.at[7 * a + k - 1],
                device_id=peer, device_id_type=MESH))

    def start():
        for cp in mine + copies:
            cp.start()

    def finish():
        for cp in copies:
            cp.wait_recv()
        for cp in copies:
            cp.wait_send()
        for cp in mine:
            cp.wait()

    return start, finish


def _forward_in(me_arr, x, prm, w_in_loc, ada, tm, shards):
    t, d = x.shape
    n = w_in_loc.shape[1]
    ns = N_DEV
    na = len(shards)
    nt = t // tm
    nloc = ada[1].shape[1]
    trows = ada[3].shape[0]

    def body(*refs):
        x_ref, prm_ref, wloc_ref = refs[1:4]
        ada_in = refs[4:8]
        shard_refs = refs[8:8 + na]
        proj_ref, h_ref, wg_ref, mod_ref, sc_ref, taps_ref = refs[8 + na:14 + na]
        gathered_refs = refs[14 + na:14 + 2 * na]
        hall_ref, wv_ref, wv_sem = refs[14 + 2 * na:17 + 2 * na]
        ada_scratch = refs[17 + 2 * na:23 + 2 * na]
        sems = refs[23 + 2 * na:]
        s, i = pl.program_id(0), pl.program_id(1)
        columns = _column_block(n)

        def gathers():
            return (_Gather([wloc_ref], [wg_ref], *sems[0:3], view=columns),
                    _Gather(shard_refs, gathered_refs, *sems[3:6]))

        def shard_copy(src, step):
            return pltpu.make_async_copy(src, wv_ref.at[step % 2], wv_sem.at[step % 2])

        @pl.when(i == 0)
        def _():
            me = _slot(*_my_coords())

            @pl.when(s == 0)
            def _():
                g_in, _ = gathers()
                _ada_exchange(*ada_in, mod_ref, sc_ref, taps_ref, *ada_scratch)
                g_in.start_near()
                own = shard_copy(wloc_ref, 0)
                own.start()
                own.wait()

            for step in range(1, ns):
                @pl.when(s == step)
                def _(step=step):
                    shard_copy(columns(wg_ref, me ^ step), step).wait()

        @pl.when(i == nt - 1)
        def _():
            g_in, g_rest = gathers()
            me = _slot(*_my_coords())
            arrivals = {1: g_in.wait_sibling_own,
                        2: functools.partial(g_in.forward_chip, 1), 3: functools.partial(g_in.wait_sibling_passed, 1),
                        4: functools.partial(g_in.forward_chip, 0), 5: functools.partial(g_in.wait_sibling_passed, 0),
                        6: functools.partial(g_in.forward_chip, 2), 7: functools.partial(g_in.wait_sibling_passed, 2)}
            for step, arrive in arrivals.items():
                @pl.when(s == step - 1)
                def _(step=step, arrive=arrive):
                    arrive()
                    if step == 2:
                        g_in.start_far()
                    if step == 4:
                        g_rest.start()
                    if step == 7:
                        g_rest.forward()
                    shard_copy(columns(wg_ref, me ^ step), step).start()

        @pl.when(s == 0)
        def _():
            xv = x_ref[...]
            r = lax.rsqrt(_rowmean(xv * xv) + EPS)
            h = xv * r * prm_ref[P_GMIX:P_GMIX + 1, :] * (1.0 + mod_ref[M_SC1:M_SC1 + 1, :]) \
                + mod_ref[M_SH1:M_SH1 + 1, :]
            hb = h.astype(BF16)
            hall_ref[i] = hb
            h_ref[...] = hb

        proj_ref[...] = _dot(hall_ref[i], wv_ref[s % 2]).astype(BF16)

        @pl.when((s == ns - 1) & (i == nt - 1))
        def _():
            g_in, g_rest = gathers()
            g_in.finish_sends()
            g_rest.finish()

    x_tile = pl.BlockSpec((tm, d), lambda s, i, me: (jnp.where(s == 0, i, nt - 1), 0))
    whole = lambda a: pl.BlockSpec(a.shape, lambda s, i, me: (0,) * len(a.shape))
    small_out = (jax.ShapeDtypeStruct((8, d), F32), jax.ShapeDtypeStruct((N_DEV, 1, d), F32),
                 jax.ShapeDtypeStruct((trows, N_DEV * LANES), F32))
    res = pl.pallas_call(
        body, name="forward_in",
        grid_spec=pltpu.PrefetchScalarGridSpec(
            num_scalar_prefetch=1, grid=(ns, nt),
            in_specs=[x_tile, whole(prm), ANY_SPEC] + [whole(a) for a in ada] + [ANY_SPEC] * na,
            out_specs=[pl.BlockSpec((tm, n), lambda s, i, me: (i, me[0] ^ s)), x_tile, ANY_SPEC]
            + [whole(a) for a in small_out] + [ANY_SPEC] * na,
            scratch_shapes=[pltpu.VMEM((nt, tm, d), BF16), pltpu.VMEM((2, d, n), BF16), pltpu.SemaphoreType.DMA((2,))]
            + _ada_scratch(d, nloc, trows) + _comm_scratch(1) + _comm_scratch(na)),
        out_shape=(jax.ShapeDtypeStruct((t, ns * n), BF16), jax.ShapeDtypeStruct((t, d), BF16),
                   jax.ShapeDtypeStruct((d, ns * n), BF16)) + small_out
        + tuple(jax.ShapeDtypeStruct((N_DEV,) + a.shape, a.dtype) for a in shards),
        compiler_params=_params(("arbitrary", "arbitrary")),
    )(me_arr, x, prm, w_in_loc, *ada, *shards)
    return res[0], res[1], res[2], res[3], res[4], res[5], res[6:]


def _forward_mix(x, proj, mod, prm, taps, w_so, w_co, w_o, tm, shards):
    t, d = x.shape
    nt = t // tm
    na = len(shards)
    hs, hc = HALO_SHORT, HALO_CONF
    r3, r31 = 0, 8

    def body(*refs):
        x_ref, proj_ref, mod_ref, prm_ref, taps_ref, wso_ref, wco_ref, wo_ref = refs[:8]
        (x1_ref, ya_ref, yb_ref, mix_ref, u1_ref, c3_ref, yapre_ref, u3_ref, merged_ref,
         sg_ref, ga_ref, gb_ref) = refs[8 + na:20 + na]
        cv_ext, u0_ext, conv3_ref, u1f_ref = refs[20 + 2 * na:24 + 2 * na]
        gather = _Gather(refs[8:8 + na], refs[20 + na:20 + 2 * na], *refs[24 + 2 * na:])
        i = pl.program_id(0)

        @pl.when(i == 0)
        def _():
            gather.start()
            cv_ext[0:hs, :] = jnp.zeros((hs, d), F32)
            u0_ext[0:hc, :] = jnp.zeros((hc, d), F32)

        @pl.when(i == (3 * nt) // 4)
        def _():
            gather.forward()

        def col(g):
            return proj_ref[:, g * d:(g + 1) * d].astype(F32)

        cv_ext[hs:hs + tm, :] = col(1) * col(2)
        sg = _sigmoid(col(4))
        sg_ref[...] = sg.astype(BF16)
        u0_ext[hc:hc + tm, :] = col(3) * sg
        _tap_conv(cv_ext, taps_ref, [(r3 + wi, off) for wi, off in _causal_taps(SHORT_K, hs)], tm, hs, conv3_ref)
        _tap_conv(u0_ext, taps_ref, [(r31 + wi, off) for wi, off in _causal_taps(CONF_K, hc)], tm, hc, u1f_ref)
        cv_ext[0:hs, :] = cv_ext[tm:tm + hs, :]
        u0_ext[0:hc, :] = u0_ext[tm:tm + hc, :]

        c3_ref[...] = conv3_ref[...].astype(BF16)
        ya_pre = (col(0) * c3_ref[...].astype(F32)).astype(BF16)
        yapre_ref[...] = ya_pre
        y_a = _dot(ya_pre, wso_ref[...])
        u1 = u1f_ref[...] + prm_ref[P_CBIAS:P_CBIAS + 1, :]
        u1_ref[...] = u1.astype(BF16)
        u1 = u1_ref[...].astype(F32)
        mu = _rowmean(u1)
        uc = u1 - mu
        rstd = lax.rsqrt(_rowmean(uc * uc) + LN_EPS)
        u2 = uc * rstd * prm_ref[P_LNG:P_LNG + 1, :] + prm_ref[P_LNB:P_LNB + 1, :]
        u3 = (u2 * _sigmoid(u2)).astype(BF16)
        u3_ref[...] = u3
        y_b = _dot(u3, wco_ref[...])
        ya_ref[...] = y_a.astype(BF16)
        yb_ref[...] = y_b.astype(BF16)
        ga = _sigmoid(col(5))
        gb = _sigmoid(col(6))
        ga_ref[...] = ga.astype(BF16)
        gb_ref[...] = gb.astype(BF16)
        merged = (ga * ya_ref[...].astype(F32) + gb * yb_ref[...].astype(F32)).astype(BF16)
        merged_ref[...] = merged
        mix = _dot(merged, wo_ref[...])
        mix_ref[...] = mix.astype(BF16)
        x1_ref[...] = x_ref[...] + mod_ref[M_G1:M_G1 + 1, :] * mix

        @pl.when(i == nt - 1)
        def _():
            gather.finish()

    tile = pl.BlockSpec((tm, d), lambda i: (i, 0))
    whole = lambda shape: pl.BlockSpec(shape, lambda i: (0,) * len(shape))
    res = pl.pallas_call(
        body, name="forward_mix", grid=(nt,),
        out_shape=(jax.ShapeDtypeStruct((t, d), F32),) + (jax.ShapeDtypeStruct((t, d), BF16),) * 11
        + tuple(jax.ShapeDtypeStruct((N_DEV,) + a.shape, a.dtype) for a in shards),
        in_specs=[tile, pl.BlockSpec((tm, 7 * d), lambda i: (i, 0)), whole((8, d)), whole((8, d)),
                  whole(taps.shape), whole((d, d)), whole((d, d)), whole((d, d))] + [ANY_SPEC] * na,
        out_specs=[tile] * 12 + [ANY_SPEC] * na,
        scratch_shapes=[pltpu.VMEM((hs + tm, d), F32), pltpu.VMEM((hc + tm, d), F32),
                        pltpu.VMEM((tm, d), F32), pltpu.VMEM((tm, d), F32)] + _comm_scratch(na),
        compiler_params=_params(("arbitrary",)),
    )(x, proj, mod, prm, taps, w_so, w_co, w_o, *shards)
    return res[:12], res[12:]


def _ffn_chunks(ff):
    mxu = 2 * LANES
    cut = (ff // mxu + 1) // 2 * mxu
    return [(0, cut), (cut, ff)] if 0 < cut < ff and ff % mxu == 0 else [(0, ff)]


def _forward_ffn(x1, tgt, mod, prm, w_fi_t, w_fo, tm):
    t, d = x1.shape
    ff = w_fo.shape[0]

    def body(x1_ref, tgt_ref, mod_ref, prm_ref, wfi_hbm, wfo_hbm,
             dx2_ref, ab_ref, h2_ref, act_ref, sums_ref, wfi_ref, wfo_ref):
        i = pl.program_id(0)

        @pl.when(i == 0)
        def _():
            pltpu.sync_copy(wfi_hbm, wfi_ref)
            pltpu.sync_copy(wfo_hbm, wfo_ref)
            sums_ref[...] = jnp.zeros((8, d), F32)

        x1v = x1_ref[...]
        r2 = lax.rsqrt(_rowmean(x1v * x1v) + EPS)
        h2 = (x1v * r2 * prm_ref[P_GFFN:P_GFFN + 1, :] * (1.0 + mod_ref[M_SC2:M_SC2 + 1, :])
              + mod_ref[M_SH2:M_SH2 + 1, :]).astype(BF16)
        h2_ref[...] = h2
        f = jnp.zeros((tm, d), F32)
        for c0, c1 in _ffn_chunks(ff):
            ab_ref[:, c0:c1] = _dot_nt(h2, wfi_ref[c0:c1, :]).astype(BF16)
            ab_ref[:, ff + c0:ff + c1] = _dot_nt(h2, wfi_ref[ff + c0:ff + c1, :]).astype(BF16)
            a = ab_ref[:, c0:c1].astype(F32)
            act = (a * _sigmoid(a) * ab_ref[:, ff + c0:ff + c1].astype(F32)).astype(BF16)
            act_ref[:, c0:c1] = act
            f = f + _dot(act, wfo_ref[c0:c1, :])
        x2 = x1v + mod_ref[M_G2:M_G2 + 1, :] * f
        r3 = lax.rsqrt(_rowmean(x2 * x2) + EPS)
        xn3 = x2 * r3
        gfin = prm_ref[P_GFIN:P_GFIN + 1, :]
        err = xn3 * gfin - tgt_ref[...]
        dy = err * (1.0 / d)
        dxn3 = dy * gfin
        dx2 = r3 * (dxn3 - xn3 * _rowmean(dxn3 * xn3))
        dx2_ref[...] = dx2
        sums_ref[0:1, :] = sums_ref[0:1, :] + _colsum(dy * xn3)
        sums_ref[1:2, :] = sums_ref[1:2, :] + _colsum(dx2 * f)
        sums_ref[2:3, :] = sums_ref[2:3, :] + _colsum(err * err) * (0.5 / d)

    tile = pl.BlockSpec((tm, d), lambda i: (i, 0))
    whole = lambda shape: pl.BlockSpec(shape, lambda i: (0,) * len(shape))
    return pl.pallas_call(
        body, name="forward_ffn", grid=(t // tm,),
        out_shape=(jax.ShapeDtypeStruct((t, d), F32), jax.ShapeDtypeStruct((t, 2 * ff), BF16),
                   jax.ShapeDtypeStruct((t, d), BF16), jax.ShapeDtypeStruct((t, ff), BF16),
                   jax.ShapeDtypeStruct((8, d), F32)),
        in_specs=[tile, tile, whole((8, d)), whole((8, d)), ANY_SPEC, ANY_SPEC],
        out_specs=[tile, pl.BlockSpec((tm, 2 * ff), lambda i: (i, 0)), tile,
                   pl.BlockSpec((tm, ff), lambda i: (i, 0)), whole((8, d))],
        scratch_shapes=[pltpu.VMEM(w_fi_t.shape, BF16), pltpu.VMEM(w_fo.shape, BF16)],
        compiler_params=_params(("arbitrary",)),
    )(x1, tgt, mod, prm, w_fi_t, w_fo)


def _backward_ffn(dx2, x1, ab, mod, prm, w_fi_t, w_fo, tm):
    t, d = x1.shape
    ff = w_fo.shape[0]

    def body(dx2_ref, x1_ref, ab_ref, mod_ref, prm_ref, wfi_hbm, wfo_hbm,
             dx1_ref, df_ref, dab_ref, sums_ref, wfi_ref, wfo_ref):
        i = pl.program_id(0)

        @pl.when(i == 0)
        def _():
            pltpu.sync_copy(wfi_hbm, wfi_ref)
            pltpu.sync_copy(wfo_hbm, wfo_ref)
            sums_ref[...] = jnp.zeros((8, d), F32)

        dx2v = dx2_ref[...]
        df = (mod_ref[M_G2:M_G2 + 1, :] * dx2v).astype(BF16)
        df_ref[...] = df
        dh2 = jnp.zeros((tm, d), F32)
        for c0, c1 in _ffn_chunks(ff):
            dact = _dot_nt(df, wfo_ref[c0:c1, :])
            a = ab_ref[:, c0:c1].astype(F32)
            b = ab_ref[:, ff + c0:ff + c1].astype(F32)
            s = _sigmoid(a)
            sil = a * s
            da =(dact * b * (s * (1.0 + a * (1.0 - s)))).astype(BF16)
            db = (dact * sil).astype(BF16)
            dab_ref[:, c0:c1] = da
            dab_ref[:, ff + c0:ff + c1] = db
            dh2 = dh2 + _dot(da, wfi_ref[c0:c1, :]) + _dot(db, wfi_ref[ff + c0:ff + c1, :])
        x1v = x1_ref[...]
        r2 = lax.rsqrt(_rowmean(x1v * x1v) + EPS)
        xn2 = x1v * r2
        gffn = prm_ref[P_GFFN:P_GFFN + 1, :]
        scale = 1.0 + mod_ref[M_SC2:M_SC2 + 1, :]
        dxn2 = dh2 * gffn * scale
        dx1_ref[...] = dx2v + r2 * (dxn2 - xn2 * _rowmean(dxn2 * xn2))
        hx = dh2 * xn2
        sums_ref[0:1, :] = sums_ref[0:1, :] + _colsum(dh2)
        sums_ref[1:2, :] = sums_ref[1:2, :] + _colsum(hx) * gffn
        sums_ref[2:3, :] = sums_ref[2:3, :] + _colsum(hx) * scale

    tile = pl.BlockSpec((tm, d), lambda i: (i, 0))
    whole = lambda shape: pl.BlockSpec(shape, lambda i: (0,) * len(shape))
    wide = lambda n: pl.BlockSpec((tm, n), lambda i: (i, 0))
    return pl.pallas_call(
        body, name="backward_ffn", grid=(t // tm,),
        out_shape=(jax.ShapeDtypeStruct((t, d), F32), jax.ShapeDtypeStruct((t, d), BF16),
                   jax.ShapeDtypeStruct((t, 2 * ff), BF16), jax.ShapeDtypeStruct((8, d), F32)),
        in_specs=[tile, tile, wide(2 * ff), whole((8, d)), whole((8, d)), ANY_SPEC, ANY_SPEC],
        out_specs=[tile, tile, wide(2 * ff), whole((8, d))],
        scratch_shapes=[pltpu.VMEM(w_fi_t.shape, BF16), pltpu.VMEM(w_fo.shape, BF16)],
        compiler_params=_params(("arbitrary",)),
    )(dx2, x1, ab, mod, prm, w_fi_t, w_fo)


def _backward_mix(dx1, proj, y_a, y_b, mix, u1, conv3, sg, ga, gb, mod, prm, taps, w_so, w_co, w_o, tm, partials):
    t, d = dx1.shape
    nt = t // tm
    na = len(partials)
    hs, hc = HALO_SHORT, HALO_CONF
    r3, r31 = 0, 8

    def body(*refs):
        (dx1_ref, proj_ref, halo_ref, ya_ref, yb_ref, mix_ref, u1_ref, c3_ref, sg_ref, ga_ref, gb_ref,
         mod_ref, prm_ref, taps_ref, wso_ref, wco_ref, wo_ref) = refs[:17]
        dproj_ref, dmix_ref, dya_ref, dyb_ref, sums_ref, dw3_ref, dw31_ref = refs[17 + na:24 + na]
        cv_ext, u0_ext, d3_ext, du1_ext, tmp_ref = refs[24 + 2 * na:29 + 2 * na]
        scatter_start, scatter_finish = _scatter_plan(refs[17:17 + na], refs[24 + na:24 + 2 * na], *refs[29 + 2 * na:])
        i = pl.program_id(0)
        first_tile = i == nt - 1

        @pl.when(i == 0)
        def _():
            scatter_start()
            sums_ref[...] = jnp.zeros((8, d), F32)
            dw3_ref[...] = jnp.zeros(dw3_ref.shape, F32)
            dw31_ref[...] = jnp.zeros(dw31_ref.shape, F32)
            d3_ext[tm:tm + hs, :] = jnp.zeros((hs, d), F32)
            du1_ext[tm:tm + hc, :] = jnp.zeros((hc, d), F32)

        def col(g):
            return proj_ref[:, g * d:(g + 1) * d].astype(F32)

        def hcol(g, rows):
            v = halo_ref[HALO_CONF - rows:HALO_CONF, g * d:(g + 1) * d].astype(F32)
            return jnp.where(first_tile, 0.0, v)

        dx1v = dx1_ref[...]
        mixv = mix_ref[...].astype(F32)
        dmix = (mod_ref[M_G1:M_G1 + 1, :] * dx1v).astype(BF16)
        dmix_ref[...] = dmix
        sums_ref[0:1, :] = sums_ref[0:1, :] + _colsum(dx1v * mixv)
        dmerged = _dot_nt(dmix, wo_ref[...])
        ga = ga_ref[...].astype(F32)
        gb = gb_ref[...].astype(F32)
        yav = ya_ref[...].astype(F32)
        ybv = yb_ref[...].astype(F32)
        dya_f = dmerged * ga
        dyb_f = dmerged * gb
        dya = dya_f.astype(BF16)
        dyb = dyb_f.astype(BF16)
        dya_ref[...] = dya
        dyb_ref[...] = dyb
        dproj_ref[:, 5 * d:6 * d] = (dya_f * yav * (1.0 - ga)).astype(BF16)
        dproj_ref[:, 6 * d:7 * d] = (dyb_f * ybv * (1.0 - gb)).astype(BF16)

        dya_pre = _dot_nt(dya, wso_ref[...])
        c_s, v_s, b_s = col(1), col(2), col(0)
        cv_ext[0:hs, :] = hcol(1, hs) * hcol(2, hs)
        cv_ext[hs:hs + tm, :] = c_s * v_s
        dproj_ref[:, 0:d] = (dya_pre * c3_ref[...].astype(F32)).astype(BF16)
        d3_ext[0:tm, :] = dya_pre * b_s
        _tap_wgrad(d3_ext, cv_ext, _causal_taps(SHORT_K, hs), tm, hs, dw3_ref)
        _tap_conv(d3_ext, taps_ref, [(r3 + wi, off) for wi, off in _anticausal_taps(SHORT_K)], tm, hs, tmp_ref)
        dcv = tmp_ref[...]
        dproj_ref[:, d:2 * d] = (dcv * v_s).astype(BF16)
        dproj_ref[:, 2 * d:3 * d] = (dcv * c_s).astype(BF16)
        d3_ext[tm:tm + hs, :] = d3_ext[0:hs, :]

        du3 = _dot_nt(dyb, wco_ref[...])
        u1v = u1_ref[...].astype(F32)
        mu = _rowmean(u1v)
        uc = u1v - mu
        rstd = lax.rsqrt(_rowmean(uc * uc) + LN_EPS)
        uhat = uc * rstd
        lng = prm_ref[P_LNG:P_LNG + 1, :]
        u2 = uhat * lng + prm_ref[P_LNB:P_LNB + 1, :]
        s2 = _sigmoid(u2)
        du2 = du3 * (s2 * (1.0 + u2 * (1.0 - s2)))
        sums_ref[1:2, :] = sums_ref[1:2, :] + _colsum(du2 * uhat)
        sums_ref[2:3, :] = sums_ref[2:3, :] + _colsum(du2)
        duhat = du2 * lng
        du1 = rstd * (duhat - _rowmean(duhat) - uhat * _rowmean(duhat * uhat))
        sums_ref[3:4, :] = sums_ref[3:4, :] + _colsum(du1)
        du1_ext[0:tm, :] = du1
        v_c = col(3)
        sg = sg_ref[...].astype(F32)
        u0_ext[0:hc, :] = hcol(3, hc) * _sigmoid(hcol(4, hc))
        u0_ext[hc:hc + tm, :] = v_c * sg
        _tap_wgrad(du1_ext, u0_ext, _causal_taps(CONF_K, hc), tm, hc, dw31_ref)
        _tap_conv(du1_ext, taps_ref, [(r31 + wi, off) for wi, off in _anticausal_taps(CONF_K)], tm, hc, tmp_ref)
        dv_c = tmp_ref[...] * sg
        dproj_ref[:, 3 * d:4 * d] = dv_c.astype(BF16)
        dproj_ref[:, 4 * d:5 * d] = (dv_c * v_c * (1.0 - sg)).astype(BF16)
        du1_ext[tm:tm + hc, :] = du1_ext[0:hc, :]

        @pl.when(i == nt - 1)
        def _():
            scatter_finish()

    rev = lambda i: (nt - 1 - i, 0)
    tile = pl.BlockSpec((tm, d), rev)
    whole = lambda shape: pl.BlockSpec(shape, lambda i: (0,) * len(shape))
    hblocks = tm // HALO_CONF
    halo = pl.BlockSpec((HALO_CONF, 7 * d), lambda i: (jnp.maximum((nt - 1 - i) * hblocks - 1, 0), 0))
    bf = jax.ShapeDtypeStruct((t, d), BF16)
    res = pl.pallas_call(
        body, name="backward_mix", grid=(nt,),
        out_shape=(jax.ShapeDtypeStruct((t, 7 * d), BF16), bf, bf, bf,
                   jax.ShapeDtypeStruct((8, d), F32),
                   jax.ShapeDtypeStruct((SUBLANES * SHORT_K, d), F32),
                   jax.ShapeDtypeStruct((SUBLANES * CONF_K, d), F32))
        + tuple(jax.ShapeDtypeStruct(p.shape, p.dtype) for p in partials),
        in_specs=[tile, pl.BlockSpec((tm, 4 * d), rev), halo, tile, tile, tile, tile, tile, tile, tile, tile,
                  whole((8, d)), whole((8, d)), whole(taps.shape), whole((d, d)), whole((d, d)), whole((d, d))]
        + [ANY_SPEC] * na,
        out_specs=[pl.BlockSpec((tm, 7 * d), rev), tile, tile, tile,
                   whole((8, d)), whole((SUBLANES * SHORT_K, d)), whole((SUBLANES * CONF_K, d))] + [ANY_SPEC] * na,
        scratch_shapes=[pltpu.VMEM((hs + tm, d), F32), pltpu.VMEM((hc + tm, d), F32),
                        pltpu.VMEM((tm + hs, d), F32), pltpu.VMEM((tm + hc, d), F32),
                        pltpu.VMEM((tm, d), F32)] + _comm_scratch(na),
        compiler_params=_params(("arbitrary",)),
    )(dx1, proj, proj, y_a, y_b, mix, u1, conv3, sg, ga, gb, mod, prm, taps, w_so, w_co, w_o, *partials)
    return res[:7], res[7:]


def _backward_in(dproj, x, dx1, mod, prm, w_in_g, tm, partials, partial_view, exchanged_shapes):
    t, d = x.shape
    n_all = w_in_g.shape[1]
    na = len(partials)
    nt = t // tm

    def body(*refs):
        dproj_ref, x_ref, dx1_ref, mod_ref, prm_ref, w_hbm = refs[:6]
        gx_ref, sums_ref = refs[6 + na:8 + na]
        w_ref = refs[8 + 2 * na]
        scatter_start, scatter_finish = _scatter_plan(refs[6:6 + na], refs[8 + na:8 + 2 * na], *refs[9 + 2 * na:],
                                                      view=partial_view)

        @pl.when(pl.program_id(0) == 0)
        def _():
            scatter_start()
            pltpu.sync_copy(w_hbm, w_ref)
            sums_ref[...] = jnp.zeros((8, d), F32)

        dh = _dot_nt(dproj_ref[...], w_ref[...])
        xv = x_ref[...]
        r1 = lax.rsqrt(_rowmean(xv * xv) + EPS)
        xn = xv * r1
        gmix = prm_ref[P_GMIX:P_GMIX + 1, :]
        scale = 1.0 + mod_ref[M_SC1:M_SC1 + 1, :]
        dxn = dh * gmix * scale
        gx_ref[...] = dx1_ref[...] + r1 * (dxn - xn * _rowmean(dxn * xn))
        hx = dh * xn
        sums_ref[0:1, :] = sums_ref[0:1, :] + _colsum(dh)
        sums_ref[1:2, :] = sums_ref[1:2, :] + _colsum(hx) * gmix
        sums_ref[2:3, :] = sums_ref[2:3, :] + _colsum(hx) * scale

        @pl.when(pl.program_id(0) == nt - 1)
        def _():
            scatter_finish()

    tile = pl.BlockSpec((tm, d), lambda i: (i, 0))
    whole = pl.BlockSpec((8, d), lambda i: (0, 0))
    res = pl.pallas_call(
        body, name="backward_in", grid=(nt,),
        out_shape=(jax.ShapeDtypeStruct((t, d), F32), jax.ShapeDtypeStruct((8, d), F32))
        + tuple(jax.ShapeDtypeStruct(s, p.dtype) for p, s in zip(partials, exchanged_shapes)),
        in_specs=[pl.BlockSpec((tm, n_all), lambda i: (i, 0)), tile, tile, whole, whole, ANY_SPEC] + [ANY_SPEC] * na,
        out_specs=[tile, whole] + [ANY_SPEC] * na,
        scratch_shapes=[pltpu.VMEM(w_in_g.shape, BF16)] + _comm_scratch(na),
        compiler_params=_params(("arbitrary",)),
    )(dproj, x, dx1, mod, prm, w_in_g, *partials)
    return res[0], res[1], res[2:]


def _weight_grad(a, b, a_spec, b_spec, ns, m, n, nk, name, partials=(), by_columns=False):
    na = len(partials)

    def body(*refs):
        a_ref, b_ref = refs[:2]
        o_ref = refs[2 + na]
        acc_ref = refs[3 + 2 * na]
        s, k = pl.program_id(0), pl.program_id(1)
        if na:
            scatter_start, scatter_finish = _scatter_plan(refs[2:2 + na], refs[3 + na:3 + 2 * na], *refs[4 + 2 * na:])

            @pl.when((s == 0) & (k == 0))
            def _():
                scatter_start()

        av = a_ref[0] if len(a_ref.shape) == 3 else a_ref[...]
        bv = b_ref[0] if len(b_ref.shape) == 3 else b_ref[...]
        part = _dot_tn(av, bv)

        @pl.when(k == 0)
        def _():
            acc_ref[...] = part

        @pl.when(k > 0)
        def _():
            acc_ref[...] = acc_ref[...] + part

        @pl.when(k == nk - 1)
        def _():
            if by_columns:
                o_ref[...] = acc_ref[...].astype(BF16)
            else:
                o_ref[0] = acc_ref[...].astype(BF16)

        if na:
            @pl.when((s == ns - 1) & (k == nk - 1))
            def _():
                scatter_finish()

    if by_columns:
        out_shape, out_spec = (m, ns * n), pl.BlockSpec((m, n), lambda s, k: (0, s))
    else:
        out_shape, out_spec = (ns, m, n), pl.BlockSpec((1, m, n), lambda s, k: (s, 0, 0))
    res = pl.pallas_call(
        body, name=name, grid=(ns, nk),
        out_shape=(jax.ShapeDtypeStruct(out_shape, BF16),)
        + tuple(jax.ShapeDtypeStruct(p.shape, p.dtype) for p in partials),
        in_specs=[a_spec, b_spec] + [ANY_SPEC] * na,
        out_specs=[out_spec] + [ANY_SPEC] * na,
        scratch_shapes=[pltpu.VMEM((m, n), F32)] + (_comm_scratch(na) if na else []),
        compiler_params=_params(("arbitrary", "arbitrary")),
    )(a, b, *partials)
    return (res[0], res[1:]) if na else res[0]


def _adamw(w, g, m, v):
    m = ADAM_B1 * m + (1.0 - ADAM_B1) * g
    v = ADAM_B2 * v + (1.0 - ADAM_B2) * (g * g)
    m_hat = m / (1.0 - ADAM_B1 ** ADAM_STEP)
    v_hat = v / (1.0 - ADAM_B2 ** ADAM_STEP)
    delta = -ADAM_LR * (m_hat / (jnp.sqrt(v_hat) + ADAM_EPS) + ADAM_WD * w)
    return delta, m, v


def _adamw_shard(parts, w, m, v, tr, name):
    r, c = w.shape

    def body(p_ref, w_ref, m_ref, v_ref, g_ref, d_ref, nm_ref, nv_ref):
        g = p_ref[0].astype(F32)
        for s in range(1, N_DEV):
            g = g + p_ref[s].astype(F32)
        delta, nm, nv = _adamw(w_ref[...], g, m_ref[...], v_ref[...])
        g_ref[...] = g
        d_ref[...] = delta
        nm_ref[...] = nm
        nv_ref[...] = nv

    tile = pl.BlockSpec((tr, c), lambda i: (i, 0))
    return pl.pallas_call(
        body, name=name, grid=(r // tr,),
        out_shape=(jax.ShapeDtypeStruct((r, c), F32),) * 4,
        in_specs=[pl.BlockSpec((N_DEV, tr, c), lambda i: (0, i, 0)), tile, tile, tile],
        out_specs=[tile] * 4,
        compiler_params=_params(("arbitrary",)),
    )(parts, w, m, v)


def _ada_update(sc_all, dmod_cols, w, m, v, tr):
    d, n = w.shape

    def body(sc_ref, dm_ref, w_ref, m_ref, v_ref, g_ref, d_ref, nm_ref, nv_ref):
        g = lax.dot_general(sc_ref[...], dm_ref[...], (((0,), (0,)), ((), ())),
                            preferred_element_type=F32, precision=lax.Precision.HIGHEST)
        delta, nm, nv = _adamw(w_ref[...], g, m_ref[...], v_ref[...])
        g_ref[...] = g
        d_ref[...] = delta
        nm_ref[...] = nm
        nv_ref[...] = nv

    tile = pl.BlockSpec((tr, n), lambda i: (i, 0))
    return pl.pallas_call(
        body, name="ada_update", grid=(d // tr,),
        out_shape=(jax.ShapeDtypeStruct((d, n), F32),) * 4,
        in_specs=[pl.BlockSpec((N_DEV, tr), lambda i: (0, i)), pl.BlockSpec((N_DEV, n), lambda i: (0, 0)),
                  tile, tile, tile],
        out_specs=[tile] * 4,
        compiler_params=_params(("arbitrary",)),
    )(sc_all, dmod_cols, w, m, v)


def _small_exchange(vec, cg):
    l = vec.shape[2]
    rows = cg.shape[1]

    def body(vec_ref, cg_ref, vall_ref, cgr_ref, send_sems, recv_sems):
        x, y, c = _my_coords()
        me = _slot(x, y, c)
        vall_ref[me] = vec_ref[0]
        cgr_ref[me] = cg_ref[me]
        copies = []
        for k in range(1, N_DEV):
            peer = (_flip(x, k & 4), _flip(y, k & 2), _flip(c, k & 1))
            copies.append(pltpu.make_async_remote_copy(
                src_ref=vall_ref.at[me], dst_ref=vall_ref.at[me], send_sem=send_sems.at[k - 1],
                recv_sem=recv_sems.at[k - 1], device_id=peer, device_id_type=MESH))
            copies.append(pltpu.make_async_remote_copy(
                src_ref=cg_ref.at[_slot(*peer)], dst_ref=cgr_ref.at[me], send_sem=send_sems.at[7 + k - 1],
                recv_sem=recv_sems.at[7 + k - 1], device_id=peer, device_id_type=MESH))
        for cp in copies:
            cp.start()
        for cp in copies:
            cp.wait_recv()
        for cp in copies:
            cp.wait_send()

    vm = pl.BlockSpec(memory_space=pltpu.VMEM)
    return pl.pallas_call(
        body, name="small_exchange",
        out_shape=(jax.ShapeDtypeStruct((N_DEV, 1, l), F32), jax.ShapeDtypeStruct((N_DEV, rows, LANES), F32)),
        in_specs=[vm, vm], out_specs=[vm, vm],
        scratch_shapes=[pltpu.SemaphoreType.DMA((14,)), pltpu.SemaphoreType.DMA((14,))],
        compiler_params=_params(),
    )(vec, cg)


def _small_update(vall, cgr, smalls):
    ns = len(smalls)

    def body(*refs):
        vall_ref, cgr_ref = refs[0], refs[1]
        wmv = refs[2:2 + 3 * ns]
        outs = refs[2 + 3 * ns:]
        for p, (_, _, _, lo, hi, kind) in enumerate(smalls):
            w_ref, m_ref, v_ref = wmv[3 * p:3 * p + 3]
            part = (lambda s: vall_ref[s, :, lo:hi]) if kind == "vec" else (lambda s: cgr_ref[s, lo:hi, :])
            g = part(0)
            for s in range(1, N_DEV):
                g = g + part(s)
            delta, nm, nv = _adamw(w_ref[...], g, m_ref[...], v_ref[...])
            for o_ref, val in zip(outs[4 * p:4 * p + 4], (g, delta, nm, nv)):
                o_ref[...] = val

    vm = pl.BlockSpec(memory_space=pltpu.VMEM)
    args = [vall, cgr]
    out_shape = []
    for w, m, v, _, _, _ in smalls:
        args += [w, m, v]
        out_shape += [jax.ShapeDtypeStruct(w.shape, F32)] * 4
    res = pl.pallas_call(
        body, name="small_update",
        out_shape=tuple(out_shape),
        in_specs=[vm] * len(args), out_specs=[vm] * len(out_shape),
        compiler_params=_params(),
    )(*args)
    return [res[4 * p:4 + 4 * p] for p in range(ns)]


def _pick(t, want):
    return want if t % want == 0 else t


def kernel(x, c, w_ada, b_ada, norm_mix_g, w_in, conv_short_w, w_short_out, conv_conf_w, conv_conf_b, conf_ln_g, conf_ln_b, w_conf_out, w_o, norm_ffn_g, w_ffn_in, w_ffn_out, final_norm_g, loss_target, m_w_ada, m_b_ada, m_norm_mix_g, m_w_in, m_conv_short_w, m_w_short_out, m_conv_conf_w, m_conv_conf_b, m_conf_ln_g, m_conf_ln_b, m_w_conf_out, m_w_o, m_norm_ffn_g, m_w_ffn_in, m_w_ffn_out, m_final_norm_g, v_w_ada, v_b_ada, v_norm_mix_g, v_w_in, v_conv_short_w, v_w_short_out, v_conv_conf_w, v_conv_conf_b, v_conf_ln_g, v_conf_ln_b, v_w_conf_out, v_w_o, v_norm_ffn_g, v_w_ffn_in, v_w_ffn_out, v_final_norm_g):
    t, d = x.shape[1], x.shape[2]
    x2 = x.reshape(t, d)
    tgt = loss_target.reshape(t, d)
    me = _slot(*_my_coords())
    tm = _pick(t, 256)
    tm_fwd_in = _pick(t, 1024)
    tk = _pick(t, 2048)

    taps_loc = jnp.zeros((40, LANES), F32)
    taps_loc = taps_loc.at[0:SHORT_K].set(conv_short_w[0]).at[8:8 + CONF_K].set(conv_conf_w[0])
    prm = jnp.concatenate([norm_mix_g, norm_ffn_g, final_norm_g.reshape(1, d), conv_conf_b, conf_ln_g, conf_ln_b,
                           jnp.zeros((2, d), F32)], axis=0)

    n_in = w_in.shape[2]
    nk = t // tk
    tok = pl.BlockSpec((tk, d), lambda s, k: (k, 0))
    rows = d // N_DEV
    frows = w_ffn_out.shape[1]

    proj, h, w_in_g, mod, sc_all3, taps, (w_so_g, w_co_g, w_o_g) = _forward_in(
        jnp.reshape(me, (1,)).astype(jnp.int32), x2, prm, w_in[0].astype(BF16),
        (c, w_ada[0], b_ada, taps_loc), tm_fwd_in,
        [w_short_out[0].astype(BF16), w_conf_out[0].astype(BF16), w_o[0].astype(BF16)])
    w_so = w_so_g.reshape(d, d)
    w_co = w_co_g.reshape(d, d)
    w_oo = w_o_g.reshape(d, d)
    w_fi_t, m_fi_t, v_fi_t = w_ffn_in[0].T, m_w_ffn_in[0].T, v_w_ffn_in[0].T
    (x1, y_a, y_b, mix, u1, conv3, ya_pre, u3, merged, sg, ga, gb), (w_fi_g, w_fo_g) = _forward_mix(
        x2, proj, mod, prm, taps, w_so, w_co, w_oo, tm, [w_fi_t.astype(BF16), w_ffn_out[0].astype(BF16)])
    fb = w_fi_g.shape[1]
    ff = N_DEV * frows
    w_fi_all = w_fi_g.reshape(N_DEV * fb, d)
    w_fo_all = w_fo_g.reshape(ff, d)
    dx2, ab, h2, act, sums_f = _forward_ffn(x1, tgt, mod, prm, w_fi_all, w_fo_all, tm)

    dx1, df, dab, sums_b = _backward_ffn(dx2, x1, ab, mod, prm, w_fi_all, w_fo_all, tm)
    fb2 = 2 * fb
    pair_tok = pl.BlockSpec((tk, fb2), lambda s, k: (k, s))
    g_fi = _weight_grad(dab, h2, pair_tok, tok, N_DEV // 2, fb2, d, nk, "grad_w_ffn_in")
    g_fo = _weight_grad(act, df, pair_tok, tok, N_DEV // 4, fb2, d, nk, "grad_w_ffn_out")
    (dproj, dmix, dya, dyb, sums_m, dw3p, dw31p), (p_fi, p_fo) = _backward_mix(
        dx1, proj, y_a, y_b, mix, u1, conv3, sg, ga, gb, mod, prm, taps, w_so, w_co, w_oo, tm,
        [g_fi.reshape(N_DEV, fb, d), g_fo.reshape(N_DEV, frows, d)])
    tk_in = _pick(t, 4096)
    nk_in = t // tk_in
    tok_in = pl.BlockSpec((tk_in, d), lambda s, k: (k, 0))
    g_so = _weight_grad(ya_pre, dya, tok, tok, 1, d, d, nk, "grad_w_short_out")
    g_co = _weight_grad(u3, dyb, tok, tok, 1, d, d, nk, "grad_w_conf_out")
    g_oo = _weight_grad(merged, dmix, tok, tok, 1, d, d, nk, "grad_w_o")
    n_blk = _pick(N_DEV * n_in, d)
    g_in, (p_so, p_co, p_oo) = _weight_grad(
        h, dproj, tok_in, pl.BlockSpec((tk_in, n_blk), lambda s, k: (k, s)),
        N_DEV * n_in // n_blk, d, n_blk, nk_in, "grad_w_in",
        [g_so.reshape(N_DEV, rows, d), g_co.reshape(N_DEV, rows, d), g_oo.reshape(N_DEV, rows, d)], by_columns=True)
    grad_x, sums_i, (p_in,) = _backward_in(dproj, x2, dx1, mod, prm, w_in_g, tm, [g_in], _column_block(n_in),
                                           [(N_DEV, d, n_in)])

    up_in = _adamw_shard(p_in, w_in[0], m_w_in[0], v_w_in[0], _pick(d, 256), "adamw_w_in")
    up_so = _adamw_shard(p_so, w_short_out[0], m_w_short_out[0], v_w_short_out[0], rows, "adamw_w_short_out")
    up_co = _adamw_shard(p_co, w_conf_out[0], m_w_conf_out[0], v_w_conf_out[0], rows, "adamw_w_conf_out")
    up_oo = _adamw_shard(p_oo, w_o[0], m_w_o[0], v_w_o[0], rows, "adamw_w_o")
    up_fi = tuple(a.T for a in _adamw_shard(p_fi, w_fi_t, m_fi_t, v_fi_t, fb // 2, "adamw_w_ffn_in"))
    up_fo = _adamw_shard(p_fo, w_ffn_out[0], m_w_ffn_out[0], v_w_ffn_out[0], frows, "adamw_w_ffn_out")

    vec = jnp.concatenate([sums_i[0:2], sums_m[0:1], sums_b[0:2], sums_f[1:2],
                           sums_i[2:3], sums_m[3:4], sums_m[1:3], sums_b[2:3], sums_f[0:1],
                           sums_f[2:3]], axis=0)
    vec = vec.reshape(1, 1, 13 * d)
    dw3 = dw3p.reshape(SHORT_K, SUBLANES, d).sum(axis=1)
    dw31 = dw31p.reshape(CONF_K, SUBLANES, d).sum(axis=1)
    cg = jnp.zeros((40, d), F32).at[0:SHORT_K].set(dw3).at[8:8 + CONF_K].set(dw31)
    cg = cg.reshape(40, N_DEV, LANES).transpose(1, 0, 2)
    fin = lambda a: a.reshape(1, d)
    tap = lambda a: a.reshape(a.shape[1:])
    smalls = [
        (b_ada, m_b_ada, v_b_ada, 0, 6 * d, "vec"),
        (norm_mix_g, m_norm_mix_g, v_norm_mix_g, 6 * d, 7 * d, "vec"),
        (tap(conv_short_w), tap(m_conv_short_w), tap(v_conv_short_w), 0, SHORT_K, "cg"),
        (tap(conv_conf_w), tap(m_conv_conf_w), tap(v_conv_conf_w), 8, 8 + CONF_K, "cg"),
        (conv_conf_b, m_conv_conf_b, v_conv_conf_b, 7 * d, 8 * d, "vec"),
        (conf_ln_g, m_conf_ln_g, v_conf_ln_g, 8 * d, 9 * d, "vec"),
        (conf_ln_b, m_conf_ln_b, v_conf_ln_b, 9 * d, 10 * d, "vec"),
        (norm_ffn_g, m_norm_ffn_g, v_norm_ffn_g, 10 * d, 11 * d, "vec"),
        (fin(final_norm_g), fin(m_final_norm_g), fin(v_final_norm_g), 11 * d, 12 * d, "vec"),
    ]
    vall, cgr = _small_exchange(vec, cg)
    up_small = _small_update(vall, cgr, smalls)
    n_ada = w_ada.shape[2]
    dmod_all = vall.reshape(N_DEV, 13 * d)[:, 0:6 * d]
    dmod_cols = lax.dynamic_slice(dmod_all, (0, me * n_ada), (N_DEV, n_ada))
    up_ada = _ada_update(sc_all3.reshape(N_DEV, d), dmod_cols, w_ada[0], m_w_ada[0], v_w_ada[0], _pick(d, 256))

    loss = jnp.sum(vall.reshape(N_DEV, 13 * d)[:, 12 * d:])

    lead = lambda a: a.reshape((1,) + a.shape)
    ups = [tuple(lead(a) for a in up_ada), up_small[0], up_small[1], tuple(lead(a) for a in up_in),
           tuple(lead(a) for a in up_small[2]), tuple(lead(a) for a in up_so), tuple(lead(a) for a in up_small[3]),
           up_small[4], up_small[5], up_small[6],
           tuple(lead(a) for a in up_co), tuple(lead(a) for a in up_oo), up_small[7],
           tuple(lead(a) for a in up_fi), tuple(lead(a) for a in up_fo),
           tuple(a.reshape(d) for a in up_small[8])]
    grads = [u[0] for u in ups]
    deltas = [u[1] for u in ups]
    new_m = [u[2] for u in ups]
    new_v = [u[3] for u in ups]
    return (loss, grad_x.reshape(1, t, d), *grads, *deltas, *new_m, *new_v)
```

```python
import functools

import jax
import jax.numpy as jnp
from jax import lax
from jax.experimental import pallas as pl
from jax.experimental.pallas import tpu as pltpu

F32 = jnp.float32
BF16 = jnp.bfloat16
MESH = pl.DeviceIdType.MESH

N_DEV = 8
EPS = 1e-6
LN_EPS = 1e-5
SHORT_K = 3
CONF_K = 31
ADAM_LR = 0.001
ADAM_B1 = 0.9
ADAM_B2 = 0.999
ADAM_EPS = 1e-08
ADAM_WD = 0.01
ADAM_STEP = 10

LANES = 128
SUBLANES = 8
CONV_ROWS = 64
HALO_SHORT = 8
HALO_CONF = 32
VMEM_LIMIT = 56 * 1024 * 1024

M_SH1, M_SC1, M_G1, M_SH2, M_SC2, M_G2 = range(6)
P_GMIX, P_GFFN, P_GFIN, P_CBIAS, P_LNG, P_LNB = range(6)


def _params(sem=None, **kw):
    return pltpu.CompilerParams(dimension_semantics=sem, vmem_limit_bytes=VMEM_LIMIT, **kw)


def _sigmoid(v):
    return jax.nn.sigmoid(v)


def _dot(a, b):
    return jnp.dot(a, b, preferred_element_type=F32)


def _dot_nt(a, b):
    return lax.dot_general(a, b, (((1,), (1,)), ((), ())), preferred_element_type=F32)


def _dot_tn(a, b):
    return lax.dot_general(a, b, (((0,), (0,)), ((), ())), preferred_element_type=F32)


def _colsum(v):
    return jnp.sum(v, axis=0, keepdims=True)


def _rowmean(v):
    return jnp.mean(v, axis=-1, keepdims=True)


def _my_coords():
    return lax.axis_index("x"), lax.axis_index("y"), lax.axis_index("c")


def _slot(px, py, pc):
    return 4 * px + 2 * py + pc


def _flip(v, bit):
    return 1 - v if bit else v


def _taps_by_residue(taps):
    by_res = {}
    for wi, off in taps:
        by_res.setdefault(off % SUBLANES, []).append((wi, off // SUBLANES))
    return sorted(by_res.items())


def _tap_conv(ext_ref, w_ref, taps, tm, extra, out_ref):
    d = out_ref.shape[1]
    rb = min(CONV_ROWS, tm)
    wrows = rb + extra
    groups = _taps_by_residue(taps)

    def block(i, carry):
        base = pl.multiple_of(i * rb, SUBLANES)
        for lc in range(d // LANES):
            ls = pl.ds(lc * LANES, LANES)
            win = ext_ref[pl.ds(base, wrows), ls]
            acc = None
            for r, lst in groups:
                sh = win if r == 0 else pltpu.roll(win, wrows - r, 0)
                for wi, q in lst:
                    term = w_ref[wi:wi + 1, ls] * sh[SUBLANES * q:SUBLANES * q + rb, :]
                    acc = term if acc is None else acc + term
            out_ref[pl.ds(base, rb), ls] = acc
        return carry

    lax.fori_loop(0, tm // rb, block, 0)


def _tap_wgrad(a_ref, ext_ref, taps, tm, extra, acc_ref):
    d = a_ref.shape[1]
    rb = min(CONV_ROWS, tm)
    wrows = rb + extra
    groups = _taps_by_residue(taps)

    def block(i, carry):
        base = pl.multiple_of(i * rb, SUBLANES)
        for lc in range(d // LANES):
            ls = pl.ds(lc * LANES, LANES)
            a_blk = a_ref[pl.ds(base, rb), ls]
            win = ext_ref[pl.ds(base, wrows), ls]
            for r, lst in groups:
                sh = win if r == 0 else pltpu.roll(win, wrows - r, 0)
                for wi, q in lst:
                    prod = a_blk * sh[SUBLANES * q:SUBLANES * q + rb, :]
                    part = prod[0:SUBLANES, :]
                    for s in range(1, rb // SUBLANES):
                        part = part + prod[SUBLANES * s:SUBLANES * (s + 1), :]
                    rows = pl.ds(SUBLANES * wi, SUBLANES)
                    acc_ref[rows, ls] = acc_ref[rows, ls] + part
        return carry

    lax.fori_loop(0, tm // rb, block, 0)


def _causal_taps(k, halo):
    return [(i, halo - (k - 1) + i) for i in range(k)]


def _anticausal_taps(k):
    return [(i, (k - 1) - i) for i in range(k)]


def _ada_scratch(d, nloc, trows):
    return [pltpu.VMEM((N_DEV, 1, d), F32), pltpu.VMEM((N_DEV, 1, nloc), F32), pltpu.VMEM((N_DEV, 1, nloc), F32),
            pltpu.VMEM((N_DEV, trows, LANES), F32), pltpu.SemaphoreType.DMA((21,)), pltpu.SemaphoreType.DMA((21,))]


def _ada_exchange(c_ref, w_ref, b_ref, t_ref, mod_ref, sc_ref, taps_ref,
                  scall_ref, part_ref, modrecv_ref, tapsall_ref, send_sems, recv_sems):
    x, y, cc = _my_coords()
    me = _slot(x, y, cc)
    d = c_ref.shape[1]
    cv = c_ref[...]
    scall_ref[me] = cv * _sigmoid(cv)
    tapsall_ref[me] = t_ref[...]

    def peer_of(k):
        return (_flip(x, k & 4), _flip(y, k & 2), _flip(cc, k & 1))

    def gather_copy(ref, base, k):
        return pltpu.make_async_remote_copy(
            src_ref=ref.at[me], dst_ref=ref.at[me], send_sem=send_sems.at[base + k - 1],
            recv_sem=recv_sems.at[base + k - 1], device_id=peer_of(k), device_id_type=MESH)

    first = [gather_copy(scall_ref, 0, k) for k in range(1, N_DEV)]
    first += [gather_copy(tapsall_ref, 7, k) for k in range(1, N_DEV)]
    for cp in first:
        cp.start()
    for cp in first[:7]:
        cp.wait_recv()
    sc_all = jnp.concatenate([scall_ref[s] for s in range(N_DEV)], axis=0)
    for s in range(N_DEV):
        sc_ref[s] = scall_ref[s]
    part = jnp.dot(sc_all, w_ref[...], preferred_element_type=F32,
                   precision=lax.Precision.HIGHEST)
    for b in range(N_DEV):
        part_ref[b] = part[b:b + 1, :]
    modrecv_ref[me] = part_ref[me]
    second = []
    for k in range(1, N_DEV):
        px, py, pc = peer_of(k)
        second.append(pltpu.make_async_remote_copy(
            src_ref=part_ref.at[_slot(px, py, pc)], dst_ref=modrecv_ref.at[me],
            send_sem=send_sems.at[14 + k - 1], recv_sem=recv_sems.at[14 + k - 1],
            device_id=(px, py, pc), device_id_type=MESH))
    for cp in second:
        cp.start()
    for cp in second:
        cp.wait_recv()
    mod = jnp.concatenate([modrecv_ref[s] for s in range(N_DEV)], axis=1) + b_ref[...]
    for r in range(6):
        mod_ref[r:r + 1, :] = mod[:, r * d:(r + 1) * d]
    mod_ref[6:8, :] = jnp.zeros((2, d), F32)
    for cp in first[7:]:
        cp.wait_recv()
    taps_ref[...] = jnp.concatenate([tapsall_ref[s] for s in range(N_DEV)], axis=1)
    for cp in first + second:
        cp.wait_send()


ANY_SPEC = pl.BlockSpec(memory_space=pl.ANY)


def _comm_scratch(na):
    return [pltpu.SemaphoreType.DMA((7 * na,)), pltpu.SemaphoreType.DMA((7 * na,)), pltpu.SemaphoreType.DMA((na,))]


def _leading_block(ref, slot):
    return ref.at[slot]


def _column_block(width):
    def view(ref, slot):
        return ref.at[:, pl.ds(pl.multiple_of(slot * width, LANES), width)]
    return view


class _Gather:
    def __init__(self, ins, outs, send_sems, recv_sems, local_sems, view=_leading_block):
        self.na = len(ins)
        x, y, c = _my_coords()
        self.c = c
        self.view = view
        self.me, self.sibling = (x, y, c), (x, y, 1 - c)
        self.chips = [(1 - x, y), (x, 1 - y), (1 - x, 1 - y)]
        self.outs, self.send_sems, self.recv_sems = outs, send_sems, recv_sems
        self.mine = [pltpu.make_async_copy(ins[a], view(outs[a], _slot(*self.me)), local_sems.at[a])
                     for a in range(self.na)]
        self.first = []
        for a in range(self.na):
            self.first.append(self._copy(a, 0, self.me, self.sibling, src=ins[a]))
            self.first += [self._copy(a, 1 + j, self.me, (*chip, c), src=ins[a]) for j, chip in enumerate(self.chips)]
        self.passed = [self._copy(a, 4 + j, (*chip, c), self.sibling)
                       for a in range(self.na) for j, chip in enumerate(self.chips)]

    def _copy(self, a, k, block, to, src=None):
        dst = self.view(self.outs[a], _slot(*block))
        return pltpu.make_async_remote_copy(
            src_ref=dst if src is None else src, dst_ref=dst,
            send_sem=self.send_sems.at[7 * a + k], recv_sem=self.recv_sems.at[7 * a + k],
            device_id=to, device_id_type=MESH)

    def start(self):
        self.start_near()
        self.start_far()

    def start_near(self):
        for cp in self.mine + [cp for n, cp in enumerate(self.first) if n % 4 != 3]:
            cp.start()

    def start_far(self):
        for cp in self.first[3::4]:
            cp.start()

    def forward_chip(self, j):
        for a in range(self.na):
            self._copy(a, 1 + j, (*self.chips[j], self.c), self.me).wait_recv()
            self.passed[3 * a + j].start()

    def wait_sibling_own(self):
        for a in range(self.na):
            self._copy(a, 0, self.sibling, self.me).wait_recv()

    def wait_sibling_passed(self, j):
        for a in range(self.na):
            self._copy(a, 4 + j, (*self.chips[j], 1 - self.c), self.me).wait_recv()

    def finish_sends(self):
        for cp in self.first + self.passed:
            cp.wait_send()
        for cp in self.mine:
            cp.wait()

    def forward(self):
        for j in range(3):
            self.forward_chip(j)

    def finish(self):
        self.wait_sibling_own()
        for j in range(3):
            self.wait_sibling_passed(j)
        self.finish_sends()


def _scatter_plan(ins, outs, send_sems, recv_sems, local_sems, view=_leading_block):
    na = len(ins)
    x, y, c = _my_coords()
    me = _slot(x, y, c)
    mine = [pltpu.make_async_copy(view(ins[a], me), outs[a].at[me], local_sems.at[a]) for a in range(na)]
    copies = []
    for k in range(1, N_DEV):
        peer = (_flip(x, k & 4), _flip(y, k & 2), _flip(c, k & 1))
        for a in range(na):
            copies.append(pltpu.make_async_remote_copy(
                src_ref=view(ins[a], _slot(*peer)), dst_ref=outs[a].at[me],
                send_sem=send_sems.at[7 * a + k - 1], recv_sem=recv_sems.at[7 * a + k - 1],
                device_id=peer, device_id_type=MESH))

    def start():
        for cp in mine + copies:
            cp.start()

    def finish():
        for cp in copies:
            cp.wait_recv()
        for cp in copies:
            cp.wait_send()
        for cp in mine:
            cp.wait()

    return start, finish


def _forward_in(me_arr, x, prm, w_in_loc, ada, tm, shards):
    t, d = x.shape
    n = w_in_loc.shape[1]
    ns = N_DEV
    na = len(shards)
    nt = t // tm
    nloc = ada[1].shape[1]
    trows = ada[3].shape[0]

    def body(*refs):
        x_ref, prm_ref, wloc_ref = refs[1:4]
        ada_in = refs[4:8]
        shard_refs = refs[8:8 + na]
        proj_ref, h_ref, wg_ref, mod_ref, sc_ref, taps_ref = refs[8 + na:14 + na]
        gathered_refs = refs[14 + na:14 + 2 * na]
        hall_ref, wv_ref, wv_sem = refs[14 + 2 * na:17 + 2 * na]
        ada_scratch = refs[17 + 2 * na:23 + 2 * na]
        sems = refs[23 + 2 * na:]
        s, i = pl.program_id(0), pl.program_id(1)
        columns = _column_block(n)

        def gathers():
            return (_Gather([wloc_ref], [wg_ref], *sems[0:3], view=columns),
                    _Gather(shard_refs, gathered_refs, *sems[3:6]))

        def shard_copy(src, step):
            return pltpu.make_async_copy(src, wv_ref.at[step % 2], wv_sem.at[step % 2])

        @pl.when(i == 0)
        def _():
            me = _slot(*_my_coords())

            @pl.when(s == 0)
            def _():
                g_in, _ = gathers()
                _ada_exchange(*ada_in, mod_ref, sc_ref, taps_ref, *ada_scratch)
                g_in.start_near()
                own = shard_copy(wloc_ref, 0)
                own.start()
                own.wait()

            for step in range(1, ns):
                @pl.when(s == step)
                def _(step=step):
                    shard_copy(columns(wg_ref, me ^ step), step).wait()

        @pl.when(i == nt - 1)
        def _():
            g_in, g_rest = gathers()
            me = _slot(*_my_coords())
            arrivals = {1: g_in.wait_sibling_own,
                        2: functools.partial(g_in.forward_chip, 1), 3: functools.partial(g_in.wait_sibling_passed, 1),
                        4: functools.partial(g_in.forward_chip, 0), 5: functools.partial(g_in.wait_sibling_passed, 0),
                        6: functools.partial(g_in.forward_chip, 2), 7: functools.partial(g_in.wait_sibling_passed, 2)}
            for step, arrive in arrivals.items():
                @pl.when(s == step - 1)
                def _(step=step, arrive=arrive):
                    arrive()
                    if step == 2:
                        g_in.start_far()
                    if step == 4:
                        g_rest.start()
                    if step == 7:
                        g_rest.forward()
                    shard_copy(columns(wg_ref, me ^ step), step).start()

        @pl.when(s == 0)
        def _():
            xv = x_ref[...]
            r = lax.rsqrt(_rowmean(xv * xv) + EPS)
            h = xv * r * prm_ref[P_GMIX:P_GMIX + 1, :] * (1.0 + mod_ref[M_SC1:M_SC1 + 1, :]) \
                + mod_ref[M_SH1:M_SH1 + 1, :]
            hb = h.astype(BF16)
            hall_ref[i] = hb
            h_ref[...] = hb

        proj_ref[...] = _dot(hall_ref[i], wv_ref[s % 2]).astype(BF16)

        @pl.when((s == ns - 1) & (i == nt - 1))
        def _():
            g_in, g_rest = gathers()
            g_in.finish_sends()
            g_rest.finish()

    x_tile = pl.BlockSpec((tm, d), lambda s, i, me: (jnp.where(s == 0, i, nt - 1), 0))
    whole = lambda a: pl.BlockSpec(a.shape, lambda s, i, me: (0,) * len(a.shape))
    small_out = (jax.ShapeDtypeStruct((8, d), F32), jax.ShapeDtypeStruct((N_DEV, 1, d), F32),
                 jax.ShapeDtypeStruct((trows, N_DEV * LANES), F32))
    res = pl.pallas_call(
        body, name="forward_in",
        grid_spec=pltpu.PrefetchScalarGridSpec(
            num_scalar_prefetch=1, grid=(ns, nt),
            in_specs=[x_tile, whole(prm), ANY_SPEC] + [whole(a) for a in ada] + [ANY_SPEC] * na,
            out_specs=[pl.BlockSpec((tm, n), lambda s, i, me: (i, me[0] ^ s)), x_tile, ANY_SPEC]
            + [whole(a) for a in small_out] + [ANY_SPEC] * na,
            scratch_shapes=[pltpu.VMEM((nt, tm, d), BF16), pltpu.VMEM((2, d, n), BF16), pltpu.SemaphoreType.DMA((2,))]
            + _ada_scratch(d, nloc, trows) + _comm_scratch(1) + _comm_scratch(na)),
        out_shape=(jax.ShapeDtypeStruct((t, ns * n), BF16), jax.ShapeDtypeStruct((t, d), BF16),
                   jax.ShapeDtypeStruct((d, ns * n), BF16)) + small_out
        + tuple(jax.ShapeDtypeStruct((N_DEV,) + a.shape, a.dtype) for a in shards),
        compiler_params=_params(("arbitrary", "arbitrary")),
    )(me_arr, x, prm, w_in_loc, *ada, *shards)
    return res[0], res[1], res[2], res[3], res[4], res[5], res[6:]


def _forward_mix(x, proj, mod, prm, taps, w_so, w_co, w_o, tm, shards):
    t, d = x.shape
    nt = t // tm
    na = len(shards)
    hs, hc = HALO_SHORT, HALO_CONF
    r3, r31 = 0, 8

    def body(*refs):
        x_ref, proj_ref, mod_ref, prm_ref, taps_ref, wso_ref, wco_ref, wo_ref = refs[:8]
        (x1_ref, ya_ref, yb_ref, mix_ref, u1_ref, c3_ref, sg_ref, ga_ref, gb_ref, lhs_ref) = refs[8 + na:18 + na]
        cv_ext, u0_ext, conv3_ref, u1f_ref = refs[18 + 2 * na:22 + 2 * na]
        gather = _Gather(refs[8:8 + na], refs[18 + na:18 + 2 * na], *refs[22 + 2 * na:])
        i = pl.program_id(0)

        @pl.when(i == 0)
        def _():
            gather.start()
            cv_ext[0:hs, :] = jnp.zeros((hs, d), F32)
            u0_ext[0:hc, :] = jnp.zeros((hc, d), F32)

        @pl.when(i == (3 * nt) // 4)
        def _():
            gather.forward()

        def col(g):
            return proj_ref[:, g * d:(g + 1) * d].astype(F32)

        cv_ext[hs:hs + tm, :] = col(1) * col(2)
        sg = _sigmoid(col(4))
        sg_ref[...] = sg.astype(BF16)
        u0_ext[hc:hc + tm, :] = col(3) * sg
        _tap_conv(cv_ext, taps_ref, [(r3 + wi, off) for wi, off in _causal_taps(SHORT_K, hs)], tm, hs, conv3_ref)
        _tap_conv(u0_ext, taps_ref, [(r31 + wi, off) for wi, off in _causal_taps(CONF_K, hc)], tm, hc, u1f_ref)
        cv_ext[0:hs, :] = cv_ext[tm:tm + hs, :]
        u0_ext[0:hc, :] = u0_ext[tm:tm + hc, :]

        c3_ref[...] = conv3_ref[...].astype(BF16)
        ya_pre = (col(0) * c3_ref[...].astype(F32)).astype(BF16)
        lhs_ref[0] = ya_pre
        y_a = _dot(ya_pre, wso_ref[...])
        u1 = u1f_ref[...] + prm_ref[P_CBIAS:P_CBIAS + 1, :]
        u1_ref[...] = u1.astype(BF16)
        u1 = u1_ref[...].astype(F32)
        mu = _rowmean(u1)
        uc = u1 - mu
        rstd = lax.rsqrt(_rowmean(uc * uc) + LN_EPS)
        u2 = uc * rstd * prm_ref[P_LNG:P_LNG + 1, :] + prm_ref[P_LNB:P_LNB + 1, :]
        u3 = (u2 * _sigmoid(u2)).astype(BF16)
        lhs_ref[1] = u3
        y_b = _dot(u3, wco_ref[...])
        ya_ref[...] = y_a.astype(BF16)
        yb_ref[...] = y_b.astype(BF16)
        ga = _sigmoid(col(5))
        gb = _sigmoid(col(6))
        ga_ref[...] = ga.astype(BF16)
        gb_ref[...] = gb.astype(BF16)
        merged = (ga * ya_ref[...].astype(F32) + gb * yb_ref[...].astype(F32)).astype(BF16)
        lhs_ref[2] = merged
        mix = _dot(merged, wo_ref[...])
        mix_ref[...] = mix.astype(BF16)
        x1_ref[...] = x_ref[...] + mod_ref[M_G1:M_G1 + 1, :] * mix

        @pl.when(i == nt - 1)
        def _():
            gather.finish()

    tile = pl.BlockSpec((tm, d), lambda i: (i, 0))
    whole = lambda shape: pl.BlockSpec(shape, lambda i: (0,) * len(shape))
    res = pl.pallas_call(
        body, name="forward_mix", grid=(nt,),
        out_shape=(jax.ShapeDtypeStruct((t, d), F32),) + (jax.ShapeDtypeStruct((t, d), BF16),) * 8
        + (jax.ShapeDtypeStruct((3, t, d), BF16),)
        + tuple(jax.ShapeDtypeStruct((N_DEV,) + a.shape, a.dtype) for a in shards),
        in_specs=[tile, pl.BlockSpec((tm, 7 * d), lambda i: (i, 0)), whole((8, d)), whole((8, d)),
                  whole(taps.shape), whole((d, d)), whole((d, d)), whole((d, d))] + [ANY_SPEC] * na,
        out_specs=[tile] * 9 + [pl.BlockSpec((3, tm, d), lambda i: (0, i, 0))] + [ANY_SPEC] * na,
        scratch_shapes=[pltpu.VMEM((hs + tm, d), F32), pltpu.VMEM((hc + tm, d), F32),
                        pltpu.VMEM((tm, d), F32), pltpu.VMEM((tm, d), F32)] + _comm_scratch(na),
        compiler_params=_params(("arbitrary",)),
    )(x, proj, mod, prm, taps, w_so, w_co, w_o, *shards)
    return res[:10], res[10:]


def _ffn_chunks(ff):
    mxu = 2 * LANES
    cut = (ff // mxu + 1) // 2 * mxu
    return [(0, cut), (cut, ff)] if 0 < cut < ff and ff % mxu == 0 else [(0, ff)]


def _forward_ffn(x1, tgt, mod, prm, w_fi_t, w_fo, tm):
    t, d = x1.shape
    ff = w_fo.shape[0]

    def body(x1_ref, tgt_ref, mod_ref, prm_ref, wfi_hbm, wfo_hbm,
             dx2_ref, ab_ref, h2_ref, act_ref, sums_ref, wfi_ref, wfo_ref):
        i = pl.program_id(0)

        @pl.when(i == 0)
        def _():
            pltpu.sync_copy(wfi_hbm, wfi_ref)
            pltpu.sync_copy(wfo_hbm, wfo_ref)
            sums_ref[...] = jnp.zeros((8, d), F32)

        x1v = x1_ref[...]
        r2 = lax.rsqrt(_rowmean(x1v * x1v) + EPS)
        h2 = (x1v * r2 * prm_ref[P_GFFN:P_GFFN + 1, :] * (1.0 + mod_ref[M_SC2:M_SC2 + 1, :])
              + mod_ref[M_SH2:M_SH2 + 1, :]).astype(BF16)
        h2_ref[...] = h2
        f = jnp.zeros((tm, d), F32)
        for c0, c1 in _ffn_chunks(ff):
            ab_ref[:, c0:c1] = _dot_nt(h2, wfi_ref[c0:c1, :]).astype(BF16)
            ab_ref[:, ff + c0:ff + c1] = _dot_nt(h2, wfi_ref[ff + c0:ff + c1, :]).astype(BF16)
            a = ab_ref[:, c0:c1].astype(F32)
            act = (a * _sigmoid(a) * ab_ref[:, ff + c0:ff + c1].astype(F32)).astype(BF16)
            act_ref[:, c0:c1] = act
            f = f + _dot(act, wfo_ref[c0:c1, :])
        x2 = x1v + mod_ref[M_G2:M_G2 + 1, :] * f
        r3 = lax.rsqrt(_rowmean(x2 * x2) + EPS)
        xn3 = x2 * r3
        gfin = prm_ref[P_GFIN:P_GFIN + 1, :]
        err = xn3 * gfin - tgt_ref[...]
        dy = err * (1.0 / d)
        dxn3 = dy * gfin
        dx2 = r3 * (dxn3 - xn3 * _rowmean(dxn3 * xn3))
        dx2_ref[...] = dx2
        sums_ref[0:1, :] = sums_ref[0:1, :] + _colsum(dy * xn3)
        sums_ref[1:2, :] = sums_ref[1:2, :] + _colsum(dx2 * f)
        sums_ref[2:3, :] = sums_ref[2:3, :] + _colsum(err * err) * (0.5 / d)

    tile = pl.BlockSpec((tm, d), lambda i: (i, 0))
    whole = lambda shape: pl.BlockSpec(shape, lambda i: (0,) * len(shape))
    return pl.pallas_call(
        body, name="forward_ffn", grid=(t // tm,),
        out_shape=(jax.ShapeDtypeStruct((t, d), F32), jax.ShapeDtypeStruct((t, 2 * ff), BF16),
                   jax.ShapeDtypeStruct((t, d), BF16), jax.ShapeDtypeStruct((t, ff), BF16),
                   jax.ShapeDtypeStruct((8, d), F32)),
        in_specs=[tile, tile, whole((8, d)), whole((8, d)), ANY_SPEC, ANY_SPEC],
        out_specs=[tile, pl.BlockSpec((tm, 2 * ff), lambda i: (i, 0)), tile,
                   pl.BlockSpec((tm, ff), lambda i: (i, 0)), whole((8, d))],
        scratch_shapes=[pltpu.VMEM(w_fi_t.shape, BF16), pltpu.VMEM(w_fo.shape, BF16)],
        compiler_params=_params(("arbitrary",)),
    )(x1, tgt, mod, prm, w_fi_t, w_fo)


def _backward_ffn(dx2, x1, ab, mod, prm, w_fi_t, w_fo, tm):
    t, d = x1.shape
    ff = w_fo.shape[0]

    def body(dx2_ref, x1_ref, ab_ref, mod_ref, prm_ref, wfi_hbm, wfo_hbm,
             dx1_ref, df_ref, dab_ref, sums_ref, wfi_ref, wfo_ref):
        i = pl.program_id(0)

        @pl.when(i == 0)
        def _():
            pltpu.sync_copy(wfi_hbm, wfi_ref)
            pltpu.sync_copy(wfo_hbm, wfo_ref)
            sums_ref[...] = jnp.zeros((8, d), F32)

        dx2v = dx2_ref[...]
        df = (mod_ref[M_G2:M_G2 + 1, :] * dx2v).astype(BF16)
        df_ref[...] = df
        dh2 = jnp.zeros((tm, d), F32)
        for c0, c1 in _ffn_chunks(ff):
            dact = _dot_nt(df, wfo_ref[c0:c1, :])
            a = ab_ref[:, c0:c1].astype(F32)
            b = ab_ref[:, ff + c0:ff + c1].astype(F32)
            s = _sigmoid(a)
            sil = a * s
            da =(dact * b * (s * (1.0 + a * (1.0 - s)))).astype(BF16)
            db = (dact * sil).astype(BF16)
            dab_ref[:, c0:c1] = da
            dab_ref[:, ff + c0:ff + c1] = db
            dh2 = dh2 + _dot(da, wfi_ref[c0:c1, :]) + _dot(db, wfi_ref[ff + c0:ff + c1, :])
        x1v = x1_ref[...]
        r2 = lax.rsqrt(_rowmean(x1v * x1v) + EPS)
        xn2 = x1v * r2
        gffn = prm_ref[P_GFFN:P_GFFN + 1, :]
        scale = 1.0 + mod_ref[M_SC2:M_SC2 + 1, :]
        dxn2 = dh2 * gffn * scale
        dx1_ref[...] = dx2v + r2 * (dxn2 - xn2 * _rowmean(dxn2 * xn2))
        hx = dh2 * xn2
        sums_ref[0:1, :] = sums_ref[0:1, :] + _colsum(dh2)
        sums_ref[1:2, :] = sums_ref[1:2, :] + _colsum(hx) * gffn
        sums_ref[2:3, :] = sums_ref[2:3, :] + _colsum(hx) * scale

    tile = pl.BlockSpec((tm, d), lambda i: (i, 0))
    whole = lambda shape: pl.BlockSpec(shape, lambda i: (0,) * len(shape))
    wide = lambda n: pl.BlockSpec((tm, n), lambda i: (i, 0))
    return pl.pallas_call(
        body, name="backward_ffn", grid=(t // tm,),
        out_shape=(jax.ShapeDtypeStruct((t, d), F32), jax.ShapeDtypeStruct((t, d), BF16),
                   jax.ShapeDtypeStruct((t, 2 * ff), BF16), jax.ShapeDtypeStruct((8, d), F32)),
        in_specs=[tile, tile, wide(2 * ff), whole((8, d)), whole((8, d)), ANY_SPEC, ANY_SPEC],
        out_specs=[tile, tile, wide(2 * ff), whole((8, d))],
        scratch_shapes=[pltpu.VMEM(w_fi_t.shape, BF16), pltpu.VMEM(w_fo.shape, BF16)],
        compiler_params=_params(("arbitrary",)),
    )(dx2, x1, ab, mod, prm, w_fi_t, w_fo)


def _backward_mix(dx1, proj, y_a, y_b, mix, u1, conv3, sg, ga, gb, mod, prm, taps, w_so, w_co, w_o, tm, partials):
    t, d = dx1.shape
    nt = t // tm
    na = len(partials)
    hs, hc = HALO_SHORT, HALO_CONF
    r3, r31 = 0, 8

    def body(*refs):
        (dx1_ref, proj_ref, halo_ref, ya_ref, yb_ref, mix_ref, u1_ref, c3_ref, sg_ref, ga_ref, gb_ref,
         mod_ref, prm_ref, taps_ref, wso_ref, wco_ref, wo_ref) = refs[:17]
        dproj_ref, rhs_ref, sums_ref, dw3_ref, dw31_ref = refs[17 + na:22 + na]
        cv_ext, u0_ext, d3_ext, du1_ext, tmp_ref = refs[22 + 2 * na:27 + 2 * na]
        scatter_start, scatter_finish = _scatter_plan(refs[17:17 + na], refs[22 + na:22 + 2 * na], *refs[27 + 2 * na:])
        i = pl.program_id(0)
        first_tile = i == nt - 1

        @pl.when(i == 0)
        def _():
            scatter_start()
            sums_ref[...] = jnp.zeros((8, d), F32)
            dw3_ref[...] = jnp.zeros(dw3_ref.shape, F32)
            dw31_ref[...] = jnp.zeros(dw31_ref.shape, F32)
            d3_ext[tm:tm + hs, :] = jnp.zeros((hs, d), F32)
            du1_ext[tm:tm + hc, :] = jnp.zeros((hc, d), F32)

        def col(g):
            return proj_ref[:, g * d:(g + 1) * d].astype(F32)

        def hcol(g, rows):
            v = halo_ref[HALO_CONF - rows:HALO_CONF, g * d:(g + 1) * d].astype(F32)
            return jnp.where(first_tile, 0.0, v)

        dx1v = dx1_ref[...]
        mixv = mix_ref[...].astype(F32)
        dmix = (mod_ref[M_G1:M_G1 + 1, :] * dx1v).astype(BF16)
        rhs_ref[2] = dmix
        sums_ref[0:1, :] = sums_ref[0:1, :] + _colsum(dx1v * mixv)
        dmerged = _dot_nt(dmix, wo_ref[...])
        ga = ga_ref[...].astype(F32)
        gb = gb_ref[...].astype(F32)
        yav = ya_ref[...].astype(F32)
        ybv = yb_ref[...].astype(F32)
        dya_f = dmerged * ga
        dyb_f = dmerged * gb
        dya = dya_f.astype(BF16)
        dyb = dyb_f.astype(BF16)
        rhs_ref[0] = dya
        rhs_ref[1] = dyb
        dproj_ref[:, 5 * d:6 * d] = (dya_f * yav * (1.0 - ga)).astype(BF16)
        dproj_ref[:, 6 * d:7 * d] = (dyb_f * ybv * (1.0 - gb)).astype(BF16)

        dya_pre = _dot_nt(dya, wso_ref[...])
        c_s, v_s, b_s = col(1), col(2), col(0)
        cv_ext[0:hs, :] = hcol(1, hs) * hcol(2, hs)
        cv_ext[hs:hs + tm, :] = c_s * v_s
        dproj_ref[:, 0:d] = (dya_pre * c3_ref[...].astype(F32)).astype(BF16)
        d3_ext[0:tm, :] = dya_pre * b_s
        _tap_wgrad(d3_ext, cv_ext, _causal_taps(SHORT_K, hs), tm, hs, dw3_ref)
        _tap_conv(d3_ext, taps_ref, [(r3 + wi, off) for wi, off in _anticausal_taps(SHORT_K)], tm, hs, tmp_ref)
        dcv = tmp_ref[...]
        dproj_ref[:, d:2 * d] = (dcv * v_s).astype(BF16)
        dproj_ref[:, 2 * d:3 * d] = (dcv * c_s).astype(BF16)
        d3_ext[tm:tm + hs, :] = d3_ext[0:hs, :]

        du3 = _dot_nt(dyb, wco_ref[...])
        u1v = u1_ref[...].astype(F32)
        mu = _rowmean(u1v)
        uc = u1v - mu
        rstd = lax.rsqrt(_rowmean(uc * uc) + LN_EPS)
        uhat = uc * rstd
        lng = prm_ref[P_LNG:P_LNG + 1, :]
        u2 = uhat * lng + prm_ref[P_LNB:P_LNB + 1, :]
        s2 = _sigmoid(u2)
        du2 = du3 * (s2 * (1.0 + u2 * (1.0 - s2)))
        sums_ref[1:2, :] = sums_ref[1:2, :] + _colsum(du2 * uhat)
        sums_ref[2:3, :] = sums_ref[2:3, :] + _colsum(du2)
        duhat = du2 * lng
        du1 = rstd * (duhat - _rowmean(duhat) - uhat * _rowmean(duhat * uhat))
        sums_ref[3:4, :] = sums_ref[3:4, :] + _colsum(du1)
        du1_ext[0:tm, :] = du1
        v_c = col(3)
        sg = sg_ref[...].astype(F32)
        u0_ext[0:hc, :] = hcol(3, hc) * _sigmoid(hcol(4, hc))
        u0_ext[hc:hc + tm, :] = v_c * sg
        _tap_wgrad(du1_ext, u0_ext, _causal_taps(CONF_K, hc), tm, hc, dw31_ref)
        _tap_conv(du1_ext, taps_ref, [(r31 + wi, off) for wi, off in _anticausal_taps(CONF_K)], tm, hc, tmp_ref)
        dv_c = tmp_ref[...] * sg
        dproj_ref[:, 3 * d:4 * d] = dv_c.astype(BF16)
        dproj_ref[:, 4 * d:5 * d] = (dv_c * v_c * (1.0 - sg)).astype(BF16)
        du1_ext[tm:tm + hc, :] = du1_ext[0:hc, :]

        @pl.when(i == nt - 1)
        def _():
            scatter_finish()

    rev = lambda i: (nt - 1 - i, 0)
    tile = pl.BlockSpec((tm, d), rev)
    whole = lambda shape: pl.BlockSpec(shape, lambda i: (0,) * len(shape))
    hblocks = tm // HALO_CONF
    halo = pl.BlockSpec((HALO_CONF, 7 * d), lambda i: (jnp.maximum((nt - 1 - i) * hblocks - 1, 0), 0))
    res = pl.pallas_call(
        body, name="backward_mix", grid=(nt,),
        out_shape=(jax.ShapeDtypeStruct((t, 7 * d), BF16), jax.ShapeDtypeStruct((3, t, d), BF16),
                   jax.ShapeDtypeStruct((8, d), F32),
                   jax.ShapeDtypeStruct((SUBLANES * SHORT_K, d), F32),
                   jax.ShapeDtypeStruct((SUBLANES * CONF_K, d), F32))
        + tuple(jax.ShapeDtypeStruct(p.shape, p.dtype) for p in partials),
        in_specs=[tile, pl.BlockSpec((tm, 4 * d), rev), halo, tile, tile, tile, tile, tile, tile, tile, tile,
                  whole((8, d)), whole((8, d)), whole(taps.shape), whole((d, d)), whole((d, d)), whole((d, d))]
        + [ANY_SPEC] * na,
        out_specs=[pl.BlockSpec((tm, 7 * d), rev), pl.BlockSpec((3, tm, d), lambda i: (0, nt - 1 - i, 0)),
                   whole((8, d)), whole((SUBLANES * SHORT_K, d)), whole((SUBLANES * CONF_K, d))] + [ANY_SPEC] * na,
        scratch_shapes=[pltpu.VMEM((hs + tm, d), F32), pltpu.VMEM((hc + tm, d), F32),
                        pltpu.VMEM((tm + hs, d), F32), pltpu.VMEM((tm + hc, d), F32),
                        pltpu.VMEM((tm, d), F32)] + _comm_scratch(na),
        compiler_params=_params(("arbitrary",)),
    )(dx1, proj, proj, y_a, y_b, mix, u1, conv3, sg, ga, gb, mod, prm, taps, w_so, w_co, w_o, *partials)
    return res[:5], res[5:]


def _backward_in(dproj, x, dx1, mod, prm, w_in_g, tm, partials, partial_view, exchanged_shapes):
    t, d = x.shape
    n_all = w_in_g.shape[1]
    na = len(partials)
    nt = t // tm

    def body(*refs):
        dproj_ref, x_ref, dx1_ref, mod_ref, prm_ref, w_hbm = refs[:6]
        gx_ref, sums_ref = refs[6 + na:8 + na]
        w_ref = refs[8 + 2 * na]
        scatter_start, scatter_finish = _scatter_plan(refs[6:6 + na], refs[8 + na:8 + 2 * na], *refs[9 + 2 * na:],
                                                      view=partial_view)

        @pl.when(pl.program_id(0) == 0)
        def _():
            scatter_start()
            pltpu.sync_copy(w_hbm, w_ref)
            sums_ref[...] = jnp.zeros((8, d), F32)

        dh = _dot_nt(dproj_ref[...], w_ref[...])
        xv = x_ref[...]
        r1 = lax.rsqrt(_rowmean(xv * xv) + EPS)
        xn = xv * r1
        gmix = prm_ref[P_GMIX:P_GMIX + 1, :]
        scale = 1.0 + mod_ref[M_SC1:M_SC1 + 1, :]
        dxn = dh * gmix * scale
        gx_ref[...] = dx1_ref[...] + r1 * (dxn - xn * _rowmean(dxn * xn))
        hx = dh * xn
        sums_ref[0:1, :] = sums_ref[0:1, :] + _colsum(dh)
        sums_ref[1:2, :] = sums_ref[1:2, :] + _colsum(hx) * gmix
        sums_ref[2:3, :] = sums_ref[2:3, :] + _colsum(hx) * scale

        @pl.when(pl.program_id(0) == nt - 1)
        def _():
            scatter_finish()

    tile = pl.BlockSpec((tm, d), lambda i: (i, 0))
    whole = pl.BlockSpec((8, d), lambda i: (0, 0))
    res = pl.pallas_call(
        body, name="backward_in", grid=(nt,),
        out_shape=(jax.ShapeDtypeStruct((t, d), F32), jax.ShapeDtypeStruct((8, d), F32))
        + tuple(jax.ShapeDtypeStruct(s, p.dtype) for p, s in zip(partials, exchanged_shapes)),
        in_specs=[pl.BlockSpec((tm, n_all), lambda i: (i, 0)), tile, tile, whole, whole, ANY_SPEC] + [ANY_SPEC] * na,
        out_specs=[tile, whole] + [ANY_SPEC] * na,
        scratch_shapes=[pltpu.VMEM(w_in_g.shape, BF16)] + _comm_scratch(na),
        compiler_params=_params(("arbitrary",)),
    )(dproj, x, dx1, mod, prm, w_in_g, *partials)
    return res[0], res[1], res[2:]


def _weight_grad(a, b, a_spec, b_spec, ns, m, n, nk, name, partials=(), by_columns=False,
                 partial_view=_leading_block, exchanged_shapes=None):
    na = len(partials)
    exchanged_shapes = exchanged_shapes or [p.shape for p in partials]

    def body(*refs):
        a_ref, b_ref = refs[:2]
        o_ref = refs[2 + na]
        acc_ref = refs[3 + 2 * na]
        s, k = pl.program_id(0), pl.program_id(1)
        if na:
            scatter_start, scatter_finish = _scatter_plan(refs[2:2 + na], refs[3 + na:3 + 2 * na], *refs[4 + 2 * na:],
                                                          view=partial_view)

            @pl.when((s == 0) & (k == 0))
            def _():
                scatter_start()

        av = a_ref[0] if len(a_ref.shape) == 3 else a_ref[...]
        bv = b_ref[0] if len(b_ref.shape) == 3 else b_ref[...]
        part = _dot_tn(av, bv)

        @pl.when(k == 0)
        def _():
            acc_ref[...] = part

        @pl.when(k > 0)
        def _():
            acc_ref[...] = acc_ref[...] + part

        @pl.when(k == nk - 1)
        def _():
            if by_columns:
                o_ref[...] = acc_ref[...].astype(BF16)
            else:
                o_ref[0] = acc_ref[...].astype(BF16)

        if na:
            @pl.when((s == ns - 1) & (k == nk - 1))
            def _():
                scatter_finish()

    if by_columns:
        out_shape, out_spec = (m, ns * n), pl.BlockSpec((m, n), lambda s, k: (0, s))
    else:
        out_shape, out_spec = (ns, m, n), pl.BlockSpec((1, m, n), lambda s, k: (s, 0, 0))
    res = pl.pallas_call(
        body, name=name, grid=(ns, nk),
        out_shape=(jax.ShapeDtypeStruct(out_shape, BF16),)
        + tuple(jax.ShapeDtypeStruct(sh, p.dtype) for p, sh in zip(partials, exchanged_shapes)),
        in_specs=[a_spec, b_spec] + [ANY_SPEC] * na,
        out_specs=[out_spec] + [ANY_SPEC] * na,
        scratch_shapes=[pltpu.VMEM((m, n), F32)] + (_comm_scratch(na) if na else []),
        compiler_params=_params(("arbitrary", "arbitrary")),
    )(a, b, *partials)
    return (res[0], res[1:]) if na else res[0]


def _adamw(w, g, m, v):
    m = ADAM_B1 * m + (1.0 - ADAM_B1) * g
    v = ADAM_B2 * v + (1.0 - ADAM_B2) * (g * g)
    m_hat = m / (1.0 - ADAM_B1 ** ADAM_STEP)
    v_hat = v / (1.0 - ADAM_B2 ** ADAM_STEP)
    delta = -ADAM_LR * (m_hat / (jnp.sqrt(v_hat) + ADAM_EPS) + ADAM_WD * w)
    return delta, m, v


def _adamw_shard(parts, w, m, v, tr, name, member=None):
    r, c = w.shape

    def body(p_ref, w_ref, m_ref, v_ref, g_ref, d_ref, nm_ref, nv_ref):
        part = (lambda s: p_ref[s]) if member is None else (lambda s: p_ref[s, 0])
        g = part(0).astype(F32)
        for s in range(1, N_DEV):
            g = g + part(s).astype(F32)
        delta, nm, nv = _adamw(w_ref[...], g, m_ref[...], v_ref[...])
        g_ref[...] = g
        d_ref[...] = delta
        nm_ref[...] = nm
        nv_ref[...] = nv

    tile = pl.BlockSpec((tr, c), lambda i: (i, 0))
    if member is None:
        parts_spec = pl.BlockSpec((N_DEV, tr, c), lambda i: (0, i, 0))
    else:
        parts_spec = pl.BlockSpec((N_DEV, 1, tr, c), lambda i: (0, member, i, 0))
    return pl.pallas_call(
        body, name=name, grid=(r // tr,),
        out_shape=(jax.ShapeDtypeStruct((r, c), F32),) * 4,
        in_specs=[parts_spec, tile, tile, tile],
        out_specs=[tile] * 4,
        compiler_params=_params(("arbitrary",)),
    )(parts, w, m, v)


def _ada_update(sc_all, dmod_cols, w, m, v, tr):
    d, n = w.shape

    def body(sc_ref, dm_ref, w_ref, m_ref, v_ref, g_ref, d_ref, nm_ref, nv_ref):
        g = lax.dot_general(sc_ref[...], dm_ref[...], (((0,), (0,)), ((), ())),
                            preferred_element_type=F32, precision=lax.Precision.HIGHEST)
        delta, nm, nv = _adamw(w_ref[...], g, m_ref[...], v_ref[...])
        g_ref[...] = g
        d_ref[...] = delta
        nm_ref[...] = nm
        nv_ref[...] = nv

    tile = pl.BlockSpec((tr, n), lambda i: (i, 0))
    return pl.pallas_call(
        body, name="ada_update", grid=(d // tr,),
        out_shape=(jax.ShapeDtypeStruct((d, n), F32),) * 4,
        in_specs=[pl.BlockSpec((N_DEV, tr), lambda i: (0, i)), pl.BlockSpec((N_DEV, n), lambda i: (0, 0)),
                  tile, tile, tile],
        out_specs=[tile] * 4,
        compiler_params=_params(("arbitrary",)),
    )(sc_all, dmod_cols, w, m, v)


def _small_exchange(vec, cg):
    l = vec.shape[2]
    rows = cg.shape[1]

    def body(vec_ref, cg_ref, vall_ref, cgr_ref, send_sems, recv_sems):
        x, y, c = _my_coords()
        me = _slot(x, y, c)
        vall_ref[me] = vec_ref[0]
        cgr_ref[me] = cg_ref[me]
        copies = []
        for k in range(1, N_DEV):
            peer = (_flip(x, k & 4), _flip(y, k & 2), _flip(c, k & 1))
            copies.append(pltpu.make_async_remote_copy(
                src_ref=vall_ref.at[me], dst_ref=vall_ref.at[me], send_sem=send_sems.at[k - 1],
                recv_sem=recv_sems.at[k - 1], device_id=peer, device_id_type=MESH))
            copies.append(pltpu.make_async_remote_copy(
                src_ref=cg_ref.at[_slot(*peer)], dst_ref=cgr_ref.at[me], send_sem=send_sems.at[7 + k - 1],
                recv_sem=recv_sems.at[7 + k - 1], device_id=peer, device_id_type=MESH))
        for cp in copies:
            cp.start()
        for cp in copies:
            cp.wait_recv()
        for cp in copies:
            cp.wait_send()

    vm = pl.BlockSpec(memory_space=pltpu.VMEM)
    return pl.pallas_call(
        body, name="small_exchange",
        out_shape=(jax.ShapeDtypeStruct((N_DEV, 1, l), F32), jax.ShapeDtypeStruct((N_DEV, rows, LANES), F32)),
        in_specs=[vm, vm], out_specs=[vm, vm],
        scratch_shapes=[pltpu.SemaphoreType.DMA((14,)), pltpu.SemaphoreType.DMA((14,))],
        compiler_params=_params(),
    )(vec, cg)


def _small_update(vall, cgr, smalls):
    ns = len(smalls)

    def body(*refs):
        vall_ref, cgr_ref = refs[0], refs[1]
        wmv = refs[2:2 + 3 * ns]
        outs = refs[2 + 3 * ns:]
        for p, (_, _, _, lo, hi, kind) in enumerate(smalls):
            w_ref, m_ref, v_ref = wmv[3 * p:3 * p + 3]
            part = (lambda s: vall_ref[s, :, lo:hi]) if kind == "vec" else (lambda s: cgr_ref[s, lo:hi, :])
            g = part(0)
            for s in range(1, N_DEV):
                g = g + part(s)
            delta, nm, nv = _adamw(w_ref[...], g, m_ref[...], v_ref[...])
            for o_ref, val in zip(outs[4 * p:4 * p + 4], (g, delta, nm, nv)):
                o_ref[...] = val

    vm = pl.BlockSpec(memory_space=pltpu.VMEM)
    args = [vall, cgr]
    out_shape = []
    for w, m, v, _, _, _ in smalls:
        args += [w, m, v]
        out_shape += [jax.ShapeDtypeStruct(w.shape, F32)] * 4
    res = pl.pallas_call(
        body, name="small_update",
        out_shape=tuple(out_shape),
        in_specs=[vm] * len(args), out_specs=[vm] * len(out_shape),
        compiler_params=_params(),
    )(*args)
    return [res[4 * p:4 + 4 * p] for p in range(ns)]


def _pick(t, want):
    return want if t % want == 0 else t


def kernel(x, c, w_ada, b_ada, norm_mix_g, w_in, conv_short_w, w_short_out, conv_conf_w, conv_conf_b, conf_ln_g, conf_ln_b, w_conf_out, w_o, norm_ffn_g, w_ffn_in, w_ffn_out, final_norm_g, loss_target, m_w_ada, m_b_ada, m_norm_mix_g, m_w_in, m_conv_short_w, m_w_short_out, m_conv_conf_w, m_conv_conf_b, m_conf_ln_g, m_conf_ln_b, m_w_conf_out, m_w_o, m_norm_ffn_g, m_w_ffn_in, m_w_ffn_out, m_final_norm_g, v_w_ada, v_b_ada, v_norm_mix_g, v_w_in, v_conv_short_w, v_w_short_out, v_conv_conf_w, v_conv_conf_b, v_conf_ln_g, v_conf_ln_b, v_w_conf_out, v_w_o, v_norm_ffn_g, v_w_ffn_in, v_w_ffn_out, v_final_norm_g):
    t, d = x.shape[1], x.shape[2]
    x2 = x.reshape(t, d)
    tgt = loss_target.reshape(t, d)
    me = _slot(*_my_coords())
    tm = _pick(t, 256)
    tm_fwd_in = _pick(t, 1024)
    tk = _pick(t, 2048)

    taps_loc = jnp.zeros((40, LANES), F32)
    taps_loc = taps_loc.at[0:SHORT_K].set(conv_short_w[0]).at[8:8 + CONF_K].set(conv_conf_w[0])
    prm = jnp.concatenate([norm_mix_g, norm_ffn_g, final_norm_g.reshape(1, d), conv_conf_b, conf_ln_g, conf_ln_b,
                           jnp.zeros((2, d), F32)], axis=0)

    n_in = w_in.shape[2]
    nk = t // tk
    tok = pl.BlockSpec((tk, d), lambda s, k: (k, 0))
    rows = d // N_DEV
    frows = w_ffn_out.shape[1]

    proj, h, w_in_g, mod, sc_all3, taps, (w_so_g, w_co_g, w_o_g) = _forward_in(
        jnp.reshape(me, (1,)).astype(jnp.int32), x2, prm, w_in[0].astype(BF16),
        (c, w_ada[0], b_ada, taps_loc), tm_fwd_in,
        [w_short_out[0].astype(BF16), w_conf_out[0].astype(BF16), w_o[0].astype(BF16)])
    w_so = w_so_g.reshape(d, d)
    w_co = w_co_g.reshape(d, d)
    w_oo = w_o_g.reshape(d, d)
    w_fi_t, m_fi_t, v_fi_t = w_ffn_in[0].T, m_w_ffn_in[0].T, v_w_ffn_in[0].T
    (x1, y_a, y_b, mix, u1, conv3, sg, ga, gb, mixer_lhs), (w_fi_g, w_fo_g) = _forward_mix(
        x2, proj, mod, prm, taps, w_so, w_co, w_oo, tm, [w_fi_t.astype(BF16), w_ffn_out[0].astype(BF16)])
    fb = w_fi_g.shape[1]
    ff = N_DEV * frows
    w_fi_all = w_fi_g.reshape(N_DEV * fb, d)
    w_fo_all = w_fo_g.reshape(ff, d)
    dx2, ab, h2, act, sums_f = _forward_ffn(x1, tgt, mod, prm, w_fi_all, w_fo_all, tm)

    dx1, df, dab, sums_b = _backward_ffn(dx2, x1, ab, mod, prm, w_fi_all, w_fo_all, tm)
    fb2 = 2 * fb
    pair_tok = pl.BlockSpec((tk, fb2), lambda s, k: (k, s))
    g_fi = _weight_grad(dab, h2, pair_tok, tok, N_DEV // 2, fb2, d, nk, "grad_w_ffn_in")
    g_fo = _weight_grad(act, df, pair_tok, tok, N_DEV // 4, fb2, d, nk, "grad_w_ffn_out")
    (dproj, mixer_rhs, sums_m, dw3p, dw31p), (p_fi, p_fo) = _backward_mix(
        dx1, proj, y_a, y_b, mix, u1, conv3, sg, ga, gb, mod, prm, taps, w_so, w_co, w_oo, tm,
        [g_fi.reshape(N_DEV, fb, d), g_fo.reshape(N_DEV, frows, d)])
    tk_in = _pick(t, 4096)
    nk_in = t // tk_in
    tok_in = pl.BlockSpec((tk_in, d), lambda s, k: (k, 0))
    member_tok = pl.BlockSpec((1, tk, d), lambda s, k: (s, k, 0))
    g_mixers = _weight_grad(mixer_lhs, mixer_rhs, member_tok, member_tok, 3, d, d, nk, "grad_w_mixers")
    n_blk = _pick(N_DEV * n_in, d)
    g_in, (p_mixers,) = _weight_grad(
        h, dproj, tok_in, pl.BlockSpec((tk_in, n_blk), lambda s, k: (k, s)),
        N_DEV * n_in // n_blk, d, n_blk, nk_in, "grad_w_in",
        [g_mixers.reshape(3, N_DEV, rows, d)], by_columns=True,
        partial_view=lambda ref, dev: ref.at[:, dev], exchanged_shapes=[(N_DEV, 3, rows, d)])
    grad_x, sums_i, (p_in,) = _backward_in(dproj, x2, dx1, mod, prm, w_in_g, tm, [g_in], _column_block(n_in),
                                           [(N_DEV, d, n_in)])

    up_in = _adamw_shard(p_in, w_in[0], m_w_in[0], v_w_in[0], _pick(d, 256), "adamw_w_in")
    up_so = _adamw_shard(p_mixers, w_short_out[0], m_w_short_out[0], v_w_short_out[0], rows, "adamw_w_short_out", 0)
    up_co = _adamw_shard(p_mixers, w_conf_out[0], m_w_conf_out[0], v_w_conf_out[0], rows, "adamw_w_conf_out", 1)
    up_oo = _adamw_shard(p_mixers, w_o[0], m_w_o[0], v_w_o[0], rows, "adamw_w_o", 2)
    up_fi = tuple(a.T for a in _adamw_shard(p_fi, w_fi_t, m_fi_t, v_fi_t, fb // 2, "adamw_w_ffn_in"))
    up_fo = _adamw_shard(p_fo, w_ffn_out[0], m_w_ffn_out[0], v_w_ffn_out[0], frows, "adamw_w_ffn_out")

    vec = jnp.concatenate([sums_i[0:2], sums_m[0:1], sums_b[0:2], sums_f[1:2],
                           sums_i[2:3], sums_m[3:4], sums_m[1:3], sums_b[2:3], sums_f[0:1],
                           sums_f[2:3]], axis=0)
    vec = vec.reshape(1, 1, 13 * d)
    dw3 = dw3p.reshape(SHORT_K, SUBLANES, d).sum(axis=1)
    dw31 = dw31p.reshape(CONF_K, SUBLANES, d).sum(axis=1)
    cg = jnp.zeros((40, d), F32).at[0:SHORT_K].set(dw3).at[8:8 + CONF_K].set(dw31)
    cg = cg.reshape(40, N_DEV, LANES).transpose(1, 0, 2)
    fin = lambda a: a.reshape(1, d)
    tap = lambda a: a.reshape(a.shape[1:])
    smalls = [
        (b_ada, m_b_ada, v_b_ada, 0, 6 * d, "vec"),
        (norm_mix_g, m_norm_mix_g, v_norm_mix_g, 6 * d, 7 * d, "vec"),
        (tap(conv_short_w), tap(m_conv_short_w), tap(v_conv_short_w), 0, SHORT_K, "cg"),
        (tap(conv_conf_w), tap(m_conv_conf_w), tap(v_conv_conf_w), 8, 8 + CONF_K, "cg"),
        (conv_conf_b, m_conv_conf_b, v_conv_conf_b, 7 * d, 8 * d, "vec"),
        (conf_ln_g, m_conf_ln_g, v_conf_ln_g, 8 * d, 9 * d, "vec"),
        (conf_ln_b, m_conf_ln_b, v_conf_ln_b, 9 * d, 10 * d, "vec"),
        (norm_ffn_g, m_norm_ffn_g, v_norm_ffn_g, 10 * d, 11 * d, "vec"),
        (fin(final_norm_g), fin(m_final_norm_g), fin(v_final_norm_g), 11 * d, 12 * d, "vec"),
    ]
    vall, cgr = _small_exchange(vec, cg)
    up_small = _small_update(vall, cgr, smalls)
    n_ada = w_ada.shape[2]
    dmod_all = vall.reshape(N_DEV, 13 * d)[:, 0:6 * d]
    dmod_cols = lax.dynamic_slice(dmod_all, (0, me * n_ada), (N_DEV, n_ada))
    up_ada = _ada_update(sc_all3.reshape(N_DEV, d), dmod_cols, w_ada[0], m_w_ada[0], v_w_ada[0], _pick(d, 256))

    loss = jnp.sum(vall.reshape(N_DEV, 13 * d)[:, 12 * d:])

    lead = lambda a: a.reshape((1,) + a.shape)
    ups = [tuple(lead(a) for a in up_ada), up_small[0], up_small[1], tuple(lead(a) for a in up_in),
           tuple(lead(a) for a in up_small[2]), tuple(lead(a) for a in up_so), tuple(lead(a) for a in up_small[3]),
           up_small[4], up_small[5], up_small[6],
           tuple(lead(a) for a in up_co), tuple(lead(a) for a in up_oo), up_small[7],
           tuple(lead(a) for a in up_fi), tuple(lead(a) for a in up_fo),
           tuple(a.reshape(d) for a in up_small[8])]
    grads = [u[0] for u in ups]
    deltas = [u[1] for u in ups]
    new_m = [u[2] for u in ups]
    new_v = [u[3] for u in ups]
    return (loss, grad_x.reshape(1, t, d), *grads, *deltas, *new_m, *new_v)
```

```python
import functools

import jax
import jax.numpy as jnp
from jax import lax
from jax.experimental import pallas as pl
from jax.experimental.pallas import tpu as pltpu

F32 = jnp.float32
BF16 = jnp.bfloat16
MESH = pl.DeviceIdType.MESH

N_DEV = 8
EPS = 1e-6
LN_EPS = 1e-5
SHORT_K = 3
CONF_K = 31
ADAM_LR = 0.001
ADAM_B1 = 0.9
ADAM_B2 = 0.999
ADAM_EPS = 1e-08
ADAM_WD = 0.01
ADAM_STEP = 10

LANES = 128
SUBLANES = 8
CONV_ROWS = 64
HALO_SHORT = 8
HALO_CONF = 32
VMEM_LIMIT = 56 * 1024 * 1024

M_SH1, M_SC1, M_G1, M_SH2, M_SC2, M_G2 = range(6)
P_GMIX, P_GFFN, P_GFIN, P_CBIAS, P_LNG, P_LNB = range(6)


def _params(sem=None, **kw):
    return pltpu.CompilerParams(dimension_semantics=sem, vmem_limit_bytes=VMEM_LIMIT, **kw)


def _sigmoid(v):
    return jax.nn.sigmoid(v)


def _dot(a, b):
    return jnp.dot(a, b, preferred_element_type=F32)


def _dot_nt(a, b):
    return lax.dot_general(a, b, (((1,), (1,)), ((), ())), preferred_element_type=F32)


def _dot_tn(a, b):
    return lax.dot_general(a, b, (((0,), (0,)), ((), ())), preferred_element_type=F32)


def _colsum(v):
    return jnp.sum(v, axis=0, keepdims=True)


def _rowmean(v):
    return jnp.mean(v, axis=-1, keepdims=True)


def _my_coords():
    return lax.axis_index("x"), lax.axis_index("y"), lax.axis_index("c")


def _slot(px, py, pc):
    return 4 * px + 2 * py + pc


def _flip(v, bit):
    return 1 - v if bit else v


def _taps_by_residue(taps):
    by_res = {}
    for wi, off in taps:
        by_res.setdefault(off % SUBLANES, []).append((wi, off // SUBLANES))
    return sorted(by_res.items())


def _tap_conv(ext_ref, w_ref, taps, tm, extra, out_ref):
    d = out_ref.shape[1]
    rb = min(CONV_ROWS, tm)
    wrows = rb + extra
    groups = _taps_by_residue(taps)

    def block(i, carry):
        base = pl.multiple_of(i * rb, SUBLANES)
        for lc in range(d // LANES):
            ls = pl.ds(lc * LANES, LANES)
            win = ext_ref[pl.ds(base, wrows), ls]
            acc = None
            for r, lst in groups:
                sh = win if r == 0 else pltpu.roll(win, wrows - r, 0)
                for wi, q in lst:
                    term = w_ref[wi:wi + 1, ls] * sh[SUBLANES * q:SUBLANES * q + rb, :]
                    acc = term if acc is None else acc + term
            out_ref[pl.ds(base, rb), ls] = acc
        return carry

    lax.fori_loop(0, tm // rb, block, 0)


def _tap_wgrad(a_ref, ext_ref, taps, tm, extra, acc_ref):
    d = a_ref.shape[1]
    rb = min(CONV_ROWS, tm)
    wrows = rb + extra
    groups = _taps_by_residue(taps)

    def block(i, carry):
        base = pl.multiple_of(i * rb, SUBLANES)
        for lc in range(d // LANES):
            ls = pl.ds(lc * LANES, LANES)
            a_blk = a_ref[pl.ds(base, rb), ls]
            win = ext_ref[pl.ds(base, wrows), ls]
            for r, lst in groups:
                sh = win if r == 0 else pltpu.roll(win, wrows - r, 0)
                for wi, q in lst:
                    prod = a_blk * sh[SUBLANES * q:SUBLANES * q + rb, :]
                    part = prod[0:SUBLANES, :]
                    for s in range(1, rb // SUBLANES):
                        part = part + prod[SUBLANES * s:SUBLANES * (s + 1), :]
                    rows = pl.ds(SUBLANES * wi, SUBLANES)
                    acc_ref[rows, ls] = acc_ref[rows, ls] + part
        return carry

    lax.fori_loop(0, tm // rb, block, 0)


def _causal_taps(k, halo):
    return [(i, halo - (k - 1) + i) for i in range(k)]


def _anticausal_taps(k):
    return [(i, (k - 1) - i) for i in range(k)]


def _ada_scratch(d, nloc, trows):
    return [pltpu.VMEM((N_DEV, 1, d), F32), pltpu.VMEM((N_DEV, 1, nloc), F32), pltpu.VMEM((N_DEV, 1, nloc), F32),
            pltpu.VMEM((N_DEV, trows, LANES), F32), pltpu.SemaphoreType.DMA((21,)), pltpu.SemaphoreType.DMA((21,))]


def _ada_exchange(c_ref, w_ref, b_ref, t_ref, mod_ref, sc_ref, taps_ref,
                  scall_ref, part_ref, modrecv_ref, tapsall_ref, send_sems, recv_sems):
    x, y, cc = _my_coords()
    me = _slot(x, y, cc)
    d = c_ref.shape[1]
    cv = c_ref[...]
    scall_ref[me] = cv * _sigmoid(cv)
    tapsall_ref[me] = t_ref[...]

    def peer_of(k):
        return (_flip(x, k & 4), _flip(y, k & 2), _flip(cc, k & 1))

    def gather_copy(ref, base, k):
        return pltpu.make_async_remote_copy(
            src_ref=ref.at[me], dst_ref=ref.at[me], send_sem=send_sems.at[base + k - 1],
            recv_sem=recv_sems.at[base + k - 1], device_id=peer_of(k), device_id_type=MESH)

    first = [gather_copy(scall_ref, 0, k) for k in range(1, N_DEV)]
    first += [gather_copy(tapsall_ref, 7, k) for k in range(1, N_DEV)]
    for cp in first:
        cp.start()
    for cp in first[:7]:
        cp.wait_recv()
    sc_all = jnp.concatenate([scall_ref[s] for s in range(N_DEV)], axis=0)
    for s in range(N_DEV):
        sc_ref[s] = scall_ref[s]
    part = jnp.dot(sc_all, w_ref[...], preferred_element_type=F32,
                   precision=lax.Precision.HIGHEST)
    for b in range(N_DEV):
        part_ref[b] = part[b:b + 1, :]
    modrecv_ref[me] = part_ref[me]
    second = []
    for k in range(1, N_DEV):
        px, py, pc = peer_of(k)
        second.append(pltpu.make_async_remote_copy(
            src_ref=part_ref.at[_slot(px, py, pc)], dst_ref=modrecv_ref.at[me],
            send_sem=send_sems.at[14 + k - 1], recv_sem=recv_sems.at[14 + k - 1],
            device_id=(px, py, pc), device_id_type=MESH))
    for cp in second:
        cp.start()
    for cp in second:
        cp.wait_recv()
    mod = jnp.concatenate([modrecv_ref[s] for s in range(N_DEV)], axis=1) + b_ref[...]
    for r in range(6):
        mod_ref[r:r + 1, :] = mod[:, r * d:(r + 1) * d]
    mod_ref[6:8, :] = jnp.zeros((2, d), F32)
    for cp in first[7:]:
        cp.wait_recv()
    taps_ref[...] = jnp.concatenate([tapsall_ref[s] for s in range(N_DEV)], axis=1)
    for cp in first + second:
        cp.wait_send()


ANY_SPEC = pl.BlockSpec(memory_space=pl.ANY)


def _comm_scratch(na):
    return [pltpu.SemaphoreType.DMA((7 * na,)), pltpu.SemaphoreType.DMA((7 * na,)), pltpu.SemaphoreType.DMA((na,))]


def _leading_block(ref, slot):
    return ref.at[slot]


def _column_block(width):
    def view(ref, slot):
        return ref.at[:, pl.ds(pl.multiple_of(slot * width, LANES), width)]
    return view


class _Gather:
    def __init__(self, ins, outs, send_sems, recv_sems, local_sems, view=_leading_block):
        self.na = len(ins)
        x, y, c = _my_coords()
        self.c = c
        self.view = view
        self.me, self.sibling = (x, y, c), (x, y, 1 - c)
        self.chips = [(1 - x, y), (x, 1 - y), (1 - x, 1 - y)]
        self.outs, self.send_sems, self.recv_sems = outs, send_sems, recv_sems
        self.mine = [pltpu.make_async_copy(ins[a], view(outs[a], _slot(*self.me)), local_sems.at[a])
                     for a in range(self.na)]
        self.first = []
        for a in range(self.na):
            self.first.append(self._copy(a, 0, self.me, self.sibling, src=ins[a]))
            self.first += [self._copy(a, 1 + j, self.me, (*chip, c), src=ins[a]) for j, chip in enumerate(self.chips)]
        self.passed = [self._copy(a, 4 + j, (*chip, c), self.sibling)
                       for a in range(self.na) for j, chip in enumerate(self.chips)]

    def _copy(self, a, k, block, to, src=None):
        dst = self.view(self.outs[a], _slot(*block))
        return pltpu.make_async_remote_copy(
            src_ref=dst if src is None else src, dst_ref=dst,
            send_sem=self.send_sems.at[7 * a + k], recv_sem=self.recv_sems.at[7 * a + k],
            device_id=to, device_id_type=MESH)

    def start(self):
        self.start_near()
        self.start_far()

    def start_near(self):
        self.start_local()
        self.start_one_hop()

    def start_local(self):
        for cp in self.mine + self.first[0::4]:
            cp.start()

    def start_one_hop(self):
        for cp in self.first[1::4] + self.first[2::4]:
            cp.start()

    def start_far(self):
        for cp in self.first[3::4]:
            cp.start()

    def forward_chip(self, j):
        for a in range(self.na):
            self._copy(a, 1 + j, (*self.chips[j], self.c), self.me).wait_recv()
            self.passed[3 * a + j].start()

    def wait_sibling_own(self):
        for a in range(self.na):
            self._copy(a, 0, self.sibling, self.me).wait_recv()

    def wait_sibling_passed(self, j):
        for a in range(self.na):
            self._copy(a, 4 + j, (*self.chips[j], 1 - self.c), self.me).wait_recv()

    def finish_sends(self):
        for cp in self.first + self.passed:
            cp.wait_send()
        for cp in self.mine:
            cp.wait()

    def forward(self):
        for j in range(3):
            self.forward_chip(j)

    def finish(self):
        self.wait_sibling_own()
        for j in range(3):
            self.wait_sibling_passed(j)
        self.finish_sends()


def _scatter_plan(ins, outs, send_sems, recv_sems, local_sems, view=_leading_block):
    na = len(ins)
    x, y, c = _my_coords()
    me = _slot(x, y, c)
    mine = [pltpu.make_async_copy(view(ins[a], me), outs[a].at[me], local_sems.at[a]) for a in range(na)]
    copies = []
    for k in range(1, N_DEV):
        peer = (_flip(x, k & 4), _flip(y, k & 2), _flip(c, k & 1))
        for a in range(na):
            copies.append(pltpu.make_async_remote_copy(
                src_ref=view(ins[a], _slot(*peer)), dst_ref=outs[a].at[me],
                send_sem=send_sems.at[7 * a + k - 1], recv_sem=recv_sems.at[7 * a + k - 1],
                device_id=peer, device_id_type=MESH))

    def start():
        for cp in mine + copies:
            cp.start()

    def finish():
        for cp in copies:
            cp.wait_recv()
        for cp in copies:
            cp.wait_send()
        for cp in mine:
            cp.wait()

    return start, finish


def _forward_in(me_arr, x, prm, w_in_loc, ada, tm, shards):
    t, d = x.shape
    n = w_in_loc.shape[1]
    ns = N_DEV // 2
    na = len(shards)
    nt = t // tm
    nloc = ada[1].shape[1]
    trows = ada[3].shape[0]
    chip_of_pass = {1: 1, 2: 0, 3: 2}

    def body(*refs):
        x_ref, prm_ref, wloc_ref = refs[1:4]
        ada_in = refs[4:8]
        shard_refs = refs[8:8 + na]
        proj_ref, h_ref, wg_ref, mod_ref, sc_ref, taps_ref = refs[8 + na:14 + na]
        gathered_refs = refs[14 + na:14 + 2 * na]
        hall_ref, wv_ref, wv_sem = refs[14 + 2 * na:17 + 2 * na]
        ada_scratch = refs[17 + 2 * na:23 + 2 * na]
        sems = refs[23 + 2 * na:]
        s, i = pl.program_id(0), pl.program_id(1)
        columns = _column_block(n)

        def gathers():
            return (_Gather([wloc_ref], [wg_ref], *sems[0:3], view=columns),
                    _Gather(shard_refs, gathered_refs, *sems[3:6]))

        def shard_copy(dev, p, other_core, src=None):
            half = pl.ds(pl.multiple_of((dev & 1) * n, LANES), n)
            return pltpu.make_async_copy(columns(wg_ref, dev) if src is None else src,
                                         wv_ref.at[p % 2, :, half], wv_sem.at[2 * (p % 2) + other_core])

        def pair_copies(p, me):
            return [shard_copy(me ^ (2 * p), p, 0), shard_copy(me ^ (2 * p) ^ 1, p, 1)]

        @pl.when(i == 0)
        def _():
            me = _slot(*_my_coords())

            @pl.when(s == 0)
            def _():
                g_in, _ = gathers()
                g_in.start_local()
                _ada_exchange(*ada_in, mod_ref, sc_ref, taps_ref, *ada_scratch)
                g_in.start_one_hop()
                own = shard_copy(me, 0, 0, src=wloc_ref)
                own.start()
                g_in.wait_sibling_own()
                sib = shard_copy(me ^ 1, 0, 1)
                sib.start()
                own.wait()
                sib.wait()

            for p in range(1, ns):
                @pl.when(s == p)
                def _(p=p):
                    for cp in pair_copies(p, me):
                        cp.wait()

        @pl.when(i == max(nt - 3, 0))
        def _():
            g_in, g_rest = gathers()
            for p in range(1, ns):
                @pl.when(s == p - 1)
                def _(p=p):
                    g_in.forward_chip(chip_of_pass[p])
                    if p == 1:
                        g_in.start_far()
                    if p == 2:
                        g_rest.start()
                    if p == 3:
                        g_rest.forward()

        @pl.when(i == nt - 1)
        def _():
            g_in, _ = gathers()
            me = _slot(*_my_coords())
            for p in range(1, ns):
                @pl.when(s == p - 1)
                def _(p=p):
                    g_in.wait_sibling_passed(chip_of_pass[p])
                    for cp in pair_copies(p, me):
                        cp.start()

        @pl.when(s == 0)
        def _():
            xv = x_ref[...]
            r = lax.rsqrt(_rowmean(xv * xv) + EPS)
            h = xv * r * prm_ref[P_GMIX:P_GMIX + 1, :] * (1.0 + mod_ref[M_SC1:M_SC1 + 1, :]) \
                + mod_ref[M_SH1:M_SH1 + 1, :]
            hb = h.astype(BF16)
            hall_ref[i] = hb
            h_ref[...] = hb

        proj_ref[...] = _dot(hall_ref[i], wv_ref[s % 2]).astype(BF16)

        @pl.when((s == ns - 1) & (i == nt - 1))
        def _():
            g_in, g_rest = gathers()
            g_in.finish_sends()
            g_rest.finish()

    x_tile = pl.BlockSpec((tm, d), lambda s, i, me: (jnp.where(s == 0, i, nt - 1), 0))
    whole = lambda a: pl.BlockSpec(a.shape, lambda s, i, me: (0,) * len(a.shape))
    small_out = (jax.ShapeDtypeStruct((8, d), F32), jax.ShapeDtypeStruct((N_DEV, 1, d), F32),
                 jax.ShapeDtypeStruct((trows, N_DEV * LANES), F32))
    res = pl.pallas_call(
        body, name="forward_in",
        grid_spec=pltpu.PrefetchScalarGridSpec(
            num_scalar_prefetch=1, grid=(ns, nt),
            in_specs=[x_tile, whole(prm), ANY_SPEC] + [whole(a) for a in ada] + [ANY_SPEC] * na,
            out_specs=[pl.BlockSpec((tm, 2 * n), lambda s, i, me: (i, (me[0] // 2) ^ s)), x_tile, ANY_SPEC]
            + [whole(a) for a in small_out] + [ANY_SPEC] * na,
            scratch_shapes=[pltpu.VMEM((nt, tm, d), BF16), pltpu.VMEM((2, d, 2 * n), BF16),
                            pltpu.SemaphoreType.DMA((4,))]
            + _ada_scratch(d, nloc, trows) + _comm_scratch(1) + _comm_scratch(na)),
        out_shape=(jax.ShapeDtypeStruct((t, N_DEV * n), BF16), jax.ShapeDtypeStruct((t, d), BF16),
                   jax.ShapeDtypeStruct((d, N_DEV * n), BF16)) + small_out
        + tuple(jax.ShapeDtypeStruct((N_DEV,) + a.shape, a.dtype) for a in shards),
        compiler_params=_params(("arbitrary", "arbitrary")),
    )(me_arr, x, prm, w_in_loc, *ada, *shards)
    return res[0], res[1], res[2], res[3], res[4], res[5], res[6:]


def _forward_mix(x, proj, mod, prm, taps, w_so, w_co, w_o, tm, shards):
    t, d = x.shape
    nt = t // tm
    na = len(shards)
    hs, hc = HALO_SHORT, HALO_CONF
    r3, r31 = 0, 8

    def body(*refs):
        x_ref, proj_ref, mod_ref, prm_ref, taps_ref, wso_ref, wco_ref, wo_ref = refs[:8]
        (x1_ref, ya_ref, yb_ref, mix_ref, u1_ref, c3_ref, sg_ref, ga_ref, gb_ref, lhs_ref) = refs[8 + na:18 + na]
        cv_ext, u0_ext, conv3_ref, u1f_ref = refs[18 + 2 * na:22 + 2 * na]
        gather = _Gather(refs[8:8 + na], refs[18 + na:18 + 2 * na], *refs[22 + 2 * na:])
        i = pl.program_id(0)

        @pl.when(i == 0)
        def _():
            gather.start()
            cv_ext[0:hs, :] = jnp.zeros((hs, d), F32)
            u0_ext[0:hc, :] = jnp.zeros((hc, d), F32)

        @pl.when(i == (3 * nt) // 4)
        def _():
            gather.forward()

        def col(g):
            return proj_ref[:, g * d:(g + 1) * d].astype(F32)

        cv_ext[hs:hs + tm, :] = col(1) * col(2)
        sg = _sigmoid(col(4))
        sg_ref[...] = sg.astype(BF16)
        u0_ext[hc:hc + tm, :] = col(3) * sg
        _tap_conv(cv_ext, taps_ref, [(r3 + wi, off) for wi, off in _causal_taps(SHORT_K, hs)], tm, hs, conv3_ref)
        _tap_conv(u0_ext, taps_ref, [(r31 + wi, off) for wi, off in _causal_taps(CONF_K, hc)], tm, hc, u1f_ref)
        cv_ext[0:hs, :] = cv_ext[tm:tm + hs, :]
        u0_ext[0:hc, :] = u0_ext[tm:tm + hc, :]

        c3_ref[...] = conv3_ref[...].astype(BF16)
        ya_pre = (col(0) * c3_ref[...].astype(F32)).astype(BF16)
        lhs_ref[0] = ya_pre
        y_a = _dot(ya_pre, wso_ref[...])
        u1 = u1f_ref[...] + prm_ref[P_CBIAS:P_CBIAS + 1, :]
        u1_ref[...] = u1.astype(BF16)
        u1 = u1_ref[...].astype(F32)
        mu = _rowmean(u1)
        uc = u1 - mu
        rstd = lax.rsqrt(_rowmean(uc * uc) + LN_EPS)
        u2 = uc * rstd * prm_ref[P_LNG:P_LNG + 1, :] + prm_ref[P_LNB:P_LNB + 1, :]
        u3 = (u2 * _sigmoid(u2)).astype(BF16)
        lhs_ref[1] = u3
        y_b = _dot(u3, wco_ref[...])
        ya_ref[...] = y_a.astype(BF16)
        yb_ref[...] = y_b.astype(BF16)
        ga = _sigmoid(col(5))
        gb = _sigmoid(col(6))
        ga_ref[...] = ga.astype(BF16)
        gb_ref[...] = gb.astype(BF16)
        merged = (ga * ya_ref[...].astype(F32) + gb * yb_ref[...].astype(F32)).astype(BF16)
        lhs_ref[2] = merged
        mix = _dot(merged, wo_ref[...])
        mix_ref[...] = mix.astype(BF16)
        x1_ref[...] = x_ref[...] + mod_ref[M_G1:M_G1 + 1, :] * mix

        @pl.when(i == nt - 1)
        def _():
            gather.finish()

    tile = pl.BlockSpec((tm, d), lambda i: (i, 0))
    whole = lambda shape: pl.BlockSpec(shape, lambda i: (0,) * len(shape))
    res = pl.pallas_call(
        body, name="forward_mix", grid=(nt,),
        out_shape=(jax.ShapeDtypeStruct((t, d), F32),) + (jax.ShapeDtypeStruct((t, d), BF16),) * 8
        + (jax.ShapeDtypeStruct((3, t, d), BF16),)
        + tuple(jax.ShapeDtypeStruct((N_DEV,) + a.shape, a.dtype) for a in shards),
        in_specs=[tile, pl.BlockSpec((tm, 7 * d), lambda i: (i, 0)), whole((8, d)), whole((8, d)),
                  whole(taps.shape), whole((d, d)), whole((d, d)), whole((d, d))] + [ANY_SPEC] * na,
        out_specs=[tile] * 9 + [pl.BlockSpec((3, tm, d), lambda i: (0, i, 0))] + [ANY_SPEC] * na,
        scratch_shapes=[pltpu.VMEM((hs + tm, d), F32), pltpu.VMEM((hc + tm, d), F32),
                        pltpu.VMEM((tm, d), F32), pltpu.VMEM((tm, d), F32)] + _comm_scratch(na),
        compiler_params=_params(("arbitrary",)),
    )(x, proj, mod, prm, taps, w_so, w_co, w_o, *shards)
    return res[:10], res[10:]


def _ffn_chunks(ff):
    mxu = 2 * LANES
    cut = (ff // mxu + 1) // 2 * mxu
    return [(0, cut), (cut, ff)] if 0 < cut < ff and ff % mxu == 0 else [(0, ff)]


def _forward_ffn(x1, tgt, mod, prm, w_fi_t, w_fo, tm):
    t, d = x1.shape
    ff = w_fo.shape[0]

    def body(x1_ref, tgt_ref, mod_ref, prm_ref, wfi_hbm, wfo_hbm,
             dx2_ref, ab_ref, h2_ref, act_ref, sums_ref, wfi_ref, wfo_ref):
        i = pl.program_id(0)

        @pl.when(i == 0)
        def _():
            pltpu.sync_copy(wfi_hbm, wfi_ref)
            pltpu.sync_copy(wfo_hbm, wfo_ref)
            sums_ref[...] = jnp.zeros((8, d), F32)

        x1v = x1_ref[...]
        r2 = lax.rsqrt(_rowmean(x1v * x1v) + EPS)
        h2 = (x1v * r2 * prm_ref[P_GFFN:P_GFFN + 1, :] * (1.0 + mod_ref[M_SC2:M_SC2 + 1, :])
              + mod_ref[M_SH2:M_SH2 + 1, :]).astype(BF16)
        h2_ref[...] = h2
        f = jnp.zeros((tm, d), F32)
        for c0, c1 in _ffn_chunks(ff):
            ab_ref[:, c0:c1] = _dot_nt(h2, wfi_ref[c0:c1, :]).astype(BF16)
            ab_ref[:, ff + c0:ff + c1] = _dot_nt(h2, wfi_ref[ff + c0:ff + c1, :]).astype(BF16)
            a = ab_ref[:, c0:c1].astype(F32)
            act = (a * _sigmoid(a) * ab_ref[:, ff + c0:ff + c1].astype(F32)).astype(BF16)
            act_ref[:, c0:c1] = act
            f = f + _dot(act, wfo_ref[c0:c1, :])
        x2 = x1v + mod_ref[M_G2:M_G2 + 1, :] * f
        r3 = lax.rsqrt(_rowmean(x2 * x2) + EPS)
        xn3 = x2 * r3
        gfin = prm_ref[P_GFIN:P_GFIN + 1, :]
        err = xn3 * gfin - tgt_ref[...]
        dy = err * (1.0 / d)
        dxn3 = dy * gfin
        dx2 = r3 * (dxn3 - xn3 * _rowmean(dxn3 * xn3))
        dx2_ref[...] = dx2
        sums_ref[0:1, :] = sums_ref[0:1, :] + _colsum(dy * xn3)
        sums_ref[1:2, :] = sums_ref[1:2, :] + _colsum(dx2 * f)
        sums_ref[2:3, :] = sums_ref[2:3, :] + _colsum(err * err) * (0.5 / d)

    tile = pl.BlockSpec((tm, d), lambda i: (i, 0))
    whole = lambda shape: pl.BlockSpec(shape, lambda i: (0,) * len(shape))
    return pl.pallas_call(
        body, name="forward_ffn", grid=(t // tm,),
        out_shape=(jax.ShapeDtypeStruct((t, d), F32), jax.ShapeDtypeStruct((t, 2 * ff), BF16),
                   jax.ShapeDtypeStruct((t, d), BF16), jax.ShapeDtypeStruct((t, ff), BF16),
                   jax.ShapeDtypeStruct((8, d), F32)),
        in_specs=[tile, tile, whole((8, d)), whole((8, d)), ANY_SPEC, ANY_SPEC],
        out_specs=[tile, pl.BlockSpec((tm, 2 * ff), lambda i: (i, 0)), tile,
                   pl.BlockSpec((tm, ff), lambda i: (i, 0)), whole((8, d))],
        scratch_shapes=[pltpu.VMEM(w_fi_t.shape, BF16), pltpu.VMEM(w_fo.shape, BF16)],
        compiler_params=_params(("arbitrary",)),
    )(x1, tgt, mod, prm, w_fi_t, w_fo)


def _backward_ffn(dx2, x1, ab, mod, prm, w_fi_t, w_fo, tm):
    t, d = x1.shape
    ff = w_fo.shape[0]

    def body(dx2_ref, x1_ref, ab_ref, mod_ref, prm_ref, wfi_hbm, wfo_hbm,
             dx1_ref, df_ref, dab_ref, sums_ref, wfi_ref, wfo_ref):
        i = pl.program_id(0)

        @pl.when(i == 0)
        def _():
            pltpu.sync_copy(wfi_hbm, wfi_ref)
            pltpu.sync_copy(wfo_hbm, wfo_ref)
            sums_ref[...] = jnp.zeros((8, d), F32)

        dx2v = dx2_ref[...]
        df = (mod_ref[M_G2:M_G2 + 1, :] * dx2v).astype(BF16)
        df_ref[...] = df
        dh2 = jnp.zeros((tm, d), F32)
        for c0, c1 in _ffn_chunks(ff):
            dact = _dot_nt(df, wfo_ref[c0:c1, :])
            a = ab_ref[:, c0:c1].astype(F32)
            b = ab_ref[:, ff + c0:ff + c1].astype(F32)
            s = _sigmoid(a)
            sil = a * s
            da =(dact * b * (s * (1.0 + a * (1.0 - s)))).astype(BF16)
            db = (dact * sil).astype(BF16)
            dab_ref[:, c0:c1] = da
            dab_ref[:, ff + c0:ff + c1] = db
            dh2 = dh2 + _dot(da, wfi_ref[c0:c1, :]) + _dot(db, wfi_ref[ff + c0:ff + c1, :])
        x1v = x1_ref[...]
        r2 = lax.rsqrt(_rowmean(x1v * x1v) + EPS)
        xn2 = x1v * r2
        gffn = prm_ref[P_GFFN:P_GFFN + 1, :]
        scale = 1.0 + mod_ref[M_SC2:M_SC2 + 1, :]
        dxn2 = dh2 * gffn * scale
        dx1_ref[...] = dx2v + r2 * (dxn2 - xn2 * _rowmean(dxn2 * xn2))
        hx = dh2 * xn2
        sums_ref[0:1, :] = sums_ref[0:1, :] + _colsum(dh2)
        sums_ref[1:2, :] = sums_ref[1:2, :] + _colsum(hx) * gffn
        sums_ref[2:3, :] = sums_ref[2:3, :] + _colsum(hx) * scale

    tile = pl.BlockSpec((tm, d), lambda i: (i, 0))
    whole = lambda shape: pl.BlockSpec(shape, lambda i: (0,) * len(shape))
    wide = lambda n: pl.BlockSpec((tm, n), lambda i: (i, 0))
    return pl.pallas_call(
        body, name="backward_ffn", grid=(t // tm,),
        out_shape=(jax.ShapeDtypeStruct((t, d), F32), jax.ShapeDtypeStruct((t, d), BF16),
                   jax.ShapeDtypeStruct((t, 2 * ff), BF16), jax.ShapeDtypeStruct((8, d), F32)),
        in_specs=[tile, tile, wide(2 * ff), whole((8, d)), whole((8, d)), ANY_SPEC, ANY_SPEC],
        out_specs=[tile, tile, wide(2 * ff), whole((8, d))],
        scratch_shapes=[pltpu.VMEM(w_fi_t.shape, BF16), pltpu.VMEM(w_fo.shape, BF16)],
        compiler_params=_params(("arbitrary",)),
    )(dx2, x1, ab, mod, prm, w_fi_t, w_fo)


def _backward_mix(dx1, proj, y_a, y_b, mix, u1, conv3, sg, ga, gb, mod, prm, taps, w_so, w_co, w_o, tm, partials):
    t, d = dx1.shape
    nt = t // tm
    na = len(partials)
    hs, hc = HALO_SHORT, HALO_CONF
    r3, r31 = 0, 8

    def body(*refs):
        (dx1_ref, proj_ref, halo_ref, ya_ref, yb_ref, mix_ref, u1_ref, c3_ref, sg_ref, ga_ref, gb_ref,
         mod_ref, prm_ref, taps_ref, wso_ref, wco_ref, wo_ref) = refs[:17]
        dproj_ref, rhs_ref, sums_ref, dw3_ref, dw31_ref = refs[17 + na:22 + na]
        cv_ext, u0_ext, d3_ext, du1_ext, tmp_ref = refs[22 + 2 * na:27 + 2 * na]
        scatter_start, scatter_finish = _scatter_plan(refs[17:17 + na], refs[22 + na:22 + 2 * na], *refs[27 + 2 * na:])
        i = pl.program_id(0)
        first_tile = i == nt - 1

        @pl.when(i == 0)
        def _():
            scatter_start()
            sums_ref[...] = jnp.zeros((8, d), F32)
            dw3_ref[...] = jnp.zeros(dw3_ref.shape, F32)
            dw31_ref[...] = jnp.zeros(dw31_ref.shape, F32)
            d3_ext[tm:tm + hs, :] = jnp.zeros((hs, d), F32)
            du1_ext[tm:tm + hc, :] = jnp.zeros((hc, d), F32)

        def col(g):
            return proj_ref[:, g * d:(g + 1) * d].astype(F32)

        def hcol(g, rows):
            v = halo_ref[HALO_CONF - rows:HALO_CONF, g * d:(g + 1) * d].astype(F32)
            return jnp.where(first_tile, 0.0, v)

        dx1v = dx1_ref[...]
        mixv = mix_ref[...].astype(F32)
        dmix = (mod_ref[M_G1:M_G1 + 1, :] * dx1v).astype(BF16)
        rhs_ref[2] = dmix
        sums_ref[0:1, :] = sums_ref[0:1, :] + _colsum(dx1v * mixv)
        dmerged = _dot_nt(dmix, wo_ref[...])
        ga = ga_ref[...].astype(F32)
        gb = gb_ref[...].astype(F32)
        yav = ya_ref[...].astype(F32)
        ybv = yb_ref[...].astype(F32)
        dya_f = dmerged * ga
        dyb_f = dmerged * gb
        dya = dya_f.astype(BF16)
        dyb = dyb_f.astype(BF16)
        rhs_ref[0] = dya
        rhs_ref[1] = dyb
        dproj_ref[:, 5 * d:6 * d] = (dya_f * yav * (1.0 - ga)).astype(BF16)
        dproj_ref[:, 6 * d:7 * d] = (dyb_f * ybv * (1.0 - gb)).astype(BF16)

        dya_pre = _dot_nt(dya, wso_ref[...])
        c_s, v_s, b_s = col(1), col(2), col(0)
        cv_ext[0:hs, :] = hcol(1, hs) * hcol(2, hs)
        cv_ext[hs:hs + tm, :] = c_s * v_s
        dproj_ref[:, 0:d] = (dya_pre * c3_ref[...].astype(F32)).astype(BF16)
        d3_ext[0:tm, :] = dya_pre * b_s
        _tap_wgrad(d3_ext, cv_ext, _causal_taps(SHORT_K, hs), tm, hs, dw3_ref)
        _tap_conv(d3_ext, taps_ref, [(r3 + wi, off) for wi, off in _anticausal_taps(SHORT_K)], tm, hs, tmp_ref)
        dcv = tmp_ref[...]
        dproj_ref[:, d:2 * d] = (dcv * v_s).astype(BF16)
        dproj_ref[:, 2 * d:3 * d] = (dcv * c_s).astype(BF16)
        d3_ext[tm:tm + hs, :] = d3_ext[0:hs, :]

        du3 = _dot_nt(dyb, wco_ref[...])
        u1v = u1_ref[...].astype(F32)
        mu = _rowmean(u1v)
        uc = u1v - mu
        rstd = lax.rsqrt(_rowmean(uc * uc) + LN_EPS)
        uhat = uc * rstd
        lng = prm_ref[P_LNG:P_LNG + 1, :]
        u2 = uhat * lng + prm_ref[P_LNB:P_LNB + 1, :]
        s2 = _sigmoid(u2)
        du2 = du3 * (s2 * (1.0 + u2 * (1.0 - s2)))
        sums_ref[1:2, :] = sums_ref[1:2, :] + _colsum(du2 * uhat)
        sums_ref[2:3, :] = sums_ref[2:3, :] + _colsum(du2)
        duhat = du2 * lng
        du1 = rstd * (duhat - _rowmean(duhat) - uhat * _rowmean(duhat * uhat))
        sums_ref[3:4, :] = sums_ref[3:4, :] + _colsum(du1)
        du1_ext[0:tm, :] = du1
        v_c = col(3)
        sg = sg_ref[...].astype(F32)
        u0_ext[0:hc, :] = hcol(3, hc) * _sigmoid(hcol(4, hc))
        u0_ext[hc:hc + tm, :] = v_c * sg
        _tap_wgrad(du1_ext, u0_ext, _causal_taps(CONF_K, hc), tm, hc, dw31_ref)
        _tap_conv(du1_ext, taps_ref, [(r31 + wi, off) for wi, off in _anticausal_taps(CONF_K)], tm, hc, tmp_ref)
        dv_c = tmp_ref[...] * sg
        dproj_ref[:, 3 * d:4 * d] = dv_c.astype(BF16)
        dproj_ref[:, 4 * d:5 * d] = (dv_c * v_c * (1.0 - sg)).astype(BF16)
        du1_ext[tm:tm + hc, :] = du1_ext[0:hc, :]

        @pl.when(i == nt - 1)
        def _():
            scatter_finish()

    rev = lambda i: (nt - 1 - i, 0)
    tile = pl.BlockSpec((tm, d), rev)
    whole = lambda shape: pl.BlockSpec(shape, lambda i: (0,) * len(shape))
    hblocks = tm // HALO_CONF
    halo = pl.BlockSpec((HALO_CONF, 7 * d), lambda i: (jnp.maximum((nt - 1 - i) * hblocks - 1, 0), 0))
    res = pl.pallas_call(
        body, name="backward_mix", grid=(nt,),
        out_shape=(jax.ShapeDtypeStruct((t, 7 * d), BF16), jax.ShapeDtypeStruct((3, t, d), BF16),
                   jax.ShapeDtypeStruct((8, d), F32),
                   jax.ShapeDtypeStruct((SUBLANES * SHORT_K, d), F32),
                   jax.ShapeDtypeStruct((SUBLANES * CONF_K, d), F32))
        + tuple(jax.ShapeDtypeStruct(p.shape, p.dtype) for p in partials),
        in_specs=[tile, pl.BlockSpec((tm, 4 * d), rev), halo, tile, tile, tile, tile, tile, tile, tile, tile,
                  whole((8, d)), whole((8, d)), whole(taps.shape), whole((d, d)), whole((d, d)), whole((d, d))]
        + [ANY_SPEC] * na,
        out_specs=[pl.BlockSpec((tm, 7 * d), rev), pl.BlockSpec((3, tm, d), lambda i: (0, nt - 1 - i, 0)),
                   whole((8, d)), whole((SUBLANES * SHORT_K, d)), whole((SUBLANES * CONF_K, d))] + [ANY_SPEC] * na,
        scratch_shapes=[pltpu.VMEM((hs + tm, d), F32), pltpu.VMEM((hc + tm, d), F32),
                        pltpu.VMEM((tm + hs, d), F32), pltpu.VMEM((tm + hc, d), F32),
                        pltpu.VMEM((tm, d), F32)] + _comm_scratch(na),
        compiler_params=_params(("arbitrary",)),
    )(dx1, proj, proj, y_a, y_b, mix, u1, conv3, sg, ga, gb, mod, prm, taps, w_so, w_co, w_o, *partials)
    return res[:5], res[5:]


def _backward_in(dproj, x, dx1, mod, prm, w_in_g, tm, partials, partial_view, exchanged_shapes):
    t, d = x.shape
    n_all = w_in_g.shape[1]
    na = len(partials)
    nt = t // tm

    def body(*refs):
        dproj_ref, x_ref, dx1_ref, mod_ref, prm_ref, w_hbm = refs[:6]
        gx_ref, sums_ref = refs[6 + na:8 + na]
        w_ref = refs[8 + 2 * na]
        scatter_start, scatter_finish = _scatter_plan(refs[6:6 + na], refs[8 + na:8 + 2 * na], *refs[9 + 2 * na:],
                                                      view=partial_view)

        @pl.when(pl.program_id(0) == 0)
        def _():
            scatter_start()
            pltpu.sync_copy(w_hbm, w_ref)
            sums_ref[...] = jnp.zeros((8, d), F32)

        dh = _dot_nt(dproj_ref[...], w_ref[...])
        xv = x_ref[...]
        r1 = lax.rsqrt(_rowmean(xv * xv) + EPS)
        xn = xv * r1
        gmix = prm_ref[P_GMIX:P_GMIX + 1, :]
        scale = 1.0 + mod_ref[M_SC1:M_SC1 + 1, :]
        dxn = dh * gmix * scale
        gx_ref[...] = dx1_ref[...] + r1 * (dxn - xn * _rowmean(dxn * xn))
        hx = dh * xn
        sums_ref[0:1, :] = sums_ref[0:1, :] + _colsum(dh)
        sums_ref[1:2, :] = sums_ref[1:2, :] + _colsum(hx) * gmix
        sums_ref[2:3, :] = sums_ref[2:3, :] + _colsum(hx) * scale

        @pl.when(pl.program_id(0) == nt - 1)
        def _():
            scatter_finish()

    tile = pl.BlockSpec((tm, d), lambda i: (i, 0))
    whole = pl.BlockSpec((8, d), lambda i: (0, 0))
    res = pl.pallas_call(
        body, name="backward_in", grid=(nt,),
        out_shape=(jax.ShapeDtypeStruct((t, d), F32), jax.ShapeDtypeStruct((8, d), F32))
        + tuple(jax.ShapeDtypeStruct(s, p.dtype) for p, s in zip(partials, exchanged_shapes)),
        in_specs=[pl.BlockSpec((tm, n_all), lambda i: (i, 0)), tile, tile, whole, whole, ANY_SPEC] + [ANY_SPEC] * na,
        out_specs=[tile, whole] + [ANY_SPEC] * na,
        scratch_shapes=[pltpu.VMEM(w_in_g.shape, BF16)] + _comm_scratch(na),
        compiler_params=_params(("arbitrary",)),
    )(dproj, x, dx1, mod, prm, w_in_g, *partials)
    return res[0], res[1], res[2:]


def _weight_grad(a, b, a_spec, b_spec, ns, m, n, nk, name, partials=(), by_columns=False,
                 partial_view=_leading_block, exchanged_shapes=None):
    na = len(partials)
    exchanged_shapes = exchanged_shapes or [p.shape for p in partials]

    def body(*refs):
        a_ref, b_ref = refs[:2]
        o_ref = refs[2 + na]
        acc_ref = refs[3 + 2 * na]
        s, k = pl.program_id(0), pl.program_id(1)
        if na:
            scatter_start, scatter_finish = _scatter_plan(refs[2:2 + na], refs[3 + na:3 + 2 * na], *refs[4 + 2 * na:],
                                                          view=partial_view)

            @pl.when((s == 0) & (k == 0))
            def _():
                scatter_start()

        av = a_ref[0] if len(a_ref.shape) == 3 else a_ref[...]
        bv = b_ref[0] if len(b_ref.shape) == 3 else b_ref[...]
        part = _dot_tn(av, bv)

        @pl.when(k == 0)
        def _():
            acc_ref[...] = part

        @pl.when(k > 0)
        def _():
            acc_ref[...] = acc_ref[...] + part

        @pl.when(k == nk - 1)
        def _():
            if by_columns:
                o_ref[...] = acc_ref[...].astype(BF16)
            else:
                o_ref[0] = acc_ref[...].astype(BF16)

        if na:
            @pl.when((s == ns - 1) & (k == nk - 1))
            def _():
                scatter_finish()

    if by_columns:
        out_shape, out_spec = (m, ns * n), pl.BlockSpec((m, n), lambda s, k: (0, s))
    else:
        out_shape, out_spec = (ns, m, n), pl.BlockSpec((1, m, n), lambda s, k: (s, 0, 0))
    res = pl.pallas_call(
        body, name=name, grid=(ns, nk),
        out_shape=(jax.ShapeDtypeStruct(out_shape, BF16),)
        + tuple(jax.ShapeDtypeStruct(sh, p.dtype) for p, sh in zip(partials, exchanged_shapes)),
        in_specs=[a_spec, b_spec] + [ANY_SPEC] * na,
        out_specs=[out_spec] + [ANY_SPEC] * na,
        scratch_shapes=[pltpu.VMEM((m, n), F32)] + (_comm_scratch(na) if na else []),
        compiler_params=_params(("arbitrary", "arbitrary")),
    )(a, b, *partials)
    return (res[0], res[1:]) if na else res[0]


def _adamw(w, g, m, v):
    m = ADAM_B1 * m + (1.0 - ADAM_B1) * g
    v = ADAM_B2 * v + (1.0 - ADAM_B2) * (g * g)
    m_hat = m / (1.0 - ADAM_B1 ** ADAM_STEP)
    v_hat = v / (1.0 - ADAM_B2 ** ADAM_STEP)
    delta = -ADAM_LR * (m_hat / (jnp.sqrt(v_hat) + ADAM_EPS) + ADAM_WD * w)
    return delta, m, v


def _adamw_shard(parts, w, m, v, tr, name, member=None):
    r, c = w.shape

    def body(p_ref, w_ref, m_ref, v_ref, g_ref, d_ref, nm_ref, nv_ref):
        part = (lambda s: p_ref[s]) if member is None else (lambda s: p_ref[s, 0])
        g = part(0).astype(F32)
        for s in range(1, N_DEV):
            g = g + part(s).astype(F32)
        delta, nm, nv = _adamw(w_ref[...], g, m_ref[...], v_ref[...])
        g_ref[...] = g
        d_ref[...] = delta
        nm_ref[...] = nm
        nv_ref[...] = nv

    tile = pl.BlockSpec((tr, c), lambda i: (i, 0))
    if member is None:
        parts_spec = pl.BlockSpec((N_DEV, tr, c), lambda i: (0, i, 0))
    else:
        parts_spec = pl.BlockSpec((N_DEV, 1, tr, c), lambda i: (0, member, i, 0))
    return pl.pallas_call(
        body, name=name, grid=(r // tr,),
        out_shape=(jax.ShapeDtypeStruct((r, c), F32),) * 4,
        in_specs=[parts_spec, tile, tile, tile],
        out_specs=[tile] * 4,
        compiler_params=_params(("arbitrary",)),
    )(parts, w, m, v)


def _ada_update(sc_all, dmod_cols, w, m, v, tr):
    d, n = w.shape

    def body(sc_ref, dm_ref, w_ref, m_ref, v_ref, g_ref, d_ref, nm_ref, nv_ref):
        g = lax.dot_general(sc_ref[...], dm_ref[...], (((0,), (0,)), ((), ())),
                            preferred_element_type=F32, precision=lax.Precision.HIGHEST)
        delta, nm, nv = _adamw(w_ref[...], g, m_ref[...], v_ref[...])
        g_ref[...] = g
        d_ref[...] = delta
        nm_ref[...] = nm
        nv_ref[...] = nv

    tile = pl.BlockSpec((tr, n), lambda i: (i, 0))
    return pl.pallas_call(
        body, name="ada_update", grid=(d // tr,),
        out_shape=(jax.ShapeDtypeStruct((d, n), F32),) * 4,
        in_specs=[pl.BlockSpec((N_DEV, tr), lambda i: (0, i)), pl.BlockSpec((N_DEV, n), lambda i: (0, 0)),
                  tile, tile, tile],
        out_specs=[tile] * 4,
        compiler_params=_params(("arbitrary",)),
    )(sc_all, dmod_cols, w, m, v)


def _small_exchange(vec, cg):
    l = vec.shape[2]
    rows = cg.shape[1]

    def body(vec_ref, cg_ref, vall_ref, cgr_ref, send_sems, recv_sems):
        x, y, c = _my_coords()
        me = _slot(x, y, c)
        vall_ref[me] = vec_ref[0]
        cgr_ref[me] = cg_ref[me]
        copies = []
        for k in range(1, N_DEV):
            peer = (_flip(x, k & 4), _flip(y, k & 2), _flip(c, k & 1))
            copies.append(pltpu.make_async_remote_copy(
                src_ref=vall_ref.at[me], dst_ref=vall_ref.at[me], send_sem=send_sems.at[k - 1],
                recv_sem=recv_sems.at[k - 1], device_id=peer, device_id_type=MESH))
            copies.append(pltpu.make_async_remote_copy(
                src_ref=cg_ref.at[_slot(*peer)], dst_ref=cgr_ref.at[me], send_sem=send_sems.at[7 + k - 1],
                recv_sem=recv_sems.at[7 + k - 1], device_id=peer, device_id_type=MESH))
        for cp in copies:
            cp.start()
        for cp in copies:
            cp.wait_recv()
        for cp in copies:
            cp.wait_send()

    vm = pl.BlockSpec(memory_space=pltpu.VMEM)
    return pl.pallas_call(
        body, name="small_exchange",
        out_shape=(jax.ShapeDtypeStruct((N_DEV, 1, l), F32), jax.ShapeDtypeStruct((N_DEV, rows, LANES), F32)),
        in_specs=[vm, vm], out_specs=[vm, vm],
        scratch_shapes=[pltpu.SemaphoreType.DMA((14,)), pltpu.SemaphoreType.DMA((14,))],
        compiler_params=_params(),
    )(vec, cg)


def _small_update(vall, cgr, smalls):
    ns = len(smalls)

    def body(*refs):
        vall_ref, cgr_ref = refs[0], refs[1]
        wmv = refs[2:2 + 3 * ns]
        outs = refs[2 + 3 * ns:]
        for p, (_, _, _, lo, hi, kind) in enumerate(smalls):
            w_ref, m_ref, v_ref = wmv[3 * p:3 * p + 3]
            part = (lambda s: vall_ref[s, :, lo:hi]) if kind == "vec" else (lambda s: cgr_ref[s, lo:hi, :])
            full = (lambda ref: ref[...]) if kind == "vec" else (lambda ref: ref[0])
            g = part(0)
            for s in range(1, N_DEV):
                g = g + part(s)
            delta, nm, nv = _adamw(full(w_ref), g, full(m_ref), full(v_ref))
            for o_ref, val in zip(outs[4 * p:4 * p + 4], (g, delta, nm, nv)):
                if kind == "vec":
                    o_ref[...] = val
                else:
                    o_ref[0] = val

    vm = pl.BlockSpec(memory_space=pltpu.VMEM)
    args = [vall, cgr]
    out_shape = []
    for w, m, v, _, _, _ in smalls:
        args += [w, m, v]
        out_shape += [jax.ShapeDtypeStruct(w.shape, F32)] * 4
    res = pl.pallas_call(
        body, name="small_update",
        out_shape=tuple(out_shape),
        in_specs=[vm] * len(args), out_specs=[vm] * len(out_shape),
        compiler_params=_params(),
    )(*args)
    return [res[4 * p:4 + 4 * p] for p in range(ns)]


def _pick(t, want):
    return want if t % want == 0 else t


def kernel(x, c, w_ada, b_ada, norm_mix_g, w_in, conv_short_w, w_short_out, conv_conf_w, conv_conf_b, conf_ln_g, conf_ln_b, w_conf_out, w_o, norm_ffn_g, w_ffn_in, w_ffn_out, final_norm_g, loss_target, m_w_ada, m_b_ada, m_norm_mix_g, m_w_in, m_conv_short_w, m_w_short_out, m_conv_conf_w, m_conv_conf_b, m_conf_ln_g, m_conf_ln_b, m_w_conf_out, m_w_o, m_norm_ffn_g, m_w_ffn_in, m_w_ffn_out, m_final_norm_g, v_w_ada, v_b_ada, v_norm_mix_g, v_w_in, v_conv_short_w, v_w_short_out, v_conv_conf_w, v_conv_conf_b, v_conf_ln_g, v_conf_ln_b, v_w_conf_out, v_w_o, v_norm_ffn_g, v_w_ffn_in, v_w_ffn_out, v_final_norm_g):
    t, d = x.shape[1], x.shape[2]
    x2 = x.reshape(t, d)
    tgt = loss_target.reshape(t, d)
    me = _slot(*_my_coords())
    tm = _pick(t, 256)
    tm_fwd_in = _pick(t, 512)
    tk = _pick(t, 2048)

    taps_loc = jnp.zeros((40, LANES), F32)
    taps_loc = taps_loc.at[0:SHORT_K].set(conv_short_w[0]).at[8:8 + CONF_K].set(conv_conf_w[0])
    prm = jnp.concatenate([norm_mix_g, norm_ffn_g, final_norm_g.reshape(1, d), conv_conf_b, conf_ln_g, conf_ln_b,
                           jnp.zeros((2, d), F32)], axis=0)

    n_in = w_in.shape[2]
    nk = t // tk
    tok = pl.BlockSpec((tk, d), lambda s, k: (k, 0))
    rows = d // N_DEV
    frows = w_ffn_out.shape[1]

    proj, h, w_in_g, mod, sc_all3, taps, (w_so_g, w_co_g, w_o_g) = _forward_in(
        jnp.reshape(me, (1,)).astype(jnp.int32), x2, prm, w_in[0].astype(BF16),
        (c, w_ada[0], b_ada, taps_loc), tm_fwd_in,
        [w_short_out[0].astype(BF16), w_conf_out[0].astype(BF16), w_o[0].astype(BF16)])
    w_so = w_so_g.reshape(d, d)
    w_co = w_co_g.reshape(d, d)
    w_oo = w_o_g.reshape(d, d)
    w_fi_t, m_fi_t, v_fi_t = w_ffn_in[0].T, m_w_ffn_in[0].T, v_w_ffn_in[0].T
    (x1, y_a, y_b, mix, u1, conv3, sg, ga, gb, mixer_lhs), (w_fi_g, w_fo_g) = _forward_mix(
        x2, proj, mod, prm, taps, w_so, w_co, w_oo, tm, [w_fi_t.astype(BF16), w_ffn_out[0].astype(BF16)])
    fb = w_fi_g.shape[1]
    ff = N_DEV * frows
    w_fi_all = w_fi_g.reshape(N_DEV * fb, d)
    w_fo_all = w_fo_g.reshape(ff, d)
    dx2, ab, h2, act, sums_f = _forward_ffn(x1, tgt, mod, prm, w_fi_all, w_fo_all, tm)

    dx1, df, dab, sums_b = _backward_ffn(dx2, x1, ab, mod, prm, w_fi_all, w_fo_all, tm)
    fb2 = 2 * fb
    pair_tok = pl.BlockSpec((tk, fb2), lambda s, k: (k, s))
    g_fi = _weight_grad(dab, h2, pair_tok, tok, N_DEV // 2, fb2, d, nk, "grad_w_ffn_in")
    g_fo = _weight_grad(act, df, pair_tok, tok, N_DEV // 4, fb2, d, nk, "grad_w_ffn_out")
    (dproj, mixer_rhs, sums_m, dw3p, dw31p), (p_fi, p_fo) = _backward_mix(
        dx1, proj, y_a, y_b, mix, u1, conv3, sg, ga, gb, mod, prm, taps, w_so, w_co, w_oo, tm,
        [g_fi.reshape(N_DEV, fb, d), g_fo.reshape(N_DEV, frows, d)])
    tk_in = _pick(t, 4096)
    nk_in = t // tk_in
    tok_in = pl.BlockSpec((tk_in, d), lambda s, k: (k, 0))
    member_tok = pl.BlockSpec((1, tk, d), lambda s, k: (s, k, 0))
    g_mixers = _weight_grad(mixer_lhs, mixer_rhs, member_tok, member_tok, 3, d, d, nk, "grad_w_mixers")
    n_blk = _pick(N_DEV * n_in, d)
    g_in, (p_mixers,) = _weight_grad(
        h, dproj, tok_in, pl.BlockSpec((tk_in, n_blk), lambda s, k: (k, s)),
        N_DEV * n_in // n_blk, d, n_blk, nk_in, "grad_w_in",
        [g_mixers.reshape(3, N_DEV, rows, d)], by_columns=True,
        partial_view=lambda ref, dev: ref.at[:, dev], exchanged_shapes=[(N_DEV, 3, rows, d)])
    grad_x, sums_i, (p_in,) = _backward_in(dproj, x2, dx1, mod, prm, w_in_g, tm, [g_in], _column_block(n_in),
                                           [(N_DEV, d, n_in)])

    up_in = _adamw_shard(p_in, w_in[0], m_w_in[0], v_w_in[0], _pick(d, 256), "adamw_w_in")
    up_so = _adamw_shard(p_mixers, w_short_out[0], m_w_short_out[0], v_w_short_out[0], rows, "adamw_w_short_out", 0)
    up_co = _adamw_shard(p_mixers, w_conf_out[0], m_w_conf_out[0], v_w_conf_out[0], rows, "adamw_w_conf_out", 1)
    up_oo = _adamw_shard(p_mixers, w_o[0], m_w_o[0], v_w_o[0], rows, "adamw_w_o", 2)
    up_fi = tuple(a.T for a in _adamw_shard(p_fi, w_fi_t, m_fi_t, v_fi_t, fb // 2, "adamw_w_ffn_in"))
    up_fo = _adamw_shard(p_fo, w_ffn_out[0], m_w_ffn_out[0], v_w_ffn_out[0], frows, "adamw_w_ffn_out")

    vec = jnp.concatenate([sums_i[0:2], sums_m[0:1], sums_b[0:2], sums_f[1:2],
                           sums_i[2:3], sums_m[3:4], sums_m[1:3], sums_b[2:3], sums_f[0:1],
                           sums_f[2:3]], axis=0)
    vec = vec.reshape(1, 1, 13 * d)
    dw3 = dw3p.reshape(SHORT_K, SUBLANES, d).sum(axis=1)
    dw31 = dw31p.reshape(CONF_K, SUBLANES, d).sum(axis=1)
    cg = jnp.zeros((40, d), F32).at[0:SHORT_K].set(dw3).at[8:8 + CONF_K].set(dw31)
    cg = cg.reshape(40, N_DEV, LANES).transpose(1, 0, 2)
    fin = lambda a: a.reshape(1, d)
    smalls = [
        (b_ada, m_b_ada, v_b_ada, 0, 6 * d, "vec"),
        (norm_mix_g, m_norm_mix_g, v_norm_mix_g, 6 * d, 7 * d, "vec"),
        (conv_short_w, m_conv_short_w, v_conv_short_w, 0, SHORT_K, "cg"),
        (conv_conf_w, m_conv_conf_w, v_conv_conf_w, 8, 8 + CONF_K, "cg"),
        (conv_conf_b, m_conv_conf_b, v_conv_conf_b, 7 * d, 8 * d, "vec"),
        (conf_ln_g, m_conf_ln_g, v_conf_ln_g, 8 * d, 9 * d, "vec"),
        (conf_ln_b, m_conf_ln_b, v_conf_ln_b, 9 * d, 10 * d, "vec"),
        (norm_ffn_g, m_norm_ffn_g, v_norm_ffn_g, 10 * d, 11 * d, "vec"),
        (fin(final_norm_g), fin(m_final_norm_g), fin(v_final_norm_g), 11 * d, 12 * d, "vec"),
    ]
    vall, cgr = _small_exchange(vec, cg)
    up_small = _small_update(vall, cgr, smalls)
    n_ada = w_ada.shape[2]
    dmod_all = vall.reshape(N_DEV, 13 * d)[:, 0:6 * d]
    dmod_cols = lax.dynamic_slice(dmod_all, (0, me * n_ada), (N_DEV, n_ada))
    up_ada = _ada_update(sc_all3.reshape(N_DEV, d), dmod_cols, w_ada[0], m_w_ada[0], v_w_ada[0], _pick(d, 256))

    loss = jnp.sum(vall.reshape(N_DEV, 13 * d)[:, 12 * d:])

    lead = lambda a: a.reshape((1,) + a.shape)
    ups = [tuple(lead(a) for a in up_ada), up_small[0], up_small[1], tuple(lead(a) for a in up_in),
           up_small[2], tuple(lead(a) for a in up_so), up_small[3],
           up_small[4], up_small[5], up_small[6],
           tuple(lead(a) for a in up_co), tuple(lead(a) for a in up_oo), up_small[7],
           tuple(lead(a) for a in up_fi), tuple(lead(a) for a in up_fo),
           tuple(a.reshape(d) for a in up_small[8])]
    grads = [u[0] for u in ups]
    deltas = [u[1] for u in ups]
    new_m = [u[2] for u in ups]
    new_v = [u[3] for u in ups]
    return (loss, grad_x.reshape(1, t, d), *grads, *deltas, *new_m, *new_v)
```

```python
import functools

import jax
import jax.numpy as jnp
from jax import lax
from jax.experimental import pallas as pl
from jax.experimental.pallas import tpu as pltpu

F32 = jnp.float32
BF16 = jnp.bfloat16
MESH = pl.DeviceIdType.MESH

N_DEV = 8
EPS = 1e-6
LN_EPS = 1e-5
SHORT_K = 3
CONF_K = 31
ADAM_LR = 0.001
ADAM_B1 = 0.9
ADAM_B2 = 0.999
ADAM_EPS = 1e-08
ADAM_WD = 0.01
ADAM_STEP = 10

LANES = 128
SUBLANES = 8
CONV_ROWS = 64
HALO_SHORT = 8
HALO_CONF = 32
VMEM_LIMIT = 56 * 1024 * 1024

M_SH1, M_SC1, M_G1, M_SH2, M_SC2, M_G2 = range(6)
P_GMIX, P_GFFN, P_GFIN, P_CBIAS, P_LNG, P_LNB = range(6)


def _params(sem=None, **kw):
    return pltpu.CompilerParams(dimension_semantics=sem, vmem_limit_bytes=VMEM_LIMIT, **kw)


def _sigmoid(v):
    return jax.nn.sigmoid(v)


def _dot(a, b):
    return jnp.dot(a, b, preferred_element_type=F32)


def _dot_nt(a, b):
    return lax.dot_general(a, b, (((1,), (1,)), ((), ())), preferred_element_type=F32)


def _dot_tn(a, b):
    return lax.dot_general(a, b, (((0,), (0,)), ((), ())), preferred_element_type=F32)


def _colsum(v):
    return jnp.sum(v, axis=0, keepdims=True)


def _rowmean(v):
    return jnp.mean(v, axis=-1, keepdims=True)


def _my_coords():
    return lax.axis_index("x"), lax.axis_index("y"), lax.axis_index("c")


def _slot(px, py, pc):
    return 4 * px + 2 * py + pc


def _flip(v, bit):
    return 1 - v if bit else v


def _taps_by_residue(taps):
    by_res = {}
    for wi, off in taps:
        by_res.setdefault(off % SUBLANES, []).append((wi, off // SUBLANES))
    return sorted(by_res.items())


def _tap_conv(ext_ref, w_ref, taps, tm, extra, out_ref):
    d = out_ref.shape[1]
    rb = min(CONV_ROWS, tm)
    wrows = rb + extra
    groups = _taps_by_residue(taps)

    def block(i, carry):
        base = pl.multiple_of(i * rb, SUBLANES)
        for lc in range(d // LANES):
            ls = pl.ds(lc * LANES, LANES)
            win = ext_ref[pl.ds(base, wrows), ls]
            acc = None
            for r, lst in groups:
                sh = win if r == 0 else pltpu.roll(win, wrows - r, 0)
                for wi, q in lst:
                    term = w_ref[wi:wi + 1, ls] * sh[SUBLANES * q:SUBLANES * q + rb, :]
                    acc = term if acc is None else acc + term
            out_ref[pl.ds(base, rb), ls] = acc
        return carry

    lax.fori_loop(0, tm // rb, block, 0)


def _tap_wgrad(a_ref, ext_ref, taps, tm, extra, acc_ref):
    d = a_ref.shape[1]
    rb = min(CONV_ROWS, tm)
    wrows = rb + extra
    groups = _taps_by_residue(taps)

    def block(i, carry):
        base = pl.multiple_of(i * rb, SUBLANES)
        for lc in range(d // LANES):
            ls = pl.ds(lc * LANES, LANES)
            a_blk = a_ref[pl.ds(base, rb), ls]
            win = ext_ref[pl.ds(base, wrows), ls]
            for r, lst in groups:
                sh = win if r == 0 else pltpu.roll(win, wrows - r, 0)
                for wi, q in lst:
                    prod = a_blk * sh[SUBLANES * q:SUBLANES * q + rb, :]
                    part = prod[0:SUBLANES, :]
                    for s in range(1, rb // SUBLANES):
                        part = part + prod[SUBLANES * s:SUBLANES * (s + 1), :]
                    rows = pl.ds(SUBLANES * wi, SUBLANES)
                    acc_ref[rows, ls] = acc_ref[rows, ls] + part
        return carry

    lax.fori_loop(0, tm // rb, block, 0)


def _causal_taps(k, halo):
    return [(i, halo - (k - 1) + i) for i in range(k)]


def _anticausal_taps(k):
    return [(i, (k - 1) - i) for i in range(k)]


def _ada_scratch(d, nloc, trows):
    return [pltpu.VMEM((N_DEV, 1, d), F32), pltpu.VMEM((N_DEV, 1, nloc), F32), pltpu.VMEM((N_DEV, 1, nloc), F32),
            pltpu.VMEM((N_DEV, trows, LANES), F32), pltpu.SemaphoreType.DMA((21,)), pltpu.SemaphoreType.DMA((21,))]


def _ada_exchange(c_ref, w_ref, b_ref, t_ref, mod_ref, sc_ref, taps_ref,
                  scall_ref, part_ref, modrecv_ref, tapsall_ref, send_sems, recv_sems):
    x, y, cc = _my_coords()
    me = _slot(x, y, cc)
    d = c_ref.shape[1]
    cv = c_ref[...]
    scall_ref[me] = cv * _sigmoid(cv)
    tapsall_ref[me] = t_ref[...]

    def peer_of(k):
        return (_flip(x, k & 4), _flip(y, k & 2), _flip(cc, k & 1))

    def gather_copy(ref, base, k):
        return pltpu.make_async_remote_copy(
            src_ref=ref.at[me], dst_ref=ref.at[me], send_sem=send_sems.at[base + k - 1],
            recv_sem=recv_sems.at[base + k - 1], device_id=peer_of(k), device_id_type=MESH)

    first = [gather_copy(scall_ref, 0, k) for k in range(1, N_DEV)]
    first += [gather_copy(tapsall_ref, 7, k) for k in range(1, N_DEV)]
    for cp in first:
        cp.start()
    for cp in first[:7]:
        cp.wait_recv()
    sc_all = jnp.concatenate([scall_ref[s] for s in range(N_DEV)], axis=0)
    for s in range(N_DEV):
        sc_ref[s] = scall_ref[s]
    part = jnp.dot(sc_all, w_ref[...], preferred_element_type=F32,
                   precision=lax.Precision.HIGHEST)
    for b in range(N_DEV):
        part_ref[b] = part[b:b + 1, :]
    modrecv_ref[me] = part_ref[me]
    second = []
    for k in range(1, N_DEV):
        px, py, pc = peer_of(k)
        second.append(pltpu.make_async_remote_copy(
            src_ref=part_ref.at[_slot(px, py, pc)], dst_ref=modrecv_ref.at[me],
            send_sem=send_sems.at[14 + k - 1], recv_sem=recv_sems.at[14 + k - 1],
            device_id=(px, py, pc), device_id_type=MESH))
    for cp in second:
        cp.start()
    for cp in second:
        cp.wait_recv()
    mod = jnp.concatenate([modrecv_ref[s] for s in range(N_DEV)], axis=1) + b_ref[...]
    for r in range(6):
        mod_ref[r:r + 1, :] = mod[:, r * d:(r + 1) * d]
    mod_ref[6:8, :] = jnp.zeros((2, d), F32)
    for cp in first[7:]:
        cp.wait_recv()
    taps_ref[...] = jnp.concatenate([tapsall_ref[s] for s in range(N_DEV)], axis=1)
    for cp in first + second:
        cp.wait_send()


ANY_SPEC = pl.BlockSpec(memory_space=pl.ANY)


def _comm_scratch(na):
    return [pltpu.SemaphoreType.DMA((7 * na,)), pltpu.SemaphoreType.DMA((7 * na,)), pltpu.SemaphoreType.DMA((na,))]


def _leading_block(ref, slot):
    return ref.at[slot]


def _column_block(width):
    def view(ref, slot):
        return ref.at[:, pl.ds(pl.multiple_of(slot * width, LANES), width)]
    return view


class _Gather:
    def __init__(self, ins, outs, send_sems, recv_sems, local_sems, view=_leading_block):
        self.na = len(ins)
        x, y, c = _my_coords()
        self.c = c
        self.view = view
        self.me, self.sibling = (x, y, c), (x, y, 1 - c)
        self.chips = [(1 - x, y), (x, 1 - y), (1 - x, 1 - y)]
        self.outs, self.send_sems, self.recv_sems = outs, send_sems, recv_sems
        self.mine = [pltpu.make_async_copy(ins[a], view(outs[a], _slot(*self.me)), local_sems.at[a])
                     for a in range(self.na)]
        self.first = []
        for a in range(self.na):
            self.first.append(self._copy(a, 0, self.me, self.sibling, src=ins[a]))
            self.first += [self._copy(a, 1 + j, self.me, (*chip, c), src=ins[a]) for j, chip in enumerate(self.chips)]
        self.passed = [self._copy(a, 4 + j, (*chip, c), self.sibling)
                       for a in range(self.na) for j, chip in enumerate(self.chips)]

    def _copy(self, a, k, block, to, src=None):
        dst = self.view(self.outs[a], _slot(*block))
        return pltpu.make_async_remote_copy(
            src_ref=dst if src is None else src, dst_ref=dst,
            send_sem=self.send_sems.at[7 * a + k], recv_sem=self.recv_sems.at[7 * a + k],
            device_id=to, device_id_type=MESH)

    def start(self):
        self.start_near()
        self.start_far()

    def start_near(self):
        self.start_local()
        self.start_one_hop()

    def start_local(self):
        for cp in self.mine + self.first[0::4]:
            cp.start()

    def start_one_hop(self):
        for cp in self.first[1::4] + self.first[2::4]:
            cp.start()

    def start_far(self):
        for cp in self.first[3::4]:
            cp.start()

    def forward_chip(self, j):
        for a in range(self.na):
            self._copy(a, 1 + j, (*self.chips[j], self.c), self.me).wait_recv()
            self.passed[3 * a + j].start()

    def wait_sibling_own(self):
        for a in range(self.na):
            self._copy(a, 0, self.sibling, self.me).wait_recv()

    def wait_sibling_passed(self, j):
        for a in range(self.na):
            self._copy(a, 4 + j, (*self.chips[j], 1 - self.c), self.me).wait_recv()

    def finish_sends(self):
        for cp in self.first + self.passed:
            cp.wait_send()
        for cp in self.mine:
            cp.wait()

    def forward(self):
        for j in range(3):
            self.forward_chip(j)

    def finish(self):
        self.wait_sibling_own()
        for j in range(3):
            self.wait_sibling_passed(j)
        self.finish_sends()


def _scatter_plan(ins, outs, send_sems, recv_sems, local_sems, view=_leading_block):
    na = len(ins)
    x, y, c = _my_coords()
    me = _slot(x, y, c)
    mine = [pltpu.make_async_copy(view(ins[a], me), outs[a].at[me], local_sems.at[a]) for a in range(na)]
    copies = []
    for k in range(1, N_DEV):
        peer = (_flip(x, k & 4), _flip(y, k & 2), _flip(c, k & 1))
        for a in range(na):
            copies.append(pltpu.make_async_remote_copy(
                src_ref=view(ins[a], _slot(*peer)), dst_ref=outs[a].at[me],
                send_sem=send_sems.at[7 * a + k - 1], recv_sem=recv_sems.at[7 * a + k - 1],
                device_id=peer, device_id_type=MESH))

    def start():
        for cp in mine + copies:
            cp.start()

    def finish():
        for cp in copies:
            cp.wait_recv()
        for cp in copies:
            cp.wait_send()
        for cp in mine:
            cp.wait()

    return start, finish


def _forward_in(me_arr, x, prm, w_in_loc, ada, tm, shards):
    t, d = x.shape
    n = w_in_loc.shape[1]
    ns = N_DEV // 2
    na = len(shards)
    nt = t // tm
    nloc = ada[1].shape[1]
    trows = ada[3].shape[0]
    chip_of_pass = {1: 1, 2: 0, 3: 2}

    def body(*refs):
        x_ref, prm_ref, wloc_ref = refs[1:4]
        ada_in = refs[4:8]
        shard_refs = refs[8:8 + na]
        proj_ref, h_ref, wg_ref, mod_ref, sc_ref, taps_ref = refs[8 + na:14 + na]
        gathered_refs = refs[14 + na:14 + 2 * na]
        hall_ref, wv_ref, wv_sem = refs[14 + 2 * na:17 + 2 * na]
        ada_scratch = refs[17 + 2 * na:23 + 2 * na]
        sems = refs[23 + 2 * na:]
        s, i = pl.program_id(0), pl.program_id(1)
        columns = _column_block(n)

        def gathers():
            return (_Gather([wloc_ref], [wg_ref], *sems[0:3], view=columns),
                    _Gather(shard_refs, gathered_refs, *sems[3:6]))

        def shard_copy(dev, p, other_core, src=None):
            half = pl.ds(pl.multiple_of((dev & 1) * n, LANES), n)
            return pltpu.make_async_copy(columns(wg_ref, dev) if src is None else src,
                                         wv_ref.at[p % 2, :, half], wv_sem.at[2 * (p % 2) + other_core])

        def pair_copies(p, me):
            return [shard_copy(me ^ (2 * p), p, 0), shard_copy(me ^ (2 * p) ^ 1, p, 1)]

        @pl.when(i == 0)
        def _():
            me = _slot(*_my_coords())

            @pl.when(s == 0)
            def _():
                g_in, _ = gathers()
                g_in.start_local()
                _ada_exchange(*ada_in, mod_ref, sc_ref, taps_ref, *ada_scratch)
                g_in.start_one_hop()
                own = shard_copy(me, 0, 0, src=wloc_ref)
                own.start()
                g_in.wait_sibling_own()
                sib = shard_copy(me ^ 1, 0, 1)
                sib.start()
                own.wait()
                sib.wait()

            for p in range(1, ns):
                @pl.when(s == p)
                def _(p=p):
                    for cp in pair_copies(p, me):
                        cp.wait()

        @pl.when(i == max(nt - 3, 0))
        def _():
            g_in, g_rest = gathers()
            for p in range(1, ns):
                @pl.when(s == p - 1)
                def _(p=p):
                    g_in.forward_chip(chip_of_pass[p])
                    if p == 1:
                        g_in.start_far()
                    if p == 2:
                        g_rest.start()
                    if p == 3:
                        g_rest.forward()

        @pl.when(i == nt - 1)
        def _():
            g_in, _ = gathers()
            me = _slot(*_my_coords())
            for p in range(1, ns):
                @pl.when(s == p - 1)
                def _(p=p):
                    g_in.wait_sibling_passed(chip_of_pass[p])
                    for cp in pair_copies(p, me):
                        cp.start()

        @pl.when(s == 0)
        def _():
            xv = x_ref[...]
            r = lax.rsqrt(_rowmean(xv * xv) + EPS)
            h = xv * r * prm_ref[P_GMIX:P_GMIX + 1, :] * (1.0 + mod_ref[M_SC1:M_SC1 + 1, :]) \
                + mod_ref[M_SH1:M_SH1 + 1, :]
            hb = h.astype(BF16)
            hall_ref[i] = hb
            h_ref[...] = hb

        proj_ref[...] = _dot(hall_ref[i], wv_ref[s % 2]).astype(BF16)

        @pl.when((s == ns - 1) & (i == nt - 1))
        def _():
            g_in, g_rest = gathers()
            g_in.finish_sends()
            g_rest.finish()

    x_tile = pl.BlockSpec((tm, d), lambda s, i, me: (jnp.where(s == 0, i, nt - 1), 0))
    whole = lambda a: pl.BlockSpec(a.shape, lambda s, i, me: (0,) * len(a.shape))
    small_out = (jax.ShapeDtypeStruct((8, d), F32), jax.ShapeDtypeStruct((N_DEV, 1, d), F32),
                 jax.ShapeDtypeStruct((trows, N_DEV * LANES), F32))
    res = pl.pallas_call(
        body, name="forward_in",
        grid_spec=pltpu.PrefetchScalarGridSpec(
            num_scalar_prefetch=1, grid=(ns, nt),
            in_specs=[x_tile, whole(prm), ANY_SPEC] + [whole(a) for a in ada] + [ANY_SPEC] * na,
            out_specs=[pl.BlockSpec((tm, 2 * n), lambda s, i, me: (i, (me[0] // 2) ^ s)), x_tile, ANY_SPEC]
            + [whole(a) for a in small_out] + [ANY_SPEC] * na,
            scratch_shapes=[pltpu.VMEM((nt, tm, d), BF16), pltpu.VMEM((2, d, 2 * n), BF16),
                            pltpu.SemaphoreType.DMA((4,))]
            + _ada_scratch(d, nloc, trows) + _comm_scratch(1) + _comm_scratch(na)),
        out_shape=(jax.ShapeDtypeStruct((t, N_DEV * n), BF16), jax.ShapeDtypeStruct((t, d), BF16),
                   jax.ShapeDtypeStruct((d, N_DEV * n), BF16)) + small_out
        + tuple(jax.ShapeDtypeStruct((N_DEV,) + a.shape, a.dtype) for a in shards),
        compiler_params=_params(("arbitrary", "arbitrary")),
    )(me_arr, x, prm, w_in_loc, *ada, *shards)
    return res[0], res[1], res[2], res[3], res[4], res[5], res[6:]


def _forward_mix(x, proj, mod, prm, taps, w_so, w_co, w_o, tm, shards):
    t, d = x.shape
    nt = t // tm
    na = len(shards)
    hs, hc = HALO_SHORT, HALO_CONF
    r3, r31 = 0, 8

    def body(*refs):
        x_ref, proj_ref, mod_ref, prm_ref, taps_ref, wso_ref, wco_ref, wo_ref = refs[:8]
        (x1_ref, ya_ref, yb_ref, mix_ref, uhat_ref, c3_ref, sg_ref, ga_ref, gb_ref, lhs_ref,
         rstd_ref) = refs[8 + na:19 + na]
        cv_ext, u0_ext, conv3_ref, u1f_ref = refs[19 + 2 * na:23 + 2 * na]
        gather = _Gather(refs[8:8 + na], refs[19 + na:19 + 2 * na], *refs[23 + 2 * na:])
        i = pl.program_id(0)

        @pl.when(i == 0)
        def _():
            gather.start()
            cv_ext[0:hs, :] = jnp.zeros((hs, d), F32)
            u0_ext[0:hc, :] = jnp.zeros((hc, d), F32)

        @pl.when(i == (3 * nt) // 4)
        def _():
            gather.forward()

        def col(g):
            return proj_ref[:, g * d:(g + 1) * d].astype(F32)

        cv_ext[hs:hs + tm, :] = col(1) * col(2)
        sg = _sigmoid(col(4))
        sg_ref[...] = sg.astype(BF16)
        u0_ext[hc:hc + tm, :] = col(3) * sg
        _tap_conv(cv_ext, taps_ref, [(r3 + wi, off) for wi, off in _causal_taps(SHORT_K, hs)], tm, hs, conv3_ref)
        _tap_conv(u0_ext, taps_ref, [(r31 + wi, off) for wi, off in _causal_taps(CONF_K, hc)], tm, hc, u1f_ref)
        cv_ext[0:hs, :] = cv_ext[tm:tm + hs, :]
        u0_ext[0:hc, :] = u0_ext[tm:tm + hc, :]

        c3_ref[...] = conv3_ref[...].astype(BF16)
        ya_pre = (col(0) * c3_ref[...].astype(F32)).astype(BF16)
        lhs_ref[0] = ya_pre
        y_a = _dot(ya_pre, wso_ref[...])
        u1 = u1f_ref[...] + prm_ref[P_CBIAS:P_CBIAS + 1, :]
        mu = _rowmean(u1)
        uc = u1 - mu
        rstd = lax.rsqrt(_rowmean(uc * uc) + LN_EPS)
        rstd_ref[...] = rstd
        uhat_ref[...] = (uc * rstd).astype(BF16)
        u2 = uhat_ref[...].astype(F32) * prm_ref[P_LNG:P_LNG + 1, :] + prm_ref[P_LNB:P_LNB + 1, :]
        u3 = (u2 * _sigmoid(u2)).astype(BF16)
        lhs_ref[1] = u3
        y_b = _dot(u3, wco_ref[...])
        ya_ref[...] = y_a.astype(BF16)
        yb_ref[...] = y_b.astype(BF16)
        ga = _sigmoid(col(5))
        gb = _sigmoid(col(6))
        ga_ref[...] = ga.astype(BF16)
        gb_ref[...] = gb.astype(BF16)
        merged = (ga * ya_ref[...].astype(F32) + gb * yb_ref[...].astype(F32)).astype(BF16)
        lhs_ref[2] = merged
        mix = _dot(merged, wo_ref[...])
        mix_ref[...] = mix.astype(BF16)
        x1_ref[...] = x_ref[...] + mod_ref[M_G1:M_G1 + 1, :] * mix

        @pl.when(i == nt - 1)
        def _():
            gather.finish()

    tile = pl.BlockSpec((tm, d), lambda i: (i, 0))
    whole = lambda shape: pl.BlockSpec(shape, lambda i: (0,) * len(shape))
    res = pl.pallas_call(
        body, name="forward_mix", grid=(nt,),
        out_shape=(jax.ShapeDtypeStruct((t, d), F32),) + (jax.ShapeDtypeStruct((t, d), BF16),) * 8
        + (jax.ShapeDtypeStruct((3, t, d), BF16), jax.ShapeDtypeStruct((t, 1), F32))
        + tuple(jax.ShapeDtypeStruct((N_DEV,) + a.shape, a.dtype) for a in shards),
        in_specs=[tile, pl.BlockSpec((tm, 7 * d), lambda i: (i, 0)), whole((8, d)), whole((8, d)),
                  whole(taps.shape), whole((d, d)), whole((d, d)), whole((d, d))] + [ANY_SPEC] * na,
        out_specs=[tile] * 9 + [pl.BlockSpec((3, tm, d), lambda i: (0, i, 0)), pl.BlockSpec((tm, 1), lambda i: (i, 0))]
        + [ANY_SPEC] * na,
        scratch_shapes=[pltpu.VMEM((hs + tm, d), F32), pltpu.VMEM((hc + tm, d), F32),
                        pltpu.VMEM((tm, d), F32), pltpu.VMEM((tm, d), F32)] + _comm_scratch(na),
        compiler_params=_params(("arbitrary",)),
    )(x, proj, mod, prm, taps, w_so, w_co, w_o, *shards)
    return res[:11], res[11:]


def _ffn_chunks(ff):
    mxu = 2 * LANES
    cut = (ff // mxu + 1) // 2 * mxu
    return [(0, cut), (cut, ff)] if 0 < cut < ff and ff % mxu == 0 else [(0, ff)]


def _forward_ffn(x1, tgt, mod, prm, w_fi_t, w_fo, tm):
    t, d = x1.shape
    ff = w_fo.shape[0]

    def body(x1_ref, tgt_ref, mod_ref, prm_ref, wfi_hbm, wfo_hbm,
             dx2_ref, ab_ref, h2_ref, act_ref, sums_ref, wfi_ref, wfo_ref):
        i = pl.program_id(0)

        @pl.when(i == 0)
        def _():
            pltpu.sync_copy(wfi_hbm, wfi_ref)
            pltpu.sync_copy(wfo_hbm, wfo_ref)
            sums_ref[...] = jnp.zeros((8, d), F32)

        x1v = x1_ref[...]
        r2 = lax.rsqrt(_rowmean(x1v * x1v) + EPS)
        h2 = (x1v * r2 * prm_ref[P_GFFN:P_GFFN + 1, :] * (1.0 + mod_ref[M_SC2:M_SC2 + 1, :])
              + mod_ref[M_SH2:M_SH2 + 1, :]).astype(BF16)
        h2_ref[...] = h2
        f = jnp.zeros((tm, d), F32)
        for c0, c1 in _ffn_chunks(ff):
            ab_ref[:, c0:c1] = _dot_nt(h2, wfi_ref[c0:c1, :]).astype(BF16)
            ab_ref[:, ff + c0:ff + c1] = _dot_nt(h2, wfi_ref[ff + c0:ff + c1, :]).astype(BF16)
            a = ab_ref[:, c0:c1].astype(F32)
            act = (a * _sigmoid(a) * ab_ref[:, ff + c0:ff + c1].astype(F32)).astype(BF16)
            act_ref[:, c0:c1] = act
            f = f + _dot(act, wfo_ref[c0:c1, :])
        x2 = x1v + mod_ref[M_G2:M_G2 + 1, :] * f
        r3 = lax.rsqrt(_rowmean(x2 * x2) + EPS)
        xn3 = x2 * r3
        gfin = prm_ref[P_GFIN:P_GFIN + 1, :]
        err = xn3 * gfin - tgt_ref[...]
        dy = err * (1.0 / d)
        dxn3 = dy * gfin
        dx2 = r3 * (dxn3 - xn3 * _rowmean(dxn3 * xn3))
        dx2_ref[...] = dx2
        sums_ref[0:1, :] = sums_ref[0:1, :] + _colsum(dy * xn3)
        sums_ref[1:2, :] = sums_ref[1:2, :] + _colsum(dx2 * f)
        sums_ref[2:3, :] = sums_ref[2:3, :] + _colsum(err * err) * (0.5 / d)

    tile = pl.BlockSpec((tm, d), lambda i: (i, 0))
    whole = lambda shape: pl.BlockSpec(shape, lambda i: (0,) * len(shape))
    return pl.pallas_call(
        body, name="forward_ffn", grid=(t // tm,),
        out_shape=(jax.ShapeDtypeStruct((t, d), F32), jax.ShapeDtypeStruct((t, 2 * ff), BF16),
                   jax.ShapeDtypeStruct((t, d), BF16), jax.ShapeDtypeStruct((t, ff), BF16),
                   jax.ShapeDtypeStruct((8, d), F32)),
        in_specs=[tile, tile, whole((8, d)), whole((8, d)), ANY_SPEC, ANY_SPEC],
        out_specs=[tile, pl.BlockSpec((tm, 2 * ff), lambda i: (i, 0)), tile,
                   pl.BlockSpec((tm, ff), lambda i: (i, 0)), whole((8, d))],
        scratch_shapes=[pltpu.VMEM(w_fi_t.shape, BF16), pltpu.VMEM(w_fo.shape, BF16)],
        compiler_params=_params(("arbitrary",)),
    )(x1, tgt, mod, prm, w_fi_t, w_fo)


def _backward_ffn(dx2, x1, ab, mod, prm, w_fi_t, w_fo, tm):
    t, d = x1.shape
    ff = w_fo.shape[0]

    def body(dx2_ref, x1_ref, ab_ref, mod_ref, prm_ref, wfi_hbm, wfo_hbm,
             dx1_ref, df_ref, dab_ref, sums_ref, wfi_ref, wfo_ref):
        i = pl.program_id(0)

        @pl.when(i == 0)
        def _():
            pltpu.sync_copy(wfi_hbm, wfi_ref)
            pltpu.sync_copy(wfo_hbm, wfo_ref)
            sums_ref[...] = jnp.zeros((8, d), F32)

        dx2v = dx2_ref[...]
        df = (mod_ref[M_G2:M_G2 + 1, :] * dx2v).astype(BF16)
        df_ref[...] = df
        dh2 = jnp.zeros((tm, d), F32)
        for c0, c1 in _ffn_chunks(ff):
            dact = _dot_nt(df, wfo_ref[c0:c1, :])
            a = ab_ref[:, c0:c1].astype(F32)
            b = ab_ref[:, ff + c0:ff + c1].astype(F32)
            s = _sigmoid(a)
            sil = a * s
            da =(dact * b * (s * (1.0 + a * (1.0 - s)))).astype(BF16)
            db = (dact * sil).astype(BF16)
            dab_ref[:, c0:c1] = da
            dab_ref[:, ff + c0:ff + c1] = db
            dh2 = dh2 + _dot(da, wfi_ref[c0:c1, :]) + _dot(db, wfi_ref[ff + c0:ff + c1, :])
        x1v = x1_ref[...]
        r2 = lax.rsqrt(_rowmean(x1v * x1v) + EPS)
        xn2 = x1v * r2
        gffn = prm_ref[P_GFFN:P_GFFN + 1, :]
        scale = 1.0 + mod_ref[M_SC2:M_SC2 + 1, :]
        dxn2 = dh2 * gffn * scale
        dx1_ref[...] = dx2v + r2 * (dxn2 - xn2 * _rowmean(dxn2 * xn2))
        hx = dh2 * xn2
        sums_ref[0:1, :] = sums_ref[0:1, :] + _colsum(dh2)
        sums_ref[1:2, :] = sums_ref[1:2, :] + _colsum(hx) * gffn
        sums_ref[2:3, :] = sums_ref[2:3, :] + _colsum(hx) * scale

    tile = pl.BlockSpec((tm, d), lambda i: (i, 0))
    whole = lambda shape: pl.BlockSpec(shape, lambda i: (0,) * len(shape))
    wide = lambda n: pl.BlockSpec((tm, n), lambda i: (i, 0))
    return pl.pallas_call(
        body, name="backward_ffn", grid=(t // tm,),
        out_shape=(jax.ShapeDtypeStruct((t, d), F32), jax.ShapeDtypeStruct((t, d), BF16),
                   jax.ShapeDtypeStruct((t, 2 * ff), BF16), jax.ShapeDtypeStruct((8, d), F32)),
        in_specs=[tile, tile, wide(2 * ff), whole((8, d)), whole((8, d)), ANY_SPEC, ANY_SPEC],
        out_specs=[tile, tile, wide(2 * ff), whole((8, d))],
        scratch_shapes=[pltpu.VMEM(w_fi_t.shape, BF16), pltpu.VMEM(w_fo.shape, BF16)],
        compiler_params=_params(("arbitrary",)),
    )(dx2, x1, ab, mod, prm, w_fi_t, w_fo)


def _backward_mix(dx1, proj, y_a, y_b, mix, uhat, rstd, conv3, sg, ga, gb, mod, prm, taps, w_so, w_co, w_o, tm,
                  partials):
    t, d = dx1.shape
    nt = t // tm
    na = len(partials)
    hs, hc = HALO_SHORT, HALO_CONF
    r3, r31 = 0, 8

    def body(*refs):
        (dx1_ref, proj_ref, halo_ref, ya_ref, yb_ref, mix_ref, uhat_ref, rstd_ref, c3_ref, sg_ref, ga_ref, gb_ref,
         mod_ref, prm_ref, taps_ref, wso_ref, wco_ref, wo_ref) = refs[:18]
        dproj_ref, rhs_ref, sums_ref, dw3_ref, dw31_ref = refs[18 + na:23 + na]
        cv_ext, u0_ext, d3_ext, du1_ext, tmp_ref = refs[23 + 2 * na:28 + 2 * na]
        scatter_start, scatter_finish = _scatter_plan(refs[18:18 + na], refs[23 + na:23 + 2 * na], *refs[28 + 2 * na:])
        i = pl.program_id(0)
        first_tile = i == nt - 1

        @pl.when(i == 0)
        def _():
            scatter_start()
            sums_ref[...] = jnp.zeros((8, d), F32)
            dw3_ref[...] = jnp.zeros(dw3_ref.shape, F32)
            dw31_ref[...] = jnp.zeros(dw31_ref.shape, F32)
            d3_ext[tm:tm + hs, :] = jnp.zeros((hs, d), F32)
            du1_ext[tm:tm + hc, :] = jnp.zeros((hc, d), F32)

        def col(g):
            return proj_ref[:, g * d:(g + 1) * d].astype(F32)

        def hcol(g, rows):
            v = halo_ref[HALO_CONF - rows:HALO_CONF, g * d:(g + 1) * d].astype(F32)
            return jnp.where(first_tile, 0.0, v)

        dx1v = dx1_ref[...]
        mixv = mix_ref[...].astype(F32)
        dmix = (mod_ref[M_G1:M_G1 + 1, :] * dx1v).astype(BF16)
        rhs_ref[2] = dmix
        sums_ref[0:1, :] = sums_ref[0:1, :] + _colsum(dx1v * mixv)
        dmerged = _dot_nt(dmix, wo_ref[...])
        ga = ga_ref[...].astype(F32)
        gb = gb_ref[...].astype(F32)
        yav = ya_ref[...].astype(F32)
        ybv = yb_ref[...].astype(F32)
        dya_f = dmerged * ga
        dyb_f = dmerged * gb
        dya = dya_f.astype(BF16)
        dyb = dyb_f.astype(BF16)
        rhs_ref[0] = dya
        rhs_ref[1] = dyb
        dproj_ref[:, 5 * d:6 * d] = (dya_f * yav * (1.0 - ga)).astype(BF16)
        dproj_ref[:, 6 * d:7 * d] = (dyb_f * ybv * (1.0 - gb)).astype(BF16)

        dya_pre = _dot_nt(dya, wso_ref[...])
        c_s, v_s, b_s = col(1), col(2), col(0)
        cv_ext[0:hs, :] = hcol(1, hs) * hcol(2, hs)
        cv_ext[hs:hs + tm, :] = c_s * v_s
        dproj_ref[:, 0:d] = (dya_pre * c3_ref[...].astype(F32)).astype(BF16)
        d3_ext[0:tm, :] = dya_pre * b_s
        _tap_wgrad(d3_ext, cv_ext, _causal_taps(SHORT_K, hs), tm, hs, dw3_ref)
        _tap_conv(d3_ext, taps_ref, [(r3 + wi, off) for wi, off in _anticausal_taps(SHORT_K)], tm, hs, tmp_ref)
        dcv = tmp_ref[...]
        dproj_ref[:, d:2 * d] = (dcv * v_s).astype(BF16)
        dproj_ref[:, 2 * d:3 * d] = (dcv * c_s).astype(BF16)
        d3_ext[tm:tm + hs, :] = d3_ext[0:hs, :]

        du3 = _dot_nt(dyb, wco_ref[...])
        uhat = uhat_ref[...].astype(F32)
        rstd = rstd_ref[...]
        lng = prm_ref[P_LNG:P_LNG + 1, :]
        u2 = uhat * lng + prm_ref[P_LNB:P_LNB + 1, :]
        s2 = _sigmoid(u2)
        du2 = du3 * (s2 * (1.0 + u2 * (1.0 - s2)))
        sums_ref[1:2, :] = sums_ref[1:2, :] + _colsum(du2 * uhat)
        sums_ref[2:3, :] = sums_ref[2:3, :] + _colsum(du2)
        duhat = du2 * lng
        du1 = rstd * (duhat - _rowmean(duhat) - uhat * _rowmean(duhat * uhat))
        sums_ref[3:4, :] = sums_ref[3:4, :] + _colsum(du1)
        du1_ext[0:tm, :] = du1
        v_c = col(3)
        sg = sg_ref[...].astype(F32)
        u0_ext[0:hc, :] = hcol(3, hc) * _sigmoid(hcol(4, hc))
        u0_ext[hc:hc + tm, :] = v_c * sg
        _tap_wgrad(du1_ext, u0_ext, _causal_taps(CONF_K, hc), tm, hc, dw31_ref)
        _tap_conv(du1_ext, taps_ref, [(r31 + wi, off) for wi, off in _anticausal_taps(CONF_K)], tm, hc, tmp_ref)
        dv_c = tmp_ref[...] * sg
        dproj_ref[:, 3 * d:4 * d] = dv_c.astype(BF16)
        dproj_ref[:, 4 * d:5 * d] = (dv_c * v_c * (1.0 - sg)).astype(BF16)
        du1_ext[tm:tm + hc, :] = du1_ext[0:hc, :]

        @pl.when(i == nt - 1)
        def _():
            scatter_finish()

    rev = lambda i: (nt - 1 - i, 0)
    tile = pl.BlockSpec((tm, d), rev)
    whole = lambda shape: pl.BlockSpec(shape, lambda i: (0,) * len(shape))
    hblocks = tm // HALO_CONF
    halo = pl.BlockSpec((HALO_CONF, 7 * d), lambda i: (jnp.maximum((nt - 1 - i) * hblocks - 1, 0), 0))
    res = pl.pallas_call(
        body, name="backward_mix", grid=(nt,),
        out_shape=(jax.ShapeDtypeStruct((t, 7 * d), BF16), jax.ShapeDtypeStruct((3, t, d), BF16),
                   jax.ShapeDtypeStruct((8, d), F32),
                   jax.ShapeDtypeStruct((SUBLANES * SHORT_K, d), F32),
                   jax.ShapeDtypeStruct((SUBLANES * CONF_K, d), F32))
        + tuple(jax.ShapeDtypeStruct(p.shape, p.dtype) for p in partials),
        in_specs=[tile, pl.BlockSpec((tm, 4 * d), rev), halo, tile, tile, tile, tile, pl.BlockSpec((tm, 1), rev),
                  tile, tile, tile, tile,
                  whole((8, d)), whole((8, d)), whole(taps.shape), whole((d, d)), whole((d, d)), whole((d, d))]
        + [ANY_SPEC] * na,
        out_specs=[pl.BlockSpec((tm, 7 * d), rev), pl.BlockSpec((3, tm, d), lambda i: (0, nt - 1 - i, 0)),
                   whole((8, d)), whole((SUBLANES * SHORT_K, d)), whole((SUBLANES * CONF_K, d))] + [ANY_SPEC] * na,
        scratch_shapes=[pltpu.VMEM((hs + tm, d), F32), pltpu.VMEM((hc + tm, d), F32),
                        pltpu.VMEM((tm + hs, d), F32), pltpu.VMEM((tm + hc, d), F32),
                        pltpu.VMEM((tm, d), F32)] + _comm_scratch(na),
        compiler_params=_params(("arbitrary",)),
    )(dx1, proj, proj, y_a, y_b, mix, uhat, rstd, conv3, sg, ga, gb, mod, prm, taps, w_so, w_co, w_o, *partials)
    return res[:5], res[5:]


def _backward_in(dproj, x, dx1, mod, prm, w_in_g, tm, partials, partial_view, exchanged_shapes):
    t, d = x.shape
    n_all = w_in_g.shape[1]
    na = len(partials)
    nt = t // tm

    def body(*refs):
        dproj_ref, x_ref, dx1_ref, mod_ref, prm_ref, w_hbm = refs[:6]
        gx_ref, sums_ref = refs[6 + na:8 + na]
        w_ref = refs[8 + 2 * na]
        scatter_start, scatter_finish = _scatter_plan(refs[6:6 + na], refs[8 + na:8 + 2 * na], *refs[9 + 2 * na:],
                                                      view=partial_view)

        @pl.when(pl.program_id(0) == 0)
        def _():
            scatter_start()
            pltpu.sync_copy(w_hbm, w_ref)
            sums_ref[...] = jnp.zeros((8, d), F32)

        dh = _dot_nt(dproj_ref[...], w_ref[...])
        xv = x_ref[...]
        r1 = lax.rsqrt(_rowmean(xv * xv) + EPS)
        xn = xv * r1
        gmix = prm_ref[P_GMIX:P_GMIX + 1, :]
        scale = 1.0 + mod_ref[M_SC1:M_SC1 + 1, :]
        dxn = dh * gmix * scale
        gx_ref[...] = dx1_ref[...] + r1 * (dxn - xn * _rowmean(dxn * xn))
        hx = dh * xn
        sums_ref[0:1, :] = sums_ref[0:1, :] + _colsum(dh)
        sums_ref[1:2, :] = sums_ref[1:2, :] + _colsum(hx) * gmix
        sums_ref[2:3, :] = sums_ref[2:3, :] + _colsum(hx) * scale

        @pl.when(pl.program_id(0) == nt - 1)
        def _():
            scatter_finish()

    tile = pl.BlockSpec((tm, d), lambda i: (i, 0))
    whole = pl.BlockSpec((8, d), lambda i: (0, 0))
    res = pl.pallas_call(
        body, name="backward_in", grid=(nt,),
        out_shape=(jax.ShapeDtypeStruct((t, d), F32), jax.ShapeDtypeStruct((8, d), F32))
        + tuple(jax.ShapeDtypeStruct(s, p.dtype) for p, s in zip(partials, exchanged_shapes)),
        in_specs=[pl.BlockSpec((tm, n_all), lambda i: (i, 0)), tile, tile, whole, whole, ANY_SPEC] + [ANY_SPEC] * na,
        out_specs=[tile, whole] + [ANY_SPEC] * na,
        scratch_shapes=[pltpu.VMEM(w_in_g.shape, BF16)] + _comm_scratch(na),
        compiler_params=_params(("arbitrary",)),
    )(dproj, x, dx1, mod, prm, w_in_g, *partials)
    return res[0], res[1], res[2:]


def _weight_grad(a, b, a_spec, b_spec, ns, m, n, nk, name, partials=(), by_columns=False,
                 partial_view=_leading_block, exchanged_shapes=None):
    na = len(partials)
    exchanged_shapes = exchanged_shapes or [p.shape for p in partials]

    def body(*refs):
        a_ref, b_ref = refs[:2]
        o_ref = refs[2 + na]
        acc_ref = refs[3 + 2 * na]
        s, k = pl.program_id(0), pl.program_id(1)
        if na:
            scatter_start, scatter_finish = _scatter_plan(refs[2:2 + na], refs[3 + na:3 + 2 * na], *refs[4 + 2 * na:],
                                                          view=partial_view)

            @pl.when((s == 0) & (k == 0))
            def _():
                scatter_start()

        av = a_ref[0] if len(a_ref.shape) == 3 else a_ref[...]
        bv = b_ref[0] if len(b_ref.shape) == 3 else b_ref[...]
        part = _dot_tn(av, bv)

        @pl.when(k == 0)
        def _():
            acc_ref[...] = part

        @pl.when(k > 0)
        def _():
            acc_ref[...] = acc_ref[...] + part

        @pl.when(k == nk - 1)
        def _():
            if by_columns:
                o_ref[...] = acc_ref[...].astype(BF16)
            else:
                o_ref[0] = acc_ref[...].astype(BF16)

        if na:
            @pl.when((s == ns - 1) & (k == nk - 1))
            def _():
                scatter_finish()

    if by_columns:
        out_shape, out_spec = (m, ns * n), pl.BlockSpec((m, n), lambda s, k: (0, s))
    else:
        out_shape, out_spec = (ns, m, n), pl.BlockSpec((1, m, n), lambda s, k: (s, 0, 0))
    res = pl.pallas_call(
        body, name=name, grid=(ns, nk),
        out_shape=(jax.ShapeDtypeStruct(out_shape, BF16),)
        + tuple(jax.ShapeDtypeStruct(sh, p.dtype) for p, sh in zip(partials, exchanged_shapes)),
        in_specs=[a_spec, b_spec] + [ANY_SPEC] * na,
        out_specs=[out_spec] + [ANY_SPEC] * na,
        scratch_shapes=[pltpu.VMEM((m, n), F32)] + (_comm_scratch(na) if na else []),
        compiler_params=_params(("arbitrary", "arbitrary")),
    )(a, b, *partials)
    return (res[0], res[1:]) if na else res[0]


def _adamw(w, g, m, v):
    m = ADAM_B1 * m + (1.0 - ADAM_B1) * g
    v = ADAM_B2 * v + (1.0 - ADAM_B2) * (g * g)
    m_hat = m / (1.0 - ADAM_B1 ** ADAM_STEP)
    v_hat = v / (1.0 - ADAM_B2 ** ADAM_STEP)
    delta = -ADAM_LR * (m_hat / (jnp.sqrt(v_hat) + ADAM_EPS) + ADAM_WD * w)
    return delta, m, v


def _adamw_shard(parts, w, m, v, tr, name):
    r, c = w.shape

    def body(p_ref, w_ref, m_ref, v_ref, g_ref, d_ref, nm_ref, nv_ref):
        g = p_ref[0].astype(F32)
        for s in range(1, N_DEV):
            g = g + p_ref[s].astype(F32)
        delta, nm, nv = _adamw(w_ref[...], g, m_ref[...], v_ref[...])
        g_ref[...] = g
        d_ref[...] = delta
        nm_ref[...] = nm
        nv_ref[...] = nv

    tile = pl.BlockSpec((tr, c), lambda i: (i, 0))
    return pl.pallas_call(
        body, name=name, grid=(r // tr,),
        out_shape=(jax.ShapeDtypeStruct((r, c), F32),) * 4,
        in_specs=[pl.BlockSpec((N_DEV, tr, c), lambda i: (0, i, 0)), tile, tile, tile],
        out_specs=[tile] * 4,
        compiler_params=_params(("arbitrary",)),
    )(parts, w, m, v)


def _adamw_members(parts, wmv, name):
    nm = len(wmv)
    r, c = wmv[0][0].shape

    def body(*refs):
        p_ref = refs[0]
        for j in range(nm):
            w_ref, m_ref, v_ref = refs[1 + 3 * j:4 + 3 * j]
            g = p_ref[0, j].astype(F32)
            for s in range(1, N_DEV):
                g = g + p_ref[s, j].astype(F32)
            delta, new_m, new_v = _adamw(w_ref[...], g, m_ref[...], v_ref[...])
            for o_ref, val in zip(refs[1 + 3 * nm + 4 * j:5 + 3 * nm + 4 * j], (g, delta, new_m, new_v)):
                o_ref[...] = val

    vm = pl.BlockSpec(memory_space=pltpu.VMEM)
    res = pl.pallas_call(
        body, name=name,
        out_shape=(jax.ShapeDtypeStruct((r, c), F32),) * (4 * nm),
        in_specs=[vm] * (1 + 3 * nm), out_specs=[vm] * (4 * nm),
        compiler_params=_params(),
    )(parts, *[a for triple in wmv for a in triple])
    return [tuple(res[4 * j:4 * j + 4]) for j in range(nm)]


def _ada_update(sc_all, dmod_cols, w, m, v, tr):
    d, n = w.shape

    def body(sc_ref, dm_ref, w_ref, m_ref, v_ref, g_ref, d_ref, nm_ref, nv_ref):
        g = lax.dot_general(sc_ref[...], dm_ref[...], (((0,), (0,)), ((), ())),
                            preferred_element_type=F32, precision=lax.Precision.HIGHEST)
        delta, nm, nv = _adamw(w_ref[...], g, m_ref[...], v_ref[...])
        g_ref[...] = g
        d_ref[...] = delta
        nm_ref[...] = nm
        nv_ref[...] = nv

    tile = pl.BlockSpec((tr, n), lambda i: (i, 0))
    return pl.pallas_call(
        body, name="ada_update", grid=(d // tr,),
        out_shape=(jax.ShapeDtypeStruct((d, n), F32),) * 4,
        in_specs=[pl.BlockSpec((N_DEV, tr), lambda i: (0, i)), pl.BlockSpec((N_DEV, n), lambda i: (0, 0)),
                  tile, tile, tile],
        out_specs=[tile] * 4,
        compiler_params=_params(("arbitrary",)),
    )(sc_all, dmod_cols, w, m, v)


def _small_exchange(vec, cg):
    l = vec.shape[2]
    rows = cg.shape[1]

    def body(vec_ref, cg_ref, vall_ref, cgr_ref, send_sems, recv_sems):
        x, y, c = _my_coords()
        me = _slot(x, y, c)
        vall_ref[me] = vec_ref[0]
        cgr_ref[me] = cg_ref[me]
        copies = []
        for k in range(1, N_DEV):
            peer = (_flip(x, k & 4), _flip(y, k & 2), _flip(c, k & 1))
            copies.append(pltpu.make_async_remote_copy(
                src_ref=vall_ref.at[me], dst_ref=vall_ref.at[me], send_sem=send_sems.at[k - 1],
                recv_sem=recv_sems.at[k - 1], device_id=peer, device_id_type=MESH))
            copies.append(pltpu.make_async_remote_copy(
                src_ref=cg_ref.at[_slot(*peer)], dst_ref=cgr_ref.at[me], send_sem=send_sems.at[7 + k - 1],
                recv_sem=recv_sems.at[7 + k - 1], device_id=peer, device_id_type=MESH))
        for cp in copies:
            cp.start()
        for cp in copies:
            cp.wait_recv()
        for cp in copies:
            cp.wait_send()

    vm = pl.BlockSpec(memory_space=pltpu.VMEM)
    return pl.pallas_call(
        body, name="small_exchange",
        out_shape=(jax.ShapeDtypeStruct((N_DEV, 1, l), F32), jax.ShapeDtypeStruct((N_DEV, rows, LANES), F32)),
        in_specs=[vm, vm], out_specs=[vm, vm],
        scratch_shapes=[pltpu.SemaphoreType.DMA((14,)), pltpu.SemaphoreType.DMA((14,))],
        compiler_params=_params(),
    )(vec, cg)


def _small_update(vall, cgr, smalls):
    ns = len(smalls)

    def body(*refs):
        vall_ref, cgr_ref = refs[0], refs[1]
        wmv = refs[2:2 + 3 * ns]
        outs = refs[2 + 3 * ns:]
        for p, (_, _, _, lo, hi, kind) in enumerate(smalls):
            w_ref, m_ref, v_ref = wmv[3 * p:3 * p + 3]
            part = (lambda s: vall_ref[s, :, lo:hi]) if kind == "vec" else (lambda s: cgr_ref[s, lo:hi, :])
            full = (lambda ref: ref[...]) if kind == "vec" else (lambda ref: ref[0])
            g = part(0)
            for s in range(1, N_DEV):
                g = g + part(s)
            delta, nm, nv = _adamw(full(w_ref), g, full(m_ref), full(v_ref))
            for o_ref, val in zip(outs[4 * p:4 * p + 4], (g, delta, nm, nv)):
                if kind == "vec":
                    o_ref[...] = val
                else:
                    o_ref[0] = val

    vm = pl.BlockSpec(memory_space=pltpu.VMEM)
    args = [vall, cgr]
    out_shape = []
    for w, m, v, _, _, _ in smalls:
        args += [w, m, v]
        out_shape += [jax.ShapeDtypeStruct(w.shape, F32)] * 4
    res = pl.pallas_call(
        body, name="small_update",
        out_shape=tuple(out_shape),
        in_specs=[vm] * len(args), out_specs=[vm] * len(out_shape),
        compiler_params=_params(),
    )(*args)
    return [res[4 * p:4 + 4 * p] for p in range(ns)]


def _pick(t, want):
    return want if t % want == 0 else t


def kernel(x, c, w_ada, b_ada, norm_mix_g, w_in, conv_short_w, w_short_out, conv_conf_w, conv_conf_b, conf_ln_g, conf_ln_b, w_conf_out, w_o, norm_ffn_g, w_ffn_in, w_ffn_out, final_norm_g, loss_target, m_w_ada, m_b_ada, m_norm_mix_g, m_w_in, m_conv_short_w, m_w_short_out, m_conv_conf_w, m_conv_conf_b, m_conf_ln_g, m_conf_ln_b, m_w_conf_out, m_w_o, m_norm_ffn_g, m_w_ffn_in, m_w_ffn_out, m_final_norm_g, v_w_ada, v_b_ada, v_norm_mix_g, v_w_in, v_conv_short_w, v_w_short_out, v_conv_conf_w, v_conv_conf_b, v_conf_ln_g, v_conf_ln_b, v_w_conf_out, v_w_o, v_norm_ffn_g, v_w_ffn_in, v_w_ffn_out, v_final_norm_g):
    t, d = x.shape[1], x.shape[2]
    x2 = x.reshape(t, d)
    tgt = loss_target.reshape(t, d)
    me = _slot(*_my_coords())
    tm = _pick(t, 256)
    tm_fwd_in = _pick(t, 512)
    tk = _pick(t, 2048)

    taps_loc = jnp.zeros((40, LANES), F32)
    taps_loc = taps_loc.at[0:SHORT_K].set(conv_short_w[0]).at[8:8 + CONF_K].set(conv_conf_w[0])
    prm = jnp.concatenate([norm_mix_g, norm_ffn_g, final_norm_g.reshape(1, d), conv_conf_b, conf_ln_g, conf_ln_b,
                           jnp.zeros((2, d), F32)], axis=0)

    n_in = w_in.shape[2]
    nk = t // tk
    tok = pl.BlockSpec((tk, d), lambda s, k: (k, 0))
    rows = d // N_DEV
    frows = w_ffn_out.shape[1]

    proj, h, w_in_g, mod, sc_all3, taps, (w_so_g, w_co_g, w_o_g) = _forward_in(
        jnp.reshape(me, (1,)).astype(jnp.int32), x2, prm, w_in[0].astype(BF16),
        (c, w_ada[0], b_ada, taps_loc), tm_fwd_in,
        [w_short_out[0].astype(BF16), w_conf_out[0].astype(BF16), w_o[0].astype(BF16)])
    w_so = w_so_g.reshape(d, d)
    w_co = w_co_g.reshape(d, d)
    w_oo = w_o_g.reshape(d, d)
    w_fi_t, m_fi_t, v_fi_t = w_ffn_in[0].T, m_w_ffn_in[0].T, v_w_ffn_in[0].T
    (x1, y_a, y_b, mix, uhat, conv3, sg, ga, gb, mixer_lhs, rstd), (w_fi_g, w_fo_g) = _forward_mix(
        x2, proj, mod, prm, taps, w_so, w_co, w_oo, tm, [w_fi_t.astype(BF16), w_ffn_out[0].astype(BF16)])
    fb = w_fi_g.shape[1]
    ff = N_DEV * frows
    w_fi_all = w_fi_g.reshape(N_DEV * fb, d)
    w_fo_all = w_fo_g.reshape(ff, d)
    dx2, ab, h2, act, sums_f = _forward_ffn(x1, tgt, mod, prm, w_fi_all, w_fo_all, tm)

    dx1, df, dab, sums_b = _backward_ffn(dx2, x1, ab, mod, prm, w_fi_all, w_fo_all, tm)
    fb2 = 2 * fb
    pair_tok = pl.BlockSpec((tk, fb2), lambda s, k: (k, s))
    g_fi = _weight_grad(dab, h2, pair_tok, tok, N_DEV // 2, fb2, d, nk, "grad_w_ffn_in")
    g_fo = _weight_grad(act, df, pair_tok, tok, N_DEV // 4, fb2, d, nk, "grad_w_ffn_out")
    (dproj, mixer_rhs, sums_m, dw3p, dw31p), (p_fi, p_fo) = _backward_mix(
        dx1, proj, y_a, y_b, mix, uhat, rstd, conv3, sg, ga, gb, mod, prm, taps, w_so, w_co, w_oo, tm,
        [g_fi.reshape(N_DEV, fb, d), g_fo.reshape(N_DEV, frows, d)])
    tk_in = _pick(t, 4096)
    nk_in = t // tk_in
    tok_in = pl.BlockSpec((tk_in, d), lambda s, k: (k, 0))
    member_tok = pl.BlockSpec((1, tk, d), lambda s, k: (s, k, 0))
    g_mixers = _weight_grad(mixer_lhs, mixer_rhs, member_tok, member_tok, 3, d, d, nk, "grad_w_mixers")
    n_blk = _pick(N_DEV * n_in, d)
    g_in, (p_mixers,) = _weight_grad(
        h, dproj, tok_in, pl.BlockSpec((tk_in, n_blk), lambda s, k: (k, s)),
        N_DEV * n_in // n_blk, d, n_blk, nk_in, "grad_w_in",
        [g_mixers.reshape(3, N_DEV, rows, d)], by_columns=True,
        partial_view=lambda ref, dev: ref.at[:, dev], exchanged_shapes=[(N_DEV, 3, rows, d)])
    grad_x, sums_i, (p_in,) = _backward_in(dproj, x2, dx1, mod, prm, w_in_g, tm, [g_in], _column_block(n_in),
                                           [(N_DEV, d, n_in)])

    up_in = _adamw_shard(p_in, w_in[0], m_w_in[0], v_w_in[0], _pick(d, 256), "adamw_w_in")
    up_so, up_co, up_oo = _adamw_members(
        p_mixers, [(w_short_out[0], m_w_short_out[0], v_w_short_out[0]),
                   (w_conf_out[0], m_w_conf_out[0], v_w_conf_out[0]), (w_o[0], m_w_o[0], v_w_o[0])], "adamw_w_mixers")
    up_fi = tuple(a.T for a in _adamw_shard(p_fi, w_fi_t, m_fi_t, v_fi_t, fb // 2, "adamw_w_ffn_in"))
    up_fo = _adamw_shard(p_fo, w_ffn_out[0], m_w_ffn_out[0], v_w_ffn_out[0], frows, "adamw_w_ffn_out")

    vec = jnp.concatenate([sums_i[0:2], sums_m[0:1], sums_b[0:2], sums_f[1:2],
                           sums_i[2:3], sums_m[3:4], sums_m[1:3], sums_b[2:3], sums_f[0:1],
                           sums_f[2:3]], axis=0)
    vec = vec.reshape(1, 1, 13 * d)
    dw3 = dw3p.reshape(SHORT_K, SUBLANES, d).sum(axis=1)
    dw31 = dw31p.reshape(CONF_K, SUBLANES, d).sum(axis=1)
    cg = jnp.zeros((40, d), F32).at[0:SHORT_K].set(dw3).at[8:8 + CONF_K].set(dw31)
    cg = cg.reshape(40, N_DEV, LANES).transpose(1, 0, 2)
    fin = lambda a: a.reshape(1, d)
    smalls = [
        (b_ada, m_b_ada, v_b_ada, 0, 6 * d, "vec"),
        (norm_mix_g, m_norm_mix_g, v_norm_mix_g, 6 * d, 7 * d, "vec"),
        (conv_short_w, m_conv_short_w, v_conv_short_w, 0, SHORT_K, "cg"),
        (conv_conf_w, m_conv_conf_w, v_conv_conf_w, 8, 8 + CONF_K, "cg"),
        (conv_conf_b, m_conv_conf_b, v_conv_conf_b, 7 * d, 8 * d, "vec"),
        (conf_ln_g, m_conf_ln_g, v_conf_ln_g, 8 * d, 9 * d, "vec"),
        (conf_ln_b, m_conf_ln_b, v_conf_ln_b, 9 * d, 10 * d, "vec"),
        (norm_ffn_g, m_norm_ffn_g, v_norm_ffn_g, 10 * d, 11 * d, "vec"),
        (fin(final_norm_g), fin(m_final_norm_g), fin(v_final_norm_g), 11 * d, 12 * d, "vec"),
    ]
    vall, cgr = _small_exchange(vec, cg)
    up_small = _small_update(vall, cgr, smalls)
    n_ada = w_ada.shape[2]
    dmod_all = vall.reshape(N_DEV, 13 * d)[:, 0:6 * d]
    dmod_cols = lax.dynamic_slice(dmod_all, (0, me * n_ada), (N_DEV, n_ada))
    up_ada = _ada_update(sc_all3.reshape(N_DEV, d), dmod_cols, w_ada[0], m_w_ada[0], v_w_ada[0], _pick(d, 256))

    loss = jnp.sum(vall.reshape(N_DEV, 13 * d)[:, 12 * d:])

    lead = lambda a: a.reshape((1,) + a.shape)
    ups = [tuple(lead(a) for a in up_ada), up_small[0], up_small[1], tuple(lead(a) for a in up_in),
           up_small[2], tuple(lead(a) for a in up_so), up_small[3],
           up_small[4], up_small[5], up_small[6],
           tuple(lead(a) for a in up_co), tuple(lead(a) for a in up_oo), up_small[7],
           tuple(lead(a) for a in up_fi), tuple(lead(a) for a in up_fo),
           tuple(a.reshape(d) for a in up_small[8])]
    grads = [u[0] for u in ups]
    deltas = [u[1] for u in ups]
    new_m = [u[2] for u in ups]
    new_v = [u[3] for u in ups]
    return (loss, grad_x.reshape(1, t, d), *grads, *deltas, *new_m, *new_v)
```

```python
import functools

import jax
import jax.numpy as jnp
from jax import lax
from jax.experimental import pallas as pl
from jax.experimental.pallas import tpu as pltpu

F32 = jnp.float32
BF16 = jnp.bfloat16
MESH = pl.DeviceIdType.MESH

N_DEV = 8
EPS = 1e-6
LN_EPS = 1e-5
SHORT_K = 3
CONF_K = 31
ADAM_LR = 0.001
ADAM_B1 = 0.9
ADAM_B2 = 0.999
ADAM_EPS = 1e-08
ADAM_WD = 0.01
ADAM_STEP = 10

LANES = 128
SUBLANES = 8
CONV_ROWS = 64
HALO_SHORT = 8
HALO_CONF = 32
VMEM_LIMIT = 56 * 1024 * 1024

M_SH1, M_SC1, M_G1, M_SH2, M_SC2, M_G2 = range(6)
P_GMIX, P_GFFN, P_GFIN, P_CBIAS, P_LNG, P_LNB = range(6)


def _params(sem=None, **kw):
    return pltpu.CompilerParams(dimension_semantics=sem, vmem_limit_bytes=VMEM_LIMIT, **kw)


def _sigmoid(v):
    return jax.nn.sigmoid(v)


def _dot(a, b):
    return jnp.dot(a, b, preferred_element_type=F32)


def _dot_nt(a, b):
    return lax.dot_general(a, b, (((1,), (1,)), ((), ())), preferred_element_type=F32)


def _dot_tn(a, b):
    return lax.dot_general(a, b, (((0,), (0,)), ((), ())), preferred_element_type=F32)


def _colsum(v):
    return jnp.sum(v, axis=0, keepdims=True)


def _rowmean(v):
    return jnp.mean(v, axis=-1, keepdims=True)


def _my_coords():
    return lax.axis_index("x"), lax.axis_index("y"), lax.axis_index("c")


def _slot(px, py, pc):
    return 4 * px + 2 * py + pc


def _flip(v, bit):
    return 1 - v if bit else v


def _taps_by_residue(taps):
    by_res = {}
    for wi, off in taps:
        by_res.setdefault(off % SUBLANES, []).append((wi, off // SUBLANES))
    return sorted(by_res.items())


def _tap_conv(ext_ref, w_ref, taps, tm, extra, out_ref):
    d = out_ref.shape[1]
    rb = min(CONV_ROWS, tm)
    wrows = rb + extra
    groups = _taps_by_residue(taps)

    def block(i, carry):
        base = pl.multiple_of(i * rb, SUBLANES)
        for lc in range(d // LANES):
            ls = pl.ds(lc * LANES, LANES)
            win = ext_ref[pl.ds(base, wrows), ls]
            acc = None
            for r, lst in groups:
                sh = win if r == 0 else pltpu.roll(win, wrows - r, 0)
                for wi, q in lst:
                    term = w_ref[wi:wi + 1, ls] * sh[SUBLANES * q:SUBLANES * q + rb, :]
                    acc = term if acc is None else acc + term
            out_ref[pl.ds(base, rb), ls] = acc
        return carry

    lax.fori_loop(0, tm // rb, block, 0)


def _tap_wgrad(a_ref, ext_ref, taps, tm, extra, acc_ref):
    d = a_ref.shape[1]
    rb = min(CONV_ROWS, tm)
    wrows = rb + extra
    groups = _taps_by_residue(taps)

    def block(i, carry):
        base = pl.multiple_of(i * rb, SUBLANES)
        for lc in range(d // LANES):
            ls = pl.ds(lc * LANES, LANES)
            a_blk = a_ref[pl.ds(base, rb), ls]
            win = ext_ref[pl.ds(base, wrows), ls]
            for r, lst in groups:
                sh = win if r == 0 else pltpu.roll(win, wrows - r, 0)
                for wi, q in lst:
                    prod = a_blk * sh[SUBLANES * q:SUBLANES * q + rb, :]
                    part = prod[0:SUBLANES, :]
                    for s in range(1, rb // SUBLANES):
                        part = part + prod[SUBLANES * s:SUBLANES * (s + 1), :]
                    rows = pl.ds(SUBLANES * wi, SUBLANES)
                    acc_ref[rows, ls] = acc_ref[rows, ls] + part
        return carry

    lax.fori_loop(0, tm // rb, block, 0)


def _causal_taps(k, halo):
    return [(i, halo - (k - 1) + i) for i in range(k)]


def _anticausal_taps(k):
    return [(i, (k - 1) - i) for i in range(k)]


def _ada_scratch(d, nloc, trows):
    return [pltpu.VMEM((N_DEV, 1, d), F32), pltpu.VMEM((N_DEV, 1, nloc), F32), pltpu.VMEM((N_DEV, 1, nloc), F32),
            pltpu.VMEM((N_DEV, trows, LANES), F32), pltpu.SemaphoreType.DMA((21,)), pltpu.SemaphoreType.DMA((21,))]


def _ada_exchange(c_ref, w_ref, b_ref, t_ref, mod_ref, sc_ref, taps_ref,
                  scall_ref, part_ref, modrecv_ref, tapsall_ref, send_sems, recv_sems):
    x, y, cc = _my_coords()
    me = _slot(x, y, cc)
    d = c_ref.shape[1]
    cv = c_ref[...]
    scall_ref[me] = cv * _sigmoid(cv)
    tapsall_ref[me] = t_ref[...]

    def peer_of(k):
        return (_flip(x, k & 4), _flip(y, k & 2), _flip(cc, k & 1))

    def gather_copy(ref, base, k):
        return pltpu.make_async_remote_copy(
            src_ref=ref.at[me], dst_ref=ref.at[me], send_sem=send_sems.at[base + k - 1],
            recv_sem=recv_sems.at[base + k - 1], device_id=peer_of(k), device_id_type=MESH)

    first = [gather_copy(scall_ref, 0, k) for k in range(1, N_DEV)]
    first += [gather_copy(tapsall_ref, 7, k) for k in range(1, N_DEV)]
    for cp in first:
        cp.start()
    for cp in first[:7]:
        cp.wait_recv()
    sc_all = jnp.concatenate([scall_ref[s] for s in range(N_DEV)], axis=0)
    for s in range(N_DEV):
        sc_ref[s] = scall_ref[s]
    part = jnp.dot(sc_all, w_ref[...], preferred_element_type=F32,
                   precision=lax.Precision.HIGHEST)
    for b in range(N_DEV):
        part_ref[b] = part[b:b + 1, :]
    modrecv_ref[me] = part_ref[me]
    second = []
    for k in range(1, N_DEV):
        px, py, pc = peer_of(k)
        second.append(pltpu.make_async_remote_copy(
            src_ref=part_ref.at[_slot(px, py, pc)], dst_ref=modrecv_ref.at[me],
            send_sem=send_sems.at[14 + k - 1], recv_sem=recv_sems.at[14 + k - 1],
            device_id=(px, py, pc), device_id_type=MESH))
    for cp in second:
        cp.start()
    for cp in second:
        cp.wait_recv()
    mod = jnp.concatenate([modrecv_ref[s] for s in range(N_DEV)], axis=1) + b_ref[...]
    for r in range(6):
        mod_ref[r:r + 1, :] = mod[:, r * d:(r + 1) * d]
    mod_ref[6:8, :] = jnp.zeros((2, d), F32)
    for cp in first[7:]:
        cp.wait_recv()
    taps_ref[...] = jnp.concatenate([tapsall_ref[s] for s in range(N_DEV)], axis=1)
    for cp in first + second:
        cp.wait_send()


ANY_SPEC = pl.BlockSpec(memory_space=pl.ANY)


def _comm_scratch(na):
    return [pltpu.SemaphoreType.DMA((7 * na,)), pltpu.SemaphoreType.DMA((7 * na,)), pltpu.SemaphoreType.DMA((na,))]


def _leading_block(ref, slot):
    return ref.at[slot]


def _column_block(width):
    def view(ref, slot):
        return ref.at[:, pl.ds(pl.multiple_of(slot * width, LANES), width)]
    return view


class _Gather:
    def __init__(self, ins, outs, send_sems, recv_sems, local_sems, view=_leading_block):
        self.na = len(ins)
        x, y, c = _my_coords()
        self.c = c
        self.view = view
        self.me, self.sibling = (x, y, c), (x, y, 1 - c)
        self.chips = [(1 - x, y), (x, 1 - y), (1 - x, 1 - y)]
        self.outs, self.send_sems, self.recv_sems = outs, send_sems, recv_sems
        self.mine = [pltpu.make_async_copy(ins[a], view(outs[a], _slot(*self.me)), local_sems.at[a])
                     for a in range(self.na)]
        self.first = []
        for a in range(self.na):
            self.first.append(self._copy(a, 0, self.me, self.sibling, src=ins[a]))
            self.first += [self._copy(a, 1 + j, self.me, (*chip, c), src=ins[a]) for j, chip in enumerate(self.chips)]
        self.passed = [self._copy(a, 4 + j, (*chip, c), self.sibling)
                       for a in range(self.na) for j, chip in enumerate(self.chips)]

    def _copy(self, a, k, block, to, src=None):
        dst = self.view(self.outs[a], _slot(*block))
        return pltpu.make_async_remote_copy(
            src_ref=dst if src is None else src, dst_ref=dst,
            send_sem=self.send_sems.at[7 * a + k], recv_sem=self.recv_sems.at[7 * a + k],
            device_id=to, device_id_type=MESH)

    def start(self):
        self.start_near()
        self.start_far()

    def start_near(self):
        self.start_local()
        self.start_one_hop()

    def start_local(self):
        for cp in self.mine + self.first[0::4]:
            cp.start()

    def start_one_hop(self):
        for cp in self.first[1::4] + self.first[2::4]:
            cp.start()

    def start_far(self):
        for cp in self.first[3::4]:
            cp.start()

    def forward_chip(self, j):
        for a in range(self.na):
            self._copy(a, 1 + j, (*self.chips[j], self.c), self.me).wait_recv()
            self.passed[3 * a + j].start()

    def wait_sibling_own(self):
        for a in range(self.na):
            self._copy(a, 0, self.sibling, self.me).wait_recv()

    def wait_sibling_passed(self, j):
        for a in range(self.na):
            self._copy(a, 4 + j, (*self.chips[j], 1 - self.c), self.me).wait_recv()

    def finish_sends(self):
        for cp in self.first + self.passed:
            cp.wait_send()
        for cp in self.mine:
            cp.wait()

    def forward(self):
        for j in range(3):
            self.forward_chip(j)

    def finish(self):
        self.wait_sibling_own()
        for j in range(3):
            self.wait_sibling_passed(j)
        self.finish_sends()


def _scatter_plan(ins, outs, send_sems, recv_sems, local_sems, view=_leading_block):
    na = len(ins)
    x, y, c = _my_coords()
    me = _slot(x, y, c)
    mine = [pltpu.make_async_copy(view(ins[a], me), outs[a].at[me], local_sems.at[a]) for a in range(na)]
    copies = []
    for k in range(1, N_DEV):
        peer = (_flip(x, k & 4), _flip(y, k & 2), _flip(c, k & 1))
        for a in range(na):
            copies.append(pltpu.make_async_remote_copy(
                src_ref=view(ins[a], _slot(*peer)), dst_ref=outs[a].at[me],
                send_sem=send_sems.at[7 * a + k - 1], recv_sem=recv_sems.at[7 * a + k - 1],
                device_id=peer, device_id_type=MESH))

    def start():
        for cp in mine + copies:
            cp.start()

    def finish():
        for cp in copies:
            cp.wait_recv()
        for cp in copies:
            cp.wait_send()
        for cp in mine:
            cp.wait()

    return start, finish


def _forward_in(me_arr, x, prm, w_in_loc, ada, tm, shards):
    t, d = x.shape
    n = w_in_loc.shape[1]
    ns = N_DEV // 2
    na = len(shards)
    nt = t // tm
    nloc = ada[1].shape[1]
    trows = ada[3].shape[0]
    chip_of_pass = {1: 1, 2: 0, 3: 2}

    def body(*refs):
        x_ref, prm_ref, wloc_ref = refs[1:4]
        ada_in = refs[4:8]
        shard_refs = refs[8:8 + na]
        proj_ref, h_ref, wg_ref, mod_ref, sc_ref, taps_ref = refs[8 + na:14 + na]
        gathered_refs = refs[14 + na:14 + 2 * na]
        hall_ref, wv_ref, wv_sem = refs[14 + 2 * na:17 + 2 * na]
        ada_scratch = refs[17 + 2 * na:23 + 2 * na]
        sems = refs[23 + 2 * na:]
        s, i = pl.program_id(0), pl.program_id(1)
        columns = _column_block(n)

        def gathers():
            return (_Gather([wloc_ref], [wg_ref], *sems[0:3], view=columns),
                    _Gather(shard_refs, gathered_refs, *sems[3:6]))

        def shard_copy(dev, p, other_core, src=None):
            half = pl.ds(pl.multiple_of((dev & 1) * n, LANES), n)
            return pltpu.make_async_copy(columns(wg_ref, dev) if src is None else src,
                                         wv_ref.at[p % 2, :, half], wv_sem.at[2 * (p % 2) + other_core])

        def pair_copies(p, me):
            return [shard_copy(me ^ (2 * p), p, 0), shard_copy(me ^ (2 * p) ^ 1, p, 1)]

        @pl.when(i == 0)
        def _():
            me = _slot(*_my_coords())

            @pl.when(s == 0)
            def _():
                g_in, _ = gathers()
                g_in.start_local()
                _ada_exchange(*ada_in, mod_ref, sc_ref, taps_ref, *ada_scratch)
                g_in.start_one_hop()
                own = shard_copy(me, 0, 0, src=wloc_ref)
                own.start()
                g_in.wait_sibling_own()
                sib = shard_copy(me ^ 1, 0, 1)
                sib.start()
                own.wait()
                sib.wait()

            for p in range(1, ns):
                @pl.when(s == p)
                def _(p=p):
                    for cp in pair_copies(p, me):
                        cp.wait()

        @pl.when(i == max(nt - 3, 0))
        def _():
            g_in, g_rest = gathers()
            for p in range(1, ns):
                @pl.when(s == p - 1)
                def _(p=p):
                    g_in.forward_chip(chip_of_pass[p])
                    if p == 1:
                        g_in.start_far()
                    if p == 2:
                        g_rest.start()
                    if p == 3:
                        g_rest.forward()

        @pl.when(i == nt - 1)
        def _():
            g_in, _ = gathers()
            me = _slot(*_my_coords())
            for p in range(1, ns):
                @pl.when(s == p - 1)
                def _(p=p):
                    g_in.wait_sibling_passed(chip_of_pass[p])
                    for cp in pair_copies(p, me):
                        cp.start()

        @pl.when(s == 0)
        def _():
            xv = x_ref[...]
            r = lax.rsqrt(_rowmean(xv * xv) + EPS)
            h = xv * r * prm_ref[P_GMIX:P_GMIX + 1, :] * (1.0 + mod_ref[M_SC1:M_SC1 + 1, :]) \
                + mod_ref[M_SH1:M_SH1 + 1, :]
            hb = h.astype(BF16)
            hall_ref[i] = hb
            h_ref[...] = hb

        proj_ref[...] = _dot(hall_ref[i], wv_ref[s % 2]).astype(BF16)

        @pl.when((s == ns - 1) & (i == nt - 1))
        def _():
            g_in, g_rest = gathers()
            g_in.finish_sends()
            g_rest.finish()

    x_tile = pl.BlockSpec((tm, d), lambda s, i, me: (jnp.where(s == 0, i, nt - 1), 0))
    whole = lambda a: pl.BlockSpec(a.shape, lambda s, i, me: (0,) * len(a.shape))
    small_out = (jax.ShapeDtypeStruct((8, d), F32), jax.ShapeDtypeStruct((N_DEV, 1, d), F32),
                 jax.ShapeDtypeStruct((trows, N_DEV * LANES), F32))
    res = pl.pallas_call(
        body, name="forward_in",
        grid_spec=pltpu.PrefetchScalarGridSpec(
            num_scalar_prefetch=1, grid=(ns, nt),
            in_specs=[x_tile, whole(prm), ANY_SPEC] + [whole(a) for a in ada] + [ANY_SPEC] * na,
            out_specs=[pl.BlockSpec((tm, 2 * n), lambda s, i, me: (i, (me[0] // 2) ^ s)), x_tile, ANY_SPEC]
            + [whole(a) for a in small_out] + [ANY_SPEC] * na,
            scratch_shapes=[pltpu.VMEM((nt, tm, d), BF16), pltpu.VMEM((2, d, 2 * n), BF16),
                            pltpu.SemaphoreType.DMA((4,))]
            + _ada_scratch(d, nloc, trows) + _comm_scratch(1) + _comm_scratch(na)),
        out_shape=(jax.ShapeDtypeStruct((t, N_DEV * n), BF16), jax.ShapeDtypeStruct((t, d), BF16),
                   jax.ShapeDtypeStruct((d, N_DEV * n), BF16)) + small_out
        + tuple(jax.ShapeDtypeStruct((N_DEV,) + a.shape, a.dtype) for a in shards),
        compiler_params=_params(("arbitrary", "arbitrary")),
    )(me_arr, x, prm, w_in_loc, *ada, *shards)
    return res[0], res[1], res[2], res[3], res[4], res[5], res[6:]


def _forward_mix(x, proj, mod, prm, taps, w_so, w_co, w_o, tm, shards):
    t, d = x.shape
    nt = t // tm
    na = len(shards)
    hs, hc = HALO_SHORT, HALO_CONF
    r3, r31 = 0, 8

    def body(*refs):
        x_ref, proj_ref, mod_ref, prm_ref, taps_ref, wso_ref, wco_ref, wo_ref = refs[:8]
        (x1_ref, ya_ref, yb_ref, mix_ref, uhat_ref, c3_ref, sg_ref, ga_ref, gb_ref, lhs_ref,
         rstd_ref) = refs[8 + na:19 + na]
        cv_ext, u0_ext, conv3_ref, u1f_ref = refs[19 + 2 * na:23 + 2 * na]
        gather = _Gather(refs[8:8 + na], refs[19 + na:19 + 2 * na], *refs[23 + 2 * na:])
        i = pl.program_id(0)

        @pl.when(i == 0)
        def _():
            gather.start()
            cv_ext[0:hs, :] = jnp.zeros((hs, d), F32)
            u0_ext[0:hc, :] = jnp.zeros((hc, d), F32)

        @pl.when(i == (3 * nt) // 4)
        def _():
            gather.forward()

        def col(g):
            return proj_ref[:, g * d:(g + 1) * d].astype(F32)

        cv_ext[hs:hs + tm, :] = col(1) * col(2)
        sg = _sigmoid(col(4))
        sg_ref[...] = sg.astype(BF16)
        u0_ext[hc:hc + tm, :] = col(3) * sg
        _tap_conv(cv_ext, taps_ref, [(r3 + wi, off) for wi, off in _causal_taps(SHORT_K, hs)], tm, hs, conv3_ref)
        _tap_conv(u0_ext, taps_ref, [(r31 + wi, off) for wi, off in _causal_taps(CONF_K, hc)], tm, hc, u1f_ref)
        cv_ext[0:hs, :] = cv_ext[tm:tm + hs, :]
        u0_ext[0:hc, :] = u0_ext[tm:tm + hc, :]

        c3_ref[...] = conv3_ref[...].astype(BF16)
        ya_pre = (col(0) * c3_ref[...].astype(F32)).astype(BF16)
        lhs_ref[0] = ya_pre
        y_a = _dot(ya_pre, wso_ref[...])
        u1 = u1f_ref[...] + prm_ref[P_CBIAS:P_CBIAS + 1, :]
        mu = _rowmean(u1)
        uc = u1 - mu
        rstd = lax.rsqrt(_rowmean(uc * uc) + LN_EPS)
        rstd_ref[...] = rstd
        uhat_ref[...] = (uc * rstd).astype(BF16)
        u2 = uhat_ref[...].astype(F32) * prm_ref[P_LNG:P_LNG + 1, :] + prm_ref[P_LNB:P_LNB + 1, :]
        u3 = (u2 * _sigmoid(u2)).astype(BF16)
        lhs_ref[1] = u3
        y_b = _dot(u3, wco_ref[...])
        ya_ref[...] = y_a.astype(BF16)
        yb_ref[...] = y_b.astype(BF16)
        ga = _sigmoid(col(5))
        gb = _sigmoid(col(6))
        ga_ref[...] = ga.astype(BF16)
        gb_ref[...] = gb.astype(BF16)
        merged = (ga * ya_ref[...].astype(F32) + gb * yb_ref[...].astype(F32)).astype(BF16)
        lhs_ref[2] = merged
        mix = _dot(merged, wo_ref[...])
        mix_ref[...] = mix.astype(BF16)
        x1_ref[...] = x_ref[...] + mod_ref[M_G1:M_G1 + 1, :] * mix

        @pl.when(i == nt - 1)
        def _():
            gather.finish()

    tile = pl.BlockSpec((tm, d), lambda i: (i, 0))
    whole = lambda shape: pl.BlockSpec(shape, lambda i: (0,) * len(shape))
    res = pl.pallas_call(
        body, name="forward_mix", grid=(nt,),
        out_shape=(jax.ShapeDtypeStruct((t, d), F32),) + (jax.ShapeDtypeStruct((t, d), BF16),) * 8
        + (jax.ShapeDtypeStruct((3, t, d), BF16), jax.ShapeDtypeStruct((t, 1), F32))
        + tuple(jax.ShapeDtypeStruct((N_DEV,) + a.shape, a.dtype) for a in shards),
        in_specs=[tile, pl.BlockSpec((tm, 7 * d), lambda i: (i, 0)), whole((8, d)), whole((8, d)),
                  whole(taps.shape), whole((d, d)), whole((d, d)), whole((d, d))] + [ANY_SPEC] * na,
        out_specs=[tile] * 9 + [pl.BlockSpec((3, tm, d), lambda i: (0, i, 0)), pl.BlockSpec((tm, 1), lambda i: (i, 0))]
        + [ANY_SPEC] * na,
        scratch_shapes=[pltpu.VMEM((hs + tm, d), F32), pltpu.VMEM((hc + tm, d), F32),
                        pltpu.VMEM((tm, d), F32), pltpu.VMEM((tm, d), F32)] + _comm_scratch(na),
        compiler_params=_params(("arbitrary",)),
    )(x, proj, mod, prm, taps, w_so, w_co, w_o, *shards)
    return res[:11], res[11:]


def _ffn_chunks(ff):
    mxu = 2 * LANES
    cut = (ff // mxu + 1) // 2 * mxu
    return [(0, cut), (cut, ff)] if 0 < cut < ff and ff % mxu == 0 else [(0, ff)]


def _forward_ffn(x1, tgt, mod, prm, w_fi_t, w_fo, tm):
    t, d = x1.shape
    ff = w_fo.shape[0]

    def body(x1_ref, tgt_ref, mod_ref, prm_ref, wfi_hbm, wfo_hbm,
             dx2_ref, ab_ref, h2_ref, act_ref, sums_ref, wfi_ref, wfo_ref):
        i = pl.program_id(0)

        @pl.when(i == 0)
        def _():
            pltpu.sync_copy(wfi_hbm, wfi_ref)
            pltpu.sync_copy(wfo_hbm, wfo_ref)
            sums_ref[...] = jnp.zeros((8, d), F32)

        x1v = x1_ref[...]
        r2 = lax.rsqrt(_rowmean(x1v * x1v) + EPS)
        h2 = (x1v * r2 * prm_ref[P_GFFN:P_GFFN + 1, :] * (1.0 + mod_ref[M_SC2:M_SC2 + 1, :])
              + mod_ref[M_SH2:M_SH2 + 1, :]).astype(BF16)
        h2_ref[...] = h2
        f = jnp.zeros((tm, d), F32)
        for c0, c1 in _ffn_chunks(ff):
            ab_ref[:, c0:c1] = _dot_nt(h2, wfi_ref[c0:c1, :]).astype(BF16)
            ab_ref[:, ff + c0:ff + c1] = _dot_nt(h2, wfi_ref[ff + c0:ff + c1, :]).astype(BF16)
            a = ab_ref[:, c0:c1].astype(F32)
            act = (a * _sigmoid(a) * ab_ref[:, ff + c0:ff + c1].astype(F32)).astype(BF16)
            act_ref[:, c0:c1] = act
            f = f + _dot(act, wfo_ref[c0:c1, :])
        x2 = x1v + mod_ref[M_G2:M_G2 + 1, :] * f
        r3 = lax.rsqrt(_rowmean(x2 * x2) + EPS)
        xn3 = x2 * r3
        gfin = prm_ref[P_GFIN:P_GFIN + 1, :]
        err = xn3 * gfin - tgt_ref[...]
        dy = err * (1.0 / d)
        dxn3 = dy * gfin
        dx2 = r3 * (dxn3 - xn3 * _rowmean(dxn3 * xn3))
        dx2_ref[...] = dx2
        sums_ref[0:1, :] = sums_ref[0:1, :] + _colsum(dy * xn3)
        sums_ref[1:2, :] = sums_ref[1:2, :] + _colsum(dx2 * f)
        sums_ref[2:3, :] = sums_ref[2:3, :] + _colsum(err * err) * (0.5 / d)

    tile = pl.BlockSpec((tm, d), lambda i: (i, 0))
    whole = lambda shape: pl.BlockSpec(shape, lambda i: (0,) * len(shape))
    return pl.pallas_call(
        body, name="forward_ffn", grid=(t // tm,),
        out_shape=(jax.ShapeDtypeStruct((t, d), F32), jax.ShapeDtypeStruct((t, 2 * ff), BF16),
                   jax.ShapeDtypeStruct((t, d), BF16), jax.ShapeDtypeStruct((t, ff), BF16),
                   jax.ShapeDtypeStruct((8, d), F32)),
        in_specs=[tile, tile, whole((8, d)), whole((8, d)), ANY_SPEC, ANY_SPEC],
        out_specs=[tile, pl.BlockSpec((tm, 2 * ff), lambda i: (i, 0)), tile,
                   pl.BlockSpec((tm, ff), lambda i: (i, 0)), whole((8, d))],
        scratch_shapes=[pltpu.VMEM(w_fi_t.shape, BF16), pltpu.VMEM(w_fo.shape, BF16)],
        compiler_params=_params(("arbitrary",)),
    )(x1, tgt, mod, prm, w_fi_t, w_fo)


def _backward_ffn(dx2, x1, ab, mod, prm, w_fi_t, w_fo, tm):
    t, d = x1.shape
    ff = w_fo.shape[0]

    def body(dx2_ref, x1_ref, ab_ref, mod_ref, prm_ref, wfi_hbm, wfo_hbm,
             dx1_ref, df_ref, dab_ref, sums_ref, wfi_ref, wfo_ref):
        i = pl.program_id(0)

        @pl.when(i == 0)
        def _():
            pltpu.sync_copy(wfi_hbm, wfi_ref)
            pltpu.sync_copy(wfo_hbm, wfo_ref)
            sums_ref[...] = jnp.zeros((8, d), F32)

        dx2v = dx2_ref[...]
        df = (mod_ref[M_G2:M_G2 + 1, :] * dx2v).astype(BF16)
        df_ref[...] = df
        dh2 = jnp.zeros((tm, d), F32)
        for c0, c1 in _ffn_chunks(ff):
            dact = _dot_nt(df, wfo_ref[c0:c1, :])
            a = ab_ref[:, c0:c1].astype(F32)
            b = ab_ref[:, ff + c0:ff + c1].astype(F32)
            s = _sigmoid(a)
            sil = a * s
            da =(dact * b * (s * (1.0 + a * (1.0 - s)))).astype(BF16)
            db = (dact * sil).astype(BF16)
            dab_ref[:, c0:c1] = da
            dab_ref[:, ff + c0:ff + c1] = db
            dh2 = dh2 + _dot(da, wfi_ref[c0:c1, :]) + _dot(db, wfi_ref[ff + c0:ff + c1, :])
        x1v = x1_ref[...]
        r2 = lax.rsqrt(_rowmean(x1v * x1v) + EPS)
        xn2 = x1v * r2
        gffn = prm_ref[P_GFFN:P_GFFN + 1, :]
        scale = 1.0 + mod_ref[M_SC2:M_SC2 + 1, :]
        dxn2 = dh2 * gffn * scale
        dx1_ref[...] = dx2v + r2 * (dxn2 - xn2 * _rowmean(dxn2 * xn2))
        hx = dh2 * xn2
        sums_ref[0:1, :] = sums_ref[0:1, :] + _colsum(dh2)
        sums_ref[1:2, :] = sums_ref[1:2, :] + _colsum(hx) * gffn
        sums_ref[2:3, :] = sums_ref[2:3, :] + _colsum(hx) * scale

    tile = pl.BlockSpec((tm, d), lambda i: (i, 0))
    whole = lambda shape: pl.BlockSpec(shape, lambda i: (0,) * len(shape))
    wide = lambda n: pl.BlockSpec((tm, n), lambda i: (i, 0))
    return pl.pallas_call(
        body, name="backward_ffn", grid=(t // tm,),
        out_shape=(jax.ShapeDtypeStruct((t, d), F32), jax.ShapeDtypeStruct((t, d), BF16),
                   jax.ShapeDtypeStruct((t, 2 * ff), BF16), jax.ShapeDtypeStruct((8, d), F32)),
        in_specs=[tile, tile, wide(2 * ff), whole((8, d)), whole((8, d)), ANY_SPEC, ANY_SPEC],
        out_specs=[tile, tile, wide(2 * ff), whole((8, d))],
        scratch_shapes=[pltpu.VMEM(w_fi_t.shape, BF16), pltpu.VMEM(w_fo.shape, BF16)],
        compiler_params=_params(("arbitrary",)),
    )(dx2, x1, ab, mod, prm, w_fi_t, w_fo)


def _backward_mix(dx1, proj, y_a, y_b, mix, uhat, rstd, conv3, sg, ga, gb, mod, prm, taps, w_so, w_co, w_o, tm,
                  partials):
    t, d = dx1.shape
    nt = t // tm
    na = len(partials)
    hs, hc = HALO_SHORT, HALO_CONF
    r3, r31 = 0, 8

    def body(*refs):
        (dx1_ref, proj_ref, halo_ref, ya_ref, yb_ref, mix_ref, uhat_ref, rstd_ref, c3_ref, sg_ref, ga_ref, gb_ref,
         mod_ref, prm_ref, taps_ref, wso_ref, wco_ref, wo_ref) = refs[:18]
        dproj_ref, rhs_ref, sums_ref, dw3_ref, dw31_ref = refs[18 + na:23 + na]
        cv_ext, u0_ext, d3_ext, du1_ext, tmp_ref = refs[23 + 2 * na:28 + 2 * na]
        scatter_start, scatter_finish = _scatter_plan(refs[18:18 + na], refs[23 + na:23 + 2 * na], *refs[28 + 2 * na:])
        i = pl.program_id(0)
        first_tile = i == nt - 1

        @pl.when(i == 0)
        def _():
            scatter_start()
            sums_ref[...] = jnp.zeros((8, d), F32)
            dw3_ref[...] = jnp.zeros(dw3_ref.shape, F32)
            dw31_ref[...] = jnp.zeros(dw31_ref.shape, F32)
            d3_ext[tm:tm + hs, :] = jnp.zeros((hs, d), F32)
            du1_ext[tm:tm + hc, :] = jnp.zeros((hc, d), F32)

        def col(g):
            return proj_ref[:, g * d:(g + 1) * d].astype(F32)

        def hcol(g, rows):
            v = halo_ref[HALO_CONF - rows:HALO_CONF, g * d:(g + 1) * d].astype(F32)
            return jnp.where(first_tile, 0.0, v)

        dx1v = dx1_ref[...]
        mixv = mix_ref[...].astype(F32)
        dmix = (mod_ref[M_G1:M_G1 + 1, :] * dx1v).astype(BF16)
        rhs_ref[2] = dmix
        sums_ref[0:1, :] = sums_ref[0:1, :] + _colsum(dx1v * mixv)
        dmerged = _dot_nt(dmix, wo_ref[...])
        ga = ga_ref[...].astype(F32)
        gb = gb_ref[...].astype(F32)
        yav = ya_ref[...].astype(F32)
        ybv = yb_ref[...].astype(F32)
        dya_f = dmerged * ga
        dyb_f = dmerged * gb
        dya = dya_f.astype(BF16)
        dyb = dyb_f.astype(BF16)
        rhs_ref[0] = dya
        rhs_ref[1] = dyb
        dproj_ref[:, 5 * d:6 * d] = (dya_f * yav * (1.0 - ga)).astype(BF16)
        dproj_ref[:, 6 * d:7 * d] = (dyb_f * ybv * (1.0 - gb)).astype(BF16)

        dya_pre = _dot_nt(dya, wso_ref[...])
        c_s, v_s, b_s = col(1), col(2), col(0)
        cv_ext[0:hs, :] = hcol(1, hs) * hcol(2, hs)
        cv_ext[hs:hs + tm, :] = c_s * v_s
        dproj_ref[:, 0:d] = (dya_pre * c3_ref[...].astype(F32)).astype(BF16)
        d3_ext[0:tm, :] = dya_pre * b_s
        _tap_wgrad(d3_ext, cv_ext, _causal_taps(SHORT_K, hs), tm, hs, dw3_ref)
        _tap_conv(d3_ext, taps_ref, [(r3 + wi, off) for wi, off in _anticausal_taps(SHORT_K)], tm, hs, tmp_ref)
        dcv = tmp_ref[...]
        dproj_ref[:, d:2 * d] = (dcv * v_s).astype(BF16)
        dproj_ref[:, 2 * d:3 * d] = (dcv * c_s).astype(BF16)
        d3_ext[tm:tm + hs, :] = d3_ext[0:hs, :]

        du3 = _dot_nt(dyb, wco_ref[...])
        uhat = uhat_ref[...].astype(F32)
        rstd = rstd_ref[...]
        lng = prm_ref[P_LNG:P_LNG + 1, :]
        u2 = uhat * lng + prm_ref[P_LNB:P_LNB + 1, :]
        s2 = _sigmoid(u2)
        du2 = du3 * (s2 * (1.0 + u2 * (1.0 - s2)))
        sums_ref[1:2, :] = sums_ref[1:2, :] + _colsum(du2 * uhat)
        sums_ref[2:3, :] = sums_ref[2:3, :] + _colsum(du2)
        duhat = du2 * lng
        du1 = rstd * (duhat - _rowmean(duhat) - uhat * _rowmean(duhat * uhat))
        sums_ref[3:4, :] = sums_ref[3:4, :] + _colsum(du1)
        du1_ext[0:tm, :] = du1
        v_c = col(3)
        sg = sg_ref[...].astype(F32)
        u0_ext[0:hc, :] = hcol(3, hc) * _sigmoid(hcol(4, hc))
        u0_ext[hc:hc + tm, :] = v_c * sg
        _tap_wgrad(du1_ext, u0_ext, _causal_taps(CONF_K, hc), tm, hc, dw31_ref)
        _tap_conv(du1_ext, taps_ref, [(r31 + wi, off) for wi, off in _anticausal_taps(CONF_K)], tm, hc, tmp_ref)
        dv_c = tmp_ref[...] * sg
        dproj_ref[:, 3 * d:4 * d] = dv_c.astype(BF16)
        dproj_ref[:, 4 * d:5 * d] = (dv_c * v_c * (1.0 - sg)).astype(BF16)
        du1_ext[tm:tm + hc, :] = du1_ext[0:hc, :]

        @pl.when(i == nt - 1)
        def _():
            scatter_finish()

    rev = lambda i: (nt - 1 - i, 0)
    tile = pl.BlockSpec((tm, d), rev)
    whole = lambda shape: pl.BlockSpec(shape, lambda i: (0,) * len(shape))
    hblocks = tm // HALO_CONF
    halo = pl.BlockSpec((HALO_CONF, 7 * d), lambda i: (jnp.maximum((nt - 1 - i) * hblocks - 1, 0), 0))
    res = pl.pallas_call(
        body, name="backward_mix", grid=(nt,),
        out_shape=(jax.ShapeDtypeStruct((t, 7 * d), BF16), jax.ShapeDtypeStruct((3, t, d), BF16),
                   jax.ShapeDtypeStruct((8, d), F32),
                   jax.ShapeDtypeStruct((SUBLANES * SHORT_K, d), F32),
                   jax.ShapeDtypeStruct((SUBLANES * CONF_K, d), F32))
        + tuple(jax.ShapeDtypeStruct(p.shape, p.dtype) for p in partials),
        in_specs=[tile, pl.BlockSpec((tm, 4 * d), rev), halo, tile, tile, tile, tile, pl.BlockSpec((tm, 1), rev),
                  tile, tile, tile, tile,
                  whole((8, d)), whole((8, d)), whole(taps.shape), whole((d, d)), whole((d, d)), whole((d, d))]
        + [ANY_SPEC] * na,
        out_specs=[pl.BlockSpec((tm, 7 * d), rev), pl.BlockSpec((3, tm, d), lambda i: (0, nt - 1 - i, 0)),
                   whole((8, d)), whole((SUBLANES * SHORT_K, d)), whole((SUBLANES * CONF_K, d))] + [ANY_SPEC] * na,
        scratch_shapes=[pltpu.VMEM((hs + tm, d), F32), pltpu.VMEM((hc + tm, d), F32),
                        pltpu.VMEM((tm + hs, d), F32), pltpu.VMEM((tm + hc, d), F32),
                        pltpu.VMEM((tm, d), F32)] + _comm_scratch(na),
        compiler_params=_params(("arbitrary",)),
    )(dx1, proj, proj, y_a, y_b, mix, uhat, rstd, conv3, sg, ga, gb, mod, prm, taps, w_so, w_co, w_o, *partials)
    return res[:5], res[5:]


def _backward_in(dproj, x, dx1, mod, prm, w_in_g, tm):
    t, d = x.shape
    n_all = w_in_g.shape[1]

    def body(dproj_ref, x_ref, dx1_ref, mod_ref, prm_ref, w_hbm, gx_ref, sums_ref, w_ref):
        @pl.when(pl.program_id(0) == 0)
        def _():
            pltpu.sync_copy(w_hbm, w_ref)
            sums_ref[...] = jnp.zeros((8, d), F32)

        dh = _dot_nt(dproj_ref[...], w_ref[...])
        xv = x_ref[...]
        r1 = lax.rsqrt(_rowmean(xv * xv) + EPS)
        xn = xv * r1
        gmix = prm_ref[P_GMIX:P_GMIX + 1, :]
        scale = 1.0 + mod_ref[M_SC1:M_SC1 + 1, :]
        dxn = dh * gmix * scale
        gx_ref[...] = dx1_ref[...] + r1 * (dxn - xn * _rowmean(dxn * xn))
        hx = dh * xn
        sums_ref[0:1, :] = sums_ref[0:1, :] + _colsum(dh)
        sums_ref[1:2, :] = sums_ref[1:2, :] + _colsum(hx) * gmix
        sums_ref[2:3, :] = sums_ref[2:3, :] + _colsum(hx) * scale

    tile = pl.BlockSpec((tm, d), lambda i: (i, 0))
    whole = pl.BlockSpec((8, d), lambda i: (0, 0))
    return pl.pallas_call(
        body, name="backward_in", grid=(t // tm,),
        out_shape=(jax.ShapeDtypeStruct((t, d), F32), jax.ShapeDtypeStruct((8, d), F32)),
        in_specs=[pl.BlockSpec((tm, n_all), lambda i: (i, 0)), tile, tile, whole, whole, ANY_SPEC],
        out_specs=[tile, whole],
        scratch_shapes=[pltpu.VMEM(w_in_g.shape, BF16)],
        compiler_params=_params(("arbitrary",)),
    )(dproj, x, dx1, mod, prm, w_in_g)


def _grad_w_in(me_arr, h, dproj, n, tk, partials, partial_view, exchanged_shapes):
    t, d = h.shape
    nk = t // tk
    nc = N_DEV // 2
    na = len(partials)

    def body(*refs):
        a_ref, b_ref = refs[1:3]
        pin_ref = refs[3 + na]
        acc_ref, stage_ref, send_sems, recv_sems, own_sem = refs[4 + 2 * na:9 + 2 * na]
        scatter_start, scatter_finish = _scatter_plan(refs[3:3 + na], refs[4 + na:4 + 2 * na], *refs[9 + 2 * na:],
                                                      view=partial_view)
        p, k = pl.program_id(0), pl.program_id(1)

        @pl.when((p == 0) & (k == 0))
        def _():
            scatter_start()

        part = _dot_tn(a_ref[...], b_ref[...])

        @pl.when(k == 0)
        def _():
            acc_ref[...] = part

        @pl.when(k > 0)
        def _():
            acc_ref[...] = acc_ref[...] + part

        def sends(step):
            x, y, c = _my_coords()
            me = _slot(x, y, c)
            flip = (step + 1) % nc
            out = []
            for other in (0, 1):
                dest = (_flip(x, flip & 2), _flip(y, flip & 1), _flip(c, other))
                cols = pl.ds(pl.multiple_of(dest[2] * n, LANES), n)
                src = stage_ref.at[step % 2, :, cols]
                if flip == 0 and other == 0:
                    local = pltpu.make_async_copy(src, pin_ref.at[me], own_sem)
                    out.append((local, local.wait))
                else:
                    kk = 2 * flip + other
                    remote = pltpu.make_async_remote_copy(
                        src_ref=src, dst_ref=pin_ref.at[me], send_sem=send_sems.at[kk], recv_sem=recv_sems.at[kk],
                        device_id=dest, device_id_type=MESH)
                    out.append((remote, remote.wait_send))
            return out

        for step in range(nc):
            @pl.when((p == step) & (k == nk - 1))
            def _(step=step):
                if step >= 2:
                    for _, wait_sent in sends(step - 2):
                        wait_sent()
                stage_ref[step % 2] = acc_ref[...].astype(BF16)
                for cp, _ in sends(step):
                    cp.start()

        @pl.when((p == nc - 1) & (k == nk - 1))
        def _():
            x, y, c = _my_coords()
            me = _slot(x, y, c)
            for kk in range(1, N_DEV):
                peer = (_flip(x, kk & 4), _flip(y, kk & 2), _flip(c, kk & 1))
                pltpu.make_async_remote_copy(
                    src_ref=pin_ref.at[me], dst_ref=pin_ref.at[_slot(*peer)], send_sem=send_sems.at[kk],
                    recv_sem=recv_sems.at[kk], device_id=peer, device_id_type=MESH).wait_recv()
            for step in (nc - 2, nc - 1):
                for _, wait_sent in sends(step):
                    wait_sent()
            scatter_finish()

    res = pl.pallas_call(
        body, name="grad_w_in",
        grid_spec=pltpu.PrefetchScalarGridSpec(
            num_scalar_prefetch=1, grid=(nc, nk),
            in_specs=[pl.BlockSpec((tk, d), lambda p, k, me: (k, 0)),
                      pl.BlockSpec((tk, 2 * n), lambda p, k, me: (k, (me[0] // 2) ^ ((p + 1) % nc)))] + [ANY_SPEC] * na,
            out_specs=[ANY_SPEC] * (1 + na),
            scratch_shapes=[pltpu.VMEM((d, 2 * n), F32), pltpu.VMEM((2, d, 2 * n), BF16),
                            pltpu.SemaphoreType.DMA((N_DEV,)), pltpu.SemaphoreType.DMA((N_DEV,)),
                            pltpu.SemaphoreType.DMA] + _comm_scratch(na)),
        out_shape=(jax.ShapeDtypeStruct((N_DEV, d, n), BF16),)
        + tuple(jax.ShapeDtypeStruct(s, q.dtype) for q, s in zip(partials, exchanged_shapes)),
        compiler_params=_params(("arbitrary", "arbitrary")),
    )(me_arr, h, dproj, *partials)
    return res[0], res[1:]


def _weight_grad(a, b, a_spec, b_spec, ns, m, n, nk, name, partials=(), by_columns=False,
                 partial_view=_leading_block, exchanged_shapes=None):
    na = len(partials)
    exchanged_shapes = exchanged_shapes or [p.shape for p in partials]

    def body(*refs):
        a_ref, b_ref = refs[:2]
        o_ref = refs[2 + na]
        acc_ref = refs[3 + 2 * na]
        s, k = pl.program_id(0), pl.program_id(1)
        if na:
            scatter_start, scatter_finish = _scatter_plan(refs[2:2 + na], refs[3 + na:3 + 2 * na], *refs[4 + 2 * na:],
                                                          view=partial_view)

            @pl.when((s == 0) & (k == 0))
            def _():
                scatter_start()

        av = a_ref[0] if len(a_ref.shape) == 3 else a_ref[...]
        bv = b_ref[0] if len(b_ref.shape) == 3 else b_ref[...]
        part = _dot_tn(av, bv)

        @pl.when(k == 0)
        def _():
            acc_ref[...] = part

        @pl.when(k > 0)
        def _():
            acc_ref[...] = acc_ref[...] + part

        @pl.when(k == nk - 1)
        def _():
            if by_columns:
                o_ref[...] = acc_ref[...].astype(BF16)
            else:
                o_ref[0] = acc_ref[...].astype(BF16)

        if na:
            @pl.when((s == ns - 1) & (k == nk - 1))
            def _():
                scatter_finish()

    if by_columns:
        out_shape, out_spec = (m, ns * n), pl.BlockSpec((m, n), lambda s, k: (0, s))
    else:
        out_shape, out_spec = (ns, m, n), pl.BlockSpec((1, m, n), lambda s, k: (s, 0, 0))
    res = pl.pallas_call(
        body, name=name, grid=(ns, nk),
        out_shape=(jax.ShapeDtypeStruct(out_shape, BF16),)
        + tuple(jax.ShapeDtypeStruct(sh, p.dtype) for p, sh in zip(partials, exchanged_shapes)),
        in_specs=[a_spec, b_spec] + [ANY_SPEC] * na,
        out_specs=[out_spec] + [ANY_SPEC] * na,
        scratch_shapes=[pltpu.VMEM((m, n), F32)] + (_comm_scratch(na) if na else []),
        compiler_params=_params(("arbitrary", "arbitrary")),
    )(a, b, *partials)
    return (res[0], res[1:]) if na else res[0]


def _adamw(w, g, m, v):
    m = ADAM_B1 * m + (1.0 - ADAM_B1) * g
    v = ADAM_B2 * v + (1.0 - ADAM_B2) * (g * g)
    m_hat = m / (1.0 - ADAM_B1 ** ADAM_STEP)
    v_hat = v / (1.0 - ADAM_B2 ** ADAM_STEP)
    delta = -ADAM_LR * (m_hat / (jnp.sqrt(v_hat) + ADAM_EPS) + ADAM_WD * w)
    return delta, m, v


def _adamw_shard(parts, w, m, v, tr, name):
    r, c = w.shape

    def body(p_ref, w_ref, m_ref, v_ref, g_ref, d_ref, nm_ref, nv_ref):
        g = p_ref[0].astype(F32)
        for s in range(1, N_DEV):
            g = g + p_ref[s].astype(F32)
        delta, nm, nv = _adamw(w_ref[...], g, m_ref[...], v_ref[...])
        g_ref[...] = g
        d_ref[...] = delta
        nm_ref[...] = nm
        nv_ref[...] = nv

    tile = pl.BlockSpec((tr, c), lambda i: (i, 0))
    return pl.pallas_call(
        body, name=name, grid=(r // tr,),
        out_shape=(jax.ShapeDtypeStruct((r, c), F32),) * 4,
        in_specs=[pl.BlockSpec((N_DEV, tr, c), lambda i: (0, i, 0)), tile, tile, tile],
        out_specs=[tile] * 4,
        compiler_params=_params(("arbitrary",)),
    )(parts, w, m, v)


def _adamw_members(parts, wmv, name):
    nm = len(wmv)
    r, c = wmv[0][0].shape

    def body(*refs):
        p_ref = refs[0]
        for j in range(nm):
            w_ref, m_ref, v_ref = refs[1 + 3 * j:4 + 3 * j]
            g = p_ref[0, j].astype(F32)
            for s in range(1, N_DEV):
                g = g + p_ref[s, j].astype(F32)
            delta, new_m, new_v = _adamw(w_ref[...], g, m_ref[...], v_ref[...])
            for o_ref, val in zip(refs[1 + 3 * nm + 4 * j:5 + 3 * nm + 4 * j], (g, delta, new_m, new_v)):
                o_ref[...] = val

    vm = pl.BlockSpec(memory_space=pltpu.VMEM)
    res = pl.pallas_call(
        body, name=name,
        out_shape=(jax.ShapeDtypeStruct((r, c), F32),) * (4 * nm),
        in_specs=[vm] * (1 + 3 * nm), out_specs=[vm] * (4 * nm),
        compiler_params=_params(),
    )(parts, *[a for triple in wmv for a in triple])
    return [tuple(res[4 * j:4 * j + 4]) for j in range(nm)]


def _ada_update(sc_all, dmod_cols, w, m, v, tr):
    d, n = w.shape

    def body(sc_ref, dm_ref, w_ref, m_ref, v_ref, g_ref, d_ref, nm_ref, nv_ref):
        g = lax.dot_general(sc_ref[...], dm_ref[...], (((0,), (0,)), ((), ())),
                            preferred_element_type=F32, precision=lax.Precision.HIGHEST)
        delta, nm, nv = _adamw(w_ref[...], g, m_ref[...], v_ref[...])
        g_ref[...] = g
        d_ref[...] = delta
        nm_ref[...] = nm
        nv_ref[...] = nv

    tile = pl.BlockSpec((tr, n), lambda i: (i, 0))
    return pl.pallas_call(
        body, name="ada_update", grid=(d // tr,),
        out_shape=(jax.ShapeDtypeStruct((d, n), F32),) * 4,
        in_specs=[pl.BlockSpec((N_DEV, tr), lambda i: (0, i)), pl.BlockSpec((N_DEV, n), lambda i: (0, 0)),
                  tile, tile, tile],
        out_specs=[tile] * 4,
        compiler_params=_params(("arbitrary",)),
    )(sc_all, dmod_cols, w, m, v)


def _small_exchange(vec, cg):
    l = vec.shape[2]
    rows = cg.shape[1]

    def body(vec_ref, cg_ref, vall_ref, cgr_ref, send_sems, recv_sems):
        x, y, c = _my_coords()
        me = _slot(x, y, c)
        vall_ref[me] = vec_ref[0]
        cgr_ref[me] = cg_ref[me]
        copies = []
        for k in range(1, N_DEV):
            peer = (_flip(x, k & 4), _flip(y, k & 2), _flip(c, k & 1))
            copies.append(pltpu.make_async_remote_copy(
                src_ref=vall_ref.at[me], dst_ref=vall_ref.at[me], send_sem=send_sems.at[k - 1],
                recv_sem=recv_sems.at[k - 1], device_id=peer, device_id_type=MESH))
            copies.append(pltpu.make_async_remote_copy(
                src_ref=cg_ref.at[_slot(*peer)], dst_ref=cgr_ref.at[me], send_sem=send_sems.at[7 + k - 1],
                recv_sem=recv_sems.at[7 + k - 1], device_id=peer, device_id_type=MESH))
        for cp in copies:
            cp.start()
        for cp in copies:
            cp.wait_recv()
        for cp in copies:
            cp.wait_send()

    vm = pl.BlockSpec(memory_space=pltpu.VMEM)
    return pl.pallas_call(
        body, name="small_exchange",
        out_shape=(jax.ShapeDtypeStruct((N_DEV, 1, l), F32), jax.ShapeDtypeStruct((N_DEV, rows, LANES), F32)),
        in_specs=[vm, vm], out_specs=[vm, vm],
        scratch_shapes=[pltpu.SemaphoreType.DMA((14,)), pltpu.SemaphoreType.DMA((14,))],
        compiler_params=_params(),
    )(vec, cg)


def _small_update(vall, cgr, smalls):
    ns = len(smalls)

    def body(*refs):
        vall_ref, cgr_ref = refs[0], refs[1]
        wmv = refs[2:2 + 3 * ns]
        outs = refs[2 + 3 * ns:]
        for p, (_, _, _, lo, hi, kind) in enumerate(smalls):
            w_ref, m_ref, v_ref = wmv[3 * p:3 * p + 3]
            part = (lambda s: vall_ref[s, :, lo:hi]) if kind == "vec" else (lambda s: cgr_ref[s, lo:hi, :])
            full = (lambda ref: ref[...]) if kind == "vec" else (lambda ref: ref[0])
            g = part(0)
            for s in range(1, N_DEV):
                g = g + part(s)
            delta, nm, nv = _adamw(full(w_ref), g, full(m_ref), full(v_ref))
            for o_ref, val in zip(outs[4 * p:4 * p + 4], (g, delta, nm, nv)):
                if kind == "vec":
                    o_ref[...] = val
                else:
                    o_ref[0] = val

    vm = pl.BlockSpec(memory_space=pltpu.VMEM)
    args = [vall, cgr]
    out_shape = []
    for w, m, v, _, _, _ in smalls:
        args += [w, m, v]
        out_shape += [jax.ShapeDtypeStruct(w.shape, F32)] * 4
    res = pl.pallas_call(
        body, name="small_update",
        out_shape=tuple(out_shape),
        in_specs=[vm] * len(args), out_specs=[vm] * len(out_shape),
        compiler_params=_params(),
    )(*args)
    return [res[4 * p:4 + 4 * p] for p in range(ns)]


def _pick(t, want):
    return want if t % want == 0 else t


def kernel(x, c, w_ada, b_ada, norm_mix_g, w_in, conv_short_w, w_short_out, conv_conf_w, conv_conf_b, conf_ln_g, conf_ln_b, w_conf_out, w_o, norm_ffn_g, w_ffn_in, w_ffn_out, final_norm_g, loss_target, m_w_ada, m_b_ada, m_norm_mix_g, m_w_in, m_conv_short_w, m_w_short_out, m_conv_conf_w, m_conv_conf_b, m_conf_ln_g, m_conf_ln_b, m_w_conf_out, m_w_o, m_norm_ffn_g, m_w_ffn_in, m_w_ffn_out, m_final_norm_g, v_w_ada, v_b_ada, v_norm_mix_g, v_w_in, v_conv_short_w, v_w_short_out, v_conv_conf_w, v_conv_conf_b, v_conf_ln_g, v_conf_ln_b, v_w_conf_out, v_w_o, v_norm_ffn_g, v_w_ffn_in, v_w_ffn_out, v_final_norm_g):
    t, d = x.shape[1], x.shape[2]
    x2 = x.reshape(t, d)
    tgt = loss_target.reshape(t, d)
    me = _slot(*_my_coords())
    tm = _pick(t, 256)
    tm_fwd_in = _pick(t, 512)
    tk = _pick(t, 2048)

    taps_loc = jnp.zeros((40, LANES), F32)
    taps_loc = taps_loc.at[0:SHORT_K].set(conv_short_w[0]).at[8:8 + CONF_K].set(conv_conf_w[0])
    prm = jnp.concatenate([norm_mix_g, norm_ffn_g, final_norm_g.reshape(1, d), conv_conf_b, conf_ln_g, conf_ln_b,
                           jnp.zeros((2, d), F32)], axis=0)

    n_in = w_in.shape[2]
    nk = t // tk
    tok = pl.BlockSpec((tk, d), lambda s, k: (k, 0))
    rows = d // N_DEV
    frows = w_ffn_out.shape[1]

    proj, h, w_in_g, mod, sc_all3, taps, (w_so_g, w_co_g, w_o_g) = _forward_in(
        jnp.reshape(me, (1,)).astype(jnp.int32), x2, prm, w_in[0].astype(BF16),
        (c, w_ada[0], b_ada, taps_loc), tm_fwd_in,
        [w_short_out[0].astype(BF16), w_conf_out[0].astype(BF16), w_o[0].astype(BF16)])
    w_so = w_so_g.reshape(d, d)
    w_co = w_co_g.reshape(d, d)
    w_oo = w_o_g.reshape(d, d)
    w_fi_t, m_fi_t, v_fi_t = w_ffn_in[0].T, m_w_ffn_in[0].T, v_w_ffn_in[0].T
    (x1, y_a, y_b, mix, uhat, conv3, sg, ga, gb, mixer_lhs, rstd), (w_fi_g, w_fo_g) = _forward_mix(
        x2, proj, mod, prm, taps, w_so, w_co, w_oo, tm, [w_fi_t.astype(BF16), w_ffn_out[0].astype(BF16)])
    fb = w_fi_g.shape[1]
    ff = N_DEV * frows
    w_fi_all = w_fi_g.reshape(N_DEV * fb, d)
    w_fo_all = w_fo_g.reshape(ff, d)
    dx2, ab, h2, act, sums_f = _forward_ffn(x1, tgt, mod, prm, w_fi_all, w_fo_all, tm)

    dx1, df, dab, sums_b = _backward_ffn(dx2, x1, ab, mod, prm, w_fi_all, w_fo_all, tm)
    fb2 = 2 * fb
    pair_tok = pl.BlockSpec((tk, fb2), lambda s, k: (k, s))
    g_fi = _weight_grad(dab, h2, pair_tok, tok, N_DEV // 2, fb2, d, nk, "grad_w_ffn_in")
    g_fo = _weight_grad(act, df, pair_tok, tok, N_DEV // 4, fb2, d, nk, "grad_w_ffn_out")
    (dproj, mixer_rhs, sums_m, dw3p, dw31p), (p_fi, p_fo) = _backward_mix(
        dx1, proj, y_a, y_b, mix, uhat, rstd, conv3, sg, ga, gb, mod, prm, taps, w_so, w_co, w_oo, tm,
        [g_fi.reshape(N_DEV, fb, d), g_fo.reshape(N_DEV, frows, d)])
    member_tok = pl.BlockSpec((1, tk, d), lambda s, k: (s, k, 0))
    g_mixers = _weight_grad(mixer_lhs, mixer_rhs, member_tok, member_tok, 3, d, d, nk, "grad_w_mixers")
    p_in, (p_mixers,) = _grad_w_in(
        jnp.reshape(me, (1,)).astype(jnp.int32), h, dproj, n_in, _pick(t, 1024),
        [g_mixers.reshape(3, N_DEV, rows, d)], lambda ref, dev: ref.at[:, dev], [(N_DEV, 3, rows, d)])
    grad_x, sums_i = _backward_in(dproj, x2, dx1, mod, prm, w_in_g, tm)

    up_in = _adamw_shard(p_in, w_in[0], m_w_in[0], v_w_in[0], _pick(d, 256), "adamw_w_in")
    up_so, up_co, up_oo = _adamw_members(
        p_mixers, [(w_short_out[0], m_w_short_out[0], v_w_short_out[0]),
                   (w_conf_out[0], m_w_conf_out[0], v_w_conf_out[0]), (w_o[0], m_w_o[0], v_w_o[0])], "adamw_w_mixers")
    up_fi = tuple(a.T for a in _adamw_shard(p_fi, w_fi_t, m_fi_t, v_fi_t, fb // 2, "adamw_w_ffn_in"))
    up_fo = _adamw_shard(p_fo, w_ffn_out[0], m_w_ffn_out[0], v_w_ffn_out[0], frows, "adamw_w_ffn_out")

    vec = jnp.concatenate([sums_i[0:2], sums_m[0:1], sums_b[0:2], sums_f[1:2],
                           sums_i[2:3], sums_m[3:4], sums_m[1:3], sums_b[2:3], sums_f[0:1],
                           sums_f[2:3]], axis=0)
    vec = vec.reshape(1, 1, 13 * d)
    dw3 = dw3p.reshape(SHORT_K, SUBLANES, d).sum(axis=1)
    dw31 = dw31p.reshape(CONF_K, SUBLANES, d).sum(axis=1)
    cg = jnp.zeros((40, d), F32).at[0:SHORT_K].set(dw3).at[8:8 + CONF_K].set(dw31)
    cg = cg.reshape(40, N_DEV, LANES).transpose(1, 0, 2)
    fin = lambda a: a.reshape(1, d)
    smalls = [
        (b_ada, m_b_ada, v_b_ada, 0, 6 * d, "vec"),
        (norm_mix_g, m_norm_mix_g, v_norm_mix_g, 6 * d, 7 * d, "vec"),
        (conv_short_w, m_conv_short_w, v_conv_short_w, 0, SHORT_K, "cg"),
        (conv_conf_w, m_conv_conf_w, v_conv_conf_w, 8, 8 + CONF_K, "cg"),
        (conv_conf_b, m_conv_conf_b, v_conv_conf_b, 7 * d, 8 * d, "vec"),
        (conf_ln_g, m_conf_ln_g, v_conf_ln_g, 8 * d, 9 * d, "vec"),
        (conf_ln_b, m_conf_ln_b, v_conf_ln_b, 9 * d, 10 * d, "vec"),
        (norm_ffn_g, m_norm_ffn_g, v_norm_ffn_g, 10 * d, 11 * d, "vec"),
        (fin(final_norm_g), fin(m_final_norm_g), fin(v_final_norm_g), 11 * d, 12 * d, "vec"),
    ]
    vall, cgr = _small_exchange(vec, cg)
    up_small = _small_update(vall, cgr, smalls)
    n_ada = w_ada.shape[2]
    dmod_all = vall.reshape(N_DEV, 13 * d)[:, 0:6 * d]
    dmod_cols = lax.dynamic_slice(dmod_all, (0, me * n_ada), (N_DEV, n_ada))
    up_ada = _ada_update(sc_all3.reshape(N_DEV, d), dmod_cols, w_ada[0], m_w_ada[0], v_w_ada[0], _pick(d, 256))

    loss = jnp.sum(vall.reshape(N_DEV, 13 * d)[:, 12 * d:])

    lead = lambda a: a.reshape((1,) + a.shape)
    ups = [tuple(lead(a) for a in up_ada), up_small[0], up_small[1], tuple(lead(a) for a in up_in),
           up_small[2], tuple(lead(a) for a in up_so), up_small[3],
           up_small[4], up_small[5], up_small[6],
           tuple(lead(a) for a in up_co), tuple(lead(a) for a in up_oo), up_small[7],
           tuple(lead(a) for a in up_fi), tuple(lead(a) for a in up_fo),
           tuple(a.reshape(d) for a in up_small[8])]
    grads = [u[0] for u in ups]
    deltas = [u[1] for u in ups]
    new_m = [u[2] for u in ups]
    new_v = [u[3] for u in ups]
    return (loss, grad_x.reshape(1, t, d), *grads, *deltas, *new_m, *new_v)
```

```python
import functools

import jax
import jax.numpy as jnp
from jax import lax
from jax.experimental import pallas as pl
from jax.experimental.pallas import tpu as pltpu

F32 = jnp.float32
BF16 = jnp.bfloat16
MESH = pl.DeviceIdType.MESH

N_DEV = 8
EPS = 1e-6
LN_EPS = 1e-5
SHORT_K = 3
CONF_K = 31
ADAM_LR = 0.001
ADAM_B1 = 0.9
ADAM_B2 = 0.999
ADAM_EPS = 1e-08
ADAM_WD = 0.01
ADAM_STEP = 10

LANES = 128
SUBLANES = 8
CONV_ROWS = 64
HALO_SHORT = 8
HALO_CONF = 32
VMEM_LIMIT = 56 * 1024 * 1024

M_SH1, M_SC1, M_G1, M_SH2, M_SC2, M_G2 = range(6)
P_GMIX, P_GFFN, P_GFIN, P_CBIAS, P_LNG, P_LNB = range(6)


def _params(sem=None, **kw):
    return pltpu.CompilerParams(dimension_semantics=sem, vmem_limit_bytes=VMEM_LIMIT, **kw)


def _sigmoid(v):
    return jax.nn.sigmoid(v)


def _dot(a, b):
    return jnp.dot(a, b, preferred_element_type=F32)


def _dot_nt(a, b):
    return lax.dot_general(a, b, (((1,), (1,)), ((), ())), preferred_element_type=F32)


def _dot_tn(a, b):
    return lax.dot_general(a, b, (((0,), (0,)), ((), ())), preferred_element_type=F32)


def _colsum(v):
    return jnp.sum(v, axis=0, keepdims=True)


def _rowmean(v):
    return jnp.mean(v, axis=-1, keepdims=True)


def _my_coords():
    return lax.axis_index("x"), lax.axis_index("y"), lax.axis_index("c")


def _slot(px, py, pc):
    return 4 * px + 2 * py + pc


def _flip(v, bit):
    return 1 - v if bit else v


def _taps_by_residue(taps):
    by_res = {}
    for wi, off in taps:
        by_res.setdefault(off % SUBLANES, []).append((wi, off // SUBLANES))
    return sorted(by_res.items())


def _tap_conv(ext_ref, w_ref, taps, tm, extra, out_ref):
    d = out_ref.shape[1]
    rb = min(CONV_ROWS, tm)
    wrows = rb + extra
    groups = _taps_by_residue(taps)

    def block(i, carry):
        base = pl.multiple_of(i * rb, SUBLANES)
        for lc in range(d // LANES):
            ls = pl.ds(lc * LANES, LANES)
            win = ext_ref[pl.ds(base, wrows), ls]
            acc = None
            for r, lst in groups:
                sh = win if r == 0 else pltpu.roll(win, wrows - r, 0)
                for wi, q in lst:
                    term = w_ref[wi:wi + 1, ls] * sh[SUBLANES * q:SUBLANES * q + rb, :]
                    acc = term if acc is None else acc + term
            out_ref[pl.ds(base, rb), ls] = acc
        return carry

    lax.fori_loop(0, tm // rb, block, 0)


def _tap_wgrad(a_ref, ext_ref, taps, tm, extra, acc_ref):
    d = a_ref.shape[1]
    rb = min(CONV_ROWS, tm)
    wrows = rb + extra
    groups = _taps_by_residue(taps)

    def block(i, carry):
        base = pl.multiple_of(i * rb, SUBLANES)
        for lc in range(d // LANES):
            ls = pl.ds(lc * LANES, LANES)
            a_blk = a_ref[pl.ds(base, rb), ls]
            win = ext_ref[pl.ds(base, wrows), ls]
            for r, lst in groups:
                sh = win if r == 0 else pltpu.roll(win, wrows - r, 0)
                for wi, q in lst:
                    prod = a_blk * sh[SUBLANES * q:SUBLANES * q + rb, :]
                    part = prod[0:SUBLANES, :]
                    for s in range(1, rb // SUBLANES):
                        part = part + prod[SUBLANES * s:SUBLANES * (s + 1), :]
                    rows = pl.ds(SUBLANES * wi, SUBLANES)
                    acc_ref[rows, ls] = acc_ref[rows, ls] + part
        return carry

    lax.fori_loop(0, tm // rb, block, 0)


def _causal_taps(k, halo):
    return [(i, halo - (k - 1) + i) for i in range(k)]


def _anticausal_taps(k):
    return [(i, (k - 1) - i) for i in range(k)]


def _ada_scratch(d, nloc, trows):
    return [pltpu.VMEM((N_DEV, 1, d), F32), pltpu.VMEM((N_DEV, 1, nloc), F32), pltpu.VMEM((N_DEV, 1, nloc), F32),
            pltpu.VMEM((N_DEV, trows, LANES), F32), pltpu.SemaphoreType.DMA((21,)), pltpu.SemaphoreType.DMA((21,))]


def _ada_exchange(c_ref, w_ref, b_ref, t_ref, mod_ref, sc_ref, taps_ref,
                  scall_ref, part_ref, modrecv_ref, tapsall_ref, send_sems, recv_sems):
    x, y, cc = _my_coords()
    me = _slot(x, y, cc)
    d = c_ref.shape[1]
    cv = c_ref[...]
    scall_ref[me] = cv * _sigmoid(cv)
    tapsall_ref[me] = t_ref[...]

    def peer_of(k):
        return (_flip(x, k & 4), _flip(y, k & 2), _flip(cc, k & 1))

    def gather_copy(ref, base, k):
        return pltpu.make_async_remote_copy(
            src_ref=ref.at[me], dst_ref=ref.at[me], send_sem=send_sems.at[base + k - 1],
            recv_sem=recv_sems.at[base + k - 1], device_id=peer_of(k), device_id_type=MESH)

    first = [gather_copy(scall_ref, 0, k) for k in range(1, N_DEV)]
    first += [gather_copy(tapsall_ref, 7, k) for k in range(1, N_DEV)]
    for cp in first:
        cp.start()
    for cp in first[:7]:
        cp.wait_recv()
    sc_all = jnp.concatenate([scall_ref[s] for s in range(N_DEV)], axis=0)
    for s in range(N_DEV):
        sc_ref[s] = scall_ref[s]
    part = jnp.dot(sc_all, w_ref[...], preferred_element_type=F32,
                   precision=lax.Precision.HIGHEST)
    for b in range(N_DEV):
        part_ref[b] = part[b:b + 1, :]
    modrecv_ref[me] = part_ref[me]
    second = []
    for k in range(1, N_DEV):
        px, py, pc = peer_of(k)
        second.append(pltpu.make_async_remote_copy(
            src_ref=part_ref.at[_slot(px, py, pc)], dst_ref=modrecv_ref.at[me],
            send_sem=send_sems.at[14 + k - 1], recv_sem=recv_sems.at[14 + k - 1],
            device_id=(px, py, pc), device_id_type=MESH))
    for cp in second:
        cp.start()
    for cp in second:
        cp.wait_recv()
    mod = jnp.concatenate([modrecv_ref[s] for s in range(N_DEV)], axis=1) + b_ref[...]
    for r in range(6):
        mod_ref[r:r + 1, :] = mod[:, r * d:(r + 1) * d]
    mod_ref[6:8, :] = jnp.zeros((2, d), F32)
    for cp in first[7:]:
        cp.wait_recv()
    taps_ref[...] = jnp.concatenate([tapsall_ref[s] for s in range(N_DEV)], axis=1)
    for cp in first + second:
        cp.wait_send()


ANY_SPEC = pl.BlockSpec(memory_space=pl.ANY)


def _comm_scratch(na):
    return [pltpu.SemaphoreType.DMA((7 * na,)), pltpu.SemaphoreType.DMA((7 * na,)), pltpu.SemaphoreType.DMA((na,))]


def _leading_block(ref, slot):
    return ref.at[slot]


def _column_block(width):
    def view(ref, slot):
        return ref.at[:, pl.ds(pl.multiple_of(slot * width, LANES), width)]
    return view


class _Gather:
    def __init__(self, ins, outs, send_sems, recv_sems, local_sems, view=_leading_block):
        self.na = len(ins)
        x, y, c = _my_coords()
        self.c = c
        self.view = view
        self.me, self.sibling = (x, y, c), (x, y, 1 - c)
        self.chips = [(1 - x, y), (x, 1 - y), (1 - x, 1 - y)]
        self.outs, self.send_sems, self.recv_sems = outs, send_sems, recv_sems
        self.mine = [pltpu.make_async_copy(ins[a], view(outs[a], _slot(*self.me)), local_sems.at[a])
                     for a in range(self.na)]
        self.first = []
        for a in range(self.na):
            self.first.append(self._copy(a, 0, self.me, self.sibling, src=ins[a]))
            self.first += [self._copy(a, 1 + j, self.me, (*chip, c), src=ins[a]) for j, chip in enumerate(self.chips)]
        self.passed = [self._copy(a, 4 + j, (*chip, c), self.sibling)
                       for a in range(self.na) for j, chip in enumerate(self.chips)]

    def _copy(self, a, k, block, to, src=None):
        dst = self.view(self.outs[a], _slot(*block))
        return pltpu.make_async_remote_copy(
            src_ref=dst if src is None else src, dst_ref=dst,
            send_sem=self.send_sems.at[7 * a + k], recv_sem=self.recv_sems.at[7 * a + k],
            device_id=to, device_id_type=MESH)

    def start(self):
        self.start_near()
        self.start_far()

    def start_near(self):
        self.start_local()
        self.start_one_hop()

    def start_local(self):
        for cp in self.mine + self.first[0::4]:
            cp.start()

    def start_one_hop(self):
        for cp in self.first[1::4] + self.first[2::4]:
            cp.start()

    def start_far(self):
        for cp in self.first[3::4]:
            cp.start()

    def forward_chip(self, j):
        for a in range(self.na):
            self._copy(a, 1 + j, (*self.chips[j], self.c), self.me).wait_recv()
            self.passed[3 * a + j].start()

    def wait_sibling_own(self):
        for a in range(self.na):
            self._copy(a, 0, self.sibling, self.me).wait_recv()

    def wait_sibling_passed(self, j):
        for a in range(self.na):
            self._copy(a, 4 + j, (*self.chips[j], 1 - self.c), self.me).wait_recv()

    def finish_sends(self):
        for cp in self.first + self.passed:
            cp.wait_send()
        for cp in self.mine:
            cp.wait()

    def forward(self):
        for j in range(3):
            self.forward_chip(j)

    def finish(self):
        self.wait_sibling_own()
        for j in range(3):
            self.wait_sibling_passed(j)
        self.finish_sends()


def _scatter_plan(ins, outs, send_sems, recv_sems, local_sems, view=_leading_block):
    na = len(ins)
    x, y, c = _my_coords()
    me = _slot(x, y, c)
    mine = [pltpu.make_async_copy(view(ins[a], me), outs[a].at[me], local_sems.at[a]) for a in range(na)]
    copies = []
    for k in range(1, N_DEV):
        peer = (_flip(x, k & 4), _flip(y, k & 2), _flip(c, k & 1))
        for a in range(na):
            copies.append(pltpu.make_async_remote_copy(
                src_ref=view(ins[a], _slot(*peer)), dst_ref=outs[a].at[me],
                send_sem=send_sems.at[7 * a + k - 1], recv_sem=recv_sems.at[7 * a + k - 1],
                device_id=peer, device_id_type=MESH))

    def start():
        for cp in mine + copies:
            cp.start()

    def finish():
        for cp in copies:
            cp.wait_recv()
        for cp in copies:
            cp.wait_send()
        for cp in mine:
            cp.wait()

    return start, finish


def _forward_in(me_arr, x, prm, w_in_loc, ada, tm, shards):
    t, d = x.shape
    n = w_in_loc.shape[1]
    ns = N_DEV // 2
    na = len(shards)
    nt = t // tm
    nloc = ada[1].shape[1]
    trows = ada[3].shape[0]
    chip_of_pass = {1: 1, 2: 0, 3: 2}

    def body(*refs):
        x_ref, prm_ref, wloc_ref = refs[1:4]
        ada_in = refs[4:8]
        shard_refs = refs[8:8 + na]
        proj_ref, h_ref, wg_ref, mod_ref, sc_ref, taps_ref = refs[8 + na:14 + na]
        gathered_refs = refs[14 + na:14 + 2 * na]
        hall_ref, wv_ref, wv_sem = refs[14 + 2 * na:17 + 2 * na]
        ada_scratch = refs[17 + 2 * na:23 + 2 * na]
        sems = refs[23 + 2 * na:]
        s, i = pl.program_id(0), pl.program_id(1)
        columns = _column_block(n)

        def gathers():
            return (_Gather([wloc_ref], [wg_ref], *sems[0:3], view=columns),
                    _Gather(shard_refs, gathered_refs, *sems[3:6]))

        def shard_copy(dev, p, other_core, src=None):
            half = pl.ds(pl.multiple_of((dev & 1) * n, LANES), n)
            return pltpu.make_async_copy(columns(wg_ref, dev) if src is None else src,
                                         wv_ref.at[p % 2, :, half], wv_sem.at[2 * (p % 2) + other_core])

        def pair_copies(p, me):
            return [shard_copy(me ^ (2 * p), p, 0), shard_copy(me ^ (2 * p) ^ 1, p, 1)]

        @pl.when(i == 0)
        def _():
            me = _slot(*_my_coords())

            @pl.when(s == 0)
            def _():
                g_in, _ = gathers()
                g_in.start_local()
                _ada_exchange(*ada_in, mod_ref, sc_ref, taps_ref, *ada_scratch)
                g_in.start_one_hop()
                own = shard_copy(me, 0, 0, src=wloc_ref)
                own.start()
                g_in.wait_sibling_own()
                sib = shard_copy(me ^ 1, 0, 1)
                sib.start()
                own.wait()
                sib.wait()

            for p in range(1, ns):
                @pl.when(s == p)
                def _(p=p):
                    for cp in pair_copies(p, me):
                        cp.wait()

        @pl.when(i == max(nt - 3, 0))
        def _():
            g_in, g_rest = gathers()
            for p in range(1, ns):
                @pl.when(s == p - 1)
                def _(p=p):
                    g_in.forward_chip(chip_of_pass[p])
                    if p == 1:
                        g_in.start_far()
                    if p == 2:
                        g_rest.start()
                    if p == 3:
                        g_rest.forward()

        @pl.when(i == nt - 1)
        def _():
            g_in, _ = gathers()
            me = _slot(*_my_coords())
            for p in range(1, ns):
                @pl.when(s == p - 1)
                def _(p=p):
                    g_in.wait_sibling_passed(chip_of_pass[p])
                    for cp in pair_copies(p, me):
                        cp.start()

        @pl.when(s == 0)
        def _():
            xv = x_ref[...]
            r = lax.rsqrt(_rowmean(xv * xv) + EPS)
            h = xv * r * prm_ref[P_GMIX:P_GMIX + 1, :] * (1.0 + mod_ref[M_SC1:M_SC1 + 1, :]) \
                + mod_ref[M_SH1:M_SH1 + 1, :]
            hb = h.astype(BF16)
            hall_ref[i] = hb
            h_ref[...] = hb

        proj_ref[...] = _dot(hall_ref[i], wv_ref[s % 2]).astype(BF16)

        @pl.when((s == ns - 1) & (i == nt - 1))
        def _():
            g_in, g_rest = gathers()
            g_in.finish_sends()
            g_rest.finish()

    x_tile = pl.BlockSpec((tm, d), lambda s, i, me: (jnp.where(s == 0, i, nt - 1), 0))
    whole = lambda a: pl.BlockSpec(a.shape, lambda s, i, me: (0,) * len(a.shape))
    small_out = (jax.ShapeDtypeStruct((8, d), F32), jax.ShapeDtypeStruct((N_DEV, 1, d), F32),
                 jax.ShapeDtypeStruct((trows, N_DEV * LANES), F32))
    res = pl.pallas_call(
        body, name="forward_in",
        grid_spec=pltpu.PrefetchScalarGridSpec(
            num_scalar_prefetch=1, grid=(ns, nt),
            in_specs=[x_tile, whole(prm), ANY_SPEC] + [whole(a) for a in ada] + [ANY_SPEC] * na,
            out_specs=[pl.BlockSpec((tm, 2 * n), lambda s, i, me: (i, (me[0] // 2) ^ s)), x_tile, ANY_SPEC]
            + [whole(a) for a in small_out] + [ANY_SPEC] * na,
            scratch_shapes=[pltpu.VMEM((nt, tm, d), BF16), pltpu.VMEM((2, d, 2 * n), BF16),
                            pltpu.SemaphoreType.DMA((4,))]
            + _ada_scratch(d, nloc, trows) + _comm_scratch(1) + _comm_scratch(na)),
        out_shape=(jax.ShapeDtypeStruct((t, N_DEV * n), BF16), jax.ShapeDtypeStruct((t, d), BF16),
                   jax.ShapeDtypeStruct((d, N_DEV * n), BF16)) + small_out
        + tuple(jax.ShapeDtypeStruct((N_DEV,) + a.shape, a.dtype) for a in shards),
        compiler_params=_params(("arbitrary", "arbitrary")),
    )(me_arr, x, prm, w_in_loc, *ada, *shards)
    return res[0], res[1], res[2], res[3], res[4], res[5], res[6:]


def _forward_mix(x, proj, mod, prm, taps, w_so, w_co, w_o, tm, shards):
    t, d = x.shape
    nt = t // tm
    na = len(shards)
    hs, hc = HALO_SHORT, HALO_CONF
    r3, r31 = 0, 8

    def body(*refs):
        x_ref, proj_ref, mod_ref, prm_ref, taps_ref, wso_ref, wco_ref, wo_ref = refs[:8]
        (x1_ref, ya_ref, yb_ref, mix_ref, uhat_ref, c3_ref, sg_ref, ga_ref, gb_ref, lhs_ref,
         rstd_ref) = refs[8 + na:19 + na]
        cv_ext, u0_ext, conv3_ref, u1f_ref = refs[19 + 2 * na:23 + 2 * na]
        gather = _Gather(refs[8:8 + na], refs[19 + na:19 + 2 * na], *refs[23 + 2 * na:])
        i = pl.program_id(0)

        @pl.when(i == 0)
        def _():
            gather.start()
            cv_ext[0:hs, :] = jnp.zeros((hs, d), F32)
            u0_ext[0:hc, :] = jnp.zeros((hc, d), F32)

        @pl.when(i == (3 * nt) // 4)
        def _():
            gather.forward()

        def col(g):
            return proj_ref[:, g * d:(g + 1) * d].astype(F32)

        cv_ext[hs:hs + tm, :] = col(1) * col(2)
        sg = _sigmoid(col(4))
        sg_ref[...] = sg.astype(BF16)
        u0_ext[hc:hc + tm, :] = col(3) * sg
        _tap_conv(cv_ext, taps_ref, [(r3 + wi, off) for wi, off in _causal_taps(SHORT_K, hs)], tm, hs, conv3_ref)
        _tap_conv(u0_ext, taps_ref, [(r31 + wi, off) for wi, off in _causal_taps(CONF_K, hc)], tm, hc, u1f_ref)
        cv_ext[0:hs, :] = cv_ext[tm:tm + hs, :]
        u0_ext[0:hc, :] = u0_ext[tm:tm + hc, :]

        c3_ref[...] = conv3_ref[...].astype(BF16)
        ya_pre = (col(0) * c3_ref[...].astype(F32)).astype(BF16)
        lhs_ref[0] = ya_pre
        y_a = _dot(ya_pre, wso_ref[...])
        u1 = u1f_ref[...] + prm_ref[P_CBIAS:P_CBIAS + 1, :]
        mu = _rowmean(u1)
        uc = u1 - mu
        rstd = lax.rsqrt(_rowmean(uc * uc) + LN_EPS)
        rstd_ref[...] = rstd
        uhat_ref[...] = (uc * rstd).astype(BF16)
        u2 = uhat_ref[...].astype(F32) * prm_ref[P_LNG:P_LNG + 1, :] + prm_ref[P_LNB:P_LNB + 1, :]
        u3 = (u2 * _sigmoid(u2)).astype(BF16)
        lhs_ref[1] = u3
        y_b = _dot(u3, wco_ref[...])
        ya_ref[...] = y_a.astype(BF16)
        yb_ref[...] = y_b.astype(BF16)
        ga = _sigmoid(col(5))
        gb = _sigmoid(col(6))
        ga_ref[...] = ga.astype(BF16)
        gb_ref[...] = gb.astype(BF16)
        merged = (ga * ya_ref[...].astype(F32) + gb * yb_ref[...].astype(F32)).astype(BF16)
        lhs_ref[2] = merged
        mix = _dot(merged, wo_ref[...])
        mix_ref[...] = mix.astype(BF16)
        x1_ref[...] = x_ref[...] + mod_ref[M_G1:M_G1 + 1, :] * mix

        @pl.when(i == nt - 1)
        def _():
            gather.finish()

    tile = pl.BlockSpec((tm, d), lambda i: (i, 0))
    whole = lambda shape: pl.BlockSpec(shape, lambda i: (0,) * len(shape))
    res = pl.pallas_call(
        body, name="forward_mix", grid=(nt,),
        out_shape=(jax.ShapeDtypeStruct((t, d), F32),) + (jax.ShapeDtypeStruct((t, d), BF16),) * 8
        + (jax.ShapeDtypeStruct((3, t, d), BF16), jax.ShapeDtypeStruct((t, 1), F32))
        + tuple(jax.ShapeDtypeStruct((N_DEV,) + a.shape, a.dtype) for a in shards),
        in_specs=[tile, pl.BlockSpec((tm, 7 * d), lambda i: (i, 0)), whole((8, d)), whole((8, d)),
                  whole(taps.shape), whole((d, d)), whole((d, d)), whole((d, d))] + [ANY_SPEC] * na,
        out_specs=[tile] * 9 + [pl.BlockSpec((3, tm, d), lambda i: (0, i, 0)), pl.BlockSpec((tm, 1), lambda i: (i, 0))]
        + [ANY_SPEC] * na,
        scratch_shapes=[pltpu.VMEM((hs + tm, d), F32), pltpu.VMEM((hc + tm, d), F32),
                        pltpu.VMEM((tm, d), F32), pltpu.VMEM((tm, d), F32)] + _comm_scratch(na),
        compiler_params=_params(("arbitrary",)),
    )(x, proj, mod, prm, taps, w_so, w_co, w_o, *shards)
    return res[:11], res[11:]


def _ffn_chunks(ff):
    mxu = 2 * LANES
    cut = (ff // mxu + 1) // 2 * mxu
    return [(0, cut), (cut, ff)] if 0 < cut < ff and ff % mxu == 0 else [(0, ff)]


def _forward_ffn(x1, tgt, mod, prm, w_fi_t, w_fo, tm):
    t, d = x1.shape
    ff = w_fo.shape[0]

    def body(x1_ref, tgt_ref, mod_ref, prm_ref, wfi_hbm, wfo_hbm,
             dx2_ref, ab_ref, h2_ref, act_ref, sums_ref, wfi_ref, wfo_ref):
        i = pl.program_id(0)

        @pl.when(i == 0)
        def _():
            pltpu.sync_copy(wfi_hbm, wfi_ref)
            pltpu.sync_copy(wfo_hbm, wfo_ref)
            sums_ref[...] = jnp.zeros((8, d), F32)

        x1v = x1_ref[...]
        r2 = lax.rsqrt(_rowmean(x1v * x1v) + EPS)
        h2 = (x1v * r2 * prm_ref[P_GFFN:P_GFFN + 1, :] * (1.0 + mod_ref[M_SC2:M_SC2 + 1, :])
              + mod_ref[M_SH2:M_SH2 + 1, :]).astype(BF16)
        h2_ref[...] = h2
        f = jnp.zeros((tm, d), F32)
        for c0, c1 in _ffn_chunks(ff):
            ab_ref[:, c0:c1] = _dot_nt(h2, wfi_ref[c0:c1, :]).astype(BF16)
            ab_ref[:, ff + c0:ff + c1] = _dot_nt(h2, wfi_ref[ff + c0:ff + c1, :]).astype(BF16)
            a = ab_ref[:, c0:c1].astype(F32)
            act = (a * _sigmoid(a) * ab_ref[:, ff + c0:ff + c1].astype(F32)).astype(BF16)
            act_ref[:, c0:c1] = act
            f = f + _dot(act, wfo_ref[c0:c1, :])
        x2 = x1v + mod_ref[M_G2:M_G2 + 1, :] * f
        r3 = lax.rsqrt(_rowmean(x2 * x2) + EPS)
        xn3 = x2 * r3
        gfin = prm_ref[P_GFIN:P_GFIN + 1, :]
        err = xn3 * gfin - tgt_ref[...]
        dy = err * (1.0 / d)
        dxn3 = dy * gfin
        dx2 = r3 * (dxn3 - xn3 * _rowmean(dxn3 * xn3))
        dx2_ref[...] = dx2
        sums_ref[0:1, :] = sums_ref[0:1, :] + _colsum(dy * xn3)
        sums_ref[1:2, :] = sums_ref[1:2, :] + _colsum(dx2 * f)
        sums_ref[2:3, :] = sums_ref[2:3, :] + _colsum(err * err) * (0.5 / d)

    tile = pl.BlockSpec((tm, d), lambda i: (i, 0))
    whole = lambda shape: pl.BlockSpec(shape, lambda i: (0,) * len(shape))
    return pl.pallas_call(
        body, name="forward_ffn", grid=(t // tm,),
        out_shape=(jax.ShapeDtypeStruct((t, d), F32), jax.ShapeDtypeStruct((t, 2 * ff), BF16),
                   jax.ShapeDtypeStruct((t, d), BF16), jax.ShapeDtypeStruct((t, ff), BF16),
                   jax.ShapeDtypeStruct((8, d), F32)),
        in_specs=[tile, tile, whole((8, d)), whole((8, d)), ANY_SPEC, ANY_SPEC],
        out_specs=[tile, pl.BlockSpec((tm, 2 * ff), lambda i: (i, 0)), tile,
                   pl.BlockSpec((tm, ff), lambda i: (i, 0)), whole((8, d))],
        scratch_shapes=[pltpu.VMEM(w_fi_t.shape, BF16), pltpu.VMEM(w_fo.shape, BF16)],
        compiler_params=_params(("arbitrary",)),
    )(x1, tgt, mod, prm, w_fi_t, w_fo)


def _backward_ffn(dx2, x1, ab, mod, prm, w_fi_t, w_fo, tm):
    t, d = x1.shape
    ff = w_fo.shape[0]

    def body(dx2_ref, x1_ref, ab_ref, mod_ref, prm_ref, wfi_hbm, wfo_hbm,
             dx1_ref, df_ref, dab_ref, sums_ref, wfi_ref, wfo_ref):
        i = pl.program_id(0)

        @pl.when(i == 0)
        def _():
            pltpu.sync_copy(wfi_hbm, wfi_ref)
            pltpu.sync_copy(wfo_hbm, wfo_ref)
            sums_ref[...] = jnp.zeros((8, d), F32)

        dx2v = dx2_ref[...]
        df = (mod_ref[M_G2:M_G2 + 1, :] * dx2v).astype(BF16)
        df_ref[...] = df
        dh2 = jnp.zeros((tm, d), F32)
        for c0, c1 in _ffn_chunks(ff):
            dact = _dot_nt(df, wfo_ref[c0:c1, :])
            a = ab_ref[:, c0:c1].astype(F32)
            b = ab_ref[:, ff + c0:ff + c1].astype(F32)
            s = _sigmoid(a)
            sil = a * s
            da =(dact * b * (s * (1.0 + a * (1.0 - s)))).astype(BF16)
            db = (dact * sil).astype(BF16)
            dab_ref[:, c0:c1] = da
            dab_ref[:, ff + c0:ff + c1] = db
            dh2 = dh2 + _dot(da, wfi_ref[c0:c1, :]) + _dot(db, wfi_ref[ff + c0:ff + c1, :])
        x1v = x1_ref[...]
        r2 = lax.rsqrt(_rowmean(x1v * x1v) + EPS)
        xn2 = x1v * r2
        gffn = prm_ref[P_GFFN:P_GFFN + 1, :]
        scale = 1.0 + mod_ref[M_SC2:M_SC2 + 1, :]
        dxn2 = dh2 * gffn * scale
        dx1_ref[...] = dx2v + r2 * (dxn2 - xn2 * _rowmean(dxn2 * xn2))
        hx = dh2 * xn2
        sums_ref[0:1, :] = sums_ref[0:1, :] + _colsum(dh2)
        sums_ref[1:2, :] = sums_ref[1:2, :] + _colsum(hx) * gffn
        sums_ref[2:3, :] = sums_ref[2:3, :] + _colsum(hx) * scale

    tile = pl.BlockSpec((tm, d), lambda i: (i, 0))
    whole = lambda shape: pl.BlockSpec(shape, lambda i: (0,) * len(shape))
    wide = lambda n: pl.BlockSpec((tm, n), lambda i: (i, 0))
    return pl.pallas_call(
        body, name="backward_ffn", grid=(t // tm,),
        out_shape=(jax.ShapeDtypeStruct((t, d), F32), jax.ShapeDtypeStruct((t, d), BF16),
                   jax.ShapeDtypeStruct((t, 2 * ff), BF16), jax.ShapeDtypeStruct((8, d), F32)),
        in_specs=[tile, tile, wide(2 * ff), whole((8, d)), whole((8, d)), ANY_SPEC, ANY_SPEC],
        out_specs=[tile, tile, wide(2 * ff), whole((8, d))],
        scratch_shapes=[pltpu.VMEM(w_fi_t.shape, BF16), pltpu.VMEM(w_fo.shape, BF16)],
        compiler_params=_params(("arbitrary",)),
    )(dx2, x1, ab, mod, prm, w_fi_t, w_fo)


def _backward_mix(dx1, proj, y_a, y_b, mix, uhat, rstd, conv3, sg, ga, gb, mod, prm, taps, w_so, w_co, w_o, tm,
                  partials):
    t, d = dx1.shape
    nt = t // tm
    na = len(partials)
    hs, hc = HALO_SHORT, HALO_CONF
    r3, r31 = 0, 8

    def body(*refs):
        (dx1_ref, proj_ref, halo_ref, ya_ref, yb_ref, mix_ref, uhat_ref, rstd_ref, c3_ref, sg_ref, ga_ref, gb_ref,
         mod_ref, prm_ref, taps_ref, wso_ref, wco_ref, wo_ref) = refs[:18]
        dproj_ref, rhs_ref, sums_ref, dw3_ref, dw31_ref = refs[18 + na:23 + na]
        cv_ext, u0_ext, d3_ext, du1_ext, tmp_ref = refs[23 + 2 * na:28 + 2 * na]
        scatter_start, scatter_finish = _scatter_plan(refs[18:18 + na], refs[23 + na:23 + 2 * na], *refs[28 + 2 * na:])
        i = pl.program_id(0)
        first_tile = i == nt - 1

        @pl.when(i == 0)
        def _():
            scatter_start()
            sums_ref[...] = jnp.zeros((8, d), F32)
            dw3_ref[...] = jnp.zeros(dw3_ref.shape, F32)
            dw31_ref[...] = jnp.zeros(dw31_ref.shape, F32)
            d3_ext[tm:tm + hs, :] = jnp.zeros((hs, d), F32)
            du1_ext[tm:tm + hc, :] = jnp.zeros((hc, d), F32)

        def col(g):
            return proj_ref[:, g * d:(g + 1) * d].astype(F32)

        def hcol(g, rows):
            v = halo_ref[HALO_CONF - rows:HALO_CONF, g * d:(g + 1) * d].astype(F32)
            return jnp.where(first_tile, 0.0, v)

        dx1v = dx1_ref[...]
        mixv = mix_ref[...].astype(F32)
        dmix = (mod_ref[M_G1:M_G1 + 1, :] * dx1v).astype(BF16)
        rhs_ref[2] = dmix
        sums_ref[0:1, :] = sums_ref[0:1, :] + _colsum(dx1v * mixv)
        dmerged = _dot_nt(dmix, wo_ref[...])
        ga = ga_ref[...].astype(F32)
        gb = gb_ref[...].astype(F32)
        yav = ya_ref[...].astype(F32)
        ybv = yb_ref[...].astype(F32)
        dya_f = dmerged * ga
        dyb_f = dmerged * gb
        dya = dya_f.astype(BF16)
        dyb = dyb_f.astype(BF16)
        rhs_ref[0] = dya
        rhs_ref[1] = dyb
        dproj_ref[:, 5 * d:6 * d] = (dya_f * yav * (1.0 - ga)).astype(BF16)
        dproj_ref[:, 6 * d:7 * d] = (dyb_f * ybv * (1.0 - gb)).astype(BF16)

        dya_pre = _dot_nt(dya, wso_ref[...])
        c_s, v_s, b_s = col(1), col(2), col(0)
        cv_ext[0:hs, :] = hcol(1, hs) * hcol(2, hs)
        cv_ext[hs:hs + tm, :] = c_s * v_s
        dproj_ref[:, 0:d] = (dya_pre * c3_ref[...].astype(F32)).astype(BF16)
        d3_ext[0:tm, :] = dya_pre * b_s
        _tap_wgrad(d3_ext, cv_ext, _causal_taps(SHORT_K, hs), tm, hs, dw3_ref)
        _tap_conv(d3_ext, taps_ref, [(r3 + wi, off) for wi, off in _anticausal_taps(SHORT_K)], tm, hs, tmp_ref)
        dcv = tmp_ref[...]
        dproj_ref[:, d:2 * d] = (dcv * v_s).astype(BF16)
        dproj_ref[:, 2 * d:3 * d] = (dcv * c_s).astype(BF16)
        d3_ext[tm:tm + hs, :] = d3_ext[0:hs, :]

        du3 = _dot_nt(dyb, wco_ref[...])
        uhat = uhat_ref[...].astype(F32)
        rstd = rstd_ref[...]
        lng = prm_ref[P_LNG:P_LNG + 1, :]
        u2 = uhat * lng + prm_ref[P_LNB:P_LNB + 1, :]
        s2 = _sigmoid(u2)
        du2 = du3 * (s2 * (1.0 + u2 * (1.0 - s2)))
        sums_ref[1:2, :] = sums_ref[1:2, :] + _colsum(du2 * uhat)
        sums_ref[2:3, :] = sums_ref[2:3, :] + _colsum(du2)
        duhat = du2 * lng
        du1 = rstd * (duhat - _rowmean(duhat) - uhat * _rowmean(duhat * uhat))
        sums_ref[3:4, :] = sums_ref[3:4, :] + _colsum(du1)
        du1_ext[0:tm, :] = du1
        v_c = col(3)
        sg = sg_ref[...].astype(F32)
        u0_ext[0:hc, :] = hcol(3, hc) * _sigmoid(hcol(4, hc))
        u0_ext[hc:hc + tm, :] = v_c * sg
        _tap_wgrad(du1_ext, u0_ext, _causal_taps(CONF_K, hc), tm, hc, dw31_ref)
        _tap_conv(du1_ext, taps_ref, [(r31 + wi, off) for wi, off in _anticausal_taps(CONF_K)], tm, hc, tmp_ref)
        dv_c = tmp_ref[...] * sg
        dproj_ref[:, 3 * d:4 * d] = dv_c.astype(BF16)
        dproj_ref[:, 4 * d:5 * d] = (dv_c * v_c * (1.0 - sg)).astype(BF16)
        du1_ext[tm:tm + hc, :] = du1_ext[0:hc, :]

        @pl.when(i == nt - 1)
        def _():
            scatter_finish()

    rev = lambda i: (nt - 1 - i, 0)
    tile = pl.BlockSpec((tm, d), rev)
    whole = lambda shape: pl.BlockSpec(shape, lambda i: (0,) * len(shape))
    hblocks = tm // HALO_CONF
    halo = pl.BlockSpec((HALO_CONF, 7 * d), lambda i: (jnp.maximum((nt - 1 - i) * hblocks - 1, 0), 0))
    res = pl.pallas_call(
        body, name="backward_mix", grid=(nt,),
        out_shape=(jax.ShapeDtypeStruct((t, 7 * d), BF16), jax.ShapeDtypeStruct((3, t, d), BF16),
                   jax.ShapeDtypeStruct((8, d), F32),
                   jax.ShapeDtypeStruct((SUBLANES * SHORT_K, d), F32),
                   jax.ShapeDtypeStruct((SUBLANES * CONF_K, d), F32))
        + tuple(jax.ShapeDtypeStruct(p.shape, p.dtype) for p in partials),
        in_specs=[tile, pl.BlockSpec((tm, 4 * d), rev), halo, tile, tile, tile, tile, pl.BlockSpec((tm, 1), rev),
                  tile, tile, tile, tile,
                  whole((8, d)), whole((8, d)), whole(taps.shape), whole((d, d)), whole((d, d)), whole((d, d))]
        + [ANY_SPEC] * na,
        out_specs=[pl.BlockSpec((tm, 7 * d), rev), pl.BlockSpec((3, tm, d), lambda i: (0, nt - 1 - i, 0)),
                   whole((8, d)), whole((SUBLANES * SHORT_K, d)), whole((SUBLANES * CONF_K, d))] + [ANY_SPEC] * na,
        scratch_shapes=[pltpu.VMEM((hs + tm, d), F32), pltpu.VMEM((hc + tm, d), F32),
                        pltpu.VMEM((tm + hs, d), F32), pltpu.VMEM((tm + hc, d), F32),
                        pltpu.VMEM((tm, d), F32)] + _comm_scratch(na),
        compiler_params=_params(("arbitrary",)),
    )(dx1, proj, proj, y_a, y_b, mix, uhat, rstd, conv3, sg, ga, gb, mod, prm, taps, w_so, w_co, w_o, *partials)
    return res[:5], res[5:]


def _backward_in(dproj, x, dx1, mod, prm, w_in_g, tm, partials, partial_view, exchanged_shapes):
    t, d = x.shape
    n_all = w_in_g.shape[1]
    na = len(partials)
    nt = t // tm

    def body(*refs):
        dproj_ref, x_ref, dx1_ref, mod_ref, prm_ref, w_hbm = refs[:6]
        gx_ref, sums_ref = refs[6 + na:8 + na]
        w_ref = refs[8 + 2 * na]
        scatter_start, scatter_finish = _scatter_plan(refs[6:6 + na], refs[8 + na:8 + 2 * na], *refs[9 + 2 * na:],
                                                      view=partial_view)

        @pl.when(pl.program_id(0) == 0)
        def _():
            scatter_start()
            pltpu.sync_copy(w_hbm, w_ref)
            sums_ref[...] = jnp.zeros((8, d), F32)

        dh = _dot_nt(dproj_ref[...], w_ref[...])
        xv = x_ref[...]
        r1 = lax.rsqrt(_rowmean(xv * xv) + EPS)
        xn = xv * r1
        gmix = prm_ref[P_GMIX:P_GMIX + 1, :]
        scale = 1.0 + mod_ref[M_SC1:M_SC1 + 1, :]
        dxn = dh * gmix * scale
        gx_ref[...] = dx1_ref[...] + r1 * (dxn - xn * _rowmean(dxn * xn))
        hx = dh * xn
        sums_ref[0:1, :] = sums_ref[0:1, :] + _colsum(dh)
        sums_ref[1:2, :] = sums_ref[1:2, :] + _colsum(hx) * gmix
        sums_ref[2:3, :] = sums_ref[2:3, :] + _colsum(hx) * scale

        @pl.when(pl.program_id(0) == nt - 1)
        def _():
            scatter_finish()

    tile = pl.BlockSpec((tm, d), lambda i: (i, 0))
    whole = pl.BlockSpec((8, d), lambda i: (0, 0))
    res = pl.pallas_call(
        body, name="backward_in", grid=(nt,),
        out_shape=(jax.ShapeDtypeStruct((t, d), F32), jax.ShapeDtypeStruct((8, d), F32))
        + tuple(jax.ShapeDtypeStruct(s, p.dtype) for p, s in zip(partials, exchanged_shapes)),
        in_specs=[pl.BlockSpec((tm, n_all), lambda i: (i, 0)), tile, tile, whole, whole, ANY_SPEC] + [ANY_SPEC] * na,
        out_specs=[tile, whole] + [ANY_SPEC] * na,
        scratch_shapes=[pltpu.VMEM(w_in_g.shape, BF16)] + _comm_scratch(na),
        compiler_params=_params(("arbitrary",)),
    )(dproj, x, dx1, mod, prm, w_in_g, *partials)
    return res[0], res[1], res[2:]


def _weight_grad(a, b, a_spec, b_spec, ns, m, n, nk, name, partials=(), by_columns=False,
                 partial_view=_leading_block, exchanged_shapes=None):
    na = len(partials)
    exchanged_shapes = exchanged_shapes or [p.shape for p in partials]

    def body(*refs):
        a_ref, b_ref = refs[:2]
        o_ref = refs[2 + na]
        acc_ref = refs[3 + 2 * na]
        s, k = pl.program_id(0), pl.program_id(1)
        if na:
            scatter_start, scatter_finish = _scatter_plan(refs[2:2 + na], refs[3 + na:3 + 2 * na], *refs[4 + 2 * na:],
                                                          view=partial_view)

            @pl.when((s == 0) & (k == 0))
            def _():
                scatter_start()

        av = a_ref[0] if len(a_ref.shape) == 3 else a_ref[...]
        bv = b_ref[0] if len(b_ref.shape) == 3 else b_ref[...]
        part = _dot_tn(av, bv)

        @pl.when(k == 0)
        def _():
            acc_ref[...] = part

        @pl.when(k > 0)
        def _():
            acc_ref[...] = acc_ref[...] + part

        @pl.when(k == nk - 1)
        def _():
            if by_columns:
                o_ref[...] = acc_ref[...].astype(BF16)
            else:
                o_ref[0] = acc_ref[...].astype(BF16)

        if na:
            @pl.when((s == ns - 1) & (k == nk - 1))
            def _():
                scatter_finish()

    if by_columns:
        out_shape, out_spec = (m, ns * n), pl.BlockSpec((m, n), lambda s, k: (0, s))
    else:
        out_shape, out_spec = (ns, m, n), pl.BlockSpec((1, m, n), lambda s, k: (s, 0, 0))
    res = pl.pallas_call(
        body, name=name, grid=(ns, nk),
        out_shape=(jax.ShapeDtypeStruct(out_shape, BF16),)
        + tuple(jax.ShapeDtypeStruct(sh, p.dtype) for p, sh in zip(partials, exchanged_shapes)),
        in_specs=[a_spec, b_spec] + [ANY_SPEC] * na,
        out_specs=[out_spec] + [ANY_SPEC] * na,
        scratch_shapes=[pltpu.VMEM((m, n), F32)] + (_comm_scratch(na) if na else []),
        compiler_params=_params(("arbitrary", "arbitrary")),
    )(a, b, *partials)
    return (res[0], res[1:]) if na else res[0]


def _adamw(w, g, m, v):
    m = ADAM_B1 * m + (1.0 - ADAM_B1) * g
    v = ADAM_B2 * v + (1.0 - ADAM_B2) * (g * g)
    m_hat = m / (1.0 - ADAM_B1 ** ADAM_STEP)
    v_hat = v / (1.0 - ADAM_B2 ** ADAM_STEP)
    delta = -ADAM_LR * (m_hat / (jnp.sqrt(v_hat) + ADAM_EPS) + ADAM_WD * w)
    return delta, m, v


def _adamw_shard(parts, w, m, v, tr, name):
    r, c = w.shape

    def body(p_ref, w_ref, m_ref, v_ref, g_ref, d_ref, nm_ref, nv_ref):
        g = p_ref[0].astype(F32)
        for s in range(1, N_DEV):
            g = g + p_ref[s].astype(F32)
        delta, nm, nv = _adamw(w_ref[...], g, m_ref[...], v_ref[...])
        g_ref[...] = g
        d_ref[...] = delta
        nm_ref[...] = nm
        nv_ref[...] = nv

    tile = pl.BlockSpec((tr, c), lambda i: (i, 0))
    return pl.pallas_call(
        body, name=name, grid=(r // tr,),
        out_shape=(jax.ShapeDtypeStruct((r, c), F32),) * 4,
        in_specs=[pl.BlockSpec((N_DEV, tr, c), lambda i: (0, i, 0)), tile, tile, tile],
        out_specs=[tile] * 4,
        compiler_params=_params(("arbitrary",)),
    )(parts, w, m, v)


def _adamw_members(parts, wmv, name):
    nm = len(wmv)
    r, c = wmv[0][0].shape

    def body(*refs):
        p_ref = refs[0]
        for j in range(nm):
            w_ref, m_ref, v_ref = refs[1 + 3 * j:4 + 3 * j]
            g = p_ref[0, j].astype(F32)
            for s in range(1, N_DEV):
                g = g + p_ref[s, j].astype(F32)
            delta, new_m, new_v = _adamw(w_ref[...], g, m_ref[...], v_ref[...])
            for o_ref, val in zip(refs[1 + 3 * nm + 4 * j:5 + 3 * nm + 4 * j], (g, delta, new_m, new_v)):
                o_ref[...] = val

    vm = pl.BlockSpec(memory_space=pltpu.VMEM)
    res = pl.pallas_call(
        body, name=name,
        out_shape=(jax.ShapeDtypeStruct((r, c), F32),) * (4 * nm),
        in_specs=[vm] * (1 + 3 * nm), out_specs=[vm] * (4 * nm),
        compiler_params=_params(),
    )(parts, *[a for triple in wmv for a in triple])
    return [tuple(res[4 * j:4 * j + 4]) for j in range(nm)]


def _ada_update(sc_all, dmod_cols, w, m, v, tr):
    d, n = w.shape

    def body(sc_ref, dm_ref, w_ref, m_ref, v_ref, g_ref, d_ref, nm_ref, nv_ref):
        g = lax.dot_general(sc_ref[...], dm_ref[...], (((0,), (0,)), ((), ())),
                            preferred_element_type=F32, precision=lax.Precision.HIGHEST)
        delta, nm, nv = _adamw(w_ref[...], g, m_ref[...], v_ref[...])
        g_ref[...] = g
        d_ref[...] = delta
        nm_ref[...] = nm
        nv_ref[...] = nv

    tile = pl.BlockSpec((tr, n), lambda i: (i, 0))
    return pl.pallas_call(
        body, name="ada_update", grid=(d // tr,),
        out_shape=(jax.ShapeDtypeStruct((d, n), F32),) * 4,
        in_specs=[pl.BlockSpec((N_DEV, tr), lambda i: (0, i)), pl.BlockSpec((N_DEV, n), lambda i: (0, 0)),
                  tile, tile, tile],
        out_specs=[tile] * 4,
        compiler_params=_params(("arbitrary",)),
    )(sc_all, dmod_cols, w, m, v)


def _small_exchange(vec, cg):
    l = vec.shape[2]
    rows = cg.shape[1]

    def body(vec_ref, cg_ref, vall_ref, cgr_ref, send_sems, recv_sems):
        x, y, c = _my_coords()
        me = _slot(x, y, c)
        vall_ref[me] = vec_ref[0]
        cgr_ref[me] = cg_ref[me]
        copies = []
        for k in range(1, N_DEV):
            peer = (_flip(x, k & 4), _flip(y, k & 2), _flip(c, k & 1))
            copies.append(pltpu.make_async_remote_copy(
                src_ref=vall_ref.at[me], dst_ref=vall_ref.at[me], send_sem=send_sems.at[k - 1],
                recv_sem=recv_sems.at[k - 1], device_id=peer, device_id_type=MESH))
            copies.append(pltpu.make_async_remote_copy(
                src_ref=cg_ref.at[_slot(*peer)], dst_ref=cgr_ref.at[me], send_sem=send_sems.at[7 + k - 1],
                recv_sem=recv_sems.at[7 + k - 1], device_id=peer, device_id_type=MESH))
        for cp in copies:
            cp.start()
        for cp in copies:
            cp.wait_recv()
        for cp in copies:
            cp.wait_send()

    vm = pl.BlockSpec(memory_space=pltpu.VMEM)
    return pl.pallas_call(
        body, name="small_exchange",
        out_shape=(jax.ShapeDtypeStruct((N_DEV, 1, l), F32), jax.ShapeDtypeStruct((N_DEV, rows, LANES), F32)),
        in_specs=[vm, vm], out_specs=[vm, vm],
        scratch_shapes=[pltpu.SemaphoreType.DMA((14,)), pltpu.SemaphoreType.DMA((14,))],
        compiler_params=_params(),
    )(vec, cg)


def _small_update(vall, cgr, smalls):
    ns = len(smalls)

    def body(*refs):
        vall_ref, cgr_ref = refs[0], refs[1]
        wmv = refs[2:2 + 3 * ns]
        outs = refs[2 + 3 * ns:]
        for p, (_, _, _, lo, hi, kind) in enumerate(smalls):
            w_ref, m_ref, v_ref = wmv[3 * p:3 * p + 3]
            part = (lambda s: vall_ref[s, :, lo:hi]) if kind == "vec" else (lambda s: cgr_ref[s, lo:hi, :])
            full = (lambda ref: ref[...]) if kind == "vec" else (lambda ref: ref[0])
            g = part(0)
            for s in range(1, N_DEV):
                g = g + part(s)
            delta, nm, nv = _adamw(full(w_ref), g, full(m_ref), full(v_ref))
            for o_ref, val in zip(outs[4 * p:4 * p + 4], (g, delta, nm, nv)):
                if kind == "vec":
                    o_ref[...] = val
                else:
                    o_ref[0] = val

    vm = pl.BlockSpec(memory_space=pltpu.VMEM)
    args = [vall, cgr]
    out_shape = []
    for w, m, v, _, _, _ in smalls:
        args += [w, m, v]
        out_shape += [jax.ShapeDtypeStruct(w.shape, F32)] * 4
    res = pl.pallas_call(
        body, name="small_update",
        out_shape=tuple(out_shape),
        in_specs=[vm] * len(args), out_specs=[vm] * len(out_shape),
        compiler_params=_params(),
    )(*args)
    return [res[4 * p:4 + 4 * p] for p in range(ns)]


def _pick(t, want):
    return want if t % want == 0 else t


def kernel(x, c, w_ada, b_ada, norm_mix_g, w_in, conv_short_w, w_short_out, conv_conf_w, conv_conf_b, conf_ln_g, conf_ln_b, w_conf_out, w_o, norm_ffn_g, w_ffn_in, w_ffn_out, final_norm_g, loss_target, m_w_ada, m_b_ada, m_norm_mix_g, m_w_in, m_conv_short_w, m_w_short_out, m_conv_conf_w, m_conv_conf_b, m_conf_ln_g, m_conf_ln_b, m_w_conf_out, m_w_o, m_norm_ffn_g, m_w_ffn_in, m_w_ffn_out, m_final_norm_g, v_w_ada, v_b_ada, v_norm_mix_g, v_w_in, v_conv_short_w, v_w_short_out, v_conv_conf_w, v_conv_conf_b, v_conf_ln_g, v_conf_ln_b, v_w_conf_out, v_w_o, v_norm_ffn_g, v_w_ffn_in, v_w_ffn_out, v_final_norm_g):
    t, d = x.shape[1], x.shape[2]
    x2 = x.reshape(t, d)
    tgt = loss_target.reshape(t, d)
    me = _slot(*_my_coords())
    tm = _pick(t, 256)
    tm_fwd_in = _pick(t, 512)
    tk = _pick(t, 2048)

    taps_loc = jnp.zeros((40, LANES), F32)
    taps_loc = taps_loc.at[0:SHORT_K].set(conv_short_w[0]).at[8:8 + CONF_K].set(conv_conf_w[0])
    prm = jnp.concatenate([norm_mix_g, norm_ffn_g, final_norm_g.reshape(1, d), conv_conf_b, conf_ln_g, conf_ln_b,
                           jnp.zeros((2, d), F32)], axis=0)

    n_in = w_in.shape[2]
    nk = t // tk
    tok = pl.BlockSpec((tk, d), lambda s, k: (k, 0))
    rows = d // N_DEV
    frows = w_ffn_out.shape[1]

    proj, h, w_in_g, mod, sc_all3, taps, (w_so_g, w_co_g, w_o_g) = _forward_in(
        jnp.reshape(me, (1,)).astype(jnp.int32), x2, prm, w_in[0].astype(BF16),
        (c, w_ada[0], b_ada, taps_loc), tm_fwd_in,
        [w_short_out[0].astype(BF16), w_conf_out[0].astype(BF16), w_o[0].astype(BF16)])
    w_so = w_so_g.reshape(d, d)
    w_co = w_co_g.reshape(d, d)
    w_oo = w_o_g.reshape(d, d)
    w_fi_t, m_fi_t, v_fi_t = w_ffn_in[0].T, m_w_ffn_in[0].T, v_w_ffn_in[0].T
    (x1, y_a, y_b, mix, uhat, conv3, sg, ga, gb, mixer_lhs, rstd), (w_fi_g, w_fo_g) = _forward_mix(
        x2, proj, mod, prm, taps, w_so, w_co, w_oo, tm, [w_fi_t.astype(BF16), w_ffn_out[0].astype(BF16)])
    fb = w_fi_g.shape[1]
    ff = N_DEV * frows
    w_fi_all = w_fi_g.reshape(N_DEV * fb, d)
    w_fo_all = w_fo_g.reshape(ff, d)
    dx2, ab, h2, act, sums_f = _forward_ffn(x1, tgt, mod, prm, w_fi_all, w_fo_all, _pick(t, 512))

    dx1, df, dab, sums_b = _backward_ffn(dx2, x1, ab, mod, prm, w_fi_all, w_fo_all, tm)
    fb2 = 2 * fb
    pair_tok = pl.BlockSpec((tk, fb2), lambda s, k: (k, s))
    g_fi = _weight_grad(dab, h2, pair_tok, tok, N_DEV // 2, fb2, d, nk, "grad_w_ffn_in")
    g_fo = _weight_grad(act, df, pair_tok, tok, N_DEV // 4, fb2, d, nk, "grad_w_ffn_out")
    (dproj, mixer_rhs, sums_m, dw3p, dw31p), (p_fi, p_fo) = _backward_mix(
        dx1, proj, y_a, y_b, mix, uhat, rstd, conv3, sg, ga, gb, mod, prm, taps, w_so, w_co, w_oo, tm,
        [g_fi.reshape(N_DEV, fb, d), g_fo.reshape(N_DEV, frows, d)])
    tk_in = _pick(t, 4096)
    nk_in = t // tk_in
    tok_in = pl.BlockSpec((tk_in, d), lambda s, k: (k, 0))
    member_tok = pl.BlockSpec((1, tk, d), lambda s, k: (s, k, 0))
    g_mixers = _weight_grad(mixer_lhs, mixer_rhs, member_tok, member_tok, 3, d, d, nk, "grad_w_mixers")
    n_blk = _pick(N_DEV * n_in, d)
    g_in, (p_mixers,) = _weight_grad(
        h, dproj, tok_in, pl.BlockSpec((tk_in, n_blk), lambda s, k: (k, s)),
        N_DEV * n_in // n_blk, d, n_blk, nk_in, "grad_w_in",
        [g_mixers.reshape(3, N_DEV, rows, d)], by_columns=True,
        partial_view=lambda ref, dev: ref.at[:, dev], exchanged_shapes=[(N_DEV, 3, rows, d)])
    grad_x, sums_i, (p_in,) = _backward_in(dproj, x2, dx1, mod, prm, w_in_g, tm, [g_in], _column_block(n_in),
                                           [(N_DEV, d, n_in)])

    up_in = _adamw_shard(p_in, w_in[0], m_w_in[0], v_w_in[0], _pick(d, 256), "adamw_w_in")
    up_so, up_co, up_oo = _adamw_members(
        p_mixers, [(w_short_out[0], m_w_short_out[0], v_w_short_out[0]),
                   (w_conf_out[0], m_w_conf_out[0], v_w_conf_out[0]), (w_o[0], m_w_o[0], v_w_o[0])], "adamw_w_mixers")
    up_fi = tuple(a.T for a in _adamw_shard(p_fi, w_fi_t, m_fi_t, v_fi_t, fb // 2, "adamw_w_ffn_in"))
    up_fo = _adamw_shard(p_fo, w_ffn_out[0], m_w_ffn_out[0], v_w_ffn_out[0], frows, "adamw_w_ffn_out")

    vec = jnp.concatenate([sums_i[0:2], sums_m[0:1], sums_b[0:2], sums_f[1:2],
                           sums_i[2:3], sums_m[3:4], sums_m[1:3], sums_b[2:3], sums_f[0:1],
                           sums_f[2:3]], axis=0)
    vec = vec.reshape(1, 1, 13 * d)
    dw3 = dw3p.reshape(SHORT_K, SUBLANES, d).sum(axis=1)
    dw31 = dw31p.reshape(CONF_K, SUBLANES, d).sum(axis=1)
    cg = jnp.zeros((40, d), F32).at[0:SHORT_K].set(dw3).at[8:8 + CONF_K].set(dw31)
    cg = cg.reshape(40, N_DEV, LANES).transpose(1, 0, 2)
    fin = lambda a: a.reshape(1, d)
    smalls = [
        (b_ada, m_b_ada, v_b_ada, 0, 6 * d, "vec"),
        (norm_mix_g, m_norm_mix_g, v_norm_mix_g, 6 * d, 7 * d, "vec"),
        (conv_short_w, m_conv_short_w, v_conv_short_w, 0, SHORT_K, "cg"),
        (conv_conf_w, m_conv_conf_w, v_conv_conf_w, 8, 8 + CONF_K, "cg"),
        (conv_conf_b, m_conv_conf_b, v_conv_conf_b, 7 * d, 8 * d, "vec"),
        (conf_ln_g, m_conf_ln_g, v_conf_ln_g, 8 * d, 9 * d, "vec"),
        (conf_ln_b, m_conf_ln_b, v_conf_ln_b, 9 * d, 10 * d, "vec"),
        (norm_ffn_g, m_norm_ffn_g, v_norm_ffn_g, 10 * d, 11 * d, "vec"),
        (fin(final_norm_g), fin(m_final_norm_g), fin(v_final_norm_g), 11 * d, 12 * d, "vec"),
    ]
    vall, cgr = _small_exchange(vec, cg)
    up_small = _small_update(vall, cgr, smalls)
    n_ada = w_ada.shape[2]
    dmod_all = vall.reshape(N_DEV, 13 * d)[:, 0:6 * d]
    dmod_cols = lax.dynamic_slice(dmod_all, (0, me * n_ada), (N_DEV, n_ada))
    up_ada = _ada_update(sc_all3.reshape(N_DEV, d), dmod_cols, w_ada[0], m_w_ada[0], v_w_ada[0], _pick(d, 256))

    loss = jnp.sum(vall.reshape(N_DEV, 13 * d)[:, 12 * d:])

    lead = lambda a: a.reshape((1,) + a.shape)
    ups = [tuple(lead(a) for a in up_ada), up_small[0], up_small[1], tuple(lead(a) for a in up_in),
           up_small[2], tuple(lead(a) for a in up_so), up_small[3],
           up_small[4], up_small[5], up_small[6],
           tuple(lead(a) for a in up_co), tuple(lead(a) for a in up_oo), up_small[7],
           tuple(lead(a) for a in up_fi), tuple(lead(a) for a in up_fo),
           tuple(a.reshape(d) for a in up_small[8])]
    grads = [u[0] for u in ups]
    deltas = [u[1] for u in ups]
    new_m = [u[2] for u in ups]
    new_v = [u[3] for u in ups]
    return (loss, grad_x.reshape(1, t, d), *grads, *deltas, *new_m, *new_v)
```

```python
import functools

import jax
import jax.numpy as jnp
from jax import lax
from jax.experimental import pallas as pl
from jax.experimental.pallas import tpu as pltpu

F32 = jnp.float32
BF16 = jnp.bfloat16
MESH = pl.DeviceIdType.MESH

N_DEV = 8
EPS = 1e-6
LN_EPS = 1e-5
SHORT_K = 3
CONF_K = 31
ADAM_LR = 0.001
ADAM_B1 = 0.9
ADAM_B2 = 0.999
ADAM_EPS = 1e-08
ADAM_WD = 0.01
ADAM_STEP = 10

LANES = 128
SUBLANES = 8
CONV_ROWS = 64
HALO_SHORT = 8
HALO_CONF = 32
VMEM_LIMIT = 56 * 1024 * 1024

M_SH1, M_SC1, M_G1, M_SH2, M_SC2, M_G2 = range(6)
P_GMIX, P_GFFN, P_GFIN, P_CBIAS, P_LNG, P_LNB = range(6)


def _params(sem=None, **kw):
    return pltpu.CompilerParams(dimension_semantics=sem, vmem_limit_bytes=VMEM_LIMIT, **kw)


def _sigmoid(v):
    return jax.nn.sigmoid(v)


def _dot(a, b):
    return jnp.dot(a, b, preferred_element_type=F32)


def _dot_nt(a, b):
    return lax.dot_general(a, b, (((1,), (1,)), ((), ())), preferred_element_type=F32)


def _dot_tn(a, b):
    return lax.dot_general(a, b, (((0,), (0,)), ((), ())), preferred_element_type=F32)


def _colsum(v):
    return jnp.sum(v, axis=0, keepdims=True)


def _rowmean(v):
    return jnp.mean(v, axis=-1, keepdims=True)


def _my_coords():
    return lax.axis_index("x"), lax.axis_index("y"), lax.axis_index("c")


def _slot(px, py, pc):
    return 4 * px + 2 * py + pc


def _flip(v, bit):
    return 1 - v if bit else v


def _taps_by_residue(taps):
    by_res = {}
    for wi, off in taps:
        by_res.setdefault(off % SUBLANES, []).append((wi, off // SUBLANES))
    return sorted(by_res.items())


def _tap_conv(ext_ref, w_ref, taps, tm, extra, out_ref):
    d = out_ref.shape[1]
    rb = min(CONV_ROWS, tm)
    wrows = rb + extra
    groups = _taps_by_residue(taps)

    def block(i, carry):
        base = pl.multiple_of(i * rb, SUBLANES)
        for lc in range(d // LANES):
            ls = pl.ds(lc * LANES, LANES)
            win = ext_ref[pl.ds(base, wrows), ls]
            acc = None
            for r, lst in groups:
                sh = win if r == 0 else pltpu.roll(win, wrows - r, 0)
                for wi, q in lst:
                    term = w_ref[wi:wi + 1, ls] * sh[SUBLANES * q:SUBLANES * q + rb, :]
                    acc = term if acc is None else acc + term
            out_ref[pl.ds(base, rb), ls] = acc
        return carry

    lax.fori_loop(0, tm // rb, block, 0)


def _tap_wgrad(a_ref, ext_ref, taps, tm, extra, acc_ref):
    d = a_ref.shape[1]
    rb = min(CONV_ROWS, tm)
    wrows = rb + extra
    groups = _taps_by_residue(taps)

    def block(i, carry):
        base = pl.multiple_of(i * rb, SUBLANES)
        for lc in range(d // LANES):
            ls = pl.ds(lc * LANES, LANES)
            a_blk = a_ref[pl.ds(base, rb), ls]
            win = ext_ref[pl.ds(base, wrows), ls]
            for r, lst in groups:
                sh = win if r == 0 else pltpu.roll(win, wrows - r, 0)
                for wi, q in lst:
                    prod = a_blk * sh[SUBLANES * q:SUBLANES * q + rb, :]
                    part = prod[0:SUBLANES, :]
                    for s in range(1, rb // SUBLANES):
                        part = part + prod[SUBLANES * s:SUBLANES * (s + 1), :]
                    rows = pl.ds(SUBLANES * wi, SUBLANES)
                    acc_ref[rows, ls] = acc_ref[rows, ls] + part
        return carry

    lax.fori_loop(0, tm // rb, block, 0)


def _causal_taps(k, halo):
    return [(i, halo - (k - 1) + i) for i in range(k)]


def _anticausal_taps(k):
    return [(i, (k - 1) - i) for i in range(k)]


def _ada_scratch(d, nloc, trows):
    return [pltpu.VMEM((N_DEV, 1, d), F32), pltpu.VMEM((N_DEV, 1, nloc), F32), pltpu.VMEM((N_DEV, 1, nloc), F32),
            pltpu.VMEM((N_DEV, trows, LANES), F32), pltpu.SemaphoreType.DMA((21,)), pltpu.SemaphoreType.DMA((21,))]


def _ada_exchange(c_ref, w_ref, b_ref, t_ref, mod_ref, sc_ref, taps_ref,
                  scall_ref, part_ref, modrecv_ref, tapsall_ref, send_sems, recv_sems):
    x, y, cc = _my_coords()
    me = _slot(x, y, cc)
    d = c_ref.shape[1]
    cv = c_ref[...]
    scall_ref[me] = cv * _sigmoid(cv)
    tapsall_ref[me] = t_ref[...]

    def peer_of(k):
        return (_flip(x, k & 4), _flip(y, k & 2), _flip(cc, k & 1))

    def gather_copy(ref, base, k):
        return pltpu.make_async_remote_copy(
            src_ref=ref.at[me], dst_ref=ref.at[me], send_sem=send_sems.at[base + k - 1],
            recv_sem=recv_sems.at[base + k - 1], device_id=peer_of(k), device_id_type=MESH)

    first = [gather_copy(scall_ref, 0, k) for k in range(1, N_DEV)]
    first += [gather_copy(tapsall_ref, 7, k) for k in range(1, N_DEV)]
    for cp in first:
        cp.start()
    for cp in first[:7]:
        cp.wait_recv()
    sc_all = jnp.concatenate([scall_ref[s] for s in range(N_DEV)], axis=0)
    for s in range(N_DEV):
        sc_ref[s] = scall_ref[s]
    part = jnp.dot(sc_all, w_ref[...], preferred_element_type=F32,
                   precision=lax.Precision.HIGHEST)
    for b in range(N_DEV):
        part_ref[b] = part[b:b + 1, :]
    modrecv_ref[me] = part_ref[me]
    second = []
    for k in range(1, N_DEV):
        px, py, pc = peer_of(k)
        second.append(pltpu.make_async_remote_copy(
            src_ref=part_ref.at[_slot(px, py, pc)], dst_ref=modrecv_ref.at[me],
            send_sem=send_sems.at[14 + k - 1], recv_sem=recv_sems.at[14 + k - 1],
            device_id=(px, py, pc), device_id_type=MESH))
    for cp in second:
        cp.start()
    for cp in second:
        cp.wait_recv()
    mod = jnp.concatenate([modrecv_ref[s] for s in range(N_DEV)], axis=1) + b_ref[...]
    for r in range(6):
        mod_ref[r:r + 1, :] = mod[:, r * d:(r + 1) * d]
    mod_ref[6:8, :] = jnp.zeros((2, d), F32)
    for cp in first[7:]:
        cp.wait_recv()
    taps_ref[...] = jnp.concatenate([tapsall_ref[s] for s in range(N_DEV)], axis=1)
    for cp in first + second:
        cp.wait_send()


ANY_SPEC = pl.BlockSpec(memory_space=pl.ANY)


def _comm_scratch(na):
    return [pltpu.SemaphoreType.DMA((7 * na,)), pltpu.SemaphoreType.DMA((7 * na,)), pltpu.SemaphoreType.DMA((na,))]


def _leading_block(ref, slot):
    return ref.at[slot]


def _column_block(width):
    def view(ref, slot):
        return ref.at[:, pl.ds(pl.multiple_of(slot * width, LANES), width)]
    return view


class _Gather:
    def __init__(self, ins, outs, send_sems, recv_sems, local_sems, view=_leading_block):
        self.na = len(ins)
        x, y, c = _my_coords()
        self.c = c
        self.view = view
        self.me, self.sibling = (x, y, c), (x, y, 1 - c)
        self.chips = [(1 - x, y), (x, 1 - y), (1 - x, 1 - y)]
        self.outs, self.send_sems, self.recv_sems = outs, send_sems, recv_sems
        self.mine = [pltpu.make_async_copy(ins[a], view(outs[a], _slot(*self.me)), local_sems.at[a])
                     for a in range(self.na)]
        self.first = []
        for a in range(self.na):
            self.first.append(self._copy(a, 0, self.me, self.sibling, src=ins[a]))
            self.first += [self._copy(a, 1 + j, self.me, (*chip, c), src=ins[a]) for j, chip in enumerate(self.chips)]
        self.passed = [self._copy(a, 4 + j, (*chip, c), self.sibling)
                       for a in range(self.na) for j, chip in enumerate(self.chips)]

    def _copy(self, a, k, block, to, src=None):
        dst = self.view(self.outs[a], _slot(*block))
        return pltpu.make_async_remote_copy(
            src_ref=dst if src is None else src, dst_ref=dst,
            send_sem=self.send_sems.at[7 * a + k], recv_sem=self.recv_sems.at[7 * a + k],
            device_id=to, device_id_type=MESH)

    def start(self):
        self.start_near()
        self.start_far()

    def start_near(self):
        self.start_local()
        self.start_one_hop()

    def start_local(self):
        for cp in self.mine + self.first[0::4]:
            cp.start()

    def start_one_hop(self):
        for cp in self.first[1::4] + self.first[2::4]:
            cp.start()

    def start_far(self):
        for cp in self.first[3::4]:
            cp.start()

    def forward_chip(self, j):
        for a in range(self.na):
            self._copy(a, 1 + j, (*self.chips[j], self.c), self.me).wait_recv()
            self.passed[3 * a + j].start()

    def wait_sibling_own(self):
        for a in range(self.na):
            self._copy(a, 0, self.sibling, self.me).wait_recv()

    def wait_sibling_passed(self, j):
        for a in range(self.na):
            self._copy(a, 4 + j, (*self.chips[j], 1 - self.c), self.me).wait_recv()

    def finish_sends(self):
        for cp in self.first + self.passed:
            cp.wait_send()
        for cp in self.mine:
            cp.wait()

    def forward(self):
        for j in range(3):
            self.forward_chip(j)

    def finish(self):
        self.wait_sibling_own()
        for j in range(3):
            self.wait_sibling_passed(j)
        self.finish_sends()


def _scatter_plan(ins, outs, send_sems, recv_sems, local_sems, view=_leading_block):
    na = len(ins)
    x, y, c = _my_coords()
    me = _slot(x, y, c)
    mine = [pltpu.make_async_copy(view(ins[a], me), outs[a].at[me], local_sems.at[a]) for a in range(na)]
    copies = []
    for k in range(1, N_DEV):
        peer = (_flip(x, k & 4), _flip(y, k & 2), _flip(c, k & 1))
        for a in range(na):
            copies.append(pltpu.make_async_remote_copy(
                src_ref=view(ins[a], _slot(*peer)), dst_ref=outs[a].at[me],
                send_sem=send_sems.at[7 * a + k - 1], recv_sem=recv_sems.at[7 * a + k - 1],
                device_id=peer, device_id_type=MESH))

    def start():
        for cp in mine + copies:
            cp.start()

    def finish():
        for cp in copies:
            cp.wait_recv()
        for cp in copies:
            cp.wait_send()
        for cp in mine:
            cp.wait()

    return start, finish


def _forward_in(me_arr, x, prm, w_in_loc, ada, tm, shards):
    t, d = x.shape
    n = w_in_loc.shape[1]
    ns = N_DEV // 2
    na = len(shards)
    nt = t // tm
    nloc = ada[1].shape[1]
    trows = ada[3].shape[0]
    chip_of_pass = {1: 1, 2: 0, 3: 2}

    def body(*refs):
        x_ref, prm_ref, wloc_ref = refs[1:4]
        ada_in = refs[4:8]
        shard_refs = refs[8:8 + na]
        proj_ref, h_ref, wg_ref, mod_ref, sc_ref, taps_ref = refs[8 + na:14 + na]
        gathered_refs = refs[14 + na:14 + 2 * na]
        hall_ref, wv_ref, wv_sem = refs[14 + 2 * na:17 + 2 * na]
        ada_scratch = refs[17 + 2 * na:23 + 2 * na]
        sems = refs[23 + 2 * na:]
        s, i = pl.program_id(0), pl.program_id(1)
        columns = _column_block(n)

        def gathers():
            return (_Gather([wloc_ref], [wg_ref], *sems[0:3], view=columns),
                    _Gather(shard_refs, gathered_refs, *sems[3:6]))

        def shard_copy(dev, p, other_core, src=None):
            half = pl.ds(pl.multiple_of((dev & 1) * n, LANES), n)
            return pltpu.make_async_copy(columns(wg_ref, dev) if src is None else src,
                                         wv_ref.at[p % 2, :, half], wv_sem.at[2 * (p % 2) + other_core])

        def pair_copies(p, me):
            return [shard_copy(me ^ (2 * p), p, 0), shard_copy(me ^ (2 * p) ^ 1, p, 1)]

        @pl.when(i == 0)
        def _():
            me = _slot(*_my_coords())

            @pl.when(s == 0)
            def _():
                g_in, _ = gathers()
                g_in.start_local()
                _ada_exchange(*ada_in, mod_ref, sc_ref, taps_ref, *ada_scratch)
                g_in.start_one_hop()
                own = shard_copy(me, 0, 0, src=wloc_ref)
                own.start()
                g_in.wait_sibling_own()
                sib = shard_copy(me ^ 1, 0, 1)
                sib.start()
                own.wait()
                sib.wait()

            for p in range(1, ns):
                @pl.when(s == p)
                def _(p=p):
                    for cp in pair_copies(p, me):
                        cp.wait()

        @pl.when(i == max(nt - 3, 0))
        def _():
            g_in, g_rest = gathers()
            for p in range(1, ns):
                @pl.when(s == p - 1)
                def _(p=p):
                    g_in.forward_chip(chip_of_pass[p])
                    if p == 1:
                        g_in.start_far()
                    if p == 2:
                        g_rest.start()
                    if p == 3:
                        g_rest.forward()

        @pl.when(i == nt - 1)
        def _():
            g_in, _ = gathers()
            me = _slot(*_my_coords())
            for p in range(1, ns):
                @pl.when(s == p - 1)
                def _(p=p):
                    g_in.wait_sibling_passed(chip_of_pass[p])
                    for cp in pair_copies(p, me):
                        cp.start()

        @pl.when(s == 0)
        def _():
            xv = x_ref[...]
            r = lax.rsqrt(_rowmean(xv * xv) + EPS)
            h = xv * r * prm_ref[P_GMIX:P_GMIX + 1, :] * (1.0 + mod_ref[M_SC1:M_SC1 + 1, :]) \
                + mod_ref[M_SH1:M_SH1 + 1, :]
            hb = h.astype(BF16)
            hall_ref[i] = hb
            h_ref[...] = hb

        proj_ref[...] = _dot(hall_ref[i], wv_ref[s % 2]).astype(BF16)

        @pl.when((s == ns - 1) & (i == nt - 1))
        def _():
            g_in, g_rest = gathers()
            g_in.finish_sends()
            g_rest.finish()

    x_tile = pl.BlockSpec((tm, d), lambda s, i, me: (jnp.where(s == 0, i, nt - 1), 0))
    whole = lambda a: pl.BlockSpec(a.shape, lambda s, i, me: (0,) * len(a.shape))
    small_out = (jax.ShapeDtypeStruct((8, d), F32), jax.ShapeDtypeStruct((N_DEV, 1, d), F32),
                 jax.ShapeDtypeStruct((trows, N_DEV * LANES), F32))
    res = pl.pallas_call(
        body, name="forward_in",
        grid_spec=pltpu.PrefetchScalarGridSpec(
            num_scalar_prefetch=1, grid=(ns, nt),
            in_specs=[x_tile, whole(prm), ANY_SPEC] + [whole(a) for a in ada] + [ANY_SPEC] * na,
            out_specs=[pl.BlockSpec((tm, 2 * n), lambda s, i, me: (i, (me[0] // 2) ^ s)), x_tile, ANY_SPEC]
            + [whole(a) for a in small_out] + [ANY_SPEC] * na,
            scratch_shapes=[pltpu.VMEM((nt, tm, d), BF16), pltpu.VMEM((2, d, 2 * n), BF16),
                            pltpu.SemaphoreType.DMA((4,))]
            + _ada_scratch(d, nloc, trows) + _comm_scratch(1) + _comm_scratch(na)),
        out_shape=(jax.ShapeDtypeStruct((t, N_DEV * n), BF16), jax.ShapeDtypeStruct((t, d), BF16),
                   jax.ShapeDtypeStruct((d, N_DEV * n), BF16)) + small_out
        + tuple(jax.ShapeDtypeStruct((N_DEV,) + a.shape, a.dtype) for a in shards),
        compiler_params=_params(("arbitrary", "arbitrary")),
    )(me_arr, x, prm, w_in_loc, *ada, *shards)
    return res[0], res[1], res[2], res[3], res[4], res[5], res[6:]


def _forward_mix(x, proj, mod, prm, taps, w_so, w_co, w_o, tm, shards):
    t, d = x.shape
    nt = t // tm
    na = len(shards)
    hs, hc = HALO_SHORT, HALO_CONF
    r3, r31 = 0, 8

    def body(*refs):
        x_ref, proj_ref, mod_ref, prm_ref, taps_ref, wso_ref, wco_ref, wo_ref = refs[:8]
        (x1_ref, ya_ref, yb_ref, mix_ref, uhat_ref, c3_ref, sg_ref, ga_ref, gb_ref, lhs_ref,
         rstd_ref) = refs[8 + na:19 + na]
        cv_ext, u0_ext, conv3_ref, u1f_ref = refs[19 + 2 * na:23 + 2 * na]
        gather = _Gather(refs[8:8 + na], refs[19 + na:19 + 2 * na], *refs[23 + 2 * na:])
        i = pl.program_id(0)

        @pl.when(i == 0)
        def _():
            gather.start()
            cv_ext[0:hs, :] = jnp.zeros((hs, d), F32)
            u0_ext[0:hc, :] = jnp.zeros((hc, d), F32)

        @pl.when(i == (3 * nt) // 4)
        def _():
            gather.forward()

        def col(g):
            return proj_ref[:, g * d:(g + 1) * d].astype(F32)

        cv_ext[hs:hs + tm, :] = col(1) * col(2)
        sg = _sigmoid(col(4))
        sg_ref[...] = sg.astype(BF16)
        u0_ext[hc:hc + tm, :] = col(3) * sg
        _tap_conv(cv_ext, taps_ref, [(r3 + wi, off) for wi, off in _causal_taps(SHORT_K, hs)], tm, hs, conv3_ref)
        _tap_conv(u0_ext, taps_ref, [(r31 + wi, off) for wi, off in _causal_taps(CONF_K, hc)], tm, hc, u1f_ref)
        cv_ext[0:hs, :] = cv_ext[tm:tm + hs, :]
        u0_ext[0:hc, :] = u0_ext[tm:tm + hc, :]

        c3_ref[...] = conv3_ref[...].astype(BF16)
        ya_pre = (col(0) * c3_ref[...].astype(F32)).astype(BF16)
        lhs_ref[0] = ya_pre
        y_a = _dot(ya_pre, wso_ref[...])
        u1 = u1f_ref[...] + prm_ref[P_CBIAS:P_CBIAS + 1, :]
        mu = _rowmean(u1)
        uc = u1 - mu
        rstd = lax.rsqrt(_rowmean(uc * uc) + LN_EPS)
        rstd_ref[...] = rstd
        uhat_ref[...] = (uc * rstd).astype(BF16)
        u2 = uhat_ref[...].astype(F32) * prm_ref[P_LNG:P_LNG + 1, :] + prm_ref[P_LNB:P_LNB + 1, :]
        u3 = (u2 * _sigmoid(u2)).astype(BF16)
        lhs_ref[1] = u3
        y_b = _dot(u3, wco_ref[...])
        ya_ref[...] = y_a.astype(BF16)
        yb_ref[...] = y_b.astype(BF16)
        ga = _sigmoid(col(5))
        gb = _sigmoid(col(6))
        ga_ref[...] = ga.astype(BF16)
        gb_ref[...] = gb.astype(BF16)
        merged = (ga * ya_ref[...].astype(F32) + gb * yb_ref[...].astype(F32)).astype(BF16)
        lhs_ref[2] = merged
        mix = _dot(merged, wo_ref[...])
        mix_ref[...] = mix.astype(BF16)
        x1_ref[...] = x_ref[...] + mod_ref[M_G1:M_G1 + 1, :] * mix

        @pl.when(i == nt - 1)
        def _():
            gather.finish()

    tile = pl.BlockSpec((tm, d), lambda i: (i, 0))
    whole = lambda shape: pl.BlockSpec(shape, lambda i: (0,) * len(shape))
    res = pl.pallas_call(
        body, name="forward_mix", grid=(nt,),
        out_shape=(jax.ShapeDtypeStruct((t, d), F32),) + (jax.ShapeDtypeStruct((t, d), BF16),) * 8
        + (jax.ShapeDtypeStruct((3, t, d), BF16), jax.ShapeDtypeStruct((t, 1), F32))
        + tuple(jax.ShapeDtypeStruct((N_DEV,) + a.shape, a.dtype) for a in shards),
        in_specs=[tile, pl.BlockSpec((tm, 7 * d), lambda i: (i, 0)), whole((8, d)), whole((8, d)),
                  whole(taps.shape), whole((d, d)), whole((d, d)), whole((d, d))] + [ANY_SPEC] * na,
        out_specs=[tile] * 9 + [pl.BlockSpec((3, tm, d), lambda i: (0, i, 0)), pl.BlockSpec((tm, 1), lambda i: (i, 0))]
        + [ANY_SPEC] * na,
        scratch_shapes=[pltpu.VMEM((hs + tm, d), F32), pltpu.VMEM((hc + tm, d), F32),
                        pltpu.VMEM((tm, d), F32), pltpu.VMEM((tm, d), F32)] + _comm_scratch(na),
        compiler_params=_params(("arbitrary",)),
    )(x, proj, mod, prm, taps, w_so, w_co, w_o, *shards)
    return res[:11], res[11:]


def _ffn_chunks(ff):
    mxu = 2 * LANES
    cut = (ff // mxu + 1) // 2 * mxu
    return [(0, cut), (cut, ff)] if 0 < cut < ff and ff % mxu == 0 else [(0, ff)]


def _forward_ffn(x1, tgt, mod, prm, w_fi_t, w_fo, tm):
    t, d = x1.shape
    ff = w_fo.shape[0]

    def body(x1_ref, tgt_ref, mod_ref, prm_ref, wfi_hbm, wfo_hbm,
             dx2_ref, ab_ref, h2_ref, act_ref, sums_ref, wfi_ref, wfo_ref):
        i = pl.program_id(0)

        @pl.when(i == 0)
        def _():
            pltpu.sync_copy(wfi_hbm, wfi_ref)
            pltpu.sync_copy(wfo_hbm, wfo_ref)
            sums_ref[...] = jnp.zeros((8, d), F32)

        x1v = x1_ref[...]
        r2 = lax.rsqrt(_rowmean(x1v * x1v) + EPS)
        h2 = (x1v * r2 * prm_ref[P_GFFN:P_GFFN + 1, :] * (1.0 + mod_ref[M_SC2:M_SC2 + 1, :])
              + mod_ref[M_SH2:M_SH2 + 1, :]).astype(BF16)
        h2_ref[...] = h2
        f = jnp.zeros((tm, d), F32)
        for c0, c1 in _ffn_chunks(ff):
            ab_ref[:, c0:c1] = _dot_nt(h2, wfi_ref[c0:c1, :]).astype(BF16)
            ab_ref[:, ff + c0:ff + c1] = _dot_nt(h2, wfi_ref[ff + c0:ff + c1, :]).astype(BF16)
            a = ab_ref[:, c0:c1].astype(F32)
            act = (a * _sigmoid(a) * ab_ref[:, ff + c0:ff + c1].astype(F32)).astype(BF16)
            act_ref[:, c0:c1] = act
            f = f + _dot(act, wfo_ref[c0:c1, :])
        x2 = x1v + mod_ref[M_G2:M_G2 + 1, :] * f
        r3 = lax.rsqrt(_rowmean(x2 * x2) + EPS)
        xn3 = x2 * r3
        gfin = prm_ref[P_GFIN:P_GFIN + 1, :]
        err = xn3 * gfin - tgt_ref[...]
        dy = err * (1.0 / d)
        dxn3 = dy * gfin
        dx2 = r3 * (dxn3 - xn3 * _rowmean(dxn3 * xn3))
        dx2_ref[...] = dx2
        sums_ref[0:1, :] = sums_ref[0:1, :] + _colsum(dy * xn3)
        sums_ref[1:2, :] = sums_ref[1:2, :] + _colsum(dx2 * f)
        sums_ref[2:3, :] = sums_ref[2:3, :] + _colsum(err * err) * (0.5 / d)

    tile = pl.BlockSpec((tm, d), lambda i: (i, 0))
    whole = lambda shape: pl.BlockSpec(shape, lambda i: (0,) * len(shape))
    return pl.pallas_call(
        body, name="forward_ffn", grid=(t // tm,),
        out_shape=(jax.ShapeDtypeStruct((t, d), F32), jax.ShapeDtypeStruct((t, 2 * ff), BF16),
                   jax.ShapeDtypeStruct((t, d), BF16), jax.ShapeDtypeStruct((t, ff), BF16),
                   jax.ShapeDtypeStruct((8, d), F32)),
        in_specs=[tile, tile, whole((8, d)), whole((8, d)), ANY_SPEC, ANY_SPEC],
        out_specs=[tile, pl.BlockSpec((tm, 2 * ff), lambda i: (i, 0)), tile,
                   pl.BlockSpec((tm, ff), lambda i: (i, 0)), whole((8, d))],
        scratch_shapes=[pltpu.VMEM(w_fi_t.shape, BF16), pltpu.VMEM(w_fo.shape, BF16)],
        compiler_params=_params(("arbitrary",)),
    )(x1, tgt, mod, prm, w_fi_t, w_fo)


def _backward_ffn(dx2, x1, ab, mod, prm, w_fi_t, w_fo, tm):
    t, d = x1.shape
    ff = w_fo.shape[0]

    def body(dx2_ref, x1_ref, ab_ref, mod_ref, prm_ref, wfi_hbm, wfo_hbm,
             dx1_ref, df_ref, dab_ref, sums_ref, wfi_ref, wfo_ref):
        i = pl.program_id(0)

        @pl.when(i == 0)
        def _():
            pltpu.sync_copy(wfi_hbm, wfi_ref)
            pltpu.sync_copy(wfo_hbm, wfo_ref)
            sums_ref[...] = jnp.zeros((8, d), F32)

        dx2v = dx2_ref[...]
        df = (mod_ref[M_G2:M_G2 + 1, :] * dx2v).astype(BF16)
        df_ref[...] = df
        dh2 = jnp.zeros((tm, d), F32)
        for c0, c1 in _ffn_chunks(ff):
            dact = _dot_nt(df, wfo_ref[c0:c1, :])
            a = ab_ref[:, c0:c1].astype(F32)
            b = ab_ref[:, ff + c0:ff + c1].astype(F32)
            s = _sigmoid(a)
            sil = a * s
            da =(dact * b * (s * (1.0 + a * (1.0 - s)))).astype(BF16)
            db = (dact * sil).astype(BF16)
            dab_ref[:, c0:c1] = da
            dab_ref[:, ff + c0:ff + c1] = db
            dh2 = dh2 + _dot(da, wfi_ref[c0:c1, :]) + _dot(db, wfi_ref[ff + c0:ff + c1, :])
        x1v = x1_ref[...]
        r2 = lax.rsqrt(_rowmean(x1v * x1v) + EPS)
        xn2 = x1v * r2
        gffn = prm_ref[P_GFFN:P_GFFN + 1, :]
        scale = 1.0 + mod_ref[M_SC2:M_SC2 + 1, :]
        dxn2 = dh2 * gffn * scale
        dx1_ref[...] = dx2v + r2 * (dxn2 - xn2 * _rowmean(dxn2 * xn2))
        hx = dh2 * xn2
        sums_ref[0:1, :] = sums_ref[0:1, :] + _colsum(dh2)
        sums_ref[1:2, :] = sums_ref[1:2, :] + _colsum(hx) * gffn
        sums_ref[2:3, :] = sums_ref[2:3, :] + _colsum(hx) * scale

    tile = pl.BlockSpec((tm, d), lambda i: (i, 0))
    whole = lambda shape: pl.BlockSpec(shape, lambda i: (0,) * len(shape))
    wide = lambda n: pl.BlockSpec((tm, n), lambda i: (i, 0))
    return pl.pallas_call(
        body, name="backward_ffn", grid=(t // tm,),
        out_shape=(jax.ShapeDtypeStruct((t, d), F32), jax.ShapeDtypeStruct((t, d), BF16),
                   jax.ShapeDtypeStruct((t, 2 * ff), BF16), jax.ShapeDtypeStruct((8, d), F32)),
        in_specs=[tile, tile, wide(2 * ff), whole((8, d)), whole((8, d)), ANY_SPEC, ANY_SPEC],
        out_specs=[tile, tile, wide(2 * ff), whole((8, d))],
        scratch_shapes=[pltpu.VMEM(w_fi_t.shape, BF16), pltpu.VMEM(w_fo.shape, BF16)],
        compiler_params=_params(("arbitrary",)),
    )(dx2, x1, ab, mod, prm, w_fi_t, w_fo)


def _backward_mix(dx1, proj, y_a, y_b, mix, uhat, rstd, conv3, sg, ga, gb, mod, prm, taps, w_so, w_co, w_o, tm,
                  partials):
    t, d = dx1.shape
    nt = t // tm
    na = len(partials)
    hs, hc = HALO_SHORT, HALO_CONF
    r3, r31 = 0, 8

    def body(*refs):
        (dx1_ref, proj_ref, halo_ref, ya_ref, yb_ref, mix_ref, uhat_ref, rstd_ref, c3_ref, sg_ref, ga_ref, gb_ref,
         mod_ref, prm_ref, taps_ref, wso_ref, wco_ref, wo_ref) = refs[:18]
        dproj_ref, rhs_ref, sums_ref, dw3_ref, dw31_ref = refs[18 + na:23 + na]
        cv_ext, u0_ext, d3_ext, du1_ext, tmp_ref = refs[23 + 2 * na:28 + 2 * na]
        scatter_start, scatter_finish = _scatter_plan(refs[18:18 + na], refs[23 + na:23 + 2 * na], *refs[28 + 2 * na:])
        i = pl.program_id(0)
        first_tile = i == nt - 1

        @pl.when(i == 0)
        def _():
            scatter_start()
            sums_ref[...] = jnp.zeros((8, d), F32)
            dw3_ref[...] = jnp.zeros(dw3_ref.shape, F32)
            dw31_ref[...] = jnp.zeros(dw31_ref.shape, F32)
            d3_ext[tm:tm + hs, :] = jnp.zeros((hs, d), F32)
            du1_ext[tm:tm + hc, :] = jnp.zeros((hc, d), F32)

        def col(g):
            return proj_ref[:, g * d:(g + 1) * d].astype(F32)

        def hcol(g, rows):
            v = halo_ref[HALO_CONF - rows:HALO_CONF, g * d:(g + 1) * d].astype(F32)
            return jnp.where(first_tile, 0.0, v)

        dx1v = dx1_ref[...]
        mixv = mix_ref[...].astype(F32)
        dmix = (mod_ref[M_G1:M_G1 + 1, :] * dx1v).astype(BF16)
        rhs_ref[2] = dmix
        sums_ref[0:1, :] = sums_ref[0:1, :] + _colsum(dx1v * mixv)
        dmerged = _dot_nt(dmix, wo_ref[...])
        ga = ga_ref[...].astype(F32)
        gb = gb_ref[...].astype(F32)
        yav = ya_ref[...].astype(F32)
        ybv = yb_ref[...].astype(F32)
        dya_f = dmerged * ga
        dyb_f = dmerged * gb
        dya = dya_f.astype(BF16)
        dyb = dyb_f.astype(BF16)
        rhs_ref[0] = dya
        rhs_ref[1] = dyb
        dproj_ref[:, 5 * d:6 * d] = (dya_f * yav * (1.0 - ga)).astype(BF16)
        dproj_ref[:, 6 * d:7 * d] = (dyb_f * ybv * (1.0 - gb)).astype(BF16)

        dya_pre = _dot_nt(dya, wso_ref[...])
        c_s, v_s, b_s = col(1), col(2), col(0)
        cv_ext[0:hs, :] = hcol(1, hs) * hcol(2, hs)
        cv_ext[hs:hs + tm, :] = c_s * v_s
        dproj_ref[:, 0:d] = (dya_pre * c3_ref[...].astype(F32)).astype(BF16)
        d3_ext[0:tm, :] = dya_pre * b_s
        _tap_wgrad(d3_ext, cv_ext, _causal_taps(SHORT_K, hs), tm, hs, dw3_ref)
        _tap_conv(d3_ext, taps_ref, [(r3 + wi, off) for wi, off in _anticausal_taps(SHORT_K)], tm, hs, tmp_ref)
        dcv = tmp_ref[...]
        dproj_ref[:, d:2 * d] = (dcv * v_s).astype(BF16)
        dproj_ref[:, 2 * d:3 * d] = (dcv * c_s).astype(BF16)
        d3_ext[tm:tm + hs, :] = d3_ext[0:hs, :]

        du3 = _dot_nt(dyb, wco_ref[...])
        uhat = uhat_ref[...].astype(F32)
        rstd = rstd_ref[...]
        lng = prm_ref[P_LNG:P_LNG + 1, :]
        u2 = uhat * lng + prm_ref[P_LNB:P_LNB + 1, :]
        s2 = _sigmoid(u2)
        du2 = du3 * (s2 * (1.0 + u2 * (1.0 - s2)))
        sums_ref[1:2, :] = sums_ref[1:2, :] + _colsum(du2 * uhat)
        sums_ref[2:3, :] = sums_ref[2:3, :] + _colsum(du2)
        duhat = du2 * lng
        du1 = rstd * (duhat - _rowmean(duhat) - uhat * _rowmean(duhat * uhat))
        sums_ref[3:4, :] = sums_ref[3:4, :] + _colsum(du1)
        du1_ext[0:tm, :] = du1
        v_c = col(3)
        sg = sg_ref[...].astype(F32)
        u0_ext[0:hc, :] = hcol(3, hc) * _sigmoid(hcol(4, hc))
        u0_ext[hc:hc + tm, :] = v_c * sg
        _tap_wgrad(du1_ext, u0_ext, _causal_taps(CONF_K, hc), tm, hc, dw31_ref)
        _tap_conv(du1_ext, taps_ref, [(r31 + wi, off) for wi, off in _anticausal_taps(CONF_K)], tm, hc, tmp_ref)
        dv_c = tmp_ref[...] * sg
        dproj_ref[:, 3 * d:4 * d] = dv_c.astype(BF16)
        dproj_ref[:, 4 * d:5 * d] = (dv_c * v_c * (1.0 - sg)).astype(BF16)
        du1_ext[tm:tm + hc, :] = du1_ext[0:hc, :]

        @pl.when(i == nt - 1)
        def _():
            scatter_finish()

    rev = lambda i: (nt - 1 - i, 0)
    tile = pl.BlockSpec((tm, d), rev)
    whole = lambda shape: pl.BlockSpec(shape, lambda i: (0,) * len(shape))
    hblocks = tm // HALO_CONF
    halo = pl.BlockSpec((HALO_CONF, 7 * d), lambda i: (jnp.maximum((nt - 1 - i) * hblocks - 1, 0), 0))
    res = pl.pallas_call(
        body, name="backward_mix", grid=(nt,),
        out_shape=(jax.ShapeDtypeStruct((t, 7 * d), BF16), jax.ShapeDtypeStruct((3, t, d), BF16),
                   jax.ShapeDtypeStruct((8, d), F32),
                   jax.ShapeDtypeStruct((SUBLANES * SHORT_K, d), F32),
                   jax.ShapeDtypeStruct((SUBLANES * CONF_K, d), F32))
        + tuple(jax.ShapeDtypeStruct(p.shape, p.dtype) for p in partials),
        in_specs=[tile, pl.BlockSpec((tm, 4 * d), rev), halo, tile, tile, tile, tile, pl.BlockSpec((tm, 1), rev),
                  tile, tile, tile, tile,
                  whole((8, d)), whole((8, d)), whole(taps.shape), whole((d, d)), whole((d, d)), whole((d, d))]
        + [ANY_SPEC] * na,
        out_specs=[pl.BlockSpec((tm, 7 * d), rev), pl.BlockSpec((3, tm, d), lambda i: (0, nt - 1 - i, 0)),
                   whole((8, d)), whole((SUBLANES * SHORT_K, d)), whole((SUBLANES * CONF_K, d))] + [ANY_SPEC] * na,
        scratch_shapes=[pltpu.VMEM((hs + tm, d), F32), pltpu.VMEM((hc + tm, d), F32),
                        pltpu.VMEM((tm + hs, d), F32), pltpu.VMEM((tm + hc, d), F32),
                        pltpu.VMEM((tm, d), F32)] + _comm_scratch(na),
        compiler_params=_params(("arbitrary",)),
    )(dx1, proj, proj, y_a, y_b, mix, uhat, rstd, conv3, sg, ga, gb, mod, prm, taps, w_so, w_co, w_o, *partials)
    return res[:5], res[5:]


def _backward_in(dproj, x, dx1, mod, prm, w_in_g, tm, partials, partial_view, exchanged_shapes):
    t, d = x.shape
    n_all = w_in_g.shape[1]
    na = len(partials)
    nt = t // tm

    def body(*refs):
        dproj_ref, x_ref, dx1_ref, mod_ref, prm_ref, w_hbm = refs[:6]
        gx_ref, sums_ref = refs[6 + na:8 + na]
        w_ref = refs[8 + 2 * na]
        scatter_start, scatter_finish = _scatter_plan(refs[6:6 + na], refs[8 + na:8 + 2 * na], *refs[9 + 2 * na:],
                                                      view=partial_view)

        @pl.when(pl.program_id(0) == 0)
        def _():
            scatter_start()
            pltpu.sync_copy(w_hbm, w_ref)
            sums_ref[...] = jnp.zeros((8, d), F32)

        dh = _dot_nt(dproj_ref[...], w_ref[...])
        xv = x_ref[...]
        r1 = lax.rsqrt(_rowmean(xv * xv) + EPS)
        xn = xv * r1
        gmix = prm_ref[P_GMIX:P_GMIX + 1, :]
        scale = 1.0 + mod_ref[M_SC1:M_SC1 + 1, :]
        dxn = dh * gmix * scale
        gx_ref[...] = dx1_ref[...] + r1 * (dxn - xn * _rowmean(dxn * xn))
        hx = dh * xn
        sums_ref[0:1, :] = sums_ref[0:1, :] + _colsum(dh)
        sums_ref[1:2, :] = sums_ref[1:2, :] + _colsum(hx) * gmix
        sums_ref[2:3, :] = sums_ref[2:3, :] + _colsum(hx) * scale

        @pl.when(pl.program_id(0) == nt - 1)
        def _():
            scatter_finish()

    tile = pl.BlockSpec((tm, d), lambda i: (i, 0))
    whole = pl.BlockSpec((8, d), lambda i: (0, 0))
    res = pl.pallas_call(
        body, name="backward_in", grid=(nt,),
        out_shape=(jax.ShapeDtypeStruct((t, d), F32), jax.ShapeDtypeStruct((8, d), F32))
        + tuple(jax.ShapeDtypeStruct(s, p.dtype) for p, s in zip(partials, exchanged_shapes)),
        in_specs=[pl.BlockSpec((tm, n_all), lambda i: (i, 0)), tile, tile, whole, whole, ANY_SPEC] + [ANY_SPEC] * na,
        out_specs=[tile, whole] + [ANY_SPEC] * na,
        scratch_shapes=[pltpu.VMEM(w_in_g.shape, BF16)] + _comm_scratch(na),
        compiler_params=_params(("arbitrary",)),
    )(dproj, x, dx1, mod, prm, w_in_g, *partials)
    return res[0], res[1], res[2:]


def _weight_grad(a, b, a_spec, b_spec, ns, m, n, nk, name, partials=(), by_columns=False,
                 partial_view=_leading_block, exchanged_shapes=None):
    na = len(partials)
    exchanged_shapes = exchanged_shapes or [p.shape for p in partials]

    def body(*refs):
        a_ref, b_ref = refs[:2]
        o_ref = refs[2 + na]
        acc_ref = refs[3 + 2 * na]
        s, k = pl.program_id(0), pl.program_id(1)
        if na:
            scatter_start, scatter_finish = _scatter_plan(refs[2:2 + na], refs[3 + na:3 + 2 * na], *refs[4 + 2 * na:],
                                                          view=partial_view)

            @pl.when((s == 0) & (k == 0))
            def _():
                scatter_start()

        av = a_ref[0] if len(a_ref.shape) == 3 else a_ref[...]
        bv = b_ref[0] if len(b_ref.shape) == 3 else b_ref[...]
        part = _dot_tn(av, bv)

        @pl.when(k == 0)
        def _():
            acc_ref[...] = part

        @pl.when(k > 0)
        def _():
            acc_ref[...] = acc_ref[...] + part

        @pl.when(k == nk - 1)
        def _():
            if by_columns:
                o_ref[...] = acc_ref[...].astype(BF16)
            else:
                o_ref[0] = acc_ref[...].astype(BF16)

        if na:
            @pl.when((s == ns - 1) & (k == nk - 1))
            def _():
                scatter_finish()

    if by_columns:
        out_shape, out_spec = (m, ns * n), pl.BlockSpec((m, n), lambda s, k: (0, s))
    else:
        out_shape, out_spec = (ns, m, n), pl.BlockSpec((1, m, n), lambda s, k: (s, 0, 0))
    res = pl.pallas_call(
        body, name=name, grid=(ns, nk),
        out_shape=(jax.ShapeDtypeStruct(out_shape, BF16),)
        + tuple(jax.ShapeDtypeStruct(sh, p.dtype) for p, sh in zip(partials, exchanged_shapes)),
        in_specs=[a_spec, b_spec] + [ANY_SPEC] * na,
        out_specs=[out_spec] + [ANY_SPEC] * na,
        scratch_shapes=[pltpu.VMEM((m, n), F32)] + (_comm_scratch(na) if na else []),
        compiler_params=_params(("arbitrary", "arbitrary")),
    )(a, b, *partials)
    return (res[0], res[1:]) if na else res[0]


def _adamw(w, g, m, v):
    m = ADAM_B1 * m + (1.0 - ADAM_B1) * g
    v = ADAM_B2 * v + (1.0 - ADAM_B2) * (g * g)
    m_hat = m / (1.0 - ADAM_B1 ** ADAM_STEP)
    v_hat = v / (1.0 - ADAM_B2 ** ADAM_STEP)
    delta = -ADAM_LR * (m_hat / (jnp.sqrt(v_hat) + ADAM_EPS) + ADAM_WD * w)
    return delta, m, v


def _adamw_shard(parts, w, m, v, tr, name):
    r, c = w.shape

    def body(p_ref, w_ref, m_ref, v_ref, g_ref, d_ref, nm_ref, nv_ref):
        g = p_ref[0].astype(F32)
        for s in range(1, N_DEV):
            g = g + p_ref[s].astype(F32)
        delta, nm, nv = _adamw(w_ref[...], g, m_ref[...], v_ref[...])
        g_ref[...] = g
        d_ref[...] = delta
        nm_ref[...] = nm
        nv_ref[...] = nv

    tile = pl.BlockSpec((tr, c), lambda i: (i, 0))
    return pl.pallas_call(
        body, name=name, grid=(r // tr,),
        out_shape=(jax.ShapeDtypeStruct((r, c), F32),) * 4,
        in_specs=[pl.BlockSpec((N_DEV, tr, c), lambda i: (0, i, 0)), tile, tile, tile],
        out_specs=[tile] * 4,
        compiler_params=_params(("arbitrary",)),
    )(parts, w, m, v)


def _adamw_members(parts, wmv, name):
    nm = len(wmv)
    r, c = wmv[0][0].shape

    def body(*refs):
        p_ref = refs[0]
        for j in range(nm):
            w_ref, m_ref, v_ref = refs[1 + 3 * j:4 + 3 * j]
            g = p_ref[0, j].astype(F32)
            for s in range(1, N_DEV):
                g = g + p_ref[s, j].astype(F32)
            delta, new_m, new_v = _adamw(w_ref[...], g, m_ref[...], v_ref[...])
            for o_ref, val in zip(refs[1 + 3 * nm + 4 * j:5 + 3 * nm + 4 * j], (g, delta, new_m, new_v)):
                o_ref[...] = val

    vm = pl.BlockSpec(memory_space=pltpu.VMEM)
    res = pl.pallas_call(
        body, name=name,
        out_shape=(jax.ShapeDtypeStruct((r, c), F32),) * (4 * nm),
        in_specs=[vm] * (1 + 3 * nm), out_specs=[vm] * (4 * nm),
        compiler_params=_params(),
    )(parts, *[a for triple in wmv for a in triple])
    return [tuple(res[4 * j:4 * j + 4]) for j in range(nm)]


def _ada_update(sc_all, dmod_cols, w, m, v, tr):
    d, n = w.shape

    def body(sc_ref, dm_ref, w_ref, m_ref, v_ref, g_ref, d_ref, nm_ref, nv_ref):
        g = lax.dot_general(sc_ref[...], dm_ref[...], (((0,), (0,)), ((), ())),
                            preferred_element_type=F32, precision=lax.Precision.HIGHEST)
        delta, nm, nv = _adamw(w_ref[...], g, m_ref[...], v_ref[...])
        g_ref[...] = g
        d_ref[...] = delta
        nm_ref[...] = nm
        nv_ref[...] = nv

    tile = pl.BlockSpec((tr, n), lambda i: (i, 0))
    return pl.pallas_call(
        body, name="ada_update", grid=(d // tr,),
        out_shape=(jax.ShapeDtypeStruct((d, n), F32),) * 4,
        in_specs=[pl.BlockSpec((N_DEV, tr), lambda i: (0, i)), pl.BlockSpec((N_DEV, n), lambda i: (0, 0)),
                  tile, tile, tile],
        out_specs=[tile] * 4,
        compiler_params=_params(("arbitrary",)),
    )(sc_all, dmod_cols, w, m, v)


def _small_exchange(vec, cg):
    l = vec.shape[2]
    rows = cg.shape[1]

    def body(vec_ref, cg_ref, vall_ref, cgr_ref, send_sems, recv_sems):
        x, y, c = _my_coords()
        me = _slot(x, y, c)
        vall_ref[me] = vec_ref[0]
        cgr_ref[me] = cg_ref[me]
        copies = []
        for k in range(1, N_DEV):
            peer = (_flip(x, k & 4), _flip(y, k & 2), _flip(c, k & 1))
            copies.append(pltpu.make_async_remote_copy(
                src_ref=vall_ref.at[me], dst_ref=vall_ref.at[me], send_sem=send_sems.at[k - 1],
                recv_sem=recv_sems.at[k - 1], device_id=peer, device_id_type=MESH))
            copies.append(pltpu.make_async_remote_copy(
                src_ref=cg_ref.at[_slot(*peer)], dst_ref=cgr_ref.at[me], send_sem=send_sems.at[7 + k - 1],
                recv_sem=recv_sems.at[7 + k - 1], device_id=peer, device_id_type=MESH))
        for cp in copies:
            cp.start()
        for cp in copies:
            cp.wait_recv()
        for cp in copies:
            cp.wait_send()

    vm = pl.BlockSpec(memory_space=pltpu.VMEM)
    return pl.pallas_call(
        body, name="small_exchange",
        out_shape=(jax.ShapeDtypeStruct((N_DEV, 1, l), F32), jax.ShapeDtypeStruct((N_DEV, rows, LANES), F32)),
        in_specs=[vm, vm], out_specs=[vm, vm],
        scratch_shapes=[pltpu.SemaphoreType.DMA((14,)), pltpu.SemaphoreType.DMA((14,))],
        compiler_params=_params(),
    )(vec, cg)


def _small_update(vall, cgr, smalls):
    ns = len(smalls)

    def body(*refs):
        vall_ref, cgr_ref = refs[0], refs[1]
        wmv = refs[2:2 + 3 * ns]
        outs = refs[2 + 3 * ns:]
        for p, (_, _, _, lo, hi, kind) in enumerate(smalls):
            w_ref, m_ref, v_ref = wmv[3 * p:3 * p + 3]
            part = (lambda s: vall_ref[s, :, lo:hi]) if kind == "vec" else (lambda s: cgr_ref[s, lo:hi, :])
            full = (lambda ref: ref[...]) if kind == "vec" else (lambda ref: ref[0])
            g = part(0)
            for s in range(1, N_DEV):
                g = g + part(s)
            delta, nm, nv = _adamw(full(w_ref), g, full(m_ref), full(v_ref))
            for o_ref, val in zip(outs[4 * p:4 * p + 4], (g, delta, nm, nv)):
                if kind == "vec":
                    o_ref[...] = val
                else:
                    o_ref[0] = val

    vm = pl.BlockSpec(memory_space=pltpu.VMEM)
    args = [vall, cgr]
    out_shape = []
    for w, m, v, _, _, _ in smalls:
        args += [w, m, v]
        out_shape += [jax.ShapeDtypeStruct(w.shape, F32)] * 4
    res = pl.pallas_call(
        body, name="small_update",
        out_shape=tuple(out_shape),
        in_specs=[vm] * len(args), out_specs=[vm] * len(out_shape),
        compiler_params=_params(),
    )(*args)
    return [res[4 * p:4 + 4 * p] for p in range(ns)]


def _pick(t, want):
    return want if t % want == 0 else t


def kernel(x, c, w_ada, b_ada, norm_mix_g, w_in, conv_short_w, w_short_out, conv_conf_w, conv_conf_b, conf_ln_g, conf_ln_b, w_conf_out, w_o, norm_ffn_g, w_ffn_in, w_ffn_out, final_norm_g, loss_target, m_w_ada, m_b_ada, m_norm_mix_g, m_w_in, m_conv_short_w, m_w_short_out, m_conv_conf_w, m_conv_conf_b, m_conf_ln_g, m_conf_ln_b, m_w_conf_out, m_w_o, m_norm_ffn_g, m_w_ffn_in, m_w_ffn_out, m_final_norm_g, v_w_ada, v_b_ada, v_norm_mix_g, v_w_in, v_conv_short_w, v_w_short_out, v_conv_conf_w, v_conv_conf_b, v_conf_ln_g, v_conf_ln_b, v_w_conf_out, v_w_o, v_norm_ffn_g, v_w_ffn_in, v_w_ffn_out, v_final_norm_g):
    t, d = x.shape[1], x.shape[2]
    x2 = x.reshape(t, d)
    tgt = loss_target.reshape(t, d)
    me = _slot(*_my_coords())
    tm = _pick(t, 256)
    tm_fwd_in = _pick(t, 512)
    tk = _pick(t, 2048)

    taps_loc = jnp.zeros((40, LANES), F32)
    taps_loc = taps_loc.at[0:SHORT_K].set(conv_short_w[0]).at[8:8 + CONF_K].set(conv_conf_w[0])
    prm = jnp.concatenate([norm_mix_g, norm_ffn_g, final_norm_g.reshape(1, d), conv_conf_b, conf_ln_g, conf_ln_b,
                           jnp.zeros((2, d), F32)], axis=0)

    n_in = w_in.shape[2]
    nk = t // tk
    tok = pl.BlockSpec((tk, d), lambda s, k: (k, 0))
    rows = d // N_DEV
    frows = w_ffn_out.shape[1]

    proj, h, w_in_g, mod, sc_all3, taps, (w_so_g, w_co_g, w_o_g) = _forward_in(
        jnp.reshape(me, (1,)).astype(jnp.int32), x2, prm, w_in[0].astype(BF16),
        (c, w_ada[0], b_ada, taps_loc), tm_fwd_in,
        [w_short_out[0].astype(BF16), w_conf_out[0].astype(BF16), w_o[0].astype(BF16)])
    w_so = w_so_g.reshape(d, d)
    w_co = w_co_g.reshape(d, d)
    w_oo = w_o_g.reshape(d, d)
    w_fi_t, m_fi_t, v_fi_t = w_ffn_in[0].T, m_w_ffn_in[0].T, v_w_ffn_in[0].T
    (x1, y_a, y_b, mix, uhat, conv3, sg, ga, gb, mixer_lhs, rstd), (w_fi_g, w_fo_g) = _forward_mix(
        x2, proj, mod, prm, taps, w_so, w_co, w_oo, tm, [w_fi_t.astype(BF16), w_ffn_out[0].astype(BF16)])
    fb = w_fi_g.shape[1]
    ff = N_DEV * frows
    w_fi_all = w_fi_g.reshape(N_DEV * fb, d)
    w_fo_all = w_fo_g.reshape(ff, d)
    dx2, ab, h2, act, sums_f = _forward_ffn(x1, tgt, mod, prm, w_fi_all, w_fo_all, _pick(t, 512))

    dx1, df, dab, sums_b = _backward_ffn(dx2, x1, ab, mod, prm, w_fi_all, w_fo_all, tm)
    fb2 = 2 * fb
    pair_tok = pl.BlockSpec((tk, fb2), lambda s, k: (k, s))
    g_fi = _weight_grad(dab, h2, pair_tok, tok, N_DEV // 2, fb2, d, nk, "grad_w_ffn_in")
    g_fo = _weight_grad(act, df, pair_tok, tok, N_DEV // 4, fb2, d, nk, "grad_w_ffn_out")
    (dproj, mixer_rhs, sums_m, dw3p, dw31p), (p_fi, p_fo) = _backward_mix(
        dx1, proj, y_a, y_b, mix, uhat, rstd, conv3, sg, ga, gb, mod, prm, taps, w_so, w_co, w_oo, tm,
        [g_fi.reshape(N_DEV, fb, d), g_fo.reshape(N_DEV, frows, d)])
    tk_in = _pick(t, 4096)
    nk_in = t // tk_in
    tok_in = pl.BlockSpec((tk_in, d), lambda s, k: (k, 0))
    member_tok = pl.BlockSpec((1, tk, d), lambda s, k: (s, k, 0))
    g_mixers = _weight_grad(mixer_lhs, mixer_rhs, member_tok, member_tok, 3, d, d, nk, "grad_w_mixers")
    n_blk = _pick(N_DEV * n_in, d)
    g_in, (p_mixers,) = _weight_grad(
        h, dproj, tok_in, pl.BlockSpec((tk_in, n_blk), lambda s, k: (k, s)),
        N_DEV * n_in // n_blk, d, n_blk, nk_in, "grad_w_in",
        [g_mixers.reshape(3, N_DEV, rows, d)], by_columns=True,
        partial_view=lambda ref, dev: ref.at[:, dev], exchanged_shapes=[(N_DEV, 3, rows, d)])
    grad_x, sums_i, (p_in,) = _backward_in(dproj, x2, dx1, mod, prm, w_in_g, tm, [g_in], _column_block(n_in),
                                           [(N_DEV, d, n_in)])

    up_in = _adamw_shard(p_in, w_in[0], m_w_in[0], v_w_in[0], _pick(d, 512), "adamw_w_in")
    up_so, up_co, up_oo = _adamw_members(
        p_mixers, [(w_short_out[0], m_w_short_out[0], v_w_short_out[0]),
                   (w_conf_out[0], m_w_conf_out[0], v_w_conf_out[0]), (w_o[0], m_w_o[0], v_w_o[0])], "adamw_w_mixers")
    up_fi = tuple(a.T for a in _adamw_shard(p_fi, w_fi_t, m_fi_t, v_fi_t, fb // 2, "adamw_w_ffn_in"))
    up_fo = _adamw_shard(p_fo, w_ffn_out[0], m_w_ffn_out[0], v_w_ffn_out[0], frows, "adamw_w_ffn_out")

    vec = jnp.concatenate([sums_i[0:2], sums_m[0:1], sums_b[0:2], sums_f[1:2],
                           sums_i[2:3], sums_m[3:4], sums_m[1:3], sums_b[2:3], sums_f[0:1],
                           sums_f[2:3]], axis=0)
    vec = vec.reshape(1, 1, 13 * d)
    dw3 = dw3p.reshape(SHORT_K, SUBLANES, d).sum(axis=1)
    dw31 = dw31p.reshape(CONF_K, SUBLANES, d).sum(axis=1)
    cg = jnp.zeros((40, d), F32).at[0:SHORT_K].set(dw3).at[8:8 + CONF_K].set(dw31)
    cg = cg.reshape(40, N_DEV, LANES).transpose(1, 0, 2)
    fin = lambda a: a.reshape(1, d)
    smalls = [
        (b_ada, m_b_ada, v_b_ada, 0, 6 * d, "vec"),
        (norm_mix_g, m_norm_mix_g, v_norm_mix_g, 6 * d, 7 * d, "vec"),
        (conv_short_w, m_conv_short_w, v_conv_short_w, 0, SHORT_K, "cg"),
        (conv_conf_w, m_conv_conf_w, v_conv_conf_w, 8, 8 + CONF_K, "cg"),
        (conv_conf_b, m_conv_conf_b, v_conv_conf_b, 7 * d, 8 * d, "vec"),
        (conf_ln_g, m_conf_ln_g, v_conf_ln_g, 8 * d, 9 * d, "vec"),
        (conf_ln_b, m_conf_ln_b, v_conf_ln_b, 9 * d, 10 * d, "vec"),
        (norm_ffn_g, m_norm_ffn_g, v_norm_ffn_g, 10 * d, 11 * d, "vec"),
        (fin(final_norm_g), fin(m_final_norm_g), fin(v_final_norm_g), 11 * d, 12 * d, "vec"),
    ]
    vall, cgr = _small_exchange(vec, cg)
    up_small = _small_update(vall, cgr, smalls)
    n_ada = w_ada.shape[2]
    dmod_all = vall.reshape(N_DEV, 13 * d)[:, 0:6 * d]
    dmod_cols = lax.dynamic_slice(dmod_all, (0, me * n_ada), (N_DEV, n_ada))
    up_ada = _ada_update(sc_all3.reshape(N_DEV, d), dmod_cols, w_ada[0], m_w_ada[0], v_w_ada[0], _pick(d, 512))

    loss = jnp.sum(vall.reshape(N_DEV, 13 * d)[:, 12 * d:])

    lead = lambda a: a.reshape((1,) + a.shape)
    ups = [tuple(lead(a) for a in up_ada), up_small[0], up_small[1], tuple(lead(a) for a in up_in),
           up_small[2], tuple(lead(a) for a in up_so), up_small[3],
           up_small[4], up_small[5], up_small[6],
           tuple(lead(a) for a in up_co), tuple(lead(a) for a in up_oo), up_small[7],
           tuple(lead(a) for a in up_fi), tuple(lead(a) for a in up_fo),
           tuple(a.reshape(d) for a in up_small[8])]
    grads = [u[0] for u in ups]
    deltas = [u[1] for u in ups]
    new_m = [u[2] for u in ups]
    new_v = [u[3] for u in ups]
    return (loss, grad_x.reshape(1, t, d), *grads, *deltas, *new_m, *new_v)
```

```python
import functools

import jax
import jax.numpy as jnp
from jax import lax
from jax.experimental import pallas as pl
from jax.experimental.pallas import tpu as pltpu

F32 = jnp.float32
BF16 = jnp.bfloat16
MESH = pl.DeviceIdType.MESH

N_DEV = 8
EPS = 1e-6
LN_EPS = 1e-5
SHORT_K = 3
CONF_K = 31
ADAM_LR = 0.001
ADAM_B1 = 0.9
ADAM_B2 = 0.999
ADAM_EPS = 1e-08
ADAM_WD = 0.01
ADAM_STEP = 10

LANES = 128
SUBLANES = 8
CONV_ROWS = 64
HALO_SHORT = 8
HALO_CONF = 32
VMEM_LIMIT = 56 * 1024 * 1024

M_SH1, M_SC1, M_G1, M_SH2, M_SC2, M_G2 = range(6)
P_GMIX, P_GFFN, P_GFIN, P_CBIAS, P_LNG, P_LNB = range(6)


def _params(sem=None, **kw):
    return pltpu.CompilerParams(dimension_semantics=sem, vmem_limit_bytes=VMEM_LIMIT, **kw)


def _sigmoid(v):
    return jax.nn.sigmoid(v)


def _dot(a, b):
    return jnp.dot(a, b, preferred_element_type=F32)


def _dot_nt(a, b):
    return lax.dot_general(a, b, (((1,), (1,)), ((), ())), preferred_element_type=F32)


def _dot_tn(a, b):
    return lax.dot_general(a, b, (((0,), (0,)), ((), ())), preferred_element_type=F32)


def _colsum(v):
    return jnp.sum(v, axis=0, keepdims=True)


def _rowmean(v):
    return jnp.mean(v, axis=-1, keepdims=True)


def _my_coords():
    return lax.axis_index("x"), lax.axis_index("y"), lax.axis_index("c")


def _slot(px, py, pc):
    return 4 * px + 2 * py + pc


def _flip(v, bit):
    return 1 - v if bit else v


def _taps_by_residue(taps):
    by_res = {}
    for wi, off in taps:
        by_res.setdefault(off % SUBLANES, []).append((wi, off // SUBLANES))
    return sorted(by_res.items())


def _tap_conv(ext_ref, w_ref, taps, tm, extra, out_ref):
    d = out_ref.shape[1]
    rb = min(CONV_ROWS, tm)
    wrows = rb + extra
    groups = _taps_by_residue(taps)

    def block(i, carry):
        base = pl.multiple_of(i * rb, SUBLANES)
        for lc in range(d // LANES):
            ls = pl.ds(lc * LANES, LANES)
            win = ext_ref[pl.ds(base, wrows), ls]
            acc = None
            for r, lst in groups:
                sh = win if r == 0 else pltpu.roll(win, wrows - r, 0)
                for wi, q in lst:
                    term = w_ref[wi:wi + 1, ls] * sh[SUBLANES * q:SUBLANES * q + rb, :]
                    acc = term if acc is None else acc + term
            out_ref[pl.ds(base, rb), ls] = acc
        return carry

    lax.fori_loop(0, tm // rb, block, 0)


def _tap_wgrad(a_ref, ext_ref, taps, tm, extra, acc_ref):
    d = a_ref.shape[1]
    rb = min(CONV_ROWS, tm)
    wrows = rb + extra
    groups = _taps_by_residue(taps)

    def block(i, carry):
        base = pl.multiple_of(i * rb, SUBLANES)
        for lc in range(d // LANES):
            ls = pl.ds(lc * LANES, LANES)
            a_blk = a_ref[pl.ds(base, rb), ls]
            win = ext_ref[pl.ds(base, wrows), ls]
            for r, lst in groups:
                sh = win if r == 0 else pltpu.roll(win, wrows - r, 0)
                for wi, q in lst:
                    prod = a_blk * sh[SUBLANES * q:SUBLANES * q + rb, :]
                    part = prod[0:SUBLANES, :]
                    for s in range(1, rb // SUBLANES):
                        part = part + prod[SUBLANES * s:SUBLANES * (s + 1), :]
                    rows = pl.ds(SUBLANES * wi, SUBLANES)
                    acc_ref[rows, ls] = acc_ref[rows, ls] + part
        return carry

    lax.fori_loop(0, tm // rb, block, 0)


def _causal_taps(k, halo):
    return [(i, halo - (k - 1) + i) for i in range(k)]


def _anticausal_taps(k):
    return [(i, (k - 1) - i) for i in range(k)]


def _ada_scratch(d, nloc, trows):
    return [pltpu.VMEM((N_DEV, 1, d), F32), pltpu.VMEM((N_DEV, 1, nloc), F32), pltpu.VMEM((N_DEV, 1, nloc), F32),
            pltpu.VMEM((N_DEV, trows, LANES), F32), pltpu.SemaphoreType.DMA((21,)), pltpu.SemaphoreType.DMA((21,))]


def _ada_exchange(c_ref, w_ref, b_ref, t_ref, mod_ref, sc_ref, taps_ref,
                  scall_ref, part_ref, modrecv_ref, tapsall_ref, send_sems, recv_sems):
    x, y, cc = _my_coords()
    me = _slot(x, y, cc)
    d = c_ref.shape[1]
    cv = c_ref[...]
    scall_ref[me] = cv * _sigmoid(cv)
    tapsall_ref[me] = t_ref[...]

    def peer_of(k):
        return (_flip(x, k & 4), _flip(y, k & 2), _flip(cc, k & 1))

    def gather_copy(ref, base, k):
        return pltpu.make_async_remote_copy(
            src_ref=ref.at[me], dst_ref=ref.at[me], send_sem=send_sems.at[base + k - 1],
            recv_sem=recv_sems.at[base + k - 1], device_id=peer_of(k), device_id_type=MESH)

    first = [gather_copy(scall_ref, 0, k) for k in range(1, N_DEV)]
    first += [gather_copy(tapsall_ref, 7, k) for k in range(1, N_DEV)]
    for cp in first:
        cp.start()
    for cp in first[:7]:
        cp.wait_recv()
    sc_all = jnp.concatenate([scall_ref[s] for s in range(N_DEV)], axis=0)
    for s in range(N_DEV):
        sc_ref[s] = scall_ref[s]
    part = jnp.dot(sc_all, w_ref[...], preferred_element_type=F32,
                   precision=lax.Precision.HIGHEST)
    for b in range(N_DEV):
        part_ref[b] = part[b:b + 1, :]
    modrecv_ref[me] = part_ref[me]
    second = []
    for k in range(1, N_DEV):
        px, py, pc = peer_of(k)
        second.append(pltpu.make_async_remote_copy(
            src_ref=part_ref.at[_slot(px, py, pc)], dst_ref=modrecv_ref.at[me],
            send_sem=send_sems.at[14 + k - 1], recv_sem=recv_sems.at[14 + k - 1],
            device_id=(px, py, pc), device_id_type=MESH))
    for cp in second:
        cp.start()
    for cp in second:
        cp.wait_recv()
    mod = jnp.concatenate([modrecv_ref[s] for s in range(N_DEV)], axis=1) + b_ref[...]
    for r in range(6):
        mod_ref[r:r + 1, :] = mod[:, r * d:(r + 1) * d]
    mod_ref[6:8, :] = jnp.zeros((2, d), F32)
    for cp in first[7:]:
        cp.wait_recv()
    taps_ref[...] = jnp.concatenate([tapsall_ref[s] for s in range(N_DEV)], axis=1)
    for cp in first + second:
        cp.wait_send()


ANY_SPEC = pl.BlockSpec(memory_space=pl.ANY)


def _comm_scratch(na):
    return [pltpu.SemaphoreType.DMA((7 * na,)), pltpu.SemaphoreType.DMA((7 * na,)), pltpu.SemaphoreType.DMA((na,))]


def _leading_block(ref, slot):
    return ref.at[slot]


def _column_block(width):
    def view(ref, slot):
        return ref.at[:, pl.ds(pl.multiple_of(slot * width, LANES), width)]
    return view


class _Gather:
    def __init__(self, ins, outs, send_sems, recv_sems, local_sems, view=_leading_block):
        self.na = len(ins)
        x, y, c = _my_coords()
        self.c = c
        self.view = view
        self.me, self.sibling = (x, y, c), (x, y, 1 - c)
        self.chips = [(1 - x, y), (x, 1 - y), (1 - x, 1 - y)]
        self.outs, self.send_sems, self.recv_sems = outs, send_sems, recv_sems
        self.mine = [pltpu.make_async_copy(ins[a], view(outs[a], _slot(*self.me)), local_sems.at[a])
                     for a in range(self.na)]
        self.first = []
        for a in range(self.na):
            self.first.append(self._copy(a, 0, self.me, self.sibling, src=ins[a]))
            self.first += [self._copy(a, 1 + j, self.me, (*chip, c), src=ins[a]) for j, chip in enumerate(self.chips)]
        self.passed = [self._copy(a, 4 + j, (*chip, c), self.sibling)
                       for a in range(self.na) for j, chip in enumerate(self.chips)]

    def _copy(self, a, k, block, to, src=None):
        dst = self.view(self.outs[a], _slot(*block))
        return pltpu.make_async_remote_copy(
            src_ref=dst if src is None else src, dst_ref=dst,
            send_sem=self.send_sems.at[7 * a + k], recv_sem=self.recv_sems.at[7 * a + k],
            device_id=to, device_id_type=MESH)

    def start(self):
        self.start_near()
        self.start_far()

    def start_near(self):
        self.start_local()
        self.start_one_hop()

    def start_local(self):
        for cp in self.mine + self.first[0::4]:
            cp.start()

    def start_one_hop(self):
        for cp in self.first[1::4] + self.first[2::4]:
            cp.start()

    def start_far(self):
        for cp in self.first[3::4]:
            cp.start()

    def forward_chip(self, j):
        for a in range(self.na):
            self._copy(a, 1 + j, (*self.chips[j], self.c), self.me).wait_recv()
            self.passed[3 * a + j].start()

    def wait_sibling_own(self):
        for a in range(self.na):
            self._copy(a, 0, self.sibling, self.me).wait_recv()

    def wait_sibling_passed(self, j):
        for a in range(self.na):
            self._copy(a, 4 + j, (*self.chips[j], 1 - self.c), self.me).wait_recv()

    def finish_sends(self):
        for cp in self.first + self.passed:
            cp.wait_send()
        for cp in self.mine:
            cp.wait()

    def forward(self):
        for j in range(3):
            self.forward_chip(j)

    def finish(self):
        self.wait_sibling_own()
        for j in range(3):
            self.wait_sibling_passed(j)
        self.finish_sends()


def _scatter_plan(ins, outs, send_sems, recv_sems, local_sems, view=_leading_block):
    na = len(ins)
    x, y, c = _my_coords()
    me = _slot(x, y, c)
    mine = [pltpu.make_async_copy(view(ins[a], me), outs[a].at[me], local_sems.at[a]) for a in range(na)]
    copies = []
    for k in range(1, N_DEV):
        peer = (_flip(x, k & 4), _flip(y, k & 2), _flip(c, k & 1))
        for a in range(na):
            copies.append(pltpu.make_async_remote_copy(
                src_ref=view(ins[a], _slot(*peer)), dst_ref=outs[a].at[me],
                send_sem=send_sems.at[7 * a + k - 1], recv_sem=recv_sems.at[7 * a + k - 1],
                device_id=peer, device_id_type=MESH))

    def start():
        for cp in mine + copies:
            cp.start()

    def finish():
        for cp in copies:
            cp.wait_recv()
        for cp in copies:
            cp.wait_send()
        for cp in mine:
            cp.wait()

    return start, finish


def _forward_in(me_arr, x, prm, w_in_loc, ada, tm, shards):
    t, d = x.shape
    n = w_in_loc.shape[1]
    ns = N_DEV // 2
    na = len(shards)
    nt = t // tm
    nloc = ada[1].shape[1]
    trows = ada[3].shape[0]
    chip_of_pass = {1: 1, 2: 0, 3: 2}

    def body(*refs):
        x_ref, prm_ref, wloc_ref = refs[1:4]
        ada_in = refs[4:8]
        shard_refs = refs[8:8 + na]
        proj_ref, h_ref, wg_ref, mod_ref, sc_ref, taps_ref = refs[8 + na:14 + na]
        gathered_refs = refs[14 + na:14 + 2 * na]
        hall_ref, wv_ref, wv_sem = refs[14 + 2 * na:17 + 2 * na]
        ada_scratch = refs[17 + 2 * na:23 + 2 * na]
        sems = refs[23 + 2 * na:]
        s, i = pl.program_id(0), pl.program_id(1)
        columns = _column_block(n)

        def gathers():
            return (_Gather([wloc_ref], [wg_ref], *sems[0:3], view=columns),
                    _Gather(shard_refs, gathered_refs, *sems[3:6]))

        def shard_copy(dev, p, other_core, src=None):
            half = pl.ds(pl.multiple_of((dev & 1) * n, LANES), n)
            return pltpu.make_async_copy(columns(wg_ref, dev) if src is None else src,
                                         wv_ref.at[p % 2, :, half], wv_sem.at[2 * (p % 2) + other_core])

        def pair_copies(p, me):
            return [shard_copy(me ^ (2 * p), p, 0), shard_copy(me ^ (2 * p) ^ 1, p, 1)]

        @pl.when(i == 0)
        def _():
            me = _slot(*_my_coords())

            @pl.when(s == 0)
            def _():
                g_in, _ = gathers()
                g_in.start_local()
                _ada_exchange(*ada_in, mod_ref, sc_ref, taps_ref, *ada_scratch)
                g_in.start_one_hop()
                own = shard_copy(me, 0, 0, src=wloc_ref)
                own.start()
                g_in.wait_sibling_own()
                sib = shard_copy(me ^ 1, 0, 1)
                sib.start()
                own.wait()
                sib.wait()

            for p in range(1, ns):
                @pl.when(s == p)
                def _(p=p):
                    for cp in pair_copies(p, me):
                        cp.wait()

        @pl.when(i == max(nt - 3, 0))
        def _():
            g_in, g_rest = gathers()
            for p in range(1, ns):
                @pl.when(s == p - 1)
                def _(p=p):
                    g_in.forward_chip(chip_of_pass[p])
                    if p == 1:
                        g_in.start_far()
                    if p == 2:
                        g_rest.start()
                    if p == 3:
                        g_rest.forward()

        @pl.when(i == nt - 1)
        def _():
            g_in, _ = gathers()
            me = _slot(*_my_coords())
            for p in range(1, ns):
                @pl.when(s == p - 1)
                def _(p=p):
                    g_in.wait_sibling_passed(chip_of_pass[p])
                    for cp in pair_copies(p, me):
                        cp.start()

        @pl.when(s == 0)
        def _():
            xv = x_ref[...]
            r = lax.rsqrt(_rowmean(xv * xv) + EPS)
            h = xv * r * prm_ref[P_GMIX:P_GMIX + 1, :] * (1.0 + mod_ref[M_SC1:M_SC1 + 1, :]) \
                + mod_ref[M_SH1:M_SH1 + 1, :]
            hb = h.astype(BF16)
            hall_ref[i] = hb
            h_ref[...] = hb

        proj_ref[...] = _dot(hall_ref[i], wv_ref[s % 2]).astype(BF16)

        @pl.when((s == ns - 1) & (i == nt - 1))
        def _():
            g_in, g_rest = gathers()
            g_in.finish_sends()
            g_rest.finish()

    x_tile = pl.BlockSpec((tm, d), lambda s, i, me: (jnp.where(s == 0, i, nt - 1), 0))
    whole = lambda a: pl.BlockSpec(a.shape, lambda s, i, me: (0,) * len(a.shape))
    small_out = (jax.ShapeDtypeStruct((8, d), F32), jax.ShapeDtypeStruct((N_DEV, 1, d), F32),
                 jax.ShapeDtypeStruct((trows, N_DEV * LANES), F32))
    res = pl.pallas_call(
        body, name="forward_in",
        grid_spec=pltpu.PrefetchScalarGridSpec(
            num_scalar_prefetch=1, grid=(ns, nt),
            in_specs=[x_tile, whole(prm), ANY_SPEC] + [whole(a) for a in ada] + [ANY_SPEC] * na,
            out_specs=[pl.BlockSpec((tm, 2 * n), lambda s, i, me: (i, (me[0] // 2) ^ s)), x_tile, ANY_SPEC]
            + [whole(a) for a in small_out] + [ANY_SPEC] * na,
            scratch_shapes=[pltpu.VMEM((nt, tm, d), BF16), pltpu.VMEM((2, d, 2 * n), BF16),
                            pltpu.SemaphoreType.DMA((4,))]
            + _ada_scratch(d, nloc, trows) + _comm_scratch(1) + _comm_scratch(na)),
        out_shape=(jax.ShapeDtypeStruct((t, N_DEV * n), BF16), jax.ShapeDtypeStruct((t, d), BF16),
                   jax.ShapeDtypeStruct((d, N_DEV * n), BF16)) + small_out
        + tuple(jax.ShapeDtypeStruct((N_DEV,) + a.shape, a.dtype) for a in shards),
        compiler_params=_params(("arbitrary", "arbitrary")),
    )(me_arr, x, prm, w_in_loc, *ada, *shards)
    return res[0], res[1], res[2], res[3], res[4], res[5], res[6:]


def _forward_mix(x, proj, mod, prm, taps, w_so, w_co, w_o, tm, shards):
    t, d = x.shape
    nt = t // tm
    na = len(shards)
    hs, hc = HALO_SHORT, HALO_CONF
    r3, r31 = 0, 8

    def body(*refs):
        x_ref, proj_ref, mod_ref, prm_ref, taps_ref, wso_ref, wco_ref, wo_ref = refs[:8]
        (x1_ref, ya_ref, yb_ref, mix_ref, uhat_ref, c3_ref, sg_ref, ga_ref, gb_ref, lhs_ref,
         rstd_ref) = refs[8 + na:19 + na]
        cv_ext, u0_ext, conv3_ref, u1f_ref = refs[19 + 2 * na:23 + 2 * na]
        gather = _Gather(refs[8:8 + na], refs[19 + na:19 + 2 * na], *refs[23 + 2 * na:])
        i = pl.program_id(0)

        @pl.when(i == 0)
        def _():
            gather.start()
            cv_ext[0:hs, :] = jnp.zeros((hs, d), F32)
            u0_ext[0:hc, :] = jnp.zeros((hc, d), F32)

        @pl.when(i == (3 * nt) // 4)
        def _():
            gather.forward()

        def col(g):
            return proj_ref[:, g * d:(g + 1) * d].astype(F32)

        cv_ext[hs:hs + tm, :] = col(1) * col(2)
        sg = _sigmoid(col(4))
        sg_ref[...] = sg.astype(BF16)
        u0_ext[hc:hc + tm, :] = col(3) * sg
        _tap_conv(cv_ext, taps_ref, [(r3 + wi, off) for wi, off in _causal_taps(SHORT_K, hs)], tm, hs, conv3_ref)
        _tap_conv(u0_ext, taps_ref, [(r31 + wi, off) for wi, off in _causal_taps(CONF_K, hc)], tm, hc, u1f_ref)
        cv_ext[0:hs, :] = cv_ext[tm:tm + hs, :]
        u0_ext[0:hc, :] = u0_ext[tm:tm + hc, :]

        c3_ref[...] = conv3_ref[...].astype(BF16)
        ya_pre = (col(0) * c3_ref[...].astype(F32)).astype(BF16)
        lhs_ref[0] = ya_pre
        y_a = _dot(ya_pre, wso_ref[...])
        u1 = u1f_ref[...] + prm_ref[P_CBIAS:P_CBIAS + 1, :]
        mu = _rowmean(u1)
        uc = u1 - mu
        rstd = lax.rsqrt(_rowmean(uc * uc) + LN_EPS)
        rstd_ref[...] = rstd
        uhat_ref[...] = (uc * rstd).astype(BF16)
        u2 = uhat_ref[...].astype(F32) * prm_ref[P_LNG:P_LNG + 1, :] + prm_ref[P_LNB:P_LNB + 1, :]
        u3 = (u2 * _sigmoid(u2)).astype(BF16)
        lhs_ref[1] = u3
        y_b = _dot(u3, wco_ref[...])
        ya_ref[...] = y_a.astype(BF16)
        yb_ref[...] = y_b.astype(BF16)
        ga = _sigmoid(col(5))
        gb = _sigmoid(col(6))
        ga_ref[...] = ga.astype(BF16)
        gb_ref[...] = gb.astype(BF16)
        merged = (ga * ya_ref[...].astype(F32) + gb * yb_ref[...].astype(F32)).astype(BF16)
        lhs_ref[2] = merged
        mix = _dot(merged, wo_ref[...])
        mix_ref[...] = mix.astype(BF16)
        x1_ref[...] = x_ref[...] + mod_ref[M_G1:M_G1 + 1, :] * mix

        @pl.when(i == nt - 1)
        def _():
            gather.finish()

    tile = pl.BlockSpec((tm, d), lambda i: (i, 0))
    whole = lambda shape: pl.BlockSpec(shape, lambda i: (0,) * len(shape))
    res = pl.pallas_call(
        body, name="forward_mix", grid=(nt,),
        out_shape=(jax.ShapeDtypeStruct((t, d), F32),) + (jax.ShapeDtypeStruct((t, d), BF16),) * 8
        + (jax.ShapeDtypeStruct((3, t, d), BF16), jax.ShapeDtypeStruct((t, 1), F32))
        + tuple(jax.ShapeDtypeStruct((N_DEV,) + a.shape, a.dtype) for a in shards),
        in_specs=[tile, pl.BlockSpec((tm, 7 * d), lambda i: (i, 0)), whole((8, d)), whole((8, d)),
                  whole(taps.shape), whole((d, d)), whole((d, d)), whole((d, d))] + [ANY_SPEC] * na,
        out_specs=[tile] * 9 + [pl.BlockSpec((3, tm, d), lambda i: (0, i, 0)), pl.BlockSpec((tm, 1), lambda i: (i, 0))]
        + [ANY_SPEC] * na,
        scratch_shapes=[pltpu.VMEM((hs + tm, d), F32), pltpu.VMEM((hc + tm, d), F32),
                        pltpu.VMEM((tm, d), F32), pltpu.VMEM((tm, d), F32)] + _comm_scratch(na),
        compiler_params=_params(("arbitrary",)),
    )(x, proj, mod, prm, taps, w_so, w_co, w_o, *shards)
    return res[:11], res[11:]


def _ffn_chunks(ff):
    mxu = 2 * LANES
    cut = (ff // mxu + 1) // 2 * mxu
    return [(0, cut), (cut, ff)] if 0 < cut < ff and ff % mxu == 0 else [(0, ff)]


def _forward_ffn(x1, tgt, mod, prm, w_fi_t, w_fo, tm):
    t, d = x1.shape
    ff = w_fo.shape[0]

    def body(x1_ref, tgt_ref, mod_ref, prm_ref, wfi_hbm, wfo_hbm,
             dx2_ref, ab_ref, h2_ref, act_ref, sums_ref, wfi_ref, wfo_ref):
        i = pl.program_id(0)

        @pl.when(i == 0)
        def _():
            pltpu.sync_copy(wfi_hbm, wfi_ref)
            pltpu.sync_copy(wfo_hbm, wfo_ref)
            sums_ref[...] = jnp.zeros((8, d), F32)

        x1v = x1_ref[...]
        r2 = lax.rsqrt(_rowmean(x1v * x1v) + EPS)
        h2 = (x1v * r2 * prm_ref[P_GFFN:P_GFFN + 1, :] * (1.0 + mod_ref[M_SC2:M_SC2 + 1, :])
              + mod_ref[M_SH2:M_SH2 + 1, :]).astype(BF16)
        h2_ref[...] = h2
        f = jnp.zeros((tm, d), F32)
        for c0, c1 in _ffn_chunks(ff):
            ab_ref[:, c0:c1] = _dot_nt(h2, wfi_ref[c0:c1, :]).astype(BF16)
            ab_ref[:, ff + c0:ff + c1] = _dot_nt(h2, wfi_ref[ff + c0:ff + c1, :]).astype(BF16)
            a = ab_ref[:, c0:c1].astype(F32)
            act = (a * _sigmoid(a) * ab_ref[:, ff + c0:ff + c1].astype(F32)).astype(BF16)
            act_ref[:, c0:c1] = act
            f = f + _dot(act, wfo_ref[c0:c1, :])
        x2 = x1v + mod_ref[M_G2:M_G2 + 1, :] * f
        r3 = lax.rsqrt(_rowmean(x2 * x2) + EPS)
        xn3 = x2 * r3
        gfin = prm_ref[P_GFIN:P_GFIN + 1, :]
        err = xn3 * gfin - tgt_ref[...]
        dy = err * (1.0 / d)
        dxn3 = dy * gfin
        dx2 = r3 * (dxn3 - xn3 * _rowmean(dxn3 * xn3))
        dx2_ref[...] = dx2
        sums_ref[0:1, :] = sums_ref[0:1, :] + _colsum(dy * xn3)
        sums_ref[1:2, :] = sums_ref[1:2, :] + _colsum(dx2 * f)
        sums_ref[2:3, :] = sums_ref[2:3, :] + _colsum(err * err) * (0.5 / d)

    tile = pl.BlockSpec((tm, d), lambda i: (i, 0))
    whole = lambda shape: pl.BlockSpec(shape, lambda i: (0,) * len(shape))
    return pl.pallas_call(
        body, name="forward_ffn", grid=(t // tm,),
        out_shape=(jax.ShapeDtypeStruct((t, d), F32), jax.ShapeDtypeStruct((t, 2 * ff), BF16),
                   jax.ShapeDtypeStruct((t, d), BF16), jax.ShapeDtypeStruct((t, ff), BF16),
                   jax.ShapeDtypeStruct((8, d), F32)),
        in_specs=[tile, tile, whole((8, d)), whole((8, d)), ANY_SPEC, ANY_SPEC],
        out_specs=[tile, pl.BlockSpec((tm, 2 * ff), lambda i: (i, 0)), tile,
                   pl.BlockSpec((tm, ff), lambda i: (i, 0)), whole((8, d))],
        scratch_shapes=[pltpu.VMEM(w_fi_t.shape, BF16), pltpu.VMEM(w_fo.shape, BF16)],
        compiler_params=_params(("arbitrary",)),
    )(x1, tgt, mod, prm, w_fi_t, w_fo)


def _backward_ffn(dx2, x1, ab, mod, prm, w_fi_t, w_fo, tm):
    t, d = x1.shape
    ff = w_fo.shape[0]

    def body(dx2_ref, x1_ref, ab_ref, mod_ref, prm_ref, wfi_hbm, wfo_hbm,
             dx1_ref, df_ref, dab_ref, sums_ref, wfi_ref, wfo_ref, w_sems):
        i = pl.program_id(0)
        load_fo = pltpu.make_async_copy(wfo_hbm, wfo_ref, w_sems.at[0])
        load_fi = pltpu.make_async_copy(wfi_hbm, wfi_ref, w_sems.at[1])

        @pl.when(i == 0)
        def _():
            load_fo.start()
            load_fi.start()
            load_fo.wait()
            sums_ref[...] = jnp.zeros((8, d), F32)

        dx2v = dx2_ref[...]
        df = (mod_ref[M_G2:M_G2 + 1, :] * dx2v).astype(BF16)
        df_ref[...] = df
        dh2 = jnp.zeros((tm, d), F32)
        for c0, c1 in _ffn_chunks(ff):
            dact = _dot_nt(df, wfo_ref[c0:c1, :])
            a = ab_ref[:, c0:c1].astype(F32)
            b = ab_ref[:, ff + c0:ff + c1].astype(F32)
            s = _sigmoid(a)
            sil = a * s
            da = (dact * b * (s * (1.0 + a * (1.0 - s)))).astype(BF16)
            db = (dact * sil).astype(BF16)
            dab_ref[:, c0:c1] = da
            dab_ref[:, ff + c0:ff + c1] = db
            if c0 == 0:
                @pl.when(i == 0)
                def _():
                    load_fi.wait()
            dh2 = dh2 + _dot(da, wfi_ref[c0:c1, :]) + _dot(db, wfi_ref[ff + c0:ff + c1, :])
        x1v = x1_ref[...]
        r2 = lax.rsqrt(_rowmean(x1v * x1v) + EPS)
        xn2 = x1v * r2
        gffn = prm_ref[P_GFFN:P_GFFN + 1, :]
        scale = 1.0 + mod_ref[M_SC2:M_SC2 + 1, :]
        dxn2 = dh2 * gffn * scale
        dx1_ref[...] = dx2v + r2 * (dxn2 - xn2 * _rowmean(dxn2 * xn2))
        hx = dh2 * xn2
        sums_ref[0:1, :] = sums_ref[0:1, :] + _colsum(dh2)
        sums_ref[1:2, :] = sums_ref[1:2, :] + _colsum(hx) * gffn
        sums_ref[2:3, :] = sums_ref[2:3, :] + _colsum(hx) * scale

    tile = pl.BlockSpec((tm, d), lambda i: (i, 0))
    whole = lambda shape: pl.BlockSpec(shape, lambda i: (0,) * len(shape))
    wide = lambda n: pl.BlockSpec((tm, n), lambda i: (i, 0))
    return pl.pallas_call(
        body, name="backward_ffn", grid=(t // tm,),
        out_shape=(jax.ShapeDtypeStruct((t, d), F32), jax.ShapeDtypeStruct((t, d), BF16),
                   jax.ShapeDtypeStruct((t, 2 * ff), BF16), jax.ShapeDtypeStruct((8, d), F32)),
        in_specs=[tile, tile, wide(2 * ff), whole((8, d)), whole((8, d)), ANY_SPEC, ANY_SPEC],
        out_specs=[tile, tile, wide(2 * ff), whole((8, d))],
        scratch_shapes=[pltpu.VMEM(w_fi_t.shape, BF16), pltpu.VMEM(w_fo.shape, BF16), pltpu.SemaphoreType.DMA((2,))],
        compiler_params=_params(("arbitrary",)),
    )(dx2, x1, ab, mod, prm, w_fi_t, w_fo)


def _backward_mix(dx1, proj, y_a, y_b, mix, uhat, rstd, conv3, sg, ga, gb, mod, prm, taps, w_so, w_co, w_o, tm,
                  partials):
    t, d = dx1.shape
    nt = t // tm
    na = len(partials)
    hs, hc = HALO_SHORT, HALO_CONF
    r3, r31 = 0, 8

    def body(*refs):
        (dx1_ref, proj_ref, halo_ref, ya_ref, yb_ref, mix_ref, uhat_ref, rstd_ref, c3_ref, sg_ref, ga_ref, gb_ref,
         mod_ref, prm_ref, taps_ref, wso_ref, wco_ref, wo_ref) = refs[:18]
        dproj_ref, rhs_ref, sums_ref, dw3_ref, dw31_ref = refs[18 + na:23 + na]
        cv_ext, u0_ext, d3_ext, du1_ext, tmp_ref = refs[23 + 2 * na:28 + 2 * na]
        scatter_start, scatter_finish = _scatter_plan(refs[18:18 + na], refs[23 + na:23 + 2 * na], *refs[28 + 2 * na:])
        i = pl.program_id(0)
        first_tile = i == nt - 1

        @pl.when(i == 0)
        def _():
            scatter_start()
            sums_ref[...] = jnp.zeros((8, d), F32)
            dw3_ref[...] = jnp.zeros(dw3_ref.shape, F32)
            dw31_ref[...] = jnp.zeros(dw31_ref.shape, F32)
            d3_ext[tm:tm + hs, :] = jnp.zeros((hs, d), F32)
            du1_ext[tm:tm + hc, :] = jnp.zeros((hc, d), F32)

        def col(g):
            return proj_ref[:, g * d:(g + 1) * d].astype(F32)

        def hcol(g, rows):
            v = halo_ref[HALO_CONF - rows:HALO_CONF, g * d:(g + 1) * d].astype(F32)
            return jnp.where(first_tile, 0.0, v)

        dx1v = dx1_ref[...]
        mixv = mix_ref[...].astype(F32)
        dmix = (mod_ref[M_G1:M_G1 + 1, :] * dx1v).astype(BF16)
        rhs_ref[2] = dmix
        sums_ref[0:1, :] = sums_ref[0:1, :] + _colsum(dx1v * mixv)
        dmerged = _dot_nt(dmix, wo_ref[...])
        ga = ga_ref[...].astype(F32)
        gb = gb_ref[...].astype(F32)
        yav = ya_ref[...].astype(F32)
        ybv = yb_ref[...].astype(F32)
        dya_f = dmerged * ga
        dyb_f = dmerged * gb
        dya = dya_f.astype(BF16)
        dyb = dyb_f.astype(BF16)
        rhs_ref[0] = dya
        rhs_ref[1] = dyb
        dproj_ref[:, 5 * d:6 * d] = (dya_f * yav * (1.0 - ga)).astype(BF16)
        dproj_ref[:, 6 * d:7 * d] = (dyb_f * ybv * (1.0 - gb)).astype(BF16)

        dya_pre = _dot_nt(dya, wso_ref[...])
        c_s, v_s, b_s = col(1), col(2), col(0)
        cv_ext[0:hs, :] = hcol(1, hs) * hcol(2, hs)
        cv_ext[hs:hs + tm, :] = c_s * v_s
        dproj_ref[:, 0:d] = (dya_pre * c3_ref[...].astype(F32)).astype(BF16)
        d3_ext[0:tm, :] = dya_pre * b_s
        _tap_wgrad(d3_ext, cv_ext, _causal_taps(SHORT_K, hs), tm, hs, dw3_ref)
        _tap_conv(d3_ext, taps_ref, [(r3 + wi, off) for wi, off in _anticausal_taps(SHORT_K)], tm, hs, tmp_ref)
        dcv = tmp_ref[...]
        dproj_ref[:, d:2 * d] = (dcv * v_s).astype(BF16)
        dproj_ref[:, 2 * d:3 * d] = (dcv * c_s).astype(BF16)
        d3_ext[tm:tm + hs, :] = d3_ext[0:hs, :]

        du3 = _dot_nt(dyb, wco_ref[...])
        uhat = uhat_ref[...].astype(F32)
        rstd = rstd_ref[...]
        lng = prm_ref[P_LNG:P_LNG + 1, :]
        u2 = uhat * lng + prm_ref[P_LNB:P_LNB + 1, :]
        s2 = _sigmoid(u2)
        du2 = du3 * (s2 * (1.0 + u2 * (1.0 - s2)))
        sums_ref[1:2, :] = sums_ref[1:2, :] + _colsum(du2 * uhat)
        sums_ref[2:3, :] = sums_ref[2:3, :] + _colsum(du2)
        duhat = du2 * lng
        du1 = rstd * (duhat - _rowmean(duhat) - uhat * _rowmean(duhat * uhat))
        sums_ref[3:4, :] = sums_ref[3:4, :] + _colsum(du1)
        du1_ext[0:tm, :] = du1
        v_c = col(3)
        sg = sg_ref[...].astype(F32)
        u0_ext[0:hc, :] = hcol(3, hc) * _sigmoid(hcol(4, hc))
        u0_ext[hc:hc + tm, :] = v_c * sg
        _tap_wgrad(du1_ext, u0_ext, _causal_taps(CONF_K, hc), tm, hc, dw31_ref)
        _tap_conv(du1_ext, taps_ref, [(r31 + wi, off) for wi, off in _anticausal_taps(CONF_K)], tm, hc, tmp_ref)
        dv_c = tmp_ref[...] * sg
        dproj_ref[:, 3 * d:4 * d] = dv_c.astype(BF16)
        dproj_ref[:, 4 * d:5 * d] = (dv_c * v_c * (1.0 - sg)).astype(BF16)
        du1_ext[tm:tm + hc, :] = du1_ext[0:hc, :]

        @pl.when(i == nt - 1)
        def _():
            scatter_finish()

    rev = lambda i: (nt - 1 - i, 0)
    tile = pl.BlockSpec((tm, d), rev)
    whole = lambda shape: pl.BlockSpec(shape, lambda i: (0,) * len(shape))
    hblocks = tm // HALO_CONF
    halo = pl.BlockSpec((HALO_CONF, 7 * d), lambda i: (jnp.maximum((nt - 1 - i) * hblocks - 1, 0), 0))
    res = pl.pallas_call(
        body, name="backward_mix", grid=(nt,),
        out_shape=(jax.ShapeDtypeStruct((t, 7 * d), BF16), jax.ShapeDtypeStruct((3, t, d), BF16),
                   jax.ShapeDtypeStruct((8, d), F32),
                   jax.ShapeDtypeStruct((SUBLANES * SHORT_K, d), F32),
                   jax.ShapeDtypeStruct((SUBLANES * CONF_K, d), F32))
        + tuple(jax.ShapeDtypeStruct(p.shape, p.dtype) for p in partials),
        in_specs=[tile, pl.BlockSpec((tm, 4 * d), rev), halo, tile, tile, tile, tile, pl.BlockSpec((tm, 1), rev),
                  tile, tile, tile, tile,
                  whole((8, d)), whole((8, d)), whole(taps.shape), whole((d, d)), whole((d, d)), whole((d, d))]
        + [ANY_SPEC] * na,
        out_specs=[pl.BlockSpec((tm, 7 * d), rev), pl.BlockSpec((3, tm, d), lambda i: (0, nt - 1 - i, 0)),
                   whole((8, d)), whole((SUBLANES * SHORT_K, d)), whole((SUBLANES * CONF_K, d))] + [ANY_SPEC] * na,
        scratch_shapes=[pltpu.VMEM((hs + tm, d), F32), pltpu.VMEM((hc + tm, d), F32),
                        pltpu.VMEM((tm + hs, d), F32), pltpu.VMEM((tm + hc, d), F32),
                        pltpu.VMEM((tm, d), F32)] + _comm_scratch(na),
        compiler_params=_params(("arbitrary",)),
    )(dx1, proj, proj, y_a, y_b, mix, uhat, rstd, conv3, sg, ga, gb, mod, prm, taps, w_so, w_co, w_o, *partials)
    return res[:5], res[5:]


def _backward_in(dproj, x, dx1, mod, prm, w_in_g, tm, partials, partial_view, exchanged_shapes):
    t, d = x.shape
    n_all = w_in_g.shape[1]
    na = len(partials)
    nt = t // tm

    def body(*refs):
        dproj_ref, x_ref, dx1_ref, mod_ref, prm_ref, w_hbm = refs[:6]
        gx_ref, sums_ref = refs[6 + na:8 + na]
        w_ref = refs[8 + 2 * na]
        scatter_start, scatter_finish = _scatter_plan(refs[6:6 + na], refs[8 + na:8 + 2 * na], *refs[9 + 2 * na:],
                                                      view=partial_view)

        @pl.when(pl.program_id(0) == 0)
        def _():
            scatter_start()
            pltpu.sync_copy(w_hbm, w_ref)
            sums_ref[...] = jnp.zeros((8, d), F32)

        dh = _dot_nt(dproj_ref[...], w_ref[...])
        xv = x_ref[...]
        r1 = lax.rsqrt(_rowmean(xv * xv) + EPS)
        xn = xv * r1
        gmix = prm_ref[P_GMIX:P_GMIX + 1, :]
        scale = 1.0 + mod_ref[M_SC1:M_SC1 + 1, :]
        dxn = dh * gmix * scale
        gx_ref[...] = dx1_ref[...] + r1 * (dxn - xn * _rowmean(dxn * xn))
        hx = dh * xn
        sums_ref[0:1, :] = sums_ref[0:1, :] + _colsum(dh)
        sums_ref[1:2, :] = sums_ref[1:2, :] + _colsum(hx) * gmix
        sums_ref[2:3, :] = sums_ref[2:3, :] + _colsum(hx) * scale

        @pl.when(pl.program_id(0) == nt - 1)
        def _():
            scatter_finish()

    tile = pl.BlockSpec((tm, d), lambda i: (i, 0))
    whole = pl.BlockSpec((8, d), lambda i: (0, 0))
    res = pl.pallas_call(
        body, name="backward_in", grid=(nt,),
        out_shape=(jax.ShapeDtypeStruct((t, d), F32), jax.ShapeDtypeStruct((8, d), F32))
        + tuple(jax.ShapeDtypeStruct(s, p.dtype) for p, s in zip(partials, exchanged_shapes)),
        in_specs=[pl.BlockSpec((tm, n_all), lambda i: (i, 0)), tile, tile, whole, whole, ANY_SPEC] + [ANY_SPEC] * na,
        out_specs=[tile, whole] + [ANY_SPEC] * na,
        scratch_shapes=[pltpu.VMEM(w_in_g.shape, BF16)] + _comm_scratch(na),
        compiler_params=_params(("arbitrary",)),
    )(dproj, x, dx1, mod, prm, w_in_g, *partials)
    return res[0], res[1], res[2:]


def _weight_grad(a, b, a_spec, b_spec, ns, m, n, nk, name, partials=(), by_columns=False,
                 partial_view=_leading_block, exchanged_shapes=None):
    na = len(partials)
    exchanged_shapes = exchanged_shapes or [p.shape for p in partials]

    def body(*refs):
        a_ref, b_ref = refs[:2]
        o_ref = refs[2 + na]
        acc_ref = refs[3 + 2 * na]
        s, k = pl.program_id(0), pl.program_id(1)
        if na:
            scatter_start, scatter_finish = _scatter_plan(refs[2:2 + na], refs[3 + na:3 + 2 * na], *refs[4 + 2 * na:],
                                                          view=partial_view)

            @pl.when((s == 0) & (k == 0))
            def _():
                scatter_start()

        av = a_ref[0] if len(a_ref.shape) == 3 else a_ref[...]
        bv = b_ref[0] if len(b_ref.shape) == 3 else b_ref[...]
        part = _dot_tn(av, bv)

        @pl.when(k == 0)
        def _():
            acc_ref[...] = part

        @pl.when(k > 0)
        def _():
            acc_ref[...] = acc_ref[...] + part

        @pl.when(k == nk - 1)
        def _():
            if by_columns:
                o_ref[...] = acc_ref[...].astype(BF16)
            else:
                o_ref[0] = acc_ref[...].astype(BF16)

        if na:
            @pl.when((s == ns - 1) & (k == nk - 1))
            def _():
                scatter_finish()

    if by_columns:
        out_shape, out_spec = (m, ns * n), pl.BlockSpec((m, n), lambda s, k: (0, s))
    else:
        out_shape, out_spec = (ns, m, n), pl.BlockSpec((1, m, n), lambda s, k: (s, 0, 0))
    res = pl.pallas_call(
        body, name=name, grid=(ns, nk),
        out_shape=(jax.ShapeDtypeStruct(out_shape, BF16),)
        + tuple(jax.ShapeDtypeStruct(sh, p.dtype) for p, sh in zip(partials, exchanged_shapes)),
        in_specs=[a_spec, b_spec] + [ANY_SPEC] * na,
        out_specs=[out_spec] + [ANY_SPEC] * na,
        scratch_shapes=[pltpu.VMEM((m, n), F32)] + (_comm_scratch(na) if na else []),
        compiler_params=_params(("arbitrary", "arbitrary")),
    )(a, b, *partials)
    return (res[0], res[1:]) if na else res[0]


def _adamw(w, g, m, v):
    m = ADAM_B1 * m + (1.0 - ADAM_B1) * g
    v = ADAM_B2 * v + (1.0 - ADAM_B2) * (g * g)
    m_hat = m / (1.0 - ADAM_B1 ** ADAM_STEP)
    v_hat = v / (1.0 - ADAM_B2 ** ADAM_STEP)
    delta = -ADAM_LR * (m_hat / (jnp.sqrt(v_hat) + ADAM_EPS) + ADAM_WD * w)
    return delta, m, v


def _adamw_shard(parts, w, m, v, tr, name):
    r, c = w.shape

    def body(p_ref, w_ref, m_ref, v_ref, g_ref, d_ref, nm_ref, nv_ref):
        g = p_ref[0].astype(F32)
        for s in range(1, N_DEV):
            g = g + p_ref[s].astype(F32)
        delta, nm, nv = _adamw(w_ref[...], g, m_ref[...], v_ref[...])
        g_ref[...] = g
        d_ref[...] = delta
        nm_ref[...] = nm
        nv_ref[...] = nv

    tile = pl.BlockSpec((tr, c), lambda i: (i, 0))
    return pl.pallas_call(
        body, name=name, grid=(r // tr,),
        out_shape=(jax.ShapeDtypeStruct((r, c), F32),) * 4,
        in_specs=[pl.BlockSpec((N_DEV, tr, c), lambda i: (0, i, 0)), tile, tile, tile],
        out_specs=[tile] * 4,
        compiler_params=_params(("arbitrary",)),
    )(parts, w, m, v)


def _adamw_members(parts, wmv, name):
    nm = len(wmv)
    r, c = wmv[0][0].shape

    def body(*refs):
        p_ref = refs[0]
        for j in range(nm):
            w_ref, m_ref, v_ref = refs[1 + 3 * j:4 + 3 * j]
            g = p_ref[0, j].astype(F32)
            for s in range(1, N_DEV):
                g = g + p_ref[s, j].astype(F32)
            delta, new_m, new_v = _adamw(w_ref[...], g, m_ref[...], v_ref[...])
            for o_ref, val in zip(refs[1 + 3 * nm + 4 * j:5 + 3 * nm + 4 * j], (g, delta, new_m, new_v)):
                o_ref[...] = val

    vm = pl.BlockSpec(memory_space=pltpu.VMEM)
    res = pl.pallas_call(
        body, name=name,
        out_shape=(jax.ShapeDtypeStruct((r, c), F32),) * (4 * nm),
        in_specs=[vm] * (1 + 3 * nm), out_specs=[vm] * (4 * nm),
        compiler_params=_params(),
    )(parts, *[a for triple in wmv for a in triple])
    return [tuple(res[4 * j:4 * j + 4]) for j in range(nm)]


def _ada_update(sc_all, dmod_cols, w, m, v, tr):
    d, n = w.shape

    def body(sc_ref, dm_ref, w_ref, m_ref, v_ref, g_ref, d_ref, nm_ref, nv_ref):
        g = lax.dot_general(sc_ref[...], dm_ref[...], (((0,), (0,)), ((), ())),
                            preferred_element_type=F32, precision=lax.Precision.HIGHEST)
        delta, nm, nv = _adamw(w_ref[...], g, m_ref[...], v_ref[...])
        g_ref[...] = g
        d_ref[...] = delta
        nm_ref[...] = nm
        nv_ref[...] = nv

    tile = pl.BlockSpec((tr, n), lambda i: (i, 0))
    return pl.pallas_call(
        body, name="ada_update", grid=(d // tr,),
        out_shape=(jax.ShapeDtypeStruct((d, n), F32),) * 4,
        in_specs=[pl.BlockSpec((N_DEV, tr), lambda i: (0, i)), pl.BlockSpec((N_DEV, n), lambda i: (0, 0)),
                  tile, tile, tile],
        out_specs=[tile] * 4,
        compiler_params=_params(("arbitrary",)),
    )(sc_all, dmod_cols, w, m, v)


def _small_exchange(vec, cg):
    l = vec.shape[2]
    rows = cg.shape[1]

    def body(vec_ref, cg_ref, vall_ref, cgr_ref, send_sems, recv_sems):
        x, y, c = _my_coords()
        me = _slot(x, y, c)
        vall_ref[me] = vec_ref[0]
        cgr_ref[me] = cg_ref[me]
        copies = []
        for k in range(1, N_DEV):
            peer = (_flip(x, k & 4), _flip(y, k & 2), _flip(c, k & 1))
            copies.append(pltpu.make_async_remote_copy(
                src_ref=vall_ref.at[me], dst_ref=vall_ref.at[me], send_sem=send_sems.at[k - 1],
                recv_sem=recv_sems.at[k - 1], device_id=peer, device_id_type=MESH))
            copies.append(pltpu.make_async_remote_copy(
                src_ref=cg_ref.at[_slot(*peer)], dst_ref=cgr_ref.at[me], send_sem=send_sems.at[7 + k - 1],
                recv_sem=recv_sems.at[7 + k - 1], device_id=peer, device_id_type=MESH))
        for cp in copies:
            cp.start()
        for cp in copies:
            cp.wait_recv()
        for cp in copies:
            cp.wait_send()

    vm = pl.BlockSpec(memory_space=pltpu.VMEM)
    return pl.pallas_call(
        body, name="small_exchange",
        out_shape=(jax.ShapeDtypeStruct((N_DEV, 1, l), F32), jax.ShapeDtypeStruct((N_DEV, rows, LANES), F32)),
        in_specs=[vm, vm], out_specs=[vm, vm],
        scratch_shapes=[pltpu.SemaphoreType.DMA((14,)), pltpu.SemaphoreType.DMA((14,))],
        compiler_params=_params(),
    )(vec, cg)


def _small_update(vall, cgr, smalls):
    ns = len(smalls)

    def body(*refs):
        vall_ref, cgr_ref = refs[0], refs[1]
        wmv = refs[2:2 + 3 * ns]
        outs = refs[2 + 3 * ns:]
        for p, (_, _, _, lo, hi, kind) in enumerate(smalls):
            w_ref, m_ref, v_ref = wmv[3 * p:3 * p + 3]
            part = (lambda s: vall_ref[s, :, lo:hi]) if kind == "vec" else (lambda s: cgr_ref[s, lo:hi, :])
            full = (lambda ref: ref[...]) if kind == "vec" else (lambda ref: ref[0])
            g = part(0)
            for s in range(1, N_DEV):
                g = g + part(s)
            delta, nm, nv = _adamw(full(w_ref), g, full(m_ref), full(v_ref))
            for o_ref, val in zip(outs[4 * p:4 * p + 4], (g, delta, nm, nv)):
                if kind == "vec":
                    o_ref[...] = val
                else:
                    o_ref[0] = val

    vm = pl.BlockSpec(memory_space=pltpu.VMEM)
    args = [vall, cgr]
    out_shape = []
    for w, m, v, _, _, _ in smalls:
        args += [w, m, v]
        out_shape += [jax.ShapeDtypeStruct(w.shape, F32)] * 4
    res = pl.pallas_call(
        body, name="small_update",
        out_shape=tuple(out_shape),
        in_specs=[vm] * len(args), out_specs=[vm] * len(out_shape),
        compiler_params=_params(),
    )(*args)
    return [res[4 * p:4 + 4 * p] for p in range(ns)]


def _pick(t, want):
    return want if t % want == 0 else t


def kernel(x, c, w_ada, b_ada, norm_mix_g, w_in, conv_short_w, w_short_out, conv_conf_w, conv_conf_b, conf_ln_g, conf_ln_b, w_conf_out, w_o, norm_ffn_g, w_ffn_in, w_ffn_out, final_norm_g, loss_target, m_w_ada, m_b_ada, m_norm_mix_g, m_w_in, m_conv_short_w, m_w_short_out, m_conv_conf_w, m_conv_conf_b, m_conf_ln_g, m_conf_ln_b, m_w_conf_out, m_w_o, m_norm_ffn_g, m_w_ffn_in, m_w_ffn_out, m_final_norm_g, v_w_ada, v_b_ada, v_norm_mix_g, v_w_in, v_conv_short_w, v_w_short_out, v_conv_conf_w, v_conv_conf_b, v_conf_ln_g, v_conf_ln_b, v_w_conf_out, v_w_o, v_norm_ffn_g, v_w_ffn_in, v_w_ffn_out, v_final_norm_g):
    t, d = x.shape[1], x.shape[2]
    x2 = x.reshape(t, d)
    tgt = loss_target.reshape(t, d)
    me = _slot(*_my_coords())
    tm = _pick(t, 256)
    tm_fwd_in = _pick(t, 512)
    tk = _pick(t, 2048)

    taps_loc = jnp.zeros((40, LANES), F32)
    taps_loc = taps_loc.at[0:SHORT_K].set(conv_short_w[0]).at[8:8 + CONF_K].set(conv_conf_w[0])
    prm = jnp.concatenate([norm_mix_g, norm_ffn_g, final_norm_g.reshape(1, d), conv_conf_b, conf_ln_g, conf_ln_b,
                           jnp.zeros((2, d), F32)], axis=0)

    n_in = w_in.shape[2]
    nk = t // tk
    tok = pl.BlockSpec((tk, d), lambda s, k: (k, 0))
    rows = d // N_DEV
    frows = w_ffn_out.shape[1]

    proj, h, w_in_g, mod, sc_all3, taps, (w_so_g, w_co_g, w_o_g) = _forward_in(
        jnp.reshape(me, (1,)).astype(jnp.int32), x2, prm, w_in[0].astype(BF16),
        (c, w_ada[0], b_ada, taps_loc), tm_fwd_in,
        [w_short_out[0].astype(BF16), w_conf_out[0].astype(BF16), w_o[0].astype(BF16)])
    w_so = w_so_g.reshape(d, d)
    w_co = w_co_g.reshape(d, d)
    w_oo = w_o_g.reshape(d, d)
    w_fi_t, m_fi_t, v_fi_t = w_ffn_in[0].T, m_w_ffn_in[0].T, v_w_ffn_in[0].T
    (x1, y_a, y_b, mix, uhat, conv3, sg, ga, gb, mixer_lhs, rstd), (w_fi_g, w_fo_g) = _forward_mix(
        x2, proj, mod, prm, taps, w_so, w_co, w_oo, tm, [w_fi_t.astype(BF16), w_ffn_out[0].astype(BF16)])
    fb = w_fi_g.shape[1]
    ff = N_DEV * frows
    w_fi_all = w_fi_g.reshape(N_DEV * fb, d)
    w_fo_all = w_fo_g.reshape(ff, d)
    dx2, ab, h2, act, sums_f = _forward_ffn(x1, tgt, mod, prm, w_fi_all, w_fo_all, _pick(t, 512))

    dx1, df, dab, sums_b = _backward_ffn(dx2, x1, ab, mod, prm, w_fi_all, w_fo_all, tm)
    fb2 = 2 * fb
    pair_tok = pl.BlockSpec((tk, fb2), lambda s, k: (k, s))
    g_fi = _weight_grad(dab, h2, pair_tok, tok, N_DEV // 2, fb2, d, nk, "grad_w_ffn_in")
    g_fo = _weight_grad(act, df, pair_tok, tok, N_DEV // 4, fb2, d, nk, "grad_w_ffn_out")
    (dproj, mixer_rhs, sums_m, dw3p, dw31p), (p_fi, p_fo) = _backward_mix(
        dx1, proj, y_a, y_b, mix, uhat, rstd, conv3, sg, ga, gb, mod, prm, taps, w_so, w_co, w_oo, tm,
        [g_fi.reshape(N_DEV, fb, d), g_fo.reshape(N_DEV, frows, d)])
    tk_in = _pick(t, 4096)
    nk_in = t // tk_in
    tok_in = pl.BlockSpec((tk_in, d), lambda s, k: (k, 0))
    member_tok = pl.BlockSpec((1, tk, d), lambda s, k: (s, k, 0))
    g_mixers = _weight_grad(mixer_lhs, mixer_rhs, member_tok, member_tok, 3, d, d, nk, "grad_w_mixers")
    n_blk = _pick(N_DEV * n_in, d)
    g_in, (p_mixers,) = _weight_grad(
        h, dproj, tok_in, pl.BlockSpec((tk_in, n_blk), lambda s, k: (k, s)),
        N_DEV * n_in // n_blk, d, n_blk, nk_in, "grad_w_in",
        [g_mixers.reshape(3, N_DEV, rows, d)], by_columns=True,
        partial_view=lambda ref, dev: ref.at[:, dev], exchanged_shapes=[(N_DEV, 3, rows, d)])
    grad_x, sums_i, (p_in,) = _backward_in(dproj, x2, dx1, mod, prm, w_in_g, tm, [g_in], _column_block(n_in),
                                           [(N_DEV, d, n_in)])

    up_in = _adamw_shard(p_in, w_in[0], m_w_in[0], v_w_in[0], _pick(d, 512), "adamw_w_in")
    up_so, up_co, up_oo = _adamw_members(
        p_mixers, [(w_short_out[0], m_w_short_out[0], v_w_short_out[0]),
                   (w_conf_out[0], m_w_conf_out[0], v_w_conf_out[0]), (w_o[0], m_w_o[0], v_w_o[0])], "adamw_w_mixers")
    up_fi = tuple(a.T for a in _adamw_shard(p_fi, w_fi_t, m_fi_t, v_fi_t, fb // 2, "adamw_w_ffn_in"))
    up_fo = _adamw_shard(p_fo, w_ffn_out[0], m_w_ffn_out[0], v_w_ffn_out[0], frows, "adamw_w_ffn_out")

    vec = jnp.concatenate([sums_i[0:2], sums_m[0:1], sums_b[0:2], sums_f[1:2],
                           sums_i[2:3], sums_m[3:4], sums_m[1:3], sums_b[2:3], sums_f[0:1],
                           sums_f[2:3]], axis=0)
    vec = vec.reshape(1, 1, 13 * d)
    dw3 = dw3p.reshape(SHORT_K, SUBLANES, d).sum(axis=1)
    dw31 = dw31p.reshape(CONF_K, SUBLANES, d).sum(axis=1)
    cg = jnp.zeros((40, d), F32).at[0:SHORT_K].set(dw3).at[8:8 + CONF_K].set(dw31)
    cg = cg.reshape(40, N_DEV, LANES).transpose(1, 0, 2)
    fin = lambda a: a.reshape(1, d)
    smalls = [
        (b_ada, m_b_ada, v_b_ada, 0, 6 * d, "vec"),
        (norm_mix_g, m_norm_mix_g, v_norm_mix_g, 6 * d, 7 * d, "vec"),
        (conv_short_w, m_conv_short_w, v_conv_short_w, 0, SHORT_K, "cg"),
        (conv_conf_w, m_conv_conf_w, v_conv_conf_w, 8, 8 + CONF_K, "cg"),
        (conv_conf_b, m_conv_conf_b, v_conv_conf_b, 7 * d, 8 * d, "vec"),
        (conf_ln_g, m_conf_ln_g, v_conf_ln_g, 8 * d, 9 * d, "vec"),
        (conf_ln_b, m_conf_ln_b, v_conf_ln_b, 9 * d, 10 * d, "vec"),
        (norm_ffn_g, m_norm_ffn_g, v_norm_ffn_g, 10 * d, 11 * d, "vec"),
        (fin(final_norm_g), fin(m_final_norm_g), fin(v_final_norm_g), 11 * d, 12 * d, "vec"),
    ]
    vall, cgr = _small_exchange(vec, cg)
    up_small = _small_update(vall, cgr, smalls)
    n_ada = w_ada.shape[2]
    dmod_all = vall.reshape(N_DEV, 13 * d)[:, 0:6 * d]
    dmod_cols = lax.dynamic_slice(dmod_all, (0, me * n_ada), (N_DEV, n_ada))
    up_ada = _ada_update(sc_all3.reshape(N_DEV, d), dmod_cols, w_ada[0], m_w_ada[0], v_w_ada[0], _pick(d, 512))

    loss = jnp.sum(vall.reshape(N_DEV, 13 * d)[:, 12 * d:])

    lead = lambda a: a.reshape((1,) + a.shape)
    ups = [tuple(lead(a) for a in up_ada), up_small[0], up_small[1], tuple(lead(a) for a in up_in),
           up_small[2], tuple(lead(a) for a in up_so), up_small[3],
           up_small[4], up_small[5], up_small[6],
           tuple(lead(a) for a in up_co), tuple(lead(a) for a in up_oo), up_small[7],
           tuple(lead(a) for a in up_fi), tuple(lead(a) for a in up_fo),
           tuple(a.reshape(d) for a in up_small[8])]
    grads = [u[0] for u in ups]
    deltas = [u[1] for u in ups]
    new_m = [u[2] for u in ups]
    new_v = [u[3] for u in ups]
    return (loss, grad_x.reshape(1, t, d), *grads, *deltas, *new_m, *new_v)
```
